```python
import math
import jax, jax.numpy as jnp
from jax import lax
import numpy as np

D_MODEL = 1024
BATCH = 8
SEQ = 8192
DEPTH = 1

HEAD_DIM = 64
N_HEADS = D_MODEL // HEAD_DIM
SB_HEADS = N_HEADS // 2
SWA_HEADS = N_HEADS - SB_HEADS
SWA_KV_HEADS = 2
SWA_GROUP = SWA_HEADS // SWA_KV_HEADS
WINDOW = 128
BLOCK_Q = 128
SB_W = SB_HEADS * HEAD_DIM
SWA_QW = SWA_HEADS * HEAD_DIM
SWA_KW = SWA_KV_HEADS * HEAD_DIM
D_IN = 3 * SB_W + SWA_QW + 2 * SWA_KW
D_FF = ((-(-8 * D_MODEL // 3)) + 255) // 256 * 256
N_MOD = 6
DEEPNORM_ALPHA = (2.0 * DEPTH) ** 0.25
DEEPNORM_BETA = (8.0 * DEPTH) ** -0.25
LN_EPS = 1e-5
RMS_EPS = 1e-6
MASK_VALUE = -1e30

kernel_name = "hymba_stickbreak_swa_sink_deepnorm_adaln"


def layer_norm(x, g, b):
    xf = x.astype(jnp.float32)
    mu = jnp.mean(xf, axis=-1, keepdims=True)
    var = jnp.mean(jnp.square(xf - mu), axis=-1, keepdims=True)
    return ((xf - mu) * lax.rsqrt(var + LN_EPS)).astype(x.dtype) * g + b


def rms_norm(x, g):
    xf = x.astype(jnp.float32)
    ms = jnp.mean(jnp.square(xf), axis=-1, keepdims=True)
    return (xf * lax.rsqrt(ms + RMS_EPS)).astype(x.dtype) * g


def alibi_slopes(n_heads):
    return jnp.exp2(-8.0 * jnp.arange(1, n_heads + 1, dtype=jnp.float32) / n_heads)


def stick_breaking_attention(q, k, v):
    B, S, H, Dh = q.shape
    nblk = S // BLOCK_Q
    scale = 1.0 / math.sqrt(Dh)
    qb = q.reshape(B, nblk, BLOCK_Q, H, Dh).transpose(1, 0, 3, 2, 4)
    kpos = jnp.arange(S)

    def one_block(args):
        qi, i = args
        z = jnp.einsum('bhqd,bshd->bhqs', qi, k).astype(jnp.float32) * scale
        qpos = i * BLOCK_Q + jnp.arange(BLOCK_Q)
        before = kpos[None, :] < qpos[:, None]
        log_beta = jax.nn.log_sigmoid(z)
        log_rem = jnp.where(before, jax.nn.log_sigmoid(-z), 0.0)
        suffix = lax.cumsum(log_rem, axis=3, reverse=True) - log_rem
        w = jnp.where(before, jnp.exp(log_beta + suffix), 0.0)
        return jnp.einsum('bhqs,bshd->bqhd', w.astype(v.dtype), v)

    out = lax.map(one_block, (qb, jnp.arange(nblk)))
    return out.transpose(1, 0, 2, 3, 4).reshape(B, S, H * Dh)


def sliding_window_sink_attention(q, k, v, sinks):
    B, S, Hq, Dh = q.shape
    nblk = S // WINDOW
    qb = q.reshape(B, nblk, WINDOW, SWA_KV_HEADS, SWA_GROUP, Dh)

    def banded(t):
        tb = t.reshape(B, nblk, WINDOW, SWA_KV_HEADS, Dh)
        prev = jnp.pad(tb, ((0, 0), (1, 0), (0, 0), (0, 0), (0, 0)))[:, :-1]
        return jnp.concatenate([prev, tb], axis=2)

    kb, vb = banded(k), banded(v)
    s = jnp.einsum('bnqkgd,bnskd->bnkgqs', qb, kb).astype(jnp.float32) / math.sqrt(Dh)
    qi = jnp.arange(WINDOW)
    kj = jnp.arange(2 * WINDOW)
    dist = (qi[:, None] + WINDOW - kj[None, :]).astype(jnp.float32)
    in_band = (dist >= 0) & (dist < WINDOW)
    key_pos = jnp.arange(nblk)[:, None] * WINDOW - WINDOW + kj[None, :]
    mask = in_band[None, :, :] & (key_pos >= 0)[:, None, :]
    slopes = alibi_slopes(SWA_HEADS).reshape(SWA_KV_HEADS, SWA_GROUP)
    s = s - slopes[None, None, :, :, None, None] * dist[None, None, None, None]
    s = jnp.where(mask[None, :, None, None], s, MASK_VALUE)
    sink = sinks.astype(jnp.float32).reshape(SWA_KV_HEADS, SWA_GROUP)[None, None, :, :, None, None]
    m = jnp.maximum(jnp.max(s, axis=-1, keepdims=True), sink)
    p = jnp.exp(s - m)
    p = p / (jnp.sum(p, axis=-1, keepdims=True) + jnp.exp(sink - m))
    o = jnp.einsum('bnkgqs,bnskd->bnqkgd', p.astype(v.dtype), vb)
    return o.reshape(B, S, Hq * Dh)


def _fwd_setup_inputs(seed: int = 0) -> dict:
    key = jax.random.key(seed)
    ks = jax.random.split(key, 16)
    f32 = jnp.float32
    nrm = lambda k, shape: jax.random.normal(k, shape, f32)
    return {
        "x": nrm(ks[0], (BATCH, SEQ, D_MODEL)),
        "c": nrm(ks[1], (BATCH, D_MODEL)),
        "w_ada": nrm(ks[2], (DEPTH, D_MODEL, N_MOD * D_MODEL)) * (0.1 * D_MODEL ** -0.5),
        "b_ada": nrm(ks[3], (DEPTH, N_MOD * D_MODEL)) * 0.01,
        "w_in": nrm(ks[4], (DEPTH, D_MODEL, D_IN)) * D_MODEL ** -0.5,
        "b_in": nrm(ks[5], (DEPTH, D_IN)) * 0.01,
        "sinks": nrm(ks[6], (DEPTH, SWA_HEADS)) * 0.5,
        "gn_sb": 1.0 + 0.01 * nrm(ks[7], (DEPTH, SB_W)),
        "gn_swa": 1.0 + 0.01 * nrm(ks[8], (DEPTH, SWA_QW)),
        "w_out": nrm(ks[9], (DEPTH, D_MODEL, D_MODEL)) * (DEEPNORM_BETA * D_MODEL ** -0.5),
        "ln1_g": 1.0 + 0.01 * nrm(ks[10], (DEPTH, D_MODEL)),
        "ln1_b": 0.01 * nrm(ks[11], (DEPTH, D_MODEL)),
        "w_gu": nrm(ks[12], (DEPTH, D_MODEL, 2 * D_FF)) * D_MODEL ** -0.5,
        "w_down": nrm(ks[13], (DEPTH, D_FF, D_MODEL)) * (DEEPNORM_BETA * D_FF ** -0.5),
        "ln2_g": 1.0 + 0.01 * nrm(ks[14], (DEPTH, D_MODEL)),
        "ln2_b": 0.01 * nrm(ks[15], (DEPTH, D_MODEL)),
    }


def _fwd_reference(x, c, w_ada, b_ada, w_in, b_in, sinks, gn_sb, gn_swa, w_out,
              ln1_g, ln1_b, w_gu, w_down, ln2_g, ln2_b):
    B, S, _ = x.shape
    for l in range(DEPTH):
        mod = jax.nn.silu(c) @ w_ada[l] + b_ada[l]
        sh_a, sc_a, g_a, sh_f, sc_f, g_f = jnp.split(mod[:, None, :], N_MOD, axis=-1)

        h = x * (1.0 + sc_a) + sh_a
        proj = h @ w_in[l] + b_in[l]
        o0, o1, o2, o3, o4 = np.cumsum([SB_W, SB_W, SB_W, SWA_QW, SWA_KW])
        q_sb = proj[..., :o0].reshape(B, S, SB_HEADS, HEAD_DIM)
        k_sb = proj[..., o0:o1].reshape(B, S, SB_HEADS, HEAD_DIM)
        v_sb = proj[..., o1:o2].reshape(B, S, SB_HEADS, HEAD_DIM)
        q_sw = proj[..., o2:o3].reshape(B, S, SWA_HEADS, HEAD_DIM)
        k_sw = proj[..., o3:o4].reshape(B, S, SWA_KV_HEADS, HEAD_DIM)
        v_sw = proj[..., o4:].reshape(B, S, SWA_KV_HEADS, HEAD_DIM)

        y_sb = stick_breaking_attention(q_sb, k_sb, v_sb)
        y_sw = sliding_window_sink_attention(q_sw, k_sw, v_sw, sinks[l])
        mixed = jnp.concatenate([rms_norm(y_sb, gn_sb[l]), rms_norm(y_sw, gn_swa[l])], axis=-1)
        attn = mixed @ w_out[l]
        x = layer_norm(DEEPNORM_ALPHA * x + (1.0 + g_a) * attn, ln1_g[l], ln1_b[l])

        h = x * (1.0 + sc_f) + sh_f
        gate, up = jnp.split(h @ w_gu[l], 2, axis=-1)
        ffn = (jax.nn.silu(gate) * up) @ w_down[l]
        x = layer_norm(DEEPNORM_ALPHA * x + (1.0 + g_f) * ffn, ln2_g[l], ln2_b[l])
    return x


import jax as _jax
import jax.numpy as _jnp

TWIN_FORMAT = 'train_step'
FWD_PARAMS = ['x', 'c', 'w_ada', 'b_ada', 'w_in', 'b_in', 'sinks', 'gn_sb', 'gn_swa', 'w_out', 'ln1_g', 'ln1_b', 'w_gu', 'w_down', 'ln2_g', 'ln2_b']
TWIN_WEIGHTS = ['w_ada', 'b_ada', 'w_in', 'b_in', 'sinks', 'gn_sb', 'gn_swa', 'w_out', 'ln1_g', 'ln1_b', 'w_gu', 'w_down', 'ln2_g', 'ln2_b']
TWIN_DIFF_INPUT = 'x'
TWIN_INPUTS = ['x', 'c', 'w_ada', 'b_ada', 'w_in', 'b_in', 'sinks', 'gn_sb', 'gn_swa', 'w_out', 'ln1_g', 'ln1_b', 'w_gu', 'w_down', 'ln2_g', 'ln2_b', 'loss_target', 'm_w_ada', 'm_b_ada', 'm_w_in', 'm_b_in', 'm_sinks', 'm_gn_sb', 'm_gn_swa', 'm_w_out', 'm_ln1_g', 'm_ln1_b', 'm_w_gu', 'm_w_down', 'm_ln2_g', 'm_ln2_b', 'v_w_ada', 'v_b_ada', 'v_w_in', 'v_b_in', 'v_sinks', 'v_gn_sb', 'v_gn_swa', 'v_w_out', 'v_ln1_g', 'v_ln1_b', 'v_w_gu', 'v_w_down', 'v_ln2_g', 'v_ln2_b']
TWIN_OUTPUTS = ['loss', 'grad_x', 'grad_w_ada', 'grad_b_ada', 'grad_w_in', 'grad_b_in', 'grad_sinks', 'grad_gn_sb', 'grad_gn_swa', 'grad_w_out', 'grad_ln1_g', 'grad_ln1_b', 'grad_w_gu', 'grad_w_down', 'grad_ln2_g', 'grad_ln2_b', 'delta_w_ada', 'delta_b_ada', 'delta_w_in', 'delta_b_in', 'delta_sinks', 'delta_gn_sb', 'delta_gn_swa', 'delta_w_out', 'delta_ln1_g', 'delta_ln1_b', 'delta_w_gu', 'delta_w_down', 'delta_ln2_g', 'delta_ln2_b', 'new_m_w_ada', 'new_m_b_ada', 'new_m_w_in', 'new_m_b_in', 'new_m_sinks', 'new_m_gn_sb', 'new_m_gn_swa', 'new_m_w_out', 'new_m_ln1_g', 'new_m_ln1_b', 'new_m_w_gu', 'new_m_w_down', 'new_m_ln2_g', 'new_m_ln2_b', 'new_v_w_ada', 'new_v_b_ada', 'new_v_w_in', 'new_v_b_in', 'new_v_sinks', 'new_v_gn_sb', 'new_v_gn_swa', 'new_v_w_out', 'new_v_ln1_g', 'new_v_ln1_b', 'new_v_w_gu', 'new_v_w_down', 'new_v_ln2_g', 'new_v_ln2_b']
TWIN_LEAF_KINDS = {'loss': 'loss', 'grad_x': 'grad_x', 'grad_w_ada': 'grad_w', 'grad_b_ada': 'grad_w', 'grad_w_in': 'grad_w', 'grad_b_in': 'grad_w', 'grad_sinks': 'grad_w', 'grad_gn_sb': 'grad_w', 'grad_gn_swa': 'grad_w', 'grad_w_out': 'grad_w', 'grad_ln1_g': 'grad_w', 'grad_ln1_b': 'grad_w', 'grad_w_gu': 'grad_w', 'grad_w_down': 'grad_w', 'grad_ln2_g': 'grad_w', 'grad_ln2_b': 'grad_w', 'delta_w_ada': 'delta_w', 'delta_b_ada': 'delta_w', 'delta_w_in': 'delta_w', 'delta_b_in': 'delta_w', 'delta_sinks': 'delta_w', 'delta_gn_sb': 'delta_w', 'delta_gn_swa': 'delta_w', 'delta_w_out': 'delta_w', 'delta_ln1_g': 'delta_w', 'delta_ln1_b': 'delta_w', 'delta_w_gu': 'delta_w', 'delta_w_down': 'delta_w', 'delta_ln2_g': 'delta_w', 'delta_ln2_b': 'delta_w', 'new_m_w_ada': 'new_m', 'new_m_b_ada': 'new_m', 'new_m_w_in': 'new_m', 'new_m_b_in': 'new_m', 'new_m_sinks': 'new_m', 'new_m_gn_sb': 'new_m', 'new_m_gn_swa': 'new_m', 'new_m_w_out': 'new_m', 'new_m_ln1_g': 'new_m', 'new_m_ln1_b': 'new_m', 'new_m_w_gu': 'new_m', 'new_m_w_down': 'new_m', 'new_m_ln2_g': 'new_m', 'new_m_ln2_b': 'new_m', 'new_v_w_ada': 'new_v', 'new_v_b_ada': 'new_v', 'new_v_w_in': 'new_v', 'new_v_b_in': 'new_v', 'new_v_sinks': 'new_v', 'new_v_gn_sb': 'new_v', 'new_v_gn_swa': 'new_v', 'new_v_w_out': 'new_v', 'new_v_ln1_g': 'new_v', 'new_v_ln1_b': 'new_v', 'new_v_w_gu': 'new_v', 'new_v_w_down': 'new_v', 'new_v_ln2_g': 'new_v', 'new_v_ln2_b': 'new_v'}


def _forward(args):
    return _fwd_reference(*[args[k] for k in FWD_PARAMS])


def _output_shape():
    def fwd():
        inp = _fwd_setup_inputs(0)
        return _fwd_reference(*[inp[k] for k in FWD_PARAMS])
    out = _jax.eval_shape(fwd)
    return out.shape, out.dtype

N_MICROBATCH = 1
ADAM_LR = 0.001
ADAM_B1 = 0.9
ADAM_B2 = 0.999
ADAM_EPS = 1e-08
ADAM_WD = 0.01
ADAM_STEP = 10
PER_EXAMPLE_BATCH_AXIS = {'x': 0, 'c': 0, 'loss_target': 0}
SHARED_INPUTS = []
_WEIGHT_DTYPES = {'w_ada': _jnp.float32, 'b_ada': _jnp.float32, 'w_in': _jnp.float32, 'b_in': _jnp.float32, 'sinks': _jnp.float32, 'gn_sb': _jnp.float32, 'gn_swa': _jnp.float32, 'w_out': _jnp.float32, 'ln1_g': _jnp.float32, 'ln1_b': _jnp.float32, 'w_gu': _jnp.float32, 'w_down': _jnp.float32, 'ln2_g': _jnp.float32, 'ln2_b': _jnp.float32}
MOMENT_SCALE = {'w_ada': 1.162344e-01, 'b_ada': 2.613232e-01, 'w_in': 1.070250e-01, 'b_in': 4.022714e-01, 'sinks': 2.450705e-01, 'gn_sb': 1.170130e-01, 'gn_swa': 1.422539e-01, 'w_out': 1.979925e-01, 'ln1_g': 8.843623e-01, 'ln1_b': 4.888212e-01, 'w_gu': 4.435120e-02, 'w_down': 1.214082e-01, 'ln2_g': 6.394799e+01, 'ln2_b': 2.439934e+00}


def _to_microbatches(a, axis):
    t = _jnp.moveaxis(a, axis, 0)
    t = t.reshape((N_MICROBATCH, t.shape[0] // N_MICROBATCH) + t.shape[1:])
    return _jnp.moveaxis(t, 1, axis + 1)


def setup_inputs(seed: int = 0) -> dict:
    inp = _fwd_setup_inputs(seed)
    key = _jax.random.fold_in(_jax.random.key(seed), 7919)
    shape, _ = _output_shape()
    out = dict(inp)
    out["loss_target"] = _jax.random.normal(_jax.random.fold_in(key, 0), shape, _jnp.float32)
    for i, name in enumerate(TWIN_WEIGHTS):
        w = inp[name].astype(_jnp.float32)
        if MOMENT_SCALE is None:
            s = _jnp.sqrt(_jnp.mean(_jnp.square(w)) + 1e-30)
        else:
            s = MOMENT_SCALE[name]
        km, kv = _jax.random.split(_jax.random.fold_in(key, i + 1))
        out[name] = w
        out["m_" + name] = s * _jax.random.normal(km, w.shape, _jnp.float32)
        out["v_" + name] = (s * s) * _jax.random.uniform(kv, w.shape, _jnp.float32, 0.5, 1.5)
    if N_MICROBATCH > 1:
        for name, axis in PER_EXAMPLE_BATCH_AXIS.items():
            out[name] = _to_microbatches(out[name], axis)
    return {'x': out['x'], 'c': out['c'], 'w_ada': out['w_ada'], 'b_ada': out['b_ada'], 'w_in': out['w_in'], 'b_in': out['b_in'], 'sinks': out['sinks'], 'gn_sb': out['gn_sb'], 'gn_swa': out['gn_swa'], 'w_out': out['w_out'], 'ln1_g': out['ln1_g'], 'ln1_b': out['ln1_b'], 'w_gu': out['w_gu'], 'w_down': out['w_down'], 'ln2_g': out['ln2_g'], 'ln2_b': out['ln2_b'], 'loss_target': out['loss_target'], 'm_w_ada': out['m_w_ada'], 'm_b_ada': out['m_b_ada'], 'm_w_in': out['m_w_in'], 'm_b_in': out['m_b_in'], 'm_sinks': out['m_sinks'], 'm_gn_sb': out['m_gn_sb'], 'm_gn_swa': out['m_gn_swa'], 'm_w_out': out['m_w_out'], 'm_ln1_g': out['m_ln1_g'], 'm_ln1_b': out['m_ln1_b'], 'm_w_gu': out['m_w_gu'], 'm_w_down': out['m_w_down'], 'm_ln2_g': out['m_ln2_g'], 'm_ln2_b': out['m_ln2_b'], 'v_w_ada': out['v_w_ada'], 'v_b_ada': out['v_b_ada'], 'v_w_in': out['v_w_in'], 'v_b_in': out['v_b_in'], 'v_sinks': out['v_sinks'], 'v_gn_sb': out['v_gn_sb'], 'v_gn_swa': out['v_gn_swa'], 'v_w_out': out['v_w_out'], 'v_ln1_g': out['v_ln1_g'], 'v_ln1_b': out['v_ln1_b'], 'v_w_gu': out['v_w_gu'], 'v_w_down': out['v_w_down'], 'v_ln2_g': out['v_ln2_g'], 'v_ln2_b': out['v_ln2_b']}


def _loss(weights, diff, rest, loss_target):
    with _jax.named_scope("forward"):
        args = {**rest, TWIN_DIFF_INPUT: diff, **{k: w.astype(_WEIGHT_DTYPES[k]) for k, w in weights.items()}}
        y = _forward(args)
    with _jax.named_scope("loss_head"):
        err = _jnp.square(y.astype(_jnp.float32) - loss_target)
        return 0.5 * _jnp.sum(_jnp.mean(err, axis=-1)) if err.ndim else 0.5 * err


def _adamw(w, g, m, v):
    m = ADAM_B1 * m + (1.0 - ADAM_B1) * g
    v = ADAM_B2 * v + (1.0 - ADAM_B2) * _jnp.square(g)
    m_hat = m / (1.0 - ADAM_B1 ** ADAM_STEP)
    v_hat = v / (1.0 - ADAM_B2 ** ADAM_STEP)
    delta = -ADAM_LR * (m_hat / (_jnp.sqrt(v_hat) + ADAM_EPS) + ADAM_WD * w)
    return delta, m, v


def reference(x, c, w_ada, b_ada, w_in, b_in, sinks, gn_sb, gn_swa, w_out, ln1_g, ln1_b, w_gu, w_down, ln2_g, ln2_b, loss_target, m_w_ada, m_b_ada, m_w_in, m_b_in, m_sinks, m_gn_sb, m_gn_swa, m_w_out, m_ln1_g, m_ln1_b, m_w_gu, m_w_down, m_ln2_g, m_ln2_b, v_w_ada, v_b_ada, v_w_in, v_b_in, v_sinks, v_gn_sb, v_gn_swa, v_w_out, v_ln1_g, v_ln1_b, v_w_gu, v_w_down, v_ln2_g, v_ln2_b):
    given = dict(x=x, c=c, w_ada=w_ada, b_ada=b_ada, w_in=w_in, b_in=b_in, sinks=sinks, gn_sb=gn_sb, gn_swa=gn_swa, w_out=w_out, ln1_g=ln1_g, ln1_b=ln1_b, w_gu=w_gu, w_down=w_down, ln2_g=ln2_g, ln2_b=ln2_b, loss_target=loss_target, m_w_ada=m_w_ada, m_b_ada=m_b_ada, m_w_in=m_w_in, m_b_in=m_b_in, m_sinks=m_sinks, m_gn_sb=m_gn_sb, m_gn_swa=m_gn_swa, m_w_out=m_w_out, m_ln1_g=m_ln1_g, m_ln1_b=m_ln1_b, m_w_gu=m_w_gu, m_w_down=m_w_down, m_ln2_g=m_ln2_g, m_ln2_b=m_ln2_b, v_w_ada=v_w_ada, v_b_ada=v_b_ada, v_w_in=v_w_in, v_b_in=v_b_in, v_sinks=v_sinks, v_gn_sb=v_gn_sb, v_gn_swa=v_gn_swa, v_w_out=v_w_out, v_ln1_g=v_ln1_g, v_ln1_b=v_ln1_b, v_w_gu=v_w_gu, v_w_down=v_w_down, v_ln2_g=v_ln2_g, v_ln2_b=v_ln2_b)
    weights = {n: given[n] for n in TWIN_WEIGHTS}
    shared = {n: given[n] for n in SHARED_INPUTS}
    per_example = {n: given[n] for n in ['x', 'c']}
    grad_fn = _jax.value_and_grad(_loss, argnums=(0, 1))

    def one_microbatch(ex, loss_target):
        ex = dict(ex)
        diff = ex.pop(TWIN_DIFF_INPUT)
        return grad_fn(weights, diff, {**shared, **ex}, loss_target)

    if N_MICROBATCH == 1:
        loss, (grad_w, grad_x) = one_microbatch(per_example, given["loss_target"])
    else:
        def body(carry, xs):
            loss_sum, grad_sum = carry
            l_k, (gw_k, gx_k) = one_microbatch(xs[0], xs[1])
            with _jax.named_scope("update"):
                return (loss_sum + l_k, _jax.tree.map(_jnp.add, grad_sum, gw_k)), gx_k

        init = (_jnp.zeros((), _jnp.float32), _jax.tree.map(_jnp.zeros_like, weights))
        (loss, grad_w), grad_x = _jax.lax.scan(body, init, (per_example, given["loss_target"]))
    with _jax.named_scope("update"):
        delta_w, new_m, new_v = {}, {}, {}
        for n in TWIN_WEIGHTS:
            delta_w[n], new_m[n], new_v[n] = _adamw(weights[n], grad_w[n], given["m_" + n], given["v_" + n])
    return (loss, grad_x, *[grad_w[n] for n in TWIN_WEIGHTS], *[delta_w[n] for n in TWIN_WEIGHTS],
            *[new_m[n] for n in TWIN_WEIGHTS], *[new_v[n] for n in TWIN_WEIGHTS])
```

```python
import functools
import math

import jax
import jax.numpy as jnp
from jax import lax
from jax.experimental import pallas as pl
from jax.experimental.pallas import tpu as pltpu

F32 = jnp.float32
BF16 = jnp.bfloat16

D = 1024
HEAD_DIM = 64
SB_W = 512
SWA_QW = 512
SWA_KW = 128
D_IN = 2304
D_FF = 2816
WINDOW = 128
ALPHA = 2.0 ** 0.25
LN_EPS = 1e-5
RMS_EPS = 1e-6
MASK_VALUE = -1e30
QK_SCALE = 1.0 / math.sqrt(HEAD_DIM)

ADAM_LR = 0.001
ADAM_B1 = 0.9
ADAM_B2 = 0.999
ADAM_EPS = 1e-08
ADAM_WD = 0.01
ADAM_STEP = 10

N_CHIPS = 4
N_DEV = 8
LANES = 128

SB_TQ = 256
SB_TK = 256
TOK_TILE = 512
FFN_TILE = 256
FFN_BWD_TILE = 128
VMEM_LIMIT = 56 * 1024 * 1024

V_SH_A, V_SC_A, V_G_A, V_SH_F, V_SC_F, V_G_F, V_LN1G, V_LN1B, V_LN2G, V_LN2B, V_GN = range(11)
VEC_ROWS = 16

SM_MOD = 0
SM_BIN = 6 * D
SM_LN1G = SM_BIN + D_IN
SM_LN1B = SM_LN1G + D
SM_LN2G = SM_LN1B + D
SM_LN2B = SM_LN2G + D
SM_GN = SM_LN2B + D
SM_SINK = SM_GN + D
SM_LOSS = SM_SINK + LANES
SM_LEN = SM_LOSS + LANES

MESH = pl.DeviceIdType.MESH


def _cparams(**kw):
    return pltpu.CompilerParams(vmem_limit_bytes=VMEM_LIMIT, **kw)


def _resident(shape):
    nd = len(shape)
    return pl.BlockSpec(shape, lambda *_: (0,) * nd, pipeline_mode=pl.Buffered(1))


def _dot(a, b):
    return jnp.dot(a, b, preferred_element_type=F32)


def _dot_nt(a, b):
    return lax.dot_general(a, b, (((1,), (1,)), ((), ())), preferred_element_type=F32)


def _dot_tn(a, b):
    return lax.dot_general(a, b, (((0,), (0,)), ((), ())), preferred_element_type=F32)


def _split_dot(x, m):
    hi = x.astype(BF16)
    lo = (x - hi.astype(F32)).astype(BF16)
    return _dot(hi, m) + _dot(lo, m)


def _allgather8(v, name):
    m_per, n = v.shape

    def body(x_ref, out_ref, send_sems, recv_sems, local_sem):
        x, y, c = lax.axis_index("x"), lax.axis_index("y"), lax.axis_index("c")
        me, sibling = (x, y, c), (x, y, 1 - c)
        chips = [(1 - x, y), (x, 1 - y), (1 - x, 1 - y)]

        def rows(px, py, pc):
            return out_ref.at[pl.ds((4 * px + 2 * py + pc) * m_per, m_per), :]

        def copy(k, block, to, src=None):
            return pltpu.make_async_remote_copy(
                src_ref=rows(*block) if src is None else src, dst_ref=rows(*block),
                send_sem=send_sems.at[k], recv_sem=recv_sems.at[k], device_id=to, device_id_type=MESH)

        mine = pltpu.make_async_copy(x_ref, rows(*me), local_sem)
        mine.start()
        first = [copy(0, me, sibling, src=x_ref)]
        first += [copy(1 + j, me, (*chip, c), src=x_ref) for j, chip in enumerate(chips)]
        for cp in first:
            cp.start()
        passed = [copy(4 + j, (*chip, c), sibling) for j, chip in enumerate(chips)]
        for j, chip in enumerate(chips):
            copy(1 + j, (*chip, c), me).wait_recv()
            passed[j].start()
        copy(0, sibling, me).wait_recv()
        for j, chip in enumerate(chips):
            copy(4 + j, (*chip, 1 - c), me).wait_recv()
        for cp in first + passed:
            cp.wait_send()
        mine.wait()

    return pl.pallas_call(
        body, name=name,
        out_shape=jax.ShapeDtypeStruct((N_DEV * m_per, n), v.dtype),
        in_specs=[pl.BlockSpec(memory_space=pltpu.VMEM)],
        out_specs=pl.BlockSpec(memory_space=pltpu.VMEM),
        scratch_shapes=[pltpu.SemaphoreType.DMA((7,)), pltpu.SemaphoreType.DMA((7,)), pltpu.SemaphoreType.DMA],
        compiler_params=_cparams(),
    )(v)


def _chip_allgather(arrs, name):
    n = len(arrs)

    def body(*refs):
        ins, outs = refs[:n], refs[n:2 * n]
        send_sems, recv_sems, local_sems = refs[2 * n:]
        x, y, c = lax.axis_index("x"), lax.axis_index("y"), lax.axis_index("c")
        slot = 2 * x + y
        chips = [(1 - x, y), (x, 1 - y), (1 - x, 1 - y)]
        local, sent = [], []
        for a in range(n):
            cp = pltpu.make_async_copy(ins[a], outs[a].at[slot], local_sems.at[a])
            cp.start()
            local.append(cp)
            for j, (px, py) in enumerate(chips):
                cp = pltpu.make_async_remote_copy(
                    src_ref=ins[a], dst_ref=outs[a].at[slot], send_sem=send_sems.at[3 * a + j],
                    recv_sem=recv_sems.at[3 * a + j], device_id=(px, py, c), device_id_type=MESH)
                cp.start()
                sent.append(cp)
        for a in range(n):
            for j, (px, py) in enumerate(chips):
                pltpu.make_async_remote_copy(
                    src_ref=ins[a], dst_ref=outs[a].at[2 * px + py], send_sem=send_sems.at[3 * a + j],
                    recv_sem=recv_sems.at[3 * a + j], device_id=(px, py, c), device_id_type=MESH).wait_recv()
        for cp in sent:
            cp.wait_send()
        for cp in local:
            cp.wait()

    hbm = pl.BlockSpec(memory_space=pl.ANY)
    return pl.pallas_call(
        body, name=name,
        out_shape=[jax.ShapeDtypeStruct((N_CHIPS,) + a.shape, a.dtype) for a in arrs],
        in_specs=[hbm] * n, out_specs=[hbm] * n,
        scratch_shapes=[pltpu.SemaphoreType.DMA((3 * n,)), pltpu.SemaphoreType.DMA((3 * n,)),
                        pltpu.SemaphoreType.DMA((n,))],
        compiler_params=_cparams(),
    )(*arrs)


def _sibling_send(v, name):
    def body(v_ref, out_ref, send_sem, recv_sem):
        x, y, c = lax.axis_index("x"), lax.axis_index("y"), lax.axis_index("c")
        cp = pltpu.make_async_remote_copy(src_ref=v_ref, dst_ref=out_ref, send_sem=send_sem, recv_sem=recv_sem,
                                          device_id=(x, y, 1 - c), device_id_type=MESH)
        cp.start()
        cp.wait()

    hbm = pl.BlockSpec(memory_space=pl.ANY)
    return pl.pallas_call(
        body, name=name, out_shape=jax.ShapeDtypeStruct(v.shape, v.dtype),
        in_specs=[hbm], out_specs=hbm,
        scratch_shapes=[pltpu.SemaphoreType.DMA, pltpu.SemaphoreType.DMA],
        compiler_params=_cparams(),
    )(v)


def _chip_scatter(p, name):
    def body(p_ref, out_ref, send_sems, recv_sems, local_sem):
        x, y, c = lax.axis_index("x"), lax.axis_index("y"), lax.axis_index("c")
        slot = 2 * x + y
        chips = [(1 - x, y), (x, 1 - y), (1 - x, 1 - y)]
        mine = pltpu.make_async_copy(p_ref.at[slot], out_ref.at[slot], local_sem)
        mine.start()
        sent = []
        for j, (px, py) in enumerate(chips):
            cp = pltpu.make_async_remote_copy(
                src_ref=p_ref.at[2 * px + py], dst_ref=out_ref.at[slot], send_sem=send_sems.at[j],
                recv_sem=recv_sems.at[j], device_id=(px, py, c), device_id_type=MESH)
            cp.start()
            sent.append(cp)
        for j, (px, py) in enumerate(chips):
            pltpu.make_async_remote_copy(
                src_ref=p_ref.at[slot], dst_ref=out_ref.at[2 * px + py], send_sem=send_sems.at[j],
                recv_sem=recv_sems.at[j], device_id=(px, py, c), device_id_type=MESH).wait_recv()
        for cp in sent:
            cp.wait_send()
        mine.wait()

    hbm = pl.BlockSpec(memory_space=pl.ANY)
    return pl.pallas_call(
        body, name=name, out_shape=jax.ShapeDtypeStruct(p.shape, p.dtype),
        in_specs=[hbm], out_specs=hbm,
        scratch_shapes=[pltpu.SemaphoreType.DMA((3,)), pltpu.SemaphoreType.DMA((3,)), pltpu.SemaphoreType.DMA],
        compiler_params=_cparams(),
    )(p)


def _add2(a, b, name):
    rows = a.shape[0]
    tr = rows // 8

    def body(a_ref, b_ref, o_ref):
        o_ref[...] = a_ref[...] + b_ref[...]

    spec = pl.BlockSpec((tr, LANES), lambda i: (i, 0))
    return pl.pallas_call(body, name=name, grid=(rows // tr,), out_shape=jax.ShapeDtypeStruct(a.shape, a.dtype),
                          in_specs=[spec, spec], out_specs=spec, compiler_params=_cparams())(a, b)


def _sum4(p, name):
    rows = p.shape[1]
    tr = rows // 8

    def body(p_ref, o_ref):
        o_ref[...] = ((p_ref[0] + p_ref[1]) + p_ref[2]) + p_ref[3]

    return pl.pallas_call(
        body, name=name, grid=(rows // tr,), out_shape=jax.ShapeDtypeStruct((rows, LANES), p.dtype),
        in_specs=[pl.BlockSpec((4, tr, LANES), lambda i: (0, i, 0))],
        out_specs=pl.BlockSpec((tr, LANES), lambda i: (i, 0)), compiler_params=_cparams())(p)


def _adam_math(w, g, m, v):
    m2 = ADAM_B1 * m + (1.0 - ADAM_B1) * g
    v2 = ADAM_B2 * v + (1.0 - ADAM_B2) * (g * g)
    m_hat = m2 / (1.0 - ADAM_B1 ** ADAM_STEP)
    v_hat = v2 / (1.0 - ADAM_B2 ** ADAM_STEP)
    delta = -ADAM_LR * (m_hat / (jnp.sqrt(v_hat) + ADAM_EPS) + ADAM_WD * w)
    return delta, m2, v2


def _adamw(w, g, m, v, name):
    rows, cols = w.shape
    tr = rows // 4 if rows % 32 == 0 else rows

    def body(w_ref, g_ref, m_ref, v_ref, d_ref, m2_ref, v2_ref):
        delta, m2, v2 = _adam_math(w_ref[...], g_ref[...], m_ref[...], v_ref[...])
        d_ref[...] = delta
        m2_ref[...] = m2
        v2_ref[...] = v2

    spec = pl.BlockSpec((tr, cols), lambda i: (i, 0))
    shp = jax.ShapeDtypeStruct(w.shape, F32)
    return pl.pallas_call(body, name=name, grid=(rows // tr,), out_shape=[shp, shp, shp],
                          in_specs=[spec] * 4, out_specs=[spec] * 3, compiler_params=_cparams())(w, g, m, v)


def _small_update(g8, w, m, v, name):
    n = w.shape[1]

    def body(g8_ref, w_ref, m_ref, v_ref, g_ref, d_ref, m2_ref, v2_ref):
        g = g8_ref[0:1, :]
        for r in range(1, N_DEV):
            g = g + g8_ref[r:r + 1, :]
        delta, m2, v2 = _adam_math(w_ref[...], g, m_ref[...], v_ref[...])
        g_ref[...] = g
        d_ref[...] = delta
        m2_ref[...] = m2
        v2_ref[...] = v2

    shp = jax.ShapeDtypeStruct((1, n), F32)
    vm = pl.BlockSpec(memory_space=pltpu.VMEM)
    return pl.pallas_call(body, name=name, out_shape=[shp] * 4, in_specs=[vm] * 4, out_specs=[vm] * 4,
                          compiler_params=_cparams())(g8, w, m, v)


def _mod_shard(c8, w_ada, b_ada_shard, name):
    n = w_ada.shape[1]
    tn = 512

    def body(c_ref, w_ref, b_ref, o_ref, s_ref):
        cv = c_ref[...]
        sc = cv * (1.0 / (1.0 + jnp.exp(-cv)))
        s_ref[...] = sc
        o_ref[...] = _dot(sc.astype(BF16), w_ref[...].astype(BF16)) + b_ref[...]

    return pl.pallas_call(
        body, name=name, grid=(n // tn,),
        out_shape=[jax.ShapeDtypeStruct((8, n), F32), jax.ShapeDtypeStruct((8, D), F32)],
        in_specs=[pl.BlockSpec((8, D), lambda j: (0, 0)), pl.BlockSpec((D, tn), lambda j: (0, j)),
                  pl.BlockSpec((1, tn), lambda j: (0, j))],
        out_specs=[pl.BlockSpec((8, tn), lambda j: (0, j)), pl.BlockSpec((8, D), lambda j: (0, 0))],
        compiler_params=_cparams())(c8, w_ada, b_ada_shard)


def _layer_norm_stats(u):
    mu = jnp.mean(u, axis=1, keepdims=True)
    d = u - mu
    var = jnp.mean(d * d, axis=1, keepdims=True)
    rstd = lax.rsqrt(var + LN_EPS)
    return d * rstd, rstd


def _in_proj(x, vec, w_in, b_in, name):
    s = x.shape[0]
    tb = min(TOK_TILE, s)

    def body(x_ref, vec_ref, w_ref, b_ref, h_ref, p_ref):
        h = x_ref[...] * (1.0 + vec_ref[V_SC_A:V_SC_A + 1, :]) + vec_ref[V_SH_A:V_SH_A + 1, :]
        hb = h.astype(BF16)
        h_ref[...] = hb
        proj = _dot(hb, w_ref[...]) + b_ref[...]
        col = lax.broadcasted_iota(jnp.int32, (1, D_IN), 1)
        is_q = (col < SB_W) | ((col >= 3 * SB_W) & (col < 3 * SB_W + SWA_QW))
        p_ref[...] = (proj * jnp.where(is_q, QK_SCALE, 1.0)).astype(BF16)

    return pl.pallas_call(
        body, name=name, grid=(s // tb,),
        out_shape=[jax.ShapeDtypeStruct((s, D), BF16), jax.ShapeDtypeStruct((s, D_IN), BF16)],
        in_specs=[pl.BlockSpec((tb, D), lambda i: (i, 0)), _resident((VEC_ROWS, D)), _resident((D, D_IN)),
                  _resident((1, D_IN))],
        out_specs=[pl.BlockSpec((tb, D), lambda i: (i, 0)), pl.BlockSpec((tb, D_IN), lambda i: (i, 0))],
        compiler_params=_cparams())(x, vec, w_in, b_in)


def _softplus_parts(z):
    e1 = jnp.exp(-jnp.abs(z))
    sp = jnp.maximum(z, 0.0) + jnp.log(1.0 + e1)
    return sp, e1


def _sb_forward(proj, name):
    s = proj.shape[0]
    tq, tk = min(SB_TQ, s), min(SB_TK, s)
    r = tq // tk

    def body(q_ref, k_ref, v_ref, o_ref, tot_ref, acc_ref, run_ref):
        i, e = pl.program_id(1), pl.program_id(2)
        lane = lax.broadcasted_iota(jnp.int32, (1, LANES), 1)
        hm = (lane // HEAD_DIM) == e
        q = jnp.where(hm, q_ref[...], jnp.zeros((), BF16))
        row = lax.broadcasted_iota(jnp.int32, (tk, tk), 0)
        col = lax.broadcasted_iota(jnp.int32, (tk, tk), 1)
        later = (row > col).astype(BF16)
        acc_ref[...] = jnp.zeros_like(acc_ref)
        run_ref[...] = jnp.zeros_like(run_ref)

        def block(j, masked):
            ks = pl.multiple_of(j * tk, tk)
            kj = k_ref[pl.ds(ks, tk), :]
            vj = v_ref[pl.ds(ks, tk), :]
            z = _dot_nt(q, kj)
            sp, _ = _softplus_parts(z)
            if masked:
                t_idx = i * tq + lax.broadcasted_iota(jnp.int32, (tq, tk), 0)
                s_idx = j * tk + lax.broadcasted_iota(jnp.int32, (tq, tk), 1)
                before = s_idx < t_idx
                spm = jnp.where(before, sp, 0.0)
            else:
                spm = sp
            a = z - sp - _split_dot(spm, later) - run_ref[...]
            w = jnp.exp(a)
            if masked:
                w = jnp.where(before, w, 0.0)
            acc_ref[...] += _dot(w.astype(BF16), vj)
            run_ref[...] += jnp.sum(spm, axis=1, keepdims=True)

        for d in range(r):
            block(i * r + (r - 1 - d), True)

        def sweep(n, carry):
            block(i * r - 1 - n, False)
            return carry

        lax.fori_loop(0, i * r, sweep, 0)
        res = jnp.where(hm, acc_ref[...], 0.0)
        tot = jnp.where(hm, run_ref[...], 0.0)

        @pl.when(e == 0)
        def _():
            o_ref[...] = res
            tot_ref[...] = tot

        @pl.when(e == 1)
        def _():
            o_ref[...] += res
            tot_ref[...] += tot

    nkb = SB_W // LANES
    shp = jax.ShapeDtypeStruct((s, SB_W), F32)
    qspec = pl.BlockSpec((tq, LANES), lambda p, i, e: (i, p))
    return pl.pallas_call(
        body, name=name, grid=(nkb, s // tq, 2),
        out_shape=[shp, shp],
        in_specs=[qspec,
                  pl.BlockSpec((s, LANES), lambda p, i, e: (0, nkb + p)),
                  pl.BlockSpec((s, LANES), lambda p, i, e: (0, 2 * nkb + p))],
        out_specs=[qspec, qspec],
        scratch_shapes=[pltpu.VMEM((tq, LANES), F32), pltpu.VMEM((tq, 1), F32)],
        compiler_params=_cparams())(proj, proj, proj)


def _swa_masks(n):
    ti = lax.broadcasted_iota(jnp.int32, (WINDOW, 2 * WINDOW), 0)
    kj = lax.broadcasted_iota(jnp.int32, (WINDOW, 2 * WINDOW), 1)
    dist = ti + WINDOW - kj
    valid = (dist >= 0) & (dist < WINDOW) & ((n * WINDOW - WINDOW + kj) >= 0)
    return valid, dist.astype(F32)


def _swa_probs(qm, ku, valid, distf, h, sink):
    slope = 2.0 ** (-(h + 1))
    sc = _dot_nt(qm, ku)
    sc = jnp.where(valid, sc - slope * distf, MASK_VALUE)
    mx = jnp.maximum(jnp.max(sc, axis=1, keepdims=True), sink)
    p = jnp.exp(sc - mx)
    es = jnp.exp(sink - mx)
    inv = 1.0 / (jnp.sum(p, axis=1, keepdims=True) + es)
    return p * inv, es * inv


def _swa_forward(proj, sinks, name):
    s = proj.shape[0]
    nb = s // WINDOW
    qb, kb, vb = 3 * SB_W // SWA_QW, (3 * SB_W + SWA_QW) // LANES, (3 * SB_W + SWA_QW + SWA_KW) // LANES

    def body(q_ref, kp_ref, kc_ref, vp_ref, vc_ref, sink_ref, o_ref):
        n = pl.program_id(0)
        k = jnp.concatenate([kp_ref[...], kc_ref[...]], axis=0)
        v = jnp.concatenate([vp_ref[...], vc_ref[...]], axis=0)
        k_sw = pltpu.roll(k.astype(F32), HEAD_DIM, 1).astype(BF16)
        v_sw = pltpu.roll(v.astype(F32), HEAD_DIM, 1).astype(BF16)
        lane = lax.broadcasted_iota(jnp.int32, (1, LANES), 1)
        halves = [lane < HEAD_DIM, lane >= HEAD_DIM]
        valid, distf = _swa_masks(n)
        for pair in range(4):
            qp = q_ref[:, pair * LANES:(pair + 1) * LANES]
            out = jnp.zeros((WINDOW, LANES), F32)
            for par in range(2):
                h = 2 * pair + par
                g = h // 4
                qm = jnp.where(halves[par], qp, jnp.zeros((), BF16))
                ku, vu = (k, v) if g == par else (k_sw, v_sw)
                p, _ = _swa_probs(qm, ku, valid, distf, h, sink_ref[h])
                out = out + jnp.where(halves[par], _dot(p.astype(BF16), vu), 0.0)
            o_ref[:, pair * LANES:(pair + 1) * LANES] = out

    prev = lambda n: jnp.maximum(n - 1, 0)
    return pl.pallas_call(
        body, name=name, grid=(nb,),
        out_shape=jax.ShapeDtypeStruct((s, SWA_QW), F32),
        in_specs=[pl.BlockSpec((WINDOW, SWA_QW), lambda n: (n, qb)),
                  pl.BlockSpec((WINDOW, LANES), lambda n: (prev(n), kb)),
                  pl.BlockSpec((WINDOW, LANES), lambda n: (n, kb)),
                  pl.BlockSpec((WINDOW, LANES), lambda n: (prev(n), vb)),
                  pl.BlockSpec((WINDOW, LANES), lambda n: (n, vb)),
                  pl.BlockSpec(memory_space=pltpu.SMEM)],
        out_specs=pl.BlockSpec((WINDOW, SWA_QW), lambda n: (n, 0)),
        compiler_params=_cparams())(proj, proj, proj, proj, proj, sinks)


def _rms_parts(y):
    return lax.rsqrt(jnp.mean(y * y, axis=1, keepdims=True) + RMS_EPS)


def _post_attention(y_sb, y_sw, x, vec, w_out, name):
    s = x.shape[0]
    tb = min(TOK_TILE, s)

    def body(ysb_ref, ysw_ref, x_ref, vec_ref, w_ref, mixed_ref, attn_ref, x1_ref, h2_ref):
        ysb, ysw = ysb_ref[...], ysw_ref[...]
        nsb = (ysb * _rms_parts(ysb) * vec_ref[V_GN:V_GN + 1, :SB_W]).astype(BF16)
        nsw = (ysw * _rms_parts(ysw) * vec_ref[V_GN:V_GN + 1, SB_W:]).astype(BF16)
        mixed_ref[:, :SB_W] = nsb
        mixed_ref[:, SB_W:] = nsw
        attn = _dot(nsb, w_ref[:SB_W, :]) + _dot(nsw, w_ref[SB_W:, :])
        attn_ref[...] = attn
        u1 = ALPHA * x_ref[...] + (1.0 + vec_ref[V_G_A:V_G_A + 1, :]) * attn
        xhat, _ = _layer_norm_stats(u1)
        x1 = xhat * vec_ref[V_LN1G:V_LN1G + 1, :] + vec_ref[V_LN1B:V_LN1B + 1, :]
        x1_ref[...] = x1
        h2_ref[...] = (x1 * (1.0 + vec_ref[V_SC_F:V_SC_F + 1, :]) + vec_ref[V_SH_F:V_SH_F + 1, :]).astype(BF16)

    half = pl.BlockSpec((tb, SB_W), lambda i: (i, 0))
    full = pl.BlockSpec((tb, D), lambda i: (i, 0))
    return pl.pallas_call(
        body, name=name, grid=(s // tb,),
        out_shape=[jax.ShapeDtypeStruct((s, D), BF16), jax.ShapeDtypeStruct((s, D), F32),
                   jax.ShapeDtypeStruct((s, D), F32), jax.ShapeDtypeStruct((s, D), BF16)],
        in_specs=[half, half, full, _resident((VEC_ROWS, D)), _resident((D, D))],
        out_specs=[full, full, full, full],
        compiler_params=_cparams())(y_sb, y_sw, x, vec, w_out)


def _ffn_forward(h2, w_gu, w_down, name):
    s = h2.shape[0]
    tb = min(FFN_TILE, s)

    def body(h_ref, wgu_ref, wd_ref, gu_ref, act_ref, ffn_ref):
        gu = _dot(h_ref[...], wgu_ref[...])
        gu_ref[...] = gu
        gate, up = gu[:, :D_FF], gu[:, D_FF:]
        act = (gate * (1.0 / (1.0 + jnp.exp(-gate))) * up).astype(BF16)
        act_ref[...] = act
        ffn_ref[...] = _dot(act, wd_ref[...])

    return pl.pallas_call(
        body, name=name, grid=(s // tb,),
        out_shape=[jax.ShapeDtypeStruct((s, 2 * D_FF), F32), jax.ShapeDtypeStruct((s, D_FF), BF16),
                   jax.ShapeDtypeStruct((s, D), F32)],
        in_specs=[pl.BlockSpec((tb, D), lambda i: (i, 0)), _resident((D, 2 * D_FF)), _resident((D_FF, D))],
        out_specs=[pl.BlockSpec((tb, 2 * D_FF), lambda i: (i, 0)), pl.BlockSpec((tb, D_FF), lambda i: (i, 0)),
                   pl.BlockSpec((tb, D), lambda i: (i, 0))],
        compiler_params=_cparams())(h2, w_gu, w_down)


def _layer_norm_bwd(dxhat, xhat, rstd):
    m1 = jnp.mean(dxhat, axis=1, keepdims=True)
    m2 = jnp.mean(dxhat * xhat, axis=1, keepdims=True)
    return rstd * (dxhat - m1 - xhat * m2)


def _colsum(a):
    return jnp.sum(a, axis=0, keepdims=True)


A_LN2G, A_LN2B, A_GF, A_SCF, A_SHF, A_LOSS = range(6)
B_LN1G, B_LN1B, B_GA, B_GN = range(4)
C_SCA, C_SHA = range(2)


def _ffn_backward(x1, ffn, target, gu, vec, w_gu, w_down, name):
    s = x1.shape[0]
    tb = min(FFN_BWD_TILE, s)

    def body(x1_ref, ffn_ref, t_ref, gu_ref, vec_ref, wgu_ref, wd_ref, dffn_ref, dgu_ref, dx1_ref, acc_ref):
        @pl.when(pl.program_id(0) == 0)
        def _():
            acc_ref[...] = jnp.zeros_like(acc_ref)

        x1v, ffn_v = x1_ref[...], ffn_ref[...]
        g_f = 1.0 + vec_ref[V_G_F:V_G_F + 1, :]
        u2 = ALPHA * x1v + g_f * ffn_v
        xhat, rstd = _layer_norm_stats(u2)
        ln_g = vec_ref[V_LN2G:V_LN2G + 1, :]
        err = xhat * ln_g + vec_ref[V_LN2B:V_LN2B + 1, :] - t_ref[...]
        dx2 = err * (1.0 / D)
        acc_ref[A_LOSS:A_LOSS + 1, :] += _colsum(err * err) * (0.5 / D)
        acc_ref[A_LN2G:A_LN2G + 1, :] += _colsum(dx2 * xhat)
        acc_ref[A_LN2B:A_LN2B + 1, :] += _colsum(dx2)
        du2 = _layer_norm_bwd(dx2 * ln_g, xhat, rstd)
        acc_ref[A_GF:A_GF + 1, :] += _colsum(du2 * ffn_v)
        dffn = (g_f * du2).astype(BF16)
        dffn_ref[...] = dffn
        dact = _dot_nt(dffn, wd_ref[...])
        gate, up = gu_ref[:, :D_FF], gu_ref[:, D_FF:]
        sg = 1.0 / (1.0 + jnp.exp(-gate))
        dgate = (dact * up * (sg * (1.0 + gate * (1.0 - sg)))).astype(BF16)
        dup = (dact * (gate * sg)).astype(BF16)
        dgu_ref[:, :D_FF] = dgate
        dgu_ref[:, D_FF:] = dup
        dh2 = _dot_nt(dgate, wgu_ref[:, :D_FF]) + _dot_nt(dup, wgu_ref[:, D_FF:])
        dx1_ref[...] = ALPHA * du2 + dh2 * (1.0 + vec_ref[V_SC_F:V_SC_F + 1, :])
        acc_ref[A_SCF:A_SCF + 1, :] += _colsum(dh2 * x1v)
        acc_ref[A_SHF:A_SHF + 1, :] += _colsum(dh2)

    full = pl.BlockSpec((tb, D), lambda i: (i, 0))
    wide = pl.BlockSpec((tb, 2 * D_FF), lambda i: (i, 0))
    return pl.pallas_call(
        body, name=name, grid=(s // tb,),
        out_shape=[jax.ShapeDtypeStruct((s, D), BF16), jax.ShapeDtypeStruct((s, 2 * D_FF), BF16),
                   jax.ShapeDtypeStruct((s, D), F32), jax.ShapeDtypeStruct((8, D), F32)],
        in_specs=[full, full, full, wide, _resident((VEC_ROWS, D)), _resident((D, 2 * D_FF)), _resident((D_FF, D))],
        out_specs=[full, wide, full, pl.BlockSpec((8, D), lambda i: (0, 0))],
        compiler_params=_cparams())(x1, ffn, target, gu, vec, w_gu, w_down)


def _attn_out_backward(dx1, x, attn, y_sb, y_sw, vec, w_out, name):
    s = x.shape[0]
    tb = min(TOK_TILE, s)

    def body(dx1_ref, x_ref, attn_ref, ysb_ref, ysw_ref, vec_ref, w_ref, du1_ref, dattn_ref, dy_ref, acc_ref):
        @pl.when(pl.program_id(0) == 0)
        def _():
            acc_ref[...] = jnp.zeros_like(acc_ref)

        attn = attn_ref[...]
        g_a = 1.0 + vec_ref[V_G_A:V_G_A + 1, :]
        xhat, rstd = _layer_norm_stats(ALPHA * x_ref[...] + g_a * attn)
        dx1v = dx1_ref[...]
        acc_ref[B_LN1G:B_LN1G + 1, :] += _colsum(dx1v * xhat)
        acc_ref[B_LN1B:B_LN1B + 1, :] += _colsum(dx1v)
        du1 = _layer_norm_bwd(dx1v * vec_ref[V_LN1G:V_LN1G + 1, :], xhat, rstd)
        du1_ref[...] = du1
        acc_ref[B_GA:B_GA + 1, :] += _colsum(du1 * attn)
        dattn = (g_a * du1).astype(BF16)
        dattn_ref[...] = dattn
        dmixed = _dot_nt(dattn, w_ref[...])
        for lo, y_ref in ((0, ysb_ref), (SB_W, ysw_ref)):
            y = y_ref[...]
            rr = _rms_parts(y)
            dn = dmixed[:, lo:lo + SB_W]
            acc_ref[B_GN:B_GN + 1, lo:lo + SB_W] += _colsum(dn * y * rr)
            dng = dn * vec_ref[V_GN:V_GN + 1, lo:lo + SB_W]
            dy_ref[:, lo:lo + SB_W] = rr * dng - y * (rr * rr * rr) * jnp.mean(dng * y, axis=1, keepdims=True)

    half = pl.BlockSpec((tb, SB_W), lambda i: (i, 0))
    full = pl.BlockSpec((tb, D), lambda i: (i, 0))
    return pl.pallas_call(
        body, name=name, grid=(s // tb,),
        out_shape=[jax.ShapeDtypeStruct((s, D), F32), jax.ShapeDtypeStruct((s, D), BF16),
                   jax.ShapeDtypeStruct((s, D), F32), jax.ShapeDtypeStruct((8, D), F32)],
        in_specs=[full, full, full, half, half, _resident((VEC_ROWS, D)), _resident((D, D))],
        out_specs=[full, full, full, pl.BlockSpec((8, D), lambda i: (0, 0))],
        compiler_params=_cparams())(dx1, x, attn, y_sb, y_sw, vec, w_out)


def _sb_backward(proj, sp_total, dy, name):
    s = proj.shape[0]
    tq, tk = min(SB_TQ, s), min(SB_TK, s)
    r = tq // tk

    def body(q_ref, k_ref, v_ref, tot_ref, do_ref, dq_ref, dk_ref, dv_ref, dq_acc, left_ref, gsum_ref):
        i, e = pl.program_id(1), pl.program_id(2)

        @pl.when((i == 0) & (e == 0))
        def _():
            dk_ref[...] = jnp.zeros_like(dk_ref)
            dv_ref[...] = jnp.zeros_like(dv_ref)

        lane = lax.broadcasted_iota(jnp.int32, (1, LANES), 1)
        hm = (lane // HEAD_DIM) == e
        q = jnp.where(hm, q_ref[...], jnp.zeros((), BF16))
        dob = jnp.where(hm, do_ref[...], 0.0).astype(BF16)
        row = lax.broadcasted_iota(jnp.int32, (tk, tk), 0)
        col = lax.broadcasted_iota(jnp.int32, (tk, tk), 1)
        later = (row > col).astype(BF16)
        earlier = (row < col).astype(BF16)
        dq_acc[...] = jnp.zeros_like(dq_acc)
        gsum_ref[...] = jnp.zeros_like(gsum_ref)
        left_ref[...] = jnp.max(jnp.where(hm, tot_ref[...], 0.0), axis=1, keepdims=True)

        def block(j, masked):
            ks = pl.multiple_of(j * tk, tk)
            kj = k_ref[pl.ds(ks, tk), :]
            vj = v_ref[pl.ds(ks, tk), :]
            z = _dot_nt(q, kj)
            sp, e1 = _softplus_parts(z)
            if masked:
                t_idx = i * tq + lax.broadcasted_iota(jnp.int32, (tq, tk), 0)
                s_idx = j * tk + lax.broadcasted_iota(jnp.int32, (tq, tk), 1)
                before = s_idx < t_idx
                spm = jnp.where(before, sp, 0.0)
            else:
                spm = sp
            after_block = left_ref[...] - jnp.sum(spm, axis=1, keepdims=True)
            a = z - sp - _split_dot(spm, later) - after_block
            w = jnp.exp(a)
            if masked:
                w = jnp.where(before, w, 0.0)
            dv_ref[pl.ds(ks, tk), :] += _dot_tn(w.astype(BF16), dob)
            g = _dot_nt(dob, vj) * w
            g_before = gsum_ref[...] + _split_dot(g, earlier)
            inv = 1.0 / (1.0 + e1)
            sig = jnp.where(z >= 0.0, inv, e1 * inv)
            dz = g - sig * (g + g_before)
            if masked:
                dz = jnp.where(before, dz, 0.0)
            dzb = dz.astype(BF16)
            dq_acc[...] += _dot(dzb, kj)
            dk_ref[pl.ds(ks, tk), :] += _dot_tn(dzb, q)
            left_ref[...] = after_block
            gsum_ref[...] += jnp.sum(g, axis=1, keepdims=True)

        def sweep(j, carry):
            block(j, False)
            return carry

        lax.fori_loop(0, i * r, sweep, 0)
        for d in range(r):
            block(i * r + d, True)
        res = jnp.where(hm, dq_acc[...], 0.0)

        @pl.when(e == 0)
        def _():
            dq_ref[...] = res

        @pl.when(e == 1)
        def _():
            dq_ref[...] += res

    nkb = SB_W // LANES
    shp = jax.ShapeDtypeStruct((s, SB_W), F32)
    qspec = pl.BlockSpec((tq, LANES), lambda p, i, e: (i, p))
    whole = pl.BlockSpec((s, LANES), lambda p, i, e: (0, p))
    return pl.pallas_call(
        body, name=name, grid=(nkb, s // tq, 2),
        out_shape=[shp, shp, shp],
        in_specs=[qspec,
                  pl.BlockSpec((s, LANES), lambda p, i, e: (0, nkb + p)),
                  pl.BlockSpec((s, LANES), lambda p, i, e: (0, 2 * nkb + p)),
                  qspec, qspec],
        out_specs=[qspec, whole, whole],
        scratch_shapes=[pltpu.VMEM((tq, LANES), F32), pltpu.VMEM((tq, 1), F32), pltpu.VMEM((tq, 1), F32)],
        compiler_params=_cparams())(proj, proj, proj, sp_total, dy)


def _swa_backward(proj, y_sw, dy, sinks, name):
    s = proj.shape[0]
    nb = s // WINDOW
    qb, kb, vb = 3 * SB_W // SWA_QW, (3 * SB_W + SWA_QW) // LANES, (3 * SB_W + SWA_QW + SWA_KW) // LANES

    def body(q_ref, kp_ref, kc_ref, vp_ref, vc_ref, o_ref, do_ref, sink_ref, dq_ref, dk_ref, dv_ref, ds_ref):
        n = pl.program_id(0)

        @pl.when(n == 0)
        def _():
            dk_ref[...] = jnp.zeros_like(dk_ref)
            dv_ref[...] = jnp.zeros_like(dv_ref)
            ds_ref[...] = jnp.zeros_like(ds_ref)

        k = jnp.concatenate([kp_ref[...], kc_ref[...]], axis=0)
        v = jnp.concatenate([vp_ref[...], vc_ref[...]], axis=0)
        k_sw = pltpu.roll(k.astype(F32), HEAD_DIM, 1).astype(BF16)
        v_sw = pltpu.roll(v.astype(F32), HEAD_DIM, 1).astype(BF16)
        lane = lax.broadcasted_iota(jnp.int32, (1, LANES), 1)
        halves = [lane < HEAD_DIM, lane >= HEAD_DIM]
        valid, distf = _swa_masks(n)
        zero = jnp.zeros((2 * WINDOW, LANES), F32)
        dk_nat, dk_rot, dv_nat, dv_rot = zero, zero, zero, zero
        for pair in range(4):
            cols = slice(pair * LANES, (pair + 1) * LANES)
            qp = q_ref[:, cols]
            dop, op = do_ref[:, cols], o_ref[:, cols]
            dq = jnp.zeros((WINDOW, LANES), F32)
            for par in range(2):
                h = 2 * pair + par
                g = h // 4
                qm = jnp.where(halves[par], qp, jnp.zeros((), BF16))
                do = jnp.where(halves[par], dop, 0.0)
                dob = do.astype(BF16)
                delta = jnp.sum(do * op, axis=1, keepdims=True)
                ku, vu = (k, v) if g == par else (k_sw, v_sw)
                p, p_sink = _swa_probs(qm, ku, valid, distf, h, sink_ref[h])
                dsc = (p * (_dot_nt(dob, vu) - delta)).astype(BF16)
                ds_ref[h:h + 1, :] += jnp.zeros((1, LANES), F32) - jnp.sum(p_sink * delta)
                dq = dq + jnp.where(halves[par], _dot(dsc, ku), 0.0)
                dk_h = _dot_tn(dsc, qm)
                dv_h = _dot_tn(p.astype(BF16), dob)
                if g == par:
                    dk_nat, dv_nat = dk_nat + dk_h, dv_nat + dv_h
                else:
                    dk_rot, dv_rot = dk_rot + dk_h, dv_rot + dv_h
            dq_ref[:, cols] = dq
        dk = dk_nat + pltpu.roll(dk_rot, HEAD_DIM, 1)
        dv = dv_nat + pltpu.roll(dv_rot, HEAD_DIM, 1)
        prev = pl.multiple_of(jnp.maximum(n - 1, 0) * WINDOW, WINDOW)
        cur = pl.multiple_of(n * WINDOW, WINDOW)
        dk_ref[pl.ds(prev, WINDOW), :] += dk[:WINDOW]
        dv_ref[pl.ds(prev, WINDOW), :] += dv[:WINDOW]
        dk_ref[pl.ds(cur, WINDOW), :] += dk[WINDOW:]
        dv_ref[pl.ds(cur, WINDOW), :] += dv[WINDOW:]

    prev_blk = lambda n: jnp.maximum(n - 1, 0)
    wide = pl.BlockSpec((WINDOW, SWA_QW), lambda n: (n, 0))
    whole = pl.BlockSpec((s, LANES), lambda n: (0, 0))
    return pl.pallas_call(
        body, name=name, grid=(nb,),
        out_shape=[jax.ShapeDtypeStruct((s, SWA_QW), F32), jax.ShapeDtypeStruct((s, LANES), F32),
                   jax.ShapeDtypeStruct((s, LANES), F32), jax.ShapeDtypeStruct((8, LANES), F32)],
        in_specs=[pl.BlockSpec((WINDOW, SWA_QW), lambda n: (n, qb)),
                  pl.BlockSpec((WINDOW, LANES), lambda n: (prev_blk(n), kb)),
                  pl.BlockSpec((WINDOW, LANES), lambda n: (n, kb)),
                  pl.BlockSpec((WINDOW, LANES), lambda n: (prev_blk(n), vb)),
                  pl.BlockSpec((WINDOW, LANES), lambda n: (n, vb)),
                  wide,
                  pl.BlockSpec((WINDOW, SWA_QW), lambda n: (n, 1)),
                  pl.BlockSpec(memory_space=pltpu.SMEM)],
        out_specs=[wide, whole, whole, pl.BlockSpec((8, LANES), lambda n: (0, 0))],
        compiler_params=_cparams())(proj, proj, proj, proj, proj, y_sw, dy, sinks)


def _in_proj_backward(dq_sb, dk_sb, dv_sb, dq_sw, dk_sw, dv_sw, du1, x, vec, w_in, name):
    s = x.shape[0]
    tb = min(TOK_TILE, s)

    def body(dqsb_ref, dksb_ref, dvsb_ref, dqsw_ref, dksw_ref, dvsw_ref, du1_ref, x_ref, vec_ref, w_ref,
             dproj_ref, gx_ref, acc_ref, bacc_ref):
        @pl.when(pl.program_id(0) == 0)
        def _():
            acc_ref[...] = jnp.zeros_like(acc_ref)
            bacc_ref[...] = jnp.zeros_like(bacc_ref)

        pieces = ((0, dqsb_ref, QK_SCALE), (SB_W, dksb_ref, 1.0), (2 * SB_W, dvsb_ref, 1.0),
                  (3 * SB_W, dqsw_ref, QK_SCALE), (3 * SB_W + SWA_QW, dksw_ref, 1.0),
                  (3 * SB_W + SWA_QW + SWA_KW, dvsw_ref, 1.0))
        for lo, ref, scale in pieces:
            width = ref.shape[1]
            piece = ref[...] * scale
            bacc_ref[0:1, lo:lo + width] += _colsum(piece)
            dproj_ref[:, lo:lo + width] = piece.astype(BF16)
        dh = _dot_nt(dproj_ref[...], w_ref[...])
        xv = x_ref[...]
        gx_ref[...] = ALPHA * du1_ref[...] + dh * (1.0 + vec_ref[V_SC_A:V_SC_A + 1, :])
        acc_ref[C_SCA:C_SCA + 1, :] += _colsum(dh * xv)
        acc_ref[C_SHA:C_SHA + 1, :] += _colsum(dh)

    half = pl.BlockSpec((tb, SB_W), lambda i: (i, 0))
    narrow = pl.BlockSpec((tb, LANES), lambda i: (i, 0))
    full = pl.BlockSpec((tb, D), lambda i: (i, 0))
    return pl.pallas_call(
        body, name=name, grid=(s // tb,),
        out_shape=[jax.ShapeDtypeStruct((s, D_IN), BF16), jax.ShapeDtypeStruct((s, D), F32),
                   jax.ShapeDtypeStruct((8, D), F32), jax.ShapeDtypeStruct((8, D_IN), F32)],
        in_specs=[half, half, half, half, narrow, narrow, full, full, _resident((VEC_ROWS, D)), _resident((D, D_IN))],
        out_specs=[pl.BlockSpec((tb, D_IN), lambda i: (i, 0)), full, pl.BlockSpec((8, D), lambda i: (0, 0)),
                   pl.BlockSpec((8, D_IN), lambda i: (0, 0))],
        compiler_params=_cparams())(dq_sb, dk_sb, dv_sb, dq_sw, dk_sw, dv_sw, du1, x, vec, w_in)


def _weight_grad(a, b, name):
    s, m = a.shape
    n = b.shape[1]
    tn = 512 if n % 512 == 0 else n
    ts = min(512, s)

    def body(a_ref, b_ref, o_ref):
        @pl.when(pl.program_id(1) == 0)
        def _():
            o_ref[...] = jnp.zeros_like(o_ref)

        o_ref[...] += _dot_tn(a_ref[...], b_ref[...])

    return pl.pallas_call(
        body, name=name, grid=(n // tn, s // ts),
        out_shape=jax.ShapeDtypeStruct((m, n), F32),
        in_specs=[pl.BlockSpec((ts, m), lambda j, k: (k, 0)), pl.BlockSpec((ts, tn), lambda j, k: (k, j))],
        out_specs=pl.BlockSpec((m, tn), lambda j, k: (0, j)),
        compiler_params=_cparams())(a, b)


def _pad_rows(v, rows):
    return jnp.concatenate([v, jnp.zeros((rows - v.shape[0], v.shape[1]), v.dtype)], axis=0)


def _col_shards(w, n_shards):
    r, n = w.shape
    return w.reshape(r, n_shards, n // n_shards).transpose(1, 0, 2)


def kernel(x, c, w_ada, b_ada, w_in, b_in, sinks, gn_sb, gn_swa, w_out, ln1_g, ln1_b, w_gu, w_down, ln2_g, ln2_b, loss_target, m_w_ada, m_b_ada, m_w_in, m_b_in, m_sinks, m_gn_sb, m_gn_swa, m_w_out, m_ln1_g, m_ln1_b, m_w_gu, m_w_down, m_ln2_g, m_ln2_b, v_w_ada, v_b_ada, v_w_in, v_b_in, v_sinks, v_gn_sb, v_gn_swa, v_w_out, v_ln1_g, v_ln1_b, v_w_gu, v_w_down, v_ln2_g, v_ln2_b):
    ix, iy, ic = lax.axis_index("x"), lax.axis_index("y"), lax.axis_index("c")
    chip = 2 * ix + iy
    dev = 4 * ix + 2 * iy + ic
    xs, target = x[0], loss_target[0]
    s = xs.shape[0]

    c_all = _allgather8(_pad_rows(c, 8), "gather_c")[::8]
    n_ada = w_ada.shape[2]
    b_ada_shard = lax.dynamic_slice_in_dim(b_ada, chip * n_ada, n_ada, axis=1)
    mod_cols, silu_c = _mod_shard(c_all, w_ada[0], b_ada_shard, "mod_shard")
    mod_all = _allgather8(mod_cols, "gather_mod").reshape(N_DEV, 8, n_ada)
    mod_mine = lax.dynamic_index_in_dim(mod_all, dev, axis=1, keepdims=False)
    mod = mod_mine.reshape(N_CHIPS, 2, n_ada)[:, 0].reshape(6, D)
    vec = jnp.concatenate([mod, ln1_g, ln1_b, ln2_g, ln2_b, jnp.concatenate([gn_sb, gn_swa], axis=1),
                           jnp.zeros((VEC_ROWS - 11, D), F32)], axis=0)

    g_in, g_out, g_gu, g_down = _chip_allgather(
        [w_in[0].astype(BF16), w_out[0].astype(BF16), w_gu[0].astype(BF16), w_down[0].astype(BF16)], "gather_weights")
    w_in_b = g_in.transpose(1, 0, 2).reshape(D, D_IN)
    w_gu_b = g_gu.transpose(1, 0, 2).reshape(D, 2 * D_FF)
    w_out_b = g_out.reshape(D, D)
    w_down_b = g_down.reshape(D_FF, D)

    h_b, proj = _in_proj(xs, vec, w_in_b, b_in, "in_proj")
    y_sb, sp_total = _sb_forward(proj, "sb_forward")
    sink_vec = sinks[0]
    y_sw = _swa_forward(proj, sink_vec, "swa_forward")
    mixed_b, attn, x1, h2_b = _post_attention(y_sb, y_sw, xs, vec, w_out_b, "post_attention")
    gu, act_b, ffn = _ffn_forward(h2_b, w_gu_b, w_down_b, "ffn_forward")

    dffn_b, dgu_b, dx1, acc_f = _ffn_backward(x1, ffn, target, gu, vec, w_gu_b, w_down_b, "ffn_backward")
    du1, dattn_b, dy, acc_a = _attn_out_backward(dx1, xs, attn, y_sb, y_sw, vec, w_out_b, "attn_out_backward")
    dq_sb, dk_sb, dv_sb = _sb_backward(proj, sp_total, dy, "sb_backward")
    dq_sw, dk_sw, dv_sw, dsink = _swa_backward(proj, y_sw, dy, sink_vec, "swa_backward")
    dproj_b, grad_x, acc_i, acc_b = _in_proj_backward(dq_sb, dk_sb, dv_sb, dq_sw, dk_sw, dv_sw, du1, xs, vec, w_in_b,
                                                      "in_proj_backward")
    dw_in = _weight_grad(h_b, dproj_b, "grad_w_in")
    dw_out = _weight_grad(mixed_b, dattn_b, "grad_w_out")
    dw_gu = _weight_grad(h2_b, dgu_b, "grad_w_gu")
    dw_down = _weight_grad(act_b, dffn_b, "grad_w_down")

    shard_sizes = [D * (D_IN // 4), (D // 4) * D, D * (2 * D_FF // 4), (D_FF // 4) * D]
    flat = jnp.concatenate([_col_shards(dw_in, 4).reshape(4, -1), dw_out.reshape(4, -1),
                            _col_shards(dw_gu, 4).reshape(4, -1), dw_down.reshape(4, -1)], axis=1)
    half_rows = flat.shape[1] // (2 * LANES)
    halves = flat.reshape(4, 2, half_rows, LANES)
    keep = lax.dynamic_index_in_dim(halves, ic, axis=1, keepdims=False)
    give = lax.dynamic_index_in_dim(halves, 1 - ic, axis=1, keepdims=False)
    got = _sibling_send(give, "grad_halves_swap")
    chip_sum = _add2(keep.reshape(4 * half_rows, LANES), got.reshape(4 * half_rows, LANES), "grad_chip_sum")
    parts = _chip_scatter(chip_sum.reshape(4, half_rows, LANES), "grad_chip_scatter")
    my_half = _sum4(parts, "grad_reduce")
    other_half = _sibling_send(my_half, "grad_half_return")
    first = jnp.where(ic == 0, my_half, other_half)
    second = jnp.where(ic == 0, other_half, my_half)
    shard_flat = jnp.concatenate([first, second], axis=0).reshape(-1)
    offs = [0]
    for n in shard_sizes:
        offs.append(offs[-1] + n)
    gw_in = shard_flat[offs[0]:offs[1]].reshape(D, D_IN // 4)
    gw_out = shard_flat[offs[1]:offs[2]].reshape(D // 4, D)
    gw_gu = shard_flat[offs[2]:offs[3]].reshape(D, 2 * D_FF // 4)
    gw_down = shard_flat[offs[3]:offs[4]].reshape(D_FF // 4, D)

    dmod = jnp.concatenate([acc_i[C_SHA:C_SHA + 1], acc_i[C_SCA:C_SCA + 1], acc_a[B_GA:B_GA + 1],
                            acc_f[A_SHF:A_SHF + 1], acc_f[A_SCF:A_SCF + 1], acc_f[A_GF:A_GF + 1]], axis=1)
    dsink_row = jnp.concatenate([dsink[:, 0].reshape(1, 8), jnp.zeros((1, LANES - 8), F32)], axis=1)
    loss_row = jnp.concatenate([jnp.sum(acc_f[A_LOSS:A_LOSS + 1], axis=1, keepdims=True),
                                jnp.zeros((1, LANES - 1), F32)], axis=1)
    small = jnp.concatenate([dmod, acc_b[0:1], acc_a[B_LN1G:B_LN1G + 1], acc_a[B_LN1B:B_LN1B + 1],
                             acc_f[A_LN2G:A_LN2G + 1], acc_f[A_LN2B:A_LN2B + 1], acc_a[B_GN:B_GN + 1],
                             dsink_row, loss_row], axis=1)
    small_all = _allgather8(_pad_rows(small, 8), "gather_small")[::8]

    def pack_small(b_ada_, b_in_, ln1g_, ln1b_, ln2g_, ln2b_, gsb_, gsw_, sinks_):
        return jnp.concatenate([b_ada_, b_in_, ln1g_, ln1b_, ln2g_, ln2b_, gsb_, gsw_, sinks_,
                                jnp.ones((1, 2 * LANES - 8), F32)], axis=1)

    w_small = pack_small(b_ada, b_in, ln1_g, ln1_b, ln2_g, ln2_b, gn_sb, gn_swa, sinks)
    m_small = pack_small(m_b_ada, m_b_in, m_ln1_g, m_ln1_b, m_ln2_g, m_ln2_b, m_gn_sb, m_gn_swa, m_sinks)
    v_small = pack_small(v_b_ada, v_b_in, v_ln1_g, v_ln1_b, v_ln2_g, v_ln2_b, v_gn_sb, v_gn_swa, v_sinks)
    small_out = _small_update(small_all, w_small, m_small, v_small, "small_update")

    def unpack_small(row):
        return {"b_ada": row[:, SM_MOD:SM_BIN], "b_in": row[:, SM_BIN:SM_LN1G], "ln1_g": row[:, SM_LN1G:SM_LN1B],
                "ln1_b": row[:, SM_LN1B:SM_LN2G], "ln2_g": row[:, SM_LN2G:SM_LN2B], "ln2_b": row[:, SM_LN2B:SM_GN],
                "gn_sb": row[:, SM_GN:SM_GN + SB_W], "gn_swa": row[:, SM_GN + SB_W:SM_SINK],
                "sinks": row[:, SM_SINK:SM_SINK + 8]}

    g_small, d_small, m2_small, v2_small = [unpack_small(r) for r in small_out]
    loss = small_out[0][0, SM_LOSS]

    dmod_cols = lax.dynamic_slice_in_dim(small_all[:, SM_MOD:SM_BIN], chip * n_ada, n_ada, axis=1)
    gw_ada = _weight_grad(_pad_rows(silu_c, LANES).astype(BF16), _pad_rows(dmod_cols, LANES).astype(BF16), "grad_w_ada")

    big = {}
    for nm, w, g, m, v in (("w_ada", w_ada, gw_ada, m_w_ada, v_w_ada), ("w_in", w_in, gw_in, m_w_in, v_w_in),
                           ("w_out", w_out, gw_out, m_w_out, v_w_out), ("w_gu", w_gu, gw_gu, m_w_gu, v_w_gu),
                           ("w_down", w_down, gw_down, m_w_down, v_w_down)):
        d_, m2_, v2_ = _adamw(w[0], g, m[0], v[0], "adamw_" + nm)
        big[nm] = (g[None], d_[None], m2_[None], v2_[None])

    order = ["w_ada", "b_ada", "w_in", "b_in", "sinks", "gn_sb", "gn_swa", "w_out", "ln1_g", "ln1_b", "w_gu", "w_down",
             "ln2_g", "ln2_b"]

    def leaf(nm, which):
        if nm in big:
            return big[nm][which]
        return (g_small, d_small, m2_small, v2_small)[which][nm]

    outs = [loss, grad_x[None]]
    for which in range(4):
        outs += [leaf(nm, which) for nm in order]
    return tuple(outs)
```

```python
import functools
import math

import jax
import jax.numpy as jnp
from jax import lax
from jax.experimental import pallas as pl
from jax.experimental.pallas import tpu as pltpu

F32 = jnp.float32
BF16 = jnp.bfloat16

D = 1024
HEAD_DIM = 64
SB_W = 512
SWA_QW = 512
SWA_KW = 128
D_IN = 2304
D_FF = 2816
WINDOW = 128
ALPHA = 2.0 ** 0.25
LN_EPS = 1e-5
RMS_EPS = 1e-6
MASK_VALUE = -1e30
QK_SCALE = 1.0 / math.sqrt(HEAD_DIM)

ADAM_LR = 0.001
ADAM_B1 = 0.9
ADAM_B2 = 0.999
ADAM_EPS = 1e-08
ADAM_WD = 0.01
ADAM_STEP = 10

N_CHIPS = 4
N_DEV = 8
LANES = 128

SB_TQ = 256
SB_TK = 256
SB_UNROLL = 2
TOK_TILE = 512
FFN_TILE = 256
FFN_BWD_TILE = 128
VMEM_LIMIT = 56 * 1024 * 1024

V_SH_A, V_SC_A, V_G_A, V_SH_F, V_SC_F, V_G_F, V_LN1G, V_LN1B, V_LN2G, V_LN2B, V_GN = range(11)
VEC_ROWS = 16

SM_MOD = 0
SM_BIN = 6 * D
SM_LN1G = SM_BIN + D_IN
SM_LN1B = SM_LN1G + D
SM_LN2G = SM_LN1B + D
SM_LN2B = SM_LN2G + D
SM_GN = SM_LN2B + D
SM_SINK = SM_GN + D
SM_LOSS = SM_SINK + LANES
SM_LEN = SM_LOSS + LANES

MESH = pl.DeviceIdType.MESH


def _cparams(**kw):
    return pltpu.CompilerParams(vmem_limit_bytes=VMEM_LIMIT, **kw)


def _resident(shape):
    nd = len(shape)
    return pl.BlockSpec(shape, lambda *_: (0,) * nd, pipeline_mode=pl.Buffered(1))


def _dot(a, b):
    return jnp.dot(a, b, preferred_element_type=F32)


def _dot_nt(a, b):
    return lax.dot_general(a, b, (((1,), (1,)), ((), ())), preferred_element_type=F32)


def _dot_tn(a, b):
    return lax.dot_general(a, b, (((0,), (0,)), ((), ())), preferred_element_type=F32)


def _split_dot(x, m):
    hi = x.astype(BF16)
    lo = (x - hi.astype(F32)).astype(BF16)
    return _dot(hi, m) + _dot(lo, m)


def _allgather8(v, name):
    m_per, n = v.shape

    def body(x_ref, out_ref, send_sems, recv_sems, local_sem):
        x, y, c = lax.axis_index("x"), lax.axis_index("y"), lax.axis_index("c")
        me, sibling = (x, y, c), (x, y, 1 - c)
        chips = [(1 - x, y), (x, 1 - y), (1 - x, 1 - y)]

        def rows(px, py, pc):
            return out_ref.at[pl.ds((4 * px + 2 * py + pc) * m_per, m_per), :]

        def copy(k, block, to, src=None):
            return pltpu.make_async_remote_copy(
                src_ref=rows(*block) if src is None else src, dst_ref=rows(*block),
                send_sem=send_sems.at[k], recv_sem=recv_sems.at[k], device_id=to, device_id_type=MESH)

        mine = pltpu.make_async_copy(x_ref, rows(*me), local_sem)
        mine.start()
        first = [copy(0, me, sibling, src=x_ref)]
        first += [copy(1 + j, me, (*chip, c), src=x_ref) for j, chip in enumerate(chips)]
        for cp in first:
            cp.start()
        passed = [copy(4 + j, (*chip, c), sibling) for j, chip in enumerate(chips)]
        for j, chip in enumerate(chips):
            copy(1 + j, (*chip, c), me).wait_recv()
            passed[j].start()
        copy(0, sibling, me).wait_recv()
        for j, chip in enumerate(chips):
            copy(4 + j, (*chip, 1 - c), me).wait_recv()
        for cp in first + passed:
            cp.wait_send()
        mine.wait()

    return pl.pallas_call(
        body, name=name,
        out_shape=jax.ShapeDtypeStruct((N_DEV * m_per, n), v.dtype),
        in_specs=[pl.BlockSpec(memory_space=pltpu.VMEM)],
        out_specs=pl.BlockSpec(memory_space=pltpu.VMEM),
        scratch_shapes=[pltpu.SemaphoreType.DMA((7,)), pltpu.SemaphoreType.DMA((7,)), pltpu.SemaphoreType.DMA],
        compiler_params=_cparams(),
    )(v)


def _chip_allgather(arrs, name):
    n = len(arrs)

    def body(*refs):
        ins, outs = refs[:n], refs[n:2 * n]
        send_sems, recv_sems, local_sems = refs[2 * n:]
        x, y, c = lax.axis_index("x"), lax.axis_index("y"), lax.axis_index("c")
        slot = 2 * x + y
        chips = [(1 - x, y), (x, 1 - y), (1 - x, 1 - y)]
        local, sent = [], []
        for a in range(n):
            cp = pltpu.make_async_copy(ins[a], outs[a].at[slot], local_sems.at[a])
            cp.start()
            local.append(cp)
            for j, (px, py) in enumerate(chips):
                cp = pltpu.make_async_remote_copy(
                    src_ref=ins[a], dst_ref=outs[a].at[slot], send_sem=send_sems.at[3 * a + j],
                    recv_sem=recv_sems.at[3 * a + j], device_id=(px, py, c), device_id_type=MESH)
                cp.start()
                sent.append(cp)
        for a in range(n):
            for j, (px, py) in enumerate(chips):
                pltpu.make_async_remote_copy(
                    src_ref=ins[a], dst_ref=outs[a].at[2 * px + py], send_sem=send_sems.at[3 * a + j],
                    recv_sem=recv_sems.at[3 * a + j], device_id=(px, py, c), device_id_type=MESH).wait_recv()
        for cp in sent:
            cp.wait_send()
        for cp in local:
            cp.wait()

    hbm = pl.BlockSpec(memory_space=pl.ANY)
    return pl.pallas_call(
        body, name=name,
        out_shape=[jax.ShapeDtypeStruct((N_CHIPS,) + a.shape, a.dtype) for a in arrs],
        in_specs=[hbm] * n, out_specs=[hbm] * n,
        scratch_shapes=[pltpu.SemaphoreType.DMA((3 * n,)), pltpu.SemaphoreType.DMA((3 * n,)),
                        pltpu.SemaphoreType.DMA((n,))],
        compiler_params=_cparams(),
    )(*arrs)


def _sibling_send(v, name):
    def body(v_ref, out_ref, send_sem, recv_sem):
        x, y, c = lax.axis_index("x"), lax.axis_index("y"), lax.axis_index("c")
        cp = pltpu.make_async_remote_copy(src_ref=v_ref, dst_ref=out_ref, send_sem=send_sem, recv_sem=recv_sem,
                                          device_id=(x, y, 1 - c), device_id_type=MESH)
        cp.start()
        cp.wait()

    hbm = pl.BlockSpec(memory_space=pl.ANY)
    return pl.pallas_call(
        body, name=name, out_shape=jax.ShapeDtypeStruct(v.shape, v.dtype),
        in_specs=[hbm], out_specs=hbm,
        scratch_shapes=[pltpu.SemaphoreType.DMA, pltpu.SemaphoreType.DMA],
        compiler_params=_cparams(),
    )(v)


def _chip_scatter(p, name):
    def body(p_ref, out_ref, send_sems, recv_sems, local_sem):
        x, y, c = lax.axis_index("x"), lax.axis_index("y"), lax.axis_index("c")
        slot = 2 * x + y
        chips = [(1 - x, y), (x, 1 - y), (1 - x, 1 - y)]
        mine = pltpu.make_async_copy(p_ref.at[slot], out_ref.at[slot], local_sem)
        mine.start()
        sent = []
        for j, (px, py) in enumerate(chips):
            cp = pltpu.make_async_remote_copy(
                src_ref=p_ref.at[2 * px + py], dst_ref=out_ref.at[slot], send_sem=send_sems.at[j],
                recv_sem=recv_sems.at[j], device_id=(px, py, c), device_id_type=MESH)
            cp.start()
            sent.append(cp)
        for j, (px, py) in enumerate(chips):
            pltpu.make_async_remote_copy(
                src_ref=p_ref.at[slot], dst_ref=out_ref.at[2 * px + py], send_sem=send_sems.at[j],
                recv_sem=recv_sems.at[j], device_id=(px, py, c), device_id_type=MESH).wait_recv()
        for cp in sent:
            cp.wait_send()
        mine.wait()

    hbm = pl.BlockSpec(memory_space=pl.ANY)
    return pl.pallas_call(
        body, name=name, out_shape=jax.ShapeDtypeStruct(p.shape, p.dtype),
        in_specs=[hbm], out_specs=hbm,
        scratch_shapes=[pltpu.SemaphoreType.DMA((3,)), pltpu.SemaphoreType.DMA((3,)), pltpu.SemaphoreType.DMA],
        compiler_params=_cparams(),
    )(p)


def _add2(a, b, name):
    rows = a.shape[0]
    tr = rows // 8

    def body(a_ref, b_ref, o_ref):
        o_ref[...] = a_ref[...] + b_ref[...]

    spec = pl.BlockSpec((tr, LANES), lambda i: (i, 0))
    return pl.pallas_call(body, name=name, grid=(rows // tr,), out_shape=jax.ShapeDtypeStruct(a.shape, a.dtype),
                          in_specs=[spec, spec], out_specs=spec, compiler_params=_cparams())(a, b)


def _sum4(p, name):
    rows = p.shape[1]
    tr = rows // 8

    def body(p_ref, o_ref):
        o_ref[...] = ((p_ref[0] + p_ref[1]) + p_ref[2]) + p_ref[3]

    return pl.pallas_call(
        body, name=name, grid=(rows // tr,), out_shape=jax.ShapeDtypeStruct((rows, LANES), p.dtype),
        in_specs=[pl.BlockSpec((4, tr, LANES), lambda i: (0, i, 0))],
        out_specs=pl.BlockSpec((tr, LANES), lambda i: (i, 0)), compiler_params=_cparams())(p)


def _adam_math(w, g, m, v):
    m2 = ADAM_B1 * m + (1.0 - ADAM_B1) * g
    v2 = ADAM_B2 * v + (1.0 - ADAM_B2) * (g * g)
    m_hat = m2 / (1.0 - ADAM_B1 ** ADAM_STEP)
    v_hat = v2 / (1.0 - ADAM_B2 ** ADAM_STEP)
    delta = -ADAM_LR * (m_hat / (jnp.sqrt(v_hat) + ADAM_EPS) + ADAM_WD * w)
    return delta, m2, v2


def _adamw(w, g, m, v, name):
    rows, cols = w.shape
    tr = rows // 4 if rows % 32 == 0 else rows

    def body(w_ref, g_ref, m_ref, v_ref, d_ref, m2_ref, v2_ref):
        delta, m2, v2 = _adam_math(w_ref[...], g_ref[...], m_ref[...], v_ref[...])
        d_ref[...] = delta
        m2_ref[...] = m2
        v2_ref[...] = v2

    spec = pl.BlockSpec((tr, cols), lambda i: (i, 0))
    shp = jax.ShapeDtypeStruct(w.shape, F32)
    return pl.pallas_call(body, name=name, grid=(rows // tr,), out_shape=[shp, shp, shp],
                          in_specs=[spec] * 4, out_specs=[spec] * 3, compiler_params=_cparams())(w, g, m, v)


def _small_update(g8, w, m, v, name):
    n = w.shape[1]

    def body(g8_ref, w_ref, m_ref, v_ref, g_ref, d_ref, m2_ref, v2_ref):
        g = g8_ref[0:1, :]
        for r in range(1, N_DEV):
            g = g + g8_ref[r:r + 1, :]
        delta, m2, v2 = _adam_math(w_ref[...], g, m_ref[...], v_ref[...])
        g_ref[...] = g
        d_ref[...] = delta
        m2_ref[...] = m2
        v2_ref[...] = v2

    shp = jax.ShapeDtypeStruct((1, n), F32)
    vm = pl.BlockSpec(memory_space=pltpu.VMEM)
    return pl.pallas_call(body, name=name, out_shape=[shp] * 4, in_specs=[vm] * 4, out_specs=[vm] * 4,
                          compiler_params=_cparams())(g8, w, m, v)


def _mod_shard(c8, w_ada, b_ada_shard, name):
    n = w_ada.shape[1]
    tn = 512

    def body(c_ref, w_ref, b_ref, o_ref, s_ref):
        cv = c_ref[...]
        sc = cv * (1.0 / (1.0 + jnp.exp(-cv)))
        s_ref[...] = sc
        o_ref[...] = _dot(sc.astype(BF16), w_ref[...].astype(BF16)) + b_ref[...]

    return pl.pallas_call(
        body, name=name, grid=(n // tn,),
        out_shape=[jax.ShapeDtypeStruct((8, n), F32), jax.ShapeDtypeStruct((8, D), F32)],
        in_specs=[pl.BlockSpec((8, D), lambda j: (0, 0)), pl.BlockSpec((D, tn), lambda j: (0, j)),
                  pl.BlockSpec((1, tn), lambda j: (0, j))],
        out_specs=[pl.BlockSpec((8, tn), lambda j: (0, j)), pl.BlockSpec((8, D), lambda j: (0, 0))],
        compiler_params=_cparams())(c8, w_ada, b_ada_shard)


def _layer_norm_stats(u):
    mu = jnp.mean(u, axis=1, keepdims=True)
    d = u - mu
    var = jnp.mean(d * d, axis=1, keepdims=True)
    rstd = lax.rsqrt(var + LN_EPS)
    return d * rstd, rstd


def _in_proj(x, vec, w_in, b_in, name):
    s = x.shape[0]
    tb = min(TOK_TILE, s)

    def body(x_ref, vec_ref, w_ref, b_ref, h_ref, p_ref):
        h = x_ref[...] * (1.0 + vec_ref[V_SC_A:V_SC_A + 1, :]) + vec_ref[V_SH_A:V_SH_A + 1, :]
        hb = h.astype(BF16)
        h_ref[...] = hb
        proj = _dot(hb, w_ref[...]) + b_ref[...]
        col = lax.broadcasted_iota(jnp.int32, (1, D_IN), 1)
        is_q = (col < SB_W) | ((col >= 3 * SB_W) & (col < 3 * SB_W + SWA_QW))
        p_ref[...] = (proj * jnp.where(is_q, QK_SCALE, 1.0)).astype(BF16)

    return pl.pallas_call(
        body, name=name, grid=(s // tb,),
        out_shape=[jax.ShapeDtypeStruct((s, D), BF16), jax.ShapeDtypeStruct((s, D_IN), BF16)],
        in_specs=[pl.BlockSpec((tb, D), lambda i: (i, 0)), _resident((VEC_ROWS, D)), _resident((D, D_IN)),
                  _resident((1, D_IN))],
        out_specs=[pl.BlockSpec((tb, D), lambda i: (i, 0)), pl.BlockSpec((tb, D_IN), lambda i: (i, 0))],
        compiler_params=_cparams())(x, vec, w_in, b_in)


def _softplus_parts(z):
    e1 = jnp.exp(-jnp.abs(z))
    sp = jnp.maximum(z, 0.0) + jnp.log(1.0 + e1)
    return sp, e1


def _sb_forward(proj, name):
    s = proj.shape[0]
    tq, tk = min(SB_TQ, s), min(SB_TK, s)
    r = tq // tk

    def body(q_ref, k_ref, v_ref, o_ref, tot_ref, acc_refs, run_refs):
        i = pl.program_id(1)
        lane = lax.broadcasted_iota(jnp.int32, (1, LANES), 1)
        first = lane < HEAD_DIM
        qp = q_ref[...]
        zero = jnp.zeros((), BF16)
        qs = (jnp.where(first, qp, zero), jnp.where(first, zero, qp))
        row = lax.broadcasted_iota(jnp.int32, (tk, tk), 0)
        col = lax.broadcasted_iota(jnp.int32, (tk, tk), 1)
        later = (row > col).astype(BF16)
        acc_refs[...] = jnp.zeros_like(acc_refs)
        run_refs[...] = jnp.zeros_like(run_refs)

        def blocks(js, masked):
            ks = [pl.multiple_of(j * tk, tk) for j in js]
            kjs = [k_ref[pl.ds(k0, tk), :] for k0 in ks]
            vjs = [v_ref[pl.ds(k0, tk), :] for k0 in ks]
            chains = [(hd, b) for b in range(len(js)) for hd in range(2)]
            zs = [_dot_nt(qs[hd], kjs[b]) for hd, b in chains]
            sps = [_softplus_parts(z)[0] for z in zs]
            if masked:
                t_idx = i * tq + lax.broadcasted_iota(jnp.int32, (tq, tk), 0)
                befores = [j * tk + lax.broadcasted_iota(jnp.int32, (tq, tk), 1) < t_idx for j in js]
                spms = [jnp.where(befores[b], sp, 0.0) for (hd, b), sp in zip(chains, sps)]
            else:
                spms = sps
            cums = [_split_dot(spm, later) for spm in spms]
            sums = [jnp.sum(spm, axis=1, keepdims=True) for spm in spms]
            runs = [run_refs[0], run_refs[1]]
            ws = []
            for (hd, b), z, sp, cum, sm in zip(chains, zs, sps, cums, sums):
                w = jnp.exp(z - sp - cum - runs[hd])
                if masked:
                    w = jnp.where(befores[b], w, 0.0)
                ws.append(w.astype(BF16))
                runs[hd] = runs[hd] + sm
            pvs = [_dot(w, vjs[b]) for (hd, b), w in zip(chains, ws)]
            for hd in range(2):
                tot = pvs[hd]
                for b in range(1, len(js)):
                    tot = tot + pvs[2 * b + hd]
                acc_refs[hd] += tot
                run_refs[hd] = runs[hd]

        for d in range(r):
            blocks([i * r + (r - 1 - d)], True)

        below = i * r

        def sweep(n, carry):
            top = below - 1 - SB_UNROLL * n
            blocks([top - u for u in range(SB_UNROLL)], False)
            return carry

        lax.fori_loop(0, below // SB_UNROLL, sweep, 0)
        for u in range(1, SB_UNROLL):
            @pl.when(below % SB_UNROLL >= u)
            def _(u=u):
                blocks([below % SB_UNROLL - u], False)
        o_ref[...] = jnp.where(first, acc_refs[0], acc_refs[1])
        tot_ref[...] = jnp.where(first, run_refs[0], run_refs[1])

    nkb = SB_W // LANES
    shp = jax.ShapeDtypeStruct((s, SB_W), F32)
    qspec = pl.BlockSpec((tq, LANES), lambda p, i: (i, p))
    return pl.pallas_call(
        body, name=name, grid=(nkb, s // tq),
        out_shape=[shp, shp],
        in_specs=[qspec,
                  pl.BlockSpec((s, LANES), lambda p, i: (0, nkb + p)),
                  pl.BlockSpec((s, LANES), lambda p, i: (0, 2 * nkb + p))],
        out_specs=[qspec, qspec],
        scratch_shapes=[pltpu.VMEM((2, tq, LANES), F32), pltpu.VMEM((2, tq, 1), F32)],
        compiler_params=_cparams())(proj, proj, proj)


def _swa_masks(n):
    ti = lax.broadcasted_iota(jnp.int32, (WINDOW, 2 * WINDOW), 0)
    kj = lax.broadcasted_iota(jnp.int32, (WINDOW, 2 * WINDOW), 1)
    dist = ti + WINDOW - kj
    valid = (dist >= 0) & (dist < WINDOW) & ((n * WINDOW - WINDOW + kj) >= 0)
    return valid, dist.astype(F32)


def _swa_probs(qm, ku, valid, distf, h, sink):
    slope = 2.0 ** (-(h + 1))
    sc = _dot_nt(qm, ku)
    sc = jnp.where(valid, sc - slope * distf, MASK_VALUE)
    mx = jnp.maximum(jnp.max(sc, axis=1, keepdims=True), sink)
    p = jnp.exp(sc - mx)
    es = jnp.exp(sink - mx)
    inv = 1.0 / (jnp.sum(p, axis=1, keepdims=True) + es)
    return p * inv, es * inv


def _swa_forward(proj, sinks, name):
    s = proj.shape[0]
    nb = s // WINDOW
    qb, kb, vb = 3 * SB_W // SWA_QW, (3 * SB_W + SWA_QW) // LANES, (3 * SB_W + SWA_QW + SWA_KW) // LANES

    def body(q_ref, kp_ref, kc_ref, vp_ref, vc_ref, sink_ref, o_ref):
        n = pl.program_id(0)
        k = jnp.concatenate([kp_ref[...], kc_ref[...]], axis=0)
        v = jnp.concatenate([vp_ref[...], vc_ref[...]], axis=0)
        k_sw = pltpu.roll(k.astype(F32), HEAD_DIM, 1).astype(BF16)
        v_sw = pltpu.roll(v.astype(F32), HEAD_DIM, 1).astype(BF16)
        lane = lax.broadcasted_iota(jnp.int32, (1, LANES), 1)
        halves = [lane < HEAD_DIM, lane >= HEAD_DIM]
        valid, distf = _swa_masks(n)
        for pair in range(4):
            qp = q_ref[:, pair * LANES:(pair + 1) * LANES]
            out = jnp.zeros((WINDOW, LANES), F32)
            for par in range(2):
                h = 2 * pair + par
                g = h // 4
                qm = jnp.where(halves[par], qp, jnp.zeros((), BF16))
                ku, vu = (k, v) if g == par else (k_sw, v_sw)
                p, _ = _swa_probs(qm, ku, valid, distf, h, sink_ref[h])
                out = out + jnp.where(halves[par], _dot(p.astype(BF16), vu), 0.0)
            o_ref[:, pair * LANES:(pair + 1) * LANES] = out

    prev = lambda n: jnp.maximum(n - 1, 0)
    return pl.pallas_call(
        body, name=name, grid=(nb,),
        out_shape=jax.ShapeDtypeStruct((s, SWA_QW), F32),
        in_specs=[pl.BlockSpec((WINDOW, SWA_QW), lambda n: (n, qb)),
                  pl.BlockSpec((WINDOW, LANES), lambda n: (prev(n), kb)),
                  pl.BlockSpec((WINDOW, LANES), lambda n: (n, kb)),
                  pl.BlockSpec((WINDOW, LANES), lambda n: (prev(n), vb)),
                  pl.BlockSpec((WINDOW, LANES), lambda n: (n, vb)),
                  pl.BlockSpec(memory_space=pltpu.SMEM)],
        out_specs=pl.BlockSpec((WINDOW, SWA_QW), lambda n: (n, 0)),
        compiler_params=_cparams())(proj, proj, proj, proj, proj, sinks)


def _rms_parts(y):
    return lax.rsqrt(jnp.mean(y * y, axis=1, keepdims=True) + RMS_EPS)


def _post_attention(y_sb, y_sw, x, vec, w_out, name):
    s = x.shape[0]
    tb = min(TOK_TILE, s)

    def body(ysb_ref, ysw_ref, x_ref, vec_ref, w_ref, mixed_ref, attn_ref, x1_ref, h2_ref):
        ysb, ysw = ysb_ref[...], ysw_ref[...]
        nsb = (ysb * _rms_parts(ysb) * vec_ref[V_GN:V_GN + 1, :SB_W]).astype(BF16)
        nsw = (ysw * _rms_parts(ysw) * vec_ref[V_GN:V_GN + 1, SB_W:]).astype(BF16)
        mixed_ref[:, :SB_W] = nsb
        mixed_ref[:, SB_W:] = nsw
        attn = _dot(nsb, w_ref[:SB_W, :]) + _dot(nsw, w_ref[SB_W:, :])
        attn_ref[...] = attn
        u1 = ALPHA * x_ref[...] + (1.0 + vec_ref[V_G_A:V_G_A + 1, :]) * attn
        xhat, _ = _layer_norm_stats(u1)
        x1 = xhat * vec_ref[V_LN1G:V_LN1G + 1, :] + vec_ref[V_LN1B:V_LN1B + 1, :]
        x1_ref[...] = x1
        h2_ref[...] = (x1 * (1.0 + vec_ref[V_SC_F:V_SC_F + 1, :]) + vec_ref[V_SH_F:V_SH_F + 1, :]).astype(BF16)

    half = pl.BlockSpec((tb, SB_W), lambda i: (i, 0))
    full = pl.BlockSpec((tb, D), lambda i: (i, 0))
    return pl.pallas_call(
        body, name=name, grid=(s // tb,),
        out_shape=[jax.ShapeDtypeStruct((s, D), BF16), jax.ShapeDtypeStruct((s, D), F32),
                   jax.ShapeDtypeStruct((s, D), F32), jax.ShapeDtypeStruct((s, D), BF16)],
        in_specs=[half, half, full, _resident((VEC_ROWS, D)), _resident((D, D))],
        out_specs=[full, full, full, full],
        compiler_params=_cparams())(y_sb, y_sw, x, vec, w_out)


def _ffn_forward(h2, w_gu, w_down, name):
    s = h2.shape[0]
    tb = min(FFN_TILE, s)

    def body(h_ref, wgu_ref, wd_ref, gu_ref, act_ref, ffn_ref):
        gu = _dot(h_ref[...], wgu_ref[...])
        gu_ref[...] = gu
        gate, up = gu[:, :D_FF], gu[:, D_FF:]
        act = (gate * (1.0 / (1.0 + jnp.exp(-gate))) * up).astype(BF16)
        act_ref[...] = act
        ffn_ref[...] = _dot(act, wd_ref[...])

    return pl.pallas_call(
        body, name=name, grid=(s // tb,),
        out_shape=[jax.ShapeDtypeStruct((s, 2 * D_FF), F32), jax.ShapeDtypeStruct((s, D_FF), BF16),
                   jax.ShapeDtypeStruct((s, D), F32)],
        in_specs=[pl.BlockSpec((tb, D), lambda i: (i, 0)), _resident((D, 2 * D_FF)), _resident((D_FF, D))],
        out_specs=[pl.BlockSpec((tb, 2 * D_FF), lambda i: (i, 0)), pl.BlockSpec((tb, D_FF), lambda i: (i, 0)),
                   pl.BlockSpec((tb, D), lambda i: (i, 0))],
        compiler_params=_cparams())(h2, w_gu, w_down)


def _layer_norm_bwd(dxhat, xhat, rstd):
    m1 = jnp.mean(dxhat, axis=1, keepdims=True)
    m2 = jnp.mean(dxhat * xhat, axis=1, keepdims=True)
    return rstd * (dxhat - m1 - xhat * m2)


def _colsum(a):
    return jnp.sum(a, axis=0, keepdims=True)


A_LN2G, A_LN2B, A_GF, A_SCF, A_SHF, A_LOSS = range(6)
B_LN1G, B_LN1B, B_GA, B_GN = range(4)
C_SCA, C_SHA = range(2)


def _ffn_backward(x1, ffn, target, gu, vec, w_gu, w_down, name):
    s = x1.shape[0]
    tb = min(FFN_BWD_TILE, s)

    def body(x1_ref, ffn_ref, t_ref, gu_ref, vec_ref, wgu_ref, wd_ref, dffn_ref, dgu_ref, dx1_ref, acc_ref):
        @pl.when(pl.program_id(0) == 0)
        def _():
            acc_ref[...] = jnp.zeros_like(acc_ref)

        x1v, ffn_v = x1_ref[...], ffn_ref[...]
        g_f = 1.0 + vec_ref[V_G_F:V_G_F + 1, :]
        u2 = ALPHA * x1v + g_f * ffn_v
        xhat, rstd = _layer_norm_stats(u2)
        ln_g = vec_ref[V_LN2G:V_LN2G + 1, :]
        err = xhat * ln_g + vec_ref[V_LN2B:V_LN2B + 1, :] - t_ref[...]
        dx2 = err * (1.0 / D)
        acc_ref[A_LOSS:A_LOSS + 1, :] += _colsum(err * err) * (0.5 / D)
        acc_ref[A_LN2G:A_LN2G + 1, :] += _colsum(dx2 * xhat)
        acc_ref[A_LN2B:A_LN2B + 1, :] += _colsum(dx2)
        du2 = _layer_norm_bwd(dx2 * ln_g, xhat, rstd)
        acc_ref[A_GF:A_GF + 1, :] += _colsum(du2 * ffn_v)
        dffn = (g_f * du2).astype(BF16)
        dffn_ref[...] = dffn
        dact = _dot_nt(dffn, wd_ref[...])
        gate, up = gu_ref[:, :D_FF], gu_ref[:, D_FF:]
        sg = 1.0 / (1.0 + jnp.exp(-gate))
        dgate = (dact * up * (sg * (1.0 + gate * (1.0 - sg)))).astype(BF16)
        dup = (dact * (gate * sg)).astype(BF16)
        dgu_ref[:, :D_FF] = dgate
        dgu_ref[:, D_FF:] = dup
        dh2 = _dot_nt(dgate, wgu_ref[:, :D_FF]) + _dot_nt(dup, wgu_ref[:, D_FF:])
        dx1_ref[...] = ALPHA * du2 + dh2 * (1.0 + vec_ref[V_SC_F:V_SC_F + 1, :])
        acc_ref[A_SCF:A_SCF + 1, :] += _colsum(dh2 * x1v)
        acc_ref[A_SHF:A_SHF + 1, :] += _colsum(dh2)

    full = pl.BlockSpec((tb, D), lambda i: (i, 0))
    wide = pl.BlockSpec((tb, 2 * D_FF), lambda i: (i, 0))
    return pl.pallas_call(
        body, name=name, grid=(s // tb,),
        out_shape=[jax.ShapeDtypeStruct((s, D), BF16), jax.ShapeDtypeStruct((s, 2 * D_FF), BF16),
                   jax.ShapeDtypeStruct((s, D), F32), jax.ShapeDtypeStruct((8, D), F32)],
        in_specs=[full, full, full, wide, _resident((VEC_ROWS, D)), _resident((D, 2 * D_FF)), _resident((D_FF, D))],
        out_specs=[full, wide, full, pl.BlockSpec((8, D), lambda i: (0, 0))],
        compiler_params=_cparams())(x1, ffn, target, gu, vec, w_gu, w_down)


def _attn_out_backward(dx1, x, attn, y_sb, y_sw, vec, w_out, name):
    s = x.shape[0]
    tb = min(TOK_TILE, s)

    def body(dx1_ref, x_ref, attn_ref, ysb_ref, ysw_ref, vec_ref, w_ref, du1_ref, dattn_ref, dy_ref, acc_ref):
        @pl.when(pl.program_id(0) == 0)
        def _():
            acc_ref[...] = jnp.zeros_like(acc_ref)

        attn = attn_ref[...]
        g_a = 1.0 + vec_ref[V_G_A:V_G_A + 1, :]
        xhat, rstd = _layer_norm_stats(ALPHA * x_ref[...] + g_a * attn)
        dx1v = dx1_ref[...]
        acc_ref[B_LN1G:B_LN1G + 1, :] += _colsum(dx1v * xhat)
        acc_ref[B_LN1B:B_LN1B + 1, :] += _colsum(dx1v)
        du1 = _layer_norm_bwd(dx1v * vec_ref[V_LN1G:V_LN1G + 1, :], xhat, rstd)
        du1_ref[...] = du1
        acc_ref[B_GA:B_GA + 1, :] += _colsum(du1 * attn)
        dattn = (g_a * du1).astype(BF16)
        dattn_ref[...] = dattn
        dmixed = _dot_nt(dattn, w_ref[...])
        for lo, y_ref in ((0, ysb_ref), (SB_W, ysw_ref)):
            y = y_ref[...]
            rr = _rms_parts(y)
            dn = dmixed[:, lo:lo + SB_W]
            acc_ref[B_GN:B_GN + 1, lo:lo + SB_W] += _colsum(dn * y * rr)
            dng = dn * vec_ref[V_GN:V_GN + 1, lo:lo + SB_W]
            dy_ref[:, lo:lo + SB_W] = rr * dng - y * (rr * rr * rr) * jnp.mean(dng * y, axis=1, keepdims=True)

    half = pl.BlockSpec((tb, SB_W), lambda i: (i, 0))
    full = pl.BlockSpec((tb, D), lambda i: (i, 0))
    return pl.pallas_call(
        body, name=name, grid=(s // tb,),
        out_shape=[jax.ShapeDtypeStruct((s, D), F32), jax.ShapeDtypeStruct((s, D), BF16),
                   jax.ShapeDtypeStruct((s, D), F32), jax.ShapeDtypeStruct((8, D), F32)],
        in_specs=[full, full, full, half, half, _resident((VEC_ROWS, D)), _resident((D, D))],
        out_specs=[full, full, full, pl.BlockSpec((8, D), lambda i: (0, 0))],
        compiler_params=_cparams())(dx1, x, attn, y_sb, y_sw, vec, w_out)


def _sb_backward(proj, sp_total, dy, name):
    s = proj.shape[0]
    tq, tk = min(SB_TQ, s), min(SB_TK, s)
    r = tq // tk

    def body(q_ref, k_ref, v_ref, tot_ref, do_ref, dq_ref, dk_ref, dv_ref, dq_acc, left_refs, gsum_refs):
        i = pl.program_id(1)

        @pl.when(i == 0)
        def _():
            dk_ref[...] = jnp.zeros_like(dk_ref)
            dv_ref[...] = jnp.zeros_like(dv_ref)

        lane = lax.broadcasted_iota(jnp.int32, (1, LANES), 1)
        first = lane < HEAD_DIM
        qp, dop, totp = q_ref[...], do_ref[...], tot_ref[...]
        zero = jnp.zeros((), BF16)
        qs = (jnp.where(first, qp, zero), jnp.where(first, zero, qp))
        dobs = (jnp.where(first, dop, 0.0).astype(BF16), jnp.where(first, 0.0, dop).astype(BF16))
        row = lax.broadcasted_iota(jnp.int32, (tk, tk), 0)
        col = lax.broadcasted_iota(jnp.int32, (tk, tk), 1)
        later = (row > col).astype(BF16)
        earlier = (row < col).astype(BF16)
        dq_acc[...] = jnp.zeros_like(dq_acc)
        gsum_refs[...] = jnp.zeros_like(gsum_refs)
        left_refs[0] = jnp.max(jnp.where(first, totp, 0.0), axis=1, keepdims=True)
        left_refs[1] = jnp.max(jnp.where(first, 0.0, totp), axis=1, keepdims=True)

        def blocks(js, masked):
            ks = [pl.multiple_of(j * tk, tk) for j in js]
            kjs = [k_ref[pl.ds(k0, tk), :] for k0 in ks]
            vjs = [v_ref[pl.ds(k0, tk), :] for k0 in ks]
            chains = [(hd, b) for b in range(len(js)) for hd in range(2)]
            zs = [_dot_nt(qs[hd], kjs[b]) for hd, b in chains]
            dws = [_dot_nt(dobs[hd], vjs[b]) for hd, b in chains]
            parts = [_softplus_parts(z) for z in zs]
            sps = [p[0] for p in parts]
            if masked:
                t_idx = i * tq + lax.broadcasted_iota(jnp.int32, (tq, tk), 0)
                befores = [j * tk + lax.broadcasted_iota(jnp.int32, (tq, tk), 1) < t_idx for j in js]
                spms = [jnp.where(befores[b], sp, 0.0) for (hd, b), sp in zip(chains, sps)]
            else:
                spms = sps
            cums = [_split_dot(spm, later) for spm in spms]
            sums = [jnp.sum(spm, axis=1, keepdims=True) for spm in spms]
            lefts = [left_refs[0], left_refs[1]]
            ws = []
            for (hd, b), z, sp, cum, sm in zip(chains, zs, sps, cums, sums):
                lefts[hd] = lefts[hd] - sm
                w = jnp.exp(z - sp - cum - lefts[hd])
                if masked:
                    w = jnp.where(befores[b], w, 0.0)
                ws.append(w)
            wbs = [w.astype(BF16) for w in ws]
            dvs = [_dot_tn(wb, dobs[hd]) for (hd, b), wb in zip(chains, wbs)]
            gs = [dw * w for dw, w in zip(dws, ws)]
            gcums = [_split_dot(g, earlier) for g in gs]
            gsums = [gsum_refs[0], gsum_refs[1]]
            dzbs = []
            for (hd, b), z, (sp, e1), g, gcum in zip(chains, zs, parts, gs, gcums):
                inv = 1.0 / (1.0 + e1)
                sig = jnp.where(z >= 0.0, inv, e1 * inv)
                dz = g - sig * (g + gsums[hd] + gcum)
                if masked:
                    dz = jnp.where(befores[b], dz, 0.0)
                dzbs.append(dz.astype(BF16))
                gsums[hd] = gsums[hd] + jnp.sum(g, axis=1, keepdims=True)
            dqs = [_dot(dzb, kjs[b]) for (hd, b), dzb in zip(chains, dzbs)]
            dks = [_dot_tn(dzb, qs[hd]) for (hd, b), dzb in zip(chains, dzbs)]
            for b, k0 in enumerate(ks):
                dv_ref[pl.ds(k0, tk), :] += dvs[2 * b] + dvs[2 * b + 1]
                dk_ref[pl.ds(k0, tk), :] += dks[2 * b] + dks[2 * b + 1]
            for hd in range(2):
                tot = dqs[hd]
                for b in range(1, len(js)):
                    tot = tot + dqs[2 * b + hd]
                dq_acc[hd] += tot
                left_refs[hd] = lefts[hd]
                gsum_refs[hd] = gsums[hd]

        below = i * r

        def sweep(n, carry):
            blocks([SB_UNROLL * n + u for u in range(SB_UNROLL)], False)
            return carry

        lax.fori_loop(0, below // SB_UNROLL, sweep, 0)
        for u in range(SB_UNROLL - 1, 0, -1):
            @pl.when(below % SB_UNROLL >= u)
            def _(u=u):
                blocks([below - u], False)
        for d in range(r):
            blocks([below + d], True)
        dq_ref[...] = jnp.where(first, dq_acc[0], dq_acc[1])

    nkb = SB_W // LANES
    shp = jax.ShapeDtypeStruct((s, SB_W), F32)
    qspec = pl.BlockSpec((tq, LANES), lambda p, i: (i, p))
    whole = pl.BlockSpec((s, LANES), lambda p, i: (0, p))
    return pl.pallas_call(
        body, name=name, grid=(nkb, s // tq),
        out_shape=[shp, shp, shp],
        in_specs=[qspec,
                  pl.BlockSpec((s, LANES), lambda p, i: (0, nkb + p)),
                  pl.BlockSpec((s, LANES), lambda p, i: (0, 2 * nkb + p)),
                  qspec, qspec],
        out_specs=[qspec, whole, whole],
        scratch_shapes=[pltpu.VMEM((2, tq, LANES), F32), pltpu.VMEM((2, tq, 1), F32), pltpu.VMEM((2, tq, 1), F32)],
        compiler_params=_cparams())(proj, proj, proj, sp_total, dy)


def _swa_backward(proj, y_sw, dy, sinks, name):
    s = proj.shape[0]
    nb = s // WINDOW
    qb, kb, vb = 3 * SB_W // SWA_QW, (3 * SB_W + SWA_QW) // LANES, (3 * SB_W + SWA_QW + SWA_KW) // LANES

    def body(q_ref, kp_ref, kc_ref, vp_ref, vc_ref, o_ref, do_ref, sink_ref, dq_ref, dk_ref, dv_ref, ds_ref):
        n = pl.program_id(0)

        @pl.when(n == 0)
        def _():
            dk_ref[...] = jnp.zeros_like(dk_ref)
            dv_ref[...] = jnp.zeros_like(dv_ref)
            ds_ref[...] = jnp.zeros_like(ds_ref)

        k = jnp.concatenate([kp_ref[...], kc_ref[...]], axis=0)
        v = jnp.concatenate([vp_ref[...], vc_ref[...]], axis=0)
        k_sw = pltpu.roll(k.astype(F32), HEAD_DIM, 1).astype(BF16)
        v_sw = pltpu.roll(v.astype(F32), HEAD_DIM, 1).astype(BF16)
        lane = lax.broadcasted_iota(jnp.int32, (1, LANES), 1)
        halves = [lane < HEAD_DIM, lane >= HEAD_DIM]
        valid, distf = _swa_masks(n)
        zero = jnp.zeros((2 * WINDOW, LANES), F32)
        dk_nat, dk_rot, dv_nat, dv_rot = zero, zero, zero, zero
        for pair in range(4):
            cols = slice(pair * LANES, (pair + 1) * LANES)
            qp = q_ref[:, cols]
            dop, op = do_ref[:, cols], o_ref[:, cols]
            dq = jnp.zeros((WINDOW, LANES), F32)
            for par in range(2):
                h = 2 * pair + par
                g = h // 4
                qm = jnp.where(halves[par], qp, jnp.zeros((), BF16))
                do = jnp.where(halves[par], dop, 0.0)
                dob = do.astype(BF16)
                delta = jnp.sum(do * op, axis=1, keepdims=True)
                ku, vu = (k, v) if g == par else (k_sw, v_sw)
                p, p_sink = _swa_probs(qm, ku, valid, distf, h, sink_ref[h])
                dsc = (p * (_dot_nt(dob, vu) - delta)).astype(BF16)
                ds_ref[h:h + 1, :] += jnp.zeros((1, LANES), F32) - jnp.sum(p_sink * delta)
                dq = dq + jnp.where(halves[par], _dot(dsc, ku), 0.0)
                dk_h = _dot_tn(dsc, qm)
                dv_h = _dot_tn(p.astype(BF16), dob)
                if g == par:
                    dk_nat, dv_nat = dk_nat + dk_h, dv_nat + dv_h
                else:
                    dk_rot, dv_rot = dk_rot + dk_h, dv_rot + dv_h
            dq_ref[:, cols] = dq
        dk = dk_nat + pltpu.roll(dk_rot, HEAD_DIM, 1)
        dv = dv_nat + pltpu.roll(dv_rot, HEAD_DIM, 1)
        prev = pl.multiple_of(jnp.maximum(n - 1, 0) * WINDOW, WINDOW)
        cur = pl.multiple_of(n * WINDOW, WINDOW)
        dk_ref[pl.ds(prev, WINDOW), :] += dk[:WINDOW]
        dv_ref[pl.ds(prev, WINDOW), :] += dv[:WINDOW]
        dk_ref[pl.ds(cur, WINDOW), :] += dk[WINDOW:]
        dv_ref[pl.ds(cur, WINDOW), :] += dv[WINDOW:]

    prev_blk = lambda n: jnp.maximum(n - 1, 0)
    wide = pl.BlockSpec((WINDOW, SWA_QW), lambda n: (n, 0))
    whole = pl.BlockSpec((s, LANES), lambda n: (0, 0))
    return pl.pallas_call(
        body, name=name, grid=(nb,),
        out_shape=[jax.ShapeDtypeStruct((s, SWA_QW), F32), jax.ShapeDtypeStruct((s, LANES), F32),
                   jax.ShapeDtypeStruct((s, LANES), F32), jax.ShapeDtypeStruct((8, LANES), F32)],
        in_specs=[pl.BlockSpec((WINDOW, SWA_QW), lambda n: (n, qb)),
                  pl.BlockSpec((WINDOW, LANES), lambda n: (prev_blk(n), kb)),
                  pl.BlockSpec((WINDOW, LANES), lambda n: (n, kb)),
                  pl.BlockSpec((WINDOW, LANES), lambda n: (prev_blk(n), vb)),
                  pl.BlockSpec((WINDOW, LANES), lambda n: (n, vb)),
                  wide,
                  pl.BlockSpec((WINDOW, SWA_QW), lambda n: (n, 1)),
                  pl.BlockSpec(memory_space=pltpu.SMEM)],
        out_specs=[wide, whole, whole, pl.BlockSpec((8, LANES), lambda n: (0, 0))],
        compiler_params=_cparams())(proj, proj, proj, proj, proj, y_sw, dy, sinks)


def _in_proj_backward(dq_sb, dk_sb, dv_sb, dq_sw, dk_sw, dv_sw, du1, x, vec, w_in, name):
    s = x.shape[0]
    tb = min(TOK_TILE, s)

    def body(dqsb_ref, dksb_ref, dvsb_ref, dqsw_ref, dksw_ref, dvsw_ref, du1_ref, x_ref, vec_ref, w_ref,
             dproj_ref, gx_ref, acc_ref, bacc_ref):
        @pl.when(pl.program_id(0) == 0)
        def _():
            acc_ref[...] = jnp.zeros_like(acc_ref)
            bacc_ref[...] = jnp.zeros_like(bacc_ref)

        pieces = ((0, dqsb_ref, QK_SCALE), (SB_W, dksb_ref, 1.0), (2 * SB_W, dvsb_ref, 1.0),
                  (3 * SB_W, dqsw_ref, QK_SCALE), (3 * SB_W + SWA_QW, dksw_ref, 1.0),
                  (3 * SB_W + SWA_QW + SWA_KW, dvsw_ref, 1.0))
        for lo, ref, scale in pieces:
            width = ref.shape[1]
            piece = ref[...] * scale
            bacc_ref[0:1, lo:lo + width] += _colsum(piece)
            dproj_ref[:, lo:lo + width] = piece.astype(BF16)
        dh = _dot_nt(dproj_ref[...], w_ref[...])
        xv = x_ref[...]
        gx_ref[...] = ALPHA * du1_ref[...] + dh * (1.0 + vec_ref[V_SC_A:V_SC_A + 1, :])
        acc_ref[C_SCA:C_SCA + 1, :] += _colsum(dh * xv)
        acc_ref[C_SHA:C_SHA + 1, :] += _colsum(dh)

    half = pl.BlockSpec((tb, SB_W), lambda i: (i, 0))
    narrow = pl.BlockSpec((tb, LANES), lambda i: (i, 0))
    full = pl.BlockSpec((tb, D), lambda i: (i, 0))
    return pl.pallas_call(
        body, name=name, grid=(s // tb,),
        out_shape=[jax.ShapeDtypeStruct((s, D_IN), BF16), jax.ShapeDtypeStruct((s, D), F32),
                   jax.ShapeDtypeStruct((8, D), F32), jax.ShapeDtypeStruct((8, D_IN), F32)],
        in_specs=[half, half, half, half, narrow, narrow, full, full, _resident((VEC_ROWS, D)), _resident((D, D_IN))],
        out_specs=[pl.BlockSpec((tb, D_IN), lambda i: (i, 0)), full, pl.BlockSpec((8, D), lambda i: (0, 0)),
                   pl.BlockSpec((8, D_IN), lambda i: (0, 0))],
        compiler_params=_cparams())(dq_sb, dk_sb, dv_sb, dq_sw, dk_sw, dv_sw, du1, x, vec, w_in)


def _weight_grad(a, b, name):
    s, m = a.shape
    n = b.shape[1]
    tn = 512 if n % 512 == 0 else n
    ts = min(512, s)

    def body(a_ref, b_ref, o_ref):
        @pl.when(pl.program_id(1) == 0)
        def _():
            o_ref[...] = jnp.zeros_like(o_ref)

        o_ref[...] += _dot_tn(a_ref[...], b_ref[...])

    return pl.pallas_call(
        body, name=name, grid=(n // tn, s // ts),
        out_shape=jax.ShapeDtypeStruct((m, n), F32),
        in_specs=[pl.BlockSpec((ts, m), lambda j, k: (k, 0)), pl.BlockSpec((ts, tn), lambda j, k: (k, j))],
        out_specs=pl.BlockSpec((m, tn), lambda j, k: (0, j)),
        compiler_params=_cparams())(a, b)


def _pad_rows(v, rows):
    return jnp.concatenate([v, jnp.zeros((rows - v.shape[0], v.shape[1]), v.dtype)], axis=0)


def _col_shards(w, n_shards):
    r, n = w.shape
    return w.reshape(r, n_shards, n // n_shards).transpose(1, 0, 2)


def kernel(x, c, w_ada, b_ada, w_in, b_in, sinks, gn_sb, gn_swa, w_out, ln1_g, ln1_b, w_gu, w_down, ln2_g, ln2_b, loss_target, m_w_ada, m_b_ada, m_w_in, m_b_in, m_sinks, m_gn_sb, m_gn_swa, m_w_out, m_ln1_g, m_ln1_b, m_w_gu, m_w_down, m_ln2_g, m_ln2_b, v_w_ada, v_b_ada, v_w_in, v_b_in, v_sinks, v_gn_sb, v_gn_swa, v_w_out, v_ln1_g, v_ln1_b, v_w_gu, v_w_down, v_ln2_g, v_ln2_b):
    ix, iy, ic = lax.axis_index("x"), lax.axis_index("y"), lax.axis_index("c")
    chip = 2 * ix + iy
    dev = 4 * ix + 2 * iy + ic
    xs, target = x[0], loss_target[0]
    s = xs.shape[0]

    c_all = _allgather8(_pad_rows(c, 8), "gather_c")[::8]
    n_ada = w_ada.shape[2]
    b_ada_shard = lax.dynamic_slice_in_dim(b_ada, chip * n_ada, n_ada, axis=1)
    mod_cols, silu_c = _mod_shard(c_all, w_ada[0], b_ada_shard, "mod_shard")
    mod_all = _allgather8(mod_cols, "gather_mod").reshape(N_DEV, 8, n_ada)
    mod_mine = lax.dynamic_index_in_dim(mod_all, dev, axis=1, keepdims=False)
    mod = mod_mine.reshape(N_CHIPS, 2, n_ada)[:, 0].reshape(6, D)
    vec = jnp.concatenate([mod, ln1_g, ln1_b, ln2_g, ln2_b, jnp.concatenate([gn_sb, gn_swa], axis=1),
                           jnp.zeros((VEC_ROWS - 11, D), F32)], axis=0)

    g_in, g_out, g_gu, g_down = _chip_allgather(
        [w_in[0].astype(BF16), w_out[0].astype(BF16), w_gu[0].astype(BF16), w_down[0].astype(BF16)], "gather_weights")
    w_in_b = g_in.transpose(1, 0, 2).reshape(D, D_IN)
    w_gu_b = g_gu.transpose(1, 0, 2).reshape(D, 2 * D_FF)
    w_out_b = g_out.reshape(D, D)
    w_down_b = g_down.reshape(D_FF, D)

    h_b, proj = _in_proj(xs, vec, w_in_b, b_in, "in_proj")
    y_sb, sp_total = _sb_forward(proj, "sb_forward")
    sink_vec = sinks[0]
    y_sw = _swa_forward(proj, sink_vec, "swa_forward")
    mixed_b, attn, x1, h2_b = _post_attention(y_sb, y_sw, xs, vec, w_out_b, "post_attention")
    gu, act_b, ffn = _ffn_forward(h2_b, w_gu_b, w_down_b, "ffn_forward")

    dffn_b, dgu_b, dx1, acc_f = _ffn_backward(x1, ffn, target, gu, vec, w_gu_b, w_down_b, "ffn_backward")
    du1, dattn_b, dy, acc_a = _attn_out_backward(dx1, xs, attn, y_sb, y_sw, vec, w_out_b, "attn_out_backward")
    dq_sb, dk_sb, dv_sb = _sb_backward(proj, sp_total, dy, "sb_backward")
    dq_sw, dk_sw, dv_sw, dsink = _swa_backward(proj, y_sw, dy, sink_vec, "swa_backward")
    dproj_b, grad_x, acc_i, acc_b = _in_proj_backward(dq_sb, dk_sb, dv_sb, dq_sw, dk_sw, dv_sw, du1, xs, vec, w_in_b,
                                                      "in_proj_backward")
    dw_in = _weight_grad(h_b, dproj_b, "grad_w_in")
    dw_out = _weight_grad(mixed_b, dattn_b, "grad_w_out")
    dw_gu = _weight_grad(h2_b, dgu_b, "grad_w_gu")
    dw_down = _weight_grad(act_b, dffn_b, "grad_w_down")

    shard_sizes = [D * (D_IN // 4), (D // 4) * D, D * (2 * D_FF // 4), (D_FF // 4) * D]
    flat = jnp.concatenate([_col_shards(dw_in, 4).reshape(4, -1), dw_out.reshape(4, -1),
                            _col_shards(dw_gu, 4).reshape(4, -1), dw_down.reshape(4, -1)], axis=1)
    half_rows = flat.shape[1] // (2 * LANES)
    halves = flat.reshape(4, 2, half_rows, LANES)
    keep = lax.dynamic_index_in_dim(halves, ic, axis=1, keepdims=False)
    give = lax.dynamic_index_in_dim(halves, 1 - ic, axis=1, keepdims=False)
    got = _sibling_send(give, "grad_halves_swap")
    chip_sum = _add2(keep.reshape(4 * half_rows, LANES), got.reshape(4 * half_rows, LANES), "grad_chip_sum")
    parts = _chip_scatter(chip_sum.reshape(4, half_rows, LANES), "grad_chip_scatter")
    my_half = _sum4(parts, "grad_reduce")
    other_half = _sibling_send(my_half, "grad_half_return")
    first = jnp.where(ic == 0, my_half, other_half)
    second = jnp.where(ic == 0, other_half, my_half)
    shard_flat = jnp.concatenate([first, second], axis=0).reshape(-1)
    offs = [0]
    for n in shard_sizes:
        offs.append(offs[-1] + n)
    gw_in = shard_flat[offs[0]:offs[1]].reshape(D, D_IN // 4)
    gw_out = shard_flat[offs[1]:offs[2]].reshape(D // 4, D)
    gw_gu = shard_flat[offs[2]:offs[3]].reshape(D, 2 * D_FF // 4)
    gw_down = shard_flat[offs[3]:offs[4]].reshape(D_FF // 4, D)

    dmod = jnp.concatenate([acc_i[C_SHA:C_SHA + 1], acc_i[C_SCA:C_SCA + 1], acc_a[B_GA:B_GA + 1],
                            acc_f[A_SHF:A_SHF + 1], acc_f[A_SCF:A_SCF + 1], acc_f[A_GF:A_GF + 1]], axis=1)
    dsink_row = jnp.concatenate([dsink[:, 0].reshape(1, 8), jnp.zeros((1, LANES - 8), F32)], axis=1)
    loss_row = jnp.concatenate([jnp.sum(acc_f[A_LOSS:A_LOSS + 1], axis=1, keepdims=True),
                                jnp.zeros((1, LANES - 1), F32)], axis=1)
    small = jnp.concatenate([dmod, acc_b[0:1], acc_a[B_LN1G:B_LN1G + 1], acc_a[B_LN1B:B_LN1B + 1],
                             acc_f[A_LN2G:A_LN2G + 1], acc_f[A_LN2B:A_LN2B + 1], acc_a[B_GN:B_GN + 1],
                             dsink_row, loss_row], axis=1)
    small_all = _allgather8(_pad_rows(small, 8), "gather_small")[::8]

    def pack_small(b_ada_, b_in_, ln1g_, ln1b_, ln2g_, ln2b_, gsb_, gsw_, sinks_):
        return jnp.concatenate([b_ada_, b_in_, ln1g_, ln1b_, ln2g_, ln2b_, gsb_, gsw_, sinks_,
                                jnp.ones((1, 2 * LANES - 8), F32)], axis=1)

    w_small = pack_small(b_ada, b_in, ln1_g, ln1_b, ln2_g, ln2_b, gn_sb, gn_swa, sinks)
    m_small = pack_small(m_b_ada, m_b_in, m_ln1_g, m_ln1_b, m_ln2_g, m_ln2_b, m_gn_sb, m_gn_swa, m_sinks)
    v_small = pack_small(v_b_ada, v_b_in, v_ln1_g, v_ln1_b, v_ln2_g, v_ln2_b, v_gn_sb, v_gn_swa, v_sinks)
    small_out = _small_update(small_all, w_small, m_small, v_small, "small_update")

    def unpack_small(row):
        return {"b_ada": row[:, SM_MOD:SM_BIN], "b_in": row[:, SM_BIN:SM_LN1G], "ln1_g": row[:, SM_LN1G:SM_LN1B],
                "ln1_b": row[:, SM_LN1B:SM_LN2G], "ln2_g": row[:, SM_LN2G:SM_LN2B], "ln2_b": row[:, SM_LN2B:SM_GN],
                "gn_sb": row[:, SM_GN:SM_GN + SB_W], "gn_swa": row[:, SM_GN + SB_W:SM_SINK],
                "sinks": row[:, SM_SINK:SM_SINK + 8]}

    g_small, d_small, m2_small, v2_small = [unpack_small(r) for r in small_out]
    loss = small_out[0][0, SM_LOSS]

    dmod_cols = lax.dynamic_slice_in_dim(small_all[:, SM_MOD:SM_BIN], chip * n_ada, n_ada, axis=1)
    gw_ada = _weight_grad(_pad_rows(silu_c, LANES).astype(BF16), _pad_rows(dmod_cols, LANES).astype(BF16), "grad_w_ada")

    big = {}
    for nm, w, g, m, v in (("w_ada", w_ada, gw_ada, m_w_ada, v_w_ada), ("w_in", w_in, gw_in, m_w_in, v_w_in),
                           ("w_out", w_out, gw_out, m_w_out, v_w_out), ("w_gu", w_gu, gw_gu, m_w_gu, v_w_gu),
                           ("w_down", w_down, gw_down, m_w_down, v_w_down)):
        d_, m2_, v2_ = _adamw(w[0], g, m[0], v[0], "adamw_" + nm)
        big[nm] = (g[None], d_[None], m2_[None], v2_[None])

    order = ["w_ada", "b_ada", "w_in", "b_in", "sinks", "gn_sb", "gn_swa", "w_out", "ln1_g", "ln1_b", "w_gu", "w_down",
             "ln2_g", "ln2_b"]

    def leaf(nm, which):
        if nm in big:
            return big[nm][which]
        return (g_small, d_small, m2_small, v2_small)[which][nm]

    outs = [loss, grad_x[None]]
    for which in range(4):
        outs += [leaf(nm, which) for nm in order]
    return tuple(outs)
```

```python
import functools
import math

import jax
import jax.numpy as jnp
from jax import lax
from jax.experimental import pallas as pl
from jax.experimental.pallas import tpu as pltpu

F32 = jnp.float32
BF16 = jnp.bfloat16

D = 1024
HEAD_DIM = 64
SB_W = 512
SWA_QW = 512
SWA_KW = 128
D_IN = 2304
D_FF = 2816
WINDOW = 128
ALPHA = 2.0 ** 0.25
LN_EPS = 1e-5
RMS_EPS = 1e-6
MASK_VALUE = -1e30
QK_SCALE = 1.0 / math.sqrt(HEAD_DIM)

ADAM_LR = 0.001
ADAM_B1 = 0.9
ADAM_B2 = 0.999
ADAM_EPS = 1e-08
ADAM_WD = 0.01
ADAM_STEP = 10

N_CHIPS = 4
N_DEV = 8
LANES = 128

SB_TQ = 512
SB_TK = 256
SB_UNROLL = 2
SB_UNROLL_BWD = 2
TOK_TILE = 512
FFN_TILE = 256
FFN_BWD_TILE = 256
VMEM_LIMIT = 56 * 1024 * 1024

V_SH_A, V_SC_A, V_G_A, V_SH_F, V_SC_F, V_G_F, V_LN1G, V_LN1B, V_LN2G, V_LN2B, V_GN = range(11)
VEC_ROWS = 16

SM_MOD = 0
SM_BIN = 6 * D
SM_LN1G = SM_BIN + D_IN
SM_LN1B = SM_LN1G + D
SM_LN2G = SM_LN1B + D
SM_LN2B = SM_LN2G + D
SM_GN = SM_LN2B + D
SM_SINK = SM_GN + D
SM_LOSS = SM_SINK + LANES
SM_LEN = SM_LOSS + LANES

MESH = pl.DeviceIdType.MESH


def _cparams(**kw):
    return pltpu.CompilerParams(vmem_limit_bytes=VMEM_LIMIT, **kw)


def _resident(shape):
    nd = len(shape)
    return pl.BlockSpec(shape, lambda *_: (0,) * nd, pipeline_mode=pl.Buffered(1))


def _dot(a, b):
    return jnp.dot(a, b, preferred_element_type=F32)


def _dot_nt(a, b):
    return lax.dot_general(a, b, (((1,), (1,)), ((), ())), preferred_element_type=F32)


def _dot_tn(a, b):
    return lax.dot_general(a, b, (((0,), (0,)), ((), ())), preferred_element_type=F32)


def _block_sums(x, m):
    return _dot(x.astype(BF16), m)


def _allgather8(v, name):
    m_per, n = v.shape

    def body(x_ref, out_ref, send_sems, recv_sems, local_sem):
        x, y, c = lax.axis_index("x"), lax.axis_index("y"), lax.axis_index("c")
        me, sibling = (x, y, c), (x, y, 1 - c)
        chips = [(1 - x, y), (x, 1 - y), (1 - x, 1 - y)]

        def rows(px, py, pc):
            return out_ref.at[pl.ds((4 * px + 2 * py + pc) * m_per, m_per), :]

        def copy(k, block, to, src=None):
            return pltpu.make_async_remote_copy(
                src_ref=rows(*block) if src is None else src, dst_ref=rows(*block),
                send_sem=send_sems.at[k], recv_sem=recv_sems.at[k], device_id=to, device_id_type=MESH)

        mine = pltpu.make_async_copy(x_ref, rows(*me), local_sem)
        mine.start()
        first = [copy(0, me, sibling, src=x_ref)]
        first += [copy(1 + j, me, (*chip, c), src=x_ref) for j, chip in enumerate(chips)]
        for cp in first:
            cp.start()
        passed = [copy(4 + j, (*chip, c), sibling) for j, chip in enumerate(chips)]
        for j, chip in enumerate(chips):
            copy(1 + j, (*chip, c), me).wait_recv()
            passed[j].start()
        copy(0, sibling, me).wait_recv()
        for j, chip in enumerate(chips):
            copy(4 + j, (*chip, 1 - c), me).wait_recv()
        for cp in first + passed:
            cp.wait_send()
        mine.wait()

    return pl.pallas_call(
        body, name=name,
        out_shape=jax.ShapeDtypeStruct((N_DEV * m_per, n), v.dtype),
        in_specs=[pl.BlockSpec(memory_space=pltpu.VMEM)],
        out_specs=pl.BlockSpec(memory_space=pltpu.VMEM),
        scratch_shapes=[pltpu.SemaphoreType.DMA((7,)), pltpu.SemaphoreType.DMA((7,)), pltpu.SemaphoreType.DMA],
        compiler_params=_cparams(),
    )(v)


def _chip_allgather(arrs, name):
    n = len(arrs)

    def body(*refs):
        ins, outs = refs[:n], refs[n:2 * n]
        send_sems, recv_sems, local_sems = refs[2 * n:]
        x, y, c = lax.axis_index("x"), lax.axis_index("y"), lax.axis_index("c")
        slot = 2 * x + y
        chips = [(1 - x, y), (x, 1 - y), (1 - x, 1 - y)]
        local, sent = [], []
        for a in range(n):
            cp = pltpu.make_async_copy(ins[a], outs[a].at[slot], local_sems.at[a])
            cp.start()
            local.append(cp)
            for j, (px, py) in enumerate(chips):
                cp = pltpu.make_async_remote_copy(
                    src_ref=ins[a], dst_ref=outs[a].at[slot], send_sem=send_sems.at[3 * a + j],
                    recv_sem=recv_sems.at[3 * a + j], device_id=(px, py, c), device_id_type=MESH)
                cp.start()
                sent.append(cp)
        for a in range(n):
            for j, (px, py) in enumerate(chips):
                pltpu.make_async_remote_copy(
                    src_ref=ins[a], dst_ref=outs[a].at[2 * px + py], send_sem=send_sems.at[3 * a + j],
                    recv_sem=recv_sems.at[3 * a + j], device_id=(px, py, c), device_id_type=MESH).wait_recv()
        for cp in sent:
            cp.wait_send()
        for cp in local:
            cp.wait()

    hbm = pl.BlockSpec(memory_space=pl.ANY)
    return pl.pallas_call(
        body, name=name,
        out_shape=[jax.ShapeDtypeStruct((N_CHIPS,) + a.shape, a.dtype) for a in arrs],
        in_specs=[hbm] * n, out_specs=[hbm] * n,
        scratch_shapes=[pltpu.SemaphoreType.DMA((3 * n,)), pltpu.SemaphoreType.DMA((3 * n,)),
                        pltpu.SemaphoreType.DMA((n,))],
        compiler_params=_cparams(),
    )(*arrs)


def _sibling_send(v, name):
    def body(v_ref, out_ref, send_sem, recv_sem):
        x, y, c = lax.axis_index("x"), lax.axis_index("y"), lax.axis_index("c")
        cp = pltpu.make_async_remote_copy(src_ref=v_ref, dst_ref=out_ref, send_sem=send_sem, recv_sem=recv_sem,
                                          device_id=(x, y, 1 - c), device_id_type=MESH)
        cp.start()
        cp.wait()

    hbm = pl.BlockSpec(memory_space=pl.ANY)
    return pl.pallas_call(
        body, name=name, out_shape=jax.ShapeDtypeStruct(v.shape, v.dtype),
        in_specs=[hbm], out_specs=hbm,
        scratch_shapes=[pltpu.SemaphoreType.DMA, pltpu.SemaphoreType.DMA],
        compiler_params=_cparams(),
    )(v)


def _chip_scatter(p, name):
    def body(p_ref, out_ref, send_sems, recv_sems, local_sem):
        x, y, c = lax.axis_index("x"), lax.axis_index("y"), lax.axis_index("c")
        slot = 2 * x + y
        chips = [(1 - x, y), (x, 1 - y), (1 - x, 1 - y)]
        mine = pltpu.make_async_copy(p_ref.at[slot], out_ref.at[slot], local_sem)
        mine.start()
        sent = []
        for j, (px, py) in enumerate(chips):
            cp = pltpu.make_async_remote_copy(
                src_ref=p_ref.at[2 * px + py], dst_ref=out_ref.at[slot], send_sem=send_sems.at[j],
                recv_sem=recv_sems.at[j], device_id=(px, py, c), device_id_type=MESH)
            cp.start()
            sent.append(cp)
        for j, (px, py) in enumerate(chips):
            pltpu.make_async_remote_copy(
                src_ref=p_ref.at[slot], dst_ref=out_ref.at[2 * px + py], send_sem=send_sems.at[j],
                recv_sem=recv_sems.at[j], device_id=(px, py, c), device_id_type=MESH).wait_recv()
        for cp in sent:
            cp.wait_send()
        mine.wait()

    hbm = pl.BlockSpec(memory_space=pl.ANY)
    return pl.pallas_call(
        body, name=name, out_shape=jax.ShapeDtypeStruct(p.shape, p.dtype),
        in_specs=[hbm], out_specs=hbm,
        scratch_shapes=[pltpu.SemaphoreType.DMA((3,)), pltpu.SemaphoreType.DMA((3,)), pltpu.SemaphoreType.DMA],
        compiler_params=_cparams(),
    )(p)


def _add2(a, b, name):
    rows = a.shape[0]
    tr = rows // 8

    def body(a_ref, b_ref, o_ref):
        o_ref[...] = a_ref[...] + b_ref[...]

    spec = pl.BlockSpec((tr, LANES), lambda i: (i, 0))
    return pl.pallas_call(body, name=name, grid=(rows // tr,), out_shape=jax.ShapeDtypeStruct(a.shape, a.dtype),
                          in_specs=[spec, spec], out_specs=spec, compiler_params=_cparams())(a, b)


def _sum4(p, name):
    rows = p.shape[1]
    tr = rows // 8

    def body(p_ref, o_ref):
        o_ref[...] = ((p_ref[0] + p_ref[1]) + p_ref[2]) + p_ref[3]

    return pl.pallas_call(
        body, name=name, grid=(rows // tr,), out_shape=jax.ShapeDtypeStruct((rows, LANES), p.dtype),
        in_specs=[pl.BlockSpec((4, tr, LANES), lambda i: (0, i, 0))],
        out_specs=pl.BlockSpec((tr, LANES), lambda i: (i, 0)), compiler_params=_cparams())(p)


def _adam_math(w, g, m, v):
    m2 = ADAM_B1 * m + (1.0 - ADAM_B1) * g
    v2 = ADAM_B2 * v + (1.0 - ADAM_B2) * (g * g)
    m_hat = m2 / (1.0 - ADAM_B1 ** ADAM_STEP)
    v_hat = v2 / (1.0 - ADAM_B2 ** ADAM_STEP)
    delta = -ADAM_LR * (m_hat / (jnp.sqrt(v_hat) + ADAM_EPS) + ADAM_WD * w)
    return delta, m2, v2


def _adamw(w, g, m, v, name):
    rows, cols = w.shape
    tr = rows // 4 if rows % 32 == 0 else rows

    def body(w_ref, g_ref, m_ref, v_ref, d_ref, m2_ref, v2_ref):
        delta, m2, v2 = _adam_math(w_ref[...], g_ref[...], m_ref[...], v_ref[...])
        d_ref[...] = delta
        m2_ref[...] = m2
        v2_ref[...] = v2

    spec = pl.BlockSpec((tr, cols), lambda i: (i, 0))
    shp = jax.ShapeDtypeStruct(w.shape, F32)
    return pl.pallas_call(body, name=name, grid=(rows // tr,), out_shape=[shp, shp, shp],
                          in_specs=[spec] * 4, out_specs=[spec] * 3, compiler_params=_cparams())(w, g, m, v)


def _small_update(g8, w, m, v, name):
    n = w.shape[1]

    def body(g8_ref, w_ref, m_ref, v_ref, g_ref, d_ref, m2_ref, v2_ref):
        g = g8_ref[0:1, :]
        for r in range(1, N_DEV):
            g = g + g8_ref[r:r + 1, :]
        delta, m2, v2 = _adam_math(w_ref[...], g, m_ref[...], v_ref[...])
        g_ref[...] = g
        d_ref[...] = delta
        m2_ref[...] = m2
        v2_ref[...] = v2

    shp = jax.ShapeDtypeStruct((1, n), F32)
    vm = pl.BlockSpec(memory_space=pltpu.VMEM)
    return pl.pallas_call(body, name=name, out_shape=[shp] * 4, in_specs=[vm] * 4, out_specs=[vm] * 4,
                          compiler_params=_cparams())(g8, w, m, v)


def _mod_shard(c8, w_ada, b_ada_shard, name):
    n = w_ada.shape[1]
    tn = 512

    def body(c_ref, w_ref, b_ref, o_ref, s_ref):
        cv = c_ref[...]
        sc = cv * (1.0 / (1.0 + jnp.exp(-cv)))
        s_ref[...] = sc
        o_ref[...] = _dot(sc.astype(BF16), w_ref[...].astype(BF16)) + b_ref[...]

    return pl.pallas_call(
        body, name=name, grid=(n // tn,),
        out_shape=[jax.ShapeDtypeStruct((8, n), F32), jax.ShapeDtypeStruct((8, D), F32)],
        in_specs=[pl.BlockSpec((8, D), lambda j: (0, 0)), pl.BlockSpec((D, tn), lambda j: (0, j)),
                  pl.BlockSpec((1, tn), lambda j: (0, j))],
        out_specs=[pl.BlockSpec((8, tn), lambda j: (0, j)), pl.BlockSpec((8, D), lambda j: (0, 0))],
        compiler_params=_cparams())(c8, w_ada, b_ada_shard)


def _layer_norm_stats(u):
    mu = jnp.mean(u, axis=1, keepdims=True)
    d = u - mu
    var = jnp.mean(d * d, axis=1, keepdims=True)
    rstd = lax.rsqrt(var + LN_EPS)
    return d * rstd, rstd


def _in_proj(x, vec, w_in, b_in, name):
    s = x.shape[0]
    tb = min(TOK_TILE, s)

    def body(x_ref, vec_ref, w_ref, b_ref, h_ref, p_ref):
        h = x_ref[...] * (1.0 + vec_ref[V_SC_A:V_SC_A + 1, :]) + vec_ref[V_SH_A:V_SH_A + 1, :]
        hb = h.astype(BF16)
        h_ref[...] = hb
        proj = _dot(hb, w_ref[...]) + b_ref[...]
        col = lax.broadcasted_iota(jnp.int32, (1, D_IN), 1)
        is_q = (col < SB_W) | ((col >= 3 * SB_W) & (col < 3 * SB_W + SWA_QW))
        p_ref[...] = (proj * jnp.where(is_q, QK_SCALE, 1.0)).astype(BF16)

    return pl.pallas_call(
        body, name=name, grid=(s // tb,),
        out_shape=[jax.ShapeDtypeStruct((s, D), BF16), jax.ShapeDtypeStruct((s, D_IN), BF16)],
        in_specs=[pl.BlockSpec((tb, D), lambda i: (i, 0)), _resident((VEC_ROWS, D)), _resident((D, D_IN)),
                  _resident((1, D_IN))],
        out_specs=[pl.BlockSpec((tb, D), lambda i: (i, 0)), pl.BlockSpec((tb, D_IN), lambda i: (i, 0))],
        compiler_params=_cparams())(x, vec, w_in, b_in)


def _softplus_parts(z):
    e1 = jnp.exp(-jnp.abs(z))
    sp = jnp.maximum(z, 0.0) + jnp.log(1.0 + e1)
    return sp, e1


def _sb_forward(proj, name):
    s = proj.shape[0]
    tq, tk = min(SB_TQ, s), min(SB_TK, s)
    r = tq // tk

    def body(q_ref, k_ref, v_ref, o_ref, tot_ref, acc_refs, run_refs):
        i = pl.program_id(1)
        lane = lax.broadcasted_iota(jnp.int32, (1, LANES), 1)
        first = lane < HEAD_DIM
        qp = q_ref[...]
        zero = jnp.zeros((), BF16)
        qs = (jnp.where(first, qp, zero), jnp.where(first, zero, qp))
        row = lax.broadcasted_iota(jnp.int32, (tk, tk), 0)
        col = lax.broadcasted_iota(jnp.int32, (tk, tk), 1)
        later = (row > col).astype(BF16)
        acc_refs[...] = jnp.zeros_like(acc_refs)
        run_refs[...] = jnp.zeros_like(run_refs)

        def blocks(js, masked):
            ks = [pl.multiple_of(j * tk, tk) for j in js]
            kjs = [k_ref[pl.ds(k0, tk), :] for k0 in ks]
            vjs = [v_ref[pl.ds(k0, tk), :] for k0 in ks]
            chains = [(hd, b) for b in range(len(js)) for hd in range(2)]
            zs = [_dot_nt(qs[hd], kjs[b]) for hd, b in chains]
            sps = [_softplus_parts(z)[0] for z in zs]
            if masked:
                t_idx = i * tq + lax.broadcasted_iota(jnp.int32, (tq, tk), 0)
                befores = [j * tk + lax.broadcasted_iota(jnp.int32, (tq, tk), 1) < t_idx for j in js]
                spms = [jnp.where(befores[b], sp, 0.0) for (hd, b), sp in zip(chains, sps)]
            else:
                spms = sps
            cums = [_block_sums(spm, later) for spm in spms]
            sums = [jnp.sum(spm, axis=1, keepdims=True) for spm in spms]
            runs = [run_refs[0], run_refs[1]]
            ws = []
            for (hd, b), z, sp, cum, sm in zip(chains, zs, sps, cums, sums):
                w = jnp.exp(z - sp - cum - runs[hd])
                if masked:
                    w = jnp.where(befores[b], w, 0.0)
                ws.append(w.astype(BF16))
                runs[hd] = runs[hd] + sm
            pvs = [_dot(w, vjs[b]) for (hd, b), w in zip(chains, ws)]
            for hd in range(2):
                tot = pvs[hd]
                for b in range(1, len(js)):
                    tot = tot + pvs[2 * b + hd]
                acc_refs[hd] += tot
                run_refs[hd] = runs[hd]

        blocks([i * r + (r - 1 - d) for d in range(r)], True)

        below = i * r

        def sweep(n, carry):
            top = below - 1 - SB_UNROLL * n
            blocks([top - u for u in range(SB_UNROLL)], False)
            return carry

        lax.fori_loop(0, below // SB_UNROLL, sweep, 0)
        for u in range(1, SB_UNROLL):
            @pl.when(below % SB_UNROLL >= u)
            def _(u=u):
                blocks([below % SB_UNROLL - u], False)
        o_ref[...] = jnp.where(first, acc_refs[0], acc_refs[1])
        tot_ref[...] = jnp.where(first, run_refs[0], run_refs[1])

    nkb = SB_W // LANES
    shp = jax.ShapeDtypeStruct((s, SB_W), F32)
    qspec = pl.BlockSpec((tq, LANES), lambda p, i: (i, p))
    return pl.pallas_call(
        body, name=name, grid=(nkb, s // tq),
        out_shape=[shp, shp],
        in_specs=[qspec,
                  pl.BlockSpec((s, LANES), lambda p, i: (0, nkb + p)),
                  pl.BlockSpec((s, LANES), lambda p, i: (0, 2 * nkb + p))],
        out_specs=[qspec, qspec],
        scratch_shapes=[pltpu.VMEM((2, tq, LANES), F32), pltpu.VMEM((2, tq, 1), F32)],
        compiler_params=_cparams())(proj, proj, proj)


def _swa_masks(n):
    ti = lax.broadcasted_iota(jnp.int32, (WINDOW, 2 * WINDOW), 0)
    kj = lax.broadcasted_iota(jnp.int32, (WINDOW, 2 * WINDOW), 1)
    dist = ti + WINDOW - kj
    valid = (dist >= 0) & (dist < WINDOW) & ((n * WINDOW - WINDOW + kj) >= 0)
    return valid, dist.astype(F32)


def _swa_probs(qm, ku, valid, distf, h, sink):
    slope = 2.0 ** (-(h + 1))
    sc = _dot_nt(qm, ku)
    sc = jnp.where(valid, sc - slope * distf, MASK_VALUE)
    mx = jnp.maximum(jnp.max(sc, axis=1, keepdims=True), sink)
    p = jnp.exp(sc - mx)
    es = jnp.exp(sink - mx)
    inv = 1.0 / (jnp.sum(p, axis=1, keepdims=True) + es)
    return p * inv, es * inv


def _swa_forward(proj, sinks, name):
    s = proj.shape[0]
    nb = s // WINDOW
    qb, kb, vb = 3 * SB_W // SWA_QW, (3 * SB_W + SWA_QW) // LANES, (3 * SB_W + SWA_QW + SWA_KW) // LANES

    def body(q_ref, kp_ref, kc_ref, vp_ref, vc_ref, sink_ref, o_ref):
        n = pl.program_id(0)
        k = jnp.concatenate([kp_ref[...], kc_ref[...]], axis=0)
        v = jnp.concatenate([vp_ref[...], vc_ref[...]], axis=0)
        k_sw = pltpu.roll(k.astype(F32), HEAD_DIM, 1).astype(BF16)
        v_sw = pltpu.roll(v.astype(F32), HEAD_DIM, 1).astype(BF16)
        lane = lax.broadcasted_iota(jnp.int32, (1, LANES), 1)
        halves = [lane < HEAD_DIM, lane >= HEAD_DIM]
        valid, distf = _swa_masks(n)
        for pair in range(4):
            qp = q_ref[:, pair * LANES:(pair + 1) * LANES]
            out = jnp.zeros((WINDOW, LANES), F32)
            for par in range(2):
                h = 2 * pair + par
                g = h // 4
                qm = jnp.where(halves[par], qp, jnp.zeros((), BF16))
                ku, vu = (k, v) if g == par else (k_sw, v_sw)
                p, _ = _swa_probs(qm, ku, valid, distf, h, sink_ref[h])
                out = out + jnp.where(halves[par], _dot(p.astype(BF16), vu), 0.0)
            o_ref[:, pair * LANES:(pair + 1) * LANES] = out

    prev = lambda n: jnp.maximum(n - 1, 0)
    return pl.pallas_call(
        body, name=name, grid=(nb,),
        out_shape=jax.ShapeDtypeStruct((s, SWA_QW), F32),
        in_specs=[pl.BlockSpec((WINDOW, SWA_QW), lambda n: (n, qb)),
                  pl.BlockSpec((WINDOW, LANES), lambda n: (prev(n), kb)),
                  pl.BlockSpec((WINDOW, LANES), lambda n: (n, kb)),
                  pl.BlockSpec((WINDOW, LANES), lambda n: (prev(n), vb)),
                  pl.BlockSpec((WINDOW, LANES), lambda n: (n, vb)),
                  pl.BlockSpec(memory_space=pltpu.SMEM)],
        out_specs=pl.BlockSpec((WINDOW, SWA_QW), lambda n: (n, 0)),
        compiler_params=_cparams())(proj, proj, proj, proj, proj, sinks)


def _rms_parts(y):
    return lax.rsqrt(jnp.mean(y * y, axis=1, keepdims=True) + RMS_EPS)


def _post_attention(y_sb, y_sw, x, vec, w_out, name):
    s = x.shape[0]
    tb = min(TOK_TILE, s)

    def body(ysb_ref, ysw_ref, x_ref, vec_ref, w_ref, mixed_ref, attn_ref, x1_ref, h2_ref):
        ysb, ysw = ysb_ref[...], ysw_ref[...]
        nsb = (ysb * _rms_parts(ysb) * vec_ref[V_GN:V_GN + 1, :SB_W]).astype(BF16)
        nsw = (ysw * _rms_parts(ysw) * vec_ref[V_GN:V_GN + 1, SB_W:]).astype(BF16)
        mixed_ref[:, :SB_W] = nsb
        mixed_ref[:, SB_W:] = nsw
        attn = _dot(nsb, w_ref[:SB_W, :]) + _dot(nsw, w_ref[SB_W:, :])
        attn_ref[...] = attn
        u1 = ALPHA * x_ref[...] + (1.0 + vec_ref[V_G_A:V_G_A + 1, :]) * attn
        xhat, _ = _layer_norm_stats(u1)
        x1 = xhat * vec_ref[V_LN1G:V_LN1G + 1, :] + vec_ref[V_LN1B:V_LN1B + 1, :]
        x1_ref[...] = x1
        h2_ref[...] = (x1 * (1.0 + vec_ref[V_SC_F:V_SC_F + 1, :]) + vec_ref[V_SH_F:V_SH_F + 1, :]).astype(BF16)

    half = pl.BlockSpec((tb, SB_W), lambda i: (i, 0))
    full = pl.BlockSpec((tb, D), lambda i: (i, 0))
    return pl.pallas_call(
        body, name=name, grid=(s // tb,),
        out_shape=[jax.ShapeDtypeStruct((s, D), BF16), jax.ShapeDtypeStruct((s, D), F32),
                   jax.ShapeDtypeStruct((s, D), F32), jax.ShapeDtypeStruct((s, D), BF16)],
        in_specs=[half, half, full, _resident((VEC_ROWS, D)), _resident((D, D))],
        out_specs=[full, full, full, full],
        compiler_params=_cparams())(y_sb, y_sw, x, vec, w_out)


def _ffn_forward(h2, w_gu, w_down, name):
    s = h2.shape[0]
    tb = min(FFN_TILE, s)

    def body(h_ref, wgu_ref, wd_ref, gu_ref, act_ref, ffn_ref):
        gu = _dot(h_ref[...], wgu_ref[...])
        gu_ref[...] = gu.astype(BF16)
        gate, up = gu[:, :D_FF], gu[:, D_FF:]
        act = (gate * (1.0 / (1.0 + jnp.exp(-gate))) * up).astype(BF16)
        act_ref[...] = act
        ffn_ref[...] = _dot(act, wd_ref[...])

    return pl.pallas_call(
        body, name=name, grid=(s // tb,),
        out_shape=[jax.ShapeDtypeStruct((s, 2 * D_FF), BF16), jax.ShapeDtypeStruct((s, D_FF), BF16),
                   jax.ShapeDtypeStruct((s, D), F32)],
        in_specs=[pl.BlockSpec((tb, D), lambda i: (i, 0)), _resident((D, 2 * D_FF)), _resident((D_FF, D))],
        out_specs=[pl.BlockSpec((tb, 2 * D_FF), lambda i: (i, 0)), pl.BlockSpec((tb, D_FF), lambda i: (i, 0)),
                   pl.BlockSpec((tb, D), lambda i: (i, 0))],
        compiler_params=_cparams())(h2, w_gu, w_down)


def _layer_norm_bwd(dxhat, xhat, rstd):
    m1 = jnp.mean(dxhat, axis=1, keepdims=True)
    m2 = jnp.mean(dxhat * xhat, axis=1, keepdims=True)
    return rstd * (dxhat - m1 - xhat * m2)


def _colsum(a):
    return jnp.sum(a, axis=0, keepdims=True)


A_LN2G, A_LN2B, A_GF, A_SCF, A_SHF, A_LOSS = range(6)
B_LN1G, B_LN1B, B_GA, B_GN = range(4)
C_SCA, C_SHA = range(2)


def _ffn_backward(x1, ffn, target, gu, vec, w_gu, w_down, name):
    s = x1.shape[0]
    tb = min(FFN_BWD_TILE, s)

    def body(x1_ref, ffn_ref, t_ref, gu_ref, vec_ref, wgu_ref, wd_ref, dffn_ref, dgu_ref, dx1_ref, acc_ref):
        @pl.when(pl.program_id(0) == 0)
        def _():
            acc_ref[...] = jnp.zeros_like(acc_ref)

        x1v, ffn_v = x1_ref[...], ffn_ref[...]
        g_f = 1.0 + vec_ref[V_G_F:V_G_F + 1, :]
        u2 = ALPHA * x1v + g_f * ffn_v
        xhat, rstd = _layer_norm_stats(u2)
        ln_g = vec_ref[V_LN2G:V_LN2G + 1, :]
        err = xhat * ln_g + vec_ref[V_LN2B:V_LN2B + 1, :] - t_ref[...]
        dx2 = err * (1.0 / D)
        acc_ref[A_LOSS:A_LOSS + 1, :] += _colsum(err * err) * (0.5 / D)
        acc_ref[A_LN2G:A_LN2G + 1, :] += _colsum(dx2 * xhat)
        acc_ref[A_LN2B:A_LN2B + 1, :] += _colsum(dx2)
        du2 = _layer_norm_bwd(dx2 * ln_g, xhat, rstd)
        acc_ref[A_GF:A_GF + 1, :] += _colsum(du2 * ffn_v)
        dffn = (g_f * du2).astype(BF16)
        dffn_ref[...] = dffn
        dact = _dot_nt(dffn, wd_ref[...])
        gate, up = gu_ref[:, :D_FF].astype(F32), gu_ref[:, D_FF:].astype(F32)
        sg = 1.0 / (1.0 + jnp.exp(-gate))
        dgate = (dact * up * (sg * (1.0 + gate * (1.0 - sg)))).astype(BF16)
        dup = (dact * (gate * sg)).astype(BF16)
        dgu_ref[:, :D_FF] = dgate
        dgu_ref[:, D_FF:] = dup
        dh2 = _dot_nt(dgate, wgu_ref[:, :D_FF]) + _dot_nt(dup, wgu_ref[:, D_FF:])
        dx1_ref[...] = ALPHA * du2 + dh2 * (1.0 + vec_ref[V_SC_F:V_SC_F + 1, :])
        acc_ref[A_SCF:A_SCF + 1, :] += _colsum(dh2 * x1v)
        acc_ref[A_SHF:A_SHF + 1, :] += _colsum(dh2)

    full = pl.BlockSpec((tb, D), lambda i: (i, 0))
    wide = pl.BlockSpec((tb, 2 * D_FF), lambda i: (i, 0))
    return pl.pallas_call(
        body, name=name, grid=(s // tb,),
        out_shape=[jax.ShapeDtypeStruct((s, D), BF16), jax.ShapeDtypeStruct((s, 2 * D_FF), BF16),
                   jax.ShapeDtypeStruct((s, D), F32), jax.ShapeDtypeStruct((8, D), F32)],
        in_specs=[full, full, full, wide, _resident((VEC_ROWS, D)), _resident((D, 2 * D_FF)), _resident((D_FF, D))],
        out_specs=[full, wide, full, pl.BlockSpec((8, D), lambda i: (0, 0))],
        compiler_params=_cparams())(x1, ffn, target, gu, vec, w_gu, w_down)


def _attn_out_backward(dx1, x, attn, y_sb, y_sw, vec, w_out, name):
    s = x.shape[0]
    tb = min(TOK_TILE, s)

    def body(dx1_ref, x_ref, attn_ref, ysb_ref, ysw_ref, vec_ref, w_ref, du1_ref, dattn_ref, dy_ref, acc_ref):
        @pl.when(pl.program_id(0) == 0)
        def _():
            acc_ref[...] = jnp.zeros_like(acc_ref)

        attn = attn_ref[...]
        g_a = 1.0 + vec_ref[V_G_A:V_G_A + 1, :]
        xhat, rstd = _layer_norm_stats(ALPHA * x_ref[...] + g_a * attn)
        dx1v = dx1_ref[...]
        acc_ref[B_LN1G:B_LN1G + 1, :] += _colsum(dx1v * xhat)
        acc_ref[B_LN1B:B_LN1B + 1, :] += _colsum(dx1v)
        du1 = _layer_norm_bwd(dx1v * vec_ref[V_LN1G:V_LN1G + 1, :], xhat, rstd)
        du1_ref[...] = du1
        acc_ref[B_GA:B_GA + 1, :] += _colsum(du1 * attn)
        dattn = (g_a * du1).astype(BF16)
        dattn_ref[...] = dattn
        dmixed = _dot_nt(dattn, w_ref[...])
        for lo, y_ref in ((0, ysb_ref), (SB_W, ysw_ref)):
            y = y_ref[...]
            rr = _rms_parts(y)
            dn = dmixed[:, lo:lo + SB_W]
            acc_ref[B_GN:B_GN + 1, lo:lo + SB_W] += _colsum(dn * y * rr)
            dng = dn * vec_ref[V_GN:V_GN + 1, lo:lo + SB_W]
            dy_ref[:, lo:lo + SB_W] = rr * dng - y * (rr * rr * rr) * jnp.mean(dng * y, axis=1, keepdims=True)

    half = pl.BlockSpec((tb, SB_W), lambda i: (i, 0))
    full = pl.BlockSpec((tb, D), lambda i: (i, 0))
    return pl.pallas_call(
        body, name=name, grid=(s // tb,),
        out_shape=[jax.ShapeDtypeStruct((s, D), F32), jax.ShapeDtypeStruct((s, D), BF16),
                   jax.ShapeDtypeStruct((s, D), F32), jax.ShapeDtypeStruct((8, D), F32)],
        in_specs=[full, full, full, half, half, _resident((VEC_ROWS, D)), _resident((D, D))],
        out_specs=[full, full, full, pl.BlockSpec((8, D), lambda i: (0, 0))],
        compiler_params=_cparams())(dx1, x, attn, y_sb, y_sw, vec, w_out)


def _sb_backward(proj, sp_total, dy, name):
    s = proj.shape[0]
    tq, tk = min(SB_TQ, s), min(SB_TK, s)
    r = tq // tk

    def body(q_ref, k_ref, v_ref, tot_ref, do_ref, dq_ref, dk_ref, dv_ref, dq_acc, left_refs, gsum_refs):
        i = pl.program_id(1)

        @pl.when(i == 0)
        def _():
            dk_ref[...] = jnp.zeros_like(dk_ref)
            dv_ref[...] = jnp.zeros_like(dv_ref)

        lane = lax.broadcasted_iota(jnp.int32, (1, LANES), 1)
        first = lane < HEAD_DIM
        qp, dop, totp = q_ref[...], do_ref[...], tot_ref[...]
        zero = jnp.zeros((), BF16)
        qs = (jnp.where(first, qp, zero), jnp.where(first, zero, qp))
        dobs = (jnp.where(first, dop, 0.0).astype(BF16), jnp.where(first, 0.0, dop).astype(BF16))
        row = lax.broadcasted_iota(jnp.int32, (tk, tk), 0)
        col = lax.broadcasted_iota(jnp.int32, (tk, tk), 1)
        later = (row > col).astype(BF16)
        earlier = (row < col).astype(BF16)
        dq_acc[...] = jnp.zeros_like(dq_acc)
        gsum_refs[...] = jnp.zeros_like(gsum_refs)
        left_refs[0] = jnp.max(jnp.where(first, totp, 0.0), axis=1, keepdims=True)
        left_refs[1] = jnp.max(jnp.where(first, 0.0, totp), axis=1, keepdims=True)

        def blocks(js, masked):
            ks = [pl.multiple_of(j * tk, tk) for j in js]
            kjs = [k_ref[pl.ds(k0, tk), :] for k0 in ks]
            vjs = [v_ref[pl.ds(k0, tk), :] for k0 in ks]
            chains = [(hd, b) for b in range(len(js)) for hd in range(2)]
            zs = [_dot_nt(qs[hd], kjs[b]) for hd, b in chains]
            dws = [_dot_nt(dobs[hd], vjs[b]) for hd, b in chains]
            parts = [_softplus_parts(z) for z in zs]
            sps = [p[0] for p in parts]
            if masked:
                t_idx = i * tq + lax.broadcasted_iota(jnp.int32, (tq, tk), 0)
                befores = [j * tk + lax.broadcasted_iota(jnp.int32, (tq, tk), 1) < t_idx for j in js]
                spms = [jnp.where(befores[b], sp, 0.0) for (hd, b), sp in zip(chains, sps)]
            else:
                spms = sps
            cums = [_block_sums(spm, later) for spm in spms]
            sums = [jnp.sum(spm, axis=1, keepdims=True) for spm in spms]
            lefts = [left_refs[0], left_refs[1]]
            ws = []
            for (hd, b), z, sp, cum, sm in zip(chains, zs, sps, cums, sums):
                lefts[hd] = lefts[hd] - sm
                w = jnp.exp(z - sp - cum - lefts[hd])
                if masked:
                    w = jnp.where(befores[b], w, 0.0)
                ws.append(w)
            wbs = [w.astype(BF16) for w in ws]
            dvs = [_dot_tn(wb, dobs[hd]) for (hd, b), wb in zip(chains, wbs)]
            gs = [dw * w for dw, w in zip(dws, ws)]
            gcums = [_block_sums(g, earlier) for g in gs]
            gsums = [gsum_refs[0], gsum_refs[1]]
            dzbs = []
            for (hd, b), z, (sp, e1), g, gcum in zip(chains, zs, parts, gs, gcums):
                inv = 1.0 / (1.0 + e1)
                sig = jnp.where(z >= 0.0, inv, e1 * inv)
                dz = g - sig * (g + gsums[hd] + gcum)
                if masked:
                    dz = jnp.where(befores[b], dz, 0.0)
                dzbs.append(dz.astype(BF16))
                gsums[hd] = gsums[hd] + jnp.sum(g, axis=1, keepdims=True)
            dqs = [_dot(dzb, kjs[b]) for (hd, b), dzb in zip(chains, dzbs)]
            dks = [_dot_tn(dzb, qs[hd]) for (hd, b), dzb in zip(chains, dzbs)]
            for b, k0 in enumerate(ks):
                dv_ref[pl.ds(k0, tk), :] += dvs[2 * b] + dvs[2 * b + 1]
                dk_ref[pl.ds(k0, tk), :] += dks[2 * b] + dks[2 * b + 1]
            for hd in range(2):
                tot = dqs[hd]
                for b in range(1, len(js)):
                    tot = tot + dqs[2 * b + hd]
                dq_acc[hd] += tot
                left_refs[hd] = lefts[hd]
                gsum_refs[hd] = gsums[hd]

        below = i * r

        def sweep(n, carry):
            blocks([SB_UNROLL_BWD * n + u for u in range(SB_UNROLL_BWD)], False)
            return carry

        lax.fori_loop(0, below // SB_UNROLL_BWD, sweep, 0)
        for u in range(SB_UNROLL_BWD - 1, 0, -1):
            @pl.when(below % SB_UNROLL_BWD >= u)
            def _(u=u):
                blocks([below - u], False)
        blocks([below + d for d in range(r)], True)
        dq_ref[...] = jnp.where(first, dq_acc[0], dq_acc[1])

    nkb = SB_W // LANES
    shp = jax.ShapeDtypeStruct((s, SB_W), F32)
    qspec = pl.BlockSpec((tq, LANES), lambda p, i: (i, p))
    whole = pl.BlockSpec((s, LANES), lambda p, i: (0, p))
    return pl.pallas_call(
        body, name=name, grid=(nkb, s // tq),
        out_shape=[shp, shp, shp],
        in_specs=[qspec,
                  pl.BlockSpec((s, LANES), lambda p, i: (0, nkb + p)),
                  pl.BlockSpec((s, LANES), lambda p, i: (0, 2 * nkb + p)),
                  qspec, qspec],
        out_specs=[qspec, whole, whole],
        scratch_shapes=[pltpu.VMEM((2, tq, LANES), F32), pltpu.VMEM((2, tq, 1), F32), pltpu.VMEM((2, tq, 1), F32)],
        compiler_params=_cparams())(proj, proj, proj, sp_total, dy)


def _swa_backward(proj, y_sw, dy, sinks, name):
    s = proj.shape[0]
    nb = s // WINDOW
    qb, kb, vb = 3 * SB_W // SWA_QW, (3 * SB_W + SWA_QW) // LANES, (3 * SB_W + SWA_QW + SWA_KW) // LANES

    def body(q_ref, kp_ref, kc_ref, vp_ref, vc_ref, o_ref, do_ref, sink_ref, dq_ref, dk_ref, dv_ref, ds_ref):
        n = pl.program_id(0)

        @pl.when(n == 0)
        def _():
            dk_ref[...] = jnp.zeros_like(dk_ref)
            dv_ref[...] = jnp.zeros_like(dv_ref)
            ds_ref[...] = jnp.zeros_like(ds_ref)

        k = jnp.concatenate([kp_ref[...], kc_ref[...]], axis=0)
        v = jnp.concatenate([vp_ref[...], vc_ref[...]], axis=0)
        k_sw = pltpu.roll(k.astype(F32), HEAD_DIM, 1).astype(BF16)
        v_sw = pltpu.roll(v.astype(F32), HEAD_DIM, 1).astype(BF16)
        lane = lax.broadcasted_iota(jnp.int32, (1, LANES), 1)
        halves = [lane < HEAD_DIM, lane >= HEAD_DIM]
        valid, distf = _swa_masks(n)
        zero = jnp.zeros((2 * WINDOW, LANES), F32)
        dk_nat, dk_rot, dv_nat, dv_rot = zero, zero, zero, zero
        for pair in range(4):
            cols = slice(pair * LANES, (pair + 1) * LANES)
            qp = q_ref[:, cols]
            dop, op = do_ref[:, cols], o_ref[:, cols]
            dq = jnp.zeros((WINDOW, LANES), F32)
            for par in range(2):
                h = 2 * pair + par
                g = h // 4
                qm = jnp.where(halves[par], qp, jnp.zeros((), BF16))
                do = jnp.where(halves[par], dop, 0.0)
                dob = do.astype(BF16)
                delta = jnp.sum(do * op, axis=1, keepdims=True)
                ku, vu = (k, v) if g == par else (k_sw, v_sw)
                p, p_sink = _swa_probs(qm, ku, valid, distf, h, sink_ref[h])
                dsc = (p * (_dot_nt(dob, vu) - delta)).astype(BF16)
                ds_ref[h:h + 1, :] += jnp.zeros((1, LANES), F32) - jnp.sum(p_sink * delta)
                dq = dq + jnp.where(halves[par], _dot(dsc, ku), 0.0)
                dk_h = _dot_tn(dsc, qm)
                dv_h = _dot_tn(p.astype(BF16), dob)
                if g == par:
                    dk_nat, dv_nat = dk_nat + dk_h, dv_nat + dv_h
                else:
                    dk_rot, dv_rot = dk_rot + dk_h, dv_rot + dv_h
            dq_ref[:, cols] = dq
        dk = dk_nat + pltpu.roll(dk_rot, HEAD_DIM, 1)
        dv = dv_nat + pltpu.roll(dv_rot, HEAD_DIM, 1)
        prev = pl.multiple_of(jnp.maximum(n - 1, 0) * WINDOW, WINDOW)
        cur = pl.multiple_of(n * WINDOW, WINDOW)
        dk_ref[pl.ds(prev, WINDOW), :] += dk[:WINDOW]
        dv_ref[pl.ds(prev, WINDOW), :] += dv[:WINDOW]
        dk_ref[pl.ds(cur, WINDOW), :] += dk[WINDOW:]
        dv_ref[pl.ds(cur, WINDOW), :] += dv[WINDOW:]

    prev_blk = lambda n: jnp.maximum(n - 1, 0)
    wide = pl.BlockSpec((WINDOW, SWA_QW), lambda n: (n, 0))
    whole = pl.BlockSpec((s, LANES), lambda n: (0, 0))
    return pl.pallas_call(
        body, name=name, grid=(nb,),
        out_shape=[jax.ShapeDtypeStruct((s, SWA_QW), F32), jax.ShapeDtypeStruct((s, LANES), F32),
                   jax.ShapeDtypeStruct((s, LANES), F32), jax.ShapeDtypeStruct((8, LANES), F32)],
        in_specs=[pl.BlockSpec((WINDOW, SWA_QW), lambda n: (n, qb)),
                  pl.BlockSpec((WINDOW, LANES), lambda n: (prev_blk(n), kb)),
                  pl.BlockSpec((WINDOW, LANES), lambda n: (n, kb)),
                  pl.BlockSpec((WINDOW, LANES), lambda n: (prev_blk(n), vb)),
                  pl.BlockSpec((WINDOW, LANES), lambda n: (n, vb)),
                  wide,
                  pl.BlockSpec((WINDOW, SWA_QW), lambda n: (n, 1)),
                  pl.BlockSpec(memory_space=pltpu.SMEM)],
        out_specs=[wide, whole, whole, pl.BlockSpec((8, LANES), lambda n: (0, 0))],
        compiler_params=_cparams())(proj, proj, proj, proj, proj, y_sw, dy, sinks)


def _in_proj_backward(dq_sb, dk_sb, dv_sb, dq_sw, dk_sw, dv_sw, du1, x, vec, w_in, name):
    s = x.shape[0]
    tb = min(TOK_TILE, s)

    def body(dqsb_ref, dksb_ref, dvsb_ref, dqsw_ref, dksw_ref, dvsw_ref, du1_ref, x_ref, vec_ref, w_ref,
             dproj_ref, gx_ref, acc_ref, bacc_ref):
        @pl.when(pl.program_id(0) == 0)
        def _():
            acc_ref[...] = jnp.zeros_like(acc_ref)
            bacc_ref[...] = jnp.zeros_like(bacc_ref)

        pieces = ((0, dqsb_ref, QK_SCALE), (SB_W, dksb_ref, 1.0), (2 * SB_W, dvsb_ref, 1.0),
                  (3 * SB_W, dqsw_ref, QK_SCALE), (3 * SB_W + SWA_QW, dksw_ref, 1.0),
                  (3 * SB_W + SWA_QW + SWA_KW, dvsw_ref, 1.0))
        for lo, ref, scale in pieces:
            width = ref.shape[1]
            piece = ref[...] * scale
            bacc_ref[0:1, lo:lo + width] += _colsum(piece)
            dproj_ref[:, lo:lo + width] = piece.astype(BF16)
        dh = _dot_nt(dproj_ref[...], w_ref[...])
        xv = x_ref[...]
        gx_ref[...] = ALPHA * du1_ref[...] + dh * (1.0 + vec_ref[V_SC_A:V_SC_A + 1, :])
        acc_ref[C_SCA:C_SCA + 1, :] += _colsum(dh * xv)
        acc_ref[C_SHA:C_SHA + 1, :] += _colsum(dh)

    half = pl.BlockSpec((tb, SB_W), lambda i: (i, 0))
    narrow = pl.BlockSpec((tb, LANES), lambda i: (i, 0))
    full = pl.BlockSpec((tb, D), lambda i: (i, 0))
    return pl.pallas_call(
        body, name=name, grid=(s // tb,),
        out_shape=[jax.ShapeDtypeStruct((s, D_IN), BF16), jax.ShapeDtypeStruct((s, D), F32),
                   jax.ShapeDtypeStruct((8, D), F32), jax.ShapeDtypeStruct((8, D_IN), F32)],
        in_specs=[half, half, half, half, narrow, narrow, full, full, _resident((VEC_ROWS, D)), _resident((D, D_IN))],
        out_specs=[pl.BlockSpec((tb, D_IN), lambda i: (i, 0)), full, pl.BlockSpec((8, D), lambda i: (0, 0)),
                   pl.BlockSpec((8, D_IN), lambda i: (0, 0))],
        compiler_params=_cparams())(dq_sb, dk_sb, dv_sb, dq_sw, dk_sw, dv_sw, du1, x, vec, w_in)


def _weight_grad(a, b, name):
    s, m = a.shape
    n = b.shape[1]
    tn = 512 if n % 512 == 0 else n
    ts = min(512, s)

    def body(a_ref, b_ref, o_ref):
        @pl.when(pl.program_id(1) == 0)
        def _():
            o_ref[...] = jnp.zeros_like(o_ref)

        o_ref[...] += _dot_tn(a_ref[...], b_ref[...])

    return pl.pallas_call(
        body, name=name, grid=(n // tn, s // ts),
        out_shape=jax.ShapeDtypeStruct((m, n), F32),
        in_specs=[pl.BlockSpec((ts, m), lambda j, k: (k, 0)), pl.BlockSpec((ts, tn), lambda j, k: (k, j))],
        out_specs=pl.BlockSpec((m, tn), lambda j, k: (0, j)),
        compiler_params=_cparams())(a, b)


def _pad_rows(v, rows):
    return jnp.concatenate([v, jnp.zeros((rows - v.shape[0], v.shape[1]), v.dtype)], axis=0)


def _col_shards(w, n_shards):
    r, n = w.shape
    return w.reshape(r, n_shards, n // n_shards).transpose(1, 0, 2)


def kernel(x, c, w_ada, b_ada, w_in, b_in, sinks, gn_sb, gn_swa, w_out, ln1_g, ln1_b, w_gu, w_down, ln2_g, ln2_b, loss_target, m_w_ada, m_b_ada, m_w_in, m_b_in, m_sinks, m_gn_sb, m_gn_swa, m_w_out, m_ln1_g, m_ln1_b, m_w_gu, m_w_down, m_ln2_g, m_ln2_b, v_w_ada, v_b_ada, v_w_in, v_b_in, v_sinks, v_gn_sb, v_gn_swa, v_w_out, v_ln1_g, v_ln1_b, v_w_gu, v_w_down, v_ln2_g, v_ln2_b):
    ix, iy, ic = lax.axis_index("x"), lax.axis_index("y"), lax.axis_index("c")
    chip = 2 * ix + iy
    dev = 4 * ix + 2 * iy + ic
    xs, target = x[0], loss_target[0]
    s = xs.shape[0]

    c_all = _allgather8(_pad_rows(c, 8), "gather_c")[::8]
    n_ada = w_ada.shape[2]
    b_ada_shard = lax.dynamic_slice_in_dim(b_ada, chip * n_ada, n_ada, axis=1)
    mod_cols, silu_c = _mod_shard(c_all, w_ada[0], b_ada_shard, "mod_shard")
    mod_all = _allgather8(mod_cols, "gather_mod").reshape(N_DEV, 8, n_ada)
    mod_mine = lax.dynamic_index_in_dim(mod_all, dev, axis=1, keepdims=False)
    mod = mod_mine.reshape(N_CHIPS, 2, n_ada)[:, 0].reshape(6, D)
    vec = jnp.concatenate([mod, ln1_g, ln1_b, ln2_g, ln2_b, jnp.concatenate([gn_sb, gn_swa], axis=1),
                           jnp.zeros((VEC_ROWS - 11, D), F32)], axis=0)

    g_in, g_out, g_gu, g_down = _chip_allgather(
        [w_in[0].astype(BF16), w_out[0].astype(BF16), w_gu[0].astype(BF16), w_down[0].astype(BF16)], "gather_weights")
    w_in_b = g_in.transpose(1, 0, 2).reshape(D, D_IN)
    w_gu_b = g_gu.transpose(1, 0, 2).reshape(D, 2 * D_FF)
    w_out_b = g_out.reshape(D, D)
    w_down_b = g_down.reshape(D_FF, D)

    h_b, proj = _in_proj(xs, vec, w_in_b, b_in, "in_proj")
    y_sb, sp_total = _sb_forward(proj, "sb_forward")
    sink_vec = sinks[0]
    y_sw = _swa_forward(proj, sink_vec, "swa_forward")
    mixed_b, attn, x1, h2_b = _post_attention(y_sb, y_sw, xs, vec, w_out_b, "post_attention")
    gu, act_b, ffn = _ffn_forward(h2_b, w_gu_b, w_down_b, "ffn_forward")

    dffn_b, dgu_b, dx1, acc_f = _ffn_backward(x1, ffn, target, gu, vec, w_gu_b, w_down_b, "ffn_backward")
    du1, dattn_b, dy, acc_a = _attn_out_backward(dx1, xs, attn, y_sb, y_sw, vec, w_out_b, "attn_out_backward")
    dq_sb, dk_sb, dv_sb = _sb_backward(proj, sp_total, dy, "sb_backward")
    dq_sw, dk_sw, dv_sw, dsink = _swa_backward(proj, y_sw, dy, sink_vec, "swa_backward")
    dproj_b, grad_x, acc_i, acc_b = _in_proj_backward(dq_sb, dk_sb, dv_sb, dq_sw, dk_sw, dv_sw, du1, xs, vec, w_in_b,
                                                      "in_proj_backward")
    dw_in = _weight_grad(h_b, dproj_b, "grad_w_in")
    dw_out = _weight_grad(mixed_b, dattn_b, "grad_w_out")
    dw_gu = _weight_grad(h2_b, dgu_b, "grad_w_gu")
    dw_down = _weight_grad(act_b, dffn_b, "grad_w_down")

    shard_sizes = [D * (D_IN // 4), (D // 4) * D, D * (2 * D_FF // 4), (D_FF // 4) * D]
    flat = jnp.concatenate([_col_shards(dw_in, 4).reshape(4, -1), dw_out.reshape(4, -1),
                            _col_shards(dw_gu, 4).reshape(4, -1), dw_down.reshape(4, -1)], axis=1)
    half_rows = flat.shape[1] // (2 * LANES)
    halves = flat.reshape(4, 2, half_rows, LANES)
    keep = lax.dynamic_index_in_dim(halves, ic, axis=1, keepdims=False)
    give = lax.dynamic_index_in_dim(halves, 1 - ic, axis=1, keepdims=False)
    got = _sibling_send(give, "grad_halves_swap")
    chip_sum = _add2(keep.reshape(4 * half_rows, LANES), got.reshape(4 * half_rows, LANES), "grad_chip_sum")
    parts = _chip_scatter(chip_sum.reshape(4, half_rows, LANES), "grad_chip_scatter")
    my_half = _sum4(parts, "grad_reduce")
    other_half = _sibling_send(my_half, "grad_half_return")
    first = jnp.where(ic == 0, my_half, other_half)
    second = jnp.where(ic == 0, other_half, my_half)
    shard_flat = jnp.concatenate([first, second], axis=0).reshape(-1)
    offs = [0]
    for n in shard_sizes:
        offs.append(offs[-1] + n)
    gw_in = shard_flat[offs[0]:offs[1]].reshape(D, D_IN // 4)
    gw_out = shard_flat[offs[1]:offs[2]].reshape(D // 4, D)
    gw_gu = shard_flat[offs[2]:offs[3]].reshape(D, 2 * D_FF // 4)
    gw_down = shard_flat[offs[3]:offs[4]].reshape(D_FF // 4, D)

    dmod = jnp.concatenate([acc_i[C_SHA:C_SHA + 1], acc_i[C_SCA:C_SCA + 1], acc_a[B_GA:B_GA + 1],
                            acc_f[A_SHF:A_SHF + 1], acc_f[A_SCF:A_SCF + 1], acc_f[A_GF:A_GF + 1]], axis=1)
    dsink_row = jnp.concatenate([dsink[:, 0].reshape(1, 8), jnp.zeros((1, LANES - 8), F32)], axis=1)
    loss_row = jnp.concatenate([jnp.sum(acc_f[A_LOSS:A_LOSS + 1], axis=1, keepdims=True),
                                jnp.zeros((1, LANES - 1), F32)], axis=1)
    small = jnp.concatenate([dmod, acc_b[0:1], acc_a[B_LN1G:B_LN1G + 1], acc_a[B_LN1B:B_LN1B + 1],
                             acc_f[A_LN2G:A_LN2G + 1], acc_f[A_LN2B:A_LN2B + 1], acc_a[B_GN:B_GN + 1],
                             dsink_row, loss_row], axis=1)
    small_all = _allgather8(_pad_rows(small, 8), "gather_small")[::8]

    def pack_small(b_ada_, b_in_, ln1g_, ln1b_, ln2g_, ln2b_, gsb_, gsw_, sinks_):
        return jnp.concatenate([b_ada_, b_in_, ln1g_, ln1b_, ln2g_, ln2b_, gsb_, gsw_, sinks_,
                                jnp.ones((1, 2 * LANES - 8), F32)], axis=1)

    w_small = pack_small(b_ada, b_in, ln1_g, ln1_b, ln2_g, ln2_b, gn_sb, gn_swa, sinks)
    m_small = pack_small(m_b_ada, m_b_in, m_ln1_g, m_ln1_b, m_ln2_g, m_ln2_b, m_gn_sb, m_gn_swa, m_sinks)
    v_small = pack_small(v_b_ada, v_b_in, v_ln1_g, v_ln1_b, v_ln2_g, v_ln2_b, v_gn_sb, v_gn_swa, v_sinks)
    small_out = _small_update(small_all, w_small, m_small, v_small, "small_update")

    def unpack_small(row):
        return {"b_ada": row[:, SM_MOD:SM_BIN], "b_in": row[:, SM_BIN:SM_LN1G], "ln1_g": row[:, SM_LN1G:SM_LN1B],
                "ln1_b": row[:, SM_LN1B:SM_LN2G], "ln2_g": row[:, SM_LN2G:SM_LN2B], "ln2_b": row[:, SM_LN2B:SM_GN],
                "gn_sb": row[:, SM_GN:SM_GN + SB_W], "gn_swa": row[:, SM_GN + SB_W:SM_SINK],
                "sinks": row[:, SM_SINK:SM_SINK + 8]}

    g_small, d_small, m2_small, v2_small = [unpack_small(r) for r in small_out]
    loss = small_out[0][0, SM_LOSS]

    dmod_cols = lax.dynamic_slice_in_dim(small_all[:, SM_MOD:SM_BIN], chip * n_ada, n_ada, axis=1)
    gw_ada = _weight_grad(_pad_rows(silu_c, LANES).astype(BF16), _pad_rows(dmod_cols, LANES).astype(BF16), "grad_w_ada")

    big = {}
    for nm, w, g, m, v in (("w_ada", w_ada, gw_ada, m_w_ada, v_w_ada), ("w_in", w_in, gw_in, m_w_in, v_w_in),
                           ("w_out", w_out, gw_out, m_w_out, v_w_out), ("w_gu", w_gu, gw_gu, m_w_gu, v_w_gu),
                           ("w_down", w_down, gw_down, m_w_down, v_w_down)):
        d_, m2_, v2_ = _adamw(w[0], g, m[0], v[0], "adamw_" + nm)
        big[nm] = (g[None], d_[None], m2_[None], v2_[None])

    order = ["w_ada", "b_ada", "w_in", "b_in", "sinks", "gn_sb", "gn_swa", "w_out", "ln1_g", "ln1_b", "w_gu", "w_down",
             "ln2_g", "ln2_b"]

    def leaf(nm, which):
        if nm in big:
            return big[nm][which]
        return (g_small, d_small, m2_small, v2_small)[which][nm]

    outs = [loss, grad_x[None]]
    for which in range(4):
        outs += [leaf(nm, which) for nm in order]
    return tuple(outs)
```

```python
import functools
import math

import jax
import jax.numpy as jnp
from jax import lax
from jax.experimental import pallas as pl
from jax.experimental.pallas import tpu as pltpu

F32 = jnp.float32
BF16 = jnp.bfloat16

D = 1024
HEAD_DIM = 64
SB_W = 512
SWA_QW = 512
SWA_KW = 128
D_IN = 2304
D_FF = 2816
WINDOW = 128
ALPHA = 2.0 ** 0.25
LN_EPS = 1e-5
RMS_EPS = 1e-6
MASK_VALUE = -1e30
QK_SCALE = 1.0 / math.sqrt(HEAD_DIM)

ADAM_LR = 0.001
ADAM_B1 = 0.9
ADAM_B2 = 0.999
ADAM_EPS = 1e-08
ADAM_WD = 0.01
ADAM_STEP = 10

N_CHIPS = 4
N_DEV = 8
LANES = 128

SB_TQ = 512
SB_TK = 256
SB_UNROLL = 2
SB_UNROLL_BWD = 2
TOK_TILE = 512
FFN_TILE = 256
FFN_BWD_TILE = 256
VMEM_LIMIT = 56 * 1024 * 1024

V_SH_A, V_SC_A, V_G_A, V_SH_F, V_SC_F, V_G_F, V_LN1G, V_LN1B, V_LN2G, V_LN2B, V_GN = range(11)
VEC_ROWS = 16

SM_MOD = 0
SM_BIN = 6 * D
SM_LN1G = SM_BIN + D_IN
SM_LN1B = SM_LN1G + D
SM_LN2G = SM_LN1B + D
SM_LN2B = SM_LN2G + D
SM_GN = SM_LN2B + D
SM_SINK = SM_GN + D
SM_LOSS = SM_SINK + LANES
SM_LEN = SM_LOSS + LANES

MESH = pl.DeviceIdType.MESH


def _cparams(**kw):
    return pltpu.CompilerParams(vmem_limit_bytes=VMEM_LIMIT, **kw)


def _resident(shape):
    nd = len(shape)
    return pl.BlockSpec(shape, lambda *_: (0,) * nd, pipeline_mode=pl.Buffered(1))


def _dot(a, b):
    return jnp.dot(a, b, preferred_element_type=F32)


def _dot_nt(a, b):
    return lax.dot_general(a, b, (((1,), (1,)), ((), ())), preferred_element_type=F32)


def _dot_tn(a, b):
    return lax.dot_general(a, b, (((0,), (0,)), ((), ())), preferred_element_type=F32)


def _block_sums(x, m):
    return _dot(x.astype(BF16), m)


def _allgather8(v, name):
    m_per, n = v.shape

    def body(x_ref, out_ref, send_sems, recv_sems, local_sem):
        x, y, c = lax.axis_index("x"), lax.axis_index("y"), lax.axis_index("c")
        me, sibling = (x, y, c), (x, y, 1 - c)
        chips = [(1 - x, y), (x, 1 - y), (1 - x, 1 - y)]

        def rows(px, py, pc):
            return out_ref.at[pl.ds((4 * px + 2 * py + pc) * m_per, m_per), :]

        def copy(k, block, to, src=None):
            return pltpu.make_async_remote_copy(
                src_ref=rows(*block) if src is None else src, dst_ref=rows(*block),
                send_sem=send_sems.at[k], recv_sem=recv_sems.at[k], device_id=to, device_id_type=MESH)

        mine = pltpu.make_async_copy(x_ref, rows(*me), local_sem)
        mine.start()
        first = [copy(0, me, sibling, src=x_ref)]
        first += [copy(1 + j, me, (*chip, c), src=x_ref) for j, chip in enumerate(chips)]
        for cp in first:
            cp.start()
        passed = [copy(4 + j, (*chip, c), sibling) for j, chip in enumerate(chips)]
        for j, chip in enumerate(chips):
            copy(1 + j, (*chip, c), me).wait_recv()
            passed[j].start()
        copy(0, sibling, me).wait_recv()
        for j, chip in enumerate(chips):
            copy(4 + j, (*chip, 1 - c), me).wait_recv()
        for cp in first + passed:
            cp.wait_send()
        mine.wait()

    return pl.pallas_call(
        body, name=name,
        out_shape=jax.ShapeDtypeStruct((N_DEV * m_per, n), v.dtype),
        in_specs=[pl.BlockSpec(memory_space=pltpu.VMEM)],
        out_specs=pl.BlockSpec(memory_space=pltpu.VMEM),
        scratch_shapes=[pltpu.SemaphoreType.DMA((7,)), pltpu.SemaphoreType.DMA((7,)), pltpu.SemaphoreType.DMA],
        compiler_params=_cparams(),
    )(v)


class _Exchange:
    def __init__(self, local, sends, arrivals):
        self.local, self.sends, self.arrivals = local, sends, arrivals

    def start(self):
        for cp in self.local + self.sends:
            cp.start()

    def wait(self):
        for cp in self.arrivals:
            cp.wait_recv()
        for cp in self.sends:
            cp.wait_send()
        for cp in self.local:
            cp.wait()


def _exchange_sems(n):
    return [pltpu.SemaphoreType.DMA((3 * n,)), pltpu.SemaphoreType.DMA((3 * n,)), pltpu.SemaphoreType.DMA((n,))]


def _gather_exchange(ins, outs, send_sems, recv_sems, local_sems):
    x, y, c = lax.axis_index("x"), lax.axis_index("y"), lax.axis_index("c")
    slot = 2 * x + y
    chips = [(1 - x, y), (x, 1 - y), (1 - x, 1 - y)]
    local, sends, arrivals = [], [], []
    for a in range(len(ins)):
        local.append(pltpu.make_async_copy(ins[a], outs[a].at[slot], local_sems.at[a]))
        for j, (px, py) in enumerate(chips):
            sems = dict(send_sem=send_sems.at[3 * a + j], recv_sem=recv_sems.at[3 * a + j],
                        device_id=(px, py, c), device_id_type=MESH)
            sends.append(pltpu.make_async_remote_copy(src_ref=ins[a], dst_ref=outs[a].at[slot], **sems))
            arrivals.append(pltpu.make_async_remote_copy(src_ref=ins[a], dst_ref=outs[a].at[2 * px + py], **sems))
    return _Exchange(local, sends, arrivals)


def _scatter_exchange(p_ref, out_ref, send_sems, recv_sems, local_sems):
    x, y, c = lax.axis_index("x"), lax.axis_index("y"), lax.axis_index("c")
    slot = 2 * x + y
    chips = [(1 - x, y), (x, 1 - y), (1 - x, 1 - y)]
    local = [pltpu.make_async_copy(p_ref.at[slot], out_ref.at[slot], local_sems.at[0])]
    sends, arrivals = [], []
    for j, (px, py) in enumerate(chips):
        sems = dict(send_sem=send_sems.at[j], recv_sem=recv_sems.at[j], device_id=(px, py, c), device_id_type=MESH)
        sends.append(pltpu.make_async_remote_copy(src_ref=p_ref.at[2 * px + py], dst_ref=out_ref.at[slot], **sems))
        arrivals.append(pltpu.make_async_remote_copy(src_ref=p_ref.at[slot], dst_ref=out_ref.at[2 * px + py], **sems))
    return _Exchange(local, sends, arrivals)


def _chip_allgather(arrs, name):
    n = len(arrs)

    def body(*refs):
        ex = _gather_exchange(refs[:n], refs[n:2 * n], *refs[2 * n:])
        ex.start()
        ex.wait()

    hbm = pl.BlockSpec(memory_space=pl.ANY)
    return pl.pallas_call(
        body, name=name,
        out_shape=[jax.ShapeDtypeStruct((N_CHIPS,) + a.shape, a.dtype) for a in arrs],
        in_specs=[hbm] * n, out_specs=[hbm] * n,
        scratch_shapes=_exchange_sems(n),
        compiler_params=_cparams(),
    )(*arrs)


def _sibling_send(v, name):
    def body(v_ref, out_ref, send_sem, recv_sem):
        x, y, c = lax.axis_index("x"), lax.axis_index("y"), lax.axis_index("c")
        cp = pltpu.make_async_remote_copy(src_ref=v_ref, dst_ref=out_ref, send_sem=send_sem, recv_sem=recv_sem,
                                          device_id=(x, y, 1 - c), device_id_type=MESH)
        cp.start()
        cp.wait()

    hbm = pl.BlockSpec(memory_space=pl.ANY)
    return pl.pallas_call(
        body, name=name, out_shape=jax.ShapeDtypeStruct(v.shape, v.dtype),
        in_specs=[hbm], out_specs=hbm,
        scratch_shapes=[pltpu.SemaphoreType.DMA, pltpu.SemaphoreType.DMA],
        compiler_params=_cparams(),
    )(v)


def _chip_scatter(p, name):
    def body(p_ref, out_ref, send_sems, recv_sems, local_sems):
        ex = _scatter_exchange(p_ref, out_ref, send_sems, recv_sems, local_sems)
        ex.start()
        ex.wait()

    hbm = pl.BlockSpec(memory_space=pl.ANY)
    return pl.pallas_call(
        body, name=name, out_shape=jax.ShapeDtypeStruct(p.shape, p.dtype),
        in_specs=[hbm], out_specs=hbm,
        scratch_shapes=_exchange_sems(1),
        compiler_params=_cparams(),
    )(p)


def _add2(a, b, name):
    rows = a.shape[0]
    tr = rows // 8

    def body(a_ref, b_ref, o_ref):
        o_ref[...] = a_ref[...] + b_ref[...]

    spec = pl.BlockSpec((tr, LANES), lambda i: (i, 0))
    return pl.pallas_call(body, name=name, grid=(rows // tr,), out_shape=jax.ShapeDtypeStruct(a.shape, a.dtype),
                          in_specs=[spec, spec], out_specs=spec, compiler_params=_cparams())(a, b)


def _sum4(p, name):
    rows = p.shape[1]
    tr = rows // 8

    def body(p_ref, o_ref):
        o_ref[...] = ((p_ref[0] + p_ref[1]) + p_ref[2]) + p_ref[3]

    return pl.pallas_call(
        body, name=name, grid=(rows // tr,), out_shape=jax.ShapeDtypeStruct((rows, LANES), p.dtype),
        in_specs=[pl.BlockSpec((4, tr, LANES), lambda i: (0, i, 0))],
        out_specs=pl.BlockSpec((tr, LANES), lambda i: (i, 0)), compiler_params=_cparams())(p)


def _adam_math(w, g, m, v):
    m2 = ADAM_B1 * m + (1.0 - ADAM_B1) * g
    v2 = ADAM_B2 * v + (1.0 - ADAM_B2) * (g * g)
    m_hat = m2 / (1.0 - ADAM_B1 ** ADAM_STEP)
    v_hat = v2 / (1.0 - ADAM_B2 ** ADAM_STEP)
    delta = -ADAM_LR * (m_hat / (jnp.sqrt(v_hat) + ADAM_EPS) + ADAM_WD * w)
    return delta, m2, v2


def _adamw(w, g, m, v, name):
    rows, cols = w.shape
    tr = rows // 4 if rows % 32 == 0 else rows

    def body(w_ref, g_ref, m_ref, v_ref, d_ref, m2_ref, v2_ref):
        delta, m2, v2 = _adam_math(w_ref[...], g_ref[...], m_ref[...], v_ref[...])
        d_ref[...] = delta
        m2_ref[...] = m2
        v2_ref[...] = v2

    spec = pl.BlockSpec((tr, cols), lambda i: (i, 0))
    shp = jax.ShapeDtypeStruct(w.shape, F32)
    return pl.pallas_call(body, name=name, grid=(rows // tr,), out_shape=[shp, shp, shp],
                          in_specs=[spec] * 4, out_specs=[spec] * 3, compiler_params=_cparams())(w, g, m, v)


def _small_update(g8, w, m, v, name):
    n = w.shape[1]

    def body(g8_ref, w_ref, m_ref, v_ref, g_ref, d_ref, m2_ref, v2_ref):
        g = g8_ref[0:1, :]
        for r in range(1, N_DEV):
            g = g + g8_ref[r:r + 1, :]
        delta, m2, v2 = _adam_math(w_ref[...], g, m_ref[...], v_ref[...])
        g_ref[...] = g
        d_ref[...] = delta
        m2_ref[...] = m2
        v2_ref[...] = v2

    shp = jax.ShapeDtypeStruct((1, n), F32)
    vm = pl.BlockSpec(memory_space=pltpu.VMEM)
    return pl.pallas_call(body, name=name, out_shape=[shp] * 4, in_specs=[vm] * 4, out_specs=[vm] * 4,
                          compiler_params=_cparams())(g8, w, m, v)


def _mod_shard(c8, w_ada, b_ada_shard, name):
    n = w_ada.shape[1]
    tn = 512

    def body(c_ref, w_ref, b_ref, o_ref, s_ref):
        cv = c_ref[...]
        sc = cv * (1.0 / (1.0 + jnp.exp(-cv)))
        s_ref[...] = sc
        o_ref[...] = _dot(sc.astype(BF16), w_ref[...].astype(BF16)) + b_ref[...]

    return pl.pallas_call(
        body, name=name, grid=(n // tn,),
        out_shape=[jax.ShapeDtypeStruct((8, n), F32), jax.ShapeDtypeStruct((8, D), F32)],
        in_specs=[pl.BlockSpec((8, D), lambda j: (0, 0)), pl.BlockSpec((D, tn), lambda j: (0, j)),
                  pl.BlockSpec((1, tn), lambda j: (0, j))],
        out_specs=[pl.BlockSpec((8, tn), lambda j: (0, j)), pl.BlockSpec((8, D), lambda j: (0, 0))],
        compiler_params=_cparams())(c8, w_ada, b_ada_shard)


def _layer_norm_stats(u):
    mu = jnp.mean(u, axis=1, keepdims=True)
    d = u - mu
    var = jnp.mean(d * d, axis=1, keepdims=True)
    rstd = lax.rsqrt(var + LN_EPS)
    return d * rstd, rstd


def _in_proj(x, vec, w_in, b_in, name):
    s = x.shape[0]
    tb = min(TOK_TILE, s)

    def body(x_ref, vec_ref, w_ref, b_ref, h_ref, p_ref):
        h = x_ref[...] * (1.0 + vec_ref[V_SC_A:V_SC_A + 1, :]) + vec_ref[V_SH_A:V_SH_A + 1, :]
        hb = h.astype(BF16)
        h_ref[...] = hb
        proj = _dot(hb, w_ref[...]) + b_ref[...]
        col = lax.broadcasted_iota(jnp.int32, (1, D_IN), 1)
        is_q = (col < SB_W) | ((col >= 3 * SB_W) & (col < 3 * SB_W + SWA_QW))
        p_ref[...] = (proj * jnp.where(is_q, QK_SCALE, 1.0)).astype(BF16)

    return pl.pallas_call(
        body, name=name, grid=(s // tb,),
        out_shape=[jax.ShapeDtypeStruct((s, D), BF16), jax.ShapeDtypeStruct((s, D_IN), BF16)],
        in_specs=[pl.BlockSpec((tb, D), lambda i: (i, 0)), _resident((VEC_ROWS, D)), _resident((D, D_IN)),
                  _resident((1, D_IN))],
        out_specs=[pl.BlockSpec((tb, D), lambda i: (i, 0)), pl.BlockSpec((tb, D_IN), lambda i: (i, 0))],
        compiler_params=_cparams())(x, vec, w_in, b_in)


def _softplus_parts(z):
    e1 = jnp.exp(-jnp.abs(z))
    sp = jnp.maximum(z, 0.0) + jnp.log(1.0 + e1)
    return sp, e1


def _sb_forward(proj, shards, name):
    s = proj.shape[0]
    tq, tk = min(SB_TQ, s), min(SB_TK, s)
    r = tq // tk

    n_sh = len(shards)
    nkb = SB_W // LANES
    nq = s // tq

    def body(q_ref, k_ref, v_ref, *refs):
        sh_refs, (o_ref, tot_ref), got_refs = refs[:n_sh], refs[n_sh:n_sh + 2], refs[n_sh + 2:2 * n_sh + 2]
        acc_refs, run_refs = refs[2 * n_sh + 2:2 * n_sh + 4]
        i = pl.program_id(1)
        step = pl.program_id(0) * nq + i
        gather = _gather_exchange(sh_refs, got_refs, *refs[2 * n_sh + 4:])

        @pl.when(step == 0)
        def _():
            gather.start()

        lane = lax.broadcasted_iota(jnp.int32, (1, LANES), 1)
        first = lane < HEAD_DIM
        qp = q_ref[...]
        zero = jnp.zeros((), BF16)
        qs = (jnp.where(first, qp, zero), jnp.where(first, zero, qp))
        row = lax.broadcasted_iota(jnp.int32, (tk, tk), 0)
        col = lax.broadcasted_iota(jnp.int32, (tk, tk), 1)
        later = (row > col).astype(BF16)
        acc_refs[...] = jnp.zeros_like(acc_refs)
        run_refs[...] = jnp.zeros_like(run_refs)

        def blocks(js, masked):
            ks = [pl.multiple_of(j * tk, tk) for j in js]
            kjs = [k_ref[pl.ds(k0, tk), :] for k0 in ks]
            vjs = [v_ref[pl.ds(k0, tk), :] for k0 in ks]
            chains = [(hd, b) for b in range(len(js)) for hd in range(2)]
            zs = [_dot_nt(qs[hd], kjs[b]) for hd, b in chains]
            sps = [_softplus_parts(z)[0] for z in zs]
            if masked:
                t_idx = i * tq + lax.broadcasted_iota(jnp.int32, (tq, tk), 0)
                befores = [j * tk + lax.broadcasted_iota(jnp.int32, (tq, tk), 1) < t_idx for j in js]
                spms = [jnp.where(befores[b], sp, 0.0) for (hd, b), sp in zip(chains, sps)]
            else:
                spms = sps
            cums = [_block_sums(spm, later) for spm in spms]
            sums = [jnp.sum(spm, axis=1, keepdims=True) for spm in spms]
            runs = [run_refs[0], run_refs[1]]
            ws = []
            for (hd, b), z, sp, cum, sm in zip(chains, zs, sps, cums, sums):
                w = jnp.exp(z - sp - cum - runs[hd])
                if masked:
                    w = jnp.where(befores[b], w, 0.0)
                ws.append(w.astype(BF16))
                runs[hd] = runs[hd] + sm
            pvs = [_dot(w, vjs[b]) for (hd, b), w in zip(chains, ws)]
            for hd in range(2):
                tot = pvs[hd]
                for b in range(1, len(js)):
                    tot = tot + pvs[2 * b + hd]
                acc_refs[hd] += tot
                run_refs[hd] = runs[hd]

        blocks([i * r + (r - 1 - d) for d in range(r)], True)

        below = i * r

        def sweep(n, carry):
            top = below - 1 - SB_UNROLL * n
            blocks([top - u for u in range(SB_UNROLL)], False)
            return carry

        lax.fori_loop(0, below // SB_UNROLL, sweep, 0)
        for u in range(1, SB_UNROLL):
            @pl.when(below % SB_UNROLL >= u)
            def _(u=u):
                blocks([below % SB_UNROLL - u], False)
        o_ref[...] = jnp.where(first, acc_refs[0], acc_refs[1])
        tot_ref[...] = jnp.where(first, run_refs[0], run_refs[1])

        @pl.when(step == nkb * nq - 1)
        def _():
            gather.wait()

    shp = jax.ShapeDtypeStruct((s, SB_W), F32)
    qspec = pl.BlockSpec((tq, LANES), lambda p, i: (i, p))
    hbm = pl.BlockSpec(memory_space=pl.ANY)
    return pl.pallas_call(
        body, name=name, grid=(nkb, nq),
        out_shape=[shp, shp] + [jax.ShapeDtypeStruct((N_CHIPS,) + a.shape, a.dtype) for a in shards],
        in_specs=[qspec,
                  pl.BlockSpec((s, LANES), lambda p, i: (0, nkb + p)),
                  pl.BlockSpec((s, LANES), lambda p, i: (0, 2 * nkb + p))] + [hbm] * n_sh,
        out_specs=[qspec, qspec] + [hbm] * n_sh,
        scratch_shapes=[pltpu.VMEM((2, tq, LANES), F32), pltpu.VMEM((2, tq, 1), F32)] + _exchange_sems(n_sh),
        compiler_params=_cparams())(proj, proj, proj, *shards)


def _swa_masks(n):
    ti = lax.broadcasted_iota(jnp.int32, (WINDOW, 2 * WINDOW), 0)
    kj = lax.broadcasted_iota(jnp.int32, (WINDOW, 2 * WINDOW), 1)
    dist = ti + WINDOW - kj
    valid = (dist >= 0) & (dist < WINDOW) & ((n * WINDOW - WINDOW + kj) >= 0)
    return valid, dist.astype(F32)


def _swa_probs(qm, ku, valid, distf, h, sink):
    slope = 2.0 ** (-(h + 1))
    sc = _dot_nt(qm, ku)
    sc = jnp.where(valid, sc - slope * distf, MASK_VALUE)
    mx = jnp.maximum(jnp.max(sc, axis=1, keepdims=True), sink)
    p = jnp.exp(sc - mx)
    es = jnp.exp(sink - mx)
    inv = 1.0 / (jnp.sum(p, axis=1, keepdims=True) + es)
    return p * inv, es * inv


def _swa_forward(proj, sinks, name):
    s = proj.shape[0]
    nb = s // WINDOW
    qb, kb, vb = 3 * SB_W // SWA_QW, (3 * SB_W + SWA_QW) // LANES, (3 * SB_W + SWA_QW + SWA_KW) // LANES

    def body(q_ref, kp_ref, kc_ref, vp_ref, vc_ref, sink_ref, o_ref):
        n = pl.program_id(0)
        k = jnp.concatenate([kp_ref[...], kc_ref[...]], axis=0)
        v = jnp.concatenate([vp_ref[...], vc_ref[...]], axis=0)
        k_sw = pltpu.roll(k.astype(F32), HEAD_DIM, 1).astype(BF16)
        v_sw = pltpu.roll(v.astype(F32), HEAD_DIM, 1).astype(BF16)
        lane = lax.broadcasted_iota(jnp.int32, (1, LANES), 1)
        halves = [lane < HEAD_DIM, lane >= HEAD_DIM]
        valid, distf = _swa_masks(n)
        for pair in range(4):
            qp = q_ref[:, pair * LANES:(pair + 1) * LANES]
            out = jnp.zeros((WINDOW, LANES), F32)
            for par in range(2):
                h = 2 * pair + par
                g = h // 4
                qm = jnp.where(halves[par], qp, jnp.zeros((), BF16))
                ku, vu = (k, v) if g == par else (k_sw, v_sw)
                p, _ = _swa_probs(qm, ku, valid, distf, h, sink_ref[h])
                out = out + jnp.where(halves[par], _dot(p.astype(BF16), vu), 0.0)
            o_ref[:, pair * LANES:(pair + 1) * LANES] = out

    prev = lambda n: jnp.maximum(n - 1, 0)
    return pl.pallas_call(
        body, name=name, grid=(nb,),
        out_shape=jax.ShapeDtypeStruct((s, SWA_QW), F32),
        in_specs=[pl.BlockSpec((WINDOW, SWA_QW), lambda n: (n, qb)),
                  pl.BlockSpec((WINDOW, LANES), lambda n: (prev(n), kb)),
                  pl.BlockSpec((WINDOW, LANES), lambda n: (n, kb)),
                  pl.BlockSpec((WINDOW, LANES), lambda n: (prev(n), vb)),
                  pl.BlockSpec((WINDOW, LANES), lambda n: (n, vb)),
                  pl.BlockSpec(memory_space=pltpu.SMEM)],
        out_specs=pl.BlockSpec((WINDOW, SWA_QW), lambda n: (n, 0)),
        compiler_params=_cparams())(proj, proj, proj, proj, proj, sinks)


def _rms_parts(y):
    return lax.rsqrt(jnp.mean(y * y, axis=1, keepdims=True) + RMS_EPS)


def _post_attention(y_sb, y_sw, x, vec, w_out, name):
    s = x.shape[0]
    tb = min(TOK_TILE, s)

    def body(ysb_ref, ysw_ref, x_ref, vec_ref, w_ref, mixed_ref, attn_ref, x1_ref, h2_ref):
        ysb, ysw = ysb_ref[...], ysw_ref[...]
        nsb = (ysb * _rms_parts(ysb) * vec_ref[V_GN:V_GN + 1, :SB_W]).astype(BF16)
        nsw = (ysw * _rms_parts(ysw) * vec_ref[V_GN:V_GN + 1, SB_W:]).astype(BF16)
        mixed_ref[:, :SB_W] = nsb
        mixed_ref[:, SB_W:] = nsw
        attn = _dot(nsb, w_ref[:SB_W, :]) + _dot(nsw, w_ref[SB_W:, :])
        attn_ref[...] = attn
        u1 = ALPHA * x_ref[...] + (1.0 + vec_ref[V_G_A:V_G_A + 1, :]) * attn
        xhat, _ = _layer_norm_stats(u1)
        x1 = xhat * vec_ref[V_LN1G:V_LN1G + 1, :] + vec_ref[V_LN1B:V_LN1B + 1, :]
        x1_ref[...] = x1
        h2_ref[...] = (x1 * (1.0 + vec_ref[V_SC_F:V_SC_F + 1, :]) + vec_ref[V_SH_F:V_SH_F + 1, :]).astype(BF16)

    half = pl.BlockSpec((tb, SB_W), lambda i: (i, 0))
    full = pl.BlockSpec((tb, D), lambda i: (i, 0))
    return pl.pallas_call(
        body, name=name, grid=(s // tb,),
        out_shape=[jax.ShapeDtypeStruct((s, D), BF16), jax.ShapeDtypeStruct((s, D), F32),
                   jax.ShapeDtypeStruct((s, D), F32), jax.ShapeDtypeStruct((s, D), BF16)],
        in_specs=[half, half, full, _resident((VEC_ROWS, D)), _resident((D, D))],
        out_specs=[full, full, full, full],
        compiler_params=_cparams())(y_sb, y_sw, x, vec, w_out)


def _ffn_forward(h2, w_gu, w_down, name):
    s = h2.shape[0]
    tb = min(FFN_TILE, s)

    def body(h_ref, wgu_ref, wd_ref, gu_ref, act_ref, ffn_ref):
        gu = _dot(h_ref[...], wgu_ref[...])
        gu_ref[...] = gu.astype(BF16)
        gate, up = gu[:, :D_FF], gu[:, D_FF:]
        act = (gate * (1.0 / (1.0 + jnp.exp(-gate))) * up).astype(BF16)
        act_ref[...] = act
        ffn_ref[...] = _dot(act, wd_ref[...])

    return pl.pallas_call(
        body, name=name, grid=(s // tb,),
        out_shape=[jax.ShapeDtypeStruct((s, 2 * D_FF), BF16), jax.ShapeDtypeStruct((s, D_FF), BF16),
                   jax.ShapeDtypeStruct((s, D), F32)],
        in_specs=[pl.BlockSpec((tb, D), lambda i: (i, 0)), _resident((D, 2 * D_FF)), _resident((D_FF, D))],
        out_specs=[pl.BlockSpec((tb, 2 * D_FF), lambda i: (i, 0)), pl.BlockSpec((tb, D_FF), lambda i: (i, 0)),
                   pl.BlockSpec((tb, D), lambda i: (i, 0))],
        compiler_params=_cparams())(h2, w_gu, w_down)


def _layer_norm_bwd(dxhat, xhat, rstd):
    m1 = jnp.mean(dxhat, axis=1, keepdims=True)
    m2 = jnp.mean(dxhat * xhat, axis=1, keepdims=True)
    return rstd * (dxhat - m1 - xhat * m2)


def _colsum(a):
    return jnp.sum(a, axis=0, keepdims=True)


A_LN2G, A_LN2B, A_GF, A_SCF, A_SHF, A_LOSS = range(6)
B_LN1G, B_LN1B, B_GA, B_GN = range(4)
C_SCA, C_SHA = range(2)


def _ffn_backward(x1, ffn, target, gu, vec, w_gu, w_down, name):
    s = x1.shape[0]
    tb = min(FFN_BWD_TILE, s)

    def body(x1_ref, ffn_ref, t_ref, gu_ref, vec_ref, wgu_ref, wd_ref, dffn_ref, dgu_ref, dx1_ref, acc_ref):
        @pl.when(pl.program_id(0) == 0)
        def _():
            acc_ref[...] = jnp.zeros_like(acc_ref)

        x1v, ffn_v = x1_ref[...], ffn_ref[...]
        g_f = 1.0 + vec_ref[V_G_F:V_G_F + 1, :]
        u2 = ALPHA * x1v + g_f * ffn_v
        xhat, rstd = _layer_norm_stats(u2)
        ln_g = vec_ref[V_LN2G:V_LN2G + 1, :]
        err = xhat * ln_g + vec_ref[V_LN2B:V_LN2B + 1, :] - t_ref[...]
        dx2 = err * (1.0 / D)
        acc_ref[A_LOSS:A_LOSS + 1, :] += _colsum(err * err) * (0.5 / D)
        acc_ref[A_LN2G:A_LN2G + 1, :] += _colsum(dx2 * xhat)
        acc_ref[A_LN2B:A_LN2B + 1, :] += _colsum(dx2)
        du2 = _layer_norm_bwd(dx2 * ln_g, xhat, rstd)
        acc_ref[A_GF:A_GF + 1, :] += _colsum(du2 * ffn_v)
        dffn = (g_f * du2).astype(BF16)
        dffn_ref[...] = dffn
        dact = _dot_nt(dffn, wd_ref[...])
        gate, up = gu_ref[:, :D_FF].astype(F32), gu_ref[:, D_FF:].astype(F32)
        sg = 1.0 / (1.0 + jnp.exp(-gate))
        dgate = (dact * up * (sg * (1.0 + gate * (1.0 - sg)))).astype(BF16)
        dup = (dact * (gate * sg)).astype(BF16)
        dgu_ref[:, :D_FF] = dgate
        dgu_ref[:, D_FF:] = dup
        dh2 = _dot_nt(dgate, wgu_ref[:, :D_FF]) + _dot_nt(dup, wgu_ref[:, D_FF:])
        dx1_ref[...] = ALPHA * du2 + dh2 * (1.0 + vec_ref[V_SC_F:V_SC_F + 1, :])
        acc_ref[A_SCF:A_SCF + 1, :] += _colsum(dh2 * x1v)
        acc_ref[A_SHF:A_SHF + 1, :] += _colsum(dh2)

    full = pl.BlockSpec((tb, D), lambda i: (i, 0))
    wide = pl.BlockSpec((tb, 2 * D_FF), lambda i: (i, 0))
    return pl.pallas_call(
        body, name=name, grid=(s // tb,),
        out_shape=[jax.ShapeDtypeStruct((s, D), BF16), jax.ShapeDtypeStruct((s, 2 * D_FF), BF16),
                   jax.ShapeDtypeStruct((s, D), F32), jax.ShapeDtypeStruct((8, D), F32)],
        in_specs=[full, full, full, wide, _resident((VEC_ROWS, D)), _resident((D, 2 * D_FF)), _resident((D_FF, D))],
        out_specs=[full, wide, full, pl.BlockSpec((8, D), lambda i: (0, 0))],
        compiler_params=_cparams())(x1, ffn, target, gu, vec, w_gu, w_down)


def _attn_out_backward(dx1, x, attn, y_sb, y_sw, vec, w_out, name):
    s = x.shape[0]
    tb = min(TOK_TILE, s)

    def body(dx1_ref, x_ref, attn_ref, ysb_ref, ysw_ref, vec_ref, w_ref, du1_ref, dattn_ref, dy_ref, acc_ref):
        @pl.when(pl.program_id(0) == 0)
        def _():
            acc_ref[...] = jnp.zeros_like(acc_ref)

        attn = attn_ref[...]
        g_a = 1.0 + vec_ref[V_G_A:V_G_A + 1, :]
        xhat, rstd = _layer_norm_stats(ALPHA * x_ref[...] + g_a * attn)
        dx1v = dx1_ref[...]
        acc_ref[B_LN1G:B_LN1G + 1, :] += _colsum(dx1v * xhat)
        acc_ref[B_LN1B:B_LN1B + 1, :] += _colsum(dx1v)
        du1 = _layer_norm_bwd(dx1v * vec_ref[V_LN1G:V_LN1G + 1, :], xhat, rstd)
        du1_ref[...] = du1
        acc_ref[B_GA:B_GA + 1, :] += _colsum(du1 * attn)
        dattn = (g_a * du1).astype(BF16)
        dattn_ref[...] = dattn
        dmixed = _dot_nt(dattn, w_ref[...])
        for lo, y_ref in ((0, ysb_ref), (SB_W, ysw_ref)):
            y = y_ref[...]
            rr = _rms_parts(y)
            dn = dmixed[:, lo:lo + SB_W]
            acc_ref[B_GN:B_GN + 1, lo:lo + SB_W] += _colsum(dn * y * rr)
            dng = dn * vec_ref[V_GN:V_GN + 1, lo:lo + SB_W]
            dy_ref[:, lo:lo + SB_W] = rr * dng - y * (rr * rr * rr) * jnp.mean(dng * y, axis=1, keepdims=True)

    half = pl.BlockSpec((tb, SB_W), lambda i: (i, 0))
    full = pl.BlockSpec((tb, D), lambda i: (i, 0))
    return pl.pallas_call(
        body, name=name, grid=(s // tb,),
        out_shape=[jax.ShapeDtypeStruct((s, D), F32), jax.ShapeDtypeStruct((s, D), BF16),
                   jax.ShapeDtypeStruct((s, D), F32), jax.ShapeDtypeStruct((8, D), F32)],
        in_specs=[full, full, full, half, half, _resident((VEC_ROWS, D)), _resident((D, D))],
        out_specs=[full, full, full, pl.BlockSpec((8, D), lambda i: (0, 0))],
        compiler_params=_cparams())(dx1, x, attn, y_sb, y_sw, vec, w_out)


def _sb_backward(proj, sp_total, dy, slabs, name):
    s = proj.shape[0]
    tq, tk = min(SB_TQ, s), min(SB_TK, s)
    r = tq // tk
    nkb = SB_W // LANES
    nq = s // tq

    def body(q_ref, k_ref, v_ref, tot_ref, do_ref, slab_ref, dq_ref, dk_ref, dv_ref, got_ref,
             dq_acc, left_refs, gsum_refs, send_sems, recv_sems, local_sems):
        i = pl.program_id(1)
        step = pl.program_id(0) * nq + i
        scatter = _scatter_exchange(slab_ref, got_ref, send_sems, recv_sems, local_sems)

        @pl.when(step == 0)
        def _():
            scatter.start()

        @pl.when(i == 0)
        def _():
            dk_ref[...] = jnp.zeros_like(dk_ref)
            dv_ref[...] = jnp.zeros_like(dv_ref)

        lane = lax.broadcasted_iota(jnp.int32, (1, LANES), 1)
        first = lane < HEAD_DIM
        qp, dop, totp = q_ref[...], do_ref[...], tot_ref[...]
        zero = jnp.zeros((), BF16)
        qs = (jnp.where(first, qp, zero), jnp.where(first, zero, qp))
        dobs = (jnp.where(first, dop, 0.0).astype(BF16), jnp.where(first, 0.0, dop).astype(BF16))
        row = lax.broadcasted_iota(jnp.int32, (tk, tk), 0)
        col = lax.broadcasted_iota(jnp.int32, (tk, tk), 1)
        later = (row > col).astype(BF16)
        earlier = (row < col).astype(BF16)
        dq_acc[...] = jnp.zeros_like(dq_acc)
        gsum_refs[...] = jnp.zeros_like(gsum_refs)
        left_refs[0] = jnp.max(jnp.where(first, totp, 0.0), axis=1, keepdims=True)
        left_refs[1] = jnp.max(jnp.where(first, 0.0, totp), axis=1, keepdims=True)

        def blocks(js, masked):
            ks = [pl.multiple_of(j * tk, tk) for j in js]
            kjs = [k_ref[pl.ds(k0, tk), :] for k0 in ks]
            vjs = [v_ref[pl.ds(k0, tk), :] for k0 in ks]
            chains = [(hd, b) for b in range(len(js)) for hd in range(2)]
            zs = [_dot_nt(qs[hd], kjs[b]) for hd, b in chains]
            dws = [_dot_nt(dobs[hd], vjs[b]) for hd, b in chains]
            parts = [_softplus_parts(z) for z in zs]
            sps = [p[0] for p in parts]
            if masked:
                t_idx = i * tq + lax.broadcasted_iota(jnp.int32, (tq, tk), 0)
                befores = [j * tk + lax.broadcasted_iota(jnp.int32, (tq, tk), 1) < t_idx for j in js]
                spms = [jnp.where(befores[b], sp, 0.0) for (hd, b), sp in zip(chains, sps)]
            else:
                spms = sps
            cums = [_block_sums(spm, later) for spm in spms]
            sums = [jnp.sum(spm, axis=1, keepdims=True) for spm in spms]
            lefts = [left_refs[0], left_refs[1]]
            ws = []
            for (hd, b), z, sp, cum, sm in zip(chains, zs, sps, cums, sums):
                lefts[hd] = lefts[hd] - sm
                w = jnp.exp(z - sp - cum - lefts[hd])
                if masked:
                    w = jnp.where(befores[b], w, 0.0)
                ws.append(w)
            wbs = [w.astype(BF16) for w in ws]
            dvs = [_dot_tn(wb, dobs[hd]) for (hd, b), wb in zip(chains, wbs)]
            gs = [dw * w for dw, w in zip(dws, ws)]
            gcums = [_block_sums(g, earlier) for g in gs]
            gsums = [gsum_refs[0], gsum_refs[1]]
            dzbs = []
            for (hd, b), z, (sp, e1), g, gcum in zip(chains, zs, parts, gs, gcums):
                inv = 1.0 / (1.0 + e1)
                sig = jnp.where(z >= 0.0, inv, e1 * inv)
                dz = g - sig * (g + gsums[hd] + gcum)
                if masked:
                    dz = jnp.where(befores[b], dz, 0.0)
                dzbs.append(dz.astype(BF16))
                gsums[hd] = gsums[hd] + jnp.sum(g, axis=1, keepdims=True)
            dqs = [_dot(dzb, kjs[b]) for (hd, b), dzb in zip(chains, dzbs)]
            dks = [_dot_tn(dzb, qs[hd]) for (hd, b), dzb in zip(chains, dzbs)]
            for b, k0 in enumerate(ks):
                dv_ref[pl.ds(k0, tk), :] += dvs[2 * b] + dvs[2 * b + 1]
                dk_ref[pl.ds(k0, tk), :] += dks[2 * b] + dks[2 * b + 1]
            for hd in range(2):
                tot = dqs[hd]
                for b in range(1, len(js)):
                    tot = tot + dqs[2 * b + hd]
                dq_acc[hd] += tot
                left_refs[hd] = lefts[hd]
                gsum_refs[hd] = gsums[hd]

        below = i * r

        def sweep(n, carry):
            blocks([SB_UNROLL_BWD * n + u for u in range(SB_UNROLL_BWD)], False)
            return carry

        lax.fori_loop(0, below // SB_UNROLL_BWD, sweep, 0)
        for u in range(SB_UNROLL_BWD - 1, 0, -1):
            @pl.when(below % SB_UNROLL_BWD >= u)
            def _(u=u):
                blocks([below - u], False)
        blocks([below + d for d in range(r)], True)
        dq_ref[...] = jnp.where(first, dq_acc[0], dq_acc[1])

        @pl.when(step == nkb * nq - 1)
        def _():
            scatter.wait()

    shp = jax.ShapeDtypeStruct((s, SB_W), F32)
    qspec = pl.BlockSpec((tq, LANES), lambda p, i: (i, p))
    whole = pl.BlockSpec((s, LANES), lambda p, i: (0, p))
    hbm = pl.BlockSpec(memory_space=pl.ANY)
    return pl.pallas_call(
        body, name=name, grid=(nkb, nq),
        out_shape=[shp, shp, shp, jax.ShapeDtypeStruct(slabs.shape, slabs.dtype)],
        in_specs=[qspec,
                  pl.BlockSpec((s, LANES), lambda p, i: (0, nkb + p)),
                  pl.BlockSpec((s, LANES), lambda p, i: (0, 2 * nkb + p)),
                  qspec, qspec, hbm],
        out_specs=[qspec, whole, whole, hbm],
        scratch_shapes=[pltpu.VMEM((2, tq, LANES), F32), pltpu.VMEM((2, tq, 1), F32), pltpu.VMEM((2, tq, 1), F32)]
        + _exchange_sems(1),
        compiler_params=_cparams())(proj, proj, proj, sp_total, dy, slabs)


def _swa_backward(proj, y_sw, dy, sinks, name):
    s = proj.shape[0]
    nb = s // WINDOW
    qb, kb, vb = 3 * SB_W // SWA_QW, (3 * SB_W + SWA_QW) // LANES, (3 * SB_W + SWA_QW + SWA_KW) // LANES

    def body(q_ref, kp_ref, kc_ref, vp_ref, vc_ref, o_ref, do_ref, sink_ref, dq_ref, dk_ref, dv_ref, ds_ref):
        n = pl.program_id(0)

        @pl.when(n == 0)
        def _():
            dk_ref[...] = jnp.zeros_like(dk_ref)
            dv_ref[...] = jnp.zeros_like(dv_ref)
            ds_ref[...] = jnp.zeros_like(ds_ref)

        k = jnp.concatenate([kp_ref[...], kc_ref[...]], axis=0)
        v = jnp.concatenate([vp_ref[...], vc_ref[...]], axis=0)
        k_sw = pltpu.roll(k.astype(F32), HEAD_DIM, 1).astype(BF16)
        v_sw = pltpu.roll(v.astype(F32), HEAD_DIM, 1).astype(BF16)
        lane = lax.broadcasted_iota(jnp.int32, (1, LANES), 1)
        halves = [lane < HEAD_DIM, lane >= HEAD_DIM]
        valid, distf = _swa_masks(n)
        zero = jnp.zeros((2 * WINDOW, LANES), F32)
        dk_nat, dk_rot, dv_nat, dv_rot = zero, zero, zero, zero
        for pair in range(4):
            cols = slice(pair * LANES, (pair + 1) * LANES)
            qp = q_ref[:, cols]
            dop, op = do_ref[:, cols], o_ref[:, cols]
            dq = jnp.zeros((WINDOW, LANES), F32)
            for par in range(2):
                h = 2 * pair + par
                g = h // 4
                qm = jnp.where(halves[par], qp, jnp.zeros((), BF16))
                do = jnp.where(halves[par], dop, 0.0)
                dob = do.astype(BF16)
                delta = jnp.sum(do * op, axis=1, keepdims=True)
                ku, vu = (k, v) if g == par else (k_sw, v_sw)
                p, p_sink = _swa_probs(qm, ku, valid, distf, h, sink_ref[h])
                dsc = (p * (_dot_nt(dob, vu) - delta)).astype(BF16)
                ds_ref[h:h + 1, :] += jnp.zeros((1, LANES), F32) - jnp.sum(p_sink * delta)
                dq = dq + jnp.where(halves[par], _dot(dsc, ku), 0.0)
                dk_h = _dot_tn(dsc, qm)
                dv_h = _dot_tn(p.astype(BF16), dob)
                if g == par:
                    dk_nat, dv_nat = dk_nat + dk_h, dv_nat + dv_h
                else:
                    dk_rot, dv_rot = dk_rot + dk_h, dv_rot + dv_h
            dq_ref[:, cols] = dq
        dk = dk_nat + pltpu.roll(dk_rot, HEAD_DIM, 1)
        dv = dv_nat + pltpu.roll(dv_rot, HEAD_DIM, 1)
        prev = pl.multiple_of(jnp.maximum(n - 1, 0) * WINDOW, WINDOW)
        cur = pl.multiple_of(n * WINDOW, WINDOW)
        dk_ref[pl.ds(prev, WINDOW), :] += dk[:WINDOW]
        dv_ref[pl.ds(prev, WINDOW), :] += dv[:WINDOW]
        dk_ref[pl.ds(cur, WINDOW), :] += dk[WINDOW:]
        dv_ref[pl.ds(cur, WINDOW), :] += dv[WINDOW:]

    prev_blk = lambda n: jnp.maximum(n - 1, 0)
    wide = pl.BlockSpec((WINDOW, SWA_QW), lambda n: (n, 0))
    whole = pl.BlockSpec((s, LANES), lambda n: (0, 0))
    return pl.pallas_call(
        body, name=name, grid=(nb,),
        out_shape=[jax.ShapeDtypeStruct((s, SWA_QW), F32), jax.ShapeDtypeStruct((s, LANES), F32),
                   jax.ShapeDtypeStruct((s, LANES), F32), jax.ShapeDtypeStruct((8, LANES), F32)],
        in_specs=[pl.BlockSpec((WINDOW, SWA_QW), lambda n: (n, qb)),
                  pl.BlockSpec((WINDOW, LANES), lambda n: (prev_blk(n), kb)),
                  pl.BlockSpec((WINDOW, LANES), lambda n: (n, kb)),
                  pl.BlockSpec((WINDOW, LANES), lambda n: (prev_blk(n), vb)),
                  pl.BlockSpec((WINDOW, LANES), lambda n: (n, vb)),
                  wide,
                  pl.BlockSpec((WINDOW, SWA_QW), lambda n: (n, 1)),
                  pl.BlockSpec(memory_space=pltpu.SMEM)],
        out_specs=[wide, whole, whole, pl.BlockSpec((8, LANES), lambda n: (0, 0))],
        compiler_params=_cparams())(proj, proj, proj, proj, proj, y_sw, dy, sinks)


def _in_proj_backward(dq_sb, dk_sb, dv_sb, dq_sw, dk_sw, dv_sw, du1, x, vec, w_in, name):
    s = x.shape[0]
    tb = min(TOK_TILE, s)

    def body(dqsb_ref, dksb_ref, dvsb_ref, dqsw_ref, dksw_ref, dvsw_ref, du1_ref, x_ref, vec_ref, w_ref,
             dproj_ref, gx_ref, acc_ref, bacc_ref):
        @pl.when(pl.program_id(0) == 0)
        def _():
            acc_ref[...] = jnp.zeros_like(acc_ref)
            bacc_ref[...] = jnp.zeros_like(bacc_ref)

        pieces = ((0, dqsb_ref, QK_SCALE), (SB_W, dksb_ref, 1.0), (2 * SB_W, dvsb_ref, 1.0),
                  (3 * SB_W, dqsw_ref, QK_SCALE), (3 * SB_W + SWA_QW, dksw_ref, 1.0),
                  (3 * SB_W + SWA_QW + SWA_KW, dvsw_ref, 1.0))
        for lo, ref, scale in pieces:
            width = ref.shape[1]
            piece = ref[...] * scale
            bacc_ref[0:1, lo:lo + width] += _colsum(piece)
            dproj_ref[:, lo:lo + width] = piece.astype(BF16)
        dh = _dot_nt(dproj_ref[...], w_ref[...])
        xv = x_ref[...]
        gx_ref[...] = ALPHA * du1_ref[...] + dh * (1.0 + vec_ref[V_SC_A:V_SC_A + 1, :])
        acc_ref[C_SCA:C_SCA + 1, :] += _colsum(dh * xv)
        acc_ref[C_SHA:C_SHA + 1, :] += _colsum(dh)

    half = pl.BlockSpec((tb, SB_W), lambda i: (i, 0))
    narrow = pl.BlockSpec((tb, LANES), lambda i: (i, 0))
    full = pl.BlockSpec((tb, D), lambda i: (i, 0))
    return pl.pallas_call(
        body, name=name, grid=(s // tb,),
        out_shape=[jax.ShapeDtypeStruct((s, D_IN), BF16), jax.ShapeDtypeStruct((s, D), F32),
                   jax.ShapeDtypeStruct((8, D), F32), jax.ShapeDtypeStruct((8, D_IN), F32)],
        in_specs=[half, half, half, half, narrow, narrow, full, full, _resident((VEC_ROWS, D)), _resident((D, D_IN))],
        out_specs=[pl.BlockSpec((tb, D_IN), lambda i: (i, 0)), full, pl.BlockSpec((8, D), lambda i: (0, 0)),
                   pl.BlockSpec((8, D_IN), lambda i: (0, 0))],
        compiler_params=_cparams())(dq_sb, dk_sb, dv_sb, dq_sw, dk_sw, dv_sw, du1, x, vec, w_in)


def _weight_grad(a, b, name):
    s, m = a.shape
    n = b.shape[1]
    tn = 512 if n % 512 == 0 else n
    ts = min(512, s)

    def body(a_ref, b_ref, o_ref):
        @pl.when(pl.program_id(1) == 0)
        def _():
            o_ref[...] = jnp.zeros_like(o_ref)

        o_ref[...] += _dot_tn(a_ref[...], b_ref[...])

    return pl.pallas_call(
        body, name=name, grid=(n // tn, s // ts),
        out_shape=jax.ShapeDtypeStruct((m, n), F32),
        in_specs=[pl.BlockSpec((ts, m), lambda j, k: (k, 0)), pl.BlockSpec((ts, tn), lambda j, k: (k, j))],
        out_specs=pl.BlockSpec((m, tn), lambda j, k: (0, j)),
        compiler_params=_cparams())(a, b)


def _pad_rows(v, rows):
    return jnp.concatenate([v, jnp.zeros((rows - v.shape[0], v.shape[1]), v.dtype)], axis=0)


def _col_shards(w, n_shards):
    r, n = w.shape
    return w.reshape(r, n_shards, n // n_shards).transpose(1, 0, 2)


def kernel(x, c, w_ada, b_ada, w_in, b_in, sinks, gn_sb, gn_swa, w_out, ln1_g, ln1_b, w_gu, w_down, ln2_g, ln2_b, loss_target, m_w_ada, m_b_ada, m_w_in, m_b_in, m_sinks, m_gn_sb, m_gn_swa, m_w_out, m_ln1_g, m_ln1_b, m_w_gu, m_w_down, m_ln2_g, m_ln2_b, v_w_ada, v_b_ada, v_w_in, v_b_in, v_sinks, v_gn_sb, v_gn_swa, v_w_out, v_ln1_g, v_ln1_b, v_w_gu, v_w_down, v_ln2_g, v_ln2_b):
    ix, iy, ic = lax.axis_index("x"), lax.axis_index("y"), lax.axis_index("c")
    chip = 2 * ix + iy
    dev = 4 * ix + 2 * iy + ic
    xs, target = x[0], loss_target[0]
    s = xs.shape[0]

    c_all = _allgather8(_pad_rows(c, 8), "gather_c")[::8]
    n_ada = w_ada.shape[2]
    b_ada_shard = lax.dynamic_slice_in_dim(b_ada, chip * n_ada, n_ada, axis=1)
    mod_cols, silu_c = _mod_shard(c_all, w_ada[0], b_ada_shard, "mod_shard")
    mod_all = _allgather8(mod_cols, "gather_mod").reshape(N_DEV, 8, n_ada)
    mod_mine = lax.dynamic_index_in_dim(mod_all, dev, axis=1, keepdims=False)
    mod = mod_mine.reshape(N_CHIPS, 2, n_ada)[:, 0].reshape(6, D)
    vec = jnp.concatenate([mod, ln1_g, ln1_b, ln2_g, ln2_b, jnp.concatenate([gn_sb, gn_swa], axis=1),
                           jnp.zeros((VEC_ROWS - 11, D), F32)], axis=0)

    (g_in,) = _chip_allgather([w_in[0].astype(BF16)], "gather_w_in")
    w_in_b = g_in.transpose(1, 0, 2).reshape(D, D_IN)

    h_b, proj = _in_proj(xs, vec, w_in_b, b_in, "in_proj")
    y_sb, sp_total, g_out, g_gu, g_down = _sb_forward(
        proj, [w_out[0].astype(BF16), w_gu[0].astype(BF16), w_down[0].astype(BF16)], "sb_forward")
    w_gu_b = g_gu.transpose(1, 0, 2).reshape(D, 2 * D_FF)
    w_out_b = g_out.reshape(D, D)
    w_down_b = g_down.reshape(D_FF, D)
    sink_vec = sinks[0]
    y_sw = _swa_forward(proj, sink_vec, "swa_forward")
    mixed_b, attn, x1, h2_b = _post_attention(y_sb, y_sw, xs, vec, w_out_b, "post_attention")
    gu, act_b, ffn = _ffn_forward(h2_b, w_gu_b, w_down_b, "ffn_forward")

    def chip_sums(per_shard, tag):
        flat = jnp.concatenate(per_shard, axis=1)
        rows = flat.shape[1] // (2 * LANES)
        halves = flat.reshape(4, 2, rows, LANES)
        keep = lax.dynamic_index_in_dim(halves, ic, axis=1, keepdims=False)
        give = lax.dynamic_index_in_dim(halves, 1 - ic, axis=1, keepdims=False)
        got = _sibling_send(give, "grad_halves_swap_" + tag)
        both = _add2(keep.reshape(4 * rows, LANES), got.reshape(4 * rows, LANES), "grad_chip_sum_" + tag)
        return both.reshape(4, rows, LANES)

    dffn_b, dgu_b, dx1, acc_f = _ffn_backward(x1, ffn, target, gu, vec, w_gu_b, w_down_b, "ffn_backward")
    dw_gu = _weight_grad(h2_b, dgu_b, "grad_w_gu")
    dw_down = _weight_grad(act_b, dffn_b, "grad_w_down")
    sums_f = chip_sums([_col_shards(dw_gu, 4).reshape(4, -1), dw_down.reshape(4, -1)], "ffn")
    du1, dattn_b, dy, acc_a = _attn_out_backward(dx1, xs, attn, y_sb, y_sw, vec, w_out_b, "attn_out_backward")
    dq_sb, dk_sb, dv_sb, parts_f = _sb_backward(proj, sp_total, dy, sums_f, "sb_backward")
    dq_sw, dk_sw, dv_sw, dsink = _swa_backward(proj, y_sw, dy, sink_vec, "swa_backward")
    dproj_b, grad_x, acc_i, acc_b = _in_proj_backward(dq_sb, dk_sb, dv_sb, dq_sw, dk_sw, dv_sw, du1, xs, vec, w_in_b,
                                                      "in_proj_backward")
    dw_in = _weight_grad(h_b, dproj_b, "grad_w_in")
    dw_out = _weight_grad(mixed_b, dattn_b, "grad_w_out")
    sums_a = chip_sums([_col_shards(dw_in, 4).reshape(4, -1), dw_out.reshape(4, -1)], "attn")
    parts_a = _chip_scatter(sums_a, "grad_chip_scatter_attn")
    my_half = jnp.concatenate([_sum4(parts_f, "grad_reduce_ffn"), _sum4(parts_a, "grad_reduce_attn")], axis=0)
    other_half = _sibling_send(my_half, "grad_half_return")
    rows_f = parts_f.shape[1]

    def whole_shard(lo, hi):
        mine, other = my_half[lo:hi], other_half[lo:hi]
        return jnp.concatenate([jnp.where(ic == 0, mine, other), jnp.where(ic == 0, other, mine)], axis=0).reshape(-1)

    flat_f, flat_a = whole_shard(0, rows_f), whole_shard(rows_f, my_half.shape[0])
    n_gu, n_in = D * (2 * D_FF // 4), D * (D_IN // 4)
    gw_gu = flat_f[:n_gu].reshape(D, 2 * D_FF // 4)
    gw_down = flat_f[n_gu:].reshape(D_FF // 4, D)
    gw_in = flat_a[:n_in].reshape(D, D_IN // 4)
    gw_out = flat_a[n_in:].reshape(D // 4, D)

    dmod = jnp.concatenate([acc_i[C_SHA:C_SHA + 1], acc_i[C_SCA:C_SCA + 1], acc_a[B_GA:B_GA + 1],
                            acc_f[A_SHF:A_SHF + 1], acc_f[A_SCF:A_SCF + 1], acc_f[A_GF:A_GF + 1]], axis=1)
    dsink_row = jnp.concatenate([dsink[:, 0].reshape(1, 8), jnp.zeros((1, LANES - 8), F32)], axis=1)
    loss_row = jnp.concatenate([jnp.sum(acc_f[A_LOSS:A_LOSS + 1], axis=1, keepdims=True),
                                jnp.zeros((1, LANES - 1), F32)], axis=1)
    small = jnp.concatenate([dmod, acc_b[0:1], acc_a[B_LN1G:B_LN1G + 1], acc_a[B_LN1B:B_LN1B + 1],
                             acc_f[A_LN2G:A_LN2G + 1], acc_f[A_LN2B:A_LN2B + 1], acc_a[B_GN:B_GN + 1],
                             dsink_row, loss_row], axis=1)
    small_all = _allgather8(_pad_rows(small, 8), "gather_small")[::8]

    def pack_small(b_ada_, b_in_, ln1g_, ln1b_, ln2g_, ln2b_, gsb_, gsw_, sinks_):
        return jnp.concatenate([b_ada_, b_in_, ln1g_, ln1b_, ln2g_, ln2b_, gsb_, gsw_, sinks_,
                                jnp.ones((1, 2 * LANES - 8), F32)], axis=1)

    w_small = pack_small(b_ada, b_in, ln1_g, ln1_b, ln2_g, ln2_b, gn_sb, gn_swa, sinks)
    m_small = pack_small(m_b_ada, m_b_in, m_ln1_g, m_ln1_b, m_ln2_g, m_ln2_b, m_gn_sb, m_gn_swa, m_sinks)
    v_small = pack_small(v_b_ada, v_b_in, v_ln1_g, v_ln1_b, v_ln2_g, v_ln2_b, v_gn_sb, v_gn_swa, v_sinks)
    small_out = _small_update(small_all, w_small, m_small, v_small, "small_update")

    def unpack_small(row):
        return {"b_ada": row[:, SM_MOD:SM_BIN], "b_in": row[:, SM_BIN:SM_LN1G], "ln1_g": row[:, SM_LN1G:SM_LN1B],
                "ln1_b": row[:, SM_LN1B:SM_LN2G], "ln2_g": row[:, SM_LN2G:SM_LN2B], "ln2_b": row[:, SM_LN2B:SM_GN],
                "gn_sb": row[:, SM_GN:SM_GN + SB_W], "gn_swa": row[:, SM_GN + SB_W:SM_SINK],
                "sinks": row[:, SM_SINK:SM_SINK + 8]}

    g_small, d_small, m2_small, v2_small = [unpack_small(r) for r in small_out]
    loss = small_out[0][0, SM_LOSS]

    dmod_cols = lax.dynamic_slice_in_dim(small_all[:, SM_MOD:SM_BIN], chip * n_ada, n_ada, axis=1)
    gw_ada = _weight_grad(_pad_rows(silu_c, LANES).astype(BF16), _pad_rows(dmod_cols, LANES).astype(BF16), "grad_w_ada")

    big = {}
    for nm, w, g, m, v in (("w_ada", w_ada, gw_ada, m_w_ada, v_w_ada), ("w_in", w_in, gw_in, m_w_in, v_w_in),
                           ("w_out", w_out, gw_out, m_w_out, v_w_out), ("w_gu", w_gu, gw_gu, m_w_gu, v_w_gu),
                           ("w_down", w_down, gw_down, m_w_down, v_w_down)):
        d_, m2_, v2_ = _adamw(w[0], g, m[0], v[0], "adamw_" + nm)
        big[nm] = (g[None], d_[None], m2_[None], v2_[None])

    order = ["w_ada", "b_ada", "w_in", "b_in", "sinks", "gn_sb", "gn_swa", "w_out", "ln1_g", "ln1_b", "w_gu", "w_down",
             "ln2_g", "ln2_b"]

    def leaf(nm, which):
        if nm in big:
            return big[nm][which]
        return (g_small, d_small, m2_small, v2_small)[which][nm]

    outs = [loss, grad_x[None]]
    for which in range(4):
        outs += [leaf(nm, which) for nm in order]
    return tuple(outs)
```

```python
import functools
import math

import jax
import jax.numpy as jnp
from jax import lax
from jax.experimental import pallas as pl
from jax.experimental.pallas import tpu as pltpu

F32 = jnp.float32
BF16 = jnp.bfloat16

D = 1024
HEAD_DIM = 64
SB_W = 512
SWA_QW = 512
SWA_KW = 128
D_IN = 2304
D_FF = 2816
WINDOW = 128
ALPHA = 2.0 ** 0.25
LN_EPS = 1e-5
RMS_EPS = 1e-6
MASK_VALUE = -1e30
QK_SCALE = 1.0 / math.sqrt(HEAD_DIM)

ADAM_LR = 0.001
ADAM_B1 = 0.9
ADAM_B2 = 0.999
ADAM_EPS = 1e-08
ADAM_WD = 0.01
ADAM_STEP = 10

N_CHIPS = 4
N_DEV = 8
LANES = 128

SB_TQ = 512
SB_TK = 256
SB_UNROLL = 2
SB_UNROLL_BWD = 2
TOK_TILE = 512
FFN_TILE = 256
FFN_BWD_TILE = 256
VMEM_LIMIT = 56 * 1024 * 1024

V_SH_A, V_SC_A, V_G_A, V_SH_F, V_SC_F, V_G_F, V_LN1G, V_LN1B, V_LN2G, V_LN2B, V_GN = range(11)
VEC_ROWS = 16

SM_MOD = 0
SM_BIN = 6 * D
SM_LN1G = SM_BIN + D_IN
SM_LN1B = SM_LN1G + D
SM_LN2G = SM_LN1B + D
SM_LN2B = SM_LN2G + D
SM_GN = SM_LN2B + D
SM_SINK = SM_GN + D
SM_LOSS = SM_SINK + LANES
SM_LEN = SM_LOSS + LANES

MESH = pl.DeviceIdType.MESH


def _cparams(**kw):
    return pltpu.CompilerParams(vmem_limit_bytes=VMEM_LIMIT, **kw)


def _resident(shape):
    nd = len(shape)
    return pl.BlockSpec(shape, lambda *_: (0,) * nd, pipeline_mode=pl.Buffered(1))


def _dot(a, b):
    return jnp.dot(a, b, preferred_element_type=F32)


def _dot_nt(a, b):
    return lax.dot_general(a, b, (((1,), (1,)), ((), ())), preferred_element_type=F32)


def _dot_tn(a, b):
    return lax.dot_general(a, b, (((0,), (0,)), ((), ())), preferred_element_type=F32)


def _sum_matrix(tk, keep):
    row = lax.broadcasted_iota(jnp.int32, (tk, tk + LANES), 0)
    col = lax.broadcasted_iota(jnp.int32, (tk, tk + LANES), 1)
    return (keep(row, col) | (col >= tk)).astype(BF16)


def _block_sums(x, m):
    tk = x.shape[1]
    res = _dot(x.astype(BF16), m)
    return res[:, :tk], res[:, tk:]


def _across(v, tk):
    return jnp.concatenate([v] * (tk // LANES), axis=1)


def _allgather8(v, name):
    m_per, n = v.shape

    def body(x_ref, out_ref, send_sems, recv_sems, local_sem):
        x, y, c = lax.axis_index("x"), lax.axis_index("y"), lax.axis_index("c")
        me, sibling = (x, y, c), (x, y, 1 - c)
        chips = [(1 - x, y), (x, 1 - y), (1 - x, 1 - y)]

        def rows(px, py, pc):
            return out_ref.at[pl.ds((4 * px + 2 * py + pc) * m_per, m_per), :]

        def copy(k, block, to, src=None):
            return pltpu.make_async_remote_copy(
                src_ref=rows(*block) if src is None else src, dst_ref=rows(*block),
                send_sem=send_sems.at[k], recv_sem=recv_sems.at[k], device_id=to, device_id_type=MESH)

        mine = pltpu.make_async_copy(x_ref, rows(*me), local_sem)
        mine.start()
        first = [copy(0, me, sibling, src=x_ref)]
        first += [copy(1 + j, me, (*chip, c), src=x_ref) for j, chip in enumerate(chips)]
        for cp in first:
            cp.start()
        passed = [copy(4 + j, (*chip, c), sibling) for j, chip in enumerate(chips)]
        for j, chip in enumerate(chips):
            copy(1 + j, (*chip, c), me).wait_recv()
            passed[j].start()
        copy(0, sibling, me).wait_recv()
        for j, chip in enumerate(chips):
            copy(4 + j, (*chip, 1 - c), me).wait_recv()
        for cp in first + passed:
            cp.wait_send()
        mine.wait()

    return pl.pallas_call(
        body, name=name,
        out_shape=jax.ShapeDtypeStruct((N_DEV * m_per, n), v.dtype),
        in_specs=[pl.BlockSpec(memory_space=pltpu.VMEM)],
        out_specs=pl.BlockSpec(memory_space=pltpu.VMEM),
        scratch_shapes=[pltpu.SemaphoreType.DMA((7,)), pltpu.SemaphoreType.DMA((7,)), pltpu.SemaphoreType.DMA],
        compiler_params=_cparams(),
    )(v)


class _Exchange:
    def __init__(self, local, sends, arrivals):
        self.local, self.sends, self.arrivals = local, sends, arrivals

    def start(self):
        for cp in self.local + self.sends:
            cp.start()

    def wait(self):
        for cp in self.arrivals:
            cp.wait_recv()
        for cp in self.sends:
            cp.wait_send()
        for cp in self.local:
            cp.wait()


def _exchange_sems(n):
    return [pltpu.SemaphoreType.DMA((3 * n,)), pltpu.SemaphoreType.DMA((3 * n,)), pltpu.SemaphoreType.DMA((n,))]


def _gather_exchange(ins, outs, send_sems, recv_sems, local_sems):
    x, y, c = lax.axis_index("x"), lax.axis_index("y"), lax.axis_index("c")
    slot = 2 * x + y
    chips = [(1 - x, y), (x, 1 - y), (1 - x, 1 - y)]
    local, sends, arrivals = [], [], []
    for a in range(len(ins)):
        local.append(pltpu.make_async_copy(ins[a], outs[a].at[slot], local_sems.at[a]))
        for j, (px, py) in enumerate(chips):
            sems = dict(send_sem=send_sems.at[3 * a + j], recv_sem=recv_sems.at[3 * a + j],
                        device_id=(px, py, c), device_id_type=MESH)
            sends.append(pltpu.make_async_remote_copy(src_ref=ins[a], dst_ref=outs[a].at[slot], **sems))
            arrivals.append(pltpu.make_async_remote_copy(src_ref=ins[a], dst_ref=outs[a].at[2 * px + py], **sems))
    return _Exchange(local, sends, arrivals)


def _scatter_exchange(p_ref, out_ref, send_sems, recv_sems, local_sems):
    x, y, c = lax.axis_index("x"), lax.axis_index("y"), lax.axis_index("c")
    slot = 2 * x + y
    chips = [(1 - x, y), (x, 1 - y), (1 - x, 1 - y)]
    local = [pltpu.make_async_copy(p_ref.at[slot], out_ref.at[slot], local_sems.at[0])]
    sends, arrivals = [], []
    for j, (px, py) in enumerate(chips):
        sems = dict(send_sem=send_sems.at[j], recv_sem=recv_sems.at[j], device_id=(px, py, c), device_id_type=MESH)
        sends.append(pltpu.make_async_remote_copy(src_ref=p_ref.at[2 * px + py], dst_ref=out_ref.at[slot], **sems))
        arrivals.append(pltpu.make_async_remote_copy(src_ref=p_ref.at[slot], dst_ref=out_ref.at[2 * px + py], **sems))
    return _Exchange(local, sends, arrivals)


def _chip_allgather(arrs, name):
    n = len(arrs)

    def body(*refs):
        ex = _gather_exchange(refs[:n], refs[n:2 * n], *refs[2 * n:])
        ex.start()
        ex.wait()

    hbm = pl.BlockSpec(memory_space=pl.ANY)
    return pl.pallas_call(
        body, name=name,
        out_shape=[jax.ShapeDtypeStruct((N_CHIPS,) + a.shape, a.dtype) for a in arrs],
        in_specs=[hbm] * n, out_specs=[hbm] * n,
        scratch_shapes=_exchange_sems(n),
        compiler_params=_cparams(),
    )(*arrs)


def _sibling_send(v, name):
    def body(v_ref, out_ref, send_sem, recv_sem):
        x, y, c = lax.axis_index("x"), lax.axis_index("y"), lax.axis_index("c")
        cp = pltpu.make_async_remote_copy(src_ref=v_ref, dst_ref=out_ref, send_sem=send_sem, recv_sem=recv_sem,
                                          device_id=(x, y, 1 - c), device_id_type=MESH)
        cp.start()
        cp.wait()

    hbm = pl.BlockSpec(memory_space=pl.ANY)
    return pl.pallas_call(
        body, name=name, out_shape=jax.ShapeDtypeStruct(v.shape, v.dtype),
        in_specs=[hbm], out_specs=hbm,
        scratch_shapes=[pltpu.SemaphoreType.DMA, pltpu.SemaphoreType.DMA],
        compiler_params=_cparams(),
    )(v)


def _chip_scatter(p, name):
    def body(p_ref, out_ref, send_sems, recv_sems, local_sems):
        ex = _scatter_exchange(p_ref, out_ref, send_sems, recv_sems, local_sems)
        ex.start()
        ex.wait()

    hbm = pl.BlockSpec(memory_space=pl.ANY)
    return pl.pallas_call(
        body, name=name, out_shape=jax.ShapeDtypeStruct(p.shape, p.dtype),
        in_specs=[hbm], out_specs=hbm,
        scratch_shapes=_exchange_sems(1),
        compiler_params=_cparams(),
    )(p)


def _add2(a, b, name):
    rows = a.shape[0]
    tr = rows // 8

    def body(a_ref, b_ref, o_ref):
        o_ref[...] = a_ref[...] + b_ref[...]

    spec = pl.BlockSpec((tr, LANES), lambda i: (i, 0))
    return pl.pallas_call(body, name=name, grid=(rows // tr,), out_shape=jax.ShapeDtypeStruct(a.shape, a.dtype),
                          in_specs=[spec, spec], out_specs=spec, compiler_params=_cparams())(a, b)


def _sum4(p, name):
    rows = p.shape[1]
    tr = rows // 8

    def body(p_ref, o_ref):
        o_ref[...] = ((p_ref[0] + p_ref[1]) + p_ref[2]) + p_ref[3]

    return pl.pallas_call(
        body, name=name, grid=(rows // tr,), out_shape=jax.ShapeDtypeStruct((rows, LANES), p.dtype),
        in_specs=[pl.BlockSpec((4, tr, LANES), lambda i: (0, i, 0))],
        out_specs=pl.BlockSpec((tr, LANES), lambda i: (i, 0)), compiler_params=_cparams())(p)


def _adam_math(w, g, m, v):
    m2 = ADAM_B1 * m + (1.0 - ADAM_B1) * g
    v2 = ADAM_B2 * v + (1.0 - ADAM_B2) * (g * g)
    m_hat = m2 / (1.0 - ADAM_B1 ** ADAM_STEP)
    v_hat = v2 / (1.0 - ADAM_B2 ** ADAM_STEP)
    delta = -ADAM_LR * (m_hat / (jnp.sqrt(v_hat) + ADAM_EPS) + ADAM_WD * w)
    return delta, m2, v2


def _adamw(w, g, m, v, name):
    rows, cols = w.shape
    tr = rows // 4 if rows % 32 == 0 else rows

    def body(w_ref, g_ref, m_ref, v_ref, d_ref, m2_ref, v2_ref):
        delta, m2, v2 = _adam_math(w_ref[...], g_ref[...], m_ref[...], v_ref[...])
        d_ref[...] = delta
        m2_ref[...] = m2
        v2_ref[...] = v2

    spec = pl.BlockSpec((tr, cols), lambda i: (i, 0))
    shp = jax.ShapeDtypeStruct(w.shape, F32)
    return pl.pallas_call(body, name=name, grid=(rows // tr,), out_shape=[shp, shp, shp],
                          in_specs=[spec] * 4, out_specs=[spec] * 3, compiler_params=_cparams())(w, g, m, v)


def _small_update(g8, w, m, v, name):
    n = w.shape[1]

    def body(g8_ref, w_ref, m_ref, v_ref, g_ref, d_ref, m2_ref, v2_ref):
        g = g8_ref[0:1, :]
        for r in range(1, N_DEV):
            g = g + g8_ref[r:r + 1, :]
        delta, m2, v2 = _adam_math(w_ref[...], g, m_ref[...], v_ref[...])
        g_ref[...] = g
        d_ref[...] = delta
        m2_ref[...] = m2
        v2_ref[...] = v2

    shp = jax.ShapeDtypeStruct((1, n), F32)
    vm = pl.BlockSpec(memory_space=pltpu.VMEM)
    return pl.pallas_call(body, name=name, out_shape=[shp] * 4, in_specs=[vm] * 4, out_specs=[vm] * 4,
                          compiler_params=_cparams())(g8, w, m, v)


def _mod_shard(c8, w_ada, b_ada_shard, name):
    n = w_ada.shape[1]
    tn = 512

    def body(c_ref, w_ref, b_ref, o_ref, s_ref):
        cv = c_ref[...]
        sc = cv * (1.0 / (1.0 + jnp.exp(-cv)))
        s_ref[...] = sc
        o_ref[...] = _dot(sc.astype(BF16), w_ref[...].astype(BF16)) + b_ref[...]

    return pl.pallas_call(
        body, name=name, grid=(n // tn,),
        out_shape=[jax.ShapeDtypeStruct((8, n), F32), jax.ShapeDtypeStruct((8, D), F32)],
        in_specs=[pl.BlockSpec((8, D), lambda j: (0, 0)), pl.BlockSpec((D, tn), lambda j: (0, j)),
                  pl.BlockSpec((1, tn), lambda j: (0, j))],
        out_specs=[pl.BlockSpec((8, tn), lambda j: (0, j)), pl.BlockSpec((8, D), lambda j: (0, 0))],
        compiler_params=_cparams())(c8, w_ada, b_ada_shard)


def _layer_norm_stats(u):
    mu = jnp.mean(u, axis=1, keepdims=True)
    d = u - mu
    var = jnp.mean(d * d, axis=1, keepdims=True)
    rstd = lax.rsqrt(var + LN_EPS)
    return d * rstd, rstd


def _in_proj(x, vec, w_in, b_in, name):
    s = x.shape[0]
    tb = min(TOK_TILE, s)

    def body(x_ref, vec_ref, w_ref, b_ref, h_ref, p_ref):
        h = x_ref[...] * (1.0 + vec_ref[V_SC_A:V_SC_A + 1, :]) + vec_ref[V_SH_A:V_SH_A + 1, :]
        hb = h.astype(BF16)
        h_ref[...] = hb
        proj = _dot(hb, w_ref[...]) + b_ref[...]
        col = lax.broadcasted_iota(jnp.int32, (1, D_IN), 1)
        is_q = (col < SB_W) | ((col >= 3 * SB_W) & (col < 3 * SB_W + SWA_QW))
        p_ref[...] = (proj * jnp.where(is_q, QK_SCALE, 1.0)).astype(BF16)

    return pl.pallas_call(
        body, name=name, grid=(s // tb,),
        out_shape=[jax.ShapeDtypeStruct((s, D), BF16), jax.ShapeDtypeStruct((s, D_IN), BF16)],
        in_specs=[pl.BlockSpec((tb, D), lambda i: (i, 0)), _resident((VEC_ROWS, D)), _resident((D, D_IN)),
                  _resident((1, D_IN))],
        out_specs=[pl.BlockSpec((tb, D), lambda i: (i, 0)), pl.BlockSpec((tb, D_IN), lambda i: (i, 0))],
        compiler_params=_cparams())(x, vec, w_in, b_in)


def _softplus_parts(z):
    e1 = jnp.exp(-jnp.abs(z))
    sp = jnp.maximum(z, 0.0) + jnp.log(1.0 + e1)
    return sp, e1


def _sb_forward(proj, shards, name):
    s = proj.shape[0]
    tq, tk = min(SB_TQ, s), min(SB_TK, s)
    r = tq // tk

    n_sh = len(shards)
    nkb = SB_W // LANES
    nq = s // tq

    def body(q_ref, k_ref, v_ref, *refs):
        sh_refs, (o_ref, tot_ref), got_refs = refs[:n_sh], refs[n_sh:n_sh + 2], refs[n_sh + 2:2 * n_sh + 2]
        acc_refs, run_refs = refs[2 * n_sh + 2:2 * n_sh + 4]
        i = pl.program_id(1)
        step = pl.program_id(0) * nq + i
        gather = _gather_exchange(sh_refs, got_refs, *refs[2 * n_sh + 4:])

        @pl.when(step == 0)
        def _():
            gather.start()

        lane = lax.broadcasted_iota(jnp.int32, (1, LANES), 1)
        first = lane < HEAD_DIM
        qp = q_ref[...]
        zero = jnp.zeros((), BF16)
        qs = (jnp.where(first, qp, zero), jnp.where(first, zero, qp))
        later = _sum_matrix(tk, lambda row, col: row > col)
        acc_refs[...] = jnp.zeros_like(acc_refs)
        run_refs[...] = jnp.zeros_like(run_refs)

        def blocks(js, masked):
            ks = [pl.multiple_of(j * tk, tk) for j in js]
            kjs = [k_ref[pl.ds(k0, tk), :] for k0 in ks]
            vjs = [v_ref[pl.ds(k0, tk), :] for k0 in ks]
            chains = [(hd, b) for b in range(len(js)) for hd in range(2)]
            zs = [_dot_nt(qs[hd], kjs[b]) for hd, b in chains]
            sps = [_softplus_parts(z)[0] for z in zs]
            if masked:
                t_idx = i * tq + lax.broadcasted_iota(jnp.int32, (tq, tk), 0)
                befores = [j * tk + lax.broadcasted_iota(jnp.int32, (tq, tk), 1) < t_idx for j in js]
                spms = [jnp.where(befores[b], sp, 0.0) for (hd, b), sp in zip(chains, sps)]
            else:
                spms = sps
            cums = [_block_sums(spm, later) for spm in spms]
            runs = [run_refs[0], run_refs[1]]
            ws = []
            for (hd, b), z, sp, (cum, sm) in zip(chains, zs, sps, cums):
                w = jnp.exp(z - sp - cum - _across(runs[hd], tk))
                if masked:
                    w = jnp.where(befores[b], w, 0.0)
                ws.append(w.astype(BF16))
                runs[hd] = runs[hd] + sm
            pvs = [_dot(w, vjs[b]) for (hd, b), w in zip(chains, ws)]
            for hd in range(2):
                tot = pvs[hd]
                for b in range(1, len(js)):
                    tot = tot + pvs[2 * b + hd]
                acc_refs[hd] += tot
                run_refs[hd] = runs[hd]

        blocks([i * r + (r - 1 - d) for d in range(r)], True)

        below = i * r

        def sweep(n, carry):
            top = below - 1 - SB_UNROLL * n
            blocks([top - u for u in range(SB_UNROLL)], False)
            return carry

        lax.fori_loop(0, below // SB_UNROLL, sweep, 0)
        for u in range(1, SB_UNROLL):
            @pl.when(below % SB_UNROLL >= u)
            def _(u=u):
                blocks([below % SB_UNROLL - u], False)
        o_ref[...] = jnp.where(first, acc_refs[0], acc_refs[1])
        tot_ref[...] = jnp.where(first, run_refs[0], run_refs[1])

        @pl.when(step == nkb * nq - 1)
        def _():
            gather.wait()

    shp = jax.ShapeDtypeStruct((s, SB_W), F32)
    qspec = pl.BlockSpec((tq, LANES), lambda p, i: (i, p))
    hbm = pl.BlockSpec(memory_space=pl.ANY)
    return pl.pallas_call(
        body, name=name, grid=(nkb, nq),
        out_shape=[shp, shp] + [jax.ShapeDtypeStruct((N_CHIPS,) + a.shape, a.dtype) for a in shards],
        in_specs=[qspec,
                  pl.BlockSpec((s, LANES), lambda p, i: (0, nkb + p)),
                  pl.BlockSpec((s, LANES), lambda p, i: (0, 2 * nkb + p))] + [hbm] * n_sh,
        out_specs=[qspec, qspec] + [hbm] * n_sh,
        scratch_shapes=[pltpu.VMEM((2, tq, LANES), F32), pltpu.VMEM((2, tq, LANES), F32)] + _exchange_sems(n_sh),
        compiler_params=_cparams())(proj, proj, proj, *shards)


def _swa_masks(n):
    ti = lax.broadcasted_iota(jnp.int32, (WINDOW, 2 * WINDOW), 0)
    kj = lax.broadcasted_iota(jnp.int32, (WINDOW, 2 * WINDOW), 1)
    dist = ti + WINDOW - kj
    valid = (dist >= 0) & (dist < WINDOW) & ((n * WINDOW - WINDOW + kj) >= 0)
    return valid, dist.astype(F32)


def _swa_probs(qm, ku, valid, distf, h, sink):
    slope = 2.0 ** (-(h + 1))
    sc = _dot_nt(qm, ku)
    sc = jnp.where(valid, sc - slope * distf, MASK_VALUE)
    mx = jnp.maximum(jnp.max(sc, axis=1, keepdims=True), sink)
    p = jnp.exp(sc - mx)
    es = jnp.exp(sink - mx)
    inv = 1.0 / (jnp.sum(p, axis=1, keepdims=True) + es)
    return p * inv, es * inv


def _swa_forward(proj, sinks, name):
    s = proj.shape[0]
    nb = s // WINDOW
    qb, kb, vb = 3 * SB_W // SWA_QW, (3 * SB_W + SWA_QW) // LANES, (3 * SB_W + SWA_QW + SWA_KW) // LANES

    def body(q_ref, kp_ref, kc_ref, vp_ref, vc_ref, sink_ref, o_ref):
        n = pl.program_id(0)
        k = jnp.concatenate([kp_ref[...], kc_ref[...]], axis=0)
        v = jnp.concatenate([vp_ref[...], vc_ref[...]], axis=0)
        k_sw = pltpu.roll(k.astype(F32), HEAD_DIM, 1).astype(BF16)
        v_sw = pltpu.roll(v.astype(F32), HEAD_DIM, 1).astype(BF16)
        lane = lax.broadcasted_iota(jnp.int32, (1, LANES), 1)
        halves = [lane < HEAD_DIM, lane >= HEAD_DIM]
        valid, distf = _swa_masks(n)
        for pair in range(4):
            qp = q_ref[:, pair * LANES:(pair + 1) * LANES]
            out = jnp.zeros((WINDOW, LANES), F32)
            for par in range(2):
                h = 2 * pair + par
                g = h // 4
                qm = jnp.where(halves[par], qp, jnp.zeros((), BF16))
                ku, vu = (k, v) if g == par else (k_sw, v_sw)
                p, _ = _swa_probs(qm, ku, valid, distf, h, sink_ref[h])
                out = out + jnp.where(halves[par], _dot(p.astype(BF16), vu), 0.0)
            o_ref[:, pair * LANES:(pair + 1) * LANES] = out

    prev = lambda n: jnp.maximum(n - 1, 0)
    return pl.pallas_call(
        body, name=name, grid=(nb,),
        out_shape=jax.ShapeDtypeStruct((s, SWA_QW), F32),
        in_specs=[pl.BlockSpec((WINDOW, SWA_QW), lambda n: (n, qb)),
                  pl.BlockSpec((WINDOW, LANES), lambda n: (prev(n), kb)),
                  pl.BlockSpec((WINDOW, LANES), lambda n: (n, kb)),
                  pl.BlockSpec((WINDOW, LANES), lambda n: (prev(n), vb)),
                  pl.BlockSpec((WINDOW, LANES), lambda n: (n, vb)),
                  pl.BlockSpec(memory_space=pltpu.SMEM)],
        out_specs=pl.BlockSpec((WINDOW, SWA_QW), lambda n: (n, 0)),
        compiler_params=_cparams())(proj, proj, proj, proj, proj, sinks)


def _rms_parts(y):
    return lax.rsqrt(jnp.mean(y * y, axis=1, keepdims=True) + RMS_EPS)


def _post_attention(y_sb, y_sw, x, vec, w_out, name):
    s = x.shape[0]
    tb = min(TOK_TILE, s)

    def body(ysb_ref, ysw_ref, x_ref, vec_ref, w_ref, mixed_ref, attn_ref, x1_ref, h2_ref):
        ysb, ysw = ysb_ref[...], ysw_ref[...]
        nsb = (ysb * _rms_parts(ysb) * vec_ref[V_GN:V_GN + 1, :SB_W]).astype(BF16)
        nsw = (ysw * _rms_parts(ysw) * vec_ref[V_GN:V_GN + 1, SB_W:]).astype(BF16)
        mixed_ref[:, :SB_W] = nsb
        mixed_ref[:, SB_W:] = nsw
        attn = _dot(nsb, w_ref[:SB_W, :]) + _dot(nsw, w_ref[SB_W:, :])
        attn_ref[...] = attn
        u1 = ALPHA * x_ref[...] + (1.0 + vec_ref[V_G_A:V_G_A + 1, :]) * attn
        xhat, _ = _layer_norm_stats(u1)
        x1 = xhat * vec_ref[V_LN1G:V_LN1G + 1, :] + vec_ref[V_LN1B:V_LN1B + 1, :]
        x1_ref[...] = x1
        h2_ref[...] = (x1 * (1.0 + vec_ref[V_SC_F:V_SC_F + 1, :]) + vec_ref[V_SH_F:V_SH_F + 1, :]).astype(BF16)

    half = pl.BlockSpec((tb, SB_W), lambda i: (i, 0))
    full = pl.BlockSpec((tb, D), lambda i: (i, 0))
    return pl.pallas_call(
        body, name=name, grid=(s // tb,),
        out_shape=[jax.ShapeDtypeStruct((s, D), BF16), jax.ShapeDtypeStruct((s, D), F32),
                   jax.ShapeDtypeStruct((s, D), F32), jax.ShapeDtypeStruct((s, D), BF16)],
        in_specs=[half, half, full, _resident((VEC_ROWS, D)), _resident((D, D))],
        out_specs=[full, full, full, full],
        compiler_params=_cparams())(y_sb, y_sw, x, vec, w_out)


def _ffn_forward(h2, w_gu, w_down, name):
    s = h2.shape[0]
    tb = min(FFN_TILE, s)

    def body(h_ref, wgu_ref, wd_ref, gu_ref, act_ref, ffn_ref):
        gu = _dot(h_ref[...], wgu_ref[...])
        gu_ref[...] = gu.astype(BF16)
        gate, up = gu[:, :D_FF], gu[:, D_FF:]
        act = (gate * (1.0 / (1.0 + jnp.exp(-gate))) * up).astype(BF16)
        act_ref[...] = act
        ffn_ref[...] = _dot(act, wd_ref[...])

    return pl.pallas_call(
        body, name=name, grid=(s // tb,),
        out_shape=[jax.ShapeDtypeStruct((s, 2 * D_FF), BF16), jax.ShapeDtypeStruct((s, D_FF), BF16),
                   jax.ShapeDtypeStruct((s, D), F32)],
        in_specs=[pl.BlockSpec((tb, D), lambda i: (i, 0)), _resident((D, 2 * D_FF)), _resident((D_FF, D))],
        out_specs=[pl.BlockSpec((tb, 2 * D_FF), lambda i: (i, 0)), pl.BlockSpec((tb, D_FF), lambda i: (i, 0)),
                   pl.BlockSpec((tb, D), lambda i: (i, 0))],
        compiler_params=_cparams())(h2, w_gu, w_down)


def _layer_norm_bwd(dxhat, xhat, rstd):
    m1 = jnp.mean(dxhat, axis=1, keepdims=True)
    m2 = jnp.mean(dxhat * xhat, axis=1, keepdims=True)
    return rstd * (dxhat - m1 - xhat * m2)


def _colsum(a):
    return jnp.sum(a, axis=0, keepdims=True)


A_LN2G, A_LN2B, A_GF, A_SCF, A_SHF, A_LOSS = range(6)
B_LN1G, B_LN1B, B_GA, B_GN = range(4)
C_SCA, C_SHA = range(2)


def _ffn_backward(x1, ffn, target, gu, vec, w_gu, w_down, name):
    s = x1.shape[0]
    tb = min(FFN_BWD_TILE, s)

    def body(x1_ref, ffn_ref, t_ref, gu_ref, vec_ref, wgu_ref, wd_ref, dffn_ref, dgu_ref, dx1_ref, acc_ref):
        @pl.when(pl.program_id(0) == 0)
        def _():
            acc_ref[...] = jnp.zeros_like(acc_ref)

        x1v, ffn_v = x1_ref[...], ffn_ref[...]
        g_f = 1.0 + vec_ref[V_G_F:V_G_F + 1, :]
        u2 = ALPHA * x1v + g_f * ffn_v
        xhat, rstd = _layer_norm_stats(u2)
        ln_g = vec_ref[V_LN2G:V_LN2G + 1, :]
        err = xhat * ln_g + vec_ref[V_LN2B:V_LN2B + 1, :] - t_ref[...]
        dx2 = err * (1.0 / D)
        acc_ref[A_LOSS:A_LOSS + 1, :] += _colsum(err * err) * (0.5 / D)
        acc_ref[A_LN2G:A_LN2G + 1, :] += _colsum(dx2 * xhat)
        acc_ref[A_LN2B:A_LN2B + 1, :] += _colsum(dx2)
        du2 = _layer_norm_bwd(dx2 * ln_g, xhat, rstd)
        acc_ref[A_GF:A_GF + 1, :] += _colsum(du2 * ffn_v)
        dffn = (g_f * du2).astype(BF16)
        dffn_ref[...] = dffn
        dact = _dot_nt(dffn, wd_ref[...])
        gate, up = gu_ref[:, :D_FF].astype(F32), gu_ref[:, D_FF:].astype(F32)
        sg = 1.0 / (1.0 + jnp.exp(-gate))
        dgate = (dact * up * (sg * (1.0 + gate * (1.0 - sg)))).astype(BF16)
        dup = (dact * (gate * sg)).astype(BF16)
        dgu_ref[:, :D_FF] = dgate
        dgu_ref[:, D_FF:] = dup
        dh2 = _dot_nt(dgate, wgu_ref[:, :D_FF]) + _dot_nt(dup, wgu_ref[:, D_FF:])
        dx1_ref[...] = ALPHA * du2 + dh2 * (1.0 + vec_ref[V_SC_F:V_SC_F + 1, :])
        acc_ref[A_SCF:A_SCF + 1, :] += _colsum(dh2 * x1v)
        acc_ref[A_SHF:A_SHF + 1, :] += _colsum(dh2)

    full = pl.BlockSpec((tb, D), lambda i: (i, 0))
    wide = pl.BlockSpec((tb, 2 * D_FF), lambda i: (i, 0))
    return pl.pallas_call(
        body, name=name, grid=(s // tb,),
        out_shape=[jax.ShapeDtypeStruct((s, D), BF16), jax.ShapeDtypeStruct((s, 2 * D_FF), BF16),
                   jax.ShapeDtypeStruct((s, D), F32), jax.ShapeDtypeStruct((8, D), F32)],
        in_specs=[full, full, full, wide, _resident((VEC_ROWS, D)), _resident((D, 2 * D_FF)), _resident((D_FF, D))],
        out_specs=[full, wide, full, pl.BlockSpec((8, D), lambda i: (0, 0))],
        compiler_params=_cparams())(x1, ffn, target, gu, vec, w_gu, w_down)


def _attn_out_backward(dx1, x, attn, y_sb, y_sw, vec, w_out, name):
    s = x.shape[0]
    tb = min(TOK_TILE, s)

    def body(dx1_ref, x_ref, attn_ref, ysb_ref, ysw_ref, vec_ref, w_ref, du1_ref, dattn_ref, dy_ref, acc_ref):
        @pl.when(pl.program_id(0) == 0)
        def _():
            acc_ref[...] = jnp.zeros_like(acc_ref)

        attn = attn_ref[...]
        g_a = 1.0 + vec_ref[V_G_A:V_G_A + 1, :]
        xhat, rstd = _layer_norm_stats(ALPHA * x_ref[...] + g_a * attn)
        dx1v = dx1_ref[...]
        acc_ref[B_LN1G:B_LN1G + 1, :] += _colsum(dx1v * xhat)
        acc_ref[B_LN1B:B_LN1B + 1, :] += _colsum(dx1v)
        du1 = _layer_norm_bwd(dx1v * vec_ref[V_LN1G:V_LN1G + 1, :], xhat, rstd)
        du1_ref[...] = du1
        acc_ref[B_GA:B_GA + 1, :] += _colsum(du1 * attn)
        dattn = (g_a * du1).astype(BF16)
        dattn_ref[...] = dattn
        dmixed = _dot_nt(dattn, w_ref[...])
        for lo, y_ref in ((0, ysb_ref), (SB_W, ysw_ref)):
            y = y_ref[...]
            rr = _rms_parts(y)
            dn = dmixed[:, lo:lo + SB_W]
            acc_ref[B_GN:B_GN + 1, lo:lo + SB_W] += _colsum(dn * y * rr)
            dng = dn * vec_ref[V_GN:V_GN + 1, lo:lo + SB_W]
            dy_ref[:, lo:lo + SB_W] = rr * dng - y * (rr * rr * rr) * jnp.mean(dng * y, axis=1, keepdims=True)

    half = pl.BlockSpec((tb, SB_W), lambda i: (i, 0))
    full = pl.BlockSpec((tb, D), lambda i: (i, 0))
    return pl.pallas_call(
        body, name=name, grid=(s // tb,),
        out_shape=[jax.ShapeDtypeStruct((s, D), F32), jax.ShapeDtypeStruct((s, D), BF16),
                   jax.ShapeDtypeStruct((s, D), F32), jax.ShapeDtypeStruct((8, D), F32)],
        in_specs=[full, full, full, half, half, _resident((VEC_ROWS, D)), _resident((D, D))],
        out_specs=[full, full, full, pl.BlockSpec((8, D), lambda i: (0, 0))],
        compiler_params=_cparams())(dx1, x, attn, y_sb, y_sw, vec, w_out)


def _sb_backward(proj, sp_total, dy, slabs, name):
    s = proj.shape[0]
    tq, tk = min(SB_TQ, s), min(SB_TK, s)
    r = tq // tk
    nkb = SB_W // LANES
    nq = s // tq

    def body(q_ref, k_ref, v_ref, tot_ref, do_ref, slab_ref, dq_ref, dk_ref, dv_ref, got_ref,
             dq_acc, left_refs, gsum_refs, send_sems, recv_sems, local_sems):
        i = pl.program_id(1)
        step = pl.program_id(0) * nq + i
        scatter = _scatter_exchange(slab_ref, got_ref, send_sems, recv_sems, local_sems)

        @pl.when(step == 0)
        def _():
            scatter.start()

        @pl.when(i == 0)
        def _():
            dk_ref[...] = jnp.zeros_like(dk_ref)
            dv_ref[...] = jnp.zeros_like(dv_ref)

        lane = lax.broadcasted_iota(jnp.int32, (1, LANES), 1)
        first = lane < HEAD_DIM
        qp, dop, totp = q_ref[...], do_ref[...], tot_ref[...]
        zero = jnp.zeros((), BF16)
        qs = (jnp.where(first, qp, zero), jnp.where(first, zero, qp))
        dofs = (jnp.where(first, dop, 0.0), jnp.where(first, 0.0, dop))
        dobs = tuple(d.astype(BF16) for d in dofs)
        dots = tuple(d.T.astype(BF16) for d in dofs)
        qts = tuple(qh.astype(F32).T.astype(BF16) for qh in qs)
        later = _sum_matrix(tk, lambda row, col: row > col)
        earlier = _sum_matrix(tk, lambda row, col: row < col)
        dq_acc[...] = jnp.zeros_like(dq_acc)
        gsum_refs[...] = jnp.zeros_like(gsum_refs)
        swapped = pltpu.roll(totp, HEAD_DIM, 1)
        left_refs[0] = jnp.where(first, totp, swapped)
        left_refs[1] = jnp.where(first, swapped, totp)

        def blocks(js, masked):
            ks = [pl.multiple_of(j * tk, tk) for j in js]
            kjs = [k_ref[pl.ds(k0, tk), :] for k0 in ks]
            vjs = [v_ref[pl.ds(k0, tk), :] for k0 in ks]
            chains = [(hd, b) for b in range(len(js)) for hd in range(2)]
            zs = [_dot_nt(qs[hd], kjs[b]) for hd, b in chains]
            dws = [_dot_nt(dobs[hd], vjs[b]) for hd, b in chains]
            parts = [_softplus_parts(z) for z in zs]
            sps = [p[0] for p in parts]
            if masked:
                t_idx = i * tq + lax.broadcasted_iota(jnp.int32, (tq, tk), 0)
                befores = [j * tk + lax.broadcasted_iota(jnp.int32, (tq, tk), 1) < t_idx for j in js]
                spms = [jnp.where(befores[b], sp, 0.0) for (hd, b), sp in zip(chains, sps)]
            else:
                spms = sps
            cums = [_block_sums(spm, later) for spm in spms]
            lefts = [left_refs[0], left_refs[1]]
            ws = []
            for (hd, b), z, sp, (cum, sm) in zip(chains, zs, sps, cums):
                lefts[hd] = lefts[hd] - sm
                w = jnp.exp(z - sp - cum - _across(lefts[hd], tk))
                if masked:
                    w = jnp.where(befores[b], w, 0.0)
                ws.append(w)
            wbs = [w.astype(BF16) for w in ws]
            dvs = [_dot(dots[hd], wb) for (hd, b), wb in zip(chains, wbs)]
            gs = [dw * w for dw, w in zip(dws, ws)]
            gcums = [_block_sums(g, earlier) for g in gs]
            gsums = [gsum_refs[0], gsum_refs[1]]
            dzbs = []
            for (hd, b), z, (sp, e1), g, (gcum, gsm) in zip(chains, zs, parts, gs, gcums):
                inv = 1.0 / (1.0 + e1)
                sig = jnp.where(z >= 0.0, inv, e1 * inv)
                dz = g - sig * (g + _across(gsums[hd], tk) + gcum)
                if masked:
                    dz = jnp.where(befores[b], dz, 0.0)
                dzbs.append(dz.astype(BF16))
                gsums[hd] = gsums[hd] + gsm
            dqs = [_dot(dzb, kjs[b]) for (hd, b), dzb in zip(chains, dzbs)]
            dks = [_dot(qts[hd], dzb) for (hd, b), dzb in zip(chains, dzbs)]
            for b, j in enumerate(js):
                dv_ref[j] += dvs[2 * b] + dvs[2 * b + 1]
                dk_ref[j] += dks[2 * b] + dks[2 * b + 1]
            for hd in range(2):
                tot = dqs[hd]
                for b in range(1, len(js)):
                    tot = tot + dqs[2 * b + hd]
                dq_acc[hd] += tot
                left_refs[hd] = lefts[hd]
                gsum_refs[hd] = gsums[hd]

        below = i * r

        def sweep(n, carry):
            blocks([SB_UNROLL_BWD * n + u for u in range(SB_UNROLL_BWD)], False)
            return carry

        lax.fori_loop(0, below // SB_UNROLL_BWD, sweep, 0)
        for u in range(SB_UNROLL_BWD - 1, 0, -1):
            @pl.when(below % SB_UNROLL_BWD >= u)
            def _(u=u):
                blocks([below - u], False)
        blocks([below + d for d in range(r)], True)
        dq_ref[...] = jnp.where(first, dq_acc[0], dq_acc[1])

        @pl.when(step == nkb * nq - 1)
        def _():
            scatter.wait()

    shp = jax.ShapeDtypeStruct((s, SB_W), F32)
    qspec = pl.BlockSpec((tq, LANES), lambda p, i: (i, p))
    whole = pl.BlockSpec((None, s // tk, LANES, tk), lambda p, i: (p, 0, 0, 0))
    shp_t = jax.ShapeDtypeStruct((nkb, s // tk, LANES, tk), F32)
    hbm = pl.BlockSpec(memory_space=pl.ANY)
    return pl.pallas_call(
        body, name=name, grid=(nkb, nq),
        out_shape=[shp, shp_t, shp_t, jax.ShapeDtypeStruct(slabs.shape, slabs.dtype)],
        in_specs=[qspec,
                  pl.BlockSpec((s, LANES), lambda p, i: (0, nkb + p)),
                  pl.BlockSpec((s, LANES), lambda p, i: (0, 2 * nkb + p)),
                  qspec, qspec, hbm],
        out_specs=[qspec, whole, whole, hbm],
        scratch_shapes=[pltpu.VMEM((2, tq, LANES), F32), pltpu.VMEM((2, tq, LANES), F32), pltpu.VMEM((2, tq, LANES), F32)]
        + _exchange_sems(1),
        compiler_params=_cparams())(proj, proj, proj, sp_total, dy, slabs)


def _swa_backward(proj, y_sw, dy, sinks, name):
    s = proj.shape[0]
    nb = s // WINDOW
    qb, kb, vb = 3 * SB_W // SWA_QW, (3 * SB_W + SWA_QW) // LANES, (3 * SB_W + SWA_QW + SWA_KW) // LANES

    def body(q_ref, kp_ref, kc_ref, vp_ref, vc_ref, o_ref, do_ref, sink_ref, dq_ref, dk_ref, dv_ref, ds_ref):
        n = pl.program_id(0)

        @pl.when(n == 0)
        def _():
            dk_ref[...] = jnp.zeros_like(dk_ref)
            dv_ref[...] = jnp.zeros_like(dv_ref)
            ds_ref[...] = jnp.zeros_like(ds_ref)

        k = jnp.concatenate([kp_ref[...], kc_ref[...]], axis=0)
        v = jnp.concatenate([vp_ref[...], vc_ref[...]], axis=0)
        k_sw = pltpu.roll(k.astype(F32), HEAD_DIM, 1).astype(BF16)
        v_sw = pltpu.roll(v.astype(F32), HEAD_DIM, 1).astype(BF16)
        lane = lax.broadcasted_iota(jnp.int32, (1, LANES), 1)
        halves = [lane < HEAD_DIM, lane >= HEAD_DIM]
        valid, distf = _swa_masks(n)
        zero = jnp.zeros((2 * WINDOW, LANES), F32)
        dk_nat, dk_rot, dv_nat, dv_rot = zero, zero, zero, zero
        for pair in range(4):
            cols = slice(pair * LANES, (pair + 1) * LANES)
            qp = q_ref[:, cols]
            dop, op = do_ref[:, cols], o_ref[:, cols]
            dq = jnp.zeros((WINDOW, LANES), F32)
            for par in range(2):
                h = 2 * pair + par
                g = h // 4
                qm = jnp.where(halves[par], qp, jnp.zeros((), BF16))
                do = jnp.where(halves[par], dop, 0.0)
                dob = do.astype(BF16)
                delta = jnp.sum(do * op, axis=1, keepdims=True)
                ku, vu = (k, v) if g == par else (k_sw, v_sw)
                p, p_sink = _swa_probs(qm, ku, valid, distf, h, sink_ref[h])
                dsc = (p * (_dot_nt(dob, vu) - delta)).astype(BF16)
                ds_ref[h:h + 1, :] += jnp.zeros((1, LANES), F32) - jnp.sum(p_sink * delta)
                dq = dq + jnp.where(halves[par], _dot(dsc, ku), 0.0)
                dk_h = _dot_tn(dsc, qm)
                dv_h = _dot_tn(p.astype(BF16), dob)
                if g == par:
                    dk_nat, dv_nat = dk_nat + dk_h, dv_nat + dv_h
                else:
                    dk_rot, dv_rot = dk_rot + dk_h, dv_rot + dv_h
            dq_ref[:, cols] = dq
        dk = dk_nat + pltpu.roll(dk_rot, HEAD_DIM, 1)
        dv = dv_nat + pltpu.roll(dv_rot, HEAD_DIM, 1)
        prev = pl.multiple_of(jnp.maximum(n - 1, 0) * WINDOW, WINDOW)
        cur = pl.multiple_of(n * WINDOW, WINDOW)
        dk_ref[pl.ds(prev, WINDOW), :] += dk[:WINDOW]
        dv_ref[pl.ds(prev, WINDOW), :] += dv[:WINDOW]
        dk_ref[pl.ds(cur, WINDOW), :] += dk[WINDOW:]
        dv_ref[pl.ds(cur, WINDOW), :] += dv[WINDOW:]

    prev_blk = lambda n: jnp.maximum(n - 1, 0)
    wide = pl.BlockSpec((WINDOW, SWA_QW), lambda n: (n, 0))
    whole = pl.BlockSpec((s, LANES), lambda n: (0, 0))
    return pl.pallas_call(
        body, name=name, grid=(nb,),
        out_shape=[jax.ShapeDtypeStruct((s, SWA_QW), F32), jax.ShapeDtypeStruct((s, LANES), F32),
                   jax.ShapeDtypeStruct((s, LANES), F32), jax.ShapeDtypeStruct((8, LANES), F32)],
        in_specs=[pl.BlockSpec((WINDOW, SWA_QW), lambda n: (n, qb)),
                  pl.BlockSpec((WINDOW, LANES), lambda n: (prev_blk(n), kb)),
                  pl.BlockSpec((WINDOW, LANES), lambda n: (n, kb)),
                  pl.BlockSpec((WINDOW, LANES), lambda n: (prev_blk(n), vb)),
                  pl.BlockSpec((WINDOW, LANES), lambda n: (n, vb)),
                  wide,
                  pl.BlockSpec((WINDOW, SWA_QW), lambda n: (n, 1)),
                  pl.BlockSpec(memory_space=pltpu.SMEM)],
        out_specs=[wide, whole, whole, pl.BlockSpec((8, LANES), lambda n: (0, 0))],
        compiler_params=_cparams())(proj, proj, proj, proj, proj, y_sw, dy, sinks)


def _in_proj_backward(dq_sb, dkt_sb, dvt_sb, dq_sw, dk_sw, dv_sw, du1, x, vec, w_in, name):
    s = x.shape[0]
    tb = min(TOK_TILE, s)
    n_pairs, _, _, tk = dkt_sb.shape

    def body(dqsb_ref, dktsb_ref, dvtsb_ref, dqsw_ref, dksw_ref, dvsw_ref, du1_ref, x_ref, vec_ref, w_ref,
             dproj_ref, gx_ref, acc_ref, bacc_ref):
        @pl.when(pl.program_id(0) == 0)
        def _():
            acc_ref[...] = jnp.zeros_like(acc_ref)
            bacc_ref[...] = jnp.zeros_like(bacc_ref)

        pieces = ((0, dqsb_ref, QK_SCALE), (3 * SB_W, dqsw_ref, QK_SCALE), (3 * SB_W + SWA_QW, dksw_ref, 1.0),
                  (3 * SB_W + SWA_QW + SWA_KW, dvsw_ref, 1.0))
        for lo, ref, scale in pieces:
            width = ref.shape[1]
            piece = ref[...] * scale
            bacc_ref[0:1, lo:lo + width] += _colsum(piece)
            dproj_ref[:, lo:lo + width] = piece.astype(BF16)
        for base, ref in ((SB_W, dktsb_ref), (2 * SB_W, dvtsb_ref)):
            for p in range(n_pairs):
                lo = base + p * LANES
                for jj in range(tb // tk):
                    piece = ref[p, jj].T
                    bacc_ref[0:1, lo:lo + LANES] += _colsum(piece)
                    dproj_ref[jj * tk:(jj + 1) * tk, lo:lo + LANES] = piece.astype(BF16)
        dh = _dot_nt(dproj_ref[...], w_ref[...])
        xv = x_ref[...]
        gx_ref[...] = ALPHA * du1_ref[...] + dh * (1.0 + vec_ref[V_SC_A:V_SC_A + 1, :])
        acc_ref[C_SCA:C_SCA + 1, :] += _colsum(dh * xv)
        acc_ref[C_SHA:C_SHA + 1, :] += _colsum(dh)

    half = pl.BlockSpec((tb, SB_W), lambda i: (i, 0))
    narrow = pl.BlockSpec((tb, LANES), lambda i: (i, 0))
    full = pl.BlockSpec((tb, D), lambda i: (i, 0))
    blocks_t = pl.BlockSpec((n_pairs, tb // tk, LANES, tk), lambda i: (0, i, 0, 0))
    return pl.pallas_call(
        body, name=name, grid=(s // tb,),
        out_shape=[jax.ShapeDtypeStruct((s, D_IN), BF16), jax.ShapeDtypeStruct((s, D), F32),
                   jax.ShapeDtypeStruct((8, D), F32), jax.ShapeDtypeStruct((8, D_IN), F32)],
        in_specs=[half, blocks_t, blocks_t, half, narrow, narrow, full, full, _resident((VEC_ROWS, D)),
                  _resident((D, D_IN))],
        out_specs=[pl.BlockSpec((tb, D_IN), lambda i: (i, 0)), full, pl.BlockSpec((8, D), lambda i: (0, 0)),
                   pl.BlockSpec((8, D_IN), lambda i: (0, 0))],
        compiler_params=_cparams())(dq_sb, dkt_sb, dvt_sb, dq_sw, dk_sw, dv_sw, du1, x, vec, w_in)


def _weight_grad(a, b, name):
    s, m = a.shape
    n = b.shape[1]
    tn = 512 if n % 512 == 0 else n
    ts = min(512, s)

    def body(a_ref, b_ref, o_ref):
        @pl.when(pl.program_id(1) == 0)
        def _():
            o_ref[...] = jnp.zeros_like(o_ref)

        o_ref[...] += _dot_tn(a_ref[...], b_ref[...])

    return pl.pallas_call(
        body, name=name, grid=(n // tn, s // ts),
        out_shape=jax.ShapeDtypeStruct((m, n), F32),
        in_specs=[pl.BlockSpec((ts, m), lambda j, k: (k, 0)), pl.BlockSpec((ts, tn), lambda j, k: (k, j))],
        out_specs=pl.BlockSpec((m, tn), lambda j, k: (0, j)),
        compiler_params=_cparams())(a, b)


def _pad_rows(v, rows):
    return jnp.concatenate([v, jnp.zeros((rows - v.shape[0], v.shape[1]), v.dtype)], axis=0)


def _col_shards(w, n_shards):
    r, n = w.shape
    return w.reshape(r, n_shards, n // n_shards).transpose(1, 0, 2)


def kernel(x, c, w_ada, b_ada, w_in, b_in, sinks, gn_sb, gn_swa, w_out, ln1_g, ln1_b, w_gu, w_down, ln2_g, ln2_b, loss_target, m_w_ada, m_b_ada, m_w_in, m_b_in, m_sinks, m_gn_sb, m_gn_swa, m_w_out, m_ln1_g, m_ln1_b, m_w_gu, m_w_down, m_ln2_g, m_ln2_b, v_w_ada, v_b_ada, v_w_in, v_b_in, v_sinks, v_gn_sb, v_gn_swa, v_w_out, v_ln1_g, v_ln1_b, v_w_gu, v_w_down, v_ln2_g, v_ln2_b):
    ix, iy, ic = lax.axis_index("x"), lax.axis_index("y"), lax.axis_index("c")
    chip = 2 * ix + iy
    dev = 4 * ix + 2 * iy + ic
    xs, target = x[0], loss_target[0]
    s = xs.shape[0]

    c_all = _allgather8(_pad_rows(c, 8), "gather_c")[::8]
    n_ada = w_ada.shape[2]
    b_ada_shard = lax.dynamic_slice_in_dim(b_ada, chip * n_ada, n_ada, axis=1)
    mod_cols, silu_c = _mod_shard(c_all, w_ada[0], b_ada_shard, "mod_shard")
    mod_all = _allgather8(mod_cols, "gather_mod").reshape(N_DEV, 8, n_ada)
    mod_mine = lax.dynamic_index_in_dim(mod_all, dev, axis=1, keepdims=False)
    mod = mod_mine.reshape(N_CHIPS, 2, n_ada)[:, 0].reshape(6, D)
    vec = jnp.concatenate([mod, ln1_g, ln1_b, ln2_g, ln2_b, jnp.concatenate([gn_sb, gn_swa], axis=1),
                           jnp.zeros((VEC_ROWS - 11, D), F32)], axis=0)

    (g_in,) = _chip_allgather([w_in[0].astype(BF16)], "gather_w_in")
    w_in_b = g_in.transpose(1, 0, 2).reshape(D, D_IN)

    h_b, proj = _in_proj(xs, vec, w_in_b, b_in, "in_proj")
    y_sb, sp_total, g_out, g_gu, g_down = _sb_forward(
        proj, [w_out[0].astype(BF16), w_gu[0].astype(BF16), w_down[0].astype(BF16)], "sb_forward")
    w_gu_b = g_gu.transpose(1, 0, 2).reshape(D, 2 * D_FF)
    w_out_b = g_out.reshape(D, D)
    w_down_b = g_down.reshape(D_FF, D)
    sink_vec = sinks[0]
    y_sw = _swa_forward(proj, sink_vec, "swa_forward")
    mixed_b, attn, x1, h2_b = _post_attention(y_sb, y_sw, xs, vec, w_out_b, "post_attention")
    gu, act_b, ffn = _ffn_forward(h2_b, w_gu_b, w_down_b, "ffn_forward")

    def chip_sums(per_shard, tag):
        flat = jnp.concatenate(per_shard, axis=1)
        rows = flat.shape[1] // (2 * LANES)
        halves = flat.reshape(4, 2, rows, LANES)
        keep = lax.dynamic_index_in_dim(halves, ic, axis=1, keepdims=False)
        give = lax.dynamic_index_in_dim(halves, 1 - ic, axis=1, keepdims=False)
        got = _sibling_send(give, "grad_halves_swap_" + tag)
        both = _add2(keep.reshape(4 * rows, LANES), got.reshape(4 * rows, LANES), "grad_chip_sum_" + tag)
        return both.reshape(4, rows, LANES)

    dffn_b, dgu_b, dx1, acc_f = _ffn_backward(x1, ffn, target, gu, vec, w_gu_b, w_down_b, "ffn_backward")
    dw_gu = _weight_grad(h2_b, dgu_b, "grad_w_gu")
    dw_down = _weight_grad(act_b, dffn_b, "grad_w_down")
    sums_f = chip_sums([_col_shards(dw_gu, 4).reshape(4, -1), dw_down.reshape(4, -1)], "ffn")
    du1, dattn_b, dy, acc_a = _attn_out_backward(dx1, xs, attn, y_sb, y_sw, vec, w_out_b, "attn_out_backward")
    dq_sb, dk_sb, dv_sb, parts_f = _sb_backward(proj, sp_total, dy, sums_f, "sb_backward")
    dq_sw, dk_sw, dv_sw, dsink = _swa_backward(proj, y_sw, dy, sink_vec, "swa_backward")
    dproj_b, grad_x, acc_i, acc_b = _in_proj_backward(dq_sb, dk_sb, dv_sb, dq_sw, dk_sw, dv_sw, du1, xs, vec, w_in_b,
                                                      "in_proj_backward")
    dw_in = _weight_grad(h_b, dproj_b, "grad_w_in")
    dw_out = _weight_grad(mixed_b, dattn_b, "grad_w_out")
    sums_a = chip_sums([_col_shards(dw_in, 4).reshape(4, -1), dw_out.reshape(4, -1)], "attn")
    parts_a = _chip_scatter(sums_a, "grad_chip_scatter_attn")
    my_half = jnp.concatenate([_sum4(parts_f, "grad_reduce_ffn"), _sum4(parts_a, "grad_reduce_attn")], axis=0)
    other_half = _sibling_send(my_half, "grad_half_return")
    rows_f = parts_f.shape[1]

    def whole_shard(lo, hi):
        mine, other = my_half[lo:hi], other_half[lo:hi]
        return jnp.concatenate([jnp.where(ic == 0, mine, other), jnp.where(ic == 0, other, mine)], axis=0).reshape(-1)

    flat_f, flat_a = whole_shard(0, rows_f), whole_shard(rows_f, my_half.shape[0])
    n_gu, n_in = D * (2 * D_FF // 4), D * (D_IN // 4)
    gw_gu = flat_f[:n_gu].reshape(D, 2 * D_FF // 4)
    gw_down = flat_f[n_gu:].reshape(D_FF // 4, D)
    gw_in = flat_a[:n_in].reshape(D, D_IN // 4)
    gw_out = flat_a[n_in:].reshape(D // 4, D)

    dmod = jnp.concatenate([acc_i[C_SHA:C_SHA + 1], acc_i[C_SCA:C_SCA + 1], acc_a[B_GA:B_GA + 1],
                            acc_f[A_SHF:A_SHF + 1], acc_f[A_SCF:A_SCF + 1], acc_f[A_GF:A_GF + 1]], axis=1)
    dsink_row = jnp.concatenate([dsink[:, 0].reshape(1, 8), jnp.zeros((1, LANES - 8), F32)], axis=1)
    loss_row = jnp.concatenate([jnp.sum(acc_f[A_LOSS:A_LOSS + 1], axis=1, keepdims=True),
                                jnp.zeros((1, LANES - 1), F32)], axis=1)
    small = jnp.concatenate([dmod, acc_b[0:1], acc_a[B_LN1G:B_LN1G + 1], acc_a[B_LN1B:B_LN1B + 1],
                             acc_f[A_LN2G:A_LN2G + 1], acc_f[A_LN2B:A_LN2B + 1], acc_a[B_GN:B_GN + 1],
                             dsink_row, loss_row], axis=1)
    small_all = _allgather8(_pad_rows(small, 8), "gather_small")[::8]

    def pack_small(b_ada_, b_in_, ln1g_, ln1b_, ln2g_, ln2b_, gsb_, gsw_, sinks_):
        return jnp.concatenate([b_ada_, b_in_, ln1g_, ln1b_, ln2g_, ln2b_, gsb_, gsw_, sinks_,
                                jnp.ones((1, 2 * LANES - 8), F32)], axis=1)

    w_small = pack_small(b_ada, b_in, ln1_g, ln1_b, ln2_g, ln2_b, gn_sb, gn_swa, sinks)
    m_small = pack_small(m_b_ada, m_b_in, m_ln1_g, m_ln1_b, m_ln2_g, m_ln2_b, m_gn_sb, m_gn_swa, m_sinks)
    v_small = pack_small(v_b_ada, v_b_in, v_ln1_g, v_ln1_b, v_ln2_g, v_ln2_b, v_gn_sb, v_gn_swa, v_sinks)
    small_out = _small_update(small_all, w_small, m_small, v_small, "small_update")

    def unpack_small(row):
        return {"b_ada": row[:, SM_MOD:SM_BIN], "b_in": row[:, SM_BIN:SM_LN1G], "ln1_g": row[:, SM_LN1G:SM_LN1B],
                "ln1_b": row[:, SM_LN1B:SM_LN2G], "ln2_g": row[:, SM_LN2G:SM_LN2B], "ln2_b": row[:, SM_LN2B:SM_GN],
                "gn_sb": row[:, SM_GN:SM_GN + SB_W], "gn_swa": row[:, SM_GN + SB_W:SM_SINK],
                "sinks": row[:, SM_SINK:SM_SINK + 8]}

    g_small, d_small, m2_small, v2_small = [unpack_small(r) for r in small_out]
    loss = small_out[0][0, SM_LOSS]

    dmod_cols = lax.dynamic_slice_in_dim(small_all[:, SM_MOD:SM_BIN], chip * n_ada, n_ada, axis=1)
    gw_ada = _weight_grad(_pad_rows(silu_c, LANES).astype(BF16), _pad_rows(dmod_cols, LANES).astype(BF16), "grad_w_ada")

    big = {}
    for nm, w, g, m, v in (("w_ada", w_ada, gw_ada, m_w_ada, v_w_ada), ("w_in", w_in, gw_in, m_w_in, v_w_in),
                           ("w_out", w_out, gw_out, m_w_out, v_w_out), ("w_gu", w_gu, gw_gu, m_w_gu, v_w_gu),
                           ("w_down", w_down, gw_down, m_w_down, v_w_down)):
        d_, m2_, v2_ = _adamw(w[0], g, m[0], v[0], "adamw_" + nm)
        big[nm] = (g[None], d_[None], m2_[None], v2_[None])

    order = ["w_ada", "b_ada", "w_in", "b_in", "sinks", "gn_sb", "gn_swa", "w_out", "ln1_g", "ln1_b", "w_gu", "w_down",
             "ln2_g", "ln2_b"]

    def leaf(nm, which):
        if nm in big:
            return big[nm][which]
        return (g_small, d_small, m2_small, v2_small)[which][nm]

    outs = [loss, grad_x[None]]
    for which in range(4):
        outs += [leaf(nm, which) for nm in order]
    return tuple(outs)
```

```python
import functools
import math

import jax
import jax.numpy as jnp
from jax import lax
from jax.experimental import pallas as pl
from jax.experimental.pallas import tpu as pltpu

F32 = jnp.float32
BF16 = jnp.bfloat16

D = 1024
HEAD_DIM = 64
SB_W = 512
SWA_QW = 512
SWA_KW = 128
D_IN = 2304
D_FF = 2816
WINDOW = 128
ALPHA = 2.0 ** 0.25
LN_EPS = 1e-5
RMS_EPS = 1e-6
MASK_VALUE = -1e30
QK_SCALE = 1.0 / math.sqrt(HEAD_DIM)

ADAM_LR = 0.001
ADAM_B1 = 0.9
ADAM_B2 = 0.999
ADAM_EPS = 1e-08
ADAM_WD = 0.01
ADAM_STEP = 10

N_CHIPS = 4
N_DEV = 8
LANES = 128

SB_TQ = 512
SB_TK = 256
SB_UNROLL = 4
SB_UNROLL_BWD = 2
TOK_TILE = 512
FFN_TILE = 256
FFN_BWD_TILE = 256
VMEM_LIMIT = 56 * 1024 * 1024

V_SH_A, V_SC_A, V_G_A, V_SH_F, V_SC_F, V_G_F, V_LN1G, V_LN1B, V_LN2G, V_LN2B, V_GN = range(11)
VEC_ROWS = 16

SM_MOD = 0
SM_BIN = 6 * D
SM_LN1G = SM_BIN + D_IN
SM_LN1B = SM_LN1G + D
SM_LN2G = SM_LN1B + D
SM_LN2B = SM_LN2G + D
SM_GN = SM_LN2B + D
SM_SINK = SM_GN + D
SM_LOSS = SM_SINK + LANES
SM_LEN = SM_LOSS + LANES

MESH = pl.DeviceIdType.MESH


def _cparams(**kw):
    return pltpu.CompilerParams(vmem_limit_bytes=VMEM_LIMIT, **kw)


def _resident(shape):
    nd = len(shape)
    return pl.BlockSpec(shape, lambda *_: (0,) * nd, pipeline_mode=pl.Buffered(1))


def _dot(a, b):
    return jnp.dot(a, b, preferred_element_type=F32)


def _dot_nt(a, b):
    return lax.dot_general(a, b, (((1,), (1,)), ((), ())), preferred_element_type=F32)


def _dot_tn(a, b):
    return lax.dot_general(a, b, (((0,), (0,)), ((), ())), preferred_element_type=F32)


def _sum_matrix(tk, keep):
    row = lax.broadcasted_iota(jnp.int32, (tk, tk + LANES), 0)
    col = lax.broadcasted_iota(jnp.int32, (tk, tk + LANES), 1)
    return (keep(row, col) | (col >= tk)).astype(BF16)


def _block_sums(x, m):
    tk = x.shape[1]
    res = _dot(x.astype(BF16), m)
    return res[:, :tk], res[:, tk:]


def _across(v, tk):
    return jnp.concatenate([v] * (tk // LANES), axis=1)


def _allgather8(v, name):
    m_per, n = v.shape

    def body(x_ref, out_ref, send_sems, recv_sems, local_sem):
        x, y, c = lax.axis_index("x"), lax.axis_index("y"), lax.axis_index("c")
        me, sibling = (x, y, c), (x, y, 1 - c)
        chips = [(1 - x, y), (x, 1 - y), (1 - x, 1 - y)]

        def rows(px, py, pc):
            return out_ref.at[pl.ds((4 * px + 2 * py + pc) * m_per, m_per), :]

        def copy(k, block, to, src=None):
            return pltpu.make_async_remote_copy(
                src_ref=rows(*block) if src is None else src, dst_ref=rows(*block),
                send_sem=send_sems.at[k], recv_sem=recv_sems.at[k], device_id=to, device_id_type=MESH)

        mine = pltpu.make_async_copy(x_ref, rows(*me), local_sem)
        mine.start()
        first = [copy(0, me, sibling, src=x_ref)]
        first += [copy(1 + j, me, (*chip, c), src=x_ref) for j, chip in enumerate(chips)]
        for cp in first:
            cp.start()
        passed = [copy(4 + j, (*chip, c), sibling) for j, chip in enumerate(chips)]
        for j, chip in enumerate(chips):
            copy(1 + j, (*chip, c), me).wait_recv()
            passed[j].start()
        copy(0, sibling, me).wait_recv()
        for j, chip in enumerate(chips):
            copy(4 + j, (*chip, 1 - c), me).wait_recv()
        for cp in first + passed:
            cp.wait_send()
        mine.wait()

    return pl.pallas_call(
        body, name=name,
        out_shape=jax.ShapeDtypeStruct((N_DEV * m_per, n), v.dtype),
        in_specs=[pl.BlockSpec(memory_space=pltpu.VMEM)],
        out_specs=pl.BlockSpec(memory_space=pltpu.VMEM),
        scratch_shapes=[pltpu.SemaphoreType.DMA((7,)), pltpu.SemaphoreType.DMA((7,)), pltpu.SemaphoreType.DMA],
        compiler_params=_cparams(),
    )(v)


class _Exchange:
    def __init__(self, local, sends, arrivals):
        self.local, self.sends, self.arrivals = local, sends, arrivals

    def start(self):
        for cp in self.local + self.sends:
            cp.start()

    def wait(self):
        for cp in self.arrivals:
            cp.wait_recv()
        for cp in self.sends:
            cp.wait_send()
        for cp in self.local:
            cp.wait()


def _exchange_sems(n):
    return [pltpu.SemaphoreType.DMA((3 * n,)), pltpu.SemaphoreType.DMA((3 * n,)), pltpu.SemaphoreType.DMA((n,))]


def _gather_exchange(ins, outs, send_sems, recv_sems, local_sems):
    x, y, c = lax.axis_index("x"), lax.axis_index("y"), lax.axis_index("c")
    slot = 2 * x + y
    chips = [(1 - x, y), (x, 1 - y), (1 - x, 1 - y)]
    local, sends, arrivals = [], [], []
    for a in range(len(ins)):
        local.append(pltpu.make_async_copy(ins[a], outs[a].at[slot], local_sems.at[a]))
        for j, (px, py) in enumerate(chips):
            sems = dict(send_sem=send_sems.at[3 * a + j], recv_sem=recv_sems.at[3 * a + j],
                        device_id=(px, py, c), device_id_type=MESH)
            sends.append(pltpu.make_async_remote_copy(src_ref=ins[a], dst_ref=outs[a].at[slot], **sems))
            arrivals.append(pltpu.make_async_remote_copy(src_ref=ins[a], dst_ref=outs[a].at[2 * px + py], **sems))
    return _Exchange(local, sends, arrivals)


def _scatter_exchange(p_ref, out_ref, send_sems, recv_sems, local_sems):
    x, y, c = lax.axis_index("x"), lax.axis_index("y"), lax.axis_index("c")
    slot = 2 * x + y
    chips = [(1 - x, y), (x, 1 - y), (1 - x, 1 - y)]
    local = [pltpu.make_async_copy(p_ref.at[slot], out_ref.at[slot], local_sems.at[0])]
    sends, arrivals = [], []
    for j, (px, py) in enumerate(chips):
        sems = dict(send_sem=send_sems.at[j], recv_sem=recv_sems.at[j], device_id=(px, py, c), device_id_type=MESH)
        sends.append(pltpu.make_async_remote_copy(src_ref=p_ref.at[2 * px + py], dst_ref=out_ref.at[slot], **sems))
        arrivals.append(pltpu.make_async_remote_copy(src_ref=p_ref.at[slot], dst_ref=out_ref.at[2 * px + py], **sems))
    return _Exchange(local, sends, arrivals)


def _chip_allgather(arrs, name):
    n = len(arrs)

    def body(*refs):
        ex = _gather_exchange(refs[:n], refs[n:2 * n], *refs[2 * n:])
        ex.start()
        ex.wait()

    hbm = pl.BlockSpec(memory_space=pl.ANY)
    return pl.pallas_call(
        body, name=name,
        out_shape=[jax.ShapeDtypeStruct((N_CHIPS,) + a.shape, a.dtype) for a in arrs],
        in_specs=[hbm] * n, out_specs=[hbm] * n,
        scratch_shapes=_exchange_sems(n),
        compiler_params=_cparams(),
    )(*arrs)


def _sibling_send(v, name):
    def body(v_ref, out_ref, send_sem, recv_sem):
        x, y, c = lax.axis_index("x"), lax.axis_index("y"), lax.axis_index("c")
        cp = pltpu.make_async_remote_copy(src_ref=v_ref, dst_ref=out_ref, send_sem=send_sem, recv_sem=recv_sem,
                                          device_id=(x, y, 1 - c), device_id_type=MESH)
        cp.start()
        cp.wait()

    hbm = pl.BlockSpec(memory_space=pl.ANY)
    return pl.pallas_call(
        body, name=name, out_shape=jax.ShapeDtypeStruct(v.shape, v.dtype),
        in_specs=[hbm], out_specs=hbm,
        scratch_shapes=[pltpu.SemaphoreType.DMA, pltpu.SemaphoreType.DMA],
        compiler_params=_cparams(),
    )(v)


def _chip_scatter(p, name):
    def body(p_ref, out_ref, send_sems, recv_sems, local_sems):
        ex = _scatter_exchange(p_ref, out_ref, send_sems, recv_sems, local_sems)
        ex.start()
        ex.wait()

    hbm = pl.BlockSpec(memory_space=pl.ANY)
    return pl.pallas_call(
        body, name=name, out_shape=jax.ShapeDtypeStruct(p.shape, p.dtype),
        in_specs=[hbm], out_specs=hbm,
        scratch_shapes=_exchange_sems(1),
        compiler_params=_cparams(),
    )(p)


def _add2(a, b, name):
    rows = a.shape[0]
    tr = rows // 8

    def body(a_ref, b_ref, o_ref):
        o_ref[...] = a_ref[...] + b_ref[...]

    spec = pl.BlockSpec((tr, LANES), lambda i: (i, 0))
    return pl.pallas_call(body, name=name, grid=(rows // tr,), out_shape=jax.ShapeDtypeStruct(a.shape, a.dtype),
                          in_specs=[spec, spec], out_specs=spec, compiler_params=_cparams())(a, b)


def _sum4(p, name):
    rows = p.shape[1]
    tr = rows // 8

    def body(p_ref, o_ref):
        o_ref[...] = ((p_ref[0] + p_ref[1]) + p_ref[2]) + p_ref[3]

    return pl.pallas_call(
        body, name=name, grid=(rows // tr,), out_shape=jax.ShapeDtypeStruct((rows, LANES), p.dtype),
        in_specs=[pl.BlockSpec((4, tr, LANES), lambda i: (0, i, 0))],
        out_specs=pl.BlockSpec((tr, LANES), lambda i: (i, 0)), compiler_params=_cparams())(p)


def _adam_math(w, g, m, v):
    m2 = ADAM_B1 * m + (1.0 - ADAM_B1) * g
    v2 = ADAM_B2 * v + (1.0 - ADAM_B2) * (g * g)
    m_hat = m2 / (1.0 - ADAM_B1 ** ADAM_STEP)
    v_hat = v2 / (1.0 - ADAM_B2 ** ADAM_STEP)
    delta = -ADAM_LR * (m_hat / (jnp.sqrt(v_hat) + ADAM_EPS) + ADAM_WD * w)
    return delta, m2, v2


def _adamw(w, g, m, v, name):
    rows, cols = w.shape
    tr = rows // 4 if rows % 32 == 0 else rows

    def body(w_ref, g_ref, m_ref, v_ref, d_ref, m2_ref, v2_ref):
        delta, m2, v2 = _adam_math(w_ref[...], g_ref[...], m_ref[...], v_ref[...])
        d_ref[...] = delta
        m2_ref[...] = m2
        v2_ref[...] = v2

    spec = pl.BlockSpec((tr, cols), lambda i: (i, 0))
    shp = jax.ShapeDtypeStruct(w.shape, F32)
    return pl.pallas_call(body, name=name, grid=(rows // tr,), out_shape=[shp, shp, shp],
                          in_specs=[spec] * 4, out_specs=[spec] * 3, compiler_params=_cparams())(w, g, m, v)


def _small_update(g8, w, m, v, name):
    n = w.shape[1]

    def body(g8_ref, w_ref, m_ref, v_ref, g_ref, d_ref, m2_ref, v2_ref):
        g = g8_ref[0:1, :]
        for r in range(1, N_DEV):
            g = g + g8_ref[r:r + 1, :]
        delta, m2, v2 = _adam_math(w_ref[...], g, m_ref[...], v_ref[...])
        g_ref[...] = g
        d_ref[...] = delta
        m2_ref[...] = m2
        v2_ref[...] = v2

    shp = jax.ShapeDtypeStruct((1, n), F32)
    vm = pl.BlockSpec(memory_space=pltpu.VMEM)
    return pl.pallas_call(body, name=name, out_shape=[shp] * 4, in_specs=[vm] * 4, out_specs=[vm] * 4,
                          compiler_params=_cparams())(g8, w, m, v)


def _mod_shard(c8, w_ada, b_ada_shard, name):
    n = w_ada.shape[1]
    tn = 512

    def body(c_ref, w_ref, b_ref, o_ref, s_ref):
        cv = c_ref[...]
        sc = cv * (1.0 / (1.0 + jnp.exp(-cv)))
        s_ref[...] = sc
        o_ref[...] = _dot(sc.astype(BF16), w_ref[...].astype(BF16)) + b_ref[...]

    return pl.pallas_call(
        body, name=name, grid=(n // tn,),
        out_shape=[jax.ShapeDtypeStruct((8, n), F32), jax.ShapeDtypeStruct((8, D), F32)],
        in_specs=[pl.BlockSpec((8, D), lambda j: (0, 0)), pl.BlockSpec((D, tn), lambda j: (0, j)),
                  pl.BlockSpec((1, tn), lambda j: (0, j))],
        out_specs=[pl.BlockSpec((8, tn), lambda j: (0, j)), pl.BlockSpec((8, D), lambda j: (0, 0))],
        compiler_params=_cparams())(c8, w_ada, b_ada_shard)


def _layer_norm_stats(u):
    mu = jnp.mean(u, axis=1, keepdims=True)
    d = u - mu
    var = jnp.mean(d * d, axis=1, keepdims=True)
    rstd = lax.rsqrt(var + LN_EPS)
    return d * rstd, rstd


def _in_proj(x, vec, w_in, b_in, name):
    s = x.shape[0]
    tb = min(TOK_TILE, s)

    def body(x_ref, vec_ref, w_ref, b_ref, ht_ref, p_ref):
        h = x_ref[...] * (1.0 + vec_ref[V_SC_A:V_SC_A + 1, :]) + vec_ref[V_SH_A:V_SH_A + 1, :]
        hb = h.astype(BF16)
        ht_ref[...] = h.T.astype(BF16)
        proj = _dot(hb, w_ref[...]) + b_ref[...]
        col = lax.broadcasted_iota(jnp.int32, (1, D_IN), 1)
        is_q = (col < SB_W) | ((col >= 3 * SB_W) & (col < 3 * SB_W + SWA_QW))
        p_ref[...] = (proj * jnp.where(is_q, QK_SCALE, 1.0)).astype(BF16)

    return pl.pallas_call(
        body, name=name, grid=(s // tb,),
        out_shape=[jax.ShapeDtypeStruct((D, s), BF16), jax.ShapeDtypeStruct((s, D_IN), BF16)],
        in_specs=[pl.BlockSpec((tb, D), lambda i: (i, 0)), _resident((VEC_ROWS, D)), _resident((D, D_IN)),
                  _resident((1, D_IN))],
        out_specs=[pl.BlockSpec((D, tb), lambda i: (0, i)), pl.BlockSpec((tb, D_IN), lambda i: (i, 0))],
        compiler_params=_cparams())(x, vec, w_in, b_in)


def _softplus_parts(z):
    e1 = jnp.exp(-jnp.abs(z))
    sp = jnp.maximum(z, 0.0) + jnp.log(1.0 + e1)
    return sp, e1


def _sb_forward(proj, shards, name):
    s = proj.shape[0]
    tq, tk = min(SB_TQ, s), min(SB_TK, s)
    r = tq // tk

    n_sh = len(shards)
    nkb = SB_W // LANES
    nq = s // tq

    def body(q_ref, k_ref, v_ref, *refs):
        sh_refs, (o_ref, tot_ref), got_refs = refs[:n_sh], refs[n_sh:n_sh + 2], refs[n_sh + 2:2 * n_sh + 2]
        acc_refs, run_refs = refs[2 * n_sh + 2:2 * n_sh + 4]
        i = pl.program_id(1)
        step = pl.program_id(0) * nq + i
        gather = _gather_exchange(sh_refs, got_refs, *refs[2 * n_sh + 4:])

        @pl.when(step == 0)
        def _():
            gather.start()

        lane = lax.broadcasted_iota(jnp.int32, (1, LANES), 1)
        first = lane < HEAD_DIM
        qp = q_ref[...]
        zero = jnp.zeros((), BF16)
        qs = (jnp.where(first, qp, zero), jnp.where(first, zero, qp))
        later = _sum_matrix(tk, lambda row, col: row > col)
        acc_refs[...] = jnp.zeros_like(acc_refs)
        run_refs[...] = jnp.zeros_like(run_refs)

        def blocks(js, masked):
            ks = [pl.multiple_of(j * tk, tk) for j in js]
            kjs = [k_ref[pl.ds(k0, tk), :] for k0 in ks]
            vjs = [v_ref[pl.ds(k0, tk), :] for k0 in ks]
            chains = [(hd, b) for b in range(len(js)) for hd in range(2)]
            zs = [_dot_nt(qs[hd], kjs[b]) for hd, b in chains]
            sps = [_softplus_parts(z)[0] for z in zs]
            if masked:
                t_idx = i * tq + lax.broadcasted_iota(jnp.int32, (tq, tk), 0)
                befores = [j * tk + lax.broadcasted_iota(jnp.int32, (tq, tk), 1) < t_idx for j in js]
                spms = [jnp.where(befores[b], sp, 0.0) for (hd, b), sp in zip(chains, sps)]
            else:
                spms = sps
            cums = [_block_sums(spm, later) for spm in spms]
            runs = [run_refs[0], run_refs[1]]
            ws = []
            for (hd, b), z, sp, (cum, sm) in zip(chains, zs, sps, cums):
                w = jnp.exp(z - sp - cum - _across(runs[hd], tk))
                if masked:
                    w = jnp.where(befores[b], w, 0.0)
                ws.append(w.astype(BF16))
                runs[hd] = runs[hd] + sm
            pvs = [_dot(w, vjs[b]) for (hd, b), w in zip(chains, ws)]
            for hd in range(2):
                tot = pvs[hd]
                for b in range(1, len(js)):
                    tot = tot + pvs[2 * b + hd]
                acc_refs[hd] += tot
                run_refs[hd] = runs[hd]

        blocks([i * r + (r - 1 - d) for d in range(r)], True)

        below = i * r

        def sweep(n, carry):
            top = below - 1 - SB_UNROLL * n
            blocks([top - u for u in range(SB_UNROLL)], False)
            return carry

        lax.fori_loop(0, below // SB_UNROLL, sweep, 0)
        size, left = SB_UNROLL // 2, below % SB_UNROLL
        while size >= 1:
            @pl.when(left % (2 * size) >= size)
            def _(size=size):
                top = left % (2 * size) - 1
                blocks([top - u for u in range(size)], False)
            size //= 2
        o_ref[...] = jnp.where(first, acc_refs[0], acc_refs[1])
        tot_ref[...] = jnp.where(first, run_refs[0], run_refs[1])

        @pl.when(step == nkb * nq - 1)
        def _():
            gather.wait()

    shp = jax.ShapeDtypeStruct((s, SB_W), F32)
    qspec = pl.BlockSpec((tq, LANES), lambda p, i: (i, p))
    hbm = pl.BlockSpec(memory_space=pl.ANY)
    return pl.pallas_call(
        body, name=name, grid=(nkb, nq),
        out_shape=[shp, shp] + [jax.ShapeDtypeStruct((N_CHIPS,) + a.shape, a.dtype) for a in shards],
        in_specs=[qspec,
                  pl.BlockSpec((s, LANES), lambda p, i: (0, nkb + p)),
                  pl.BlockSpec((s, LANES), lambda p, i: (0, 2 * nkb + p))] + [hbm] * n_sh,
        out_specs=[qspec, qspec] + [hbm] * n_sh,
        scratch_shapes=[pltpu.VMEM((2, tq, LANES), F32), pltpu.VMEM((2, tq, LANES), F32)] + _exchange_sems(n_sh),
        compiler_params=_cparams())(proj, proj, proj, *shards)


def _swa_masks(n):
    ti = lax.broadcasted_iota(jnp.int32, (WINDOW, 2 * WINDOW), 0)
    kj = lax.broadcasted_iota(jnp.int32, (WINDOW, 2 * WINDOW), 1)
    dist = ti + WINDOW - kj
    valid = (dist >= 0) & (dist < WINDOW) & ((n * WINDOW - WINDOW + kj) >= 0)
    return valid, dist.astype(F32)


def _swa_probs(sc, valid, distf, h, sink):
    slope = 2.0 ** (-(h + 1))
    sc = jnp.where(valid, sc - slope * distf, MASK_VALUE)
    mx = jnp.maximum(jnp.max(sc, axis=1, keepdims=True), sink)
    p = jnp.exp(sc - mx)
    es = jnp.exp(sink - mx)
    inv = 1.0 / (jnp.sum(p, axis=1, keepdims=True) + es)
    return p * inv, es * inv


def _swa_forward(proj, sinks, name):
    s = proj.shape[0]
    nb = s // WINDOW
    qb, kb, vb = 3 * SB_W // SWA_QW, (3 * SB_W + SWA_QW) // LANES, (3 * SB_W + SWA_QW + SWA_KW) // LANES

    def body(q_ref, kp_ref, kc_ref, vp_ref, vc_ref, sink_ref, o_ref):
        n = pl.program_id(0)
        k = jnp.concatenate([kp_ref[...], kc_ref[...]], axis=0)
        v = jnp.concatenate([vp_ref[...], vc_ref[...]], axis=0)
        k_sw = pltpu.roll(k.astype(F32), HEAD_DIM, 1).astype(BF16)
        v_sw = pltpu.roll(v.astype(F32), HEAD_DIM, 1).astype(BF16)
        lane = lax.broadcasted_iota(jnp.int32, (1, LANES), 1)
        halves = [lane < HEAD_DIM, lane >= HEAD_DIM]
        valid, distf = _swa_masks(n)
        heads = range(2 * 4)
        qms = [jnp.where(halves[h % 2], q_ref[:, (h // 2) * LANES:(h // 2 + 1) * LANES], jnp.zeros((), BF16))
               for h in heads]
        kus = [k if h // 4 == h % 2 else k_sw for h in heads]
        vus = [v if h // 4 == h % 2 else v_sw for h in heads]
        scores = [_dot_nt(qms[h], kus[h]) for h in heads]
        ps = [_swa_probs(scores[h], valid, distf, h, sink_ref[h])[0].astype(BF16) for h in heads]
        outs = [_dot(ps[h], vus[h]) for h in heads]
        for pair in range(4):
            o_ref[:, pair * LANES:(pair + 1) * LANES] = jnp.where(halves[0], outs[2 * pair], outs[2 * pair + 1])

    prev = lambda n: jnp.maximum(n - 1, 0)
    return pl.pallas_call(
        body, name=name, grid=(nb,),
        out_shape=jax.ShapeDtypeStruct((s, SWA_QW), F32),
        in_specs=[pl.BlockSpec((WINDOW, SWA_QW), lambda n: (n, qb)),
                  pl.BlockSpec((WINDOW, LANES), lambda n: (prev(n), kb)),
                  pl.BlockSpec((WINDOW, LANES), lambda n: (n, kb)),
                  pl.BlockSpec((WINDOW, LANES), lambda n: (prev(n), vb)),
                  pl.BlockSpec((WINDOW, LANES), lambda n: (n, vb)),
                  pl.BlockSpec(memory_space=pltpu.SMEM)],
        out_specs=pl.BlockSpec((WINDOW, SWA_QW), lambda n: (n, 0)),
        compiler_params=_cparams())(proj, proj, proj, proj, proj, sinks)


def _rms_parts(y):
    return lax.rsqrt(jnp.mean(y * y, axis=1, keepdims=True) + RMS_EPS)


def _post_attention(y_sb, y_sw, x, vec, w_out, name):
    s = x.shape[0]
    tb = min(TOK_TILE, s)

    def body(ysb_ref, ysw_ref, x_ref, vec_ref, w_ref, mixedt_ref, attn_ref, x1_ref, h2_ref, h2t_ref):
        ysb, ysw = ysb_ref[...], ysw_ref[...]
        nsb_f = ysb * _rms_parts(ysb) * vec_ref[V_GN:V_GN + 1, :SB_W]
        nsw_f = ysw * _rms_parts(ysw) * vec_ref[V_GN:V_GN + 1, SB_W:]
        nsb, nsw = nsb_f.astype(BF16), nsw_f.astype(BF16)
        mixedt_ref[:SB_W, :] = nsb_f.T.astype(BF16)
        mixedt_ref[SB_W:, :] = nsw_f.T.astype(BF16)
        attn = _dot(nsb, w_ref[:SB_W, :]) + _dot(nsw, w_ref[SB_W:, :])
        attn_ref[...] = attn
        u1 = ALPHA * x_ref[...] + (1.0 + vec_ref[V_G_A:V_G_A + 1, :]) * attn
        xhat, _ = _layer_norm_stats(u1)
        x1 = xhat * vec_ref[V_LN1G:V_LN1G + 1, :] + vec_ref[V_LN1B:V_LN1B + 1, :]
        x1_ref[...] = x1
        h2 = x1 * (1.0 + vec_ref[V_SC_F:V_SC_F + 1, :]) + vec_ref[V_SH_F:V_SH_F + 1, :]
        h2_ref[...] = h2.astype(BF16)
        h2t_ref[...] = h2.T.astype(BF16)

    half = pl.BlockSpec((tb, SB_W), lambda i: (i, 0))
    full = pl.BlockSpec((tb, D), lambda i: (i, 0))
    full_t = pl.BlockSpec((D, tb), lambda i: (0, i))
    return pl.pallas_call(
        body, name=name, grid=(s // tb,),
        out_shape=[jax.ShapeDtypeStruct((D, s), BF16), jax.ShapeDtypeStruct((s, D), F32),
                   jax.ShapeDtypeStruct((s, D), F32), jax.ShapeDtypeStruct((s, D), BF16),
                   jax.ShapeDtypeStruct((D, s), BF16)],
        in_specs=[half, half, full, _resident((VEC_ROWS, D)), _resident((D, D))],
        out_specs=[full_t, full, full, full, full_t],
        compiler_params=_cparams())(y_sb, y_sw, x, vec, w_out)


def _ffn_forward(h2, w_gu, w_down, name):
    s = h2.shape[0]
    tb = min(FFN_TILE, s)

    def body(h_ref, wgu_ref, wd_ref, gu_ref, actt_ref, ffn_ref):
        gu = _dot(h_ref[...], wgu_ref[...])
        gu_ref[...] = gu.astype(BF16)
        gate, up = gu[:, :D_FF], gu[:, D_FF:]
        act = gate * (1.0 / (1.0 + jnp.exp(-gate))) * up
        actt_ref[...] = act.T.astype(BF16)
        ffn_ref[...] = _dot(act.astype(BF16), wd_ref[...])

    return pl.pallas_call(
        body, name=name, grid=(s // tb,),
        out_shape=[jax.ShapeDtypeStruct((s, 2 * D_FF), BF16), jax.ShapeDtypeStruct((D_FF, s), BF16),
                   jax.ShapeDtypeStruct((s, D), F32)],
        in_specs=[pl.BlockSpec((tb, D), lambda i: (i, 0)), _resident((D, 2 * D_FF)), _resident((D_FF, D))],
        out_specs=[pl.BlockSpec((tb, 2 * D_FF), lambda i: (i, 0)), pl.BlockSpec((D_FF, tb), lambda i: (0, i)),
                   pl.BlockSpec((tb, D), lambda i: (i, 0))],
        compiler_params=_cparams())(h2, w_gu, w_down)


def _layer_norm_bwd(dxhat, xhat, rstd):
    m1 = jnp.mean(dxhat, axis=1, keepdims=True)
    m2 = jnp.mean(dxhat * xhat, axis=1, keepdims=True)
    return rstd * (dxhat - m1 - xhat * m2)


def _colsum(a):
    return jnp.sum(a, axis=0, keepdims=True)


A_LN2G, A_LN2B, A_GF, A_SCF, A_SHF, A_LOSS = range(6)
B_LN1G, B_LN1B, B_GA, B_GN = range(4)
C_SCA, C_SHA = range(2)


def _ffn_backward(x1, ffn, target, gu, vec, w_gu, w_down, name):
    s = x1.shape[0]
    tb = min(FFN_BWD_TILE, s)

    def body(x1_ref, ffn_ref, t_ref, gu_ref, vec_ref, wgu_ref, wd_ref, dffn_ref, dgu_ref, dx1_ref, acc_ref):
        @pl.when(pl.program_id(0) == 0)
        def _():
            acc_ref[...] = jnp.zeros_like(acc_ref)

        x1v, ffn_v = x1_ref[...], ffn_ref[...]
        g_f = 1.0 + vec_ref[V_G_F:V_G_F + 1, :]
        u2 = ALPHA * x1v + g_f * ffn_v
        xhat, rstd = _layer_norm_stats(u2)
        ln_g = vec_ref[V_LN2G:V_LN2G + 1, :]
        err = xhat * ln_g + vec_ref[V_LN2B:V_LN2B + 1, :] - t_ref[...]
        dx2 = err * (1.0 / D)
        acc_ref[A_LOSS:A_LOSS + 1, :] += _colsum(err * err) * (0.5 / D)
        acc_ref[A_LN2G:A_LN2G + 1, :] += _colsum(dx2 * xhat)
        acc_ref[A_LN2B:A_LN2B + 1, :] += _colsum(dx2)
        du2 = _layer_norm_bwd(dx2 * ln_g, xhat, rstd)
        acc_ref[A_GF:A_GF + 1, :] += _colsum(du2 * ffn_v)
        dffn = (g_f * du2).astype(BF16)
        dffn_ref[...] = dffn
        dact = _dot_nt(dffn, wd_ref[...])
        gate, up = gu_ref[:, :D_FF].astype(F32), gu_ref[:, D_FF:].astype(F32)
        sg = 1.0 / (1.0 + jnp.exp(-gate))
        dgate = (dact * up * (sg * (1.0 + gate * (1.0 - sg)))).astype(BF16)
        dup = (dact * (gate * sg)).astype(BF16)
        dgu_ref[:, :D_FF] = dgate
        dgu_ref[:, D_FF:] = dup
        dh2 = _dot_nt(dgate, wgu_ref[:, :D_FF]) + _dot_nt(dup, wgu_ref[:, D_FF:])
        dx1_ref[...] = ALPHA * du2 + dh2 * (1.0 + vec_ref[V_SC_F:V_SC_F + 1, :])
        acc_ref[A_SCF:A_SCF + 1, :] += _colsum(dh2 * x1v)
        acc_ref[A_SHF:A_SHF + 1, :] += _colsum(dh2)

    full = pl.BlockSpec((tb, D), lambda i: (i, 0))
    wide = pl.BlockSpec((tb, 2 * D_FF), lambda i: (i, 0))
    return pl.pallas_call(
        body, name=name, grid=(s // tb,),
        out_shape=[jax.ShapeDtypeStruct((s, D), BF16), jax.ShapeDtypeStruct((s, 2 * D_FF), BF16),
                   jax.ShapeDtypeStruct((s, D), F32), jax.ShapeDtypeStruct((8, D), F32)],
        in_specs=[full, full, full, wide, _resident((VEC_ROWS, D)), _resident((D, 2 * D_FF)), _resident((D_FF, D))],
        out_specs=[full, wide, full, pl.BlockSpec((8, D), lambda i: (0, 0))],
        compiler_params=_cparams())(x1, ffn, target, gu, vec, w_gu, w_down)


def _attn_out_backward(dx1, x, attn, y_sb, y_sw, vec, w_out, name):
    s = x.shape[0]
    tb = min(TOK_TILE, s)

    def body(dx1_ref, x_ref, attn_ref, ysb_ref, ysw_ref, vec_ref, w_ref, du1_ref, dattn_ref, dy_ref, acc_ref):
        @pl.when(pl.program_id(0) == 0)
        def _():
            acc_ref[...] = jnp.zeros_like(acc_ref)

        attn = attn_ref[...]
        g_a = 1.0 + vec_ref[V_G_A:V_G_A + 1, :]
        xhat, rstd = _layer_norm_stats(ALPHA * x_ref[...] + g_a * attn)
        dx1v = dx1_ref[...]
        acc_ref[B_LN1G:B_LN1G + 1, :] += _colsum(dx1v * xhat)
        acc_ref[B_LN1B:B_LN1B + 1, :] += _colsum(dx1v)
        du1 = _layer_norm_bwd(dx1v * vec_ref[V_LN1G:V_LN1G + 1, :], xhat, rstd)
        du1_ref[...] = du1
        acc_ref[B_GA:B_GA + 1, :] += _colsum(du1 * attn)
        dattn = (g_a * du1).astype(BF16)
        dattn_ref[...] = dattn
        dmixed = _dot_nt(dattn, w_ref[...])
        for lo, y_ref in ((0, ysb_ref), (SB_W, ysw_ref)):
            y = y_ref[...]
            rr = _rms_parts(y)
            dn = dmixed[:, lo:lo + SB_W]
            acc_ref[B_GN:B_GN + 1, lo:lo + SB_W] += _colsum(dn * y * rr)
            dng = dn * vec_ref[V_GN:V_GN + 1, lo:lo + SB_W]
            dy_ref[:, lo:lo + SB_W] = rr * dng - y * (rr * rr * rr) * jnp.mean(dng * y, axis=1, keepdims=True)

    half = pl.BlockSpec((tb, SB_W), lambda i: (i, 0))
    full = pl.BlockSpec((tb, D), lambda i: (i, 0))
    return pl.pallas_call(
        body, name=name, grid=(s // tb,),
        out_shape=[jax.ShapeDtypeStruct((s, D), F32), jax.ShapeDtypeStruct((s, D), BF16),
                   jax.ShapeDtypeStruct((s, D), F32), jax.ShapeDtypeStruct((8, D), F32)],
        in_specs=[full, full, full, half, half, _resident((VEC_ROWS, D)), _resident((D, D))],
        out_specs=[full, full, full, pl.BlockSpec((8, D), lambda i: (0, 0))],
        compiler_params=_cparams())(dx1, x, attn, y_sb, y_sw, vec, w_out)


def _sb_backward(proj, sp_total, dy, slabs, name):
    s = proj.shape[0]
    tq, tk = min(SB_TQ, s), min(SB_TK, s)
    r = tq // tk
    nkb = SB_W // LANES
    nq = s // tq

    def body(q_ref, k_ref, v_ref, tot_ref, do_ref, slab_ref, dq_ref, dk_ref, dv_ref, got_ref,
             dq_acc, left_refs, gsum_refs, send_sems, recv_sems, local_sems):
        i = pl.program_id(1)
        step = pl.program_id(0) * nq + i
        scatter = _scatter_exchange(slab_ref, got_ref, send_sems, recv_sems, local_sems)

        @pl.when(step == 0)
        def _():
            scatter.start()

        @pl.when(i == 0)
        def _():
            dk_ref[...] = jnp.zeros_like(dk_ref)
            dv_ref[...] = jnp.zeros_like(dv_ref)

        lane = lax.broadcasted_iota(jnp.int32, (1, LANES), 1)
        first = lane < HEAD_DIM
        qp, dop, totp = q_ref[...], do_ref[...], tot_ref[...]
        zero = jnp.zeros((), BF16)
        qs = (jnp.where(first, qp, zero), jnp.where(first, zero, qp))
        dofs = (jnp.where(first, dop, 0.0), jnp.where(first, 0.0, dop))
        dobs = tuple(d.astype(BF16) for d in dofs)
        dots = tuple(d.T.astype(BF16) for d in dofs)
        qts = tuple(qh.astype(F32).T.astype(BF16) for qh in qs)
        later = _sum_matrix(tk, lambda row, col: row > col)
        earlier = _sum_matrix(tk, lambda row, col: row < col)
        dq_acc[...] = jnp.zeros_like(dq_acc)
        gsum_refs[...] = jnp.zeros_like(gsum_refs)
        swapped = pltpu.roll(totp, HEAD_DIM, 1)
        left_refs[0] = jnp.where(first, totp, swapped)
        left_refs[1] = jnp.where(first, swapped, totp)

        def blocks(js, masked):
            ks = [pl.multiple_of(j * tk, tk) for j in js]
            kjs = [k_ref[pl.ds(k0, tk), :] for k0 in ks]
            vjs = [v_ref[pl.ds(k0, tk), :] for k0 in ks]
            chains = [(hd, b) for b in range(len(js)) for hd in range(2)]
            zs = [_dot_nt(qs[hd], kjs[b]) for hd, b in chains]
            dws = [_dot_nt(dobs[hd], vjs[b]) for hd, b in chains]
            parts = [_softplus_parts(z) for z in zs]
            sps = [p[0] for p in parts]
            if masked:
                t_idx = i * tq + lax.broadcasted_iota(jnp.int32, (tq, tk), 0)
                befores = [j * tk + lax.broadcasted_iota(jnp.int32, (tq, tk), 1) < t_idx for j in js]
                spms = [jnp.where(befores[b], sp, 0.0) for (hd, b), sp in zip(chains, sps)]
            else:
                spms = sps
            cums = [_block_sums(spm, later) for spm in spms]
            lefts = [left_refs[0], left_refs[1]]
            ws = []
            for (hd, b), z, sp, (cum, sm) in zip(chains, zs, sps, cums):
                lefts[hd] = lefts[hd] - sm
                w = jnp.exp(z - sp - cum - _across(lefts[hd], tk))
                if masked:
                    w = jnp.where(befores[b], w, 0.0)
                ws.append(w)
            wbs = [w.astype(BF16) for w in ws]
            dvs = [_dot(dots[hd], wb) for (hd, b), wb in zip(chains, wbs)]
            gs = [dw * w for dw, w in zip(dws, ws)]
            gcums = [_block_sums(g, earlier) for g in gs]
            gsums = [gsum_refs[0], gsum_refs[1]]
            dzbs = []
            for (hd, b), z, (sp, e1), g, (gcum, gsm) in zip(chains, zs, parts, gs, gcums):
                inv = 1.0 / (1.0 + e1)
                sig = jnp.where(z >= 0.0, inv, e1 * inv)
                dz = g - sig * (g + _across(gsums[hd], tk) + gcum)
                if masked:
                    dz = jnp.where(befores[b], dz, 0.0)
                dzbs.append(dz.astype(BF16))
                gsums[hd] = gsums[hd] + gsm
            dqs = [_dot(dzb, kjs[b]) for (hd, b), dzb in zip(chains, dzbs)]
            dks = [_dot(qts[hd], dzb) for (hd, b), dzb in zip(chains, dzbs)]
            for b, j in enumerate(js):
                dv_ref[j] += dvs[2 * b] + dvs[2 * b + 1]
                dk_ref[j] += dks[2 * b] + dks[2 * b + 1]
            for hd in range(2):
                tot = dqs[hd]
                for b in range(1, len(js)):
                    tot = tot + dqs[2 * b + hd]
                dq_acc[hd] += tot
                left_refs[hd] = lefts[hd]
                gsum_refs[hd] = gsums[hd]

        below = i * r

        def sweep(n, carry):
            blocks([SB_UNROLL_BWD * n + u for u in range(SB_UNROLL_BWD)], False)
            return carry

        lax.fori_loop(0, below // SB_UNROLL_BWD, sweep, 0)
        for u in range(SB_UNROLL_BWD - 1, 0, -1):
            @pl.when(below % SB_UNROLL_BWD >= u)
            def _(u=u):
                blocks([below - u], False)
        blocks([below + d for d in range(r)], True)
        dq_ref[...] = jnp.where(first, dq_acc[0], dq_acc[1])

        @pl.when(step == nkb * nq - 1)
        def _():
            scatter.wait()

    shp = jax.ShapeDtypeStruct((s, SB_W), F32)
    qspec = pl.BlockSpec((tq, LANES), lambda p, i: (i, p))
    whole = pl.BlockSpec((None, s // tk, LANES, tk), lambda p, i: (p, 0, 0, 0))
    shp_t = jax.ShapeDtypeStruct((nkb, s // tk, LANES, tk), F32)
    hbm = pl.BlockSpec(memory_space=pl.ANY)
    return pl.pallas_call(
        body, name=name, grid=(nkb, nq),
        out_shape=[shp, shp_t, shp_t, jax.ShapeDtypeStruct(slabs.shape, slabs.dtype)],
        in_specs=[qspec,
                  pl.BlockSpec((s, LANES), lambda p, i: (0, nkb + p)),
                  pl.BlockSpec((s, LANES), lambda p, i: (0, 2 * nkb + p)),
                  qspec, qspec, hbm],
        out_specs=[qspec, whole, whole, hbm],
        scratch_shapes=[pltpu.VMEM((2, tq, LANES), F32), pltpu.VMEM((2, tq, LANES), F32), pltpu.VMEM((2, tq, LANES), F32)]
        + _exchange_sems(1),
        compiler_params=_cparams())(proj, proj, proj, sp_total, dy, slabs)


def _swa_backward(proj, y_sw, dy, sinks, name):
    s = proj.shape[0]
    nb = s // WINDOW
    qb, kb, vb = 3 * SB_W // SWA_QW, (3 * SB_W + SWA_QW) // LANES, (3 * SB_W + SWA_QW + SWA_KW) // LANES

    def body(q_ref, kp_ref, kc_ref, vp_ref, vc_ref, o_ref, do_ref, sink_ref, dq_ref, dk_ref, dv_ref, ds_ref):
        n = pl.program_id(0)

        @pl.when(n == 0)
        def _():
            dk_ref[...] = jnp.zeros_like(dk_ref)
            dv_ref[...] = jnp.zeros_like(dv_ref)
            ds_ref[...] = jnp.zeros_like(ds_ref)

        k = jnp.concatenate([kp_ref[...], kc_ref[...]], axis=0)
        v = jnp.concatenate([vp_ref[...], vc_ref[...]], axis=0)
        k_sw = pltpu.roll(k.astype(F32), HEAD_DIM, 1).astype(BF16)
        v_sw = pltpu.roll(v.astype(F32), HEAD_DIM, 1).astype(BF16)
        lane = lax.broadcasted_iota(jnp.int32, (1, LANES), 1)
        halves = [lane < HEAD_DIM, lane >= HEAD_DIM]
        valid, distf = _swa_masks(n)
        heads = range(2 * 4)
        cols = [slice((h // 2) * LANES, (h // 2 + 1) * LANES) for h in heads]
        qms = [jnp.where(halves[h % 2], q_ref[:, cols[h]], jnp.zeros((), BF16)) for h in heads]
        dos = [jnp.where(halves[h % 2], do_ref[:, cols[h]], 0.0) for h in heads]
        dobs = [d.astype(BF16) for d in dos]
        native = [h // 4 == h % 2 for h in heads]
        kus = [k if native[h] else k_sw for h in heads]
        vus = [v if native[h] else v_sw for h in heads]
        scores = [_dot_nt(qms[h], kus[h]) for h in heads]
        dps = [_dot_nt(dobs[h], vus[h]) for h in heads]
        deltas = [jnp.sum(dos[h] * o_ref[:, cols[h]], axis=1, keepdims=True) for h in heads]
        probs = [_swa_probs(scores[h], valid, distf, h, sink_ref[h]) for h in heads]
        pbs = [probs[h][0].astype(BF16) for h in heads]
        dscs = [(probs[h][0] * (dps[h] - deltas[h])).astype(BF16) for h in heads]
        dqs = [_dot(dscs[h], kus[h]) for h in heads]
        dks = [_dot_tn(dscs[h], qms[h]) for h in heads]
        dvs = [_dot_tn(pbs[h], dobs[h]) for h in heads]
        for h in heads:
            ds_ref[h:h + 1, :] += jnp.zeros((1, LANES), F32) - jnp.sum(probs[h][1] * deltas[h])
        for pair in range(4):
            dq_ref[:, cols[2 * pair]] = jnp.where(halves[0], dqs[2 * pair], dqs[2 * pair + 1])

        def gathered(parts):
            nat = sum(parts[h] for h in heads if native[h])
            rot = sum(parts[h] for h in heads if not native[h])
            return nat + pltpu.roll(rot, HEAD_DIM, 1)

        dk, dv = gathered(dks), gathered(dvs)
        prev = pl.multiple_of(jnp.maximum(n - 1, 0) * WINDOW, WINDOW)
        cur = pl.multiple_of(n * WINDOW, WINDOW)
        dk_ref[pl.ds(prev, WINDOW), :] += dk[:WINDOW]
        dv_ref[pl.ds(prev, WINDOW), :] += dv[:WINDOW]
        dk_ref[pl.ds(cur, WINDOW), :] += dk[WINDOW:]
        dv_ref[pl.ds(cur, WINDOW), :] += dv[WINDOW:]

    prev_blk = lambda n: jnp.maximum(n - 1, 0)
    wide = pl.BlockSpec((WINDOW, SWA_QW), lambda n: (n, 0))
    whole = pl.BlockSpec((s, LANES), lambda n: (0, 0))
    return pl.pallas_call(
        body, name=name, grid=(nb,),
        out_shape=[jax.ShapeDtypeStruct((s, SWA_QW), F32), jax.ShapeDtypeStruct((s, LANES), F32),
                   jax.ShapeDtypeStruct((s, LANES), F32), jax.ShapeDtypeStruct((8, LANES), F32)],
        in_specs=[pl.BlockSpec((WINDOW, SWA_QW), lambda n: (n, qb)),
                  pl.BlockSpec((WINDOW, LANES), lambda n: (prev_blk(n), kb)),
                  pl.BlockSpec((WINDOW, LANES), lambda n: (n, kb)),
                  pl.BlockSpec((WINDOW, LANES), lambda n: (prev_blk(n), vb)),
                  pl.BlockSpec((WINDOW, LANES), lambda n: (n, vb)),
                  wide,
                  pl.BlockSpec((WINDOW, SWA_QW), lambda n: (n, 1)),
                  pl.BlockSpec(memory_space=pltpu.SMEM)],
        out_specs=[wide, whole, whole, pl.BlockSpec((8, LANES), lambda n: (0, 0))],
        compiler_params=_cparams())(proj, proj, proj, proj, proj, y_sw, dy, sinks)


def _in_proj_backward(dq_sb, dkt_sb, dvt_sb, dq_sw, dk_sw, dv_sw, du1, x, vec, w_in, name):
    s = x.shape[0]
    tb = min(TOK_TILE, s)
    n_pairs, _, _, tk = dkt_sb.shape

    def body(dqsb_ref, dktsb_ref, dvtsb_ref, dqsw_ref, dksw_ref, dvsw_ref, du1_ref, x_ref, vec_ref, w_ref,
             dproj_ref, gx_ref, acc_ref, bacc_ref):
        @pl.when(pl.program_id(0) == 0)
        def _():
            acc_ref[...] = jnp.zeros_like(acc_ref)
            bacc_ref[...] = jnp.zeros_like(bacc_ref)

        pieces = ((0, dqsb_ref, QK_SCALE), (3 * SB_W, dqsw_ref, QK_SCALE), (3 * SB_W + SWA_QW, dksw_ref, 1.0),
                  (3 * SB_W + SWA_QW + SWA_KW, dvsw_ref, 1.0))
        for lo, ref, scale in pieces:
            width = ref.shape[1]
            piece = ref[...] * scale
            bacc_ref[0:1, lo:lo + width] += _colsum(piece)
            dproj_ref[:, lo:lo + width] = piece.astype(BF16)
        for base, ref in ((SB_W, dktsb_ref), (2 * SB_W, dvtsb_ref)):
            for p in range(n_pairs):
                lo = base + p * LANES
                for jj in range(tb // tk):
                    piece = ref[p, jj].T
                    bacc_ref[0:1, lo:lo + LANES] += _colsum(piece)
                    dproj_ref[jj * tk:(jj + 1) * tk, lo:lo + LANES] = piece.astype(BF16)
        dh = _dot_nt(dproj_ref[...], w_ref[...])
        xv = x_ref[...]
        gx_ref[...] = ALPHA * du1_ref[...] + dh * (1.0 + vec_ref[V_SC_A:V_SC_A + 1, :])
        acc_ref[C_SCA:C_SCA + 1, :] += _colsum(dh * xv)
        acc_ref[C_SHA:C_SHA + 1, :] += _colsum(dh)

    half = pl.BlockSpec((tb, SB_W), lambda i: (i, 0))
    narrow = pl.BlockSpec((tb, LANES), lambda i: (i, 0))
    full = pl.BlockSpec((tb, D), lambda i: (i, 0))
    blocks_t = pl.BlockSpec((n_pairs, tb // tk, LANES, tk), lambda i: (0, i, 0, 0))
    return pl.pallas_call(
        body, name=name, grid=(s // tb,),
        out_shape=[jax.ShapeDtypeStruct((s, D_IN), BF16), jax.ShapeDtypeStruct((s, D), F32),
                   jax.ShapeDtypeStruct((8, D), F32), jax.ShapeDtypeStruct((8, D_IN), F32)],
        in_specs=[half, blocks_t, blocks_t, half, narrow, narrow, full, full, _resident((VEC_ROWS, D)),
                  _resident((D, D_IN))],
        out_specs=[pl.BlockSpec((tb, D_IN), lambda i: (i, 0)), full, pl.BlockSpec((8, D), lambda i: (0, 0)),
                   pl.BlockSpec((8, D_IN), lambda i: (0, 0))],
        compiler_params=_cparams())(dq_sb, dkt_sb, dvt_sb, dq_sw, dk_sw, dv_sw, du1, x, vec, w_in)


def _weight_grad(at, b, name, col_shards=1):
    m, s = at.shape
    n = b.shape[1]
    if col_shards > 1:
        tn = n // col_shards
        out_shape = jax.ShapeDtypeStruct((col_shards, m, tn), F32)
        out_spec = pl.BlockSpec((None, m, tn), lambda j, k: (j, 0, 0))
    else:
        tn = 512 if n % 512 == 0 else n
        out_shape = jax.ShapeDtypeStruct((m, n), F32)
        out_spec = pl.BlockSpec((m, tn), lambda j, k: (0, j))
    ts = min(512, s)

    def body(at_ref, b_ref, o_ref):
        @pl.when(pl.program_id(1) == 0)
        def _():
            o_ref[...] = jnp.zeros_like(o_ref)

        o_ref[...] += _dot(at_ref[...], b_ref[...])

    return pl.pallas_call(
        body, name=name, grid=(n // tn, s // ts),
        out_shape=out_shape,
        in_specs=[pl.BlockSpec((m, ts), lambda j, k: (0, k)), pl.BlockSpec((ts, tn), lambda j, k: (k, j))],
        out_specs=out_spec,
        compiler_params=_cparams())(at, b)


def _pad_rows(v, rows):
    return jnp.concatenate([v, jnp.zeros((rows - v.shape[0], v.shape[1]), v.dtype)], axis=0)


def _col_shards(w, n_shards):
    r, n = w.shape
    return w.reshape(r, n_shards, n // n_shards).transpose(1, 0, 2)


def kernel(x, c, w_ada, b_ada, w_in, b_in, sinks, gn_sb, gn_swa, w_out, ln1_g, ln1_b, w_gu, w_down, ln2_g, ln2_b, loss_target, m_w_ada, m_b_ada, m_w_in, m_b_in, m_sinks, m_gn_sb, m_gn_swa, m_w_out, m_ln1_g, m_ln1_b, m_w_gu, m_w_down, m_ln2_g, m_ln2_b, v_w_ada, v_b_ada, v_w_in, v_b_in, v_sinks, v_gn_sb, v_gn_swa, v_w_out, v_ln1_g, v_ln1_b, v_w_gu, v_w_down, v_ln2_g, v_ln2_b):
    ix, iy, ic = lax.axis_index("x"), lax.axis_index("y"), lax.axis_index("c")
    chip = 2 * ix + iy
    dev = 4 * ix + 2 * iy + ic
    xs, target = x[0], loss_target[0]
    s = xs.shape[0]

    c_all = _allgather8(_pad_rows(c, 8), "gather_c")[::8]
    n_ada = w_ada.shape[2]
    b_ada_shard = lax.dynamic_slice_in_dim(b_ada, chip * n_ada, n_ada, axis=1)
    mod_cols, silu_c = _mod_shard(c_all, w_ada[0], b_ada_shard, "mod_shard")
    mod_all = _allgather8(mod_cols, "gather_mod").reshape(N_DEV, 8, n_ada)
    mod_mine = lax.dynamic_index_in_dim(mod_all, dev, axis=1, keepdims=False)
    mod = mod_mine.reshape(N_CHIPS, 2, n_ada)[:, 0].reshape(6, D)
    vec = jnp.concatenate([mod, ln1_g, ln1_b, ln2_g, ln2_b, jnp.concatenate([gn_sb, gn_swa], axis=1),
                           jnp.zeros((VEC_ROWS - 11, D), F32)], axis=0)

    (g_in,) = _chip_allgather([w_in[0].astype(BF16)], "gather_w_in")
    w_in_b = g_in.transpose(1, 0, 2).reshape(D, D_IN)

    h_t, proj = _in_proj(xs, vec, w_in_b, b_in, "in_proj")
    y_sb, sp_total, g_out, g_gu, g_down = _sb_forward(
        proj, [w_out[0].astype(BF16), w_gu[0].astype(BF16), w_down[0].astype(BF16)], "sb_forward")
    w_gu_b = g_gu.transpose(1, 0, 2).reshape(D, 2 * D_FF)
    w_out_b = g_out.reshape(D, D)
    w_down_b = g_down.reshape(D_FF, D)
    sink_vec = sinks[0]
    y_sw = _swa_forward(proj, sink_vec, "swa_forward")
    mixed_t, attn, x1, h2_b, h2_t = _post_attention(y_sb, y_sw, xs, vec, w_out_b, "post_attention")
    gu, act_t, ffn = _ffn_forward(h2_b, w_gu_b, w_down_b, "ffn_forward")

    def chip_sums(per_shard, tag):
        flat = jnp.concatenate(per_shard, axis=1)
        rows = flat.shape[1] // (2 * LANES)
        halves = flat.reshape(4, 2, rows, LANES)
        keep = lax.dynamic_index_in_dim(halves, ic, axis=1, keepdims=False)
        give = lax.dynamic_index_in_dim(halves, 1 - ic, axis=1, keepdims=False)
        got = _sibling_send(give, "grad_halves_swap_" + tag)
        both = _add2(keep.reshape(4 * rows, LANES), got.reshape(4 * rows, LANES), "grad_chip_sum_" + tag)
        return both.reshape(4, rows, LANES)

    dffn_b, dgu_b, dx1, acc_f = _ffn_backward(x1, ffn, target, gu, vec, w_gu_b, w_down_b, "ffn_backward")
    dw_gu = _weight_grad(h2_t, dgu_b, "grad_w_gu", col_shards=4)
    dw_down = _weight_grad(act_t, dffn_b, "grad_w_down")
    du1, dattn_b, dy, acc_a = _attn_out_backward(dx1, xs, attn, y_sb, y_sw, vec, w_out_b, "attn_out_backward")
    dw_out = _weight_grad(mixed_t, dattn_b, "grad_w_out")
    sums_f = chip_sums([dw_gu.reshape(4, -1), dw_down.reshape(4, -1), dw_out.reshape(4, -1)], "ffn")
    dq_sb, dk_sb, dv_sb, parts_f = _sb_backward(proj, sp_total, dy, sums_f, "sb_backward")
    dq_sw, dk_sw, dv_sw, dsink = _swa_backward(proj, y_sw, dy, sink_vec, "swa_backward")
    dproj_b, grad_x, acc_i, acc_b = _in_proj_backward(dq_sb, dk_sb, dv_sb, dq_sw, dk_sw, dv_sw, du1, xs, vec, w_in_b,
                                                      "in_proj_backward")
    dw_in = _weight_grad(h_t, dproj_b, "grad_w_in")
    sums_a = chip_sums([_col_shards(dw_in, 4).reshape(4, -1)], "attn")
    parts_a = _chip_scatter(sums_a, "grad_chip_scatter_attn")
    my_half = jnp.concatenate([_sum4(parts_f, "grad_reduce_ffn"), _sum4(parts_a, "grad_reduce_attn")], axis=0)
    other_half = _sibling_send(my_half, "grad_half_return")
    rows_f = parts_f.shape[1]

    def whole_shard(lo, hi):
        mine, other = my_half[lo:hi], other_half[lo:hi]
        return jnp.concatenate([jnp.where(ic == 0, mine, other), jnp.where(ic == 0, other, mine)], axis=0).reshape(-1)

    flat_f, flat_a = whole_shard(0, rows_f), whole_shard(rows_f, my_half.shape[0])
    n_gu, n_down = D * (2 * D_FF // 4), (D_FF // 4) * D
    gw_gu = flat_f[:n_gu].reshape(D, 2 * D_FF // 4)
    gw_down = flat_f[n_gu:n_gu + n_down].reshape(D_FF // 4, D)
    gw_out = flat_f[n_gu + n_down:].reshape(D // 4, D)
    gw_in = flat_a.reshape(D, D_IN // 4)

    dmod = jnp.concatenate([acc_i[C_SHA:C_SHA + 1], acc_i[C_SCA:C_SCA + 1], acc_a[B_GA:B_GA + 1],
                            acc_f[A_SHF:A_SHF + 1], acc_f[A_SCF:A_SCF + 1], acc_f[A_GF:A_GF + 1]], axis=1)
    dsink_row = jnp.concatenate([dsink[:, 0].reshape(1, 8), jnp.zeros((1, LANES - 8), F32)], axis=1)
    loss_row = jnp.concatenate([jnp.sum(acc_f[A_LOSS:A_LOSS + 1], axis=1, keepdims=True),
                                jnp.zeros((1, LANES - 1), F32)], axis=1)
    small = jnp.concatenate([dmod, acc_b[0:1], acc_a[B_LN1G:B_LN1G + 1], acc_a[B_LN1B:B_LN1B + 1],
                             acc_f[A_LN2G:A_LN2G + 1], acc_f[A_LN2B:A_LN2B + 1], acc_a[B_GN:B_GN + 1],
                             dsink_row, loss_row], axis=1)
    small_all = _allgather8(_pad_rows(small, 8), "gather_small")[::8]

    def pack_small(b_ada_, b_in_, ln1g_, ln1b_, ln2g_, ln2b_, gsb_, gsw_, sinks_):
        return jnp.concatenate([b_ada_, b_in_, ln1g_, ln1b_, ln2g_, ln2b_, gsb_, gsw_, sinks_,
                                jnp.ones((1, 2 * LANES - 8), F32)], axis=1)

    w_small = pack_small(b_ada, b_in, ln1_g, ln1_b, ln2_g, ln2_b, gn_sb, gn_swa, sinks)
    m_small = pack_small(m_b_ada, m_b_in, m_ln1_g, m_ln1_b, m_ln2_g, m_ln2_b, m_gn_sb, m_gn_swa, m_sinks)
    v_small = pack_small(v_b_ada, v_b_in, v_ln1_g, v_ln1_b, v_ln2_g, v_ln2_b, v_gn_sb, v_gn_swa, v_sinks)
    small_out = _small_update(small_all, w_small, m_small, v_small, "small_update")

    def unpack_small(row):
        return {"b_ada": row[:, SM_MOD:SM_BIN], "b_in": row[:, SM_BIN:SM_LN1G], "ln1_g": row[:, SM_LN1G:SM_LN1B],
                "ln1_b": row[:, SM_LN1B:SM_LN2G], "ln2_g": row[:, SM_LN2G:SM_LN2B], "ln2_b": row[:, SM_LN2B:SM_GN],
                "gn_sb": row[:, SM_GN:SM_GN + SB_W], "gn_swa": row[:, SM_GN + SB_W:SM_SINK],
                "sinks": row[:, SM_SINK:SM_SINK + 8]}

    g_small, d_small, m2_small, v2_small = [unpack_small(r) for r in small_out]
    loss = small_out[0][0, SM_LOSS]

    dmod_cols = lax.dynamic_slice_in_dim(small_all[:, SM_MOD:SM_BIN], chip * n_ada, n_ada, axis=1)
    gw_ada = _weight_grad(_pad_rows(silu_c, LANES).astype(BF16).T, _pad_rows(dmod_cols, LANES).astype(BF16), "grad_w_ada")

    big = {}
    for nm, w, g, m, v in (("w_ada", w_ada, gw_ada, m_w_ada, v_w_ada), ("w_in", w_in, gw_in, m_w_in, v_w_in),
                           ("w_out", w_out, gw_out, m_w_out, v_w_out), ("w_gu", w_gu, gw_gu, m_w_gu, v_w_gu),
                           ("w_down", w_down, gw_down, m_w_down, v_w_down)):
        d_, m2_, v2_ = _adamw(w[0], g, m[0], v[0], "adamw_" + nm)
        big[nm] = (g[None], d_[None], m2_[None], v2_[None])

    order = ["w_ada", "b_ada", "w_in", "b_in", "sinks", "gn_sb", "gn_swa", "w_out", "ln1_g", "ln1_b", "w_gu", "w_down",
             "ln2_g", "ln2_b"]

    def leaf(nm, which):
        if nm in big:
            return big[nm][which]
        return (g_small, d_small, m2_small, v2_small)[which][nm]

    outs = [loss, grad_x[None]]
    for which in range(4):
        outs += [leaf(nm, which) for nm in order]
    return tuple(outs)
```

```python
import functools
import math

import jax
import jax.numpy as jnp
from jax import lax
from jax.experimental import pallas as pl
from jax.experimental.pallas import tpu as pltpu

F32 = jnp.float32
BF16 = jnp.bfloat16

D = 1024
HEAD_DIM = 64
SB_W = 512
SWA_QW = 512
SWA_KW = 128
D_IN = 2304
D_FF = 2816
WINDOW = 128
ALPHA = 2.0 ** 0.25
LN_EPS = 1e-5
RMS_EPS = 1e-6
MASK_VALUE = -1e30
QK_SCALE = 1.0 / math.sqrt(HEAD_DIM)

ADAM_LR = 0.001
ADAM_B1 = 0.9
ADAM_B2 = 0.999
ADAM_EPS = 1e-08
ADAM_WD = 0.01
ADAM_STEP = 10

N_CHIPS = 4
N_DEV = 8
LANES = 128

SB_TQ = 512
SB_TK = 256
SB_UNROLL = 2
SB_DEAD_MASS = 110.0
TOK_TILE = 512
FFN_TILE = 256
FFN_BWD_TILE = 256
VMEM_LIMIT = 56 * 1024 * 1024

V_SH_A, V_SC_A, V_G_A, V_SH_F, V_SC_F, V_G_F, V_LN1G, V_LN1B, V_LN2G, V_LN2B, V_GN = range(11)
VEC_ROWS = 16

SM_MOD = 0
SM_BIN = 6 * D
SM_LN1G = SM_BIN + D_IN
SM_LN1B = SM_LN1G + D
SM_LN2G = SM_LN1B + D
SM_LN2B = SM_LN2G + D
SM_GN = SM_LN2B + D
SM_SINK = SM_GN + D
SM_LOSS = SM_SINK + LANES
SM_LEN = SM_LOSS + LANES

MESH = pl.DeviceIdType.MESH


def _cparams(**kw):
    return pltpu.CompilerParams(vmem_limit_bytes=VMEM_LIMIT, **kw)


def _resident(shape):
    nd = len(shape)
    return pl.BlockSpec(shape, lambda *_: (0,) * nd, pipeline_mode=pl.Buffered(1))


def _dot(a, b):
    return jnp.dot(a, b, preferred_element_type=F32)


def _dot_nt(a, b):
    return lax.dot_general(a, b, (((1,), (1,)), ((), ())), preferred_element_type=F32)


def _dot_tn(a, b):
    return lax.dot_general(a, b, (((0,), (0,)), ((), ())), preferred_element_type=F32)


def _sum_matrix(tk, keep):
    row = lax.broadcasted_iota(jnp.int32, (tk, tk + LANES), 0)
    col = lax.broadcasted_iota(jnp.int32, (tk, tk + LANES), 1)
    return (keep(row, col) | (col >= tk)).astype(BF16)


def _block_sums(x, m):
    tk = x.shape[1]
    res = _dot(x.astype(BF16), m)
    return res[:, :tk], res[:, tk:]


def _across(v, tk):
    return jnp.concatenate([v] * (tk // LANES), axis=1)


def _allgather8(v, name):
    m_per, n = v.shape

    def body(x_ref, out_ref, send_sems, recv_sems, local_sem):
        x, y, c = lax.axis_index("x"), lax.axis_index("y"), lax.axis_index("c")
        me, sibling = (x, y, c), (x, y, 1 - c)
        chips = [(1 - x, y), (x, 1 - y), (1 - x, 1 - y)]

        def rows(px, py, pc):
            return out_ref.at[pl.ds((4 * px + 2 * py + pc) * m_per, m_per), :]

        def copy(k, block, to, src=None):
            return pltpu.make_async_remote_copy(
                src_ref=rows(*block) if src is None else src, dst_ref=rows(*block),
                send_sem=send_sems.at[k], recv_sem=recv_sems.at[k], device_id=to, device_id_type=MESH)

        mine = pltpu.make_async_copy(x_ref, rows(*me), local_sem)
        mine.start()
        first = [copy(0, me, sibling, src=x_ref)]
        first += [copy(1 + j, me, (*chip, c), src=x_ref) for j, chip in enumerate(chips)]
        for cp in first:
            cp.start()
        passed = [copy(4 + j, (*chip, c), sibling) for j, chip in enumerate(chips)]
        for j, chip in enumerate(chips):
            copy(1 + j, (*chip, c), me).wait_recv()
            passed[j].start()
        copy(0, sibling, me).wait_recv()
        for j, chip in enumerate(chips):
            copy(4 + j, (*chip, 1 - c), me).wait_recv()
        for cp in first + passed:
            cp.wait_send()
        mine.wait()

    return pl.pallas_call(
        body, name=name,
        out_shape=jax.ShapeDtypeStruct((N_DEV * m_per, n), v.dtype),
        in_specs=[pl.BlockSpec(memory_space=pltpu.VMEM)],
        out_specs=pl.BlockSpec(memory_space=pltpu.VMEM),
        scratch_shapes=[pltpu.SemaphoreType.DMA((7,)), pltpu.SemaphoreType.DMA((7,)), pltpu.SemaphoreType.DMA],
        compiler_params=_cparams(),
    )(v)


class _Exchange:
    def __init__(self, local, sends, arrivals):
        self.local, self.sends, self.arrivals = local, sends, arrivals

    def start(self):
        for cp in self.local + self.sends:
            cp.start()

    def wait(self):
        for cp in self.arrivals:
            cp.wait_recv()
        for cp in self.sends:
            cp.wait_send()
        for cp in self.local:
            cp.wait()


def _exchange_sems(n):
    return [pltpu.SemaphoreType.DMA((3 * n,)), pltpu.SemaphoreType.DMA((3 * n,)), pltpu.SemaphoreType.DMA((n,))]


def _gather_exchange(ins, outs, send_sems, recv_sems, local_sems):
    x, y, c = lax.axis_index("x"), lax.axis_index("y"), lax.axis_index("c")
    slot = 2 * x + y
    chips = [(1 - x, y), (x, 1 - y), (1 - x, 1 - y)]
    local, sends, arrivals = [], [], []
    for a in range(len(ins)):
        local.append(pltpu.make_async_copy(ins[a], outs[a].at[slot], local_sems.at[a]))
        for j, (px, py) in enumerate(chips):
            sems = dict(send_sem=send_sems.at[3 * a + j], recv_sem=recv_sems.at[3 * a + j],
                        device_id=(px, py, c), device_id_type=MESH)
            sends.append(pltpu.make_async_remote_copy(src_ref=ins[a], dst_ref=outs[a].at[slot], **sems))
            arrivals.append(pltpu.make_async_remote_copy(src_ref=ins[a], dst_ref=outs[a].at[2 * px + py], **sems))
    return _Exchange(local, sends, arrivals)


def _scatter_exchange(p_ref, out_ref, send_sems, recv_sems, local_sems):
    x, y, c = lax.axis_index("x"), lax.axis_index("y"), lax.axis_index("c")
    slot = 2 * x + y
    chips = [(1 - x, y), (x, 1 - y), (1 - x, 1 - y)]
    local = [pltpu.make_async_copy(p_ref.at[slot], out_ref.at[slot], local_sems.at[0])]
    sends, arrivals = [], []
    for j, (px, py) in enumerate(chips):
        sems = dict(send_sem=send_sems.at[j], recv_sem=recv_sems.at[j], device_id=(px, py, c), device_id_type=MESH)
        sends.append(pltpu.make_async_remote_copy(src_ref=p_ref.at[2 * px + py], dst_ref=out_ref.at[slot], **sems))
        arrivals.append(pltpu.make_async_remote_copy(src_ref=p_ref.at[slot], dst_ref=out_ref.at[2 * px + py], **sems))
    return _Exchange(local, sends, arrivals)


def _chip_allgather(arrs, name):
    n = len(arrs)

    def body(*refs):
        ex = _gather_exchange(refs[:n], refs[n:2 * n], *refs[2 * n:])
        ex.start()
        ex.wait()

    hbm = pl.BlockSpec(memory_space=pl.ANY)
    return pl.pallas_call(
        body, name=name,
        out_shape=[jax.ShapeDtypeStruct((N_CHIPS,) + a.shape, a.dtype) for a in arrs],
        in_specs=[hbm] * n, out_specs=[hbm] * n,
        scratch_shapes=_exchange_sems(n),
        compiler_params=_cparams(),
    )(*arrs)


def _sibling_send(v, name):
    def body(v_ref, out_ref, send_sem, recv_sem):
        x, y, c = lax.axis_index("x"), lax.axis_index("y"), lax.axis_index("c")
        cp = pltpu.make_async_remote_copy(src_ref=v_ref, dst_ref=out_ref, send_sem=send_sem, recv_sem=recv_sem,
                                          device_id=(x, y, 1 - c), device_id_type=MESH)
        cp.start()
        cp.wait()

    hbm = pl.BlockSpec(memory_space=pl.ANY)
    return pl.pallas_call(
        body, name=name, out_shape=jax.ShapeDtypeStruct(v.shape, v.dtype),
        in_specs=[hbm], out_specs=hbm,
        scratch_shapes=[pltpu.SemaphoreType.DMA, pltpu.SemaphoreType.DMA],
        compiler_params=_cparams(),
    )(v)


def _chip_scatter(p, name):
    def body(p_ref, out_ref, send_sems, recv_sems, local_sems):
        ex = _scatter_exchange(p_ref, out_ref, send_sems, recv_sems, local_sems)
        ex.start()
        ex.wait()

    hbm = pl.BlockSpec(memory_space=pl.ANY)
    return pl.pallas_call(
        body, name=name, out_shape=jax.ShapeDtypeStruct(p.shape, p.dtype),
        in_specs=[hbm], out_specs=hbm,
        scratch_shapes=_exchange_sems(1),
        compiler_params=_cparams(),
    )(p)


def _add2(a, b, name):
    rows = a.shape[0]
    tr = rows // 8

    def body(a_ref, b_ref, o_ref):
        o_ref[...] = a_ref[...] + b_ref[...]

    spec = pl.BlockSpec((tr, LANES), lambda i: (i, 0))
    return pl.pallas_call(body, name=name, grid=(rows // tr,), out_shape=jax.ShapeDtypeStruct(a.shape, a.dtype),
                          in_specs=[spec, spec], out_specs=spec, compiler_params=_cparams())(a, b)


def _sum4(p, name):
    rows = p.shape[1]
    tr = rows // 8

    def body(p_ref, o_ref):
        o_ref[...] = ((p_ref[0] + p_ref[1]) + p_ref[2]) + p_ref[3]

    return pl.pallas_call(
        body, name=name, grid=(rows // tr,), out_shape=jax.ShapeDtypeStruct((rows, LANES), p.dtype),
        in_specs=[pl.BlockSpec((4, tr, LANES), lambda i: (0, i, 0))],
        out_specs=pl.BlockSpec((tr, LANES), lambda i: (i, 0)), compiler_params=_cparams())(p)


def _adam_math(w, g, m, v):
    m2 = ADAM_B1 * m + (1.0 - ADAM_B1) * g
    v2 = ADAM_B2 * v + (1.0 - ADAM_B2) * (g * g)
    m_hat = m2 / (1.0 - ADAM_B1 ** ADAM_STEP)
    v_hat = v2 / (1.0 - ADAM_B2 ** ADAM_STEP)
    delta = -ADAM_LR * (m_hat / (jnp.sqrt(v_hat) + ADAM_EPS) + ADAM_WD * w)
    return delta, m2, v2


def _adamw(w, g, m, v, name):
    rows, cols = w.shape
    tr = rows // 4 if rows % 32 == 0 else rows

    def body(w_ref, g_ref, m_ref, v_ref, d_ref, m2_ref, v2_ref):
        delta, m2, v2 = _adam_math(w_ref[...], g_ref[...], m_ref[...], v_ref[...])
        d_ref[...] = delta
        m2_ref[...] = m2
        v2_ref[...] = v2

    spec = pl.BlockSpec((tr, cols), lambda i: (i, 0))
    shp = jax.ShapeDtypeStruct(w.shape, F32)
    return pl.pallas_call(body, name=name, grid=(rows // tr,), out_shape=[shp, shp, shp],
                          in_specs=[spec] * 4, out_specs=[spec] * 3, compiler_params=_cparams())(w, g, m, v)


def _small_update(g8, w, m, v, name):
    n = w.shape[1]

    def body(g8_ref, w_ref, m_ref, v_ref, g_ref, d_ref, m2_ref, v2_ref):
        g = g8_ref[0:1, :]
        for r in range(1, N_DEV):
            g = g + g8_ref[r:r + 1, :]
        delta, m2, v2 = _adam_math(w_ref[...], g, m_ref[...], v_ref[...])
        g_ref[...] = g
        d_ref[...] = delta
        m2_ref[...] = m2
        v2_ref[...] = v2

    shp = jax.ShapeDtypeStruct((1, n), F32)
    vm = pl.BlockSpec(memory_space=pltpu.VMEM)
    return pl.pallas_call(body, name=name, out_shape=[shp] * 4, in_specs=[vm] * 4, out_specs=[vm] * 4,
                          compiler_params=_cparams())(g8, w, m, v)


def _mod_shard(c8, w_ada, b_ada_shard, name):
    n = w_ada.shape[1]
    tn = 512

    def body(c_ref, w_ref, b_ref, o_ref, s_ref):
        cv = c_ref[...]
        sc = cv * (1.0 / (1.0 + jnp.exp(-cv)))
        s_ref[...] = sc
        o_ref[...] = _dot(sc.astype(BF16), w_ref[...].astype(BF16)) + b_ref[...]

    return pl.pallas_call(
        body, name=name, grid=(n // tn,),
        out_shape=[jax.ShapeDtypeStruct((8, n), F32), jax.ShapeDtypeStruct((8, D), F32)],
        in_specs=[pl.BlockSpec((8, D), lambda j: (0, 0)), pl.BlockSpec((D, tn), lambda j: (0, j)),
                  pl.BlockSpec((1, tn), lambda j: (0, j))],
        out_specs=[pl.BlockSpec((8, tn), lambda j: (0, j)), pl.BlockSpec((8, D), lambda j: (0, 0))],
        compiler_params=_cparams())(c8, w_ada, b_ada_shard)


def _layer_norm_stats(u):
    mu = jnp.mean(u, axis=1, keepdims=True)
    d = u - mu
    var = jnp.mean(d * d, axis=1, keepdims=True)
    rstd = lax.rsqrt(var + LN_EPS)
    return d * rstd, rstd


def _in_proj(x, vec, w_in, b_in, name):
    s = x.shape[0]
    tb = min(TOK_TILE, s)

    def body(x_ref, vec_ref, w_ref, b_ref, ht_ref, p_ref):
        h = x_ref[...] * (1.0 + vec_ref[V_SC_A:V_SC_A + 1, :]) + vec_ref[V_SH_A:V_SH_A + 1, :]
        hb = h.astype(BF16)
        ht_ref[...] = h.T.astype(BF16)
        proj = _dot(hb, w_ref[...]) + b_ref[...]
        col = lax.broadcasted_iota(jnp.int32, (1, D_IN), 1)
        is_q = (col < SB_W) | ((col >= 3 * SB_W) & (col < 3 * SB_W + SWA_QW))
        p_ref[...] = (proj * jnp.where(is_q, QK_SCALE, 1.0)).astype(BF16)

    return pl.pallas_call(
        body, name=name, grid=(s // tb,),
        out_shape=[jax.ShapeDtypeStruct((D, s), BF16), jax.ShapeDtypeStruct((s, D_IN), BF16)],
        in_specs=[pl.BlockSpec((tb, D), lambda i: (i, 0)), _resident((VEC_ROWS, D)), _resident((D, D_IN)),
                  _resident((1, D_IN))],
        out_specs=[pl.BlockSpec((D, tb), lambda i: (0, i)), pl.BlockSpec((tb, D_IN), lambda i: (i, 0))],
        compiler_params=_cparams())(x, vec, w_in, b_in)


def _softplus_parts(z):
    e1 = jnp.exp(-jnp.abs(z))
    sp = jnp.maximum(z, 0.0) + jnp.log(1.0 + e1)
    return sp, e1


def _sb_forward(proj, shards, name):
    s = proj.shape[0]
    tq, tk = min(SB_TQ, s), min(SB_TK, s)
    r = tq // tk

    n_sh = len(shards)
    nkb = SB_W // LANES
    nq = s // tq

    assert r % SB_UNROLL == 0, "the sweep below the diagonal takes whole steps"

    def body(q_ref, k_ref, v_ref, *refs):
        sh_refs, (o_ref, tot_ref, start_ref), got_refs = refs[:n_sh], refs[n_sh:n_sh + 3], refs[n_sh + 3:2 * n_sh + 3]
        acc_refs, run_refs = refs[2 * n_sh + 3:2 * n_sh + 5]
        i = pl.program_id(1)
        step = pl.program_id(0) * nq + i
        gather = _gather_exchange(sh_refs, got_refs, *refs[2 * n_sh + 5:])

        @pl.when(step == 0)
        def _():
            gather.start()

        lane = lax.broadcasted_iota(jnp.int32, (1, LANES), 1)
        first = lane < HEAD_DIM
        qp = q_ref[...]
        zero = jnp.zeros((), BF16)
        qs = (jnp.where(first, qp, zero), jnp.where(first, zero, qp))
        later = _sum_matrix(tk, lambda row, col: row > col)
        acc_refs[...] = jnp.zeros_like(acc_refs)
        run_refs[...] = jnp.zeros_like(run_refs)

        def blocks(js, masked):
            ks = [pl.multiple_of(j * tk, tk) for j in js]
            kjs = [k_ref[pl.ds(k0, tk), :] for k0 in ks]
            vjs = [v_ref[pl.ds(k0, tk), :] for k0 in ks]
            chains = [(hd, b) for b in range(len(js)) for hd in range(2)]
            zs = [_dot_nt(qs[hd], kjs[b]) for hd, b in chains]
            sps = [_softplus_parts(z)[0] for z in zs]
            if masked:
                t_idx = i * tq + lax.broadcasted_iota(jnp.int32, (tq, tk), 0)
                befores = [j * tk + lax.broadcasted_iota(jnp.int32, (tq, tk), 1) < t_idx for j in js]
                spms = [jnp.where(befores[b], sp, 0.0) for (hd, b), sp in zip(chains, sps)]
            else:
                spms = sps
            cums = [_block_sums(spm, later) for spm in spms]
            runs = [run_refs[0], run_refs[1]]
            ws = []
            for (hd, b), z, sp, (cum, sm) in zip(chains, zs, sps, cums):
                w = jnp.exp(z - sp - cum - _across(runs[hd], tk))
                if masked:
                    w = jnp.where(befores[b], w, 0.0)
                ws.append(w.astype(BF16))
                runs[hd] = runs[hd] + sm
            pvs = [_dot(w, vjs[b]) for (hd, b), w in zip(chains, ws)]
            for hd in range(2):
                tot = pvs[hd]
                for b in range(1, len(js)):
                    tot = tot + pvs[2 * b + hd]
                acc_refs[hd] += tot
                run_refs[hd] = runs[hd]

        blocks([i * r + (r - 1 - d) for d in range(r)], True)

        below = i * r

        def swept_mass():
            return jnp.min(jnp.minimum(run_refs[0], run_refs[1]))

        def more(carry):
            n, mass = carry
            return (n < below // SB_UNROLL) & (mass < SB_DEAD_MASS)

        def sweep(carry):
            n, _ = carry
            top = below - 1 - SB_UNROLL * n
            blocks([top - u for u in range(SB_UNROLL)], False)
            return n + 1, swept_mass()

        n_swept, _ = lax.while_loop(more, sweep, (0, swept_mass()))
        start_ref[pl.program_id(0), i] = (below - SB_UNROLL * n_swept).astype(F32)
        o_ref[...] = jnp.where(first, acc_refs[0], acc_refs[1])
        tot_ref[...] = jnp.where(first, run_refs[0], run_refs[1])

        @pl.when(step == nkb * nq - 1)
        def _():
            gather.wait()

    shp = jax.ShapeDtypeStruct((s, SB_W), F32)
    qspec = pl.BlockSpec((tq, LANES), lambda p, i: (i, p))
    hbm = pl.BlockSpec(memory_space=pl.ANY)
    return pl.pallas_call(
        body, name=name, grid=(nkb, nq),
        out_shape=[shp, shp, jax.ShapeDtypeStruct((nkb, nq), F32)]
        + [jax.ShapeDtypeStruct((N_CHIPS,) + a.shape, a.dtype) for a in shards],
        in_specs=[qspec,
                  pl.BlockSpec((s, LANES), lambda p, i: (0, nkb + p)),
                  pl.BlockSpec((s, LANES), lambda p, i: (0, 2 * nkb + p))] + [hbm] * n_sh,
        out_specs=[qspec, qspec, pl.BlockSpec(memory_space=pltpu.SMEM)] + [hbm] * n_sh,
        scratch_shapes=[pltpu.VMEM((2, tq, LANES), F32), pltpu.VMEM((2, tq, LANES), F32)] + _exchange_sems(n_sh),
        compiler_params=_cparams())(proj, proj, proj, *shards)


def _swa_masks(n):
    ti = lax.broadcasted_iota(jnp.int32, (WINDOW, 2 * WINDOW), 0)
    kj = lax.broadcasted_iota(jnp.int32, (WINDOW, 2 * WINDOW), 1)
    dist = ti + WINDOW - kj
    valid = (dist >= 0) & (dist < WINDOW) & ((n * WINDOW - WINDOW + kj) >= 0)
    return valid, dist.astype(F32)


def _swa_probs(sc, valid, distf, h, sink):
    slope = 2.0 ** (-(h + 1))
    sc = jnp.where(valid, sc - slope * distf, MASK_VALUE)
    mx = jnp.maximum(jnp.max(sc, axis=1, keepdims=True), sink)
    p = jnp.exp(sc - mx)
    es = jnp.exp(sink - mx)
    inv = 1.0 / (jnp.sum(p, axis=1, keepdims=True) + es)
    return p * inv, es * inv


def _swa_forward(proj, sinks, name):
    s = proj.shape[0]
    nb = s // WINDOW
    qb, kb, vb = 3 * SB_W // SWA_QW, (3 * SB_W + SWA_QW) // LANES, (3 * SB_W + SWA_QW + SWA_KW) // LANES

    def body(q_ref, kp_ref, kc_ref, vp_ref, vc_ref, sink_ref, o_ref):
        n = pl.program_id(0)
        k = jnp.concatenate([kp_ref[...], kc_ref[...]], axis=0)
        v = jnp.concatenate([vp_ref[...], vc_ref[...]], axis=0)
        k_sw = pltpu.roll(k.astype(F32), HEAD_DIM, 1).astype(BF16)
        v_sw = pltpu.roll(v.astype(F32), HEAD_DIM, 1).astype(BF16)
        lane = lax.broadcasted_iota(jnp.int32, (1, LANES), 1)
        halves = [lane < HEAD_DIM, lane >= HEAD_DIM]
        valid, distf = _swa_masks(n)
        heads = range(2 * 4)
        qms = [jnp.where(halves[h % 2], q_ref[:, (h // 2) * LANES:(h // 2 + 1) * LANES], jnp.zeros((), BF16))
               for h in heads]
        kus = [k if h // 4 == h % 2 else k_sw for h in heads]
        vus = [v if h // 4 == h % 2 else v_sw for h in heads]
        scores = [_dot_nt(qms[h], kus[h]) for h in heads]
        ps = [_swa_probs(scores[h], valid, distf, h, sink_ref[h])[0].astype(BF16) for h in heads]
        outs = [_dot(ps[h], vus[h]) for h in heads]
        for pair in range(4):
            o_ref[:, pair * LANES:(pair + 1) * LANES] = jnp.where(halves[0], outs[2 * pair], outs[2 * pair + 1])

    prev = lambda n: jnp.maximum(n - 1, 0)
    return pl.pallas_call(
        body, name=name, grid=(nb,),
        out_shape=jax.ShapeDtypeStruct((s, SWA_QW), F32),
        in_specs=[pl.BlockSpec((WINDOW, SWA_QW), lambda n: (n, qb)),
                  pl.BlockSpec((WINDOW, LANES), lambda n: (prev(n), kb)),
                  pl.BlockSpec((WINDOW, LANES), lambda n: (n, kb)),
                  pl.BlockSpec((WINDOW, LANES), lambda n: (prev(n), vb)),
                  pl.BlockSpec((WINDOW, LANES), lambda n: (n, vb)),
                  pl.BlockSpec(memory_space=pltpu.SMEM)],
        out_specs=pl.BlockSpec((WINDOW, SWA_QW), lambda n: (n, 0)),
        compiler_params=_cparams())(proj, proj, proj, proj, proj, sinks)


def _rms_parts(y):
    return lax.rsqrt(jnp.mean(y * y, axis=1, keepdims=True) + RMS_EPS)


def _post_attention(y_sb, y_sw, x, vec, w_out, name):
    s = x.shape[0]
    tb = min(TOK_TILE, s)

    def body(ysb_ref, ysw_ref, x_ref, vec_ref, w_ref, mixedt_ref, attn_ref, x1_ref, h2_ref, h2t_ref):
        ysb, ysw = ysb_ref[...], ysw_ref[...]
        nsb_f = ysb * _rms_parts(ysb) * vec_ref[V_GN:V_GN + 1, :SB_W]
        nsw_f = ysw * _rms_parts(ysw) * vec_ref[V_GN:V_GN + 1, SB_W:]
        nsb, nsw = nsb_f.astype(BF16), nsw_f.astype(BF16)
        mixedt_ref[:SB_W, :] = nsb_f.T.astype(BF16)
        mixedt_ref[SB_W:, :] = nsw_f.T.astype(BF16)
        attn = _dot(nsb, w_ref[:SB_W, :]) + _dot(nsw, w_ref[SB_W:, :])
        attn_ref[...] = attn
        u1 = ALPHA * x_ref[...] + (1.0 + vec_ref[V_G_A:V_G_A + 1, :]) * attn
        xhat, _ = _layer_norm_stats(u1)
        x1 = xhat * vec_ref[V_LN1G:V_LN1G + 1, :] + vec_ref[V_LN1B:V_LN1B + 1, :]
        x1_ref[...] = x1
        h2 = x1 * (1.0 + vec_ref[V_SC_F:V_SC_F + 1, :]) + vec_ref[V_SH_F:V_SH_F + 1, :]
        h2_ref[...] = h2.astype(BF16)
        h2t_ref[...] = h2.T.astype(BF16)

    half = pl.BlockSpec((tb, SB_W), lambda i: (i, 0))
    full = pl.BlockSpec((tb, D), lambda i: (i, 0))
    full_t = pl.BlockSpec((D, tb), lambda i: (0, i))
    return pl.pallas_call(
        body, name=name, grid=(s // tb,),
        out_shape=[jax.ShapeDtypeStruct((D, s), BF16), jax.ShapeDtypeStruct((s, D), F32),
                   jax.ShapeDtypeStruct((s, D), F32), jax.ShapeDtypeStruct((s, D), BF16),
                   jax.ShapeDtypeStruct((D, s), BF16)],
        in_specs=[half, half, full, _resident((VEC_ROWS, D)), _resident((D, D))],
        out_specs=[full_t, full, full, full, full_t],
        compiler_params=_cparams())(y_sb, y_sw, x, vec, w_out)


def _ffn_forward(h2, w_gu, w_down, name):
    s = h2.shape[0]
    tb = min(FFN_TILE, s)

    def body(h_ref, wgu_ref, wd_ref, gu_ref, actt_ref, ffn_ref):
        gu = _dot(h_ref[...], wgu_ref[...])
        gu_ref[...] = gu.astype(BF16)
        gate, up = gu[:, :D_FF], gu[:, D_FF:]
        act = gate * (1.0 / (1.0 + jnp.exp(-gate))) * up
        actt_ref[...] = act.T.astype(BF16)
        ffn_ref[...] = _dot(act.astype(BF16), wd_ref[...])

    return pl.pallas_call(
        body, name=name, grid=(s // tb,),
        out_shape=[jax.ShapeDtypeStruct((s, 2 * D_FF), BF16), jax.ShapeDtypeStruct((D_FF, s), BF16),
                   jax.ShapeDtypeStruct((s, D), F32)],
        in_specs=[pl.BlockSpec((tb, D), lambda i: (i, 0)), _resident((D, 2 * D_FF)), _resident((D_FF, D))],
        out_specs=[pl.BlockSpec((tb, 2 * D_FF), lambda i: (i, 0)), pl.BlockSpec((D_FF, tb), lambda i: (0, i)),
                   pl.BlockSpec((tb, D), lambda i: (i, 0))],
        compiler_params=_cparams())(h2, w_gu, w_down)


def _layer_norm_bwd(dxhat, xhat, rstd):
    m1 = jnp.mean(dxhat, axis=1, keepdims=True)
    m2 = jnp.mean(dxhat * xhat, axis=1, keepdims=True)
    return rstd * (dxhat - m1 - xhat * m2)


def _colsum(a):
    return jnp.sum(a, axis=0, keepdims=True)


A_LN2G, A_LN2B, A_GF, A_SCF, A_SHF, A_LOSS = range(6)
B_LN1G, B_LN1B, B_GA, B_GN = range(4)
C_SCA, C_SHA = range(2)


def _ffn_backward(x1, ffn, target, gu, vec, w_gu, w_down, name):
    s = x1.shape[0]
    tb = min(FFN_BWD_TILE, s)

    def body(x1_ref, ffn_ref, t_ref, gu_ref, vec_ref, wgu_ref, wd_ref, dffn_ref, dgu_ref, dx1_ref, acc_ref):
        @pl.when(pl.program_id(0) == 0)
        def _():
            acc_ref[...] = jnp.zeros_like(acc_ref)

        x1v, ffn_v = x1_ref[...], ffn_ref[...]
        g_f = 1.0 + vec_ref[V_G_F:V_G_F + 1, :]
        u2 = ALPHA * x1v + g_f * ffn_v
        xhat, rstd = _layer_norm_stats(u2)
        ln_g = vec_ref[V_LN2G:V_LN2G + 1, :]
        err = xhat * ln_g + vec_ref[V_LN2B:V_LN2B + 1, :] - t_ref[...]
        dx2 = err * (1.0 / D)
        acc_ref[A_LOSS:A_LOSS + 1, :] += _colsum(err * err) * (0.5 / D)
        acc_ref[A_LN2G:A_LN2G + 1, :] += _colsum(dx2 * xhat)
        acc_ref[A_LN2B:A_LN2B + 1, :] += _colsum(dx2)
        du2 = _layer_norm_bwd(dx2 * ln_g, xhat, rstd)
        acc_ref[A_GF:A_GF + 1, :] += _colsum(du2 * ffn_v)
        dffn = (g_f * du2).astype(BF16)
        dffn_ref[...] = dffn
        dact = _dot_nt(dffn, wd_ref[...])
        gate, up = gu_ref[:, :D_FF].astype(F32), gu_ref[:, D_FF:].astype(F32)
        sg = 1.0 / (1.0 + jnp.exp(-gate))
        dgate = (dact * up * (sg * (1.0 + gate * (1.0 - sg)))).astype(BF16)
        dup = (dact * (gate * sg)).astype(BF16)
        dgu_ref[:, :D_FF] = dgate
        dgu_ref[:, D_FF:] = dup
        dh2 = _dot_nt(dgate, wgu_ref[:, :D_FF]) + _dot_nt(dup, wgu_ref[:, D_FF:])
        dx1_ref[...] = ALPHA * du2 + dh2 * (1.0 + vec_ref[V_SC_F:V_SC_F + 1, :])
        acc_ref[A_SCF:A_SCF + 1, :] += _colsum(dh2 * x1v)
        acc_ref[A_SHF:A_SHF + 1, :] += _colsum(dh2)

    full = pl.BlockSpec((tb, D), lambda i: (i, 0))
    wide = pl.BlockSpec((tb, 2 * D_FF), lambda i: (i, 0))
    return pl.pallas_call(
        body, name=name, grid=(s // tb,),
        out_shape=[jax.ShapeDtypeStruct((s, D), BF16), jax.ShapeDtypeStruct((s, 2 * D_FF), BF16),
                   jax.ShapeDtypeStruct((s, D), F32), jax.ShapeDtypeStruct((8, D), F32)],
        in_specs=[full, full, full, wide, _resident((VEC_ROWS, D)), _resident((D, 2 * D_FF)), _resident((D_FF, D))],
        out_specs=[full, wide, full, pl.BlockSpec((8, D), lambda i: (0, 0))],
        compiler_params=_cparams())(x1, ffn, target, gu, vec, w_gu, w_down)


def _attn_out_backward(dx1, x, attn, y_sb, y_sw, vec, w_out, name):
    s = x.shape[0]
    tb = min(TOK_TILE, s)

    def body(dx1_ref, x_ref, attn_ref, ysb_ref, ysw_ref, vec_ref, w_ref, du1_ref, dattn_ref, dy_ref, acc_ref):
        @pl.when(pl.program_id(0) == 0)
        def _():
            acc_ref[...] = jnp.zeros_like(acc_ref)

        attn = attn_ref[...]
        g_a = 1.0 + vec_ref[V_G_A:V_G_A + 1, :]
        xhat, rstd = _layer_norm_stats(ALPHA * x_ref[...] + g_a * attn)
        dx1v = dx1_ref[...]
        acc_ref[B_LN1G:B_LN1G + 1, :] += _colsum(dx1v * xhat)
        acc_ref[B_LN1B:B_LN1B + 1, :] += _colsum(dx1v)
        du1 = _layer_norm_bwd(dx1v * vec_ref[V_LN1G:V_LN1G + 1, :], xhat, rstd)
        du1_ref[...] = du1
        acc_ref[B_GA:B_GA + 1, :] += _colsum(du1 * attn)
        dattn = (g_a * du1).astype(BF16)
        dattn_ref[...] = dattn
        dmixed = _dot_nt(dattn, w_ref[...])
        for lo, y_ref in ((0, ysb_ref), (SB_W, ysw_ref)):
            y = y_ref[...]
            rr = _rms_parts(y)
            dn = dmixed[:, lo:lo + SB_W]
            acc_ref[B_GN:B_GN + 1, lo:lo + SB_W] += _colsum(dn * y * rr)
            dng = dn * vec_ref[V_GN:V_GN + 1, lo:lo + SB_W]
            dy_ref[:, lo:lo + SB_W] = rr * dng - y * (rr * rr * rr) * jnp.mean(dng * y, axis=1, keepdims=True)

    half = pl.BlockSpec((tb, SB_W), lambda i: (i, 0))
    full = pl.BlockSpec((tb, D), lambda i: (i, 0))
    return pl.pallas_call(
        body, name=name, grid=(s // tb,),
        out_shape=[jax.ShapeDtypeStruct((s, D), F32), jax.ShapeDtypeStruct((s, D), BF16),
                   jax.ShapeDtypeStruct((s, D), F32), jax.ShapeDtypeStruct((8, D), F32)],
        in_specs=[full, full, full, half, half, _resident((VEC_ROWS, D)), _resident((D, D))],
        out_specs=[full, full, full, pl.BlockSpec((8, D), lambda i: (0, 0))],
        compiler_params=_cparams())(dx1, x, attn, y_sb, y_sw, vec, w_out)


def _sb_backward(proj, sp_total, sweep_start, dy, slabs, name):
    s = proj.shape[0]
    tq, tk = min(SB_TQ, s), min(SB_TK, s)
    r = tq // tk
    nkb = SB_W // LANES
    nq = s // tq

    assert r % SB_UNROLL == 0, "the sweep below the diagonal takes whole steps"

    def body(q_ref, k_ref, v_ref, tot_ref, do_ref, start_ref, slab_ref, dq_ref, dk_ref, dv_ref, got_ref,
             dq_acc, left_refs, gsum_refs, send_sems, recv_sems, local_sems):
        i = pl.program_id(1)
        step = pl.program_id(0) * nq + i
        scatter = _scatter_exchange(slab_ref, got_ref, send_sems, recv_sems, local_sems)

        @pl.when(step == 0)
        def _():
            scatter.start()

        @pl.when(i == 0)
        def _():
            dk_ref[...] = jnp.zeros_like(dk_ref)
            dv_ref[...] = jnp.zeros_like(dv_ref)

        lane = lax.broadcasted_iota(jnp.int32, (1, LANES), 1)
        first = lane < HEAD_DIM
        qp, dop, totp = q_ref[...], do_ref[...], tot_ref[...]
        zero = jnp.zeros((), BF16)
        qs = (jnp.where(first, qp, zero), jnp.where(first, zero, qp))
        dofs = (jnp.where(first, dop, 0.0), jnp.where(first, 0.0, dop))
        dobs = tuple(d.astype(BF16) for d in dofs)
        dots = tuple(d.T.astype(BF16) for d in dofs)
        qts = tuple(qh.astype(F32).T.astype(BF16) for qh in qs)
        later = _sum_matrix(tk, lambda row, col: row > col)
        earlier = _sum_matrix(tk, lambda row, col: row < col)
        dq_acc[...] = jnp.zeros_like(dq_acc)
        gsum_refs[...] = jnp.zeros_like(gsum_refs)
        swapped = pltpu.roll(totp, HEAD_DIM, 1)
        left_refs[0] = jnp.where(first, totp, swapped)
        left_refs[1] = jnp.where(first, swapped, totp)

        def blocks(js, masked):
            ks = [pl.multiple_of(j * tk, tk) for j in js]
            kjs = [k_ref[pl.ds(k0, tk), :] for k0 in ks]
            vjs = [v_ref[pl.ds(k0, tk), :] for k0 in ks]
            chains = [(hd, b) for b in range(len(js)) for hd in range(2)]
            zs = [_dot_nt(qs[hd], kjs[b]) for hd, b in chains]
            dws = [_dot_nt(dobs[hd], vjs[b]) for hd, b in chains]
            parts = [_softplus_parts(z) for z in zs]
            sps = [p[0] for p in parts]
            if masked:
                t_idx = i * tq + lax.broadcasted_iota(jnp.int32, (tq, tk), 0)
                befores = [j * tk + lax.broadcasted_iota(jnp.int32, (tq, tk), 1) < t_idx for j in js]
                spms = [jnp.where(befores[b], sp, 0.0) for (hd, b), sp in zip(chains, sps)]
            else:
                spms = sps
            cums = [_block_sums(spm, later) for spm in spms]
            lefts = [left_refs[0], left_refs[1]]
            ws = []
            for (hd, b), z, sp, (cum, sm) in zip(chains, zs, sps, cums):
                lefts[hd] = lefts[hd] - sm
                w = jnp.exp(z - sp - cum - _across(lefts[hd], tk))
                if masked:
                    w = jnp.where(befores[b], w, 0.0)
                ws.append(w)
            wbs = [w.astype(BF16) for w in ws]
            dvs = [_dot(dots[hd], wb) for (hd, b), wb in zip(chains, wbs)]
            gs = [dw * w for dw, w in zip(dws, ws)]
            gcums = [_block_sums(g, earlier) for g in gs]
            gsums = [gsum_refs[0], gsum_refs[1]]
            dzbs = []
            for (hd, b), z, (sp, e1), g, (gcum, gsm) in zip(chains, zs, parts, gs, gcums):
                inv = 1.0 / (1.0 + e1)
                sig = jnp.where(z >= 0.0, inv, e1 * inv)
                dz = g - sig * (g + _across(gsums[hd], tk) + gcum)
                if masked:
                    dz = jnp.where(befores[b], dz, 0.0)
                dzbs.append(dz.astype(BF16))
                gsums[hd] = gsums[hd] + gsm
            dqs = [_dot(dzb, kjs[b]) for (hd, b), dzb in zip(chains, dzbs)]
            dks = [_dot(qts[hd], dzb) for (hd, b), dzb in zip(chains, dzbs)]
            for b, j in enumerate(js):
                dv_ref[j] += dvs[2 * b] + dvs[2 * b + 1]
                dk_ref[j] += dks[2 * b] + dks[2 * b + 1]
            for hd in range(2):
                tot = dqs[hd]
                for b in range(1, len(js)):
                    tot = tot + dqs[2 * b + hd]
                dq_acc[hd] += tot
                left_refs[hd] = lefts[hd]
                gsum_refs[hd] = gsums[hd]

        below = i * r
        start = jnp.clip(start_ref[pl.program_id(0), i].astype(jnp.int32), 0, below) // SB_UNROLL * SB_UNROLL

        def sweep(n, carry):
            blocks([start + SB_UNROLL * n + u for u in range(SB_UNROLL)], False)
            return carry

        lax.fori_loop(0, (below - start) // SB_UNROLL, sweep, 0)
        blocks([below + d for d in range(r)], True)
        dq_ref[...] = jnp.where(first, dq_acc[0], dq_acc[1])

        @pl.when(step == nkb * nq - 1)
        def _():
            scatter.wait()

    shp = jax.ShapeDtypeStruct((s, SB_W), F32)
    qspec = pl.BlockSpec((tq, LANES), lambda p, i: (i, p))
    whole = pl.BlockSpec((None, s // tk, LANES, tk), lambda p, i: (p, 0, 0, 0))
    shp_t = jax.ShapeDtypeStruct((nkb, s // tk, LANES, tk), F32)
    hbm = pl.BlockSpec(memory_space=pl.ANY)
    return pl.pallas_call(
        body, name=name, grid=(nkb, nq),
        out_shape=[shp, shp_t, shp_t, jax.ShapeDtypeStruct(slabs.shape, slabs.dtype)],
        in_specs=[qspec,
                  pl.BlockSpec((s, LANES), lambda p, i: (0, nkb + p)),
                  pl.BlockSpec((s, LANES), lambda p, i: (0, 2 * nkb + p)),
                  qspec, qspec, pl.BlockSpec(memory_space=pltpu.SMEM), hbm],
        out_specs=[qspec, whole, whole, hbm],
        scratch_shapes=[pltpu.VMEM((2, tq, LANES), F32), pltpu.VMEM((2, tq, LANES), F32), pltpu.VMEM((2, tq, LANES), F32)]
        + _exchange_sems(1),
        compiler_params=_cparams())(proj, proj, proj, sp_total, dy, sweep_start, slabs)


def _swa_backward(proj, y_sw, dy, sinks, name):
    s = proj.shape[0]
    nb = s // WINDOW
    qb, kb, vb = 3 * SB_W // SWA_QW, (3 * SB_W + SWA_QW) // LANES, (3 * SB_W + SWA_QW + SWA_KW) // LANES

    def body(q_ref, kp_ref, kc_ref, vp_ref, vc_ref, o_ref, do_ref, sink_ref, dq_ref, dk_ref, dv_ref, ds_ref):
        n = pl.program_id(0)

        @pl.when(n == 0)
        def _():
            dk_ref[...] = jnp.zeros_like(dk_ref)
            dv_ref[...] = jnp.zeros_like(dv_ref)
            ds_ref[...] = jnp.zeros_like(ds_ref)

        k = jnp.concatenate([kp_ref[...], kc_ref[...]], axis=0)
        v = jnp.concatenate([vp_ref[...], vc_ref[...]], axis=0)
        k_sw = pltpu.roll(k.astype(F32), HEAD_DIM, 1).astype(BF16)
        v_sw = pltpu.roll(v.astype(F32), HEAD_DIM, 1).astype(BF16)
        lane = lax.broadcasted_iota(jnp.int32, (1, LANES), 1)
        halves = [lane < HEAD_DIM, lane >= HEAD_DIM]
        valid, distf = _swa_masks(n)
        heads = range(2 * 4)
        cols = [slice((h // 2) * LANES, (h // 2 + 1) * LANES) for h in heads]
        qms = [jnp.where(halves[h % 2], q_ref[:, cols[h]], jnp.zeros((), BF16)) for h in heads]
        dos = [jnp.where(halves[h % 2], do_ref[:, cols[h]], 0.0) for h in heads]
        dobs = [d.astype(BF16) for d in dos]
        native = [h // 4 == h % 2 for h in heads]
        kus = [k if native[h] else k_sw for h in heads]
        vus = [v if native[h] else v_sw for h in heads]
        scores = [_dot_nt(qms[h], kus[h]) for h in heads]
        dps = [_dot_nt(dobs[h], vus[h]) for h in heads]
        deltas = [jnp.sum(dos[h] * o_ref[:, cols[h]], axis=1, keepdims=True) for h in heads]
        probs = [_swa_probs(scores[h], valid, distf, h, sink_ref[h]) for h in heads]
        pbs = [probs[h][0].astype(BF16) for h in heads]
        dscs = [(probs[h][0] * (dps[h] - deltas[h])).astype(BF16) for h in heads]
        dqs = [_dot(dscs[h], kus[h]) for h in heads]
        dks = [_dot_tn(dscs[h], qms[h]) for h in heads]
        dvs = [_dot_tn(pbs[h], dobs[h]) for h in heads]
        for h in heads:
            ds_ref[h:h + 1, :] += jnp.zeros((1, LANES), F32) - jnp.sum(probs[h][1] * deltas[h])
        for pair in range(4):
            dq_ref[:, cols[2 * pair]] = jnp.where(halves[0], dqs[2 * pair], dqs[2 * pair + 1])

        def gathered(parts):
            nat = sum(parts[h] for h in heads if native[h])
            rot = sum(parts[h] for h in heads if not native[h])
            return nat + pltpu.roll(rot, HEAD_DIM, 1)

        dk, dv = gathered(dks), gathered(dvs)
        prev = pl.multiple_of(jnp.maximum(n - 1, 0) * WINDOW, WINDOW)
        cur = pl.multiple_of(n * WINDOW, WINDOW)
        dk_ref[pl.ds(prev, WINDOW), :] += dk[:WINDOW]
        dv_ref[pl.ds(prev, WINDOW), :] += dv[:WINDOW]
        dk_ref[pl.ds(cur, WINDOW), :] += dk[WINDOW:]
        dv_ref[pl.ds(cur, WINDOW), :] += dv[WINDOW:]

    prev_blk = lambda n: jnp.maximum(n - 1, 0)
    wide = pl.BlockSpec((WINDOW, SWA_QW), lambda n: (n, 0))
    whole = pl.BlockSpec((s, LANES), lambda n: (0, 0))
    return pl.pallas_call(
        body, name=name, grid=(nb,),
        out_shape=[jax.ShapeDtypeStruct((s, SWA_QW), F32), jax.ShapeDtypeStruct((s, LANES), F32),
                   jax.ShapeDtypeStruct((s, LANES), F32), jax.ShapeDtypeStruct((8, LANES), F32)],
        in_specs=[pl.BlockSpec((WINDOW, SWA_QW), lambda n: (n, qb)),
                  pl.BlockSpec((WINDOW, LANES), lambda n: (prev_blk(n), kb)),
                  pl.BlockSpec((WINDOW, LANES), lambda n: (n, kb)),
                  pl.BlockSpec((WINDOW, LANES), lambda n: (prev_blk(n), vb)),
                  pl.BlockSpec((WINDOW, LANES), lambda n: (n, vb)),
                  wide,
                  pl.BlockSpec((WINDOW, SWA_QW), lambda n: (n, 1)),
                  pl.BlockSpec(memory_space=pltpu.SMEM)],
        out_specs=[wide, whole, whole, pl.BlockSpec((8, LANES), lambda n: (0, 0))],
        compiler_params=_cparams())(proj, proj, proj, proj, proj, y_sw, dy, sinks)


def _in_proj_backward(dq_sb, dkt_sb, dvt_sb, dq_sw, dk_sw, dv_sw, du1, x, vec, w_in, name):
    s = x.shape[0]
    tb = min(TOK_TILE, s)
    n_pairs, _, _, tk = dkt_sb.shape

    def body(dqsb_ref, dktsb_ref, dvtsb_ref, dqsw_ref, dksw_ref, dvsw_ref, du1_ref, x_ref, vec_ref, w_ref,
             dproj_ref, gx_ref, acc_ref, bacc_ref):
        @pl.when(pl.program_id(0) == 0)
        def _():
            acc_ref[...] = jnp.zeros_like(acc_ref)
            bacc_ref[...] = jnp.zeros_like(bacc_ref)

        pieces = ((0, dqsb_ref, QK_SCALE), (3 * SB_W, dqsw_ref, QK_SCALE), (3 * SB_W + SWA_QW, dksw_ref, 1.0),
                  (3 * SB_W + SWA_QW + SWA_KW, dvsw_ref, 1.0))
        for lo, ref, scale in pieces:
            width = ref.shape[1]
            piece = ref[...] * scale
            bacc_ref[0:1, lo:lo + width] += _colsum(piece)
            dproj_ref[:, lo:lo + width] = piece.astype(BF16)
        for base, ref in ((SB_W, dktsb_ref), (2 * SB_W, dvtsb_ref)):
            for p in range(n_pairs):
                lo = base + p * LANES
                for jj in range(tb // tk):
                    piece = ref[p, jj].T
                    bacc_ref[0:1, lo:lo + LANES] += _colsum(piece)
                    dproj_ref[jj * tk:(jj + 1) * tk, lo:lo + LANES] = piece.astype(BF16)
        dh = _dot_nt(dproj_ref[...], w_ref[...])
        xv = x_ref[...]
        gx_ref[...] = ALPHA * du1_ref[...] + dh * (1.0 + vec_ref[V_SC_A:V_SC_A + 1, :])
        acc_ref[C_SCA:C_SCA + 1, :] += _colsum(dh * xv)
        acc_ref[C_SHA:C_SHA + 1, :] += _colsum(dh)

    half = pl.BlockSpec((tb, SB_W), lambda i: (i, 0))
    narrow = pl.BlockSpec((tb, LANES), lambda i: (i, 0))
    full = pl.BlockSpec((tb, D), lambda i: (i, 0))
    blocks_t = pl.BlockSpec((n_pairs, tb // tk, LANES, tk), lambda i: (0, i, 0, 0))
    return pl.pallas_call(
        body, name=name, grid=(s // tb,),
        out_shape=[jax.ShapeDtypeStruct((s, D_IN), BF16), jax.ShapeDtypeStruct((s, D), F32),
                   jax.ShapeDtypeStruct((8, D), F32), jax.ShapeDtypeStruct((8, D_IN), F32)],
        in_specs=[half, blocks_t, blocks_t, half, narrow, narrow, full, full, _resident((VEC_ROWS, D)),
                  _resident((D, D_IN))],
        out_specs=[pl.BlockSpec((tb, D_IN), lambda i: (i, 0)), full, pl.BlockSpec((8, D), lambda i: (0, 0)),
                   pl.BlockSpec((8, D_IN), lambda i: (0, 0))],
        compiler_params=_cparams())(dq_sb, dkt_sb, dvt_sb, dq_sw, dk_sw, dv_sw, du1, x, vec, w_in)


def _weight_grad(at, b, name, col_shards=1):
    m, s = at.shape
    n = b.shape[1]
    if col_shards > 1:
        tn = n // col_shards
        out_shape = jax.ShapeDtypeStruct((col_shards, m, tn), F32)
        out_spec = pl.BlockSpec((None, m, tn), lambda j, k: (j, 0, 0))
    else:
        tn = 512 if n % 512 == 0 else n
        out_shape = jax.ShapeDtypeStruct((m, n), F32)
        out_spec = pl.BlockSpec((m, tn), lambda j, k: (0, j))
    ts = min(512, s)

    def body(at_ref, b_ref, o_ref):
        @pl.when(pl.program_id(1) == 0)
        def _():
            o_ref[...] = jnp.zeros_like(o_ref)

        o_ref[...] += _dot(at_ref[...], b_ref[...])

    return pl.pallas_call(
        body, name=name, grid=(n // tn, s // ts),
        out_shape=out_shape,
        in_specs=[pl.BlockSpec((m, ts), lambda j, k: (0, k)), pl.BlockSpec((ts, tn), lambda j, k: (k, j))],
        out_specs=out_spec,
        compiler_params=_cparams())(at, b)


def _pad_rows(v, rows):
    return jnp.concatenate([v, jnp.zeros((rows - v.shape[0], v.shape[1]), v.dtype)], axis=0)


def _col_shards(w, n_shards):
    r, n = w.shape
    return w.reshape(r, n_shards, n // n_shards).transpose(1, 0, 2)


def kernel(x, c, w_ada, b_ada, w_in, b_in, sinks, gn_sb, gn_swa, w_out, ln1_g, ln1_b, w_gu, w_down, ln2_g, ln2_b, loss_target, m_w_ada, m_b_ada, m_w_in, m_b_in, m_sinks, m_gn_sb, m_gn_swa, m_w_out, m_ln1_g, m_ln1_b, m_w_gu, m_w_down, m_ln2_g, m_ln2_b, v_w_ada, v_b_ada, v_w_in, v_b_in, v_sinks, v_gn_sb, v_gn_swa, v_w_out, v_ln1_g, v_ln1_b, v_w_gu, v_w_down, v_ln2_g, v_ln2_b):
    ix, iy, ic = lax.axis_index("x"), lax.axis_index("y"), lax.axis_index("c")
    chip = 2 * ix + iy
    dev = 4 * ix + 2 * iy + ic
    xs, target = x[0], loss_target[0]
    s = xs.shape[0]

    c_all = _allgather8(_pad_rows(c, 8), "gather_c")[::8]
    n_ada = w_ada.shape[2]
    b_ada_shard = lax.dynamic_slice_in_dim(b_ada, chip * n_ada, n_ada, axis=1)
    mod_cols, silu_c = _mod_shard(c_all, w_ada[0], b_ada_shard, "mod_shard")
    mod_all = _allgather8(mod_cols, "gather_mod").reshape(N_DEV, 8, n_ada)
    mod_mine = lax.dynamic_index_in_dim(mod_all, dev, axis=1, keepdims=False)
    mod = mod_mine.reshape(N_CHIPS, 2, n_ada)[:, 0].reshape(6, D)
    vec = jnp.concatenate([mod, ln1_g, ln1_b, ln2_g, ln2_b, jnp.concatenate([gn_sb, gn_swa], axis=1),
                           jnp.zeros((VEC_ROWS - 11, D), F32)], axis=0)

    (g_in,) = _chip_allgather([w_in[0].astype(BF16)], "gather_w_in")
    w_in_b = g_in.transpose(1, 0, 2).reshape(D, D_IN)

    h_t, proj = _in_proj(xs, vec, w_in_b, b_in, "in_proj")
    y_sb, sp_total, sweep_start, g_out, g_gu, g_down = _sb_forward(
        proj, [w_out[0].astype(BF16), w_gu[0].astype(BF16), w_down[0].astype(BF16)], "sb_forward")
    w_gu_b = g_gu.transpose(1, 0, 2).reshape(D, 2 * D_FF)
    w_out_b = g_out.reshape(D, D)
    w_down_b = g_down.reshape(D_FF, D)
    sink_vec = sinks[0]
    y_sw = _swa_forward(proj, sink_vec, "swa_forward")
    mixed_t, attn, x1, h2_b, h2_t = _post_attention(y_sb, y_sw, xs, vec, w_out_b, "post_attention")
    gu, act_t, ffn = _ffn_forward(h2_b, w_gu_b, w_down_b, "ffn_forward")

    def chip_sums(per_shard, tag):
        flat = jnp.concatenate(per_shard, axis=1)
        rows = flat.shape[1] // (2 * LANES)
        halves = flat.reshape(4, 2, rows, LANES)
        keep = lax.dynamic_index_in_dim(halves, ic, axis=1, keepdims=False)
        give = lax.dynamic_index_in_dim(halves, 1 - ic, axis=1, keepdims=False)
        got = _sibling_send(give, "grad_halves_swap_" + tag)
        both = _add2(keep.reshape(4 * rows, LANES), got.reshape(4 * rows, LANES), "grad_chip_sum_" + tag)
        return both.reshape(4, rows, LANES)

    dffn_b, dgu_b, dx1, acc_f = _ffn_backward(x1, ffn, target, gu, vec, w_gu_b, w_down_b, "ffn_backward")
    dw_gu = _weight_grad(h2_t, dgu_b, "grad_w_gu", col_shards=4)
    dw_down = _weight_grad(act_t, dffn_b, "grad_w_down")
    du1, dattn_b, dy, acc_a = _attn_out_backward(dx1, xs, attn, y_sb, y_sw, vec, w_out_b, "attn_out_backward")
    dw_out = _weight_grad(mixed_t, dattn_b, "grad_w_out")
    sums_f = chip_sums([dw_gu.reshape(4, -1), dw_down.reshape(4, -1), dw_out.reshape(4, -1)], "ffn")
    dq_sb, dk_sb, dv_sb, parts_f = _sb_backward(proj, sp_total, sweep_start, dy, sums_f, "sb_backward")
    dq_sw, dk_sw, dv_sw, dsink = _swa_backward(proj, y_sw, dy, sink_vec, "swa_backward")
    dproj_b, grad_x, acc_i, acc_b = _in_proj_backward(dq_sb, dk_sb, dv_sb, dq_sw, dk_sw, dv_sw, du1, xs, vec, w_in_b,
                                                      "in_proj_backward")
    dw_in = _weight_grad(h_t, dproj_b, "grad_w_in")
    sums_a = chip_sums([_col_shards(dw_in, 4).reshape(4, -1)], "attn")
    parts_a = _chip_scatter(sums_a, "grad_chip_scatter_attn")
    my_half = jnp.concatenate([_sum4(parts_f, "grad_reduce_ffn"), _sum4(parts_a, "grad_reduce_attn")], axis=0)
    other_half = _sibling_send(my_half, "grad_half_return")
    rows_f = parts_f.shape[1]

    def whole_shard(lo, hi):
        mine, other = my_half[lo:hi], other_half[lo:hi]
        return jnp.concatenate([jnp.where(ic == 0, mine, other), jnp.where(ic == 0, other, mine)], axis=0).reshape(-1)

    flat_f, flat_a = whole_shard(0, rows_f), whole_shard(rows_f, my_half.shape[0])
    n_gu, n_down = D * (2 * D_FF // 4), (D_FF // 4) * D
    gw_gu = flat_f[:n_gu].reshape(D, 2 * D_FF // 4)
    gw_down = flat_f[n_gu:n_gu + n_down].reshape(D_FF // 4, D)
    gw_out = flat_f[n_gu + n_down:].reshape(D // 4, D)
    gw_in = flat_a.reshape(D, D_IN // 4)

    dmod = jnp.concatenate([acc_i[C_SHA:C_SHA + 1], acc_i[C_SCA:C_SCA + 1], acc_a[B_GA:B_GA + 1],
                            acc_f[A_SHF:A_SHF + 1], acc_f[A_SCF:A_SCF + 1], acc_f[A_GF:A_GF + 1]], axis=1)
    dsink_row = jnp.concatenate([dsink[:, 0].reshape(1, 8), jnp.zeros((1, LANES - 8), F32)], axis=1)
    loss_row = jnp.concatenate([jnp.sum(acc_f[A_LOSS:A_LOSS + 1], axis=1, keepdims=True),
                                jnp.zeros((1, LANES - 1), F32)], axis=1)
    small = jnp.concatenate([dmod, acc_b[0:1], acc_a[B_LN1G:B_LN1G + 1], acc_a[B_LN1B:B_LN1B + 1],
                             acc_f[A_LN2G:A_LN2G + 1], acc_f[A_LN2B:A_LN2B + 1], acc_a[B_GN:B_GN + 1],
                             dsink_row, loss_row], axis=1)
    small_all = _allgather8(_pad_rows(small, 8), "gather_small")[::8]

    def pack_small(b_ada_, b_in_, ln1g_, ln1b_, ln2g_, ln2b_, gsb_, gsw_, sinks_):
        return jnp.concatenate([b_ada_, b_in_, ln1g_, ln1b_, ln2g_, ln2b_, gsb_, gsw_, sinks_,
                                jnp.ones((1, 2 * LANES - 8), F32)], axis=1)

    w_small = pack_small(b_ada, b_in, ln1_g, ln1_b, ln2_g, ln2_b, gn_sb, gn_swa, sinks)
    m_small = pack_small(m_b_ada, m_b_in, m_ln1_g, m_ln1_b, m_ln2_g, m_ln2_b, m_gn_sb, m_gn_swa, m_sinks)
    v_small = pack_small(v_b_ada, v_b_in, v_ln1_g, v_ln1_b, v_ln2_g, v_ln2_b, v_gn_sb, v_gn_swa, v_sinks)
    small_out = _small_update(small_all, w_small, m_small, v_small, "small_update")

    def unpack_small(row):
        return {"b_ada": row[:, SM_MOD:SM_BIN], "b_in": row[:, SM_BIN:SM_LN1G], "ln1_g": row[:, SM_LN1G:SM_LN1B],
                "ln1_b": row[:, SM_LN1B:SM_LN2G], "ln2_g": row[:, SM_LN2G:SM_LN2B], "ln2_b": row[:, SM_LN2B:SM_GN],
                "gn_sb": row[:, SM_GN:SM_GN + SB_W], "gn_swa": row[:, SM_GN + SB_W:SM_SINK],
                "sinks": row[:, SM_SINK:SM_SINK + 8]}

    g_small, d_small, m2_small, v2_small = [unpack_small(r) for r in small_out]
    loss = small_out[0][0, SM_LOSS]

    dmod_cols = lax.dynamic_slice_in_dim(small_all[:, SM_MOD:SM_BIN], chip * n_ada, n_ada, axis=1)
    gw_ada = _weight_grad(_pad_rows(silu_c, LANES).astype(BF16).T, _pad_rows(dmod_cols, LANES).astype(BF16), "grad_w_ada")

    big = {}
    for nm, w, g, m, v in (("w_ada", w_ada, gw_ada, m_w_ada, v_w_ada), ("w_in", w_in, gw_in, m_w_in, v_w_in),
                           ("w_out", w_out, gw_out, m_w_out, v_w_out), ("w_gu", w_gu, gw_gu, m_w_gu, v_w_gu),
                           ("w_down", w_down, gw_down, m_w_down, v_w_down)):
        d_, m2_, v2_ = _adamw(w[0], g, m[0], v[0], "adamw_" + nm)
        big[nm] = (g[None], d_[None], m2_[None], v2_[None])

    order = ["w_ada", "b_ada", "w_in", "b_in", "sinks", "gn_sb", "gn_swa", "w_out", "ln1_g", "ln1_b", "w_gu", "w_down",
             "ln2_g", "ln2_b"]

    def leaf(nm, which):
        if nm in big:
            return big[nm][which]
        return (g_small, d_small, m2_small, v2_small)[which][nm]

    outs = [loss, grad_x[None]]
    for which in range(4):
        outs += [leaf(nm, which) for nm in order]
    return tuple(outs)
```

```python
import functools
import math

import jax
import jax.numpy as jnp
from jax import lax
from jax.experimental import pallas as pl
from jax.experimental.pallas import tpu as pltpu

F32 = jnp.float32
BF16 = jnp.bfloat16

D = 1024
HEAD_DIM = 64
SB_W = 512
SWA_QW = 512
SWA_KW = 128
D_IN = 2304
D_FF = 2816
WINDOW = 128
ALPHA = 2.0 ** 0.25
LN_EPS = 1e-5
RMS_EPS = 1e-6
MASK_VALUE = -1e30
QK_SCALE = 1.0 / math.sqrt(HEAD_DIM)

ADAM_LR = 0.001
ADAM_B1 = 0.9
ADAM_B2 = 0.999
ADAM_EPS = 1e-08
ADAM_WD = 0.01
ADAM_STEP = 10

N_CHIPS = 4
N_DEV = 8
LANES = 128

SB_TQ = 512
SB_TK = 256
SB_UNROLL = 1
SB_DEAD_MASS = 110.0
TOK_TILE = 512
FFN_TILE = 256
FFN_BWD_TILE = 256
VMEM_LIMIT = 56 * 1024 * 1024

V_SH_A, V_SC_A, V_G_A, V_SH_F, V_SC_F, V_G_F, V_LN1G, V_LN1B, V_LN2G, V_LN2B, V_GN = range(11)
VEC_ROWS = 16

SM_MOD = 0
SM_BIN = 6 * D
SM_LN1G = SM_BIN + D_IN
SM_LN1B = SM_LN1G + D
SM_LN2G = SM_LN1B + D
SM_LN2B = SM_LN2G + D
SM_GN = SM_LN2B + D
SM_SINK = SM_GN + D
SM_LOSS = SM_SINK + LANES
SM_LEN = SM_LOSS + LANES

MESH = pl.DeviceIdType.MESH


def _cparams(**kw):
    return pltpu.CompilerParams(vmem_limit_bytes=VMEM_LIMIT, **kw)


def _resident(shape):
    nd = len(shape)
    return pl.BlockSpec(shape, lambda *_: (0,) * nd, pipeline_mode=pl.Buffered(1))


def _dot(a, b):
    return jnp.dot(a, b, preferred_element_type=F32)


def _dot_nt(a, b):
    return lax.dot_general(a, b, (((1,), (1,)), ((), ())), preferred_element_type=F32)


def _dot_tn(a, b):
    return lax.dot_general(a, b, (((0,), (0,)), ((), ())), preferred_element_type=F32)


def _sum_matrix(tk, keep):
    row = lax.broadcasted_iota(jnp.int32, (tk, tk + LANES), 0)
    col = lax.broadcasted_iota(jnp.int32, (tk, tk + LANES), 1)
    return (keep(row, col) | (col >= tk)).astype(BF16)


def _block_sums(x, m):
    tk = x.shape[1]
    res = _dot(x.astype(BF16), m)
    return res[:, :tk], res[:, tk:]


def _across(v, tk):
    return jnp.concatenate([v] * (tk // LANES), axis=1)


def _allgather8(v, name):
    m_per, n = v.shape

    def body(x_ref, out_ref, send_sems, recv_sems, local_sem):
        x, y, c = lax.axis_index("x"), lax.axis_index("y"), lax.axis_index("c")
        me, sibling = (x, y, c), (x, y, 1 - c)
        chips = [(1 - x, y), (x, 1 - y), (1 - x, 1 - y)]

        def rows(px, py, pc):
            return out_ref.at[pl.ds((4 * px + 2 * py + pc) * m_per, m_per), :]

        def copy(k, block, to, src=None):
            return pltpu.make_async_remote_copy(
                src_ref=rows(*block) if src is None else src, dst_ref=rows(*block),
                send_sem=send_sems.at[k], recv_sem=recv_sems.at[k], device_id=to, device_id_type=MESH)

        mine = pltpu.make_async_copy(x_ref, rows(*me), local_sem)
        mine.start()
        first = [copy(0, me, sibling, src=x_ref)]
        first += [copy(1 + j, me, (*chip, c), src=x_ref) for j, chip in enumerate(chips)]
        for cp in first:
            cp.start()
        passed = [copy(4 + j, (*chip, c), sibling) for j, chip in enumerate(chips)]
        for j, chip in enumerate(chips):
            copy(1 + j, (*chip, c), me).wait_recv()
            passed[j].start()
        copy(0, sibling, me).wait_recv()
        for j, chip in enumerate(chips):
            copy(4 + j, (*chip, 1 - c), me).wait_recv()
        for cp in first + passed:
            cp.wait_send()
        mine.wait()

    return pl.pallas_call(
        body, name=name,
        out_shape=jax.ShapeDtypeStruct((N_DEV * m_per, n), v.dtype),
        in_specs=[pl.BlockSpec(memory_space=pltpu.VMEM)],
        out_specs=pl.BlockSpec(memory_space=pltpu.VMEM),
        scratch_shapes=[pltpu.SemaphoreType.DMA((7,)), pltpu.SemaphoreType.DMA((7,)), pltpu.SemaphoreType.DMA],
        compiler_params=_cparams(),
    )(v)


class _Exchange:
    def __init__(self, local, sends, arrivals):
        self.local, self.sends, self.arrivals = local, sends, arrivals

    def start(self):
        for cp in self.local + self.sends:
            cp.start()

    def wait(self):
        for cp in self.arrivals:
            cp.wait_recv()
        for cp in self.sends:
            cp.wait_send()
        for cp in self.local:
            cp.wait()


def _exchange_sems(n):
    return [pltpu.SemaphoreType.DMA((3 * n,)), pltpu.SemaphoreType.DMA((3 * n,)), pltpu.SemaphoreType.DMA((n,))]


def _gather_exchange(ins, outs, send_sems, recv_sems, local_sems):
    x, y, c = lax.axis_index("x"), lax.axis_index("y"), lax.axis_index("c")
    slot = 2 * x + y
    chips = [(1 - x, y), (x, 1 - y), (1 - x, 1 - y)]
    local, sends, arrivals = [], [], []
    for a in range(len(ins)):
        local.append(pltpu.make_async_copy(ins[a], outs[a].at[slot], local_sems.at[a]))
        for j, (px, py) in enumerate(chips):
            sems = dict(send_sem=send_sems.at[3 * a + j], recv_sem=recv_sems.at[3 * a + j],
                        device_id=(px, py, c), device_id_type=MESH)
            sends.append(pltpu.make_async_remote_copy(src_ref=ins[a], dst_ref=outs[a].at[slot], **sems))
            arrivals.append(pltpu.make_async_remote_copy(src_ref=ins[a], dst_ref=outs[a].at[2 * px + py], **sems))
    return _Exchange(local, sends, arrivals)


def _scatter_exchange(p_ref, out_ref, send_sems, recv_sems, local_sems):
    x, y, c = lax.axis_index("x"), lax.axis_index("y"), lax.axis_index("c")
    slot = 2 * x + y
    chips = [(1 - x, y), (x, 1 - y), (1 - x, 1 - y)]
    local = [pltpu.make_async_copy(p_ref.at[slot], out_ref.at[slot], local_sems.at[0])]
    sends, arrivals = [], []
    for j, (px, py) in enumerate(chips):
        sems = dict(send_sem=send_sems.at[j], recv_sem=recv_sems.at[j], device_id=(px, py, c), device_id_type=MESH)
        sends.append(pltpu.make_async_remote_copy(src_ref=p_ref.at[2 * px + py], dst_ref=out_ref.at[slot], **sems))
        arrivals.append(pltpu.make_async_remote_copy(src_ref=p_ref.at[slot], dst_ref=out_ref.at[2 * px + py], **sems))
    return _Exchange(local, sends, arrivals)


def _chip_allgather(arrs, name):
    n = len(arrs)

    def body(*refs):
        ex = _gather_exchange(refs[:n], refs[n:2 * n], *refs[2 * n:])
        ex.start()
        ex.wait()

    hbm = pl.BlockSpec(memory_space=pl.ANY)
    return pl.pallas_call(
        body, name=name,
        out_shape=[jax.ShapeDtypeStruct((N_CHIPS,) + a.shape, a.dtype) for a in arrs],
        in_specs=[hbm] * n, out_specs=[hbm] * n,
        scratch_shapes=_exchange_sems(n),
        compiler_params=_cparams(),
    )(*arrs)


def _sibling_copy(v_ref, out_ref, send_sem, recv_sem):
    x, y, c = lax.axis_index("x"), lax.axis_index("y"), lax.axis_index("c")
    return pltpu.make_async_remote_copy(src_ref=v_ref, dst_ref=out_ref, send_sem=send_sem, recv_sem=recv_sem,
                                        device_id=(x, y, 1 - c), device_id_type=MESH)


def _sibling_send(v, name):
    def body(v_ref, out_ref, send_sem, recv_sem):
        cp = _sibling_copy(v_ref, out_ref, send_sem, recv_sem)
        cp.start()
        cp.wait()

    hbm = pl.BlockSpec(memory_space=pl.ANY)
    return pl.pallas_call(
        body, name=name, out_shape=jax.ShapeDtypeStruct(v.shape, v.dtype),
        in_specs=[hbm], out_specs=hbm,
        scratch_shapes=[pltpu.SemaphoreType.DMA, pltpu.SemaphoreType.DMA],
        compiler_params=_cparams(),
    )(v)


def _chip_scatter(p, name):
    def body(p_ref, out_ref, send_sems, recv_sems, local_sems):
        ex = _scatter_exchange(p_ref, out_ref, send_sems, recv_sems, local_sems)
        ex.start()
        ex.wait()

    hbm = pl.BlockSpec(memory_space=pl.ANY)
    return pl.pallas_call(
        body, name=name, out_shape=jax.ShapeDtypeStruct(p.shape, p.dtype),
        in_specs=[hbm], out_specs=hbm,
        scratch_shapes=_exchange_sems(1),
        compiler_params=_cparams(),
    )(p)


def _add2(a, b, name):
    rows = a.shape[0]
    tr = rows // 8

    def body(a_ref, b_ref, o_ref):
        o_ref[...] = a_ref[...] + b_ref[...]

    spec = pl.BlockSpec((tr, LANES), lambda i: (i, 0))
    return pl.pallas_call(body, name=name, grid=(rows // tr,), out_shape=jax.ShapeDtypeStruct(a.shape, a.dtype),
                          in_specs=[spec, spec], out_specs=spec, compiler_params=_cparams())(a, b)


def _sum4(p, name):
    rows = p.shape[1]
    tr = rows // 8

    def body(p_ref, o_ref):
        o_ref[...] = ((p_ref[0] + p_ref[1]) + p_ref[2]) + p_ref[3]

    return pl.pallas_call(
        body, name=name, grid=(rows // tr,), out_shape=jax.ShapeDtypeStruct((rows, LANES), p.dtype),
        in_specs=[pl.BlockSpec((4, tr, LANES), lambda i: (0, i, 0))],
        out_specs=pl.BlockSpec((tr, LANES), lambda i: (i, 0)), compiler_params=_cparams())(p)


def _adam_math(w, g, m, v):
    m2 = ADAM_B1 * m + (1.0 - ADAM_B1) * g
    v2 = ADAM_B2 * v + (1.0 - ADAM_B2) * (g * g)
    m_hat = m2 / (1.0 - ADAM_B1 ** ADAM_STEP)
    v_hat = v2 / (1.0 - ADAM_B2 ** ADAM_STEP)
    delta = -ADAM_LR * (m_hat / (jnp.sqrt(v_hat) + ADAM_EPS) + ADAM_WD * w)
    return delta, m2, v2


def _adamw(w, g, m, v, name):
    rows, cols = w.shape
    tr = rows // 4 if rows % 32 == 0 else rows

    def body(w_ref, g_ref, m_ref, v_ref, d_ref, m2_ref, v2_ref):
        delta, m2, v2 = _adam_math(w_ref[...], g_ref[...], m_ref[...], v_ref[...])
        d_ref[...] = delta
        m2_ref[...] = m2
        v2_ref[...] = v2

    spec = pl.BlockSpec((tr, cols), lambda i: (i, 0))
    shp = jax.ShapeDtypeStruct(w.shape, F32)
    return pl.pallas_call(body, name=name, grid=(rows // tr,), out_shape=[shp, shp, shp],
                          in_specs=[spec] * 4, out_specs=[spec] * 3, compiler_params=_cparams())(w, g, m, v)


def _small_update(g8, w, m, v, name):
    n = w.shape[1]

    def body(g8_ref, w_ref, m_ref, v_ref, g_ref, d_ref, m2_ref, v2_ref):
        g = g8_ref[0:1, :]
        for r in range(1, N_DEV):
            g = g + g8_ref[r:r + 1, :]
        delta, m2, v2 = _adam_math(w_ref[...], g, m_ref[...], v_ref[...])
        g_ref[...] = g
        d_ref[...] = delta
        m2_ref[...] = m2
        v2_ref[...] = v2

    shp = jax.ShapeDtypeStruct((1, n), F32)
    vm = pl.BlockSpec(memory_space=pltpu.VMEM)
    return pl.pallas_call(body, name=name, out_shape=[shp] * 4, in_specs=[vm] * 4, out_specs=[vm] * 4,
                          compiler_params=_cparams())(g8, w, m, v)


def _mod_shard(c8, w_ada, b_ada_shard, name):
    n = w_ada.shape[1]
    tn = 512

    def body(c_ref, w_ref, b_ref, o_ref, s_ref):
        cv = c_ref[...]
        sc = cv * (1.0 / (1.0 + jnp.exp(-cv)))
        s_ref[...] = sc
        o_ref[...] = _dot(sc.astype(BF16), w_ref[...].astype(BF16)) + b_ref[...]

    return pl.pallas_call(
        body, name=name, grid=(n // tn,),
        out_shape=[jax.ShapeDtypeStruct((8, n), F32), jax.ShapeDtypeStruct((8, D), F32)],
        in_specs=[pl.BlockSpec((8, D), lambda j: (0, 0)), pl.BlockSpec((D, tn), lambda j: (0, j)),
                  pl.BlockSpec((1, tn), lambda j: (0, j))],
        out_specs=[pl.BlockSpec((8, tn), lambda j: (0, j)), pl.BlockSpec((8, D), lambda j: (0, 0))],
        compiler_params=_cparams())(c8, w_ada, b_ada_shard)


def _layer_norm_stats(u):
    mu = jnp.mean(u, axis=1, keepdims=True)
    d = u - mu
    var = jnp.mean(d * d, axis=1, keepdims=True)
    rstd = lax.rsqrt(var + LN_EPS)
    return d * rstd, rstd


def _in_proj(x, vec, w_in, b_in, name):
    s = x.shape[0]
    tb = min(TOK_TILE, s)

    def body(x_ref, vec_ref, w_ref, b_ref, ht_ref, p_ref):
        h = x_ref[...] * (1.0 + vec_ref[V_SC_A:V_SC_A + 1, :]) + vec_ref[V_SH_A:V_SH_A + 1, :]
        hb = h.astype(BF16)
        ht_ref[...] = h.T.astype(BF16)
        proj = _dot(hb, w_ref[...]) + b_ref[...]
        col = lax.broadcasted_iota(jnp.int32, (1, D_IN), 1)
        is_q = (col < SB_W) | ((col >= 3 * SB_W) & (col < 3 * SB_W + SWA_QW))
        p_ref[...] = (proj * jnp.where(is_q, QK_SCALE, 1.0)).astype(BF16)

    return pl.pallas_call(
        body, name=name, grid=(s // tb,),
        out_shape=[jax.ShapeDtypeStruct((D, s), BF16), jax.ShapeDtypeStruct((s, D_IN), BF16)],
        in_specs=[pl.BlockSpec((tb, D), lambda i: (i, 0)), _resident((VEC_ROWS, D)), _resident((D, D_IN)),
                  _resident((1, D_IN))],
        out_specs=[pl.BlockSpec((D, tb), lambda i: (0, i)), pl.BlockSpec((tb, D_IN), lambda i: (i, 0))],
        compiler_params=_cparams())(x, vec, w_in, b_in)


def _softplus_parts(z):
    e1 = jnp.exp(-jnp.abs(z))
    sp = jnp.maximum(z, 0.0) + jnp.log(1.0 + e1)
    return sp, e1


def _sb_forward(proj, shards, name):
    s = proj.shape[0]
    tq, tk = min(SB_TQ, s), min(SB_TK, s)
    r = tq // tk

    n_sh = len(shards)
    nkb = SB_W // LANES
    nq = s // tq

    assert r % SB_UNROLL == 0, "the sweep below the diagonal takes whole steps"

    def body(q_ref, k_ref, v_ref, *refs):
        sh_refs, (o_ref, tot_ref, start_ref), got_refs = refs[:n_sh], refs[n_sh:n_sh + 3], refs[n_sh + 3:2 * n_sh + 3]
        acc_refs, run_refs = refs[2 * n_sh + 3:2 * n_sh + 5]
        i = pl.program_id(1)
        step = pl.program_id(0) * nq + i
        gather = _gather_exchange(sh_refs, got_refs, *refs[2 * n_sh + 5:])

        @pl.when(step == 0)
        def _():
            gather.start()

        lane = lax.broadcasted_iota(jnp.int32, (1, LANES), 1)
        first = lane < HEAD_DIM
        qp = q_ref[...]
        zero = jnp.zeros((), BF16)
        qs = (jnp.where(first, qp, zero), jnp.where(first, zero, qp))
        later = _sum_matrix(tk, lambda row, col: row > col)
        acc_refs[...] = jnp.zeros_like(acc_refs)
        run_refs[...] = jnp.zeros_like(run_refs)

        def blocks(js, masked):
            ks = [pl.multiple_of(j * tk, tk) for j in js]
            kjs = [k_ref[pl.ds(k0, tk), :] for k0 in ks]
            vjs = [v_ref[pl.ds(k0, tk), :] for k0 in ks]
            chains = [(hd, b) for b in range(len(js)) for hd in range(2)]
            zs = [_dot_nt(qs[hd], kjs[b]) for hd, b in chains]
            sps = [_softplus_parts(z)[0] for z in zs]
            if masked:
                t_idx = i * tq + lax.broadcasted_iota(jnp.int32, (tq, tk), 0)
                befores = [j * tk + lax.broadcasted_iota(jnp.int32, (tq, tk), 1) < t_idx for j in js]
                spms = [jnp.where(befores[b], sp, 0.0) for (hd, b), sp in zip(chains, sps)]
            else:
                spms = sps
            cums = [_block_sums(spm, later) for spm in spms]
            runs = [run_refs[0], run_refs[1]]
            ws = []
            for (hd, b), z, sp, (cum, sm) in zip(chains, zs, sps, cums):
                w = jnp.exp(z - sp - cum - _across(runs[hd], tk))
                if masked:
                    w = jnp.where(befores[b], w, 0.0)
                ws.append(w.astype(BF16))
                runs[hd] = runs[hd] + sm
            pvs = [_dot(w, vjs[b]) for (hd, b), w in zip(chains, ws)]
            for hd in range(2):
                tot = pvs[hd]
                for b in range(1, len(js)):
                    tot = tot + pvs[2 * b + hd]
                acc_refs[hd] += tot
                run_refs[hd] = runs[hd]

        blocks([i * r + (r - 1 - d) for d in range(r)], True)

        below = i * r

        def swept_mass():
            return jnp.min(jnp.minimum(run_refs[0], run_refs[1]))

        def more(carry):
            n, mass = carry
            return (n < below // SB_UNROLL) & (mass < SB_DEAD_MASS)

        def sweep(carry):
            n, _ = carry
            top = below - 1 - SB_UNROLL * n
            blocks([top - u for u in range(SB_UNROLL)], False)
            return n + 1, swept_mass()

        n_swept, _ = lax.while_loop(more, sweep, (0, swept_mass()))
        start_ref[pl.program_id(0), i] = (below - SB_UNROLL * n_swept).astype(F32)
        o_ref[...] = jnp.where(first, acc_refs[0], acc_refs[1])
        tot_ref[...] = jnp.where(first, run_refs[0], run_refs[1])

        @pl.when(step == nkb * nq - 1)
        def _():
            gather.wait()

    shp = jax.ShapeDtypeStruct((s, SB_W), F32)
    qspec = pl.BlockSpec((tq, LANES), lambda p, i: (i, p))
    hbm = pl.BlockSpec(memory_space=pl.ANY)
    return pl.pallas_call(
        body, name=name, grid=(nkb, nq),
        out_shape=[shp, shp, jax.ShapeDtypeStruct((nkb, nq), F32)]
        + [jax.ShapeDtypeStruct((N_CHIPS,) + a.shape, a.dtype) for a in shards],
        in_specs=[qspec,
                  pl.BlockSpec((s, LANES), lambda p, i: (0, nkb + p)),
                  pl.BlockSpec((s, LANES), lambda p, i: (0, 2 * nkb + p))] + [hbm] * n_sh,
        out_specs=[qspec, qspec, pl.BlockSpec(memory_space=pltpu.SMEM)] + [hbm] * n_sh,
        scratch_shapes=[pltpu.VMEM((2, tq, LANES), F32), pltpu.VMEM((2, tq, LANES), F32)] + _exchange_sems(n_sh),
        compiler_params=_cparams())(proj, proj, proj, *shards)


def _swa_masks(n):
    ti = lax.broadcasted_iota(jnp.int32, (WINDOW, 2 * WINDOW), 0)
    kj = lax.broadcasted_iota(jnp.int32, (WINDOW, 2 * WINDOW), 1)
    dist = ti + WINDOW - kj
    valid = (dist >= 0) & (dist < WINDOW) & ((n * WINDOW - WINDOW + kj) >= 0)
    return valid, dist.astype(F32)


def _swa_probs(sc, valid, distf, h, sink):
    slope = 2.0 ** (-(h + 1))
    sc = jnp.where(valid, sc - slope * distf, MASK_VALUE)
    mx = jnp.maximum(jnp.max(sc, axis=1, keepdims=True), sink)
    p = jnp.exp(sc - mx)
    es = jnp.exp(sink - mx)
    inv = 1.0 / (jnp.sum(p, axis=1, keepdims=True) + es)
    return p * inv, es * inv


def _swa_forward(proj, sinks, name):
    s = proj.shape[0]
    nb = s // WINDOW
    qb, kb, vb = 3 * SB_W // SWA_QW, (3 * SB_W + SWA_QW) // LANES, (3 * SB_W + SWA_QW + SWA_KW) // LANES

    def body(q_ref, kp_ref, kc_ref, vp_ref, vc_ref, sink_ref, o_ref):
        n = pl.program_id(0)
        k = jnp.concatenate([kp_ref[...], kc_ref[...]], axis=0)
        v = jnp.concatenate([vp_ref[...], vc_ref[...]], axis=0)
        k_sw = pltpu.roll(k.astype(F32), HEAD_DIM, 1).astype(BF16)
        v_sw = pltpu.roll(v.astype(F32), HEAD_DIM, 1).astype(BF16)
        lane = lax.broadcasted_iota(jnp.int32, (1, LANES), 1)
        halves = [lane < HEAD_DIM, lane >= HEAD_DIM]
        valid, distf = _swa_masks(n)
        heads = range(2 * 4)
        qms = [jnp.where(halves[h % 2], q_ref[:, (h // 2) * LANES:(h // 2 + 1) * LANES], jnp.zeros((), BF16))
               for h in heads]
        kus = [k if h // 4 == h % 2 else k_sw for h in heads]
        vus = [v if h // 4 == h % 2 else v_sw for h in heads]
        scores = [_dot_nt(qms[h], kus[h]) for h in heads]
        ps = [_swa_probs(scores[h], valid, distf, h, sink_ref[h])[0].astype(BF16) for h in heads]
        outs = [_dot(ps[h], vus[h]) for h in heads]
        for pair in range(4):
            o_ref[:, pair * LANES:(pair + 1) * LANES] = jnp.where(halves[0], outs[2 * pair], outs[2 * pair + 1])

    prev = lambda n: jnp.maximum(n - 1, 0)
    return pl.pallas_call(
        body, name=name, grid=(nb,),
        out_shape=jax.ShapeDtypeStruct((s, SWA_QW), F32),
        in_specs=[pl.BlockSpec((WINDOW, SWA_QW), lambda n: (n, qb)),
                  pl.BlockSpec((WINDOW, LANES), lambda n: (prev(n), kb)),
                  pl.BlockSpec((WINDOW, LANES), lambda n: (n, kb)),
                  pl.BlockSpec((WINDOW, LANES), lambda n: (prev(n), vb)),
                  pl.BlockSpec((WINDOW, LANES), lambda n: (n, vb)),
                  pl.BlockSpec(memory_space=pltpu.SMEM)],
        out_specs=pl.BlockSpec((WINDOW, SWA_QW), lambda n: (n, 0)),
        compiler_params=_cparams())(proj, proj, proj, proj, proj, sinks)


def _rms_parts(y):
    return lax.rsqrt(jnp.mean(y * y, axis=1, keepdims=True) + RMS_EPS)


def _post_attention(y_sb, y_sw, x, vec, w_out, name):
    s = x.shape[0]
    tb = min(TOK_TILE, s)

    def body(ysb_ref, ysw_ref, x_ref, vec_ref, w_ref, mixedt_ref, attn_ref, x1_ref, h2_ref, h2t_ref):
        ysb, ysw = ysb_ref[...], ysw_ref[...]
        nsb_f = ysb * _rms_parts(ysb) * vec_ref[V_GN:V_GN + 1, :SB_W]
        nsw_f = ysw * _rms_parts(ysw) * vec_ref[V_GN:V_GN + 1, SB_W:]
        nsb, nsw = nsb_f.astype(BF16), nsw_f.astype(BF16)
        mixedt_ref[:SB_W, :] = nsb_f.T.astype(BF16)
        mixedt_ref[SB_W:, :] = nsw_f.T.astype(BF16)
        attn = _dot(nsb, w_ref[:SB_W, :]) + _dot(nsw, w_ref[SB_W:, :])
        attn_ref[...] = attn
        u1 = ALPHA * x_ref[...] + (1.0 + vec_ref[V_G_A:V_G_A + 1, :]) * attn
        xhat, _ = _layer_norm_stats(u1)
        x1 = xhat * vec_ref[V_LN1G:V_LN1G + 1, :] + vec_ref[V_LN1B:V_LN1B + 1, :]
        x1_ref[...] = x1
        h2 = x1 * (1.0 + vec_ref[V_SC_F:V_SC_F + 1, :]) + vec_ref[V_SH_F:V_SH_F + 1, :]
        h2_ref[...] = h2.astype(BF16)
        h2t_ref[...] = h2.T.astype(BF16)

    half = pl.BlockSpec((tb, SB_W), lambda i: (i, 0))
    full = pl.BlockSpec((tb, D), lambda i: (i, 0))
    full_t = pl.BlockSpec((D, tb), lambda i: (0, i))
    return pl.pallas_call(
        body, name=name, grid=(s // tb,),
        out_shape=[jax.ShapeDtypeStruct((D, s), BF16), jax.ShapeDtypeStruct((s, D), F32),
                   jax.ShapeDtypeStruct((s, D), F32), jax.ShapeDtypeStruct((s, D), BF16),
                   jax.ShapeDtypeStruct((D, s), BF16)],
        in_specs=[half, half, full, _resident((VEC_ROWS, D)), _resident((D, D))],
        out_specs=[full_t, full, full, full, full_t],
        compiler_params=_cparams())(y_sb, y_sw, x, vec, w_out)


def _ffn_forward(h2, w_gu, w_down, name):
    s = h2.shape[0]
    tb = min(FFN_TILE, s)

    def body(h_ref, wgu_ref, wd_ref, gu_ref, actt_ref, ffn_ref):
        gu = _dot(h_ref[...], wgu_ref[...])
        gu_ref[...] = gu.astype(BF16)
        gate, up = gu[:, :D_FF], gu[:, D_FF:]
        act = gate * (1.0 / (1.0 + jnp.exp(-gate))) * up
        actt_ref[...] = act.T.astype(BF16)
        ffn_ref[...] = _dot(act.astype(BF16), wd_ref[...])

    return pl.pallas_call(
        body, name=name, grid=(s // tb,),
        out_shape=[jax.ShapeDtypeStruct((s, 2 * D_FF), BF16), jax.ShapeDtypeStruct((D_FF, s), BF16),
                   jax.ShapeDtypeStruct((s, D), F32)],
        in_specs=[pl.BlockSpec((tb, D), lambda i: (i, 0)), _resident((D, 2 * D_FF)), _resident((D_FF, D))],
        out_specs=[pl.BlockSpec((tb, 2 * D_FF), lambda i: (i, 0)), pl.BlockSpec((D_FF, tb), lambda i: (0, i)),
                   pl.BlockSpec((tb, D), lambda i: (i, 0))],
        compiler_params=_cparams())(h2, w_gu, w_down)


def _layer_norm_bwd(dxhat, xhat, rstd):
    m1 = jnp.mean(dxhat, axis=1, keepdims=True)
    m2 = jnp.mean(dxhat * xhat, axis=1, keepdims=True)
    return rstd * (dxhat - m1 - xhat * m2)


def _colsum(a):
    return jnp.sum(a, axis=0, keepdims=True)


A_LN2G, A_LN2B, A_GF, A_SCF, A_SHF, A_LOSS = range(6)
B_LN1G, B_LN1B, B_GA, B_GN = range(4)
C_SCA, C_SHA = range(2)


def _ffn_backward(x1, ffn, target, gu, vec, w_gu, w_down, name):
    s = x1.shape[0]
    tb = min(FFN_BWD_TILE, s)

    def body(x1_ref, ffn_ref, t_ref, gu_ref, vec_ref, wgu_ref, wd_ref, dffn_ref, dgu_ref, dx1_ref, acc_ref):
        @pl.when(pl.program_id(0) == 0)
        def _():
            acc_ref[...] = jnp.zeros_like(acc_ref)

        x1v, ffn_v = x1_ref[...], ffn_ref[...]
        g_f = 1.0 + vec_ref[V_G_F:V_G_F + 1, :]
        u2 = ALPHA * x1v + g_f * ffn_v
        xhat, rstd = _layer_norm_stats(u2)
        ln_g = vec_ref[V_LN2G:V_LN2G + 1, :]
        err = xhat * ln_g + vec_ref[V_LN2B:V_LN2B + 1, :] - t_ref[...]
        dx2 = err * (1.0 / D)
        acc_ref[A_LOSS:A_LOSS + 1, :] += _colsum(err * err) * (0.5 / D)
        acc_ref[A_LN2G:A_LN2G + 1, :] += _colsum(dx2 * xhat)
        acc_ref[A_LN2B:A_LN2B + 1, :] += _colsum(dx2)
        du2 = _layer_norm_bwd(dx2 * ln_g, xhat, rstd)
        acc_ref[A_GF:A_GF + 1, :] += _colsum(du2 * ffn_v)
        dffn = (g_f * du2).astype(BF16)
        dffn_ref[...] = dffn
        dact = _dot_nt(dffn, wd_ref[...])
        gate, up = gu_ref[:, :D_FF].astype(F32), gu_ref[:, D_FF:].astype(F32)
        sg = 1.0 / (1.0 + jnp.exp(-gate))
        dgate = (dact * up * (sg * (1.0 + gate * (1.0 - sg)))).astype(BF16)
        dup = (dact * (gate * sg)).astype(BF16)
        dgu_ref[:, :D_FF] = dgate
        dgu_ref[:, D_FF:] = dup
        dh2 = _dot_nt(dgate, wgu_ref[:, :D_FF]) + _dot_nt(dup, wgu_ref[:, D_FF:])
        dx1_ref[...] = ALPHA * du2 + dh2 * (1.0 + vec_ref[V_SC_F:V_SC_F + 1, :])
        acc_ref[A_SCF:A_SCF + 1, :] += _colsum(dh2 * x1v)
        acc_ref[A_SHF:A_SHF + 1, :] += _colsum(dh2)

    full = pl.BlockSpec((tb, D), lambda i: (i, 0))
    wide = pl.BlockSpec((tb, 2 * D_FF), lambda i: (i, 0))
    return pl.pallas_call(
        body, name=name, grid=(s // tb,),
        out_shape=[jax.ShapeDtypeStruct((s, D), BF16), jax.ShapeDtypeStruct((s, 2 * D_FF), BF16),
                   jax.ShapeDtypeStruct((s, D), F32), jax.ShapeDtypeStruct((8, D), F32)],
        in_specs=[full, full, full, wide, _resident((VEC_ROWS, D)), _resident((D, 2 * D_FF)), _resident((D_FF, D))],
        out_specs=[full, wide, full, pl.BlockSpec((8, D), lambda i: (0, 0))],
        compiler_params=_cparams())(x1, ffn, target, gu, vec, w_gu, w_down)


def _attn_out_backward(dx1, x, attn, y_sb, y_sw, vec, w_out, name):
    s = x.shape[0]
    tb = min(TOK_TILE, s)

    def body(dx1_ref, x_ref, attn_ref, ysb_ref, ysw_ref, vec_ref, w_ref, du1_ref, dattn_ref, dy_ref, acc_ref):
        @pl.when(pl.program_id(0) == 0)
        def _():
            acc_ref[...] = jnp.zeros_like(acc_ref)

        attn = attn_ref[...]
        g_a = 1.0 + vec_ref[V_G_A:V_G_A + 1, :]
        xhat, rstd = _layer_norm_stats(ALPHA * x_ref[...] + g_a * attn)
        dx1v = dx1_ref[...]
        acc_ref[B_LN1G:B_LN1G + 1, :] += _colsum(dx1v * xhat)
        acc_ref[B_LN1B:B_LN1B + 1, :] += _colsum(dx1v)
        du1 = _layer_norm_bwd(dx1v * vec_ref[V_LN1G:V_LN1G + 1, :], xhat, rstd)
        du1_ref[...] = du1
        acc_ref[B_GA:B_GA + 1, :] += _colsum(du1 * attn)
        dattn = (g_a * du1).astype(BF16)
        dattn_ref[...] = dattn
        dmixed = _dot_nt(dattn, w_ref[...])
        for lo, y_ref in ((0, ysb_ref), (SB_W, ysw_ref)):
            y = y_ref[...]
            rr = _rms_parts(y)
            dn = dmixed[:, lo:lo + SB_W]
            acc_ref[B_GN:B_GN + 1, lo:lo + SB_W] += _colsum(dn * y * rr)
            dng = dn * vec_ref[V_GN:V_GN + 1, lo:lo + SB_W]
            dy_ref[:, lo:lo + SB_W] = rr * dng - y * (rr * rr * rr) * jnp.mean(dng * y, axis=1, keepdims=True)

    half = pl.BlockSpec((tb, SB_W), lambda i: (i, 0))
    full = pl.BlockSpec((tb, D), lambda i: (i, 0))
    return pl.pallas_call(
        body, name=name, grid=(s // tb,),
        out_shape=[jax.ShapeDtypeStruct((s, D), F32), jax.ShapeDtypeStruct((s, D), BF16),
                   jax.ShapeDtypeStruct((s, D), F32), jax.ShapeDtypeStruct((8, D), F32)],
        in_specs=[full, full, full, half, half, _resident((VEC_ROWS, D)), _resident((D, D))],
        out_specs=[full, full, full, pl.BlockSpec((8, D), lambda i: (0, 0))],
        compiler_params=_cparams())(dx1, x, attn, y_sb, y_sw, vec, w_out)


def _sb_backward(proj, sp_total, sweep_start, dy, slabs, name):
    s = proj.shape[0]
    tq, tk = min(SB_TQ, s), min(SB_TK, s)
    r = tq // tk
    nkb = SB_W // LANES
    nq = s // tq

    assert r % SB_UNROLL == 0, "the sweep below the diagonal takes whole steps"

    def body(q_ref, k_ref, v_ref, tot_ref, do_ref, start_ref, slab_ref, dq_ref, dk_ref, dv_ref, got_ref,
             dq_acc, left_refs, gsum_refs, send_sems, recv_sems, local_sems):
        i = pl.program_id(1)
        step = pl.program_id(0) * nq + i
        scatter = _scatter_exchange(slab_ref, got_ref, send_sems, recv_sems, local_sems)

        @pl.when(step == 0)
        def _():
            scatter.start()

        @pl.when(i == 0)
        def _():
            dk_ref[...] = jnp.zeros_like(dk_ref)
            dv_ref[...] = jnp.zeros_like(dv_ref)

        lane = lax.broadcasted_iota(jnp.int32, (1, LANES), 1)
        first = lane < HEAD_DIM
        qp, dop, totp = q_ref[...], do_ref[...], tot_ref[...]
        zero = jnp.zeros((), BF16)
        qs = (jnp.where(first, qp, zero), jnp.where(first, zero, qp))
        dofs = (jnp.where(first, dop, 0.0), jnp.where(first, 0.0, dop))
        dobs = tuple(d.astype(BF16) for d in dofs)
        dots = tuple(d.T.astype(BF16) for d in dofs)
        qts = tuple(qh.astype(F32).T.astype(BF16) for qh in qs)
        later = _sum_matrix(tk, lambda row, col: row > col)
        earlier = _sum_matrix(tk, lambda row, col: row < col)
        dq_acc[...] = jnp.zeros_like(dq_acc)
        gsum_refs[...] = jnp.zeros_like(gsum_refs)
        swapped = pltpu.roll(totp, HEAD_DIM, 1)
        left_refs[0] = jnp.where(first, totp, swapped)
        left_refs[1] = jnp.where(first, swapped, totp)

        def blocks(js, masked):
            ks = [pl.multiple_of(j * tk, tk) for j in js]
            kjs = [k_ref[pl.ds(k0, tk), :] for k0 in ks]
            vjs = [v_ref[pl.ds(k0, tk), :] for k0 in ks]
            chains = [(hd, b) for b in range(len(js)) for hd in range(2)]
            zs = [_dot_nt(qs[hd], kjs[b]) for hd, b in chains]
            dws = [_dot_nt(dobs[hd], vjs[b]) for hd, b in chains]
            parts = [_softplus_parts(z) for z in zs]
            sps = [p[0] for p in parts]
            if masked:
                t_idx = i * tq + lax.broadcasted_iota(jnp.int32, (tq, tk), 0)
                befores = [j * tk + lax.broadcasted_iota(jnp.int32, (tq, tk), 1) < t_idx for j in js]
                spms = [jnp.where(befores[b], sp, 0.0) for (hd, b), sp in zip(chains, sps)]
            else:
                spms = sps
            cums = [_block_sums(spm, later) for spm in spms]
            lefts = [left_refs[0], left_refs[1]]
            ws = []
            for (hd, b), z, sp, (cum, sm) in zip(chains, zs, sps, cums):
                lefts[hd] = lefts[hd] - sm
                w = jnp.exp(z - sp - cum - _across(lefts[hd], tk))
                if masked:
                    w = jnp.where(befores[b], w, 0.0)
                ws.append(w)
            wbs = [w.astype(BF16) for w in ws]
            dvs = [_dot(dots[hd], wb) for (hd, b), wb in zip(chains, wbs)]
            gs = [dw * w for dw, w in zip(dws, ws)]
            gcums = [_block_sums(g, earlier) for g in gs]
            gsums = [gsum_refs[0], gsum_refs[1]]
            dzbs = []
            for (hd, b), z, (sp, e1), g, (gcum, gsm) in zip(chains, zs, parts, gs, gcums):
                inv = 1.0 / (1.0 + e1)
                sig = jnp.where(z >= 0.0, inv, e1 * inv)
                dz = g - sig * (g + _across(gsums[hd], tk) + gcum)
                if masked:
                    dz = jnp.where(befores[b], dz, 0.0)
                dzbs.append(dz.astype(BF16))
                gsums[hd] = gsums[hd] + gsm
            dqs = [_dot(dzb, kjs[b]) for (hd, b), dzb in zip(chains, dzbs)]
            dks = [_dot(qts[hd], dzb) for (hd, b), dzb in zip(chains, dzbs)]
            for b, j in enumerate(js):
                dv_ref[j] += dvs[2 * b] + dvs[2 * b + 1]
                dk_ref[j] += dks[2 * b] + dks[2 * b + 1]
            for hd in range(2):
                tot = dqs[hd]
                for b in range(1, len(js)):
                    tot = tot + dqs[2 * b + hd]
                dq_acc[hd] += tot
                left_refs[hd] = lefts[hd]
                gsum_refs[hd] = gsums[hd]

        below = i * r
        start = jnp.clip(start_ref[pl.program_id(0), i].astype(jnp.int32), 0, below) // SB_UNROLL * SB_UNROLL

        def sweep(n, carry):
            blocks([start + SB_UNROLL * n + u for u in range(SB_UNROLL)], False)
            return carry

        lax.fori_loop(0, (below - start) // SB_UNROLL, sweep, 0)
        blocks([below + d for d in range(r)], True)
        dq_ref[...] = jnp.where(first, dq_acc[0], dq_acc[1])

        @pl.when(step == nkb * nq - 1)
        def _():
            scatter.wait()

    shp = jax.ShapeDtypeStruct((s, SB_W), F32)
    qspec = pl.BlockSpec((tq, LANES), lambda p, i: (i, p))
    whole = pl.BlockSpec((None, s // tk, LANES, tk), lambda p, i: (p, 0, 0, 0))
    shp_t = jax.ShapeDtypeStruct((nkb, s // tk, LANES, tk), F32)
    hbm = pl.BlockSpec(memory_space=pl.ANY)
    return pl.pallas_call(
        body, name=name, grid=(nkb, nq),
        out_shape=[shp, shp_t, shp_t, jax.ShapeDtypeStruct(slabs.shape, slabs.dtype)],
        in_specs=[qspec,
                  pl.BlockSpec((s, LANES), lambda p, i: (0, nkb + p)),
                  pl.BlockSpec((s, LANES), lambda p, i: (0, 2 * nkb + p)),
                  qspec, qspec, pl.BlockSpec(memory_space=pltpu.SMEM), hbm],
        out_specs=[qspec, whole, whole, hbm],
        scratch_shapes=[pltpu.VMEM((2, tq, LANES), F32), pltpu.VMEM((2, tq, LANES), F32), pltpu.VMEM((2, tq, LANES), F32)]
        + _exchange_sems(1),
        compiler_params=_cparams())(proj, proj, proj, sp_total, dy, sweep_start, slabs)


def _swa_backward(proj, y_sw, dy, sinks, give, name):
    s = proj.shape[0]
    nb = s // WINDOW
    qb, kb, vb = 3 * SB_W // SWA_QW, (3 * SB_W + SWA_QW) // LANES, (3 * SB_W + SWA_QW + SWA_KW) // LANES

    def body(q_ref, kp_ref, kc_ref, vp_ref, vc_ref, o_ref, do_ref, sink_ref, give_ref,
             dq_ref, dk_ref, dv_ref, ds_ref, got_ref, send_sem, recv_sem):
        n = pl.program_id(0)
        swap = _sibling_copy(give_ref, got_ref, send_sem, recv_sem)

        @pl.when(n == 0)
        def _():
            swap.start()

        @pl.when(n == 0)
        def _():
            dk_ref[...] = jnp.zeros_like(dk_ref)
            dv_ref[...] = jnp.zeros_like(dv_ref)
            ds_ref[...] = jnp.zeros_like(ds_ref)

        k = jnp.concatenate([kp_ref[...], kc_ref[...]], axis=0)
        v = jnp.concatenate([vp_ref[...], vc_ref[...]], axis=0)
        k_sw = pltpu.roll(k.astype(F32), HEAD_DIM, 1).astype(BF16)
        v_sw = pltpu.roll(v.astype(F32), HEAD_DIM, 1).astype(BF16)
        lane = lax.broadcasted_iota(jnp.int32, (1, LANES), 1)
        halves = [lane < HEAD_DIM, lane >= HEAD_DIM]
        valid, distf = _swa_masks(n)
        heads = range(2 * 4)
        cols = [slice((h // 2) * LANES, (h // 2 + 1) * LANES) for h in heads]
        qms = [jnp.where(halves[h % 2], q_ref[:, cols[h]], jnp.zeros((), BF16)) for h in heads]
        dos = [jnp.where(halves[h % 2], do_ref[:, cols[h]], 0.0) for h in heads]
        dobs = [d.astype(BF16) for d in dos]
        native = [h // 4 == h % 2 for h in heads]
        kus = [k if native[h] else k_sw for h in heads]
        vus = [v if native[h] else v_sw for h in heads]
        scores = [_dot_nt(qms[h], kus[h]) for h in heads]
        dps = [_dot_nt(dobs[h], vus[h]) for h in heads]
        deltas = [jnp.sum(dos[h] * o_ref[:, cols[h]], axis=1, keepdims=True) for h in heads]
        probs = [_swa_probs(scores[h], valid, distf, h, sink_ref[h]) for h in heads]
        pbs = [probs[h][0].astype(BF16) for h in heads]
        dscs = [(probs[h][0] * (dps[h] - deltas[h])).astype(BF16) for h in heads]
        dqs = [_dot(dscs[h], kus[h]) for h in heads]
        dks = [_dot_tn(dscs[h], qms[h]) for h in heads]
        dvs = [_dot_tn(pbs[h], dobs[h]) for h in heads]
        for h in heads:
            ds_ref[h:h + 1, :] += jnp.zeros((1, LANES), F32) - jnp.sum(probs[h][1] * deltas[h])
        for pair in range(4):
            dq_ref[:, cols[2 * pair]] = jnp.where(halves[0], dqs[2 * pair], dqs[2 * pair + 1])

        def gathered(parts):
            nat = sum(parts[h] for h in heads if native[h])
            rot = sum(parts[h] for h in heads if not native[h])
            return nat + pltpu.roll(rot, HEAD_DIM, 1)

        dk, dv = gathered(dks), gathered(dvs)
        prev = pl.multiple_of(jnp.maximum(n - 1, 0) * WINDOW, WINDOW)
        cur = pl.multiple_of(n * WINDOW, WINDOW)
        dk_ref[pl.ds(prev, WINDOW), :] += dk[:WINDOW]
        dv_ref[pl.ds(prev, WINDOW), :] += dv[:WINDOW]
        dk_ref[pl.ds(cur, WINDOW), :] += dk[WINDOW:]
        dv_ref[pl.ds(cur, WINDOW), :] += dv[WINDOW:]

        @pl.when(n == nb - 1)
        def _():
            swap.wait()

    prev_blk = lambda n: jnp.maximum(n - 1, 0)
    wide = pl.BlockSpec((WINDOW, SWA_QW), lambda n: (n, 0))
    whole = pl.BlockSpec((s, LANES), lambda n: (0, 0))
    hbm = pl.BlockSpec(memory_space=pl.ANY)
    return pl.pallas_call(
        body, name=name, grid=(nb,),
        out_shape=[jax.ShapeDtypeStruct((s, SWA_QW), F32), jax.ShapeDtypeStruct((s, LANES), F32),
                   jax.ShapeDtypeStruct((s, LANES), F32), jax.ShapeDtypeStruct((8, LANES), F32),
                   jax.ShapeDtypeStruct(give.shape, give.dtype)],
        in_specs=[pl.BlockSpec((WINDOW, SWA_QW), lambda n: (n, qb)),
                  pl.BlockSpec((WINDOW, LANES), lambda n: (prev_blk(n), kb)),
                  pl.BlockSpec((WINDOW, LANES), lambda n: (n, kb)),
                  pl.BlockSpec((WINDOW, LANES), lambda n: (prev_blk(n), vb)),
                  pl.BlockSpec((WINDOW, LANES), lambda n: (n, vb)),
                  wide,
                  pl.BlockSpec((WINDOW, SWA_QW), lambda n: (n, 1)),
                  pl.BlockSpec(memory_space=pltpu.SMEM), hbm],
        out_specs=[wide, whole, whole, pl.BlockSpec((8, LANES), lambda n: (0, 0)), hbm],
        scratch_shapes=[pltpu.SemaphoreType.DMA, pltpu.SemaphoreType.DMA],
        compiler_params=_cparams())(proj, proj, proj, proj, proj, y_sw, dy, sinks, give)


def _in_proj_backward(dq_sb, dkt_sb, dvt_sb, dq_sw, dk_sw, dv_sw, du1, x, vec, w_in, name):
    s = x.shape[0]
    tb = min(TOK_TILE, s)
    n_pairs, _, _, tk = dkt_sb.shape

    def body(dqsb_ref, dktsb_ref, dvtsb_ref, dqsw_ref, dksw_ref, dvsw_ref, du1_ref, x_ref, vec_ref, w_ref,
             dproj_ref, gx_ref, acc_ref, bacc_ref):
        @pl.when(pl.program_id(0) == 0)
        def _():
            acc_ref[...] = jnp.zeros_like(acc_ref)
            bacc_ref[...] = jnp.zeros_like(bacc_ref)

        pieces = ((0, dqsb_ref, QK_SCALE), (3 * SB_W, dqsw_ref, QK_SCALE), (3 * SB_W + SWA_QW, dksw_ref, 1.0),
                  (3 * SB_W + SWA_QW + SWA_KW, dvsw_ref, 1.0))
        for lo, ref, scale in pieces:
            width = ref.shape[1]
            piece = ref[...] * scale
            bacc_ref[0:1, lo:lo + width] += _colsum(piece)
            dproj_ref[:, lo:lo + width] = piece.astype(BF16)
        for base, ref in ((SB_W, dktsb_ref), (2 * SB_W, dvtsb_ref)):
            for p in range(n_pairs):
                lo = base + p * LANES
                for jj in range(tb // tk):
                    piece = ref[p, jj].T
                    bacc_ref[0:1, lo:lo + LANES] += _colsum(piece)
                    dproj_ref[jj * tk:(jj + 1) * tk, lo:lo + LANES] = piece.astype(BF16)
        dh = _dot_nt(dproj_ref[...], w_ref[...])
        xv = x_ref[...]
        gx_ref[...] = ALPHA * du1_ref[...] + dh * (1.0 + vec_ref[V_SC_A:V_SC_A + 1, :])
        acc_ref[C_SCA:C_SCA + 1, :] += _colsum(dh * xv)
        acc_ref[C_SHA:C_SHA + 1, :] += _colsum(dh)

    half = pl.BlockSpec((tb, SB_W), lambda i: (i, 0))
    narrow = pl.BlockSpec((tb, LANES), lambda i: (i, 0))
    full = pl.BlockSpec((tb, D), lambda i: (i, 0))
    blocks_t = pl.BlockSpec((n_pairs, tb // tk, LANES, tk), lambda i: (0, i, 0, 0))
    return pl.pallas_call(
        body, name=name, grid=(s // tb,),
        out_shape=[jax.ShapeDtypeStruct((s, D_IN), BF16), jax.ShapeDtypeStruct((s, D), F32),
                   jax.ShapeDtypeStruct((8, D), F32), jax.ShapeDtypeStruct((8, D_IN), F32)],
        in_specs=[half, blocks_t, blocks_t, half, narrow, narrow, full, full, _resident((VEC_ROWS, D)),
                  _resident((D, D_IN))],
        out_specs=[pl.BlockSpec((tb, D_IN), lambda i: (i, 0)), full, pl.BlockSpec((8, D), lambda i: (0, 0)),
                   pl.BlockSpec((8, D_IN), lambda i: (0, 0))],
        compiler_params=_cparams())(dq_sb, dkt_sb, dvt_sb, dq_sw, dk_sw, dv_sw, du1, x, vec, w_in)


def _weight_grad(at, b, name, col_shards=1):
    m, s = at.shape
    n = b.shape[1]
    if col_shards > 1:
        tn = n // col_shards
        out_shape = jax.ShapeDtypeStruct((col_shards, m, tn), F32)
        out_spec = pl.BlockSpec((None, m, tn), lambda j, k: (j, 0, 0))
    else:
        tn = 512 if n % 512 == 0 else n
        out_shape = jax.ShapeDtypeStruct((m, n), F32)
        out_spec = pl.BlockSpec((m, tn), lambda j, k: (0, j))
    ts = min(512, s)

    def body(at_ref, b_ref, o_ref):
        @pl.when(pl.program_id(1) == 0)
        def _():
            o_ref[...] = jnp.zeros_like(o_ref)

        o_ref[...] += _dot(at_ref[...], b_ref[...])

    return pl.pallas_call(
        body, name=name, grid=(n // tn, s // ts),
        out_shape=out_shape,
        in_specs=[pl.BlockSpec((m, ts), lambda j, k: (0, k)), pl.BlockSpec((ts, tn), lambda j, k: (k, j))],
        out_specs=out_spec,
        compiler_params=_cparams())(at, b)


def _pad_rows(v, rows):
    return jnp.concatenate([v, jnp.zeros((rows - v.shape[0], v.shape[1]), v.dtype)], axis=0)


def _col_shards(w, n_shards):
    r, n = w.shape
    return w.reshape(r, n_shards, n // n_shards).transpose(1, 0, 2)


def kernel(x, c, w_ada, b_ada, w_in, b_in, sinks, gn_sb, gn_swa, w_out, ln1_g, ln1_b, w_gu, w_down, ln2_g, ln2_b, loss_target, m_w_ada, m_b_ada, m_w_in, m_b_in, m_sinks, m_gn_sb, m_gn_swa, m_w_out, m_ln1_g, m_ln1_b, m_w_gu, m_w_down, m_ln2_g, m_ln2_b, v_w_ada, v_b_ada, v_w_in, v_b_in, v_sinks, v_gn_sb, v_gn_swa, v_w_out, v_ln1_g, v_ln1_b, v_w_gu, v_w_down, v_ln2_g, v_ln2_b):
    ix, iy, ic = lax.axis_index("x"), lax.axis_index("y"), lax.axis_index("c")
    chip = 2 * ix + iy
    dev = 4 * ix + 2 * iy + ic
    xs, target = x[0], loss_target[0]
    s = xs.shape[0]

    c_all = _allgather8(_pad_rows(c, 8), "gather_c")[::8]
    n_ada = w_ada.shape[2]
    b_ada_shard = lax.dynamic_slice_in_dim(b_ada, chip * n_ada, n_ada, axis=1)
    mod_cols, silu_c = _mod_shard(c_all, w_ada[0], b_ada_shard, "mod_shard")
    mod_all = _allgather8(mod_cols, "gather_mod").reshape(N_DEV, 8, n_ada)
    mod_mine = lax.dynamic_index_in_dim(mod_all, dev, axis=1, keepdims=False)
    mod = mod_mine.reshape(N_CHIPS, 2, n_ada)[:, 0].reshape(6, D)
    vec = jnp.concatenate([mod, ln1_g, ln1_b, ln2_g, ln2_b, jnp.concatenate([gn_sb, gn_swa], axis=1),
                           jnp.zeros((VEC_ROWS - 11, D), F32)], axis=0)

    (g_in,) = _chip_allgather([w_in[0].astype(BF16)], "gather_w_in")
    w_in_b = g_in.transpose(1, 0, 2).reshape(D, D_IN)

    h_t, proj = _in_proj(xs, vec, w_in_b, b_in, "in_proj")
    y_sb, sp_total, sweep_start, g_out, g_gu, g_down = _sb_forward(
        proj, [w_out[0].astype(BF16), w_gu[0].astype(BF16), w_down[0].astype(BF16)], "sb_forward")
    w_gu_b = g_gu.transpose(1, 0, 2).reshape(D, 2 * D_FF)
    w_out_b = g_out.reshape(D, D)
    w_down_b = g_down.reshape(D_FF, D)
    sink_vec = sinks[0]
    y_sw = _swa_forward(proj, sink_vec, "swa_forward")
    mixed_t, attn, x1, h2_b, h2_t = _post_attention(y_sb, y_sw, xs, vec, w_out_b, "post_attention")
    gu, act_t, ffn = _ffn_forward(h2_b, w_gu_b, w_down_b, "ffn_forward")

    def halves_of(per_shard):
        flat = jnp.concatenate(per_shard, axis=1)
        halves = flat.reshape(4, 2, flat.shape[1] // (2 * LANES), LANES)
        return (lax.dynamic_index_in_dim(halves, ic, axis=1, keepdims=False),
                lax.dynamic_index_in_dim(halves, 1 - ic, axis=1, keepdims=False))

    def chip_sum(keep, got, tag):
        rows = keep.shape[1]
        return _add2(keep.reshape(4 * rows, LANES), got.reshape(4 * rows, LANES), tag).reshape(4, rows, LANES)

    dffn_b, dgu_b, dx1, acc_f = _ffn_backward(x1, ffn, target, gu, vec, w_gu_b, w_down_b, "ffn_backward")
    dw_gu = _weight_grad(h2_t, dgu_b, "grad_w_gu", col_shards=4)
    dw_down = _weight_grad(act_t, dffn_b, "grad_w_down")
    du1, dattn_b, dy, acc_a = _attn_out_backward(dx1, xs, attn, y_sb, y_sw, vec, w_out_b, "attn_out_backward")
    dw_out = _weight_grad(mixed_t, dattn_b, "grad_w_out")
    keep_f, give_f = halves_of([dw_gu.reshape(4, -1), dw_down.reshape(4, -1), dw_out.reshape(4, -1)])
    dq_sw, dk_sw, dv_sw, dsink, got_f = _swa_backward(proj, y_sw, dy, sink_vec, give_f, "swa_backward")
    sums_f = chip_sum(keep_f, got_f, "grad_chip_sum_ffn")
    dq_sb, dk_sb, dv_sb, parts_f = _sb_backward(proj, sp_total, sweep_start, dy, sums_f, "sb_backward")
    dproj_b, grad_x, acc_i, acc_b = _in_proj_backward(dq_sb, dk_sb, dv_sb, dq_sw, dk_sw, dv_sw, du1, xs, vec, w_in_b,
                                                      "in_proj_backward")
    dw_in = _weight_grad(h_t, dproj_b, "grad_w_in")
    keep_a, give_a = halves_of([_col_shards(dw_in, 4).reshape(4, -1)])
    sums_a = chip_sum(keep_a, _sibling_send(give_a, "grad_halves_swap_attn"), "grad_chip_sum_attn")
    parts_a = _chip_scatter(sums_a, "grad_chip_scatter_attn")
    my_half = jnp.concatenate([_sum4(parts_f, "grad_reduce_ffn"), _sum4(parts_a, "grad_reduce_attn")], axis=0)
    other_half = _sibling_send(my_half, "grad_half_return")
    rows_f = parts_f.shape[1]

    def whole_shard(lo, hi):
        mine, other = my_half[lo:hi], other_half[lo:hi]
        return jnp.concatenate([jnp.where(ic == 0, mine, other), jnp.where(ic == 0, other, mine)], axis=0).reshape(-1)

    flat_f, flat_a = whole_shard(0, rows_f), whole_shard(rows_f, my_half.shape[0])
    n_gu, n_down = D * (2 * D_FF // 4), (D_FF // 4) * D
    gw_gu = flat_f[:n_gu].reshape(D, 2 * D_FF // 4)
    gw_down = flat_f[n_gu:n_gu + n_down].reshape(D_FF // 4, D)
    gw_out = flat_f[n_gu + n_down:].reshape(D // 4, D)
    gw_in = flat_a.reshape(D, D_IN // 4)

    dmod = jnp.concatenate([acc_i[C_SHA:C_SHA + 1], acc_i[C_SCA:C_SCA + 1], acc_a[B_GA:B_GA + 1],
                            acc_f[A_SHF:A_SHF + 1], acc_f[A_SCF:A_SCF + 1], acc_f[A_GF:A_GF + 1]], axis=1)
    dsink_row = jnp.concatenate([dsink[:, 0].reshape(1, 8), jnp.zeros((1, LANES - 8), F32)], axis=1)
    loss_row = jnp.concatenate([jnp.sum(acc_f[A_LOSS:A_LOSS + 1], axis=1, keepdims=True),
                                jnp.zeros((1, LANES - 1), F32)], axis=1)
    small = jnp.concatenate([dmod, acc_b[0:1], acc_a[B_LN1G:B_LN1G + 1], acc_a[B_LN1B:B_LN1B + 1],
                             acc_f[A_LN2G:A_LN2G + 1], acc_f[A_LN2B:A_LN2B + 1], acc_a[B_GN:B_GN + 1],
                             dsink_row, loss_row], axis=1)
    small_all = _allgather8(_pad_rows(small, 8), "gather_small")[::8]

    def pack_small(b_ada_, b_in_, ln1g_, ln1b_, ln2g_, ln2b_, gsb_, gsw_, sinks_):
        return jnp.concatenate([b_ada_, b_in_, ln1g_, ln1b_, ln2g_, ln2b_, gsb_, gsw_, sinks_,
                                jnp.ones((1, 2 * LANES - 8), F32)], axis=1)

    w_small = pack_small(b_ada, b_in, ln1_g, ln1_b, ln2_g, ln2_b, gn_sb, gn_swa, sinks)
    m_small = pack_small(m_b_ada, m_b_in, m_ln1_g, m_ln1_b, m_ln2_g, m_ln2_b, m_gn_sb, m_gn_swa, m_sinks)
    v_small = pack_small(v_b_ada, v_b_in, v_ln1_g, v_ln1_b, v_ln2_g, v_ln2_b, v_gn_sb, v_gn_swa, v_sinks)
    small_out = _small_update(small_all, w_small, m_small, v_small, "small_update")

    def unpack_small(row):
        return {"b_ada": row[:, SM_MOD:SM_BIN], "b_in": row[:, SM_BIN:SM_LN1G], "ln1_g": row[:, SM_LN1G:SM_LN1B],
                "ln1_b": row[:, SM_LN1B:SM_LN2G], "ln2_g": row[:, SM_LN2G:SM_LN2B], "ln2_b": row[:, SM_LN2B:SM_GN],
                "gn_sb": row[:, SM_GN:SM_GN + SB_W], "gn_swa": row[:, SM_GN + SB_W:SM_SINK],
                "sinks": row[:, SM_SINK:SM_SINK + 8]}

    g_small, d_small, m2_small, v2_small = [unpack_small(r) for r in small_out]
    loss = small_out[0][0, SM_LOSS]

    dmod_cols = lax.dynamic_slice_in_dim(small_all[:, SM_MOD:SM_BIN], chip * n_ada, n_ada, axis=1)
    gw_ada = _weight_grad(_pad_rows(silu_c, LANES).astype(BF16).T, _pad_rows(dmod_cols, LANES).astype(BF16), "grad_w_ada")

    big = {}
    for nm, w, g, m, v in (("w_ada", w_ada, gw_ada, m_w_ada, v_w_ada), ("w_in", w_in, gw_in, m_w_in, v_w_in),
                           ("w_out", w_out, gw_out, m_w_out, v_w_out), ("w_gu", w_gu, gw_gu, m_w_gu, v_w_gu),
                           ("w_down", w_down, gw_down, m_w_down, v_w_down)):
        d_, m2_, v2_ = _adamw(w[0], g, m[0], v[0], "adamw_" + nm)
        big[nm] = (g[None], d_[None], m2_[None], v2_[None])

    order = ["w_ada", "b_ada", "w_in", "b_in", "sinks", "gn_sb", "gn_swa", "w_out", "ln1_g", "ln1_b", "w_gu", "w_down",
             "ln2_g", "ln2_b"]

    def leaf(nm, which):
        if nm in big:
            return big[nm][which]
        return (g_small, d_small, m2_small, v2_small)[which][nm]

    outs = [loss, grad_x[None]]
    for which in range(4):
        outs += [leaf(nm, which) for nm in order]
    return tuple(outs)
```

```python
import functools
import math

import jax
import jax.numpy as jnp
from jax import lax
from jax.experimental import pallas as pl
from jax.experimental.pallas import tpu as pltpu

F32 = jnp.float32
BF16 = jnp.bfloat16

D = 1024
HEAD_DIM = 64
SB_W = 512
SWA_QW = 512
SWA_KW = 128
D_IN = 2304
D_FF = 2816
WINDOW = 128
ALPHA = 2.0 ** 0.25
LN_EPS = 1e-5
RMS_EPS = 1e-6
MASK_VALUE = -1e30
QK_SCALE = 1.0 / math.sqrt(HEAD_DIM)

ADAM_LR = 0.001
ADAM_B1 = 0.9
ADAM_B2 = 0.999
ADAM_EPS = 1e-08
ADAM_WD = 0.01
ADAM_STEP = 10

N_CHIPS = 4
N_DEV = 8
LANES = 128

SB_TQ = 512
SB_TK = 256
SB_UNROLL = 1
SB_DEAD_MASS = 110.0
TOK_TILE = 512
FFN_TILE = 256
FFN_BWD_TILE = 256
VMEM_LIMIT = 56 * 1024 * 1024

V_SH_A, V_SC_A, V_G_A, V_SH_F, V_SC_F, V_G_F, V_LN1G, V_LN1B, V_LN2G, V_LN2B, V_GN = range(11)
VEC_ROWS = 16

SM_MOD = 0
SM_BIN = 6 * D
SM_LN1G = SM_BIN + D_IN
SM_LN1B = SM_LN1G + D
SM_LN2G = SM_LN1B + D
SM_LN2B = SM_LN2G + D
SM_GN = SM_LN2B + D
SM_SINK = SM_GN + D
SM_LOSS = SM_SINK + LANES
SM_LEN = SM_LOSS + LANES

MESH = pl.DeviceIdType.MESH


def _cparams(**kw):
    return pltpu.CompilerParams(vmem_limit_bytes=VMEM_LIMIT, **kw)


def _resident(shape):
    nd = len(shape)
    return pl.BlockSpec(shape, lambda *_: (0,) * nd, pipeline_mode=pl.Buffered(1))


def _dot(a, b):
    return jnp.dot(a, b, preferred_element_type=F32)


def _dot_nt(a, b):
    return lax.dot_general(a, b, (((1,), (1,)), ((), ())), preferred_element_type=F32)


def _dot_tn(a, b):
    return lax.dot_general(a, b, (((0,), (0,)), ((), ())), preferred_element_type=F32)


def _sum_matrix(tk, keep):
    row = lax.broadcasted_iota(jnp.int32, (tk, tk + LANES), 0)
    col = lax.broadcasted_iota(jnp.int32, (tk, tk + LANES), 1)
    return (keep(row, col) | (col >= tk)).astype(BF16)


def _block_sums(x, m):
    tk = x.shape[1]
    res = _dot(x.astype(BF16), m)
    return res[:, :tk], res[:, tk:]


def _across(v, tk):
    return jnp.concatenate([v] * (tk // LANES), axis=1)


def _allgather8(v, name):
    m_per, n = v.shape

    def body(x_ref, out_ref, send_sems, recv_sems, local_sem):
        x, y, c = lax.axis_index("x"), lax.axis_index("y"), lax.axis_index("c")
        me, sibling = (x, y, c), (x, y, 1 - c)
        chips = [(1 - x, y), (x, 1 - y), (1 - x, 1 - y)]

        def rows(px, py, pc):
            return out_ref.at[pl.ds((4 * px + 2 * py + pc) * m_per, m_per), :]

        def copy(k, block, to, src=None):
            return pltpu.make_async_remote_copy(
                src_ref=rows(*block) if src is None else src, dst_ref=rows(*block),
                send_sem=send_sems.at[k], recv_sem=recv_sems.at[k], device_id=to, device_id_type=MESH)

        mine = pltpu.make_async_copy(x_ref, rows(*me), local_sem)
        mine.start()
        first = [copy(0, me, sibling, src=x_ref)]
        first += [copy(1 + j, me, (*chip, c), src=x_ref) for j, chip in enumerate(chips)]
        for cp in first:
            cp.start()
        passed = [copy(4 + j, (*chip, c), sibling) for j, chip in enumerate(chips)]
        for j, chip in enumerate(chips):
            copy(1 + j, (*chip, c), me).wait_recv()
            passed[j].start()
        copy(0, sibling, me).wait_recv()
        for j, chip in enumerate(chips):
            copy(4 + j, (*chip, 1 - c), me).wait_recv()
        for cp in first + passed:
            cp.wait_send()
        mine.wait()

    return pl.pallas_call(
        body, name=name,
        out_shape=jax.ShapeDtypeStruct((N_DEV * m_per, n), v.dtype),
        in_specs=[pl.BlockSpec(memory_space=pltpu.VMEM)],
        out_specs=pl.BlockSpec(memory_space=pltpu.VMEM),
        scratch_shapes=[pltpu.SemaphoreType.DMA((7,)), pltpu.SemaphoreType.DMA((7,)), pltpu.SemaphoreType.DMA],
        compiler_params=_cparams(),
    )(v)


class _Exchange:
    def __init__(self, local, sends, arrivals):
        self.local, self.sends, self.arrivals = local, sends, arrivals

    def start(self):
        for cp in self.local + self.sends:
            cp.start()

    def wait(self):
        for cp in self.arrivals:
            cp.wait_recv()
        for cp in self.sends:
            cp.wait_send()
        for cp in self.local:
            cp.wait()


def _exchange_sems(n):
    return [pltpu.SemaphoreType.DMA((3 * n,)), pltpu.SemaphoreType.DMA((3 * n,)), pltpu.SemaphoreType.DMA((n,))]


def _gather_exchange(ins, outs, send_sems, recv_sems, local_sems):
    x, y, c = lax.axis_index("x"), lax.axis_index("y"), lax.axis_index("c")
    slot = 2 * x + y
    chips = [(1 - x, y), (x, 1 - y), (1 - x, 1 - y)]
    local, sends, arrivals = [], [], []
    for a in range(len(ins)):
        local.append(pltpu.make_async_copy(ins[a], outs[a].at[slot], local_sems.at[a]))
        for j, (px, py) in enumerate(chips):
            sems = dict(send_sem=send_sems.at[3 * a + j], recv_sem=recv_sems.at[3 * a + j],
                        device_id=(px, py, c), device_id_type=MESH)
            sends.append(pltpu.make_async_remote_copy(src_ref=ins[a], dst_ref=outs[a].at[slot], **sems))
            arrivals.append(pltpu.make_async_remote_copy(src_ref=ins[a], dst_ref=outs[a].at[2 * px + py], **sems))
    return _Exchange(local, sends, arrivals)


def _scatter_exchange(p_refs, out_refs, send_sems, recv_sems, local_sems):
    x, y, c = lax.axis_index("x"), lax.axis_index("y"), lax.axis_index("c")
    slot = 2 * x + y
    chips = [(1 - x, y), (x, 1 - y), (1 - x, 1 - y)]
    local, sends, arrivals = [], [], []
    for a, (p_ref, out_ref) in enumerate(zip(p_refs, out_refs)):
        local.append(pltpu.make_async_copy(p_ref.at[slot], out_ref.at[slot], local_sems.at[a]))
        for j, (px, py) in enumerate(chips):
            sems = dict(send_sem=send_sems.at[3 * a + j], recv_sem=recv_sems.at[3 * a + j],
                        device_id=(px, py, c), device_id_type=MESH)
            sends.append(pltpu.make_async_remote_copy(src_ref=p_ref.at[2 * px + py], dst_ref=out_ref.at[slot], **sems))
            arrivals.append(pltpu.make_async_remote_copy(src_ref=p_ref.at[slot], dst_ref=out_ref.at[2 * px + py], **sems))
    return _Exchange(local, sends, arrivals)


def _chip_allgather(arrs, name):
    n = len(arrs)

    def body(*refs):
        ex = _gather_exchange(refs[:n], refs[n:2 * n], *refs[2 * n:])
        ex.start()
        ex.wait()

    hbm = pl.BlockSpec(memory_space=pl.ANY)
    return pl.pallas_call(
        body, name=name,
        out_shape=[jax.ShapeDtypeStruct((N_CHIPS,) + a.shape, a.dtype) for a in arrs],
        in_specs=[hbm] * n, out_specs=[hbm] * n,
        scratch_shapes=_exchange_sems(n),
        compiler_params=_cparams(),
    )(*arrs)


def _sibling_halves(give_refs, got_refs, send_sems, recv_sems):
    x, y, c = lax.axis_index("x"), lax.axis_index("y"), lax.axis_index("c")
    copies = []
    for a, (give_ref, got_ref) in enumerate(zip(give_refs, got_refs)):
        for s in range(N_CHIPS):
            copies.append(pltpu.make_async_remote_copy(
                src_ref=give_ref.at[s, 1 - c], dst_ref=got_ref.at[s], send_sem=send_sems.at[N_CHIPS * a + s],
                recv_sem=recv_sems.at[N_CHIPS * a + s], device_id=(x, y, 1 - c), device_id_type=MESH))
    return copies


def _halves_shapes(arrs):
    return [jax.ShapeDtypeStruct((a.shape[0],) + a.shape[2:], a.dtype) for a in arrs]


def _halves_sems(n):
    return [pltpu.SemaphoreType.DMA((N_CHIPS * n,)), pltpu.SemaphoreType.DMA((N_CHIPS * n,))]


def _halves_swap(arrs, name):
    n = len(arrs)

    def body(*refs):
        copies = _sibling_halves(refs[:n], refs[n:2 * n], *refs[2 * n:])
        for cp in copies:
            cp.start()
        for cp in copies:
            cp.wait()

    hbm = pl.BlockSpec(memory_space=pl.ANY)
    return pl.pallas_call(body, name=name, out_shape=_halves_shapes(arrs), in_specs=[hbm] * n, out_specs=[hbm] * n,
                          scratch_shapes=_halves_sems(n), compiler_params=_cparams())(*arrs)


def _sibling_send(arrs, name):
    n = len(arrs)

    def body(*refs):
        x, y, c = lax.axis_index("x"), lax.axis_index("y"), lax.axis_index("c")
        send_sems, recv_sems = refs[2 * n:]
        copies = [pltpu.make_async_remote_copy(src_ref=refs[a], dst_ref=refs[n + a], send_sem=send_sems.at[a],
                                               recv_sem=recv_sems.at[a], device_id=(x, y, 1 - c), device_id_type=MESH)
                  for a in range(n)]
        for cp in copies:
            cp.start()
        for cp in copies:
            cp.wait()

    hbm = pl.BlockSpec(memory_space=pl.ANY)
    return pl.pallas_call(
        body, name=name, out_shape=[jax.ShapeDtypeStruct(a.shape, a.dtype) for a in arrs],
        in_specs=[hbm] * n, out_specs=[hbm] * n,
        scratch_shapes=[pltpu.SemaphoreType.DMA((n,)), pltpu.SemaphoreType.DMA((n,))],
        compiler_params=_cparams(),
    )(*arrs)


def _chip_scatter(ps, name):
    n = len(ps)

    def body(*refs):
        ex = _scatter_exchange(refs[:n], refs[n:2 * n], *refs[2 * n:])
        ex.start()
        ex.wait()

    hbm = pl.BlockSpec(memory_space=pl.ANY)
    return pl.pallas_call(
        body, name=name, out_shape=[jax.ShapeDtypeStruct(p.shape, p.dtype) for p in ps],
        in_specs=[hbm] * n, out_specs=[hbm] * n,
        scratch_shapes=_exchange_sems(n),
        compiler_params=_cparams(),
    )(*ps)


def _row_tile(h):
    return h // 2 if (h // 2) % 8 == 0 else h


def _chip_sum(arr, got, core, name):
    _, _, h, cols = arr.shape
    tr = _row_tile(h)

    def body(core_ref, a_ref, b_ref, o_ref):
        o_ref[...] = a_ref[...] + b_ref[...]

    slab = pl.BlockSpec((None, tr, cols), lambda s, i, core_ref: (s, i, 0))
    grid_spec = pltpu.PrefetchScalarGridSpec(
        num_scalar_prefetch=1, grid=(N_CHIPS, h // tr),
        in_specs=[pl.BlockSpec((None, None, tr, cols), lambda s, i, core_ref: (s, core_ref[0], i, 0)), slab],
        out_specs=slab)
    return pl.pallas_call(body, name=name, grid_spec=grid_spec, out_shape=jax.ShapeDtypeStruct(got.shape, got.dtype),
                          compiler_params=_cparams())(core, arr, got)


def _sum4(p, name):
    _, h, cols = p.shape
    tr = _row_tile(h)

    def body(p_ref, o_ref):
        o_ref[...] = ((p_ref[0] + p_ref[1]) + p_ref[2]) + p_ref[3]

    return pl.pallas_call(
        body, name=name, grid=(h // tr,), out_shape=jax.ShapeDtypeStruct((h, cols), p.dtype),
        in_specs=[pl.BlockSpec((4, tr, cols), lambda i: (0, i, 0))],
        out_specs=pl.BlockSpec((tr, cols), lambda i: (i, 0)), compiler_params=_cparams())(p)


def _adam_math(w, g, m, v):
    m2 = ADAM_B1 * m + (1.0 - ADAM_B1) * g
    v2 = ADAM_B2 * v + (1.0 - ADAM_B2) * (g * g)
    m_hat = m2 / (1.0 - ADAM_B1 ** ADAM_STEP)
    v_hat = v2 / (1.0 - ADAM_B2 ** ADAM_STEP)
    delta = -ADAM_LR * (m_hat / (jnp.sqrt(v_hat) + ADAM_EPS) + ADAM_WD * w)
    return delta, m2, v2


def _adamw(w, g, m, v, name):
    rows, cols = w.shape
    tr = rows // 4 if rows % 32 == 0 else rows

    def body(w_ref, g_ref, m_ref, v_ref, d_ref, m2_ref, v2_ref):
        delta, m2, v2 = _adam_math(w_ref[...], g_ref[...], m_ref[...], v_ref[...])
        d_ref[...] = delta
        m2_ref[...] = m2
        v2_ref[...] = v2

    spec = pl.BlockSpec((tr, cols), lambda i: (i, 0))
    shp = jax.ShapeDtypeStruct(w.shape, F32)
    return pl.pallas_call(body, name=name, grid=(rows // tr,), out_shape=[shp, shp, shp],
                          in_specs=[spec] * 4, out_specs=[spec] * 3, compiler_params=_cparams())(w, g, m, v)


def _small_update(g8, offsets, ws, ms, vs, loss_at, name):
    k = len(ws)

    def summed(g8_ref, lo, width):
        g = g8_ref[0:1, lo:lo + width]
        for r in range(1, N_DEV):
            g = g + g8_ref[r:r + 1, lo:lo + width]
        return g

    def body(g8_ref, *refs):
        ins, outs = refs[:3 * k], refs[3 * k:]
        for j in range(k):
            g = summed(g8_ref, offsets[j], ws[j].shape[1])
            delta, m2, v2 = _adam_math(ins[j][...], g, ins[k + j][...], ins[2 * k + j][...])
            for kind, val in enumerate((g, delta, m2, v2)):
                outs[kind * k + j][...] = val
        outs[4 * k][...] = summed(g8_ref, loss_at, LANES)

    vm = pl.BlockSpec(memory_space=pltpu.VMEM)
    shapes = [jax.ShapeDtypeStruct(w.shape, F32) for w in ws] * 4 + [jax.ShapeDtypeStruct((1, LANES), F32)]
    res = pl.pallas_call(body, name=name, out_shape=shapes, in_specs=[vm] * (1 + 3 * k), out_specs=[vm] * (4 * k + 1),
                         compiler_params=_cparams())(g8, *ws, *ms, *vs)
    return res[:k], res[k:2 * k], res[2 * k:3 * k], res[3 * k:4 * k], res[4 * k]


def _mod_shard(c8, w_ada, b_ada_shard, name):
    n = w_ada.shape[1]
    tn = 512

    def body(c_ref, w_ref, b_ref, o_ref, s_ref):
        cv = c_ref[...]
        sc = cv * (1.0 / (1.0 + jnp.exp(-cv)))
        s_ref[...] = sc
        o_ref[...] = _dot(sc.astype(BF16), w_ref[...].astype(BF16)) + b_ref[...]

    return pl.pallas_call(
        body, name=name, grid=(n // tn,),
        out_shape=[jax.ShapeDtypeStruct((8, n), F32), jax.ShapeDtypeStruct((8, D), F32)],
        in_specs=[pl.BlockSpec((8, D), lambda j: (0, 0)), pl.BlockSpec((D, tn), lambda j: (0, j)),
                  pl.BlockSpec((1, tn), lambda j: (0, j))],
        out_specs=[pl.BlockSpec((8, tn), lambda j: (0, j)), pl.BlockSpec((8, D), lambda j: (0, 0))],
        compiler_params=_cparams())(c8, w_ada, b_ada_shard)


def _layer_norm_stats(u):
    mu = jnp.mean(u, axis=1, keepdims=True)
    d = u - mu
    var = jnp.mean(d * d, axis=1, keepdims=True)
    rstd = lax.rsqrt(var + LN_EPS)
    return d * rstd, rstd


def _in_proj(x, vec, w_in, b_in, name):
    s = x.shape[0]
    tb = min(TOK_TILE, s)

    def body(x_ref, vec_ref, w_ref, b_ref, ht_ref, p_ref):
        h = x_ref[...] * (1.0 + vec_ref[V_SC_A:V_SC_A + 1, :]) + vec_ref[V_SH_A:V_SH_A + 1, :]
        hb = h.astype(BF16)
        ht_ref[...] = h.T.astype(BF16)
        proj = _dot(hb, w_ref[...]) + b_ref[...]
        col = lax.broadcasted_iota(jnp.int32, (1, D_IN), 1)
        is_q = (col < SB_W) | ((col >= 3 * SB_W) & (col < 3 * SB_W + SWA_QW))
        p_ref[...] = (proj * jnp.where(is_q, QK_SCALE, 1.0)).astype(BF16)

    return pl.pallas_call(
        body, name=name, grid=(s // tb,),
        out_shape=[jax.ShapeDtypeStruct((D, s), BF16), jax.ShapeDtypeStruct((s, D_IN), BF16)],
        in_specs=[pl.BlockSpec((tb, D), lambda i: (i, 0)), _resident((VEC_ROWS, D)), _resident((D, D_IN)),
                  _resident((1, D_IN))],
        out_specs=[pl.BlockSpec((D, tb), lambda i: (0, i)), pl.BlockSpec((tb, D_IN), lambda i: (i, 0))],
        compiler_params=_cparams())(x, vec, w_in, b_in)


def _softplus_parts(z):
    e1 = jnp.exp(-jnp.abs(z))
    sp = jnp.maximum(z, 0.0) + jnp.log(1.0 + e1)
    return sp, e1


def _sb_forward(proj, shards, name):
    s = proj.shape[0]
    tq, tk = min(SB_TQ, s), min(SB_TK, s)
    r = tq // tk

    n_sh = len(shards)
    nkb = SB_W // LANES
    nq = s // tq

    assert r % SB_UNROLL == 0, "the sweep below the diagonal takes whole steps"

    def body(q_ref, k_ref, v_ref, *refs):
        sh_refs, (o_ref, tot_ref, start_ref), got_refs = refs[:n_sh], refs[n_sh:n_sh + 3], refs[n_sh + 3:2 * n_sh + 3]
        acc_refs, run_refs = refs[2 * n_sh + 3:2 * n_sh + 5]
        i = pl.program_id(1)
        step = pl.program_id(0) * nq + i
        gather = _gather_exchange(sh_refs, got_refs, *refs[2 * n_sh + 5:])

        @pl.when(step == 0)
        def _():
            gather.start()

        lane = lax.broadcasted_iota(jnp.int32, (1, LANES), 1)
        first = lane < HEAD_DIM
        qp = q_ref[...]
        zero = jnp.zeros((), BF16)
        qs = (jnp.where(first, qp, zero), jnp.where(first, zero, qp))
        later = _sum_matrix(tk, lambda row, col: row > col)
        acc_refs[...] = jnp.zeros_like(acc_refs)
        run_refs[...] = jnp.zeros_like(run_refs)

        def blocks(js, masked):
            ks = [pl.multiple_of(j * tk, tk) for j in js]
            kjs = [k_ref[pl.ds(k0, tk), :] for k0 in ks]
            vjs = [v_ref[pl.ds(k0, tk), :] for k0 in ks]
            chains = [(hd, b) for b in range(len(js)) for hd in range(2)]
            zs = [_dot_nt(qs[hd], kjs[b]) for hd, b in chains]
            sps = [_softplus_parts(z)[0] for z in zs]
            if masked:
                t_idx = i * tq + lax.broadcasted_iota(jnp.int32, (tq, tk), 0)
                befores = [j * tk + lax.broadcasted_iota(jnp.int32, (tq, tk), 1) < t_idx for j in js]
                spms = [jnp.where(befores[b], sp, 0.0) for (hd, b), sp in zip(chains, sps)]
            else:
                spms = sps
            cums = [_block_sums(spm, later) for spm in spms]
            runs = [run_refs[0], run_refs[1]]
            ws = []
            for (hd, b), z, sp, (cum, sm) in zip(chains, zs, sps, cums):
                w = jnp.exp(z - sp - cum - _across(runs[hd], tk))
                if masked:
                    w = jnp.where(befores[b], w, 0.0)
                ws.append(w.astype(BF16))
                runs[hd] = runs[hd] + sm
            pvs = [_dot(w, vjs[b]) for (hd, b), w in zip(chains, ws)]
            for hd in range(2):
                tot = pvs[hd]
                for b in range(1, len(js)):
                    tot = tot + pvs[2 * b + hd]
                acc_refs[hd] += tot
                run_refs[hd] = runs[hd]

        blocks([i * r + (r - 1 - d) for d in range(r)], True)

        below = i * r

        def swept_mass():
            return jnp.min(jnp.minimum(run_refs[0], run_refs[1]))

        def more(carry):
            n, mass = carry
            return (n < below // SB_UNROLL) & (mass < SB_DEAD_MASS)

        def sweep(carry):
            n, _ = carry
            top = below - 1 - SB_UNROLL * n
            blocks([top - u for u in range(SB_UNROLL)], False)
            return n + 1, swept_mass()

        n_swept, _ = lax.while_loop(more, sweep, (0, swept_mass()))
        start_ref[pl.program_id(0), i] = (below - SB_UNROLL * n_swept).astype(F32)
        o_ref[...] = jnp.where(first, acc_refs[0], acc_refs[1])
        tot_ref[...] = jnp.where(first, run_refs[0], run_refs[1])

        @pl.when(step == nkb * nq - 1)
        def _():
            gather.wait()

    shp = jax.ShapeDtypeStruct((s, SB_W), F32)
    qspec = pl.BlockSpec((tq, LANES), lambda p, i: (i, p))
    hbm = pl.BlockSpec(memory_space=pl.ANY)
    return pl.pallas_call(
        body, name=name, grid=(nkb, nq),
        out_shape=[shp, shp, jax.ShapeDtypeStruct((nkb, nq), F32)]
        + [jax.ShapeDtypeStruct((N_CHIPS,) + a.shape, a.dtype) for a in shards],
        in_specs=[qspec,
                  pl.BlockSpec((s, LANES), lambda p, i: (0, nkb + p)),
                  pl.BlockSpec((s, LANES), lambda p, i: (0, 2 * nkb + p))] + [hbm] * n_sh,
        out_specs=[qspec, qspec, pl.BlockSpec(memory_space=pltpu.SMEM)] + [hbm] * n_sh,
        scratch_shapes=[pltpu.VMEM((2, tq, LANES), F32), pltpu.VMEM((2, tq, LANES), F32)] + _exchange_sems(n_sh),
        compiler_params=_cparams())(proj, proj, proj, *shards)


def _swa_masks(n):
    ti = lax.broadcasted_iota(jnp.int32, (WINDOW, 2 * WINDOW), 0)
    kj = lax.broadcasted_iota(jnp.int32, (WINDOW, 2 * WINDOW), 1)
    dist = ti + WINDOW - kj
    valid = (dist >= 0) & (dist < WINDOW) & ((n * WINDOW - WINDOW + kj) >= 0)
    return valid, dist.astype(F32)


def _swa_probs(sc, valid, distf, h, sink):
    slope = 2.0 ** (-(h + 1))
    sc = jnp.where(valid, sc - slope * distf, MASK_VALUE)
    mx = jnp.maximum(jnp.max(sc, axis=1, keepdims=True), sink)
    p = jnp.exp(sc - mx)
    es = jnp.exp(sink - mx)
    inv = 1.0 / (jnp.sum(p, axis=1, keepdims=True) + es)
    return p * inv, es * inv


def _swa_forward(proj, sinks, name):
    s = proj.shape[0]
    nb = s // WINDOW
    qb, kb, vb = 3 * SB_W // SWA_QW, (3 * SB_W + SWA_QW) // LANES, (3 * SB_W + SWA_QW + SWA_KW) // LANES

    def body(q_ref, kp_ref, kc_ref, vp_ref, vc_ref, sink_ref, o_ref):
        n = pl.program_id(0)
        k = jnp.concatenate([kp_ref[...], kc_ref[...]], axis=0)
        v = jnp.concatenate([vp_ref[...], vc_ref[...]], axis=0)
        k_sw = pltpu.roll(k.astype(F32), HEAD_DIM, 1).astype(BF16)
        v_sw = pltpu.roll(v.astype(F32), HEAD_DIM, 1).astype(BF16)
        lane = lax.broadcasted_iota(jnp.int32, (1, LANES), 1)
        halves = [lane < HEAD_DIM, lane >= HEAD_DIM]
        valid, distf = _swa_masks(n)
        heads = range(2 * 4)
        qms = [jnp.where(halves[h % 2], q_ref[:, (h // 2) * LANES:(h // 2 + 1) * LANES], jnp.zeros((), BF16))
               for h in heads]
        kus = [k if h // 4 == h % 2 else k_sw for h in heads]
        vus = [v if h // 4 == h % 2 else v_sw for h in heads]
        scores = [_dot_nt(qms[h], kus[h]) for h in heads]
        ps = [_swa_probs(scores[h], valid, distf, h, sink_ref[h])[0].astype(BF16) for h in heads]
        outs = [_dot(ps[h], vus[h]) for h in heads]
        for pair in range(4):
            o_ref[:, pair * LANES:(pair + 1) * LANES] = jnp.where(halves[0], outs[2 * pair], outs[2 * pair + 1])

    prev = lambda n: jnp.maximum(n - 1, 0)
    return pl.pallas_call(
        body, name=name, grid=(nb,),
        out_shape=jax.ShapeDtypeStruct((s, SWA_QW), F32),
        in_specs=[pl.BlockSpec((WINDOW, SWA_QW), lambda n: (n, qb)),
                  pl.BlockSpec((WINDOW, LANES), lambda n: (prev(n), kb)),
                  pl.BlockSpec((WINDOW, LANES), lambda n: (n, kb)),
                  pl.BlockSpec((WINDOW, LANES), lambda n: (prev(n), vb)),
                  pl.BlockSpec((WINDOW, LANES), lambda n: (n, vb)),
                  pl.BlockSpec(memory_space=pltpu.SMEM)],
        out_specs=pl.BlockSpec((WINDOW, SWA_QW), lambda n: (n, 0)),
        compiler_params=_cparams())(proj, proj, proj, proj, proj, sinks)


def _rms_parts(y):
    return lax.rsqrt(jnp.mean(y * y, axis=1, keepdims=True) + RMS_EPS)


def _post_attention(y_sb, y_sw, x, vec, w_out, name):
    s = x.shape[0]
    tb = min(TOK_TILE, s)

    def body(ysb_ref, ysw_ref, x_ref, vec_ref, w_ref, mixedt_ref, attn_ref, x1_ref, h2_ref, h2t_ref):
        ysb, ysw = ysb_ref[...], ysw_ref[...]
        nsb_f = ysb * _rms_parts(ysb) * vec_ref[V_GN:V_GN + 1, :SB_W]
        nsw_f = ysw * _rms_parts(ysw) * vec_ref[V_GN:V_GN + 1, SB_W:]
        nsb, nsw = nsb_f.astype(BF16), nsw_f.astype(BF16)
        mixedt_ref[:SB_W, :] = nsb_f.T.astype(BF16)
        mixedt_ref[SB_W:, :] = nsw_f.T.astype(BF16)
        attn = _dot(nsb, w_ref[:SB_W, :]) + _dot(nsw, w_ref[SB_W:, :])
        attn_ref[...] = attn
        u1 = ALPHA * x_ref[...] + (1.0 + vec_ref[V_G_A:V_G_A + 1, :]) * attn
        xhat, _ = _layer_norm_stats(u1)
        x1 = xhat * vec_ref[V_LN1G:V_LN1G + 1, :] + vec_ref[V_LN1B:V_LN1B + 1, :]
        x1_ref[...] = x1
        h2 = x1 * (1.0 + vec_ref[V_SC_F:V_SC_F + 1, :]) + vec_ref[V_SH_F:V_SH_F + 1, :]
        h2_ref[...] = h2.astype(BF16)
        h2t_ref[...] = h2.T.astype(BF16)

    half = pl.BlockSpec((tb, SB_W), lambda i: (i, 0))
    full = pl.BlockSpec((tb, D), lambda i: (i, 0))
    full_t = pl.BlockSpec((D, tb), lambda i: (0, i))
    return pl.pallas_call(
        body, name=name, grid=(s // tb,),
        out_shape=[jax.ShapeDtypeStruct((D, s), BF16), jax.ShapeDtypeStruct((s, D), F32),
                   jax.ShapeDtypeStruct((s, D), F32), jax.ShapeDtypeStruct((s, D), BF16),
                   jax.ShapeDtypeStruct((D, s), BF16)],
        in_specs=[half, half, full, _resident((VEC_ROWS, D)), _resident((D, D))],
        out_specs=[full_t, full, full, full, full_t],
        compiler_params=_cparams())(y_sb, y_sw, x, vec, w_out)


def _ffn_forward(h2, w_gu, w_down, name):
    s = h2.shape[0]
    tb = min(FFN_TILE, s)

    def body(h_ref, wgu_ref, wd_ref, gu_ref, actt_ref, ffn_ref):
        gu = _dot(h_ref[...], wgu_ref[...])
        gu_ref[...] = gu.astype(BF16)
        gate, up = gu[:, :D_FF], gu[:, D_FF:]
        act = gate * (1.0 / (1.0 + jnp.exp(-gate))) * up
        actt_ref[...] = act.T.astype(BF16)
        ffn_ref[...] = _dot(act.astype(BF16), wd_ref[...])

    return pl.pallas_call(
        body, name=name, grid=(s // tb,),
        out_shape=[jax.ShapeDtypeStruct((s, 2 * D_FF), BF16), jax.ShapeDtypeStruct((D_FF, s), BF16),
                   jax.ShapeDtypeStruct((s, D), F32)],
        in_specs=[pl.BlockSpec((tb, D), lambda i: (i, 0)), _resident((D, 2 * D_FF)), _resident((D_FF, D))],
        out_specs=[pl.BlockSpec((tb, 2 * D_FF), lambda i: (i, 0)), pl.BlockSpec((D_FF, tb), lambda i: (0, i)),
                   pl.BlockSpec((tb, D), lambda i: (i, 0))],
        compiler_params=_cparams())(h2, w_gu, w_down)


def _layer_norm_bwd(dxhat, xhat, rstd):
    m1 = jnp.mean(dxhat, axis=1, keepdims=True)
    m2 = jnp.mean(dxhat * xhat, axis=1, keepdims=True)
    return rstd * (dxhat - m1 - xhat * m2)


def _colsum(a):
    return jnp.sum(a, axis=0, keepdims=True)


A_LN2G, A_LN2B, A_GF, A_SCF, A_SHF, A_LOSS = range(6)
B_LN1G, B_LN1B, B_GA, B_GN = range(4)
C_SCA, C_SHA = range(2)


def _ffn_backward(x1, ffn, target, gu, vec, w_gu, w_down, name):
    s = x1.shape[0]
    tb = min(FFN_BWD_TILE, s)

    def body(x1_ref, ffn_ref, t_ref, gu_ref, vec_ref, wgu_ref, wd_ref, dffn_ref, dgu_ref, dx1_ref, acc_ref):
        @pl.when(pl.program_id(0) == 0)
        def _():
            acc_ref[...] = jnp.zeros_like(acc_ref)

        x1v, ffn_v = x1_ref[...], ffn_ref[...]
        g_f = 1.0 + vec_ref[V_G_F:V_G_F + 1, :]
        u2 = ALPHA * x1v + g_f * ffn_v
        xhat, rstd = _layer_norm_stats(u2)
        ln_g = vec_ref[V_LN2G:V_LN2G + 1, :]
        err = xhat * ln_g + vec_ref[V_LN2B:V_LN2B + 1, :] - t_ref[...]
        dx2 = err * (1.0 / D)
        acc_ref[A_LOSS:A_LOSS + 1, :] += _colsum(err * err) * (0.5 / D)
        acc_ref[A_LN2G:A_LN2G + 1, :] += _colsum(dx2 * xhat)
        acc_ref[A_LN2B:A_LN2B + 1, :] += _colsum(dx2)
        du2 = _layer_norm_bwd(dx2 * ln_g, xhat, rstd)
        acc_ref[A_GF:A_GF + 1, :] += _colsum(du2 * ffn_v)
        dffn = (g_f * du2).astype(BF16)
        dffn_ref[...] = dffn
        dact = _dot_nt(dffn, wd_ref[...])
        gate, up = gu_ref[:, :D_FF].astype(F32), gu_ref[:, D_FF:].astype(F32)
        sg = 1.0 / (1.0 + jnp.exp(-gate))
        dgate = (dact * up * (sg * (1.0 + gate * (1.0 - sg)))).astype(BF16)
        dup = (dact * (gate * sg)).astype(BF16)
        dgu_ref[:, :D_FF] = dgate
        dgu_ref[:, D_FF:] = dup
        dh2 = _dot_nt(dgate, wgu_ref[:, :D_FF]) + _dot_nt(dup, wgu_ref[:, D_FF:])
        dx1_ref[...] = ALPHA * du2 + dh2 * (1.0 + vec_ref[V_SC_F:V_SC_F + 1, :])
        acc_ref[A_SCF:A_SCF + 1, :] += _colsum(dh2 * x1v)
        acc_ref[A_SHF:A_SHF + 1, :] += _colsum(dh2)

    full = pl.BlockSpec((tb, D), lambda i: (i, 0))
    wide = pl.BlockSpec((tb, 2 * D_FF), lambda i: (i, 0))
    return pl.pallas_call(
        body, name=name, grid=(s // tb,),
        out_shape=[jax.ShapeDtypeStruct((s, D), BF16), jax.ShapeDtypeStruct((s, 2 * D_FF), BF16),
                   jax.ShapeDtypeStruct((s, D), F32), jax.ShapeDtypeStruct((8, D), F32)],
        in_specs=[full, full, full, wide, _resident((VEC_ROWS, D)), _resident((D, 2 * D_FF)), _resident((D_FF, D))],
        out_specs=[full, wide, full, pl.BlockSpec((8, D), lambda i: (0, 0))],
        compiler_params=_cparams())(x1, ffn, target, gu, vec, w_gu, w_down)


def _attn_out_backward(dx1, x, attn, y_sb, y_sw, vec, w_out, name):
    s = x.shape[0]
    tb = min(TOK_TILE, s)

    def body(dx1_ref, x_ref, attn_ref, ysb_ref, ysw_ref, vec_ref, w_ref, du1_ref, dattn_ref, dy_ref, acc_ref):
        @pl.when(pl.program_id(0) == 0)
        def _():
            acc_ref[...] = jnp.zeros_like(acc_ref)

        attn = attn_ref[...]
        g_a = 1.0 + vec_ref[V_G_A:V_G_A + 1, :]
        xhat, rstd = _layer_norm_stats(ALPHA * x_ref[...] + g_a * attn)
        dx1v = dx1_ref[...]
        acc_ref[B_LN1G:B_LN1G + 1, :] += _colsum(dx1v * xhat)
        acc_ref[B_LN1B:B_LN1B + 1, :] += _colsum(dx1v)
        du1 = _layer_norm_bwd(dx1v * vec_ref[V_LN1G:V_LN1G + 1, :], xhat, rstd)
        du1_ref[...] = du1
        acc_ref[B_GA:B_GA + 1, :] += _colsum(du1 * attn)
        dattn = (g_a * du1).astype(BF16)
        dattn_ref[...] = dattn
        dmixed = _dot_nt(dattn, w_ref[...])
        for lo, y_ref in ((0, ysb_ref), (SB_W, ysw_ref)):
            y = y_ref[...]
            rr = _rms_parts(y)
            dn = dmixed[:, lo:lo + SB_W]
            acc_ref[B_GN:B_GN + 1, lo:lo + SB_W] += _colsum(dn * y * rr)
            dng = dn * vec_ref[V_GN:V_GN + 1, lo:lo + SB_W]
            dy_ref[:, lo:lo + SB_W] = rr * dng - y * (rr * rr * rr) * jnp.mean(dng * y, axis=1, keepdims=True)

    half = pl.BlockSpec((tb, SB_W), lambda i: (i, 0))
    full = pl.BlockSpec((tb, D), lambda i: (i, 0))
    return pl.pallas_call(
        body, name=name, grid=(s // tb,),
        out_shape=[jax.ShapeDtypeStruct((s, D), F32), jax.ShapeDtypeStruct((s, D), BF16),
                   jax.ShapeDtypeStruct((s, D), F32), jax.ShapeDtypeStruct((8, D), F32)],
        in_specs=[full, full, full, half, half, _resident((VEC_ROWS, D)), _resident((D, D))],
        out_specs=[full, full, full, pl.BlockSpec((8, D), lambda i: (0, 0))],
        compiler_params=_cparams())(dx1, x, attn, y_sb, y_sw, vec, w_out)


def _sb_backward(proj, sp_total, sweep_start, dy, slabs, name):
    s = proj.shape[0]
    tq, tk = min(SB_TQ, s), min(SB_TK, s)
    r = tq // tk
    nkb = SB_W // LANES
    nq = s // tq

    assert r % SB_UNROLL == 0, "the sweep below the diagonal takes whole steps"

    n_sl = len(slabs)

    def body(q_ref, k_ref, v_ref, tot_ref, do_ref, start_ref, *refs):
        slab_refs, (dq_ref, dk_ref, dv_ref), got_refs = refs[:n_sl], refs[n_sl:n_sl + 3], refs[n_sl + 3:2 * n_sl + 3]
        dq_acc, left_refs, gsum_refs = refs[2 * n_sl + 3:2 * n_sl + 6]
        i = pl.program_id(1)
        step = pl.program_id(0) * nq + i
        scatter = _scatter_exchange(slab_refs, got_refs, *refs[2 * n_sl + 6:])

        @pl.when(step == 0)
        def _():
            scatter.start()

        @pl.when(i == 0)
        def _():
            dk_ref[...] = jnp.zeros_like(dk_ref)
            dv_ref[...] = jnp.zeros_like(dv_ref)

        lane = lax.broadcasted_iota(jnp.int32, (1, LANES), 1)
        first = lane < HEAD_DIM
        qp, dop, totp = q_ref[...], do_ref[...], tot_ref[...]
        zero = jnp.zeros((), BF16)
        qs = (jnp.where(first, qp, zero), jnp.where(first, zero, qp))
        dofs = (jnp.where(first, dop, 0.0), jnp.where(first, 0.0, dop))
        dobs = tuple(d.astype(BF16) for d in dofs)
        dots = tuple(d.T.astype(BF16) for d in dofs)
        qts = tuple(qh.astype(F32).T.astype(BF16) for qh in qs)
        later = _sum_matrix(tk, lambda row, col: row > col)
        earlier = _sum_matrix(tk, lambda row, col: row < col)
        dq_acc[...] = jnp.zeros_like(dq_acc)
        gsum_refs[...] = jnp.zeros_like(gsum_refs)
        swapped = pltpu.roll(totp, HEAD_DIM, 1)
        left_refs[0] = jnp.where(first, totp, swapped)
        left_refs[1] = jnp.where(first, swapped, totp)

        def blocks(js, masked):
            ks = [pl.multiple_of(j * tk, tk) for j in js]
            kjs = [k_ref[pl.ds(k0, tk), :] for k0 in ks]
            vjs = [v_ref[pl.ds(k0, tk), :] for k0 in ks]
            chains = [(hd, b) for b in range(len(js)) for hd in range(2)]
            zs = [_dot_nt(qs[hd], kjs[b]) for hd, b in chains]
            dws = [_dot_nt(dobs[hd], vjs[b]) for hd, b in chains]
            parts = [_softplus_parts(z) for z in zs]
            sps = [p[0] for p in parts]
            if masked:
                t_idx = i * tq + lax.broadcasted_iota(jnp.int32, (tq, tk), 0)
                befores = [j * tk + lax.broadcasted_iota(jnp.int32, (tq, tk), 1) < t_idx for j in js]
                spms = [jnp.where(befores[b], sp, 0.0) for (hd, b), sp in zip(chains, sps)]
            else:
                spms = sps
            cums = [_block_sums(spm, later) for spm in spms]
            lefts = [left_refs[0], left_refs[1]]
            ws = []
            for (hd, b), z, sp, (cum, sm) in zip(chains, zs, sps, cums):
                lefts[hd] = lefts[hd] - sm
                w = jnp.exp(z - sp - cum - _across(lefts[hd], tk))
                if masked:
                    w = jnp.where(befores[b], w, 0.0)
                ws.append(w)
            wbs = [w.astype(BF16) for w in ws]
            dvs = [_dot(dots[hd], wb) for (hd, b), wb in zip(chains, wbs)]
            gs = [dw * w for dw, w in zip(dws, ws)]
            gcums = [_block_sums(g, earlier) for g in gs]
            gsums = [gsum_refs[0], gsum_refs[1]]
            dzbs = []
            for (hd, b), z, (sp, e1), g, (gcum, gsm) in zip(chains, zs, parts, gs, gcums):
                inv = 1.0 / (1.0 + e1)
                sig = jnp.where(z >= 0.0, inv, e1 * inv)
                dz = g - sig * (g + _across(gsums[hd], tk) + gcum)
                if masked:
                    dz = jnp.where(befores[b], dz, 0.0)
                dzbs.append(dz.astype(BF16))
                gsums[hd] = gsums[hd] + gsm
            dqs = [_dot(dzb, kjs[b]) for (hd, b), dzb in zip(chains, dzbs)]
            dks = [_dot(qts[hd], dzb) for (hd, b), dzb in zip(chains, dzbs)]
            for b, j in enumerate(js):
                dv_ref[j] += dvs[2 * b] + dvs[2 * b + 1]
                dk_ref[j] += dks[2 * b] + dks[2 * b + 1]
            for hd in range(2):
                tot = dqs[hd]
                for b in range(1, len(js)):
                    tot = tot + dqs[2 * b + hd]
                dq_acc[hd] += tot
                left_refs[hd] = lefts[hd]
                gsum_refs[hd] = gsums[hd]

        below = i * r
        start = jnp.clip(start_ref[pl.program_id(0), i].astype(jnp.int32), 0, below) // SB_UNROLL * SB_UNROLL

        def sweep(n, carry):
            blocks([start + SB_UNROLL * n + u for u in range(SB_UNROLL)], False)
            return carry

        lax.fori_loop(0, (below - start) // SB_UNROLL, sweep, 0)
        blocks([below + d for d in range(r)], True)
        dq_ref[...] = jnp.where(first, dq_acc[0], dq_acc[1])

        @pl.when(step == nkb * nq - 1)
        def _():
            scatter.wait()

    shp = jax.ShapeDtypeStruct((s, SB_W), F32)
    qspec = pl.BlockSpec((tq, LANES), lambda p, i: (i, p))
    whole = pl.BlockSpec((None, s // tk, LANES, tk), lambda p, i: (p, 0, 0, 0))
    shp_t = jax.ShapeDtypeStruct((nkb, s // tk, LANES, tk), F32)
    hbm = pl.BlockSpec(memory_space=pl.ANY)
    return pl.pallas_call(
        body, name=name, grid=(nkb, nq),
        out_shape=[shp, shp_t, shp_t] + [jax.ShapeDtypeStruct(p.shape, p.dtype) for p in slabs],
        in_specs=[qspec,
                  pl.BlockSpec((s, LANES), lambda p, i: (0, nkb + p)),
                  pl.BlockSpec((s, LANES), lambda p, i: (0, 2 * nkb + p)),
                  qspec, qspec, pl.BlockSpec(memory_space=pltpu.SMEM)] + [hbm] * n_sl,
        out_specs=[qspec, whole, whole] + [hbm] * n_sl,
        scratch_shapes=[pltpu.VMEM((2, tq, LANES), F32), pltpu.VMEM((2, tq, LANES), F32), pltpu.VMEM((2, tq, LANES), F32)]
        + _exchange_sems(n_sl),
        compiler_params=_cparams())(proj, proj, proj, sp_total, dy, sweep_start, *slabs)


def _swa_backward(proj, y_sw, dy, sinks, gives, name):
    s = proj.shape[0]
    nb = s // WINDOW
    qb, kb, vb = 3 * SB_W // SWA_QW, (3 * SB_W + SWA_QW) // LANES, (3 * SB_W + SWA_QW + SWA_KW) // LANES

    n_gv = len(gives)

    def body(q_ref, kp_ref, kc_ref, vp_ref, vc_ref, o_ref, do_ref, sink_ref, *refs):
        give_refs, (dq_ref, dk_ref, dv_ref, ds_ref), got_refs = refs[:n_gv], refs[n_gv:n_gv + 4], refs[n_gv + 4:2 * n_gv + 4]
        n = pl.program_id(0)
        swap = _sibling_halves(give_refs, got_refs, *refs[2 * n_gv + 4:])

        @pl.when(n == 0)
        def _():
            for cp in swap:
                cp.start()

        @pl.when(n == 0)
        def _():
            dk_ref[...] = jnp.zeros_like(dk_ref)
            dv_ref[...] = jnp.zeros_like(dv_ref)
            ds_ref[...] = jnp.zeros_like(ds_ref)

        k = jnp.concatenate([kp_ref[...], kc_ref[...]], axis=0)
        v = jnp.concatenate([vp_ref[...], vc_ref[...]], axis=0)
        k_sw = pltpu.roll(k.astype(F32), HEAD_DIM, 1).astype(BF16)
        v_sw = pltpu.roll(v.astype(F32), HEAD_DIM, 1).astype(BF16)
        lane = lax.broadcasted_iota(jnp.int32, (1, LANES), 1)
        halves = [lane < HEAD_DIM, lane >= HEAD_DIM]
        valid, distf = _swa_masks(n)
        heads = range(2 * 4)
        cols = [slice((h // 2) * LANES, (h // 2 + 1) * LANES) for h in heads]
        qms = [jnp.where(halves[h % 2], q_ref[:, cols[h]], jnp.zeros((), BF16)) for h in heads]
        dos = [jnp.where(halves[h % 2], do_ref[:, cols[h]], 0.0) for h in heads]
        dobs = [d.astype(BF16) for d in dos]
        native = [h // 4 == h % 2 for h in heads]
        kus = [k if native[h] else k_sw for h in heads]
        vus = [v if native[h] else v_sw for h in heads]
        scores = [_dot_nt(qms[h], kus[h]) for h in heads]
        dps = [_dot_nt(dobs[h], vus[h]) for h in heads]
        deltas = [jnp.sum(dos[h] * o_ref[:, cols[h]], axis=1, keepdims=True) for h in heads]
        probs = [_swa_probs(scores[h], valid, distf, h, sink_ref[h]) for h in heads]
        pbs = [probs[h][0].astype(BF16) for h in heads]
        dscs = [(probs[h][0] * (dps[h] - deltas[h])).astype(BF16) for h in heads]
        dqs = [_dot(dscs[h], kus[h]) for h in heads]
        dks = [_dot_tn(dscs[h], qms[h]) for h in heads]
        dvs = [_dot_tn(pbs[h], dobs[h]) for h in heads]
        for h in heads:
            ds_ref[h:h + 1, :] += jnp.zeros((1, LANES), F32) - jnp.sum(probs[h][1] * deltas[h])
        for pair in range(4):
            dq_ref[:, cols[2 * pair]] = jnp.where(halves[0], dqs[2 * pair], dqs[2 * pair + 1])

        def gathered(parts):
            nat = sum(parts[h] for h in heads if native[h])
            rot = sum(parts[h] for h in heads if not native[h])
            return nat + pltpu.roll(rot, HEAD_DIM, 1)

        dk, dv = gathered(dks), gathered(dvs)
        prev = pl.multiple_of(jnp.maximum(n - 1, 0) * WINDOW, WINDOW)
        cur = pl.multiple_of(n * WINDOW, WINDOW)
        dk_ref[pl.ds(prev, WINDOW), :] += dk[:WINDOW]
        dv_ref[pl.ds(prev, WINDOW), :] += dv[:WINDOW]
        dk_ref[pl.ds(cur, WINDOW), :] += dk[WINDOW:]
        dv_ref[pl.ds(cur, WINDOW), :] += dv[WINDOW:]

        @pl.when(n == nb - 1)
        def _():
            for cp in swap:
                cp.wait()

    prev_blk = lambda n: jnp.maximum(n - 1, 0)
    wide = pl.BlockSpec((WINDOW, SWA_QW), lambda n: (n, 0))
    whole = pl.BlockSpec((s, LANES), lambda n: (0, 0))
    hbm = pl.BlockSpec(memory_space=pl.ANY)
    return pl.pallas_call(
        body, name=name, grid=(nb,),
        out_shape=[jax.ShapeDtypeStruct((s, SWA_QW), F32), jax.ShapeDtypeStruct((s, LANES), F32),
                   jax.ShapeDtypeStruct((s, LANES), F32), jax.ShapeDtypeStruct((8, LANES), F32)] + _halves_shapes(gives),
        in_specs=[pl.BlockSpec((WINDOW, SWA_QW), lambda n: (n, qb)),
                  pl.BlockSpec((WINDOW, LANES), lambda n: (prev_blk(n), kb)),
                  pl.BlockSpec((WINDOW, LANES), lambda n: (n, kb)),
                  pl.BlockSpec((WINDOW, LANES), lambda n: (prev_blk(n), vb)),
                  pl.BlockSpec((WINDOW, LANES), lambda n: (n, vb)),
                  wide,
                  pl.BlockSpec((WINDOW, SWA_QW), lambda n: (n, 1)),
                  pl.BlockSpec(memory_space=pltpu.SMEM)] + [hbm] * n_gv,
        out_specs=[wide, whole, whole, pl.BlockSpec((8, LANES), lambda n: (0, 0))] + [hbm] * n_gv,
        scratch_shapes=_halves_sems(n_gv),
        compiler_params=_cparams())(proj, proj, proj, proj, proj, y_sw, dy, sinks, *gives)


def _in_proj_backward(dq_sb, dkt_sb, dvt_sb, dq_sw, dk_sw, dv_sw, du1, x, vec, w_in, name):
    s = x.shape[0]
    tb = min(TOK_TILE, s)
    n_pairs, _, _, tk = dkt_sb.shape

    def body(dqsb_ref, dktsb_ref, dvtsb_ref, dqsw_ref, dksw_ref, dvsw_ref, du1_ref, x_ref, vec_ref, w_ref,
             dproj_ref, gx_ref, acc_ref, bacc_ref):
        @pl.when(pl.program_id(0) == 0)
        def _():
            acc_ref[...] = jnp.zeros_like(acc_ref)
            bacc_ref[...] = jnp.zeros_like(bacc_ref)

        pieces = ((0, dqsb_ref, QK_SCALE), (3 * SB_W, dqsw_ref, QK_SCALE), (3 * SB_W + SWA_QW, dksw_ref, 1.0),
                  (3 * SB_W + SWA_QW + SWA_KW, dvsw_ref, 1.0))
        for lo, ref, scale in pieces:
            width = ref.shape[1]
            piece = ref[...] * scale
            bacc_ref[0:1, lo:lo + width] += _colsum(piece)
            dproj_ref[:, lo:lo + width] = piece.astype(BF16)
        for base, ref in ((SB_W, dktsb_ref), (2 * SB_W, dvtsb_ref)):
            for p in range(n_pairs):
                lo = base + p * LANES
                for jj in range(tb // tk):
                    piece = ref[p, jj].T
                    bacc_ref[0:1, lo:lo + LANES] += _colsum(piece)
                    dproj_ref[jj * tk:(jj + 1) * tk, lo:lo + LANES] = piece.astype(BF16)
        dh = _dot_nt(dproj_ref[...], w_ref[...])
        xv = x_ref[...]
        gx_ref[...] = ALPHA * du1_ref[...] + dh * (1.0 + vec_ref[V_SC_A:V_SC_A + 1, :])
        acc_ref[C_SCA:C_SCA + 1, :] += _colsum(dh * xv)
        acc_ref[C_SHA:C_SHA + 1, :] += _colsum(dh)

    half = pl.BlockSpec((tb, SB_W), lambda i: (i, 0))
    narrow = pl.BlockSpec((tb, LANES), lambda i: (i, 0))
    full = pl.BlockSpec((tb, D), lambda i: (i, 0))
    blocks_t = pl.BlockSpec((n_pairs, tb // tk, LANES, tk), lambda i: (0, i, 0, 0))
    return pl.pallas_call(
        body, name=name, grid=(s // tb,),
        out_shape=[jax.ShapeDtypeStruct((s, D_IN), BF16), jax.ShapeDtypeStruct((s, D), F32),
                   jax.ShapeDtypeStruct((8, D), F32), jax.ShapeDtypeStruct((8, D_IN), F32)],
        in_specs=[half, blocks_t, blocks_t, half, narrow, narrow, full, full, _resident((VEC_ROWS, D)),
                  _resident((D, D_IN))],
        out_specs=[pl.BlockSpec((tb, D_IN), lambda i: (i, 0)), full, pl.BlockSpec((8, D), lambda i: (0, 0)),
                   pl.BlockSpec((8, D_IN), lambda i: (0, 0))],
        compiler_params=_cparams())(dq_sb, dkt_sb, dvt_sb, dq_sw, dk_sw, dv_sw, du1, x, vec, w_in)


def _weight_grad(at, b, name, col_shards=1):
    m, s = at.shape
    n = b.shape[1]
    if col_shards > 1:
        tn = n // col_shards
        out_shape = jax.ShapeDtypeStruct((col_shards, m, tn), F32)
        out_spec = pl.BlockSpec((None, m, tn), lambda j, k: (j, 0, 0))
    else:
        tn = 512 if n % 512 == 0 else n
        out_shape = jax.ShapeDtypeStruct((m, n), F32)
        out_spec = pl.BlockSpec((m, tn), lambda j, k: (0, j))
    ts = min(512, s)

    def body(at_ref, b_ref, o_ref):
        @pl.when(pl.program_id(1) == 0)
        def _():
            o_ref[...] = jnp.zeros_like(o_ref)

        o_ref[...] += _dot(at_ref[...], b_ref[...])

    return pl.pallas_call(
        body, name=name, grid=(n // tn, s // ts),
        out_shape=out_shape,
        in_specs=[pl.BlockSpec((m, ts), lambda j, k: (0, k)), pl.BlockSpec((ts, tn), lambda j, k: (k, j))],
        out_specs=out_spec,
        compiler_params=_cparams())(at, b)


def _pad_rows(v, rows):
    return jnp.concatenate([v, jnp.zeros((rows - v.shape[0], v.shape[1]), v.dtype)], axis=0)


def _col_shards(w, n_shards):
    r, n = w.shape
    return w.reshape(r, n_shards, n // n_shards).transpose(1, 0, 2)


def kernel(x, c, w_ada, b_ada, w_in, b_in, sinks, gn_sb, gn_swa, w_out, ln1_g, ln1_b, w_gu, w_down, ln2_g, ln2_b, loss_target, m_w_ada, m_b_ada, m_w_in, m_b_in, m_sinks, m_gn_sb, m_gn_swa, m_w_out, m_ln1_g, m_ln1_b, m_w_gu, m_w_down, m_ln2_g, m_ln2_b, v_w_ada, v_b_ada, v_w_in, v_b_in, v_sinks, v_gn_sb, v_gn_swa, v_w_out, v_ln1_g, v_ln1_b, v_w_gu, v_w_down, v_ln2_g, v_ln2_b):
    ix, iy, ic = lax.axis_index("x"), lax.axis_index("y"), lax.axis_index("c")
    chip = 2 * ix + iy
    dev = 4 * ix + 2 * iy + ic
    xs, target = x[0], loss_target[0]
    s = xs.shape[0]

    c_all = _allgather8(_pad_rows(c, 8), "gather_c")[::8]
    n_ada = w_ada.shape[2]
    b_ada_shard = lax.dynamic_slice_in_dim(b_ada, chip * n_ada, n_ada, axis=1)
    mod_cols, silu_c = _mod_shard(c_all, w_ada[0], b_ada_shard, "mod_shard")
    mod_all = _allgather8(mod_cols, "gather_mod").reshape(N_DEV, 8, n_ada)
    mod_mine = lax.dynamic_index_in_dim(mod_all, dev, axis=1, keepdims=False)
    mod = mod_mine.reshape(N_CHIPS, 2, n_ada)[:, 0].reshape(6, D)
    vec = jnp.concatenate([mod, ln1_g, ln1_b, ln2_g, ln2_b, jnp.concatenate([gn_sb, gn_swa], axis=1),
                           jnp.zeros((VEC_ROWS - 11, D), F32)], axis=0)

    (g_in,) = _chip_allgather([w_in[0].astype(BF16)], "gather_w_in")
    w_in_b = g_in.transpose(1, 0, 2).reshape(D, D_IN)

    h_t, proj = _in_proj(xs, vec, w_in_b, b_in, "in_proj")
    y_sb, sp_total, sweep_start, g_out, g_gu, g_down = _sb_forward(
        proj, [w_out[0].astype(BF16), w_gu[0].astype(BF16), w_down[0].astype(BF16)], "sb_forward")
    w_gu_b = g_gu.transpose(1, 0, 2).reshape(D, 2 * D_FF)
    w_out_b = g_out.reshape(D, D)
    w_down_b = g_down.reshape(D_FF, D)
    sink_vec = sinks[0]
    y_sw = _swa_forward(proj, sink_vec, "swa_forward")
    mixed_t, attn, x1, h2_b, h2_t = _post_attention(y_sb, y_sw, xs, vec, w_out_b, "post_attention")
    gu, act_t, ffn = _ffn_forward(h2_b, w_gu_b, w_down_b, "ffn_forward")

    def in_halves(shards):
        n_sh, rows, cols = shards.shape
        return shards.reshape(n_sh, 2, rows // 2, cols)

    core = ic.reshape(1).astype(jnp.int32)
    dffn_b, dgu_b, dx1, acc_f = _ffn_backward(x1, ffn, target, gu, vec, w_gu_b, w_down_b, "ffn_backward")
    dw_gu = _weight_grad(h2_t, dgu_b, "grad_w_gu", col_shards=4)
    dw_down = _weight_grad(act_t, dffn_b, "grad_w_down")
    du1, dattn_b, dy, acc_a = _attn_out_backward(dx1, xs, attn, y_sb, y_sw, vec, w_out_b, "attn_out_backward")
    dw_out = _weight_grad(mixed_t, dattn_b, "grad_w_out")
    first = [in_halves(dw_gu), in_halves(dw_down.reshape(4, D_FF // 4, D)), in_halves(dw_out.reshape(4, D // 4, D))]
    dq_sw, dk_sw, dv_sw, dsink, *got_first = _swa_backward(proj, y_sw, dy, sink_vec, first, "swa_backward")
    sums_first = [_chip_sum(arr, got, core, "grad_chip_sum_" + nm)
                  for arr, got, nm in zip(first, got_first, ("gu", "down", "out"))]
    dq_sb, dk_sb, dv_sb, *parts_first = _sb_backward(proj, sp_total, sweep_start, dy, sums_first, "sb_backward")
    dproj_b, grad_x, acc_i, acc_b = _in_proj_backward(dq_sb, dk_sb, dv_sb, dq_sw, dk_sw, dv_sw, du1, xs, vec, w_in_b,
                                                      "in_proj_backward")
    dw_in = _weight_grad(h_t, dproj_b, "grad_w_in")
    last = [in_halves(_col_shards(dw_in, 4))]
    sums_last = [_chip_sum(last[0], _halves_swap(last, "grad_halves_swap_in")[0], core, "grad_chip_sum_in")]
    parts_last = _chip_scatter(sums_last, "grad_chip_scatter_in")
    mine = [_sum4(p, "grad_reduce_" + nm) for p, nm in zip([*parts_first, *parts_last], ("gu", "down", "out", "in"))]
    theirs = _sibling_send(mine, "grad_half_return")
    gw_gu, gw_down, gw_out, gw_in = [
        jnp.concatenate([jnp.where(ic == 0, m_, t_), jnp.where(ic == 0, t_, m_)], axis=0) for m_, t_ in zip(mine, theirs)]

    dmod = jnp.concatenate([acc_i[C_SHA:C_SHA + 1], acc_i[C_SCA:C_SCA + 1], acc_a[B_GA:B_GA + 1],
                            acc_f[A_SHF:A_SHF + 1], acc_f[A_SCF:A_SCF + 1], acc_f[A_GF:A_GF + 1]], axis=1)
    dsink_row = jnp.concatenate([dsink[:, 0].reshape(1, 8), jnp.zeros((1, LANES - 8), F32)], axis=1)
    loss_row = jnp.concatenate([jnp.sum(acc_f[A_LOSS:A_LOSS + 1], axis=1, keepdims=True),
                                jnp.zeros((1, LANES - 1), F32)], axis=1)
    small = jnp.concatenate([dmod, acc_b[0:1], acc_a[B_LN1G:B_LN1G + 1], acc_a[B_LN1B:B_LN1B + 1],
                             acc_f[A_LN2G:A_LN2G + 1], acc_f[A_LN2B:A_LN2B + 1], acc_a[B_GN:B_GN + 1],
                             dsink_row, loss_row], axis=1)
    small_all = _allgather8(_pad_rows(small, 8), "gather_small")[::8]

    small_names = ["b_ada", "b_in", "ln1_g", "ln1_b", "ln2_g", "ln2_b", "gn_sb", "gn_swa", "sinks"]
    small_at = [SM_MOD, SM_BIN, SM_LN1G, SM_LN1B, SM_LN2G, SM_LN2B, SM_GN, SM_GN + SB_W, SM_SINK]
    *small_out, loss_row_all = _small_update(
        small_all, small_at,
        [b_ada, b_in, ln1_g, ln1_b, ln2_g, ln2_b, gn_sb, gn_swa, sinks],
        [m_b_ada, m_b_in, m_ln1_g, m_ln1_b, m_ln2_g, m_ln2_b, m_gn_sb, m_gn_swa, m_sinks],
        [v_b_ada, v_b_in, v_ln1_g, v_ln1_b, v_ln2_g, v_ln2_b, v_gn_sb, v_gn_swa, v_sinks], SM_LOSS, "small_update")
    g_small, d_small, m2_small, v2_small = [dict(zip(small_names, leaves)) for leaves in small_out]
    loss = loss_row_all[0, 0]

    dmod_cols = lax.dynamic_slice_in_dim(small_all[:, SM_MOD:SM_BIN], chip * n_ada, n_ada, axis=1)
    gw_ada = _weight_grad(_pad_rows(silu_c, LANES).astype(BF16).T, _pad_rows(dmod_cols, LANES).astype(BF16), "grad_w_ada")

    big = {}
    for nm, w, g, m, v in (("w_ada", w_ada, gw_ada, m_w_ada, v_w_ada), ("w_in", w_in, gw_in, m_w_in, v_w_in),
                           ("w_out", w_out, gw_out, m_w_out, v_w_out), ("w_gu", w_gu, gw_gu, m_w_gu, v_w_gu),
                           ("w_down", w_down, gw_down, m_w_down, v_w_down)):
        d_, m2_, v2_ = _adamw(w[0], g, m[0], v[0], "adamw_" + nm)
        big[nm] = (g[None], d_[None], m2_[None], v2_[None])

    order = ["w_ada", "b_ada", "w_in", "b_in", "sinks", "gn_sb", "gn_swa", "w_out", "ln1_g", "ln1_b", "w_gu", "w_down",
             "ln2_g", "ln2_b"]

    def leaf(nm, which):
        if nm in big:
            return big[nm][which]
        return (g_small, d_small, m2_small, v2_small)[which][nm]

    outs = [loss, grad_x[None]]
    for which in range(4):
        outs += [leaf(nm, which) for nm in order]
    return tuple(outs)
```

```python
import functools
import math

import jax
import jax.numpy as jnp
from jax import lax
from jax.experimental import pallas as pl
from jax.experimental.pallas import tpu as pltpu

F32 = jnp.float32
BF16 = jnp.bfloat16

D = 1024
HEAD_DIM = 64
SB_W = 512
SWA_QW = 512
SWA_KW = 128
D_IN = 2304
D_FF = 2816
WINDOW = 128
ALPHA = 2.0 ** 0.25
LN_EPS = 1e-5
RMS_EPS = 1e-6
MASK_VALUE = -1e30
QK_SCALE = 1.0 / math.sqrt(HEAD_DIM)

ADAM_LR = 0.001
ADAM_B1 = 0.9
ADAM_B2 = 0.999
ADAM_EPS = 1e-08
ADAM_WD = 0.01
ADAM_STEP = 10

N_CHIPS = 4
N_DEV = 8
LANES = 128

SB_TQ = 512
SB_TK = 256
SB_UNROLL = 1
SB_DEAD_MASS = 110.0
TOK_TILE = 512
FFN_TILE = 256
FFN_BWD_TILE = 256
VMEM_LIMIT = 56 * 1024 * 1024

V_SH_A, V_SC_A, V_G_A, V_SH_F, V_SC_F, V_G_F, V_LN1G, V_LN1B, V_LN2G, V_LN2B, V_GN = range(11)
VEC_ROWS = 16

SM_MOD = 0
SM_BIN = 6 * D
SM_LN1G = SM_BIN + D_IN
SM_LN1B = SM_LN1G + D
SM_LN2G = SM_LN1B + D
SM_LN2B = SM_LN2G + D
SM_GN = SM_LN2B + D
SM_SINK = SM_GN + D
SM_LOSS = SM_SINK + LANES
SM_LEN = SM_LOSS + LANES

MESH = pl.DeviceIdType.MESH


def _cparams(**kw):
    return pltpu.CompilerParams(vmem_limit_bytes=VMEM_LIMIT, **kw)


def _resident(shape):
    nd = len(shape)
    return pl.BlockSpec(shape, lambda *_: (0,) * nd, pipeline_mode=pl.Buffered(1))


def _dot(a, b):
    return jnp.dot(a, b, preferred_element_type=F32)


def _dot_nt(a, b):
    return lax.dot_general(a, b, (((1,), (1,)), ((), ())), preferred_element_type=F32)


def _dot_tn(a, b):
    return lax.dot_general(a, b, (((0,), (0,)), ((), ())), preferred_element_type=F32)


def _sum_matrix(tk, keep):
    row = lax.broadcasted_iota(jnp.int32, (tk, tk + LANES), 0)
    col = lax.broadcasted_iota(jnp.int32, (tk, tk + LANES), 1)
    return (keep(row, col) | (col >= tk)).astype(BF16)


def _block_sums(x, m):
    tk = x.shape[1]
    res = _dot(x.astype(BF16), m)
    return res[:, :tk], res[:, tk:]


def _before(t0, n, s0, tk):
    return s0 + lax.broadcasted_iota(jnp.int32, (n, tk), 1) < t0 + lax.broadcasted_iota(jnp.int32, (n, tk), 0)


def _across(v, tk):
    return jnp.concatenate([v] * (tk // LANES), axis=1)


def _allgather8(v, name):
    m_per, n = v.shape

    def body(x_ref, out_ref, send_sems, recv_sems, local_sem):
        x, y, c = lax.axis_index("x"), lax.axis_index("y"), lax.axis_index("c")
        me, sibling = (x, y, c), (x, y, 1 - c)
        chips = [(1 - x, y), (x, 1 - y), (1 - x, 1 - y)]

        def rows(px, py, pc):
            return out_ref.at[pl.ds((4 * px + 2 * py + pc) * m_per, m_per), :]

        def copy(k, block, to, src=None):
            return pltpu.make_async_remote_copy(
                src_ref=rows(*block) if src is None else src, dst_ref=rows(*block),
                send_sem=send_sems.at[k], recv_sem=recv_sems.at[k], device_id=to, device_id_type=MESH)

        mine = pltpu.make_async_copy(x_ref, rows(*me), local_sem)
        mine.start()
        first = [copy(0, me, sibling, src=x_ref)]
        first += [copy(1 + j, me, (*chip, c), src=x_ref) for j, chip in enumerate(chips)]
        for cp in first:
            cp.start()
        passed = [copy(4 + j, (*chip, c), sibling) for j, chip in enumerate(chips)]
        for j, chip in enumerate(chips):
            copy(1 + j, (*chip, c), me).wait_recv()
            passed[j].start()
        copy(0, sibling, me).wait_recv()
        for j, chip in enumerate(chips):
            copy(4 + j, (*chip, 1 - c), me).wait_recv()
        for cp in first + passed:
            cp.wait_send()
        mine.wait()

    return pl.pallas_call(
        body, name=name,
        out_shape=jax.ShapeDtypeStruct((N_DEV * m_per, n), v.dtype),
        in_specs=[pl.BlockSpec(memory_space=pltpu.VMEM)],
        out_specs=pl.BlockSpec(memory_space=pltpu.VMEM),
        scratch_shapes=[pltpu.SemaphoreType.DMA((7,)), pltpu.SemaphoreType.DMA((7,)), pltpu.SemaphoreType.DMA],
        compiler_params=_cparams(),
    )(v)


class _Exchange:
    def __init__(self, local, sends, arrivals):
        self.local, self.sends, self.arrivals = local, sends, arrivals

    def start(self):
        for cp in self.local + self.sends:
            cp.start()

    def wait(self):
        for cp in self.arrivals:
            cp.wait_recv()
        for cp in self.sends:
            cp.wait_send()
        for cp in self.local:
            cp.wait()


def _exchange_sems(n):
    return [pltpu.SemaphoreType.DMA((3 * n,)), pltpu.SemaphoreType.DMA((3 * n,)), pltpu.SemaphoreType.DMA((n,))]


def _gather_exchange(ins, outs, send_sems, recv_sems, local_sems):
    x, y, c = lax.axis_index("x"), lax.axis_index("y"), lax.axis_index("c")
    slot = 2 * x + y
    chips = [(1 - x, y), (x, 1 - y), (1 - x, 1 - y)]
    local, sends, arrivals = [], [], []
    for a in range(len(ins)):
        local.append(pltpu.make_async_copy(ins[a], outs[a].at[slot], local_sems.at[a]))
        for j, (px, py) in enumerate(chips):
            sems = dict(send_sem=send_sems.at[3 * a + j], recv_sem=recv_sems.at[3 * a + j],
                        device_id=(px, py, c), device_id_type=MESH)
            sends.append(pltpu.make_async_remote_copy(src_ref=ins[a], dst_ref=outs[a].at[slot], **sems))
            arrivals.append(pltpu.make_async_remote_copy(src_ref=ins[a], dst_ref=outs[a].at[2 * px + py], **sems))
    return _Exchange(local, sends, arrivals)


def _scatter_exchange(p_refs, out_refs, send_sems, recv_sems, local_sems):
    x, y, c = lax.axis_index("x"), lax.axis_index("y"), lax.axis_index("c")
    slot = 2 * x + y
    chips = [(1 - x, y), (x, 1 - y), (1 - x, 1 - y)]
    local, sends, arrivals = [], [], []
    for a, (p_ref, out_ref) in enumerate(zip(p_refs, out_refs)):
        local.append(pltpu.make_async_copy(p_ref.at[slot], out_ref.at[slot], local_sems.at[a]))
        for j, (px, py) in enumerate(chips):
            sems = dict(send_sem=send_sems.at[3 * a + j], recv_sem=recv_sems.at[3 * a + j],
                        device_id=(px, py, c), device_id_type=MESH)
            sends.append(pltpu.make_async_remote_copy(src_ref=p_ref.at[2 * px + py], dst_ref=out_ref.at[slot], **sems))
            arrivals.append(pltpu.make_async_remote_copy(src_ref=p_ref.at[slot], dst_ref=out_ref.at[2 * px + py], **sems))
    return _Exchange(local, sends, arrivals)


def _chip_allgather(arrs, name):
    n = len(arrs)

    def body(*refs):
        ex = _gather_exchange(refs[:n], refs[n:2 * n], *refs[2 * n:])
        ex.start()
        ex.wait()

    hbm = pl.BlockSpec(memory_space=pl.ANY)
    return pl.pallas_call(
        body, name=name,
        out_shape=[jax.ShapeDtypeStruct((N_CHIPS,) + a.shape, a.dtype) for a in arrs],
        in_specs=[hbm] * n, out_specs=[hbm] * n,
        scratch_shapes=_exchange_sems(n),
        compiler_params=_cparams(),
    )(*arrs)


def _sibling_halves(give_refs, got_refs, send_sems, recv_sems):
    x, y, c = lax.axis_index("x"), lax.axis_index("y"), lax.axis_index("c")
    copies = []
    for a, (give_ref, got_ref) in enumerate(zip(give_refs, got_refs)):
        for s in range(N_CHIPS):
            copies.append(pltpu.make_async_remote_copy(
                src_ref=give_ref.at[s, 1 - c], dst_ref=got_ref.at[s], send_sem=send_sems.at[N_CHIPS * a + s],
                recv_sem=recv_sems.at[N_CHIPS * a + s], device_id=(x, y, 1 - c), device_id_type=MESH))
    return copies


def _halves_shapes(arrs):
    return [jax.ShapeDtypeStruct((a.shape[0],) + a.shape[2:], a.dtype) for a in arrs]


def _halves_sems(n):
    return [pltpu.SemaphoreType.DMA((N_CHIPS * n,)), pltpu.SemaphoreType.DMA((N_CHIPS * n,))]


def _halves_swap(arrs, name):
    n = len(arrs)

    def body(*refs):
        copies = _sibling_halves(refs[:n], refs[n:2 * n], *refs[2 * n:])
        for cp in copies:
            cp.start()
        for cp in copies:
            cp.wait()

    hbm = pl.BlockSpec(memory_space=pl.ANY)
    return pl.pallas_call(body, name=name, out_shape=_halves_shapes(arrs), in_specs=[hbm] * n, out_specs=[hbm] * n,
                          scratch_shapes=_halves_sems(n), compiler_params=_cparams())(*arrs)


def _sibling_send(arrs, name):
    n = len(arrs)

    def body(*refs):
        x, y, c = lax.axis_index("x"), lax.axis_index("y"), lax.axis_index("c")
        send_sems, recv_sems = refs[2 * n:]
        copies = [pltpu.make_async_remote_copy(src_ref=refs[a], dst_ref=refs[n + a], send_sem=send_sems.at[a],
                                               recv_sem=recv_sems.at[a], device_id=(x, y, 1 - c), device_id_type=MESH)
                  for a in range(n)]
        for cp in copies:
            cp.start()
        for cp in copies:
            cp.wait()

    hbm = pl.BlockSpec(memory_space=pl.ANY)
    return pl.pallas_call(
        body, name=name, out_shape=[jax.ShapeDtypeStruct(a.shape, a.dtype) for a in arrs],
        in_specs=[hbm] * n, out_specs=[hbm] * n,
        scratch_shapes=[pltpu.SemaphoreType.DMA((n,)), pltpu.SemaphoreType.DMA((n,))],
        compiler_params=_cparams(),
    )(*arrs)


def _chip_scatter(ps, name):
    n = len(ps)

    def body(*refs):
        ex = _scatter_exchange(refs[:n], refs[n:2 * n], *refs[2 * n:])
        ex.start()
        ex.wait()

    hbm = pl.BlockSpec(memory_space=pl.ANY)
    return pl.pallas_call(
        body, name=name, out_shape=[jax.ShapeDtypeStruct(p.shape, p.dtype) for p in ps],
        in_specs=[hbm] * n, out_specs=[hbm] * n,
        scratch_shapes=_exchange_sems(n),
        compiler_params=_cparams(),
    )(*ps)


def _row_tile(h):
    return h // 2 if (h // 2) % 8 == 0 else h


def _chip_sum(arr, got, core, name):
    _, _, h, cols = arr.shape
    tr = _row_tile(h)

    def body(core_ref, a_ref, b_ref, o_ref):
        o_ref[...] = a_ref[...] + b_ref[...]

    slab = pl.BlockSpec((None, tr, cols), lambda s, i, core_ref: (s, i, 0))
    grid_spec = pltpu.PrefetchScalarGridSpec(
        num_scalar_prefetch=1, grid=(N_CHIPS, h // tr),
        in_specs=[pl.BlockSpec((None, None, tr, cols), lambda s, i, core_ref: (s, core_ref[0], i, 0)), slab],
        out_specs=slab)
    return pl.pallas_call(body, name=name, grid_spec=grid_spec, out_shape=jax.ShapeDtypeStruct(got.shape, got.dtype),
                          compiler_params=_cparams())(core, arr, got)


def _sum4(p, name):
    _, h, cols = p.shape
    tr = _row_tile(h)

    def body(p_ref, o_ref):
        o_ref[...] = ((p_ref[0] + p_ref[1]) + p_ref[2]) + p_ref[3]

    return pl.pallas_call(
        body, name=name, grid=(h // tr,), out_shape=jax.ShapeDtypeStruct((h, cols), p.dtype),
        in_specs=[pl.BlockSpec((4, tr, cols), lambda i: (0, i, 0))],
        out_specs=pl.BlockSpec((tr, cols), lambda i: (i, 0)), compiler_params=_cparams())(p)


def _adam_math(w, g, m, v):
    m2 = ADAM_B1 * m + (1.0 - ADAM_B1) * g
    v2 = ADAM_B2 * v + (1.0 - ADAM_B2) * (g * g)
    m_hat = m2 / (1.0 - ADAM_B1 ** ADAM_STEP)
    v_hat = v2 / (1.0 - ADAM_B2 ** ADAM_STEP)
    delta = -ADAM_LR * (m_hat / (jnp.sqrt(v_hat) + ADAM_EPS) + ADAM_WD * w)
    return delta, m2, v2


def _adamw(w, g, m, v, name):
    rows, cols = w.shape
    tr = rows // 4 if rows % 32 == 0 else rows

    def body(w_ref, g_ref, m_ref, v_ref, d_ref, m2_ref, v2_ref):
        delta, m2, v2 = _adam_math(w_ref[...], g_ref[...], m_ref[...], v_ref[...])
        d_ref[...] = delta
        m2_ref[...] = m2
        v2_ref[...] = v2

    spec = pl.BlockSpec((tr, cols), lambda i: (i, 0))
    shp = jax.ShapeDtypeStruct(w.shape, F32)
    return pl.pallas_call(body, name=name, grid=(rows // tr,), out_shape=[shp, shp, shp],
                          in_specs=[spec] * 4, out_specs=[spec] * 3, compiler_params=_cparams())(w, g, m, v)


def _small_update(g8, offsets, ws, ms, vs, loss_at, name):
    k = len(ws)

    def summed(g8_ref, lo, width):
        g = g8_ref[0:1, lo:lo + width]
        for r in range(1, N_DEV):
            g = g + g8_ref[r:r + 1, lo:lo + width]
        return g

    def body(g8_ref, *refs):
        ins, outs = refs[:3 * k], refs[3 * k:]
        for j in range(k):
            g = summed(g8_ref, offsets[j], ws[j].shape[1])
            delta, m2, v2 = _adam_math(ins[j][...], g, ins[k + j][...], ins[2 * k + j][...])
            for kind, val in enumerate((g, delta, m2, v2)):
                outs[kind * k + j][...] = val
        outs[4 * k][...] = summed(g8_ref, loss_at, LANES)

    vm = pl.BlockSpec(memory_space=pltpu.VMEM)
    shapes = [jax.ShapeDtypeStruct(w.shape, F32) for w in ws] * 4 + [jax.ShapeDtypeStruct((1, LANES), F32)]
    res = pl.pallas_call(body, name=name, out_shape=shapes, in_specs=[vm] * (1 + 3 * k), out_specs=[vm] * (4 * k + 1),
                         compiler_params=_cparams())(g8, *ws, *ms, *vs)
    return res[:k], res[k:2 * k], res[2 * k:3 * k], res[3 * k:4 * k], res[4 * k]


def _mod_shard(c8, w_ada, b_ada_shard, name):
    n = w_ada.shape[1]
    tn = 512

    def body(c_ref, w_ref, b_ref, o_ref, s_ref):
        cv = c_ref[...]
        sc = cv * (1.0 / (1.0 + jnp.exp(-cv)))
        s_ref[...] = sc
        o_ref[...] = _dot(sc.astype(BF16), w_ref[...].astype(BF16)) + b_ref[...]

    return pl.pallas_call(
        body, name=name, grid=(n // tn,),
        out_shape=[jax.ShapeDtypeStruct((8, n), F32), jax.ShapeDtypeStruct((8, D), F32)],
        in_specs=[pl.BlockSpec((8, D), lambda j: (0, 0)), pl.BlockSpec((D, tn), lambda j: (0, j)),
                  pl.BlockSpec((1, tn), lambda j: (0, j))],
        out_specs=[pl.BlockSpec((8, tn), lambda j: (0, j)), pl.BlockSpec((8, D), lambda j: (0, 0))],
        compiler_params=_cparams())(c8, w_ada, b_ada_shard)


def _layer_norm_stats(u):
    mu = jnp.mean(u, axis=1, keepdims=True)
    d = u - mu
    var = jnp.mean(d * d, axis=1, keepdims=True)
    rstd = lax.rsqrt(var + LN_EPS)
    return d * rstd, rstd


def _in_proj(x, vec, w_in, b_in, name):
    s = x.shape[0]
    tb = min(TOK_TILE, s)

    def body(x_ref, vec_ref, w_ref, b_ref, ht_ref, p_ref):
        h = x_ref[...] * (1.0 + vec_ref[V_SC_A:V_SC_A + 1, :]) + vec_ref[V_SH_A:V_SH_A + 1, :]
        hb = h.astype(BF16)
        ht_ref[...] = h.T.astype(BF16)
        proj = _dot(hb, w_ref[...]) + b_ref[...]
        col = lax.broadcasted_iota(jnp.int32, (1, D_IN), 1)
        is_q = (col < SB_W) | ((col >= 3 * SB_W) & (col < 3 * SB_W + SWA_QW))
        p_ref[...] = (proj * jnp.where(is_q, QK_SCALE, 1.0)).astype(BF16)

    return pl.pallas_call(
        body, name=name, grid=(s // tb,),
        out_shape=[jax.ShapeDtypeStruct((D, s), BF16), jax.ShapeDtypeStruct((s, D_IN), BF16)],
        in_specs=[pl.BlockSpec((tb, D), lambda i: (i, 0)), _resident((VEC_ROWS, D)), _resident((D, D_IN)),
                  _resident((1, D_IN))],
        out_specs=[pl.BlockSpec((D, tb), lambda i: (0, i)), pl.BlockSpec((tb, D_IN), lambda i: (i, 0))],
        compiler_params=_cparams())(x, vec, w_in, b_in)


def _softplus_parts(z):
    e1 = jnp.exp(-jnp.abs(z))
    sp = jnp.maximum(z, 0.0) + jnp.log(1.0 + e1)
    return sp, e1


def _sb_forward(proj, shards, name):
    s = proj.shape[0]
    tq, tk = min(SB_TQ, s), min(SB_TK, s)
    r = tq // tk

    n_sh = len(shards)
    nkb = SB_W // LANES
    nq = s // tq

    assert r % SB_UNROLL == 0, "the sweep below the diagonal takes whole steps"

    def body(q_ref, k_ref, v_ref, *refs):
        sh_refs, (o_ref, tot_ref, start_ref), got_refs = refs[:n_sh], refs[n_sh:n_sh + 3], refs[n_sh + 3:2 * n_sh + 3]
        acc_refs, run_refs = refs[2 * n_sh + 3:2 * n_sh + 5]
        i = pl.program_id(1)
        step = pl.program_id(0) * nq + i
        gather = _gather_exchange(sh_refs, got_refs, *refs[2 * n_sh + 5:])

        @pl.when(step == 0)
        def _():
            gather.start()

        lane = lax.broadcasted_iota(jnp.int32, (1, LANES), 1)
        first = lane < HEAD_DIM
        qp = q_ref[...]
        zero = jnp.zeros((), BF16)
        qs = (jnp.where(first, qp, zero), jnp.where(first, zero, qp))
        later = _sum_matrix(tk, lambda row, col: row > col)
        acc_refs[...] = jnp.zeros_like(acc_refs)
        run_refs[...] = jnp.zeros_like(run_refs)

        def blocks(tiles):
            rows = [slice(r0, r0 + n) for r0, n, _, _ in tiles]
            kjs = [k_ref[pl.ds(pl.multiple_of(j * tk, tk), tk), :] for _, _, j, _ in tiles]
            vjs = [v_ref[pl.ds(pl.multiple_of(j * tk, tk), tk), :] for _, _, j, _ in tiles]
            chains = [(hd, t) for t in range(len(tiles)) for hd in range(2)]
            zs = [_dot_nt(qs[hd][rows[t]], kjs[t]) for hd, t in chains]
            sps = [_softplus_parts(z)[0] for z in zs]
            befores = [_before(i * tq + r0, n, j * tk, tk) if diag else None for r0, n, j, diag in tiles]
            spms = [sp if befores[t] is None else jnp.where(befores[t], sp, 0.0) for (hd, t), sp in zip(chains, sps)]
            cums = [_block_sums(spm, later) for spm in spms]
            runs, ws = {}, []
            for (hd, t), z, sp, (cum, sm) in zip(chains, zs, sps, cums):
                key = (hd, tiles[t][0])
                if key not in runs:
                    runs[key] = run_refs[hd, rows[t], :]
                w = jnp.exp(z - sp - cum - _across(runs[key], tk))
                if befores[t] is not None:
                    w = jnp.where(befores[t], w, 0.0)
                ws.append(w.astype(BF16))
                runs[key] = runs[key] + sm
            sums = {}
            for (hd, t), w in zip(chains, ws):
                key, pv = (hd, tiles[t][0]), _dot(w, vjs[t])
                sums[key] = pv if key not in sums else sums[key] + pv
            for (hd, r0), run in runs.items():
                span = slice(r0, r0 + run.shape[0])
                acc_refs[hd, span, :] += sums[(hd, r0)]
                run_refs[hd, span, :] = run

        blocks([(d * tk, tk, i * r + e, e == d) for d in range(r) for e in range(d, -1, -1)])

        below = i * r

        def swept_mass():
            return jnp.min(jnp.minimum(run_refs[0], run_refs[1]))

        def more(carry):
            n, mass = carry
            return (n < below // SB_UNROLL) & (mass < SB_DEAD_MASS)

        def sweep(carry):
            n, _ = carry
            top = below - 1 - SB_UNROLL * n
            blocks([(0, tq, top - u, False) for u in range(SB_UNROLL)])
            return n + 1, swept_mass()

        n_swept, _ = lax.while_loop(more, sweep, (0, swept_mass()))
        start_ref[pl.program_id(0), i] = (below - SB_UNROLL * n_swept).astype(F32)
        o_ref[...] = jnp.where(first, acc_refs[0], acc_refs[1])
        tot_ref[...] = jnp.where(first, run_refs[0], run_refs[1])

        @pl.when(step == nkb * nq - 1)
        def _():
            gather.wait()

    shp = jax.ShapeDtypeStruct((s, SB_W), F32)
    qspec = pl.BlockSpec((tq, LANES), lambda p, i: (i, p))
    hbm = pl.BlockSpec(memory_space=pl.ANY)
    return pl.pallas_call(
        body, name=name, grid=(nkb, nq),
        out_shape=[shp, shp, jax.ShapeDtypeStruct((nkb, nq), F32)]
        + [jax.ShapeDtypeStruct((N_CHIPS,) + a.shape, a.dtype) for a in shards],
        in_specs=[qspec,
                  pl.BlockSpec((s, LANES), lambda p, i: (0, nkb + p)),
                  pl.BlockSpec((s, LANES), lambda p, i: (0, 2 * nkb + p))] + [hbm] * n_sh,
        out_specs=[qspec, qspec, pl.BlockSpec(memory_space=pltpu.SMEM)] + [hbm] * n_sh,
        scratch_shapes=[pltpu.VMEM((2, tq, LANES), F32), pltpu.VMEM((2, tq, LANES), F32)] + _exchange_sems(n_sh),
        compiler_params=_cparams())(proj, proj, proj, *shards)


def _swa_masks(n):
    ti = lax.broadcasted_iota(jnp.int32, (WINDOW, 2 * WINDOW), 0)
    kj = lax.broadcasted_iota(jnp.int32, (WINDOW, 2 * WINDOW), 1)
    dist = ti + WINDOW - kj
    valid = (dist >= 0) & (dist < WINDOW) & ((n * WINDOW - WINDOW + kj) >= 0)
    return valid, dist.astype(F32)


def _swa_probs(sc, valid, distf, h, sink):
    slope = 2.0 ** (-(h + 1))
    sc = jnp.where(valid, sc - slope * distf, MASK_VALUE)
    mx = jnp.maximum(jnp.max(sc, axis=1, keepdims=True), sink)
    p = jnp.exp(sc - mx)
    es = jnp.exp(sink - mx)
    inv = 1.0 / (jnp.sum(p, axis=1, keepdims=True) + es)
    return p * inv, es * inv


def _swa_forward(proj, sinks, shards, name):
    s = proj.shape[0]
    nb = s // WINDOW
    qb, kb, vb = 3 * SB_W // SWA_QW, (3 * SB_W + SWA_QW) // LANES, (3 * SB_W + SWA_QW + SWA_KW) // LANES
    n_sh = len(shards)

    def body(q_ref, kp_ref, kc_ref, vp_ref, vc_ref, sink_ref, *refs):
        sh_refs, o_ref, got_refs = refs[:n_sh], refs[n_sh], refs[n_sh + 1:2 * n_sh + 1]
        n = pl.program_id(0)
        gather = _gather_exchange(sh_refs, got_refs, *refs[2 * n_sh + 1:])

        @pl.when(n == 0)
        def _():
            gather.start()

        k = jnp.concatenate([kp_ref[...], kc_ref[...]], axis=0)
        v = jnp.concatenate([vp_ref[...], vc_ref[...]], axis=0)
        k_sw = pltpu.roll(k.astype(F32), HEAD_DIM, 1).astype(BF16)
        v_sw = pltpu.roll(v.astype(F32), HEAD_DIM, 1).astype(BF16)
        lane = lax.broadcasted_iota(jnp.int32, (1, LANES), 1)
        halves = [lane < HEAD_DIM, lane >= HEAD_DIM]
        valid, distf = _swa_masks(n)
        heads = range(2 * 4)
        qms = [jnp.where(halves[h % 2], q_ref[:, (h // 2) * LANES:(h // 2 + 1) * LANES], jnp.zeros((), BF16))
               for h in heads]
        kus = [k if h // 4 == h % 2 else k_sw for h in heads]
        vus = [v if h // 4 == h % 2 else v_sw for h in heads]
        scores = [_dot_nt(qms[h], kus[h]) for h in heads]
        ps = [_swa_probs(scores[h], valid, distf, h, sink_ref[h])[0].astype(BF16) for h in heads]
        outs = [_dot(ps[h], vus[h]) for h in heads]
        for pair in range(4):
            o_ref[:, pair * LANES:(pair + 1) * LANES] = jnp.where(halves[0], outs[2 * pair], outs[2 * pair + 1])

        @pl.when(n == nb - 1)
        def _():
            gather.wait()

    prev = lambda n: jnp.maximum(n - 1, 0)
    hbm = pl.BlockSpec(memory_space=pl.ANY)
    return pl.pallas_call(
        body, name=name, grid=(nb,),
        out_shape=[jax.ShapeDtypeStruct((s, SWA_QW), F32)]
        + [jax.ShapeDtypeStruct((N_CHIPS,) + a.shape, a.dtype) for a in shards],
        in_specs=[pl.BlockSpec((WINDOW, SWA_QW), lambda n: (n, qb)),
                  pl.BlockSpec((WINDOW, LANES), lambda n: (prev(n), kb)),
                  pl.BlockSpec((WINDOW, LANES), lambda n: (n, kb)),
                  pl.BlockSpec((WINDOW, LANES), lambda n: (prev(n), vb)),
                  pl.BlockSpec((WINDOW, LANES), lambda n: (n, vb)),
                  pl.BlockSpec(memory_space=pltpu.SMEM)] + [hbm] * n_sh,
        out_specs=[pl.BlockSpec((WINDOW, SWA_QW), lambda n: (n, 0))] + [hbm] * n_sh,
        scratch_shapes=_exchange_sems(n_sh),
        compiler_params=_cparams())(proj, proj, proj, proj, proj, sinks, *shards)


def _rms_parts(y):
    return lax.rsqrt(jnp.mean(y * y, axis=1, keepdims=True) + RMS_EPS)


def _post_attention(y_sb, y_sw, x, vec, w_out, name):
    s = x.shape[0]
    tb = min(TOK_TILE, s)

    def body(ysb_ref, ysw_ref, x_ref, vec_ref, w_ref, mixedt_ref, attn_ref, x1_ref, h2_ref, h2t_ref):
        ysb, ysw = ysb_ref[...], ysw_ref[...]
        nsb_f = ysb * _rms_parts(ysb) * vec_ref[V_GN:V_GN + 1, :SB_W]
        nsw_f = ysw * _rms_parts(ysw) * vec_ref[V_GN:V_GN + 1, SB_W:]
        nsb, nsw = nsb_f.astype(BF16), nsw_f.astype(BF16)
        mixedt_ref[:SB_W, :] = nsb_f.T.astype(BF16)
        mixedt_ref[SB_W:, :] = nsw_f.T.astype(BF16)
        attn = _dot(nsb, w_ref[:SB_W, :]) + _dot(nsw, w_ref[SB_W:, :])
        attn_ref[...] = attn
        u1 = ALPHA * x_ref[...] + (1.0 + vec_ref[V_G_A:V_G_A + 1, :]) * attn
        xhat, _ = _layer_norm_stats(u1)
        x1 = xhat * vec_ref[V_LN1G:V_LN1G + 1, :] + vec_ref[V_LN1B:V_LN1B + 1, :]
        x1_ref[...] = x1
        h2 = x1 * (1.0 + vec_ref[V_SC_F:V_SC_F + 1, :]) + vec_ref[V_SH_F:V_SH_F + 1, :]
        h2_ref[...] = h2.astype(BF16)
        h2t_ref[...] = h2.T.astype(BF16)

    half = pl.BlockSpec((tb, SB_W), lambda i: (i, 0))
    full = pl.BlockSpec((tb, D), lambda i: (i, 0))
    full_t = pl.BlockSpec((D, tb), lambda i: (0, i))
    return pl.pallas_call(
        body, name=name, grid=(s // tb,),
        out_shape=[jax.ShapeDtypeStruct((D, s), BF16), jax.ShapeDtypeStruct((s, D), F32),
                   jax.ShapeDtypeStruct((s, D), F32), jax.ShapeDtypeStruct((s, D), BF16),
                   jax.ShapeDtypeStruct((D, s), BF16)],
        in_specs=[half, half, full, _resident((VEC_ROWS, D)), _resident((D, D))],
        out_specs=[full_t, full, full, full, full_t],
        compiler_params=_cparams())(y_sb, y_sw, x, vec, w_out)


def _ffn_forward(h2, w_gu, w_down, name):
    s = h2.shape[0]
    tb = min(FFN_TILE, s)

    def body(h_ref, wgu_ref, wd_ref, gu_ref, actt_ref, ffn_ref):
        gu = _dot(h_ref[...], wgu_ref[...])
        gu_ref[...] = gu.astype(BF16)
        gate, up = gu[:, :D_FF], gu[:, D_FF:]
        act = gate * (1.0 / (1.0 + jnp.exp(-gate))) * up
        actt_ref[...] = act.T.astype(BF16)
        ffn_ref[...] = _dot(act.astype(BF16), wd_ref[...])

    return pl.pallas_call(
        body, name=name, grid=(s // tb,),
        out_shape=[jax.ShapeDtypeStruct((s, 2 * D_FF), BF16), jax.ShapeDtypeStruct((D_FF, s), BF16),
                   jax.ShapeDtypeStruct((s, D), F32)],
        in_specs=[pl.BlockSpec((tb, D), lambda i: (i, 0)), _resident((D, 2 * D_FF)), _resident((D_FF, D))],
        out_specs=[pl.BlockSpec((tb, 2 * D_FF), lambda i: (i, 0)), pl.BlockSpec((D_FF, tb), lambda i: (0, i)),
                   pl.BlockSpec((tb, D), lambda i: (i, 0))],
        compiler_params=_cparams())(h2, w_gu, w_down)


def _layer_norm_bwd(dxhat, xhat, rstd):
    m1 = jnp.mean(dxhat, axis=1, keepdims=True)
    m2 = jnp.mean(dxhat * xhat, axis=1, keepdims=True)
    return rstd * (dxhat - m1 - xhat * m2)


def _colsum(a):
    return jnp.sum(a, axis=0, keepdims=True)


A_LN2G, A_LN2B, A_GF, A_SCF, A_SHF, A_LOSS = range(6)
B_LN1G, B_LN1B, B_GA, B_GN = range(4)
C_SCA, C_SHA = range(2)


def _ffn_backward(x1, ffn, target, gu, vec, w_gu, w_down, name):
    s = x1.shape[0]
    tb = min(FFN_BWD_TILE, s)

    def body(x1_ref, ffn_ref, t_ref, gu_ref, vec_ref, wgu_ref, wd_ref, dffn_ref, dgu_ref, dx1_ref, acc_ref):
        @pl.when(pl.program_id(0) == 0)
        def _():
            acc_ref[...] = jnp.zeros_like(acc_ref)

        x1v, ffn_v = x1_ref[...], ffn_ref[...]
        g_f = 1.0 + vec_ref[V_G_F:V_G_F + 1, :]
        u2 = ALPHA * x1v + g_f * ffn_v
        xhat, rstd = _layer_norm_stats(u2)
        ln_g = vec_ref[V_LN2G:V_LN2G + 1, :]
        err = xhat * ln_g + vec_ref[V_LN2B:V_LN2B + 1, :] - t_ref[...]
        dx2 = err * (1.0 / D)
        acc_ref[A_LOSS:A_LOSS + 1, :] += _colsum(err * err) * (0.5 / D)
        acc_ref[A_LN2G:A_LN2G + 1, :] += _colsum(dx2 * xhat)
        acc_ref[A_LN2B:A_LN2B + 1, :] += _colsum(dx2)
        du2 = _layer_norm_bwd(dx2 * ln_g, xhat, rstd)
        acc_ref[A_GF:A_GF + 1, :] += _colsum(du2 * ffn_v)
        dffn = (g_f * du2).astype(BF16)
        dffn_ref[...] = dffn
        dact = _dot_nt(dffn, wd_ref[...])
        gate, up = gu_ref[:, :D_FF].astype(F32), gu_ref[:, D_FF:].astype(F32)
        sg = 1.0 / (1.0 + jnp.exp(-gate))
        dgate = (dact * up * (sg * (1.0 + gate * (1.0 - sg)))).astype(BF16)
        dup = (dact * (gate * sg)).astype(BF16)
        dgu_ref[:, :D_FF] = dgate
        dgu_ref[:, D_FF:] = dup
        dh2 = _dot_nt(dgate, wgu_ref[:, :D_FF]) + _dot_nt(dup, wgu_ref[:, D_FF:])
        dx1_ref[...] = ALPHA * du2 + dh2 * (1.0 + vec_ref[V_SC_F:V_SC_F + 1, :])
        acc_ref[A_SCF:A_SCF + 1, :] += _colsum(dh2 * x1v)
        acc_ref[A_SHF:A_SHF + 1, :] += _colsum(dh2)

    full = pl.BlockSpec((tb, D), lambda i: (i, 0))
    wide = pl.BlockSpec((tb, 2 * D_FF), lambda i: (i, 0))
    return pl.pallas_call(
        body, name=name, grid=(s // tb,),
        out_shape=[jax.ShapeDtypeStruct((s, D), BF16), jax.ShapeDtypeStruct((s, 2 * D_FF), BF16),
                   jax.ShapeDtypeStruct((s, D), F32), jax.ShapeDtypeStruct((8, D), F32)],
        in_specs=[full, full, full, wide, _resident((VEC_ROWS, D)), _resident((D, 2 * D_FF)), _resident((D_FF, D))],
        out_specs=[full, wide, full, pl.BlockSpec((8, D), lambda i: (0, 0))],
        compiler_params=_cparams())(x1, ffn, target, gu, vec, w_gu, w_down)


def _attn_out_backward(dx1, x, attn, y_sb, y_sw, vec, w_out, name):
    s = x.shape[0]
    tb = min(TOK_TILE, s)

    def body(dx1_ref, x_ref, attn_ref, ysb_ref, ysw_ref, vec_ref, w_ref, du1_ref, dattn_ref, dy_ref, acc_ref):
        @pl.when(pl.program_id(0) == 0)
        def _():
            acc_ref[...] = jnp.zeros_like(acc_ref)

        attn = attn_ref[...]
        g_a = 1.0 + vec_ref[V_G_A:V_G_A + 1, :]
        xhat, rstd = _layer_norm_stats(ALPHA * x_ref[...] + g_a * attn)
        dx1v = dx1_ref[...]
        acc_ref[B_LN1G:B_LN1G + 1, :] += _colsum(dx1v * xhat)
        acc_ref[B_LN1B:B_LN1B + 1, :] += _colsum(dx1v)
        du1 = _layer_norm_bwd(dx1v * vec_ref[V_LN1G:V_LN1G + 1, :], xhat, rstd)
        du1_ref[...] = du1
        acc_ref[B_GA:B_GA + 1, :] += _colsum(du1 * attn)
        dattn = (g_a * du1).astype(BF16)
        dattn_ref[...] = dattn
        dmixed = _dot_nt(dattn, w_ref[...])
        for lo, y_ref in ((0, ysb_ref), (SB_W, ysw_ref)):
            y = y_ref[...]
            rr = _rms_parts(y)
            dn = dmixed[:, lo:lo + SB_W]
            acc_ref[B_GN:B_GN + 1, lo:lo + SB_W] += _colsum(dn * y * rr)
            dng = dn * vec_ref[V_GN:V_GN + 1, lo:lo + SB_W]
            dy_ref[:, lo:lo + SB_W] = rr * dng - y * (rr * rr * rr) * jnp.mean(dng * y, axis=1, keepdims=True)

    half = pl.BlockSpec((tb, SB_W), lambda i: (i, 0))
    full = pl.BlockSpec((tb, D), lambda i: (i, 0))
    return pl.pallas_call(
        body, name=name, grid=(s // tb,),
        out_shape=[jax.ShapeDtypeStruct((s, D), F32), jax.ShapeDtypeStruct((s, D), BF16),
                   jax.ShapeDtypeStruct((s, D), F32), jax.ShapeDtypeStruct((8, D), F32)],
        in_specs=[full, full, full, half, half, _resident((VEC_ROWS, D)), _resident((D, D))],
        out_specs=[full, full, full, pl.BlockSpec((8, D), lambda i: (0, 0))],
        compiler_params=_cparams())(dx1, x, attn, y_sb, y_sw, vec, w_out)


def _sb_backward(proj, sp_total, sweep_start, dy, slabs, name):
    s = proj.shape[0]
    tq, tk = min(SB_TQ, s), min(SB_TK, s)
    r = tq // tk
    nkb = SB_W // LANES
    nq = s // tq

    assert r % SB_UNROLL == 0, "the sweep below the diagonal takes whole steps"

    n_sl = len(slabs)

    def body(q_ref, k_ref, v_ref, tot_ref, do_ref, start_ref, *refs):
        slab_refs, (dq_ref, dk_ref, dv_ref), got_refs = refs[:n_sl], refs[n_sl:n_sl + 3], refs[n_sl + 3:2 * n_sl + 3]
        dq_acc, left_refs, gsum_refs = refs[2 * n_sl + 3:2 * n_sl + 6]
        i = pl.program_id(1)
        step = pl.program_id(0) * nq + i
        scatter = _scatter_exchange(slab_refs, got_refs, *refs[2 * n_sl + 6:])

        @pl.when(step == 0)
        def _():
            scatter.start()

        @pl.when(i == 0)
        def _():
            dk_ref[...] = jnp.zeros_like(dk_ref)
            dv_ref[...] = jnp.zeros_like(dv_ref)

        lane = lax.broadcasted_iota(jnp.int32, (1, LANES), 1)
        first = lane < HEAD_DIM
        qp, dop, totp = q_ref[...], do_ref[...], tot_ref[...]
        zero = jnp.zeros((), BF16)
        qs = (jnp.where(first, qp, zero), jnp.where(first, zero, qp))
        dofs = (jnp.where(first, dop, 0.0), jnp.where(first, 0.0, dop))
        dobs = tuple(d.astype(BF16) for d in dofs)
        dots = tuple(d.T.astype(BF16) for d in dofs)
        qts = tuple(qh.astype(F32).T.astype(BF16) for qh in qs)
        later = _sum_matrix(tk, lambda row, col: row > col)
        earlier = _sum_matrix(tk, lambda row, col: row < col)
        dq_acc[...] = jnp.zeros_like(dq_acc)
        gsum_refs[...] = jnp.zeros_like(gsum_refs)
        swapped = pltpu.roll(totp, HEAD_DIM, 1)
        left_refs[0] = jnp.where(first, totp, swapped)
        left_refs[1] = jnp.where(first, swapped, totp)

        def blocks(tiles):
            rows = [slice(r0, r0 + n) for r0, n, _, _ in tiles]
            kjs = [k_ref[pl.ds(pl.multiple_of(j * tk, tk), tk), :] for _, _, j, _ in tiles]
            vjs = [v_ref[pl.ds(pl.multiple_of(j * tk, tk), tk), :] for _, _, j, _ in tiles]
            chains = [(hd, t) for t in range(len(tiles)) for hd in range(2)]
            keys = [(hd, tiles[t][0]) for hd, t in chains]
            zs = [_dot_nt(qs[hd][rows[t]], kjs[t]) for hd, t in chains]
            dws = [_dot_nt(dobs[hd][rows[t]], vjs[t]) for hd, t in chains]
            parts = [_softplus_parts(z) for z in zs]
            sps = [p[0] for p in parts]
            befores = [_before(i * tq + r0, n, j * tk, tk) if diag else None for r0, n, j, diag in tiles]
            spms = [sp if befores[t] is None else jnp.where(befores[t], sp, 0.0) for (hd, t), sp in zip(chains, sps)]
            cums = [_block_sums(spm, later) for spm in spms]
            lefts, ws = {}, []
            for key, (hd, t), z, sp, (cum, sm) in zip(keys, chains, zs, sps, cums):
                if key not in lefts:
                    lefts[key] = left_refs[hd, rows[t], :]
                lefts[key] = lefts[key] - sm
                w = jnp.exp(z - sp - cum - _across(lefts[key], tk))
                ws.append(w if befores[t] is None else jnp.where(befores[t], w, 0.0))
            wbs = [w.astype(BF16) for w in ws]
            dvs = [_dot(dots[hd][:, rows[t]], wb) for (hd, t), wb in zip(chains, wbs)]
            gs = [dw * w for dw, w in zip(dws, ws)]
            gcums = [_block_sums(g, earlier) for g in gs]
            gsums, dzbs = {}, []
            for key, (hd, t), z, (sp, e1), g, (gcum, gsm) in zip(keys, chains, zs, parts, gs, gcums):
                if key not in gsums:
                    gsums[key] = gsum_refs[hd, rows[t], :]
                inv = 1.0 / (1.0 + e1)
                sig = jnp.where(z >= 0.0, inv, e1 * inv)
                dz = g - sig * (g + _across(gsums[key], tk) + gcum)
                dzbs.append((dz if befores[t] is None else jnp.where(befores[t], dz, 0.0)).astype(BF16))
                gsums[key] = gsums[key] + gsm
            dqs = [_dot(dzb, kjs[t]) for (hd, t), dzb in zip(chains, dzbs)]
            dks = [_dot(qts[hd][:, rows[t]], dzb) for (hd, t), dzb in zip(chains, dzbs)]
            for t, (_, _, j, _) in enumerate(tiles):
                dv_ref[j] += dvs[2 * t] + dvs[2 * t + 1]
                dk_ref[j] += dks[2 * t] + dks[2 * t + 1]
            totals = {}
            for key, dq in zip(keys, dqs):
                totals[key] = dq if key not in totals else totals[key] + dq
            for (hd, r0), tot in totals.items():
                span = slice(r0, r0 + tot.shape[0])
                dq_acc[hd, span, :] += tot
                left_refs[hd, span, :] = lefts[(hd, r0)]
                gsum_refs[hd, span, :] = gsums[(hd, r0)]

        below = i * r
        start = jnp.clip(start_ref[pl.program_id(0), i].astype(jnp.int32), 0, below) // SB_UNROLL * SB_UNROLL

        def sweep(n, carry):
            blocks([(0, tq, start + SB_UNROLL * n + u, False) for u in range(SB_UNROLL)])
            return carry

        lax.fori_loop(0, (below - start) // SB_UNROLL, sweep, 0)
        blocks([(d * tk, tk, below + e, e == d) for d in range(r) for e in range(d + 1)])
        dq_ref[...] = jnp.where(first, dq_acc[0], dq_acc[1])

        @pl.when(step == nkb * nq - 1)
        def _():
            scatter.wait()

    shp = jax.ShapeDtypeStruct((s, SB_W), F32)
    qspec = pl.BlockSpec((tq, LANES), lambda p, i: (i, p))
    whole = pl.BlockSpec((None, s // tk, LANES, tk), lambda p, i: (p, 0, 0, 0))
    shp_t = jax.ShapeDtypeStruct((nkb, s // tk, LANES, tk), F32)
    hbm = pl.BlockSpec(memory_space=pl.ANY)
    return pl.pallas_call(
        body, name=name, grid=(nkb, nq),
        out_shape=[shp, shp_t, shp_t] + [jax.ShapeDtypeStruct(p.shape, p.dtype) for p in slabs],
        in_specs=[qspec,
                  pl.BlockSpec((s, LANES), lambda p, i: (0, nkb + p)),
                  pl.BlockSpec((s, LANES), lambda p, i: (0, 2 * nkb + p)),
                  qspec, qspec, pl.BlockSpec(memory_space=pltpu.SMEM)] + [hbm] * n_sl,
        out_specs=[qspec, whole, whole] + [hbm] * n_sl,
        scratch_shapes=[pltpu.VMEM((2, tq, LANES), F32), pltpu.VMEM((2, tq, LANES), F32), pltpu.VMEM((2, tq, LANES), F32)]
        + _exchange_sems(n_sl),
        compiler_params=_cparams())(proj, proj, proj, sp_total, dy, sweep_start, *slabs)


def _swa_backward(proj, y_sw, dy, sinks, gives, name):
    s = proj.shape[0]
    nb = s // WINDOW
    qb, kb, vb = 3 * SB_W // SWA_QW, (3 * SB_W + SWA_QW) // LANES, (3 * SB_W + SWA_QW + SWA_KW) // LANES

    n_gv = len(gives)

    def body(q_ref, kp_ref, kc_ref, vp_ref, vc_ref, o_ref, do_ref, sink_ref, *refs):
        give_refs, (dq_ref, dk_ref, dv_ref, ds_ref), got_refs = refs[:n_gv], refs[n_gv:n_gv + 4], refs[n_gv + 4:2 * n_gv + 4]
        n = pl.program_id(0)
        swap = _sibling_halves(give_refs, got_refs, *refs[2 * n_gv + 4:])

        @pl.when(n == 0)
        def _():
            for cp in swap:
                cp.start()

        @pl.when(n == 0)
        def _():
            dk_ref[...] = jnp.zeros_like(dk_ref)
            dv_ref[...] = jnp.zeros_like(dv_ref)
            ds_ref[...] = jnp.zeros_like(ds_ref)

        k = jnp.concatenate([kp_ref[...], kc_ref[...]], axis=0)
        v = jnp.concatenate([vp_ref[...], vc_ref[...]], axis=0)
        k_sw = pltpu.roll(k.astype(F32), HEAD_DIM, 1).astype(BF16)
        v_sw = pltpu.roll(v.astype(F32), HEAD_DIM, 1).astype(BF16)
        lane = lax.broadcasted_iota(jnp.int32, (1, LANES), 1)
        halves = [lane < HEAD_DIM, lane >= HEAD_DIM]
        valid, distf = _swa_masks(n)
        heads = range(2 * 4)
        cols = [slice((h // 2) * LANES, (h // 2 + 1) * LANES) for h in heads]
        qms = [jnp.where(halves[h % 2], q_ref[:, cols[h]], jnp.zeros((), BF16)) for h in heads]
        dos = [jnp.where(halves[h % 2], do_ref[:, cols[h]], 0.0) for h in heads]
        dobs = [d.astype(BF16) for d in dos]
        native = [h // 4 == h % 2 for h in heads]
        kus = [k if native[h] else k_sw for h in heads]
        vus = [v if native[h] else v_sw for h in heads]
        scores = [_dot_nt(qms[h], kus[h]) for h in heads]
        dps = [_dot_nt(dobs[h], vus[h]) for h in heads]
        deltas = [jnp.sum(dos[h] * o_ref[:, cols[h]], axis=1, keepdims=True) for h in heads]
        probs = [_swa_probs(scores[h], valid, distf, h, sink_ref[h]) for h in heads]
        pbs = [probs[h][0].astype(BF16) for h in heads]
        dscs = [(probs[h][0] * (dps[h] - deltas[h])).astype(BF16) for h in heads]
        dqs = [_dot(dscs[h], kus[h]) for h in heads]
        dks = [_dot_tn(dscs[h], qms[h]) for h in heads]
        dvs = [_dot_tn(pbs[h], dobs[h]) for h in heads]
        for h in heads:
            ds_ref[h:h + 1, :] += jnp.zeros((1, LANES), F32) - jnp.sum(probs[h][1] * deltas[h])
        for pair in range(4):
            dq_ref[:, cols[2 * pair]] = jnp.where(halves[0], dqs[2 * pair], dqs[2 * pair + 1])

        def gathered(parts):
            nat = sum(parts[h] for h in heads if native[h])
            rot = sum(parts[h] for h in heads if not native[h])
            return nat + pltpu.roll(rot, HEAD_DIM, 1)

        dk, dv = gathered(dks), gathered(dvs)
        prev = pl.multiple_of(jnp.maximum(n - 1, 0) * WINDOW, WINDOW)
        cur = pl.multiple_of(n * WINDOW, WINDOW)
        dk_ref[pl.ds(prev, WINDOW), :] += dk[:WINDOW]
        dv_ref[pl.ds(prev, WINDOW), :] += dv[:WINDOW]
        dk_ref[pl.ds(cur, WINDOW), :] += dk[WINDOW:]
        dv_ref[pl.ds(cur, WINDOW), :] += dv[WINDOW:]

        @pl.when(n == nb - 1)
        def _():
            for cp in swap:
                cp.wait()

    prev_blk = lambda n: jnp.maximum(n - 1, 0)
    wide = pl.BlockSpec((WINDOW, SWA_QW), lambda n: (n, 0))
    whole = pl.BlockSpec((s, LANES), lambda n: (0, 0))
    hbm = pl.BlockSpec(memory_space=pl.ANY)
    return pl.pallas_call(
        body, name=name, grid=(nb,),
        out_shape=[jax.ShapeDtypeStruct((s, SWA_QW), F32), jax.ShapeDtypeStruct((s, LANES), F32),
                   jax.ShapeDtypeStruct((s, LANES), F32), jax.ShapeDtypeStruct((8, LANES), F32)] + _halves_shapes(gives),
        in_specs=[pl.BlockSpec((WINDOW, SWA_QW), lambda n: (n, qb)),
                  pl.BlockSpec((WINDOW, LANES), lambda n: (prev_blk(n), kb)),
                  pl.BlockSpec((WINDOW, LANES), lambda n: (n, kb)),
                  pl.BlockSpec((WINDOW, LANES), lambda n: (prev_blk(n), vb)),
                  pl.BlockSpec((WINDOW, LANES), lambda n: (n, vb)),
                  wide,
                  pl.BlockSpec((WINDOW, SWA_QW), lambda n: (n, 1)),
                  pl.BlockSpec(memory_space=pltpu.SMEM)] + [hbm] * n_gv,
        out_specs=[wide, whole, whole, pl.BlockSpec((8, LANES), lambda n: (0, 0))] + [hbm] * n_gv,
        scratch_shapes=_halves_sems(n_gv),
        compiler_params=_cparams())(proj, proj, proj, proj, proj, y_sw, dy, sinks, *gives)


def _in_proj_backward(dq_sb, dkt_sb, dvt_sb, dq_sw, dk_sw, dv_sw, du1, x, vec, w_in, name):
    s = x.shape[0]
    tb = min(TOK_TILE, s)
    n_pairs, _, _, tk = dkt_sb.shape

    def body(dqsb_ref, dktsb_ref, dvtsb_ref, dqsw_ref, dksw_ref, dvsw_ref, du1_ref, x_ref, vec_ref, w_ref,
             dproj_ref, gx_ref, acc_ref, bacc_ref):
        @pl.when(pl.program_id(0) == 0)
        def _():
            acc_ref[...] = jnp.zeros_like(acc_ref)
            bacc_ref[...] = jnp.zeros_like(bacc_ref)

        pieces = ((0, dqsb_ref, QK_SCALE), (3 * SB_W, dqsw_ref, QK_SCALE), (3 * SB_W + SWA_QW, dksw_ref, 1.0),
                  (3 * SB_W + SWA_QW + SWA_KW, dvsw_ref, 1.0))
        for lo, ref, scale in pieces:
            width = ref.shape[1]
            piece = ref[...] * scale
            bacc_ref[0:1, lo:lo + width] += _colsum(piece)
            dproj_ref[:, lo:lo + width] = piece.astype(BF16)
        for base, ref in ((SB_W, dktsb_ref), (2 * SB_W, dvtsb_ref)):
            for p in range(n_pairs):
                lo = base + p * LANES
                for jj in range(tb // tk):
                    piece = ref[p, jj].T
                    bacc_ref[0:1, lo:lo + LANES] += _colsum(piece)
                    dproj_ref[jj * tk:(jj + 1) * tk, lo:lo + LANES] = piece.astype(BF16)
        dh = _dot_nt(dproj_ref[...], w_ref[...])
        xv = x_ref[...]
        gx_ref[...] = ALPHA * du1_ref[...] + dh * (1.0 + vec_ref[V_SC_A:V_SC_A + 1, :])
        acc_ref[C_SCA:C_SCA + 1, :] += _colsum(dh * xv)
        acc_ref[C_SHA:C_SHA + 1, :] += _colsum(dh)

    half = pl.BlockSpec((tb, SB_W), lambda i: (i, 0))
    narrow = pl.BlockSpec((tb, LANES), lambda i: (i, 0))
    full = pl.BlockSpec((tb, D), lambda i: (i, 0))
    blocks_t = pl.BlockSpec((n_pairs, tb // tk, LANES, tk), lambda i: (0, i, 0, 0))
    return pl.pallas_call(
        body, name=name, grid=(s // tb,),
        out_shape=[jax.ShapeDtypeStruct((s, D_IN), BF16), jax.ShapeDtypeStruct((s, D), F32),
                   jax.ShapeDtypeStruct((8, D), F32), jax.ShapeDtypeStruct((8, D_IN), F32)],
        in_specs=[half, blocks_t, blocks_t, half, narrow, narrow, full, full, _resident((VEC_ROWS, D)),
                  _resident((D, D_IN))],
        out_specs=[pl.BlockSpec((tb, D_IN), lambda i: (i, 0)), full, pl.BlockSpec((8, D), lambda i: (0, 0)),
                   pl.BlockSpec((8, D_IN), lambda i: (0, 0))],
        compiler_params=_cparams())(dq_sb, dkt_sb, dvt_sb, dq_sw, dk_sw, dv_sw, du1, x, vec, w_in)


def _weight_grad(at, b, name, col_shards=1):
    m, s = at.shape
    n = b.shape[1]
    if col_shards > 1:
        tn = n // col_shards
        out_shape = jax.ShapeDtypeStruct((col_shards, m, tn), F32)
        out_spec = pl.BlockSpec((None, m, tn), lambda j, k: (j, 0, 0))
    else:
        tn = 512 if n % 512 == 0 else n
        out_shape = jax.ShapeDtypeStruct((m, n), F32)
        out_spec = pl.BlockSpec((m, tn), lambda j, k: (0, j))
    ts = min(512, s)

    def body(at_ref, b_ref, o_ref):
        @pl.when(pl.program_id(1) == 0)
        def _():
            o_ref[...] = jnp.zeros_like(o_ref)

        o_ref[...] += _dot(at_ref[...], b_ref[...])

    return pl.pallas_call(
        body, name=name, grid=(n // tn, s // ts),
        out_shape=out_shape,
        in_specs=[pl.BlockSpec((m, ts), lambda j, k: (0, k)), pl.BlockSpec((ts, tn), lambda j, k: (k, j))],
        out_specs=out_spec,
        compiler_params=_cparams())(at, b)


def _pad_rows(v, rows):
    return jnp.concatenate([v, jnp.zeros((rows - v.shape[0], v.shape[1]), v.dtype)], axis=0)


def _col_shards(w, n_shards):
    r, n = w.shape
    return w.reshape(r, n_shards, n // n_shards).transpose(1, 0, 2)


def kernel(x, c, w_ada, b_ada, w_in, b_in, sinks, gn_sb, gn_swa, w_out, ln1_g, ln1_b, w_gu, w_down, ln2_g, ln2_b, loss_target, m_w_ada, m_b_ada, m_w_in, m_b_in, m_sinks, m_gn_sb, m_gn_swa, m_w_out, m_ln1_g, m_ln1_b, m_w_gu, m_w_down, m_ln2_g, m_ln2_b, v_w_ada, v_b_ada, v_w_in, v_b_in, v_sinks, v_gn_sb, v_gn_swa, v_w_out, v_ln1_g, v_ln1_b, v_w_gu, v_w_down, v_ln2_g, v_ln2_b):
    ix, iy, ic = lax.axis_index("x"), lax.axis_index("y"), lax.axis_index("c")
    chip = 2 * ix + iy
    dev = 4 * ix + 2 * iy + ic
    xs, target = x[0], loss_target[0]
    s = xs.shape[0]

    c_all = _allgather8(_pad_rows(c, 8), "gather_c")[::8]
    n_ada = w_ada.shape[2]
    b_ada_shard = lax.dynamic_slice_in_dim(b_ada, chip * n_ada, n_ada, axis=1)
    mod_cols, silu_c = _mod_shard(c_all, w_ada[0], b_ada_shard, "mod_shard")
    mod_all = _allgather8(mod_cols, "gather_mod").reshape(N_DEV, 8, n_ada)
    mod_mine = lax.dynamic_index_in_dim(mod_all, dev, axis=1, keepdims=False)
    mod = mod_mine.reshape(N_CHIPS, 2, n_ada)[:, 0].reshape(6, D)
    vec = jnp.concatenate([mod, ln1_g, ln1_b, ln2_g, ln2_b, jnp.concatenate([gn_sb, gn_swa], axis=1),
                           jnp.zeros((VEC_ROWS - 11, D), F32)], axis=0)

    (g_in,) = _chip_allgather([w_in[0].astype(BF16)], "gather_w_in")
    w_in_b = g_in.transpose(1, 0, 2).reshape(D, D_IN)

    h_t, proj = _in_proj(xs, vec, w_in_b, b_in, "in_proj")
    y_sb, sp_total, sweep_start, g_out, g_gu = _sb_forward(
        proj, [w_out[0].astype(BF16), w_gu[0].astype(BF16)], "sb_forward")
    w_gu_b = g_gu.transpose(1, 0, 2).reshape(D, 2 * D_FF)
    w_out_b = g_out.reshape(D, D)
    sink_vec = sinks[0]
    y_sw, g_down = _swa_forward(proj, sink_vec, [w_down[0].astype(BF16)], "swa_forward")
    w_down_b = g_down.reshape(D_FF, D)
    mixed_t, attn, x1, h2_b, h2_t = _post_attention(y_sb, y_sw, xs, vec, w_out_b, "post_attention")
    gu, act_t, ffn = _ffn_forward(h2_b, w_gu_b, w_down_b, "ffn_forward")

    def in_halves(shards):
        n_sh, rows, cols = shards.shape
        return shards.reshape(n_sh, 2, rows // 2, cols)

    core = ic.reshape(1).astype(jnp.int32)
    dffn_b, dgu_b, dx1, acc_f = _ffn_backward(x1, ffn, target, gu, vec, w_gu_b, w_down_b, "ffn_backward")
    dw_gu = _weight_grad(h2_t, dgu_b, "grad_w_gu", col_shards=4)
    dw_down = _weight_grad(act_t, dffn_b, "grad_w_down")
    du1, dattn_b, dy, acc_a = _attn_out_backward(dx1, xs, attn, y_sb, y_sw, vec, w_out_b, "attn_out_backward")
    dw_out = _weight_grad(mixed_t, dattn_b, "grad_w_out")
    first = [in_halves(dw_gu), in_halves(dw_down.reshape(4, D_FF // 4, D)), in_halves(dw_out.reshape(4, D // 4, D))]
    dq_sw, dk_sw, dv_sw, dsink, *got_first = _swa_backward(proj, y_sw, dy, sink_vec, first, "swa_backward")
    sums_first = [_chip_sum(arr, got, core, "grad_chip_sum_" + nm)
                  for arr, got, nm in zip(first, got_first, ("gu", "down", "out"))]
    dq_sb, dk_sb, dv_sb, *parts_first = _sb_backward(proj, sp_total, sweep_start, dy, sums_first, "sb_backward")
    dproj_b, grad_x, acc_i, acc_b = _in_proj_backward(dq_sb, dk_sb, dv_sb, dq_sw, dk_sw, dv_sw, du1, xs, vec, w_in_b,
                                                      "in_proj_backward")
    dw_in = _weight_grad(h_t, dproj_b, "grad_w_in")
    last = [in_halves(_col_shards(dw_in, 4))]
    sums_last = [_chip_sum(last[0], _halves_swap(last, "grad_halves_swap_in")[0], core, "grad_chip_sum_in")]
    parts_last = _chip_scatter(sums_last, "grad_chip_scatter_in")
    mine = [_sum4(p, "grad_reduce_" + nm) for p, nm in zip([*parts_first, *parts_last], ("gu", "down", "out", "in"))]
    theirs = _sibling_send(mine, "grad_half_return")
    gw_gu, gw_down, gw_out, gw_in = [
        jnp.concatenate([jnp.where(ic == 0, m_, t_), jnp.where(ic == 0, t_, m_)], axis=0) for m_, t_ in zip(mine, theirs)]

    dmod = jnp.concatenate([acc_i[C_SHA:C_SHA + 1], acc_i[C_SCA:C_SCA + 1], acc_a[B_GA:B_GA + 1],
                            acc_f[A_SHF:A_SHF + 1], acc_f[A_SCF:A_SCF + 1], acc_f[A_GF:A_GF + 1]], axis=1)
    dsink_row = jnp.concatenate([dsink[:, 0].reshape(1, 8), jnp.zeros((1, LANES - 8), F32)], axis=1)
    loss_row = jnp.concatenate([jnp.sum(acc_f[A_LOSS:A_LOSS + 1], axis=1, keepdims=True),
                                jnp.zeros((1, LANES - 1), F32)], axis=1)
    small = jnp.concatenate([dmod, acc_b[0:1], acc_a[B_LN1G:B_LN1G + 1], acc_a[B_LN1B:B_LN1B + 1],
                             acc_f[A_LN2G:A_LN2G + 1], acc_f[A_LN2B:A_LN2B + 1], acc_a[B_GN:B_GN + 1],
                             dsink_row, loss_row], axis=1)
    small_all = _allgather8(_pad_rows(small, 8), "gather_small")[::8]

    small_names = ["b_ada", "b_in", "ln1_g", "ln1_b", "ln2_g", "ln2_b", "gn_sb", "gn_swa", "sinks"]
    small_at = [SM_MOD, SM_BIN, SM_LN1G, SM_LN1B, SM_LN2G, SM_LN2B, SM_GN, SM_GN + SB_W, SM_SINK]
    *small_out, loss_row_all = _small_update(
        small_all, small_at,
        [b_ada, b_in, ln1_g, ln1_b, ln2_g, ln2_b, gn_sb, gn_swa, sinks],
        [m_b_ada, m_b_in, m_ln1_g, m_ln1_b, m_ln2_g, m_ln2_b, m_gn_sb, m_gn_swa, m_sinks],
        [v_b_ada, v_b_in, v_ln1_g, v_ln1_b, v_ln2_g, v_ln2_b, v_gn_sb, v_gn_swa, v_sinks], SM_LOSS, "small_update")
    g_small, d_small, m2_small, v2_small = [dict(zip(small_names, leaves)) for leaves in small_out]
    loss = loss_row_all[0, 0]

    dmod_cols = lax.dynamic_slice_in_dim(small_all[:, SM_MOD:SM_BIN], chip * n_ada, n_ada, axis=1)
    gw_ada = _weight_grad(_pad_rows(silu_c, LANES).astype(BF16).T, _pad_rows(dmod_cols, LANES).astype(BF16), "grad_w_ada")

    big = {}
    for nm, w, g, m, v in (("w_ada", w_ada, gw_ada, m_w_ada, v_w_ada), ("w_in", w_in, gw_in, m_w_in, v_w_in),
                           ("w_out", w_out, gw_out, m_w_out, v_w_out), ("w_gu", w_gu, gw_gu, m_w_gu, v_w_gu),
                           ("w_down", w_down, gw_down, m_w_down, v_w_down)):
        d_, m2_, v2_ = _adamw(w[0], g, m[0], v[0], "adamw_" + nm)
        big[nm] = (g[None], d_[None], m2_[None], v2_[None])

    order = ["w_ada", "b_ada", "w_in", "b_in", "sinks", "gn_sb", "gn_swa", "w_out", "ln1_g", "ln1_b", "w_gu", "w_down",
             "ln2_g", "ln2_b"]

    def leaf(nm, which):
        if nm in big:
            return big[nm][which]
        return (g_small, d_small, m2_small, v2_small)[which][nm]

    outs = [loss, grad_x[None]]
    for which in range(4):
        outs += [leaf(nm, which) for nm in order]
    return tuple(outs)
```

```python
import functools
import math

import jax
import jax.numpy as jnp
from jax import lax
from jax.experimental import pallas as pl
from jax.experimental.pallas import tpu as pltpu

F32 = jnp.float32
BF16 = jnp.bfloat16

D = 1024
HEAD_DIM = 64
SB_W = 512
SWA_QW = 512
SWA_KW = 128
D_IN = 2304
D_FF = 2816
WINDOW = 128
ALPHA = 2.0 ** 0.25
LN_EPS = 1e-5
RMS_EPS = 1e-6
MASK_VALUE = -1e30
QK_SCALE = 1.0 / math.sqrt(HEAD_DIM)

ADAM_LR = 0.001
ADAM_B1 = 0.9
ADAM_B2 = 0.999
ADAM_EPS = 1e-08
ADAM_WD = 0.01
ADAM_STEP = 10

N_CHIPS = 4
N_DEV = 8
LANES = 128

SB_TQ = 512
SB_TK = 256
SB_UNROLL = 1
SB_DEAD_MASS = 110.0
TOK_TILE = 512
FFN_TILE = 256
FFN_BWD_TILE = 256
VMEM_LIMIT = 56 * 1024 * 1024

V_SH_A, V_SC_A, V_G_A, V_SH_F, V_SC_F, V_G_F, V_LN1G, V_LN1B, V_LN2G, V_LN2B, V_GN = range(11)
VEC_ROWS = 16

SM_MOD = 0
SM_BIN = 6 * D
SM_LN1G = SM_BIN + D_IN
SM_LN1B = SM_LN1G + D
SM_LN2G = SM_LN1B + D
SM_LN2B = SM_LN2G + D
SM_GN = SM_LN2B + D
SM_SINK = SM_GN + D
SM_LOSS = SM_SINK + LANES
SM_LEN = SM_LOSS + LANES

MESH = pl.DeviceIdType.MESH


def _cparams(**kw):
    return pltpu.CompilerParams(vmem_limit_bytes=VMEM_LIMIT, **kw)


def _resident(shape):
    nd = len(shape)
    return pl.BlockSpec(shape, lambda *_: (0,) * nd, pipeline_mode=pl.Buffered(1))


def _dot(a, b):
    return jnp.dot(a, b, preferred_element_type=F32)


def _dot_nt(a, b):
    return lax.dot_general(a, b, (((1,), (1,)), ((), ())), preferred_element_type=F32)


def _dot_tn(a, b):
    return lax.dot_general(a, b, (((0,), (0,)), ((), ())), preferred_element_type=F32)


def _sum_matrix(tk, keep):
    row = lax.broadcasted_iota(jnp.int32, (tk, tk + LANES), 0)
    col = lax.broadcasted_iota(jnp.int32, (tk, tk + LANES), 1)
    return (keep(row, col) | (col >= tk)).astype(BF16)


def _block_sums(x, m):
    tk = x.shape[1]
    res = _dot(x.astype(BF16), m)
    return res[:, :tk], res[:, tk:]


def _before(t0, n, s0, tk):
    return s0 + lax.broadcasted_iota(jnp.int32, (n, tk), 1) < t0 + lax.broadcasted_iota(jnp.int32, (n, tk), 0)


def _across(v, tk):
    return jnp.concatenate([v] * (tk // LANES), axis=1)


def _allgather8(v, name, gather=(), scatter=()):
    m_per, n = v.shape
    n_g, n_s = len(gather), len(scatter)

    def body(x_ref, *refs):
        g_in, s_in = refs[:n_g], refs[n_g:n_g + n_s]
        out_ref = refs[n_g + n_s]
        g_out, s_out = refs[n_g + n_s + 1:2 * n_g + n_s + 1], refs[2 * n_g + n_s + 1:2 * (n_g + n_s) + 1]
        send_sems, recv_sems, local_sem, *more_sems = refs[2 * (n_g + n_s) + 1:]
        beside = ([_gather_exchange(g_in, g_out, *more_sems[:3])] if n_g else []) + (
            [_scatter_exchange(s_in, s_out, *more_sems[-3:])] if n_s else [])
        for ex in beside:
            ex.start()
        x, y, c = lax.axis_index("x"), lax.axis_index("y"), lax.axis_index("c")
        me, sibling = (x, y, c), (x, y, 1 - c)
        chips = [(1 - x, y), (x, 1 - y), (1 - x, 1 - y)]

        def rows(px, py, pc):
            return out_ref.at[pl.ds((4 * px + 2 * py + pc) * m_per, m_per), :]

        def copy(k, block, to, src=None):
            return pltpu.make_async_remote_copy(
                src_ref=rows(*block) if src is None else src, dst_ref=rows(*block),
                send_sem=send_sems.at[k], recv_sem=recv_sems.at[k], device_id=to, device_id_type=MESH)

        mine = pltpu.make_async_copy(x_ref, rows(*me), local_sem)
        mine.start()
        first = [copy(0, me, sibling, src=x_ref)]
        first += [copy(1 + j, me, (*chip, c), src=x_ref) for j, chip in enumerate(chips)]
        for cp in first:
            cp.start()
        passed = [copy(4 + j, (*chip, c), sibling) for j, chip in enumerate(chips)]
        for j, chip in enumerate(chips):
            copy(1 + j, (*chip, c), me).wait_recv()
            passed[j].start()
        copy(0, sibling, me).wait_recv()
        for j, chip in enumerate(chips):
            copy(4 + j, (*chip, 1 - c), me).wait_recv()
        for cp in first + passed:
            cp.wait_send()
        mine.wait()
        for ex in beside:
            ex.wait()

    hbm = pl.BlockSpec(memory_space=pl.ANY)
    return pl.pallas_call(
        body, name=name,
        out_shape=[jax.ShapeDtypeStruct((N_DEV * m_per, n), v.dtype)]
        + [jax.ShapeDtypeStruct((N_CHIPS,) + a.shape, a.dtype) for a in gather]
        + [jax.ShapeDtypeStruct(p.shape, p.dtype) for p in scatter],
        in_specs=[pl.BlockSpec(memory_space=pltpu.VMEM)] + [hbm] * (n_g + n_s),
        out_specs=[pl.BlockSpec(memory_space=pltpu.VMEM)] + [hbm] * (n_g + n_s),
        scratch_shapes=[pltpu.SemaphoreType.DMA((7,)), pltpu.SemaphoreType.DMA((7,)), pltpu.SemaphoreType.DMA]
        + (_exchange_sems(n_g) if n_g else []) + (_exchange_sems(n_s) if n_s else []),
        compiler_params=_cparams(),
    )(v, *gather, *scatter)


class _Exchange:
    def __init__(self, local, sends, arrivals):
        self.local, self.sends, self.arrivals = local, sends, arrivals

    def start(self):
        for cp in self.local + self.sends:
            cp.start()

    def wait(self):
        for cp in self.arrivals:
            cp.wait_recv()
        for cp in self.sends:
            cp.wait_send()
        for cp in self.local:
            cp.wait()


def _exchange_sems(n):
    return [pltpu.SemaphoreType.DMA((3 * n,)), pltpu.SemaphoreType.DMA((3 * n,)), pltpu.SemaphoreType.DMA((n,))]


def _gather_exchange(ins, outs, send_sems, recv_sems, local_sems):
    x, y, c = lax.axis_index("x"), lax.axis_index("y"), lax.axis_index("c")
    slot = 2 * x + y
    chips = [(1 - x, y), (x, 1 - y), (1 - x, 1 - y)]
    local, sends, arrivals = [], [], []
    for a in range(len(ins)):
        local.append(pltpu.make_async_copy(ins[a], outs[a].at[slot], local_sems.at[a]))
        for j, (px, py) in enumerate(chips):
            sems = dict(send_sem=send_sems.at[3 * a + j], recv_sem=recv_sems.at[3 * a + j],
                        device_id=(px, py, c), device_id_type=MESH)
            sends.append(pltpu.make_async_remote_copy(src_ref=ins[a], dst_ref=outs[a].at[slot], **sems))
            arrivals.append(pltpu.make_async_remote_copy(src_ref=ins[a], dst_ref=outs[a].at[2 * px + py], **sems))
    return _Exchange(local, sends, arrivals)


def _scatter_exchange(p_refs, out_refs, send_sems, recv_sems, local_sems):
    x, y, c = lax.axis_index("x"), lax.axis_index("y"), lax.axis_index("c")
    slot = 2 * x + y
    chips = [(1 - x, y), (x, 1 - y), (1 - x, 1 - y)]
    local, sends, arrivals = [], [], []
    for a, (p_ref, out_ref) in enumerate(zip(p_refs, out_refs)):
        local.append(pltpu.make_async_copy(p_ref.at[slot], out_ref.at[slot], local_sems.at[a]))
        for j, (px, py) in enumerate(chips):
            sems = dict(send_sem=send_sems.at[3 * a + j], recv_sem=recv_sems.at[3 * a + j],
                        device_id=(px, py, c), device_id_type=MESH)
            sends.append(pltpu.make_async_remote_copy(src_ref=p_ref.at[2 * px + py], dst_ref=out_ref.at[slot], **sems))
            arrivals.append(pltpu.make_async_remote_copy(src_ref=p_ref.at[slot], dst_ref=out_ref.at[2 * px + py], **sems))
    return _Exchange(local, sends, arrivals)


def _sibling_halves(give_refs, got_refs, send_sems, recv_sems):
    x, y, c = lax.axis_index("x"), lax.axis_index("y"), lax.axis_index("c")
    copies = []
    for a, (give_ref, got_ref) in enumerate(zip(give_refs, got_refs)):
        for s in range(N_CHIPS):
            copies.append(pltpu.make_async_remote_copy(
                src_ref=give_ref.at[s, 1 - c], dst_ref=got_ref.at[s], send_sem=send_sems.at[N_CHIPS * a + s],
                recv_sem=recv_sems.at[N_CHIPS * a + s], device_id=(x, y, 1 - c), device_id_type=MESH))
    return copies


def _halves_shapes(arrs):
    return [jax.ShapeDtypeStruct((a.shape[0],) + a.shape[2:], a.dtype) for a in arrs]


def _halves_sems(n):
    return [pltpu.SemaphoreType.DMA((N_CHIPS * n,)), pltpu.SemaphoreType.DMA((N_CHIPS * n,))]


def _halves_swap(arrs, name):
    n = len(arrs)

    def body(*refs):
        copies = _sibling_halves(refs[:n], refs[n:2 * n], *refs[2 * n:])
        for cp in copies:
            cp.start()
        for cp in copies:
            cp.wait()

    hbm = pl.BlockSpec(memory_space=pl.ANY)
    return pl.pallas_call(body, name=name, out_shape=_halves_shapes(arrs), in_specs=[hbm] * n, out_specs=[hbm] * n,
                          scratch_shapes=_halves_sems(n), compiler_params=_cparams())(*arrs)


def _sibling_send(arrs, name):
    n = len(arrs)

    def body(*refs):
        x, y, c = lax.axis_index("x"), lax.axis_index("y"), lax.axis_index("c")
        send_sems, recv_sems = refs[2 * n:]
        copies = [pltpu.make_async_remote_copy(src_ref=refs[a], dst_ref=refs[n + a], send_sem=send_sems.at[a],
                                               recv_sem=recv_sems.at[a], device_id=(x, y, 1 - c), device_id_type=MESH)
                  for a in range(n)]
        for cp in copies:
            cp.start()
        for cp in copies:
            cp.wait()

    hbm = pl.BlockSpec(memory_space=pl.ANY)
    return pl.pallas_call(
        body, name=name, out_shape=[jax.ShapeDtypeStruct(a.shape, a.dtype) for a in arrs],
        in_specs=[hbm] * n, out_specs=[hbm] * n,
        scratch_shapes=[pltpu.SemaphoreType.DMA((n,)), pltpu.SemaphoreType.DMA((n,))],
        compiler_params=_cparams(),
    )(*arrs)


def _row_tile(h):
    return h // 2 if (h // 2) % 8 == 0 else h


def _chip_sum(arr, got, core, name):
    _, _, h, cols = arr.shape
    tr = _row_tile(h)

    def body(core_ref, a_ref, b_ref, o_ref):
        o_ref[...] = a_ref[...] + b_ref[...]

    slab = pl.BlockSpec((None, tr, cols), lambda s, i, core_ref: (s, i, 0))
    grid_spec = pltpu.PrefetchScalarGridSpec(
        num_scalar_prefetch=1, grid=(N_CHIPS, h // tr),
        in_specs=[pl.BlockSpec((None, None, tr, cols), lambda s, i, core_ref: (s, core_ref[0], i, 0)), slab],
        out_specs=slab)
    return pl.pallas_call(body, name=name, grid_spec=grid_spec, out_shape=jax.ShapeDtypeStruct(got.shape, got.dtype),
                          compiler_params=_cparams())(core, arr, got)


def _sum4(p, name):
    _, h, cols = p.shape
    tr = _row_tile(h)

    def body(p_ref, o_ref):
        o_ref[...] = ((p_ref[0] + p_ref[1]) + p_ref[2]) + p_ref[3]

    return pl.pallas_call(
        body, name=name, grid=(h // tr,), out_shape=jax.ShapeDtypeStruct((h, cols), p.dtype),
        in_specs=[pl.BlockSpec((4, tr, cols), lambda i: (0, i, 0))],
        out_specs=pl.BlockSpec((tr, cols), lambda i: (i, 0)), compiler_params=_cparams())(p)


def _adam_math(w, g, m, v):
    m2 = ADAM_B1 * m + (1.0 - ADAM_B1) * g
    v2 = ADAM_B2 * v + (1.0 - ADAM_B2) * (g * g)
    m_hat = m2 / (1.0 - ADAM_B1 ** ADAM_STEP)
    v_hat = v2 / (1.0 - ADAM_B2 ** ADAM_STEP)
    delta = -ADAM_LR * (m_hat / (jnp.sqrt(v_hat) + ADAM_EPS) + ADAM_WD * w)
    return delta, m2, v2


def _adamw(w, g, m, v, name):
    rows, cols = w.shape
    tr = rows // 4 if rows % 32 == 0 else rows

    def body(w_ref, g_ref, m_ref, v_ref, d_ref, m2_ref, v2_ref):
        delta, m2, v2 = _adam_math(w_ref[...], g_ref[...], m_ref[...], v_ref[...])
        d_ref[...] = delta
        m2_ref[...] = m2
        v2_ref[...] = v2

    spec = pl.BlockSpec((tr, cols), lambda i: (i, 0))
    shp = jax.ShapeDtypeStruct(w.shape, F32)
    return pl.pallas_call(body, name=name, grid=(rows // tr,), out_shape=[shp, shp, shp],
                          in_specs=[spec] * 4, out_specs=[spec] * 3, compiler_params=_cparams())(w, g, m, v)


def _small_update(g8, offsets, ws, ms, vs, loss_at, name):
    k = len(ws)

    def summed(g8_ref, lo, width):
        g = g8_ref[0:1, lo:lo + width]
        for r in range(1, N_DEV):
            g = g + g8_ref[r:r + 1, lo:lo + width]
        return g

    def body(g8_ref, *refs):
        ins, outs = refs[:3 * k], refs[3 * k:]
        for j in range(k):
            g = summed(g8_ref, offsets[j], ws[j].shape[1])
            delta, m2, v2 = _adam_math(ins[j][...], g, ins[k + j][...], ins[2 * k + j][...])
            for kind, val in enumerate((g, delta, m2, v2)):
                outs[kind * k + j][...] = val
        outs[4 * k][...] = summed(g8_ref, loss_at, LANES)

    vm = pl.BlockSpec(memory_space=pltpu.VMEM)
    shapes = [jax.ShapeDtypeStruct(w.shape, F32) for w in ws] * 4 + [jax.ShapeDtypeStruct((1, LANES), F32)]
    res = pl.pallas_call(body, name=name, out_shape=shapes, in_specs=[vm] * (1 + 3 * k), out_specs=[vm] * (4 * k + 1),
                         compiler_params=_cparams())(g8, *ws, *ms, *vs)
    return res[:k], res[k:2 * k], res[2 * k:3 * k], res[3 * k:4 * k], res[4 * k]


def _mod_shard(c8, w_ada, b_ada_shard, name):
    n = w_ada.shape[1]
    tn = 512

    def body(c_ref, w_ref, b_ref, o_ref, s_ref):
        cv = c_ref[...]
        sc = cv * (1.0 / (1.0 + jnp.exp(-cv)))
        s_ref[...] = sc
        o_ref[...] = _dot(sc.astype(BF16), w_ref[...].astype(BF16)) + b_ref[...]

    return pl.pallas_call(
        body, name=name, grid=(n // tn,),
        out_shape=[jax.ShapeDtypeStruct((8, n), F32), jax.ShapeDtypeStruct((8, D), F32)],
        in_specs=[pl.BlockSpec((8, D), lambda j: (0, 0)), pl.BlockSpec((D, tn), lambda j: (0, j)),
                  pl.BlockSpec((1, tn), lambda j: (0, j))],
        out_specs=[pl.BlockSpec((8, tn), lambda j: (0, j)), pl.BlockSpec((8, D), lambda j: (0, 0))],
        compiler_params=_cparams())(c8, w_ada, b_ada_shard)


def _layer_norm_stats(u):
    mu = jnp.mean(u, axis=1, keepdims=True)
    d = u - mu
    var = jnp.mean(d * d, axis=1, keepdims=True)
    rstd = lax.rsqrt(var + LN_EPS)
    return d * rstd, rstd


def _in_proj(x, vec, w_in, b_in, name):
    s = x.shape[0]
    tb = min(TOK_TILE, s)

    def body(x_ref, vec_ref, w_ref, b_ref, ht_ref, p_ref):
        h = x_ref[...] * (1.0 + vec_ref[V_SC_A:V_SC_A + 1, :]) + vec_ref[V_SH_A:V_SH_A + 1, :]
        hb = h.astype(BF16)
        ht_ref[...] = h.T.astype(BF16)
        proj = _dot(hb, w_ref[...]) + b_ref[...]
        col = lax.broadcasted_iota(jnp.int32, (1, D_IN), 1)
        is_q = (col < SB_W) | ((col >= 3 * SB_W) & (col < 3 * SB_W + SWA_QW))
        p_ref[...] = (proj * jnp.where(is_q, QK_SCALE, 1.0)).astype(BF16)

    return pl.pallas_call(
        body, name=name, grid=(s // tb,),
        out_shape=[jax.ShapeDtypeStruct((D, s), BF16), jax.ShapeDtypeStruct((s, D_IN), BF16)],
        in_specs=[pl.BlockSpec((tb, D), lambda i: (i, 0)), _resident((VEC_ROWS, D)), _resident((D, D_IN)),
                  _resident((1, D_IN))],
        out_specs=[pl.BlockSpec((D, tb), lambda i: (0, i)), pl.BlockSpec((tb, D_IN), lambda i: (i, 0))],
        compiler_params=_cparams())(x, vec, w_in, b_in)


def _softplus_parts(z):
    e1 = jnp.exp(-jnp.abs(z))
    sp = jnp.maximum(z, 0.0) + jnp.log(1.0 + e1)
    return sp, e1


def _sb_forward(proj, shards, name):
    s = proj.shape[0]
    tq, tk = min(SB_TQ, s), min(SB_TK, s)
    r = tq // tk

    n_sh = len(shards)
    nkb = SB_W // LANES
    nq = s // tq

    assert r % SB_UNROLL == 0, "the sweep below the diagonal takes whole steps"

    def body(q_ref, k_ref, v_ref, *refs):
        sh_refs, (o_ref, tot_ref, start_ref), got_refs = refs[:n_sh], refs[n_sh:n_sh + 3], refs[n_sh + 3:2 * n_sh + 3]
        acc_refs, run_refs = refs[2 * n_sh + 3:2 * n_sh + 5]
        i = pl.program_id(1)
        step = pl.program_id(0) * nq + i
        gather = _gather_exchange(sh_refs, got_refs, *refs[2 * n_sh + 5:])

        @pl.when(step == 0)
        def _():
            gather.start()

        lane = lax.broadcasted_iota(jnp.int32, (1, LANES), 1)
        first = lane < HEAD_DIM
        qp = q_ref[...]
        zero = jnp.zeros((), BF16)
        qs = (jnp.where(first, qp, zero), jnp.where(first, zero, qp))
        later = _sum_matrix(tk, lambda row, col: row > col)
        acc_refs[...] = jnp.zeros_like(acc_refs)
        run_refs[...] = jnp.zeros_like(run_refs)

        def blocks(tiles):
            rows = [slice(r0, r0 + n) for r0, n, _, _ in tiles]
            kjs = [k_ref[pl.ds(pl.multiple_of(j * tk, tk), tk), :] for _, _, j, _ in tiles]
            vjs = [v_ref[pl.ds(pl.multiple_of(j * tk, tk), tk), :] for _, _, j, _ in tiles]
            chains = [(hd, t) for t in range(len(tiles)) for hd in range(2)]
            zs = [_dot_nt(qs[hd][rows[t]], kjs[t]) for hd, t in chains]
            sps = [_softplus_parts(z)[0] for z in zs]
            befores = [_before(i * tq + r0, n, j * tk, tk) if diag else None for r0, n, j, diag in tiles]
            spms = [sp if befores[t] is None else jnp.where(befores[t], sp, 0.0) for (hd, t), sp in zip(chains, sps)]
            cums = [_block_sums(spm, later) for spm in spms]
            runs, ws = {}, []
            for (hd, t), z, sp, (cum, sm) in zip(chains, zs, sps, cums):
                key = (hd, tiles[t][0])
                if key not in runs:
                    runs[key] = run_refs[hd, rows[t], :]
                w = jnp.exp(z - sp - cum - _across(runs[key], tk))
                if befores[t] is not None:
                    w = jnp.where(befores[t], w, 0.0)
                ws.append(w.astype(BF16))
                runs[key] = runs[key] + sm
            sums = {}
            for (hd, t), w in zip(chains, ws):
                key, pv = (hd, tiles[t][0]), _dot(w, vjs[t])
                sums[key] = pv if key not in sums else sums[key] + pv
            for (hd, r0), run in runs.items():
                span = slice(r0, r0 + run.shape[0])
                acc_refs[hd, span, :] += sums[(hd, r0)]
                run_refs[hd, span, :] = run

        blocks([(d * tk, tk, i * r + e, e == d) for d in range(r) for e in range(d, -1, -1)])

        below = i * r

        def swept_mass():
            return jnp.min(jnp.minimum(run_refs[0], run_refs[1]))

        def more(carry):
            n, mass = carry
            return (n < below // SB_UNROLL) & (mass < SB_DEAD_MASS)

        def sweep(carry):
            n, _ = carry
            top = below - 1 - SB_UNROLL * n
            blocks([(0, tq, top - u, False) for u in range(SB_UNROLL)])
            return n + 1, swept_mass()

        n_swept, _ = lax.while_loop(more, sweep, (0, swept_mass()))
        start_ref[pl.program_id(0), i] = (below - SB_UNROLL * n_swept).astype(F32)
        o_ref[...] = jnp.where(first, acc_refs[0], acc_refs[1])
        tot_ref[...] = jnp.where(first, run_refs[0], run_refs[1])

        @pl.when(step == nkb * nq - 1)
        def _():
            gather.wait()

    shp = jax.ShapeDtypeStruct((s, SB_W), F32)
    qspec = pl.BlockSpec((tq, LANES), lambda p, i: (i, p))
    hbm = pl.BlockSpec(memory_space=pl.ANY)
    return pl.pallas_call(
        body, name=name, grid=(nkb, nq),
        out_shape=[shp, shp, jax.ShapeDtypeStruct((nkb, nq), F32)]
        + [jax.ShapeDtypeStruct((N_CHIPS,) + a.shape, a.dtype) for a in shards],
        in_specs=[qspec,
                  pl.BlockSpec((s, LANES), lambda p, i: (0, nkb + p)),
                  pl.BlockSpec((s, LANES), lambda p, i: (0, 2 * nkb + p))] + [hbm] * n_sh,
        out_specs=[qspec, qspec, pl.BlockSpec(memory_space=pltpu.SMEM)] + [hbm] * n_sh,
        scratch_shapes=[pltpu.VMEM((2, tq, LANES), F32), pltpu.VMEM((2, tq, LANES), F32)] + _exchange_sems(n_sh),
        compiler_params=_cparams())(proj, proj, proj, *shards)


def _swa_masks(n):
    ti = lax.broadcasted_iota(jnp.int32, (WINDOW, 2 * WINDOW), 0)
    kj = lax.broadcasted_iota(jnp.int32, (WINDOW, 2 * WINDOW), 1)
    dist = ti + WINDOW - kj
    valid = (dist >= 0) & (dist < WINDOW) & ((n * WINDOW - WINDOW + kj) >= 0)
    return valid, dist.astype(F32)


def _swa_probs(sc, valid, distf, h, sink):
    slope = 2.0 ** (-(h + 1))
    sc = jnp.where(valid, sc - slope * distf, MASK_VALUE)
    mx = jnp.maximum(jnp.max(sc, axis=1, keepdims=True), sink)
    p = jnp.exp(sc - mx)
    es = jnp.exp(sink - mx)
    inv = 1.0 / (jnp.sum(p, axis=1, keepdims=True) + es)
    return p * inv, es * inv


def _swa_forward(proj, sinks, shards, name):
    s = proj.shape[0]
    nb = s // WINDOW
    qb, kb, vb = 3 * SB_W // SWA_QW, (3 * SB_W + SWA_QW) // LANES, (3 * SB_W + SWA_QW + SWA_KW) // LANES
    n_sh = len(shards)

    def body(q_ref, kp_ref, kc_ref, vp_ref, vc_ref, sink_ref, *refs):
        sh_refs, o_ref, got_refs = refs[:n_sh], refs[n_sh], refs[n_sh + 1:2 * n_sh + 1]
        n = pl.program_id(0)
        gather = _gather_exchange(sh_refs, got_refs, *refs[2 * n_sh + 1:])

        @pl.when(n == 0)
        def _():
            gather.start()

        k = jnp.concatenate([kp_ref[...], kc_ref[...]], axis=0)
        v = jnp.concatenate([vp_ref[...], vc_ref[...]], axis=0)
        k_sw = pltpu.roll(k.astype(F32), HEAD_DIM, 1).astype(BF16)
        v_sw = pltpu.roll(v.astype(F32), HEAD_DIM, 1).astype(BF16)
        lane = lax.broadcasted_iota(jnp.int32, (1, LANES), 1)
        halves = [lane < HEAD_DIM, lane >= HEAD_DIM]
        valid, distf = _swa_masks(n)
        heads = range(2 * 4)
        qms = [jnp.where(halves[h % 2], q_ref[:, (h // 2) * LANES:(h // 2 + 1) * LANES], jnp.zeros((), BF16))
               for h in heads]
        kus = [k if h // 4 == h % 2 else k_sw for h in heads]
        vus = [v if h // 4 == h % 2 else v_sw for h in heads]
        scores = [_dot_nt(qms[h], kus[h]) for h in heads]
        ps = [_swa_probs(scores[h], valid, distf, h, sink_ref[h])[0].astype(BF16) for h in heads]
        outs = [_dot(ps[h], vus[h]) for h in heads]
        for pair in range(4):
            o_ref[:, pair * LANES:(pair + 1) * LANES] = jnp.where(halves[0], outs[2 * pair], outs[2 * pair + 1])

        @pl.when(n == nb - 1)
        def _():
            gather.wait()

    prev = lambda n: jnp.maximum(n - 1, 0)
    hbm = pl.BlockSpec(memory_space=pl.ANY)
    return pl.pallas_call(
        body, name=name, grid=(nb,),
        out_shape=[jax.ShapeDtypeStruct((s, SWA_QW), F32)]
        + [jax.ShapeDtypeStruct((N_CHIPS,) + a.shape, a.dtype) for a in shards],
        in_specs=[pl.BlockSpec((WINDOW, SWA_QW), lambda n: (n, qb)),
                  pl.BlockSpec((WINDOW, LANES), lambda n: (prev(n), kb)),
                  pl.BlockSpec((WINDOW, LANES), lambda n: (n, kb)),
                  pl.BlockSpec((WINDOW, LANES), lambda n: (prev(n), vb)),
                  pl.BlockSpec((WINDOW, LANES), lambda n: (n, vb)),
                  pl.BlockSpec(memory_space=pltpu.SMEM)] + [hbm] * n_sh,
        out_specs=[pl.BlockSpec((WINDOW, SWA_QW), lambda n: (n, 0))] + [hbm] * n_sh,
        scratch_shapes=_exchange_sems(n_sh),
        compiler_params=_cparams())(proj, proj, proj, proj, proj, sinks, *shards)


def _rms_parts(y):
    return lax.rsqrt(jnp.mean(y * y, axis=1, keepdims=True) + RMS_EPS)


def _post_attention(y_sb, y_sw, x, vec, w_out, name):
    s = x.shape[0]
    tb = min(TOK_TILE, s)

    def body(ysb_ref, ysw_ref, x_ref, vec_ref, w_ref, mixedt_ref, attn_ref, x1_ref, h2_ref, h2t_ref):
        ysb, ysw = ysb_ref[...], ysw_ref[...]
        nsb_f = ysb * _rms_parts(ysb) * vec_ref[V_GN:V_GN + 1, :SB_W]
        nsw_f = ysw * _rms_parts(ysw) * vec_ref[V_GN:V_GN + 1, SB_W:]
        nsb, nsw = nsb_f.astype(BF16), nsw_f.astype(BF16)
        mixedt_ref[:SB_W, :] = nsb_f.T.astype(BF16)
        mixedt_ref[SB_W:, :] = nsw_f.T.astype(BF16)
        attn = _dot(nsb, w_ref[:SB_W, :]) + _dot(nsw, w_ref[SB_W:, :])
        attn_ref[...] = attn
        u1 = ALPHA * x_ref[...] + (1.0 + vec_ref[V_G_A:V_G_A + 1, :]) * attn
        xhat, _ = _layer_norm_stats(u1)
        x1 = xhat * vec_ref[V_LN1G:V_LN1G + 1, :] + vec_ref[V_LN1B:V_LN1B + 1, :]
        x1_ref[...] = x1
        h2 = x1 * (1.0 + vec_ref[V_SC_F:V_SC_F + 1, :]) + vec_ref[V_SH_F:V_SH_F + 1, :]
        h2_ref[...] = h2.astype(BF16)
        h2t_ref[...] = h2.T.astype(BF16)

    half = pl.BlockSpec((tb, SB_W), lambda i: (i, 0))
    full = pl.BlockSpec((tb, D), lambda i: (i, 0))
    full_t = pl.BlockSpec((D, tb), lambda i: (0, i))
    return pl.pallas_call(
        body, name=name, grid=(s // tb,),
        out_shape=[jax.ShapeDtypeStruct((D, s), BF16), jax.ShapeDtypeStruct((s, D), F32),
                   jax.ShapeDtypeStruct((s, D), F32), jax.ShapeDtypeStruct((s, D), BF16),
                   jax.ShapeDtypeStruct((D, s), BF16)],
        in_specs=[half, half, full, _resident((VEC_ROWS, D)), _resident((D, D))],
        out_specs=[full_t, full, full, full, full_t],
        compiler_params=_cparams())(y_sb, y_sw, x, vec, w_out)


def _ffn_forward(h2, w_gu, w_down, name):
    s = h2.shape[0]
    tb = min(FFN_TILE, s)

    def body(h_ref, wgu_ref, wd_ref, gu_ref, actt_ref, ffn_ref):
        gu = _dot(h_ref[...], wgu_ref[...])
        gu_ref[...] = gu.astype(BF16)
        gate, up = gu[:, :D_FF], gu[:, D_FF:]
        act = gate * (1.0 / (1.0 + jnp.exp(-gate))) * up
        actt_ref[...] = act.T.astype(BF16)
        ffn_ref[...] = _dot(act.astype(BF16), wd_ref[...])

    return pl.pallas_call(
        body, name=name, grid=(s // tb,),
        out_shape=[jax.ShapeDtypeStruct((s, 2 * D_FF), BF16), jax.ShapeDtypeStruct((D_FF, s), BF16),
                   jax.ShapeDtypeStruct((s, D), F32)],
        in_specs=[pl.BlockSpec((tb, D), lambda i: (i, 0)), _resident((D, 2 * D_FF)), _resident((D_FF, D))],
        out_specs=[pl.BlockSpec((tb, 2 * D_FF), lambda i: (i, 0)), pl.BlockSpec((D_FF, tb), lambda i: (0, i)),
                   pl.BlockSpec((tb, D), lambda i: (i, 0))],
        compiler_params=_cparams())(h2, w_gu, w_down)


def _layer_norm_bwd(dxhat, xhat, rstd):
    m1 = jnp.mean(dxhat, axis=1, keepdims=True)
    m2 = jnp.mean(dxhat * xhat, axis=1, keepdims=True)
    return rstd * (dxhat - m1 - xhat * m2)


def _colsum(a):
    return jnp.sum(a, axis=0, keepdims=True)


A_LN2G, A_LN2B, A_GF, A_SCF, A_SHF, A_LOSS = range(6)
B_LN1G, B_LN1B, B_GA, B_GN = range(4)
C_SCA, C_SHA = range(2)


def _ffn_backward(x1, ffn, target, gu, vec, w_gu, w_down, name):
    s = x1.shape[0]
    tb = min(FFN_BWD_TILE, s)

    def body(x1_ref, ffn_ref, t_ref, gu_ref, vec_ref, wgu_ref, wd_ref, dffn_ref, dgu_ref, dx1_ref, acc_ref):
        @pl.when(pl.program_id(0) == 0)
        def _():
            acc_ref[...] = jnp.zeros_like(acc_ref)

        x1v, ffn_v = x1_ref[...], ffn_ref[...]
        g_f = 1.0 + vec_ref[V_G_F:V_G_F + 1, :]
        u2 = ALPHA * x1v + g_f * ffn_v
        xhat, rstd = _layer_norm_stats(u2)
        ln_g = vec_ref[V_LN2G:V_LN2G + 1, :]
        err = xhat * ln_g + vec_ref[V_LN2B:V_LN2B + 1, :] - t_ref[...]
        dx2 = err * (1.0 / D)
        acc_ref[A_LOSS:A_LOSS + 1, :] += _colsum(err * err) * (0.5 / D)
        acc_ref[A_LN2G:A_LN2G + 1, :] += _colsum(dx2 * xhat)
        acc_ref[A_LN2B:A_LN2B + 1, :] += _colsum(dx2)
        du2 = _layer_norm_bwd(dx2 * ln_g, xhat, rstd)
        acc_ref[A_GF:A_GF + 1, :] += _colsum(du2 * ffn_v)
        dffn = (g_f * du2).astype(BF16)
        dffn_ref[...] = dffn
        dact = _dot_nt(dffn, wd_ref[...])
        gate, up = gu_ref[:, :D_FF].astype(F32), gu_ref[:, D_FF:].astype(F32)
        sg = 1.0 / (1.0 + jnp.exp(-gate))
        dgate = (dact * up * (sg * (1.0 + gate * (1.0 - sg)))).astype(BF16)
        dup = (dact * (gate * sg)).astype(BF16)
        dgu_ref[:, :D_FF] = dgate
        dgu_ref[:, D_FF:] = dup
        dh2 = _dot_nt(dgate, wgu_ref[:, :D_FF]) + _dot_nt(dup, wgu_ref[:, D_FF:])
        dx1_ref[...] = ALPHA * du2 + dh2 * (1.0 + vec_ref[V_SC_F:V_SC_F + 1, :])
        acc_ref[A_SCF:A_SCF + 1, :] += _colsum(dh2 * x1v)
        acc_ref[A_SHF:A_SHF + 1, :] += _colsum(dh2)

    full = pl.BlockSpec((tb, D), lambda i: (i, 0))
    wide = pl.BlockSpec((tb, 2 * D_FF), lambda i: (i, 0))
    return pl.pallas_call(
        body, name=name, grid=(s // tb,),
        out_shape=[jax.ShapeDtypeStruct((s, D), BF16), jax.ShapeDtypeStruct((s, 2 * D_FF), BF16),
                   jax.ShapeDtypeStruct((s, D), F32), jax.ShapeDtypeStruct((8, D), F32)],
        in_specs=[full, full, full, wide, _resident((VEC_ROWS, D)), _resident((D, 2 * D_FF)), _resident((D_FF, D))],
        out_specs=[full, wide, full, pl.BlockSpec((8, D), lambda i: (0, 0))],
        compiler_params=_cparams())(x1, ffn, target, gu, vec, w_gu, w_down)


def _attn_out_backward(dx1, x, attn, y_sb, y_sw, vec, w_out, name):
    s = x.shape[0]
    tb = min(TOK_TILE, s)

    def body(dx1_ref, x_ref, attn_ref, ysb_ref, ysw_ref, vec_ref, w_ref, du1_ref, dattn_ref, dy_ref, acc_ref):
        @pl.when(pl.program_id(0) == 0)
        def _():
            acc_ref[...] = jnp.zeros_like(acc_ref)

        attn = attn_ref[...]
        g_a = 1.0 + vec_ref[V_G_A:V_G_A + 1, :]
        xhat, rstd = _layer_norm_stats(ALPHA * x_ref[...] + g_a * attn)
        dx1v = dx1_ref[...]
        acc_ref[B_LN1G:B_LN1G + 1, :] += _colsum(dx1v * xhat)
        acc_ref[B_LN1B:B_LN1B + 1, :] += _colsum(dx1v)
        du1 = _layer_norm_bwd(dx1v * vec_ref[V_LN1G:V_LN1G + 1, :], xhat, rstd)
        du1_ref[...] = du1
        acc_ref[B_GA:B_GA + 1, :] += _colsum(du1 * attn)
        dattn = (g_a * du1).astype(BF16)
        dattn_ref[...] = dattn
        dmixed = _dot_nt(dattn, w_ref[...])
        for lo, y_ref in ((0, ysb_ref), (SB_W, ysw_ref)):
            y = y_ref[...]
            rr = _rms_parts(y)
            dn = dmixed[:, lo:lo + SB_W]
            acc_ref[B_GN:B_GN + 1, lo:lo + SB_W] += _colsum(dn * y * rr)
            dng = dn * vec_ref[V_GN:V_GN + 1, lo:lo + SB_W]
            dy_ref[:, lo:lo + SB_W] = rr * dng - y * (rr * rr * rr) * jnp.mean(dng * y, axis=1, keepdims=True)

    half = pl.BlockSpec((tb, SB_W), lambda i: (i, 0))
    full = pl.BlockSpec((tb, D), lambda i: (i, 0))
    return pl.pallas_call(
        body, name=name, grid=(s // tb,),
        out_shape=[jax.ShapeDtypeStruct((s, D), F32), jax.ShapeDtypeStruct((s, D), BF16),
                   jax.ShapeDtypeStruct((s, D), F32), jax.ShapeDtypeStruct((8, D), F32)],
        in_specs=[full, full, full, half, half, _resident((VEC_ROWS, D)), _resident((D, D))],
        out_specs=[full, full, full, pl.BlockSpec((8, D), lambda i: (0, 0))],
        compiler_params=_cparams())(dx1, x, attn, y_sb, y_sw, vec, w_out)


def _sb_backward(proj, sp_total, sweep_start, dy, slabs, name):
    s = proj.shape[0]
    tq, tk = min(SB_TQ, s), min(SB_TK, s)
    r = tq // tk
    nkb = SB_W // LANES
    nq = s // tq

    assert r % SB_UNROLL == 0, "the sweep below the diagonal takes whole steps"

    n_sl = len(slabs)

    def body(q_ref, k_ref, v_ref, tot_ref, do_ref, start_ref, *refs):
        slab_refs, (dq_ref, dk_ref, dv_ref), got_refs = refs[:n_sl], refs[n_sl:n_sl + 3], refs[n_sl + 3:2 * n_sl + 3]
        dq_acc, left_refs, gsum_refs = refs[2 * n_sl + 3:2 * n_sl + 6]
        i = pl.program_id(1)
        step = pl.program_id(0) * nq + i
        scatter = _scatter_exchange(slab_refs, got_refs, *refs[2 * n_sl + 6:])

        @pl.when(step == 0)
        def _():
            scatter.start()

        @pl.when(i == 0)
        def _():
            dk_ref[...] = jnp.zeros_like(dk_ref)
            dv_ref[...] = jnp.zeros_like(dv_ref)

        lane = lax.broadcasted_iota(jnp.int32, (1, LANES), 1)
        first = lane < HEAD_DIM
        qp, dop, totp = q_ref[...], do_ref[...], tot_ref[...]
        zero = jnp.zeros((), BF16)
        qs = (jnp.where(first, qp, zero), jnp.where(first, zero, qp))
        dofs = (jnp.where(first, dop, 0.0), jnp.where(first, 0.0, dop))
        dobs = tuple(d.astype(BF16) for d in dofs)
        dots = tuple(d.T.astype(BF16) for d in dofs)
        qts = tuple(qh.astype(F32).T.astype(BF16) for qh in qs)
        later = _sum_matrix(tk, lambda row, col: row > col)
        earlier = _sum_matrix(tk, lambda row, col: row < col)
        dq_acc[...] = jnp.zeros_like(dq_acc)
        gsum_refs[...] = jnp.zeros_like(gsum_refs)
        swapped = pltpu.roll(totp, HEAD_DIM, 1)
        left_refs[0] = jnp.where(first, totp, swapped)
        left_refs[1] = jnp.where(first, swapped, totp)

        def blocks(tiles):
            rows = [slice(r0, r0 + n) for r0, n, _, _ in tiles]
            kjs = [k_ref[pl.ds(pl.multiple_of(j * tk, tk), tk), :] for _, _, j, _ in tiles]
            vjs = [v_ref[pl.ds(pl.multiple_of(j * tk, tk), tk), :] for _, _, j, _ in tiles]
            chains = [(hd, t) for t in range(len(tiles)) for hd in range(2)]
            keys = [(hd, tiles[t][0]) for hd, t in chains]
            zs = [_dot_nt(qs[hd][rows[t]], kjs[t]) for hd, t in chains]
            dws = [_dot_nt(dobs[hd][rows[t]], vjs[t]) for hd, t in chains]
            parts = [_softplus_parts(z) for z in zs]
            sps = [p[0] for p in parts]
            befores = [_before(i * tq + r0, n, j * tk, tk) if diag else None for r0, n, j, diag in tiles]
            spms = [sp if befores[t] is None else jnp.where(befores[t], sp, 0.0) for (hd, t), sp in zip(chains, sps)]
            cums = [_block_sums(spm, later) for spm in spms]
            lefts, ws = {}, []
            for key, (hd, t), z, sp, (cum, sm) in zip(keys, chains, zs, sps, cums):
                if key not in lefts:
                    lefts[key] = left_refs[hd, rows[t], :]
                lefts[key] = lefts[key] - sm
                w = jnp.exp(z - sp - cum - _across(lefts[key], tk))
                ws.append(w if befores[t] is None else jnp.where(befores[t], w, 0.0))
            wbs = [w.astype(BF16) for w in ws]
            dvs = [_dot(dots[hd][:, rows[t]], wb) for (hd, t), wb in zip(chains, wbs)]
            gs = [dw * w for dw, w in zip(dws, ws)]
            gcums = [_block_sums(g, earlier) for g in gs]
            gsums, dzbs = {}, []
            for key, (hd, t), z, (sp, e1), g, (gcum, gsm) in zip(keys, chains, zs, parts, gs, gcums):
                if key not in gsums:
                    gsums[key] = gsum_refs[hd, rows[t], :]
                inv = 1.0 / (1.0 + e1)
                sig = jnp.where(z >= 0.0, inv, e1 * inv)
                dz = g - sig * (g + _across(gsums[key], tk) + gcum)
                dzbs.append((dz if befores[t] is None else jnp.where(befores[t], dz, 0.0)).astype(BF16))
                gsums[key] = gsums[key] + gsm
            dqs = [_dot(dzb, kjs[t]) for (hd, t), dzb in zip(chains, dzbs)]
            dks = [_dot(qts[hd][:, rows[t]], dzb) for (hd, t), dzb in zip(chains, dzbs)]
            for t, (_, _, j, _) in enumerate(tiles):
                dv_ref[j] += dvs[2 * t] + dvs[2 * t + 1]
                dk_ref[j] += dks[2 * t] + dks[2 * t + 1]
            totals = {}
            for key, dq in zip(keys, dqs):
                totals[key] = dq if key not in totals else totals[key] + dq
            for (hd, r0), tot in totals.items():
                span = slice(r0, r0 + tot.shape[0])
                dq_acc[hd, span, :] += tot
                left_refs[hd, span, :] = lefts[(hd, r0)]
                gsum_refs[hd, span, :] = gsums[(hd, r0)]

        below = i * r
        start = jnp.clip(start_ref[pl.program_id(0), i].astype(jnp.int32), 0, below) // SB_UNROLL * SB_UNROLL

        def sweep(n, carry):
            blocks([(0, tq, start + SB_UNROLL * n + u, False) for u in range(SB_UNROLL)])
            return carry

        lax.fori_loop(0, (below - start) // SB_UNROLL, sweep, 0)
        blocks([(d * tk, tk, below + e, e == d) for d in range(r) for e in range(d + 1)])
        dq_ref[...] = jnp.where(first, dq_acc[0], dq_acc[1])

        @pl.when(step == nkb * nq - 1)
        def _():
            scatter.wait()

    shp = jax.ShapeDtypeStruct((s, SB_W), F32)
    qspec = pl.BlockSpec((tq, LANES), lambda p, i: (i, p))
    whole = pl.BlockSpec((None, s // tk, LANES, tk), lambda p, i: (p, 0, 0, 0))
    shp_t = jax.ShapeDtypeStruct((nkb, s // tk, LANES, tk), F32)
    hbm = pl.BlockSpec(memory_space=pl.ANY)
    return pl.pallas_call(
        body, name=name, grid=(nkb, nq),
        out_shape=[shp, shp_t, shp_t] + [jax.ShapeDtypeStruct(p.shape, p.dtype) for p in slabs],
        in_specs=[qspec,
                  pl.BlockSpec((s, LANES), lambda p, i: (0, nkb + p)),
                  pl.BlockSpec((s, LANES), lambda p, i: (0, 2 * nkb + p)),
                  qspec, qspec, pl.BlockSpec(memory_space=pltpu.SMEM)] + [hbm] * n_sl,
        out_specs=[qspec, whole, whole] + [hbm] * n_sl,
        scratch_shapes=[pltpu.VMEM((2, tq, LANES), F32), pltpu.VMEM((2, tq, LANES), F32), pltpu.VMEM((2, tq, LANES), F32)]
        + _exchange_sems(n_sl),
        compiler_params=_cparams())(proj, proj, proj, sp_total, dy, sweep_start, *slabs)


def _swa_backward(proj, y_sw, dy, sinks, gives, name):
    s = proj.shape[0]
    nb = s // WINDOW
    qb, kb, vb = 3 * SB_W // SWA_QW, (3 * SB_W + SWA_QW) // LANES, (3 * SB_W + SWA_QW + SWA_KW) // LANES

    n_gv = len(gives)

    def body(q_ref, kp_ref, kc_ref, vp_ref, vc_ref, o_ref, do_ref, sink_ref, *refs):
        give_refs, (dq_ref, dk_ref, dv_ref, ds_ref), got_refs = refs[:n_gv], refs[n_gv:n_gv + 4], refs[n_gv + 4:2 * n_gv + 4]
        n = pl.program_id(0)
        swap = _sibling_halves(give_refs, got_refs, *refs[2 * n_gv + 4:])

        @pl.when(n == 0)
        def _():
            for cp in swap:
                cp.start()

        @pl.when(n == 0)
        def _():
            dk_ref[...] = jnp.zeros_like(dk_ref)
            dv_ref[...] = jnp.zeros_like(dv_ref)
            ds_ref[...] = jnp.zeros_like(ds_ref)

        k = jnp.concatenate([kp_ref[...], kc_ref[...]], axis=0)
        v = jnp.concatenate([vp_ref[...], vc_ref[...]], axis=0)
        k_sw = pltpu.roll(k.astype(F32), HEAD_DIM, 1).astype(BF16)
        v_sw = pltpu.roll(v.astype(F32), HEAD_DIM, 1).astype(BF16)
        lane = lax.broadcasted_iota(jnp.int32, (1, LANES), 1)
        halves = [lane < HEAD_DIM, lane >= HEAD_DIM]
        valid, distf = _swa_masks(n)
        heads = range(2 * 4)
        cols = [slice((h // 2) * LANES, (h // 2 + 1) * LANES) for h in heads]
        qms = [jnp.where(halves[h % 2], q_ref[:, cols[h]], jnp.zeros((), BF16)) for h in heads]
        dos = [jnp.where(halves[h % 2], do_ref[:, cols[h]], 0.0) for h in heads]
        dobs = [d.astype(BF16) for d in dos]
        native = [h // 4 == h % 2 for h in heads]
        kus = [k if native[h] else k_sw for h in heads]
        vus = [v if native[h] else v_sw for h in heads]
        scores = [_dot_nt(qms[h], kus[h]) for h in heads]
        dps = [_dot_nt(dobs[h], vus[h]) for h in heads]
        deltas = [jnp.sum(dos[h] * o_ref[:, cols[h]], axis=1, keepdims=True) for h in heads]
        probs = [_swa_probs(scores[h], valid, distf, h, sink_ref[h]) for h in heads]
        pbs = [probs[h][0].astype(BF16) for h in heads]
        dscs = [(probs[h][0] * (dps[h] - deltas[h])).astype(BF16) for h in heads]
        dqs = [_dot(dscs[h], kus[h]) for h in heads]
        dks = [_dot_tn(dscs[h], qms[h]) for h in heads]
        dvs = [_dot_tn(pbs[h], dobs[h]) for h in heads]
        for h in heads:
            ds_ref[h:h + 1, :] += jnp.zeros((1, LANES), F32) - jnp.sum(probs[h][1] * deltas[h])
        for pair in range(4):
            dq_ref[:, cols[2 * pair]] = jnp.where(halves[0], dqs[2 * pair], dqs[2 * pair + 1])

        def gathered(parts):
            nat = sum(parts[h] for h in heads if native[h])
            rot = sum(parts[h] for h in heads if not native[h])
            return nat + pltpu.roll(rot, HEAD_DIM, 1)

        dk, dv = gathered(dks), gathered(dvs)
        prev = pl.multiple_of(jnp.maximum(n - 1, 0) * WINDOW, WINDOW)
        cur = pl.multiple_of(n * WINDOW, WINDOW)
        dk_ref[pl.ds(prev, WINDOW), :] += dk[:WINDOW]
        dv_ref[pl.ds(prev, WINDOW), :] += dv[:WINDOW]
        dk_ref[pl.ds(cur, WINDOW), :] += dk[WINDOW:]
        dv_ref[pl.ds(cur, WINDOW), :] += dv[WINDOW:]

        @pl.when(n == nb - 1)
        def _():
            for cp in swap:
                cp.wait()

    prev_blk = lambda n: jnp.maximum(n - 1, 0)
    wide = pl.BlockSpec((WINDOW, SWA_QW), lambda n: (n, 0))
    whole = pl.BlockSpec((s, LANES), lambda n: (0, 0))
    hbm = pl.BlockSpec(memory_space=pl.ANY)
    return pl.pallas_call(
        body, name=name, grid=(nb,),
        out_shape=[jax.ShapeDtypeStruct((s, SWA_QW), F32), jax.ShapeDtypeStruct((s, LANES), F32),
                   jax.ShapeDtypeStruct((s, LANES), F32), jax.ShapeDtypeStruct((8, LANES), F32)] + _halves_shapes(gives),
        in_specs=[pl.BlockSpec((WINDOW, SWA_QW), lambda n: (n, qb)),
                  pl.BlockSpec((WINDOW, LANES), lambda n: (prev_blk(n), kb)),
                  pl.BlockSpec((WINDOW, LANES), lambda n: (n, kb)),
                  pl.BlockSpec((WINDOW, LANES), lambda n: (prev_blk(n), vb)),
                  pl.BlockSpec((WINDOW, LANES), lambda n: (n, vb)),
                  wide,
                  pl.BlockSpec((WINDOW, SWA_QW), lambda n: (n, 1)),
                  pl.BlockSpec(memory_space=pltpu.SMEM)] + [hbm] * n_gv,
        out_specs=[wide, whole, whole, pl.BlockSpec((8, LANES), lambda n: (0, 0))] + [hbm] * n_gv,
        scratch_shapes=_halves_sems(n_gv),
        compiler_params=_cparams())(proj, proj, proj, proj, proj, y_sw, dy, sinks, *gives)


def _in_proj_backward(dq_sb, dkt_sb, dvt_sb, dq_sw, dk_sw, dv_sw, du1, x, vec, w_in, name):
    s = x.shape[0]
    tb = min(TOK_TILE, s)
    n_pairs, _, _, tk = dkt_sb.shape

    def body(dqsb_ref, dktsb_ref, dvtsb_ref, dqsw_ref, dksw_ref, dvsw_ref, du1_ref, x_ref, vec_ref, w_ref,
             dproj_ref, gx_ref, acc_ref, bacc_ref):
        @pl.when(pl.program_id(0) == 0)
        def _():
            acc_ref[...] = jnp.zeros_like(acc_ref)
            bacc_ref[...] = jnp.zeros_like(bacc_ref)

        pieces = ((0, dqsb_ref, QK_SCALE), (3 * SB_W, dqsw_ref, QK_SCALE), (3 * SB_W + SWA_QW, dksw_ref, 1.0),
                  (3 * SB_W + SWA_QW + SWA_KW, dvsw_ref, 1.0))
        for lo, ref, scale in pieces:
            width = ref.shape[1]
            piece = ref[...] * scale
            bacc_ref[0:1, lo:lo + width] += _colsum(piece)
            dproj_ref[:, lo:lo + width] = piece.astype(BF16)
        for base, ref in ((SB_W, dktsb_ref), (2 * SB_W, dvtsb_ref)):
            for p in range(n_pairs):
                lo = base + p * LANES
                for jj in range(tb // tk):
                    piece = ref[p, jj].T
                    bacc_ref[0:1, lo:lo + LANES] += _colsum(piece)
                    dproj_ref[jj * tk:(jj + 1) * tk, lo:lo + LANES] = piece.astype(BF16)
        dh = _dot_nt(dproj_ref[...], w_ref[...])
        xv = x_ref[...]
        gx_ref[...] = ALPHA * du1_ref[...] + dh * (1.0 + vec_ref[V_SC_A:V_SC_A + 1, :])
        acc_ref[C_SCA:C_SCA + 1, :] += _colsum(dh * xv)
        acc_ref[C_SHA:C_SHA + 1, :] += _colsum(dh)

    half = pl.BlockSpec((tb, SB_W), lambda i: (i, 0))
    narrow = pl.BlockSpec((tb, LANES), lambda i: (i, 0))
    full = pl.BlockSpec((tb, D), lambda i: (i, 0))
    blocks_t = pl.BlockSpec((n_pairs, tb // tk, LANES, tk), lambda i: (0, i, 0, 0))
    return pl.pallas_call(
        body, name=name, grid=(s // tb,),
        out_shape=[jax.ShapeDtypeStruct((s, D_IN), BF16), jax.ShapeDtypeStruct((s, D), F32),
                   jax.ShapeDtypeStruct((8, D), F32), jax.ShapeDtypeStruct((8, D_IN), F32)],
        in_specs=[half, blocks_t, blocks_t, half, narrow, narrow, full, full, _resident((VEC_ROWS, D)),
                  _resident((D, D_IN))],
        out_specs=[pl.BlockSpec((tb, D_IN), lambda i: (i, 0)), full, pl.BlockSpec((8, D), lambda i: (0, 0)),
                   pl.BlockSpec((8, D_IN), lambda i: (0, 0))],
        compiler_params=_cparams())(dq_sb, dkt_sb, dvt_sb, dq_sw, dk_sw, dv_sw, du1, x, vec, w_in)


def _weight_grad(at, b, name, col_shards=1):
    m, s = at.shape
    n = b.shape[1]
    if col_shards > 1:
        tn = n // col_shards
        out_shape = jax.ShapeDtypeStruct((col_shards, m, tn), F32)
        out_spec = pl.BlockSpec((None, m, tn), lambda j, k: (j, 0, 0))
    else:
        tn = 512 if n % 512 == 0 else n
        out_shape = jax.ShapeDtypeStruct((m, n), F32)
        out_spec = pl.BlockSpec((m, tn), lambda j, k: (0, j))
    ts = min(512, s)

    def body(at_ref, b_ref, o_ref):
        @pl.when(pl.program_id(1) == 0)
        def _():
            o_ref[...] = jnp.zeros_like(o_ref)

        o_ref[...] += _dot(at_ref[...], b_ref[...])

    return pl.pallas_call(
        body, name=name, grid=(n // tn, s // ts),
        out_shape=out_shape,
        in_specs=[pl.BlockSpec((m, ts), lambda j, k: (0, k)), pl.BlockSpec((ts, tn), lambda j, k: (k, j))],
        out_specs=out_spec,
        compiler_params=_cparams())(at, b)


def _pad_rows(v, rows):
    return jnp.concatenate([v, jnp.zeros((rows - v.shape[0], v.shape[1]), v.dtype)], axis=0)


def _col_shards(w, n_shards):
    r, n = w.shape
    return w.reshape(r, n_shards, n // n_shards).transpose(1, 0, 2)


def kernel(x, c, w_ada, b_ada, w_in, b_in, sinks, gn_sb, gn_swa, w_out, ln1_g, ln1_b, w_gu, w_down, ln2_g, ln2_b, loss_target, m_w_ada, m_b_ada, m_w_in, m_b_in, m_sinks, m_gn_sb, m_gn_swa, m_w_out, m_ln1_g, m_ln1_b, m_w_gu, m_w_down, m_ln2_g, m_ln2_b, v_w_ada, v_b_ada, v_w_in, v_b_in, v_sinks, v_gn_sb, v_gn_swa, v_w_out, v_ln1_g, v_ln1_b, v_w_gu, v_w_down, v_ln2_g, v_ln2_b):
    ix, iy, ic = lax.axis_index("x"), lax.axis_index("y"), lax.axis_index("c")
    chip = 2 * ix + iy
    dev = 4 * ix + 2 * iy + ic
    xs, target = x[0], loss_target[0]
    s = xs.shape[0]

    c_rows, g_in = _allgather8(_pad_rows(c, 8), "gather_c", gather=[w_in[0].astype(BF16)])
    c_all = c_rows[::8]
    n_ada = w_ada.shape[2]
    b_ada_shard = lax.dynamic_slice_in_dim(b_ada, chip * n_ada, n_ada, axis=1)
    mod_cols, silu_c = _mod_shard(c_all, w_ada[0], b_ada_shard, "mod_shard")
    mod_all = _allgather8(mod_cols, "gather_mod")[0].reshape(N_DEV, 8, n_ada)
    mod_mine = lax.dynamic_index_in_dim(mod_all, dev, axis=1, keepdims=False)
    mod = mod_mine.reshape(N_CHIPS, 2, n_ada)[:, 0].reshape(6, D)
    vec = jnp.concatenate([mod, ln1_g, ln1_b, ln2_g, ln2_b, jnp.concatenate([gn_sb, gn_swa], axis=1),
                           jnp.zeros((VEC_ROWS - 11, D), F32)], axis=0)

    w_in_b = g_in.transpose(1, 0, 2).reshape(D, D_IN)

    h_t, proj = _in_proj(xs, vec, w_in_b, b_in, "in_proj")
    y_sb, sp_total, sweep_start, g_out, g_gu = _sb_forward(
        proj, [w_out[0].astype(BF16), w_gu[0].astype(BF16)], "sb_forward")
    w_gu_b = g_gu.transpose(1, 0, 2).reshape(D, 2 * D_FF)
    w_out_b = g_out.reshape(D, D)
    sink_vec = sinks[0]
    y_sw, g_down = _swa_forward(proj, sink_vec, [w_down[0].astype(BF16)], "swa_forward")
    w_down_b = g_down.reshape(D_FF, D)
    mixed_t, attn, x1, h2_b, h2_t = _post_attention(y_sb, y_sw, xs, vec, w_out_b, "post_attention")
    gu, act_t, ffn = _ffn_forward(h2_b, w_gu_b, w_down_b, "ffn_forward")

    def in_halves(shards):
        n_sh, rows, cols = shards.shape
        return shards.reshape(n_sh, 2, rows // 2, cols)

    core = ic.reshape(1).astype(jnp.int32)
    dffn_b, dgu_b, dx1, acc_f = _ffn_backward(x1, ffn, target, gu, vec, w_gu_b, w_down_b, "ffn_backward")
    dw_gu = _weight_grad(h2_t, dgu_b, "grad_w_gu", col_shards=4)
    dw_down = _weight_grad(act_t, dffn_b, "grad_w_down")
    du1, dattn_b, dy, acc_a = _attn_out_backward(dx1, xs, attn, y_sb, y_sw, vec, w_out_b, "attn_out_backward")
    dw_out = _weight_grad(mixed_t, dattn_b, "grad_w_out")
    first = [in_halves(dw_gu), in_halves(dw_down.reshape(4, D_FF // 4, D)), in_halves(dw_out.reshape(4, D // 4, D))]
    dq_sw, dk_sw, dv_sw, dsink, *got_first = _swa_backward(proj, y_sw, dy, sink_vec, first, "swa_backward")
    sums_first = [_chip_sum(arr, got, core, "grad_chip_sum_" + nm)
                  for arr, got, nm in zip(first, got_first, ("gu", "down", "out"))]
    dq_sb, dk_sb, dv_sb, *parts_first = _sb_backward(proj, sp_total, sweep_start, dy, sums_first, "sb_backward")
    dproj_b, grad_x, acc_i, acc_b = _in_proj_backward(dq_sb, dk_sb, dv_sb, dq_sw, dk_sw, dv_sw, du1, xs, vec, w_in_b,
                                                      "in_proj_backward")
    dw_in = _weight_grad(h_t, dproj_b, "grad_w_in")
    last = [in_halves(_col_shards(dw_in, 4))]
    sums_last = [_chip_sum(last[0], _halves_swap(last, "grad_halves_swap_in")[0], core, "grad_chip_sum_in")]

    dmod = jnp.concatenate([acc_i[C_SHA:C_SHA + 1], acc_i[C_SCA:C_SCA + 1], acc_a[B_GA:B_GA + 1],
                            acc_f[A_SHF:A_SHF + 1], acc_f[A_SCF:A_SCF + 1], acc_f[A_GF:A_GF + 1]], axis=1)
    dsink_row = jnp.concatenate([dsink[:, 0].reshape(1, 8), jnp.zeros((1, LANES - 8), F32)], axis=1)
    loss_row = jnp.concatenate([jnp.sum(acc_f[A_LOSS:A_LOSS + 1], axis=1, keepdims=True),
                                jnp.zeros((1, LANES - 1), F32)], axis=1)
    small = jnp.concatenate([dmod, acc_b[0:1], acc_a[B_LN1G:B_LN1G + 1], acc_a[B_LN1B:B_LN1B + 1],
                             acc_f[A_LN2G:A_LN2G + 1], acc_f[A_LN2B:A_LN2B + 1], acc_a[B_GN:B_GN + 1],
                             dsink_row, loss_row], axis=1)
    small_rows, *parts_last = _allgather8(_pad_rows(small, 8), "gather_small", scatter=sums_last)
    small_all = small_rows[::8]

    mine = [_sum4(p, "grad_reduce_" + nm) for p, nm in zip([*parts_first, *parts_last], ("gu", "down", "out", "in"))]
    theirs = _sibling_send(mine, "grad_half_return")
    gw_gu, gw_down, gw_out, gw_in = [
        jnp.concatenate([jnp.where(ic == 0, m_, t_), jnp.where(ic == 0, t_, m_)], axis=0) for m_, t_ in zip(mine, theirs)]

    small_names = ["b_ada", "b_in", "ln1_g", "ln1_b", "ln2_g", "ln2_b", "gn_sb", "gn_swa", "sinks"]
    small_at = [SM_MOD, SM_BIN, SM_LN1G, SM_LN1B, SM_LN2G, SM_LN2B, SM_GN, SM_GN + SB_W, SM_SINK]
    *small_out, loss_row_all = _small_update(
        small_all, small_at,
        [b_ada, b_in, ln1_g, ln1_b, ln2_g, ln2_b, gn_sb, gn_swa, sinks],
        [m_b_ada, m_b_in, m_ln1_g, m_ln1_b, m_ln2_g, m_ln2_b, m_gn_sb, m_gn_swa, m_sinks],
        [v_b_ada, v_b_in, v_ln1_g, v_ln1_b, v_ln2_g, v_ln2_b, v_gn_sb, v_gn_swa, v_sinks], SM_LOSS, "small_update")
    g_small, d_small, m2_small, v2_small = [dict(zip(small_names, leaves)) for leaves in small_out]
    loss = loss_row_all[0, 0]

    dmod_cols = lax.dynamic_slice_in_dim(small_all[:, SM_MOD:SM_BIN], chip * n_ada, n_ada, axis=1)
    gw_ada = _weight_grad(_pad_rows(silu_c, LANES).astype(BF16).T, _pad_rows(dmod_cols, LANES).astype(BF16), "grad_w_ada")

    big = {}
    for nm, w, g, m, v in (("w_ada", w_ada, gw_ada, m_w_ada, v_w_ada), ("w_in", w_in, gw_in, m_w_in, v_w_in),
                           ("w_out", w_out, gw_out, m_w_out, v_w_out), ("w_gu", w_gu, gw_gu, m_w_gu, v_w_gu),
                           ("w_down", w_down, gw_down, m_w_down, v_w_down)):
        d_, m2_, v2_ = _adamw(w[0], g, m[0], v[0], "adamw_" + nm)
        big[nm] = (g[None], d_[None], m2_[None], v2_[None])

    order = ["w_ada", "b_ada", "w_in", "b_in", "sinks", "gn_sb", "gn_swa", "w_out", "ln1_g", "ln1_b", "w_gu", "w_down",
             "ln2_g", "ln2_b"]

    def leaf(nm, which):
        if nm in big:
            return big[nm][which]
        return (g_small, d_small, m2_small, v2_small)[which][nm]

    outs = [loss, grad_x[None]]
    for which in range(4):
        outs += [leaf(nm, which) for nm in order]
    return tuple(outs)
```

```python
import functools
import math

import jax
import jax.numpy as jnp
from jax import lax
from jax.experimental import pallas as pl
from jax.experimental.pallas import tpu as pltpu

F32 = jnp.float32
BF16 = jnp.bfloat16

D = 1024
HEAD_DIM = 64
SB_W = 512
SWA_QW = 512
SWA_KW = 128
D_IN = 2304
D_FF = 2816
WINDOW = 128
ALPHA = 2.0 ** 0.25
LN_EPS = 1e-5
RMS_EPS = 1e-6
MASK_VALUE = -1e30
QK_SCALE = 1.0 / math.sqrt(HEAD_DIM)

ADAM_LR = 0.001
ADAM_B1 = 0.9
ADAM_B2 = 0.999
ADAM_EPS = 1e-08
ADAM_WD = 0.01
ADAM_STEP = 10

N_CHIPS = 4
N_DEV = 8
LANES = 128

SB_TQ = 512
SB_TK = 256
SB_UNROLL = 1
SB_DEAD_MASS = 110.0
TOK_TILE = 512
FFN_TILE = 256
FFN_BWD_TILE = 256
VMEM_LIMIT = 56 * 1024 * 1024
WGRAD_TOKENS = 2048
WGRAD_VMEM = 40 * 1024 * 1024

V_SH_A, V_SC_A, V_G_A, V_SH_F, V_SC_F, V_G_F, V_LN1G, V_LN1B, V_LN2G, V_LN2B, V_GN = range(11)
VEC_ROWS = 16

SM_MOD = 0
SM_BIN = 6 * D
SM_LN1G = SM_BIN + D_IN
SM_LN1B = SM_LN1G + D
SM_LN2G = SM_LN1B + D
SM_LN2B = SM_LN2G + D
SM_GN = SM_LN2B + D
SM_SINK = SM_GN + D
SM_LOSS = SM_SINK + LANES
SM_LEN = SM_LOSS + LANES

MESH = pl.DeviceIdType.MESH


def _cparams(**kw):
    return pltpu.CompilerParams(vmem_limit_bytes=VMEM_LIMIT, **kw)


def _resident(shape):
    nd = len(shape)
    return pl.BlockSpec(shape, lambda *_: (0,) * nd, pipeline_mode=pl.Buffered(1))


def _dot(a, b):
    return jnp.dot(a, b, preferred_element_type=F32)


def _dot_nt(a, b):
    return lax.dot_general(a, b, (((1,), (1,)), ((), ())), preferred_element_type=F32)


def _dot_tn(a, b):
    return lax.dot_general(a, b, (((0,), (0,)), ((), ())), preferred_element_type=F32)


def _sum_matrix(tk, keep):
    row = lax.broadcasted_iota(jnp.int32, (tk, tk + LANES), 0)
    col = lax.broadcasted_iota(jnp.int32, (tk, tk + LANES), 1)
    return (keep(row, col) | (col >= tk)).astype(BF16)


def _block_sums(x, m):
    tk = x.shape[1]
    res = _dot(x.astype(BF16), m)
    return res[:, :tk], res[:, tk:]


def _before(t0, n, s0, tk):
    return s0 + lax.broadcasted_iota(jnp.int32, (n, tk), 1) < t0 + lax.broadcasted_iota(jnp.int32, (n, tk), 0)


def _across(v, tk):
    return jnp.concatenate([v] * (tk // LANES), axis=1)


def _allgather8(v, name, gather=(), scatter=()):
    m_per, n = v.shape
    n_g, n_s = len(gather), len(scatter)

    def body(x_ref, *refs):
        g_in, s_in = refs[:n_g], refs[n_g:n_g + n_s]
        out_ref = refs[n_g + n_s]
        g_out, s_out = refs[n_g + n_s + 1:2 * n_g + n_s + 1], refs[2 * n_g + n_s + 1:2 * (n_g + n_s) + 1]
        send_sems, recv_sems, local_sem, *more_sems = refs[2 * (n_g + n_s) + 1:]
        beside = ([_gather_exchange(g_in, g_out, *more_sems[:3])] if n_g else []) + (
            [_scatter_exchange(s_in, s_out, *more_sems[-3:])] if n_s else [])
        for ex in beside:
            ex.start()
        x, y, c = lax.axis_index("x"), lax.axis_index("y"), lax.axis_index("c")
        me, sibling = (x, y, c), (x, y, 1 - c)
        chips = [(1 - x, y), (x, 1 - y), (1 - x, 1 - y)]

        def rows(px, py, pc):
            return out_ref.at[pl.ds((4 * px + 2 * py + pc) * m_per, m_per), :]

        def copy(k, block, to, src=None):
            return pltpu.make_async_remote_copy(
                src_ref=rows(*block) if src is None else src, dst_ref=rows(*block),
                send_sem=send_sems.at[k], recv_sem=recv_sems.at[k], device_id=to, device_id_type=MESH)

        mine = pltpu.make_async_copy(x_ref, rows(*me), local_sem)
        mine.start()
        first = [copy(0, me, sibling, src=x_ref)]
        first += [copy(1 + j, me, (*chip, c), src=x_ref) for j, chip in enumerate(chips)]
        for cp in first:
            cp.start()
        passed = [copy(4 + j, (*chip, c), sibling) for j, chip in enumerate(chips)]
        for j, chip in enumerate(chips):
            copy(1 + j, (*chip, c), me).wait_recv()
            passed[j].start()
        copy(0, sibling, me).wait_recv()
        for j, chip in enumerate(chips):
            copy(4 + j, (*chip, 1 - c), me).wait_recv()
        for cp in first + passed:
            cp.wait_send()
        mine.wait()
        for ex in beside:
            ex.wait()

    hbm = pl.BlockSpec(memory_space=pl.ANY)
    return pl.pallas_call(
        body, name=name,
        out_shape=[jax.ShapeDtypeStruct((N_DEV * m_per, n), v.dtype)]
        + [jax.ShapeDtypeStruct((N_CHIPS,) + a.shape, a.dtype) for a in gather]
        + [jax.ShapeDtypeStruct(p.shape, p.dtype) for p in scatter],
        in_specs=[pl.BlockSpec(memory_space=pltpu.VMEM)] + [hbm] * (n_g + n_s),
        out_specs=[pl.BlockSpec(memory_space=pltpu.VMEM)] + [hbm] * (n_g + n_s),
        scratch_shapes=[pltpu.SemaphoreType.DMA((7,)), pltpu.SemaphoreType.DMA((7,)), pltpu.SemaphoreType.DMA]
        + (_exchange_sems(n_g) if n_g else []) + (_exchange_sems(n_s) if n_s else []),
        compiler_params=_cparams(),
    )(v, *gather, *scatter)


class _Exchange:
    def __init__(self, local, sends, arrivals):
        self.local, self.sends, self.arrivals = local, sends, arrivals

    def start(self):
        for cp in self.local + self.sends:
            cp.start()

    def wait(self):
        for cp in self.arrivals:
            cp.wait_recv()
        for cp in self.sends:
            cp.wait_send()
        for cp in self.local:
            cp.wait()


def _exchange_sems(n):
    return [pltpu.SemaphoreType.DMA((3 * n,)), pltpu.SemaphoreType.DMA((3 * n,)), pltpu.SemaphoreType.DMA((n,))]


def _gather_exchange(ins, outs, send_sems, recv_sems, local_sems):
    x, y, c = lax.axis_index("x"), lax.axis_index("y"), lax.axis_index("c")
    slot = 2 * x + y
    chips = [(1 - x, y), (x, 1 - y), (1 - x, 1 - y)]
    local, sends, arrivals = [], [], []
    for a in range(len(ins)):
        local.append(pltpu.make_async_copy(ins[a], outs[a].at[slot], local_sems.at[a]))
        for j, (px, py) in enumerate(chips):
            sems = dict(send_sem=send_sems.at[3 * a + j], recv_sem=recv_sems.at[3 * a + j],
                        device_id=(px, py, c), device_id_type=MESH)
            sends.append(pltpu.make_async_remote_copy(src_ref=ins[a], dst_ref=outs[a].at[slot], **sems))
            arrivals.append(pltpu.make_async_remote_copy(src_ref=ins[a], dst_ref=outs[a].at[2 * px + py], **sems))
    return _Exchange(local, sends, arrivals)


def _scatter_exchange(p_refs, out_refs, send_sems, recv_sems, local_sems):
    x, y, c = lax.axis_index("x"), lax.axis_index("y"), lax.axis_index("c")
    slot = 2 * x + y
    chips = [(1 - x, y), (x, 1 - y), (1 - x, 1 - y)]
    local, sends, arrivals = [], [], []
    for a, (p_ref, out_ref) in enumerate(zip(p_refs, out_refs)):
        local.append(pltpu.make_async_copy(p_ref.at[slot], out_ref.at[slot], local_sems.at[a]))
        for j, (px, py) in enumerate(chips):
            sems = dict(send_sem=send_sems.at[3 * a + j], recv_sem=recv_sems.at[3 * a + j],
                        device_id=(px, py, c), device_id_type=MESH)
            sends.append(pltpu.make_async_remote_copy(src_ref=p_ref.at[2 * px + py], dst_ref=out_ref.at[slot], **sems))
            arrivals.append(pltpu.make_async_remote_copy(src_ref=p_ref.at[slot], dst_ref=out_ref.at[2 * px + py], **sems))
    return _Exchange(local, sends, arrivals)


def _sibling_halves(give_refs, got_refs, send_sems, recv_sems):
    x, y, c = lax.axis_index("x"), lax.axis_index("y"), lax.axis_index("c")
    copies = []
    for a, (give_ref, got_ref) in enumerate(zip(give_refs, got_refs)):
        for s in range(N_CHIPS):
            copies.append(pltpu.make_async_remote_copy(
                src_ref=give_ref.at[s, 1 - c], dst_ref=got_ref.at[s], send_sem=send_sems.at[N_CHIPS * a + s],
                recv_sem=recv_sems.at[N_CHIPS * a + s], device_id=(x, y, 1 - c), device_id_type=MESH))
    return copies


def _halves_shapes(arrs):
    return [jax.ShapeDtypeStruct((a.shape[0],) + a.shape[2:], a.dtype) for a in arrs]


def _halves_sems(n):
    return [pltpu.SemaphoreType.DMA((N_CHIPS * n,)), pltpu.SemaphoreType.DMA((N_CHIPS * n,))]


def _halves_swap(arrs, name):
    n = len(arrs)

    def body(*refs):
        copies = _sibling_halves(refs[:n], refs[n:2 * n], *refs[2 * n:])
        for cp in copies:
            cp.start()
        for cp in copies:
            cp.wait()

    hbm = pl.BlockSpec(memory_space=pl.ANY)
    return pl.pallas_call(body, name=name, out_shape=_halves_shapes(arrs), in_specs=[hbm] * n, out_specs=[hbm] * n,
                          scratch_shapes=_halves_sems(n), compiler_params=_cparams())(*arrs)


def _sibling_send(arrs, name):
    n = len(arrs)

    def body(*refs):
        x, y, c = lax.axis_index("x"), lax.axis_index("y"), lax.axis_index("c")
        send_sems, recv_sems = refs[2 * n:]
        copies = [pltpu.make_async_remote_copy(src_ref=refs[a], dst_ref=refs[n + a], send_sem=send_sems.at[a],
                                               recv_sem=recv_sems.at[a], device_id=(x, y, 1 - c), device_id_type=MESH)
                  for a in range(n)]
        for cp in copies:
            cp.start()
        for cp in copies:
            cp.wait()

    hbm = pl.BlockSpec(memory_space=pl.ANY)
    return pl.pallas_call(
        body, name=name, out_shape=[jax.ShapeDtypeStruct(a.shape, a.dtype) for a in arrs],
        in_specs=[hbm] * n, out_specs=[hbm] * n,
        scratch_shapes=[pltpu.SemaphoreType.DMA((n,)), pltpu.SemaphoreType.DMA((n,))],
        compiler_params=_cparams(),
    )(*arrs)


def _row_tile(h):
    return h // 2 if (h // 2) % 8 == 0 else h


def _chip_sum(arr, got, core, name):
    _, _, h, cols = arr.shape
    tr = _row_tile(h)

    def body(core_ref, a_ref, b_ref, o_ref):
        o_ref[...] = a_ref[...] + b_ref[...]

    slab = pl.BlockSpec((None, tr, cols), lambda s, i, core_ref: (s, i, 0))
    grid_spec = pltpu.PrefetchScalarGridSpec(
        num_scalar_prefetch=1, grid=(N_CHIPS, h // tr),
        in_specs=[pl.BlockSpec((None, None, tr, cols), lambda s, i, core_ref: (s, core_ref[0], i, 0)), slab],
        out_specs=slab)
    return pl.pallas_call(body, name=name, grid_spec=grid_spec, out_shape=jax.ShapeDtypeStruct(got.shape, got.dtype),
                          compiler_params=_cparams())(core, arr, got)


def _sum4(p, name):
    _, h, cols = p.shape
    tr = _row_tile(h)

    def body(p_ref, o_ref):
        o_ref[...] = ((p_ref[0] + p_ref[1]) + p_ref[2]) + p_ref[3]

    return pl.pallas_call(
        body, name=name, grid=(h // tr,), out_shape=jax.ShapeDtypeStruct((h, cols), p.dtype),
        in_specs=[pl.BlockSpec((4, tr, cols), lambda i: (0, i, 0))],
        out_specs=pl.BlockSpec((tr, cols), lambda i: (i, 0)), compiler_params=_cparams())(p)


def _adam_math(w, g, m, v):
    m2 = ADAM_B1 * m + (1.0 - ADAM_B1) * g
    v2 = ADAM_B2 * v + (1.0 - ADAM_B2) * (g * g)
    m_hat = m2 / (1.0 - ADAM_B1 ** ADAM_STEP)
    v_hat = v2 / (1.0 - ADAM_B2 ** ADAM_STEP)
    delta = -ADAM_LR * (m_hat / (jnp.sqrt(v_hat) + ADAM_EPS) + ADAM_WD * w)
    return delta, m2, v2


def _adamw(w, g, m, v, name):
    rows, cols = w.shape
    tr = rows // 4 if rows % 32 == 0 else rows

    def body(w_ref, g_ref, m_ref, v_ref, d_ref, m2_ref, v2_ref):
        delta, m2, v2 = _adam_math(w_ref[...], g_ref[...], m_ref[...], v_ref[...])
        d_ref[...] = delta
        m2_ref[...] = m2
        v2_ref[...] = v2

    spec = pl.BlockSpec((tr, cols), lambda i: (i, 0))
    shp = jax.ShapeDtypeStruct(w.shape, F32)
    return pl.pallas_call(body, name=name, grid=(rows // tr,), out_shape=[shp, shp, shp],
                          in_specs=[spec] * 4, out_specs=[spec] * 3, compiler_params=_cparams())(w, g, m, v)


def _small_update(g8, offsets, ws, ms, vs, loss_at, name):
    k = len(ws)

    def summed(g8_ref, lo, width):
        g = g8_ref[0:1, lo:lo + width]
        for r in range(1, N_DEV):
            g = g + g8_ref[r:r + 1, lo:lo + width]
        return g

    def body(g8_ref, *refs):
        ins, outs = refs[:3 * k], refs[3 * k:]
        for j in range(k):
            g = summed(g8_ref, offsets[j], ws[j].shape[1])
            delta, m2, v2 = _adam_math(ins[j][...], g, ins[k + j][...], ins[2 * k + j][...])
            for kind, val in enumerate((g, delta, m2, v2)):
                outs[kind * k + j][...] = val
        outs[4 * k][...] = summed(g8_ref, loss_at, LANES)

    vm = pl.BlockSpec(memory_space=pltpu.VMEM)
    shapes = [jax.ShapeDtypeStruct(w.shape, F32) for w in ws] * 4 + [jax.ShapeDtypeStruct((1, LANES), F32)]
    res = pl.pallas_call(body, name=name, out_shape=shapes, in_specs=[vm] * (1 + 3 * k), out_specs=[vm] * (4 * k + 1),
                         compiler_params=_cparams())(g8, *ws, *ms, *vs)
    return res[:k], res[k:2 * k], res[2 * k:3 * k], res[3 * k:4 * k], res[4 * k]


def _mod_shard(c8, w_ada, b_ada_shard, name):
    n = w_ada.shape[1]
    tn = 512

    def body(c_ref, w_ref, b_ref, o_ref, s_ref):
        cv = c_ref[...]
        sc = cv * (1.0 / (1.0 + jnp.exp(-cv)))
        s_ref[...] = sc
        o_ref[...] = _dot(sc.astype(BF16), w_ref[...].astype(BF16)) + b_ref[...]

    return pl.pallas_call(
        body, name=name, grid=(n // tn,),
        out_shape=[jax.ShapeDtypeStruct((8, n), F32), jax.ShapeDtypeStruct((8, D), F32)],
        in_specs=[pl.BlockSpec((8, D), lambda j: (0, 0)), pl.BlockSpec((D, tn), lambda j: (0, j)),
                  pl.BlockSpec((1, tn), lambda j: (0, j))],
        out_specs=[pl.BlockSpec((8, tn), lambda j: (0, j)), pl.BlockSpec((8, D), lambda j: (0, 0))],
        compiler_params=_cparams())(c8, w_ada, b_ada_shard)


def _layer_norm_stats(u):
    mu = jnp.mean(u, axis=1, keepdims=True)
    d = u - mu
    var = jnp.mean(d * d, axis=1, keepdims=True)
    rstd = lax.rsqrt(var + LN_EPS)
    return d * rstd, rstd


def _in_proj(x, vec, w_in, b_in, name):
    s = x.shape[0]
    tb = min(TOK_TILE, s)

    def body(x_ref, vec_ref, w_ref, b_ref, ht_ref, p_ref):
        h = x_ref[...] * (1.0 + vec_ref[V_SC_A:V_SC_A + 1, :]) + vec_ref[V_SH_A:V_SH_A + 1, :]
        hb = h.astype(BF16)
        ht_ref[...] = h.T.astype(BF16)
        proj = _dot(hb, w_ref[...]) + b_ref[...]
        col = lax.broadcasted_iota(jnp.int32, (1, D_IN), 1)
        is_q = (col < SB_W) | ((col >= 3 * SB_W) & (col < 3 * SB_W + SWA_QW))
        p_ref[...] = (proj * jnp.where(is_q, QK_SCALE, 1.0)).astype(BF16)

    return pl.pallas_call(
        body, name=name, grid=(s // tb,),
        out_shape=[jax.ShapeDtypeStruct((D, s), BF16), jax.ShapeDtypeStruct((s, D_IN), BF16)],
        in_specs=[pl.BlockSpec((tb, D), lambda i: (i, 0)), _resident((VEC_ROWS, D)), _resident((D, D_IN)),
                  _resident((1, D_IN))],
        out_specs=[pl.BlockSpec((D, tb), lambda i: (0, i)), pl.BlockSpec((tb, D_IN), lambda i: (i, 0))],
        compiler_params=_cparams())(x, vec, w_in, b_in)


def _softplus_parts(z):
    e1 = jnp.exp(-jnp.abs(z))
    sp = jnp.maximum(z, 0.0) + jnp.log(1.0 + e1)
    return sp, e1


def _sb_forward(proj, shards, name):
    s = proj.shape[0]
    tq, tk = min(SB_TQ, s), min(SB_TK, s)
    r = tq // tk

    n_sh = len(shards)
    nkb = SB_W // LANES
    nq = s // tq

    assert r % SB_UNROLL == 0, "the sweep below the diagonal takes whole steps"

    def body(q_ref, k_ref, v_ref, *refs):
        sh_refs, (o_ref, tot_ref, start_ref), got_refs = refs[:n_sh], refs[n_sh:n_sh + 3], refs[n_sh + 3:2 * n_sh + 3]
        acc_refs, run_refs = refs[2 * n_sh + 3:2 * n_sh + 5]
        i = pl.program_id(1)
        step = pl.program_id(0) * nq + i
        gather = _gather_exchange(sh_refs, got_refs, *refs[2 * n_sh + 5:])

        @pl.when(step == 0)
        def _():
            gather.start()

        lane = lax.broadcasted_iota(jnp.int32, (1, LANES), 1)
        first = lane < HEAD_DIM
        qp = q_ref[...]
        zero = jnp.zeros((), BF16)
        qs = (jnp.where(first, qp, zero), jnp.where(first, zero, qp))
        later = _sum_matrix(tk, lambda row, col: row > col)
        acc_refs[...] = jnp.zeros_like(acc_refs)
        run_refs[...] = jnp.zeros_like(run_refs)

        def blocks(tiles):
            rows = [slice(r0, r0 + n) for r0, n, _, _ in tiles]
            kjs = [k_ref[pl.ds(pl.multiple_of(j * tk, tk), tk), :] for _, _, j, _ in tiles]
            vjs = [v_ref[pl.ds(pl.multiple_of(j * tk, tk), tk), :] for _, _, j, _ in tiles]
            chains = [(hd, t) for t in range(len(tiles)) for hd in range(2)]
            zs = [_dot_nt(qs[hd][rows[t]], kjs[t]) for hd, t in chains]
            sps = [_softplus_parts(z)[0] for z in zs]
            befores = [_before(i * tq + r0, n, j * tk, tk) if diag else None for r0, n, j, diag in tiles]
            spms = [sp if befores[t] is None else jnp.where(befores[t], sp, 0.0) for (hd, t), sp in zip(chains, sps)]
            cums = [_block_sums(spm, later) for spm in spms]
            runs, ws = {}, []
            for (hd, t), z, sp, (cum, sm) in zip(chains, zs, sps, cums):
                key = (hd, tiles[t][0])
                if key not in runs:
                    runs[key] = run_refs[hd, rows[t], :]
                w = jnp.exp(z - sp - cum - _across(runs[key], tk))
                if befores[t] is not None:
                    w = jnp.where(befores[t], w, 0.0)
                ws.append(w.astype(BF16))
                runs[key] = runs[key] + sm
            sums = {}
            for (hd, t), w in zip(chains, ws):
                key, pv = (hd, tiles[t][0]), _dot(w, vjs[t])
                sums[key] = pv if key not in sums else sums[key] + pv
            for (hd, r0), run in runs.items():
                span = slice(r0, r0 + run.shape[0])
                acc_refs[hd, span, :] += sums[(hd, r0)]
                run_refs[hd, span, :] = run

        blocks([(d * tk, tk, i * r + e, e == d) for d in range(r) for e in range(d, -1, -1)])

        below = i * r

        def swept_mass():
            return jnp.min(jnp.minimum(run_refs[0], run_refs[1]))

        def more(carry):
            n, mass = carry
            return (n < below // SB_UNROLL) & (mass < SB_DEAD_MASS)

        def sweep(carry):
            n, _ = carry
            top = below - 1 - SB_UNROLL * n
            blocks([(0, tq, top - u, False) for u in range(SB_UNROLL)])
            return n + 1, swept_mass()

        n_swept, _ = lax.while_loop(more, sweep, (0, swept_mass()))
        start_ref[pl.program_id(0), i] = (below - SB_UNROLL * n_swept).astype(F32)
        o_ref[...] = jnp.where(first, acc_refs[0], acc_refs[1])
        tot_ref[...] = jnp.where(first, run_refs[0], run_refs[1])

        @pl.when(step == nkb * nq - 1)
        def _():
            gather.wait()

    shp = jax.ShapeDtypeStruct((s, SB_W), F32)
    qspec = pl.BlockSpec((tq, LANES), lambda p, i: (i, p))
    hbm = pl.BlockSpec(memory_space=pl.ANY)
    return pl.pallas_call(
        body, name=name, grid=(nkb, nq),
        out_shape=[shp, shp, jax.ShapeDtypeStruct((nkb, nq), F32)]
        + [jax.ShapeDtypeStruct((N_CHIPS,) + a.shape, a.dtype) for a in shards],
        in_specs=[qspec,
                  pl.BlockSpec((s, LANES), lambda p, i: (0, nkb + p)),
                  pl.BlockSpec((s, LANES), lambda p, i: (0, 2 * nkb + p))] + [hbm] * n_sh,
        out_specs=[qspec, qspec, pl.BlockSpec(memory_space=pltpu.SMEM)] + [hbm] * n_sh,
        scratch_shapes=[pltpu.VMEM((2, tq, LANES), F32), pltpu.VMEM((2, tq, LANES), F32)] + _exchange_sems(n_sh),
        compiler_params=_cparams())(proj, proj, proj, *shards)


def _swa_masks(n):
    ti = lax.broadcasted_iota(jnp.int32, (WINDOW, 2 * WINDOW), 0)
    kj = lax.broadcasted_iota(jnp.int32, (WINDOW, 2 * WINDOW), 1)
    dist = ti + WINDOW - kj
    valid = (dist >= 0) & (dist < WINDOW) & ((n * WINDOW - WINDOW + kj) >= 0)
    return valid, dist.astype(F32)


def _swa_probs(sc, valid, distf, h, sink):
    slope = 2.0 ** (-(h + 1))
    sc = jnp.where(valid, sc - slope * distf, MASK_VALUE)
    mx = jnp.maximum(jnp.max(sc, axis=1, keepdims=True), sink)
    p = jnp.exp(sc - mx)
    es = jnp.exp(sink - mx)
    inv = 1.0 / (jnp.sum(p, axis=1, keepdims=True) + es)
    return p * inv, es * inv


def _swa_forward(proj, sinks, shards, name):
    s = proj.shape[0]
    nb = s // WINDOW
    qb, kb, vb = 3 * SB_W // SWA_QW, (3 * SB_W + SWA_QW) // LANES, (3 * SB_W + SWA_QW + SWA_KW) // LANES
    n_sh = len(shards)

    def body(q_ref, kp_ref, kc_ref, vp_ref, vc_ref, sink_ref, *refs):
        sh_refs, o_ref, got_refs = refs[:n_sh], refs[n_sh], refs[n_sh + 1:2 * n_sh + 1]
        n = pl.program_id(0)
        gather = _gather_exchange(sh_refs, got_refs, *refs[2 * n_sh + 1:])

        @pl.when(n == 0)
        def _():
            gather.start()

        k = jnp.concatenate([kp_ref[...], kc_ref[...]], axis=0)
        v = jnp.concatenate([vp_ref[...], vc_ref[...]], axis=0)
        k_sw = pltpu.roll(k.astype(F32), HEAD_DIM, 1).astype(BF16)
        v_sw = pltpu.roll(v.astype(F32), HEAD_DIM, 1).astype(BF16)
        lane = lax.broadcasted_iota(jnp.int32, (1, LANES), 1)
        halves = [lane < HEAD_DIM, lane >= HEAD_DIM]
        valid, distf = _swa_masks(n)
        heads = range(2 * 4)
        qms = [jnp.where(halves[h % 2], q_ref[:, (h // 2) * LANES:(h // 2 + 1) * LANES], jnp.zeros((), BF16))
               for h in heads]
        kus = [k if h // 4 == h % 2 else k_sw for h in heads]
        vus = [v if h // 4 == h % 2 else v_sw for h in heads]
        scores = [_dot_nt(qms[h], kus[h]) for h in heads]
        ps = [_swa_probs(scores[h], valid, distf, h, sink_ref[h])[0].astype(BF16) for h in heads]
        outs = [_dot(ps[h], vus[h]) for h in heads]
        for pair in range(4):
            o_ref[:, pair * LANES:(pair + 1) * LANES] = jnp.where(halves[0], outs[2 * pair], outs[2 * pair + 1])

        @pl.when(n == nb - 1)
        def _():
            gather.wait()

    prev = lambda n: jnp.maximum(n - 1, 0)
    hbm = pl.BlockSpec(memory_space=pl.ANY)
    return pl.pallas_call(
        body, name=name, grid=(nb,),
        out_shape=[jax.ShapeDtypeStruct((s, SWA_QW), F32)]
        + [jax.ShapeDtypeStruct((N_CHIPS,) + a.shape, a.dtype) for a in shards],
        in_specs=[pl.BlockSpec((WINDOW, SWA_QW), lambda n: (n, qb)),
                  pl.BlockSpec((WINDOW, LANES), lambda n: (prev(n), kb)),
                  pl.BlockSpec((WINDOW, LANES), lambda n: (n, kb)),
                  pl.BlockSpec((WINDOW, LANES), lambda n: (prev(n), vb)),
                  pl.BlockSpec((WINDOW, LANES), lambda n: (n, vb)),
                  pl.BlockSpec(memory_space=pltpu.SMEM)] + [hbm] * n_sh,
        out_specs=[pl.BlockSpec((WINDOW, SWA_QW), lambda n: (n, 0))] + [hbm] * n_sh,
        scratch_shapes=_exchange_sems(n_sh),
        compiler_params=_cparams())(proj, proj, proj, proj, proj, sinks, *shards)


def _rms_parts(y):
    return lax.rsqrt(jnp.mean(y * y, axis=1, keepdims=True) + RMS_EPS)


def _post_attention(y_sb, y_sw, x, vec, w_out, name):
    s = x.shape[0]
    tb = min(TOK_TILE, s)

    def body(ysb_ref, ysw_ref, x_ref, vec_ref, w_ref, mixedt_ref, attn_ref, x1_ref, h2_ref, h2t_ref):
        ysb, ysw = ysb_ref[...], ysw_ref[...]
        nsb_f = ysb * _rms_parts(ysb) * vec_ref[V_GN:V_GN + 1, :SB_W]
        nsw_f = ysw * _rms_parts(ysw) * vec_ref[V_GN:V_GN + 1, SB_W:]
        nsb, nsw = nsb_f.astype(BF16), nsw_f.astype(BF16)
        mixedt_ref[:SB_W, :] = nsb_f.T.astype(BF16)
        mixedt_ref[SB_W:, :] = nsw_f.T.astype(BF16)
        attn = _dot(nsb, w_ref[:SB_W, :]) + _dot(nsw, w_ref[SB_W:, :])
        attn_ref[...] = attn
        u1 = ALPHA * x_ref[...] + (1.0 + vec_ref[V_G_A:V_G_A + 1, :]) * attn
        xhat, _ = _layer_norm_stats(u1)
        x1 = xhat * vec_ref[V_LN1G:V_LN1G + 1, :] + vec_ref[V_LN1B:V_LN1B + 1, :]
        x1_ref[...] = x1
        h2 = x1 * (1.0 + vec_ref[V_SC_F:V_SC_F + 1, :]) + vec_ref[V_SH_F:V_SH_F + 1, :]
        h2_ref[...] = h2.astype(BF16)
        h2t_ref[...] = h2.T.astype(BF16)

    half = pl.BlockSpec((tb, SB_W), lambda i: (i, 0))
    full = pl.BlockSpec((tb, D), lambda i: (i, 0))
    full_t = pl.BlockSpec((D, tb), lambda i: (0, i))
    return pl.pallas_call(
        body, name=name, grid=(s // tb,),
        out_shape=[jax.ShapeDtypeStruct((D, s), BF16), jax.ShapeDtypeStruct((s, D), F32),
                   jax.ShapeDtypeStruct((s, D), F32), jax.ShapeDtypeStruct((s, D), BF16),
                   jax.ShapeDtypeStruct((D, s), BF16)],
        in_specs=[half, half, full, _resident((VEC_ROWS, D)), _resident((D, D))],
        out_specs=[full_t, full, full, full, full_t],
        compiler_params=_cparams())(y_sb, y_sw, x, vec, w_out)


def _ffn_forward(h2, w_gu, w_down, name):
    s = h2.shape[0]
    tb = min(FFN_TILE, s)

    def body(h_ref, wgu_ref, wd_ref, gu_ref, actt_ref, ffn_ref):
        gu = _dot(h_ref[...], wgu_ref[...])
        gu_ref[...] = gu.astype(BF16)
        gate, up = gu[:, :D_FF], gu[:, D_FF:]
        act = gate * (1.0 / (1.0 + jnp.exp(-gate))) * up
        actt_ref[...] = act.T.astype(BF16)
        ffn_ref[...] = _dot(act.astype(BF16), wd_ref[...])

    return pl.pallas_call(
        body, name=name, grid=(s // tb,),
        out_shape=[jax.ShapeDtypeStruct((s, 2 * D_FF), BF16), jax.ShapeDtypeStruct((D_FF, s), BF16),
                   jax.ShapeDtypeStruct((s, D), F32)],
        in_specs=[pl.BlockSpec((tb, D), lambda i: (i, 0)), _resident((D, 2 * D_FF)), _resident((D_FF, D))],
        out_specs=[pl.BlockSpec((tb, 2 * D_FF), lambda i: (i, 0)), pl.BlockSpec((D_FF, tb), lambda i: (0, i)),
                   pl.BlockSpec((tb, D), lambda i: (i, 0))],
        compiler_params=_cparams())(h2, w_gu, w_down)


def _layer_norm_bwd(dxhat, xhat, rstd):
    m1 = jnp.mean(dxhat, axis=1, keepdims=True)
    m2 = jnp.mean(dxhat * xhat, axis=1, keepdims=True)
    return rstd * (dxhat - m1 - xhat * m2)


def _colsum(a):
    return jnp.sum(a, axis=0, keepdims=True)


A_LN2G, A_LN2B, A_GF, A_SCF, A_SHF, A_LOSS = range(6)
B_LN1G, B_LN1B, B_GA, B_GN = range(4)
C_SCA, C_SHA = range(2)


def _ffn_backward(x1, ffn, target, gu, vec, w_gu, w_down, name):
    s = x1.shape[0]
    tb = min(FFN_BWD_TILE, s)

    def body(x1_ref, ffn_ref, t_ref, gu_ref, vec_ref, wgu_ref, wd_ref, dffn_ref, dgu_ref, dx1_ref, acc_ref):
        @pl.when(pl.program_id(0) == 0)
        def _():
            acc_ref[...] = jnp.zeros_like(acc_ref)

        x1v, ffn_v = x1_ref[...], ffn_ref[...]
        g_f = 1.0 + vec_ref[V_G_F:V_G_F + 1, :]
        u2 = ALPHA * x1v + g_f * ffn_v
        xhat, rstd = _layer_norm_stats(u2)
        ln_g = vec_ref[V_LN2G:V_LN2G + 1, :]
        err = xhat * ln_g + vec_ref[V_LN2B:V_LN2B + 1, :] - t_ref[...]
        dx2 = err * (1.0 / D)
        acc_ref[A_LOSS:A_LOSS + 1, :] += _colsum(err * err) * (0.5 / D)
        acc_ref[A_LN2G:A_LN2G + 1, :] += _colsum(dx2 * xhat)
        acc_ref[A_LN2B:A_LN2B + 1, :] += _colsum(dx2)
        du2 = _layer_norm_bwd(dx2 * ln_g, xhat, rstd)
        acc_ref[A_GF:A_GF + 1, :] += _colsum(du2 * ffn_v)
        dffn = (g_f * du2).astype(BF16)
        dffn_ref[...] = dffn
        dact = _dot_nt(dffn, wd_ref[...])
        gate, up = gu_ref[:, :D_FF].astype(F32), gu_ref[:, D_FF:].astype(F32)
        sg = 1.0 / (1.0 + jnp.exp(-gate))
        dgate = (dact * up * (sg * (1.0 + gate * (1.0 - sg)))).astype(BF16)
        dup = (dact * (gate * sg)).astype(BF16)
        dgu_ref[:, :D_FF] = dgate
        dgu_ref[:, D_FF:] = dup
        dh2 = _dot_nt(dgate, wgu_ref[:, :D_FF]) + _dot_nt(dup, wgu_ref[:, D_FF:])
        dx1_ref[...] = ALPHA * du2 + dh2 * (1.0 + vec_ref[V_SC_F:V_SC_F + 1, :])
        acc_ref[A_SCF:A_SCF + 1, :] += _colsum(dh2 * x1v)
        acc_ref[A_SHF:A_SHF + 1, :] += _colsum(dh2)

    full = pl.BlockSpec((tb, D), lambda i: (i, 0))
    wide = pl.BlockSpec((tb, 2 * D_FF), lambda i: (i, 0))
    return pl.pallas_call(
        body, name=name, grid=(s // tb,),
        out_shape=[jax.ShapeDtypeStruct((s, D), BF16), jax.ShapeDtypeStruct((s, 2 * D_FF), BF16),
                   jax.ShapeDtypeStruct((s, D), F32), jax.ShapeDtypeStruct((8, D), F32)],
        in_specs=[full, full, full, wide, _resident((VEC_ROWS, D)), _resident((D, 2 * D_FF)), _resident((D_FF, D))],
        out_specs=[full, wide, full, pl.BlockSpec((8, D), lambda i: (0, 0))],
        compiler_params=_cparams())(x1, ffn, target, gu, vec, w_gu, w_down)


def _attn_out_backward(dx1, x, attn, y_sb, y_sw, vec, w_out, name):
    s = x.shape[0]
    tb = min(TOK_TILE, s)

    def body(dx1_ref, x_ref, attn_ref, ysb_ref, ysw_ref, vec_ref, w_ref, du1_ref, dattn_ref, dy_ref, acc_ref):
        @pl.when(pl.program_id(0) == 0)
        def _():
            acc_ref[...] = jnp.zeros_like(acc_ref)

        attn = attn_ref[...]
        g_a = 1.0 + vec_ref[V_G_A:V_G_A + 1, :]
        xhat, rstd = _layer_norm_stats(ALPHA * x_ref[...] + g_a * attn)
        dx1v = dx1_ref[...]
        acc_ref[B_LN1G:B_LN1G + 1, :] += _colsum(dx1v * xhat)
        acc_ref[B_LN1B:B_LN1B + 1, :] += _colsum(dx1v)
        du1 = _layer_norm_bwd(dx1v * vec_ref[V_LN1G:V_LN1G + 1, :], xhat, rstd)
        du1_ref[...] = du1
        acc_ref[B_GA:B_GA + 1, :] += _colsum(du1 * attn)
        dattn = (g_a * du1).astype(BF16)
        dattn_ref[...] = dattn
        dmixed = _dot_nt(dattn, w_ref[...])
        for lo, y_ref in ((0, ysb_ref), (SB_W, ysw_ref)):
            y = y_ref[...]
            rr = _rms_parts(y)
            dn = dmixed[:, lo:lo + SB_W]
            acc_ref[B_GN:B_GN + 1, lo:lo + SB_W] += _colsum(dn * y * rr)
            dng = dn * vec_ref[V_GN:V_GN + 1, lo:lo + SB_W]
            dy_ref[:, lo:lo + SB_W] = rr * dng - y * (rr * rr * rr) * jnp.mean(dng * y, axis=1, keepdims=True)

    half = pl.BlockSpec((tb, SB_W), lambda i: (i, 0))
    full = pl.BlockSpec((tb, D), lambda i: (i, 0))
    return pl.pallas_call(
        body, name=name, grid=(s // tb,),
        out_shape=[jax.ShapeDtypeStruct((s, D), F32), jax.ShapeDtypeStruct((s, D), BF16),
                   jax.ShapeDtypeStruct((s, D), F32), jax.ShapeDtypeStruct((8, D), F32)],
        in_specs=[full, full, full, half, half, _resident((VEC_ROWS, D)), _resident((D, D))],
        out_specs=[full, full, full, pl.BlockSpec((8, D), lambda i: (0, 0))],
        compiler_params=_cparams())(dx1, x, attn, y_sb, y_sw, vec, w_out)


def _sb_backward(proj, sp_total, sweep_start, dy, slabs, name):
    s = proj.shape[0]
    tq, tk = min(SB_TQ, s), min(SB_TK, s)
    r = tq // tk
    nkb = SB_W // LANES
    nq = s // tq

    assert r % SB_UNROLL == 0, "the sweep below the diagonal takes whole steps"

    n_sl = len(slabs)

    def body(q_ref, k_ref, v_ref, tot_ref, do_ref, start_ref, *refs):
        slab_refs, (dq_ref, dk_ref, dv_ref), got_refs = refs[:n_sl], refs[n_sl:n_sl + 3], refs[n_sl + 3:2 * n_sl + 3]
        dq_acc, left_refs, gsum_refs = refs[2 * n_sl + 3:2 * n_sl + 6]
        i = pl.program_id(1)
        step = pl.program_id(0) * nq + i
        scatter = _scatter_exchange(slab_refs, got_refs, *refs[2 * n_sl + 6:])

        @pl.when(step == 0)
        def _():
            scatter.start()

        @pl.when(i == 0)
        def _():
            dk_ref[...] = jnp.zeros_like(dk_ref)
            dv_ref[...] = jnp.zeros_like(dv_ref)

        lane = lax.broadcasted_iota(jnp.int32, (1, LANES), 1)
        first = lane < HEAD_DIM
        qp, dop, totp = q_ref[...], do_ref[...], tot_ref[...]
        zero = jnp.zeros((), BF16)
        qs = (jnp.where(first, qp, zero), jnp.where(first, zero, qp))
        dofs = (jnp.where(first, dop, 0.0), jnp.where(first, 0.0, dop))
        dobs = tuple(d.astype(BF16) for d in dofs)
        dots = tuple(d.T.astype(BF16) for d in dofs)
        qts = tuple(qh.astype(F32).T.astype(BF16) for qh in qs)
        later = _sum_matrix(tk, lambda row, col: row > col)
        earlier = _sum_matrix(tk, lambda row, col: row < col)
        dq_acc[...] = jnp.zeros_like(dq_acc)
        gsum_refs[...] = jnp.zeros_like(gsum_refs)
        swapped = pltpu.roll(totp, HEAD_DIM, 1)
        left_refs[0] = jnp.where(first, totp, swapped)
        left_refs[1] = jnp.where(first, swapped, totp)

        def blocks(tiles):
            rows = [slice(r0, r0 + n) for r0, n, _, _ in tiles]
            kjs = [k_ref[pl.ds(pl.multiple_of(j * tk, tk), tk), :] for _, _, j, _ in tiles]
            vjs = [v_ref[pl.ds(pl.multiple_of(j * tk, tk), tk), :] for _, _, j, _ in tiles]
            chains = [(hd, t) for t in range(len(tiles)) for hd in range(2)]
            keys = [(hd, tiles[t][0]) for hd, t in chains]
            zs = [_dot_nt(qs[hd][rows[t]], kjs[t]) for hd, t in chains]
            dws = [_dot_nt(dobs[hd][rows[t]], vjs[t]) for hd, t in chains]
            parts = [_softplus_parts(z) for z in zs]
            sps = [p[0] for p in parts]
            befores = [_before(i * tq + r0, n, j * tk, tk) if diag else None for r0, n, j, diag in tiles]
            spms = [sp if befores[t] is None else jnp.where(befores[t], sp, 0.0) for (hd, t), sp in zip(chains, sps)]
            cums = [_block_sums(spm, later) for spm in spms]
            lefts, ws = {}, []
            for key, (hd, t), z, sp, (cum, sm) in zip(keys, chains, zs, sps, cums):
                if key not in lefts:
                    lefts[key] = left_refs[hd, rows[t], :]
                lefts[key] = lefts[key] - sm
                w = jnp.exp(z - sp - cum - _across(lefts[key], tk))
                ws.append(w if befores[t] is None else jnp.where(befores[t], w, 0.0))
            wbs = [w.astype(BF16) for w in ws]
            dvs = [_dot(dots[hd][:, rows[t]], wb) for (hd, t), wb in zip(chains, wbs)]
            gs = [dw * w for dw, w in zip(dws, ws)]
            gcums = [_block_sums(g, earlier) for g in gs]
            gsums, dzbs = {}, []
            for key, (hd, t), z, (sp, e1), g, (gcum, gsm) in zip(keys, chains, zs, parts, gs, gcums):
                if key not in gsums:
                    gsums[key] = gsum_refs[hd, rows[t], :]
                inv = 1.0 / (1.0 + e1)
                sig = jnp.where(z >= 0.0, inv, e1 * inv)
                dz = g - sig * (g + _across(gsums[key], tk) + gcum)
                dzbs.append((dz if befores[t] is None else jnp.where(befores[t], dz, 0.0)).astype(BF16))
                gsums[key] = gsums[key] + gsm
            dqs = [_dot(dzb, kjs[t]) for (hd, t), dzb in zip(chains, dzbs)]
            dks = [_dot(qts[hd][:, rows[t]], dzb) for (hd, t), dzb in zip(chains, dzbs)]
            for t, (_, _, j, _) in enumerate(tiles):
                dv_ref[j] += dvs[2 * t] + dvs[2 * t + 1]
                dk_ref[j] += dks[2 * t] + dks[2 * t + 1]
            totals = {}
            for key, dq in zip(keys, dqs):
                totals[key] = dq if key not in totals else totals[key] + dq
            for (hd, r0), tot in totals.items():
                span = slice(r0, r0 + tot.shape[0])
                dq_acc[hd, span, :] += tot
                left_refs[hd, span, :] = lefts[(hd, r0)]
                gsum_refs[hd, span, :] = gsums[(hd, r0)]

        below = i * r
        start = jnp.clip(start_ref[pl.program_id(0), i].astype(jnp.int32), 0, below) // SB_UNROLL * SB_UNROLL

        def sweep(n, carry):
            blocks([(0, tq, start + SB_UNROLL * n + u, False) for u in range(SB_UNROLL)])
            return carry

        lax.fori_loop(0, (below - start) // SB_UNROLL, sweep, 0)
        blocks([(d * tk, tk, below + e, e == d) for d in range(r) for e in range(d + 1)])
        dq_ref[...] = jnp.where(first, dq_acc[0], dq_acc[1])

        @pl.when(step == nkb * nq - 1)
        def _():
            scatter.wait()

    shp = jax.ShapeDtypeStruct((s, SB_W), F32)
    qspec = pl.BlockSpec((tq, LANES), lambda p, i: (i, p))
    whole = pl.BlockSpec((None, s // tk, LANES, tk), lambda p, i: (p, 0, 0, 0))
    shp_t = jax.ShapeDtypeStruct((nkb, s // tk, LANES, tk), F32)
    hbm = pl.BlockSpec(memory_space=pl.ANY)
    return pl.pallas_call(
        body, name=name, grid=(nkb, nq),
        out_shape=[shp, shp_t, shp_t] + [jax.ShapeDtypeStruct(p.shape, p.dtype) for p in slabs],
        in_specs=[qspec,
                  pl.BlockSpec((s, LANES), lambda p, i: (0, nkb + p)),
                  pl.BlockSpec((s, LANES), lambda p, i: (0, 2 * nkb + p)),
                  qspec, qspec, pl.BlockSpec(memory_space=pltpu.SMEM)] + [hbm] * n_sl,
        out_specs=[qspec, whole, whole] + [hbm] * n_sl,
        scratch_shapes=[pltpu.VMEM((2, tq, LANES), F32), pltpu.VMEM((2, tq, LANES), F32), pltpu.VMEM((2, tq, LANES), F32)]
        + _exchange_sems(n_sl),
        compiler_params=_cparams())(proj, proj, proj, sp_total, dy, sweep_start, *slabs)


def _swa_backward(proj, y_sw, dy, sinks, gives, name):
    s = proj.shape[0]
    nb = s // WINDOW
    qb, kb, vb = 3 * SB_W // SWA_QW, (3 * SB_W + SWA_QW) // LANES, (3 * SB_W + SWA_QW + SWA_KW) // LANES

    n_gv = len(gives)

    def body(q_ref, kp_ref, kc_ref, vp_ref, vc_ref, o_ref, do_ref, sink_ref, *refs):
        give_refs, (dq_ref, dk_ref, dv_ref, ds_ref), got_refs = refs[:n_gv], refs[n_gv:n_gv + 4], refs[n_gv + 4:2 * n_gv + 4]
        n = pl.program_id(0)
        swap = _sibling_halves(give_refs, got_refs, *refs[2 * n_gv + 4:])

        @pl.when(n == 0)
        def _():
            for cp in swap:
                cp.start()

        @pl.when(n == 0)
        def _():
            dk_ref[...] = jnp.zeros_like(dk_ref)
            dv_ref[...] = jnp.zeros_like(dv_ref)
            ds_ref[...] = jnp.zeros_like(ds_ref)

        k = jnp.concatenate([kp_ref[...], kc_ref[...]], axis=0)
        v = jnp.concatenate([vp_ref[...], vc_ref[...]], axis=0)
        k_sw = pltpu.roll(k.astype(F32), HEAD_DIM, 1).astype(BF16)
        v_sw = pltpu.roll(v.astype(F32), HEAD_DIM, 1).astype(BF16)
        lane = lax.broadcasted_iota(jnp.int32, (1, LANES), 1)
        halves = [lane < HEAD_DIM, lane >= HEAD_DIM]
        valid, distf = _swa_masks(n)
        heads = range(2 * 4)
        cols = [slice((h // 2) * LANES, (h // 2 + 1) * LANES) for h in heads]
        qms = [jnp.where(halves[h % 2], q_ref[:, cols[h]], jnp.zeros((), BF16)) for h in heads]
        dos = [jnp.where(halves[h % 2], do_ref[:, cols[h]], 0.0) for h in heads]
        dobs = [d.astype(BF16) for d in dos]
        native = [h // 4 == h % 2 for h in heads]
        kus = [k if native[h] else k_sw for h in heads]
        vus = [v if native[h] else v_sw for h in heads]
        scores = [_dot_nt(qms[h], kus[h]) for h in heads]
        dps = [_dot_nt(dobs[h], vus[h]) for h in heads]
        deltas = [jnp.sum(dos[h] * o_ref[:, cols[h]], axis=1, keepdims=True) for h in heads]
        probs = [_swa_probs(scores[h], valid, distf, h, sink_ref[h]) for h in heads]
        pbs = [probs[h][0].astype(BF16) for h in heads]
        dscs = [(probs[h][0] * (dps[h] - deltas[h])).astype(BF16) for h in heads]
        dqs = [_dot(dscs[h], kus[h]) for h in heads]
        dks = [_dot_tn(dscs[h], qms[h]) for h in heads]
        dvs = [_dot_tn(pbs[h], dobs[h]) for h in heads]
        for h in heads:
            ds_ref[h:h + 1, :] += jnp.zeros((1, LANES), F32) - jnp.sum(probs[h][1] * deltas[h])
        for pair in range(4):
            dq_ref[:, cols[2 * pair]] = jnp.where(halves[0], dqs[2 * pair], dqs[2 * pair + 1])

        def gathered(parts):
            nat = sum(parts[h] for h in heads if native[h])
            rot = sum(parts[h] for h in heads if not native[h])
            return nat + pltpu.roll(rot, HEAD_DIM, 1)

        dk, dv = gathered(dks), gathered(dvs)
        prev = pl.multiple_of(jnp.maximum(n - 1, 0) * WINDOW, WINDOW)
        cur = pl.multiple_of(n * WINDOW, WINDOW)
        dk_ref[pl.ds(prev, WINDOW), :] += dk[:WINDOW]
        dv_ref[pl.ds(prev, WINDOW), :] += dv[:WINDOW]
        dk_ref[pl.ds(cur, WINDOW), :] += dk[WINDOW:]
        dv_ref[pl.ds(cur, WINDOW), :] += dv[WINDOW:]

        @pl.when(n == nb - 1)
        def _():
            for cp in swap:
                cp.wait()

    prev_blk = lambda n: jnp.maximum(n - 1, 0)
    wide = pl.BlockSpec((WINDOW, SWA_QW), lambda n: (n, 0))
    whole = pl.BlockSpec((s, LANES), lambda n: (0, 0))
    hbm = pl.BlockSpec(memory_space=pl.ANY)
    return pl.pallas_call(
        body, name=name, grid=(nb,),
        out_shape=[jax.ShapeDtypeStruct((s, SWA_QW), F32), jax.ShapeDtypeStruct((s, LANES), F32),
                   jax.ShapeDtypeStruct((s, LANES), F32), jax.ShapeDtypeStruct((8, LANES), F32)] + _halves_shapes(gives),
        in_specs=[pl.BlockSpec((WINDOW, SWA_QW), lambda n: (n, qb)),
                  pl.BlockSpec((WINDOW, LANES), lambda n: (prev_blk(n), kb)),
                  pl.BlockSpec((WINDOW, LANES), lambda n: (n, kb)),
                  pl.BlockSpec((WINDOW, LANES), lambda n: (prev_blk(n), vb)),
                  pl.BlockSpec((WINDOW, LANES), lambda n: (n, vb)),
                  wide,
                  pl.BlockSpec((WINDOW, SWA_QW), lambda n: (n, 1)),
                  pl.BlockSpec(memory_space=pltpu.SMEM)] + [hbm] * n_gv,
        out_specs=[wide, whole, whole, pl.BlockSpec((8, LANES), lambda n: (0, 0))] + [hbm] * n_gv,
        scratch_shapes=_halves_sems(n_gv),
        compiler_params=_cparams())(proj, proj, proj, proj, proj, y_sw, dy, sinks, *gives)


def _in_proj_backward(dq_sb, dkt_sb, dvt_sb, dq_sw, dk_sw, dv_sw, du1, x, vec, w_in, name):
    s = x.shape[0]
    tb = min(TOK_TILE, s)
    n_pairs, _, _, tk = dkt_sb.shape

    def body(dqsb_ref, dktsb_ref, dvtsb_ref, dqsw_ref, dksw_ref, dvsw_ref, du1_ref, x_ref, vec_ref, w_ref,
             dproj_ref, gx_ref, acc_ref, bacc_ref):
        @pl.when(pl.program_id(0) == 0)
        def _():
            acc_ref[...] = jnp.zeros_like(acc_ref)
            bacc_ref[...] = jnp.zeros_like(bacc_ref)

        pieces = ((0, dqsb_ref, QK_SCALE), (3 * SB_W, dqsw_ref, QK_SCALE), (3 * SB_W + SWA_QW, dksw_ref, 1.0),
                  (3 * SB_W + SWA_QW + SWA_KW, dvsw_ref, 1.0))
        for lo, ref, scale in pieces:
            width = ref.shape[1]
            piece = ref[...] * scale
            bacc_ref[0:1, lo:lo + width] += _colsum(piece)
            dproj_ref[:, lo:lo + width] = piece.astype(BF16)
        for base, ref in ((SB_W, dktsb_ref), (2 * SB_W, dvtsb_ref)):
            for p in range(n_pairs):
                lo = base + p * LANES
                for jj in range(tb // tk):
                    piece = ref[p, jj].T
                    bacc_ref[0:1, lo:lo + LANES] += _colsum(piece)
                    dproj_ref[jj * tk:(jj + 1) * tk, lo:lo + LANES] = piece.astype(BF16)
        dh = _dot_nt(dproj_ref[...], w_ref[...])
        xv = x_ref[...]
        gx_ref[...] = ALPHA * du1_ref[...] + dh * (1.0 + vec_ref[V_SC_A:V_SC_A + 1, :])
        acc_ref[C_SCA:C_SCA + 1, :] += _colsum(dh * xv)
        acc_ref[C_SHA:C_SHA + 1, :] += _colsum(dh)

    half = pl.BlockSpec((tb, SB_W), lambda i: (i, 0))
    narrow = pl.BlockSpec((tb, LANES), lambda i: (i, 0))
    full = pl.BlockSpec((tb, D), lambda i: (i, 0))
    blocks_t = pl.BlockSpec((n_pairs, tb // tk, LANES, tk), lambda i: (0, i, 0, 0))
    return pl.pallas_call(
        body, name=name, grid=(s // tb,),
        out_shape=[jax.ShapeDtypeStruct((s, D_IN), BF16), jax.ShapeDtypeStruct((s, D), F32),
                   jax.ShapeDtypeStruct((8, D), F32), jax.ShapeDtypeStruct((8, D_IN), F32)],
        in_specs=[half, blocks_t, blocks_t, half, narrow, narrow, full, full, _resident((VEC_ROWS, D)),
                  _resident((D, D_IN))],
        out_specs=[pl.BlockSpec((tb, D_IN), lambda i: (i, 0)), full, pl.BlockSpec((8, D), lambda i: (0, 0)),
                   pl.BlockSpec((8, D_IN), lambda i: (0, 0))],
        compiler_params=_cparams())(dq_sb, dkt_sb, dvt_sb, dq_sw, dk_sw, dv_sw, du1, x, vec, w_in)


def _weight_grad(at, b, name, col_shards=1):
    m, s = at.shape
    n = b.shape[1]
    if col_shards > 1:
        tn = n // col_shards
        out_shape = jax.ShapeDtypeStruct((col_shards, m, tn), F32)
        out_spec = pl.BlockSpec((None, m, tn), lambda j, k: (j, 0, 0))
    else:
        tn = 512 if n % 512 == 0 else n
        out_shape = jax.ShapeDtypeStruct((m, n), F32)
        out_spec = pl.BlockSpec((m, tn), lambda j, k: (0, j))
    ts = min(WGRAD_TOKENS, s)
    while 2 * (m * ts * 2 + ts * tn * 2 + m * tn * 4) > WGRAD_VMEM and ts > 512:
        ts //= 2

    def body(at_ref, b_ref, o_ref):
        @pl.when(pl.program_id(1) == 0)
        def _():
            o_ref[...] = jnp.zeros_like(o_ref)

        o_ref[...] += _dot(at_ref[...], b_ref[...])

    return pl.pallas_call(
        body, name=name, grid=(n // tn, s // ts),
        out_shape=out_shape,
        in_specs=[pl.BlockSpec((m, ts), lambda j, k: (0, k)), pl.BlockSpec((ts, tn), lambda j, k: (k, j))],
        out_specs=out_spec,
        compiler_params=_cparams())(at, b)


def _pad_rows(v, rows):
    return jnp.concatenate([v, jnp.zeros((rows - v.shape[0], v.shape[1]), v.dtype)], axis=0)


def _col_shards(w, n_shards):
    r, n = w.shape
    return w.reshape(r, n_shards, n // n_shards).transpose(1, 0, 2)


def kernel(x, c, w_ada, b_ada, w_in, b_in, sinks, gn_sb, gn_swa, w_out, ln1_g, ln1_b, w_gu, w_down, ln2_g, ln2_b, loss_target, m_w_ada, m_b_ada, m_w_in, m_b_in, m_sinks, m_gn_sb, m_gn_swa, m_w_out, m_ln1_g, m_ln1_b, m_w_gu, m_w_down, m_ln2_g, m_ln2_b, v_w_ada, v_b_ada, v_w_in, v_b_in, v_sinks, v_gn_sb, v_gn_swa, v_w_out, v_ln1_g, v_ln1_b, v_w_gu, v_w_down, v_ln2_g, v_ln2_b):
    ix, iy, ic = lax.axis_index("x"), lax.axis_index("y"), lax.axis_index("c")
    chip = 2 * ix + iy
    dev = 4 * ix + 2 * iy + ic
    xs, target = x[0], loss_target[0]
    s = xs.shape[0]

    c_rows, g_in = _allgather8(_pad_rows(c, 8), "gather_c", gather=[w_in[0].astype(BF16)])
    c_all = c_rows[::8]
    n_ada = w_ada.shape[2]
    b_ada_shard = lax.dynamic_slice_in_dim(b_ada, chip * n_ada, n_ada, axis=1)
    mod_cols, silu_c = _mod_shard(c_all, w_ada[0], b_ada_shard, "mod_shard")
    mod_all = _allgather8(mod_cols, "gather_mod")[0].reshape(N_DEV, 8, n_ada)
    mod_mine = lax.dynamic_index_in_dim(mod_all, dev, axis=1, keepdims=False)
    mod = mod_mine.reshape(N_CHIPS, 2, n_ada)[:, 0].reshape(6, D)
    vec = jnp.concatenate([mod, ln1_g, ln1_b, ln2_g, ln2_b, jnp.concatenate([gn_sb, gn_swa], axis=1),
                           jnp.zeros((VEC_ROWS - 11, D), F32)], axis=0)

    w_in_b = g_in.transpose(1, 0, 2).reshape(D, D_IN)

    h_t, proj = _in_proj(xs, vec, w_in_b, b_in, "in_proj")
    y_sb, sp_total, sweep_start, g_out, g_gu = _sb_forward(
        proj, [w_out[0].astype(BF16), w_gu[0].astype(BF16)], "sb_forward")
    w_gu_b = g_gu.transpose(1, 0, 2).reshape(D, 2 * D_FF)
    w_out_b = g_out.reshape(D, D)
    sink_vec = sinks[0]
    y_sw, g_down = _swa_forward(proj, sink_vec, [w_down[0].astype(BF16)], "swa_forward")
    w_down_b = g_down.reshape(D_FF, D)
    mixed_t, attn, x1, h2_b, h2_t = _post_attention(y_sb, y_sw, xs, vec, w_out_b, "post_attention")
    gu, act_t, ffn = _ffn_forward(h2_b, w_gu_b, w_down_b, "ffn_forward")

    def in_halves(shards):
        n_sh, rows, cols = shards.shape
        return shards.reshape(n_sh, 2, rows // 2, cols)

    core = ic.reshape(1).astype(jnp.int32)
    dffn_b, dgu_b, dx1, acc_f = _ffn_backward(x1, ffn, target, gu, vec, w_gu_b, w_down_b, "ffn_backward")
    dw_gu = _weight_grad(h2_t, dgu_b, "grad_w_gu", col_shards=4)
    dw_down = _weight_grad(act_t, dffn_b, "grad_w_down")
    du1, dattn_b, dy, acc_a = _attn_out_backward(dx1, xs, attn, y_sb, y_sw, vec, w_out_b, "attn_out_backward")
    dw_out = _weight_grad(mixed_t, dattn_b, "grad_w_out")
    first = [in_halves(dw_gu), in_halves(dw_down.reshape(4, D_FF // 4, D)), in_halves(dw_out.reshape(4, D // 4, D))]
    dq_sw, dk_sw, dv_sw, dsink, *got_first = _swa_backward(proj, y_sw, dy, sink_vec, first, "swa_backward")
    sums_first = [_chip_sum(arr, got, core, "grad_chip_sum_" + nm)
                  for arr, got, nm in zip(first, got_first, ("gu", "down", "out"))]
    dq_sb, dk_sb, dv_sb, *parts_first = _sb_backward(proj, sp_total, sweep_start, dy, sums_first, "sb_backward")
    dproj_b, grad_x, acc_i, acc_b = _in_proj_backward(dq_sb, dk_sb, dv_sb, dq_sw, dk_sw, dv_sw, du1, xs, vec, w_in_b,
                                                      "in_proj_backward")
    dw_in = _weight_grad(h_t, dproj_b, "grad_w_in")
    last = [in_halves(_col_shards(dw_in, 4))]
    sums_last = [_chip_sum(last[0], _halves_swap(last, "grad_halves_swap_in")[0], core, "grad_chip_sum_in")]

    dmod = jnp.concatenate([acc_i[C_SHA:C_SHA + 1], acc_i[C_SCA:C_SCA + 1], acc_a[B_GA:B_GA + 1],
                            acc_f[A_SHF:A_SHF + 1], acc_f[A_SCF:A_SCF + 1], acc_f[A_GF:A_GF + 1]], axis=1)
    dsink_row = jnp.concatenate([dsink[:, 0].reshape(1, 8), jnp.zeros((1, LANES - 8), F32)], axis=1)
    loss_row = jnp.concatenate([jnp.sum(acc_f[A_LOSS:A_LOSS + 1], axis=1, keepdims=True),
                                jnp.zeros((1, LANES - 1), F32)], axis=1)
    small = jnp.concatenate([dmod, acc_b[0:1], acc_a[B_LN1G:B_LN1G + 1], acc_a[B_LN1B:B_LN1B + 1],
                             acc_f[A_LN2G:A_LN2G + 1], acc_f[A_LN2B:A_LN2B + 1], acc_a[B_GN:B_GN + 1],
                             dsink_row, loss_row], axis=1)
    small_rows, *parts_last = _allgather8(_pad_rows(small, 8), "gather_small", scatter=sums_last)
    small_all = small_rows[::8]

    mine = [_sum4(p, "grad_reduce_" + nm) for p, nm in zip([*parts_first, *parts_last], ("gu", "down", "out", "in"))]
    theirs = _sibling_send(mine, "grad_half_return")
    gw_gu, gw_down, gw_out, gw_in = [
        jnp.concatenate([jnp.where(ic == 0, m_, t_), jnp.where(ic == 0, t_, m_)], axis=0) for m_, t_ in zip(mine, theirs)]

    small_names = ["b_ada", "b_in", "ln1_g", "ln1_b", "ln2_g", "ln2_b", "gn_sb", "gn_swa", "sinks"]
    small_at = [SM_MOD, SM_BIN, SM_LN1G, SM_LN1B, SM_LN2G, SM_LN2B, SM_GN, SM_GN + SB_W, SM_SINK]
    *small_out, loss_row_all = _small_update(
        small_all, small_at,
        [b_ada, b_in, ln1_g, ln1_b, ln2_g, ln2_b, gn_sb, gn_swa, sinks],
        [m_b_ada, m_b_in, m_ln1_g, m_ln1_b, m_ln2_g, m_ln2_b, m_gn_sb, m_gn_swa, m_sinks],
        [v_b_ada, v_b_in, v_ln1_g, v_ln1_b, v_ln2_g, v_ln2_b, v_gn_sb, v_gn_swa, v_sinks], SM_LOSS, "small_update")
    g_small, d_small, m2_small, v2_small = [dict(zip(small_names, leaves)) for leaves in small_out]
    loss = loss_row_all[0, 0]

    dmod_cols = lax.dynamic_slice_in_dim(small_all[:, SM_MOD:SM_BIN], chip * n_ada, n_ada, axis=1)
    gw_ada = _weight_grad(_pad_rows(silu_c, LANES).astype(BF16).T, _pad_rows(dmod_cols, LANES).astype(BF16), "grad_w_ada")

    big = {}
    for nm, w, g, m, v in (("w_ada", w_ada, gw_ada, m_w_ada, v_w_ada), ("w_in", w_in, gw_in, m_w_in, v_w_in),
                           ("w_out", w_out, gw_out, m_w_out, v_w_out), ("w_gu", w_gu, gw_gu, m_w_gu, v_w_gu),
                           ("w_down", w_down, gw_down, m_w_down, v_w_down)):
        d_, m2_, v2_ = _adamw(w[0], g, m[0], v[0], "adamw_" + nm)
        big[nm] = (g[None], d_[None], m2_[None], v2_[None])

    order = ["w_ada", "b_ada", "w_in", "b_in", "sinks", "gn_sb", "gn_swa", "w_out", "ln1_g", "ln1_b", "w_gu", "w_down",
             "ln2_g", "ln2_b"]

    def leaf(nm, which):
        if nm in big:
            return big[nm][which]
        return (g_small, d_small, m2_small, v2_small)[which][nm]

    outs = [loss, grad_x[None]]
    for which in range(4):
        outs += [leaf(nm, which) for nm in order]
    return tuple(outs)
```

```python
import math

import jax
import jax.numpy as jnp
from jax import lax
from jax.experimental import pallas as pl
from jax.experimental.pallas import tpu as pltpu

F32 = jnp.float32
BF16 = jnp.bfloat16

D = 1024
HEAD_DIM = 64
SB_W = 512
SWA_QW = 512
SWA_KW = 128
D_IN = 2304
D_FF = 2816
WINDOW = 128
ALPHA = 2.0 ** 0.25
LN_EPS = 1e-5
RMS_EPS = 1e-6
MASK_VALUE = -1e30
QK_SCALE = 1.0 / math.sqrt(HEAD_DIM)

ADAM_LR = 0.001
ADAM_B1 = 0.9
ADAM_B2 = 0.999
ADAM_EPS = 1e-08
ADAM_WD = 0.01
ADAM_STEP = 10

N_CHIPS = 4
N_DEV = 8
LANES = 128

SB_TQ = 512
SB_TK = 256
SB_UNROLL = 1
SB_DEAD_MASS = 110.0
TOK_TILE = 512
FFN_TILE = 256
FFN_BWD_TILE = 256
VMEM_LIMIT = 56 * 1024 * 1024
WGRAD_TOKENS = 2048
WGRAD_VMEM = 40 * 1024 * 1024

V_SH_A, V_SC_A, V_G_A, V_SH_F, V_SC_F, V_G_F, V_LN1G, V_LN1B, V_LN2G, V_LN2B, V_GN = range(11)
VEC_ROWS = 16

SM_MOD = 0
SM_BIN = 6 * D
SM_LN1G = SM_BIN + D_IN
SM_LN1B = SM_LN1G + D
SM_LN2G = SM_LN1B + D
SM_LN2B = SM_LN2G + D
SM_GN = SM_LN2B + D
SM_SINK = SM_GN + D
SM_LOSS = SM_SINK + LANES
SM_LEN = SM_LOSS + LANES

MESH = pl.DeviceIdType.MESH


def _cparams(**kw):
    return pltpu.CompilerParams(vmem_limit_bytes=VMEM_LIMIT, **kw)


def _resident(shape):
    nd = len(shape)
    return pl.BlockSpec(shape, lambda *_: (0,) * nd, pipeline_mode=pl.Buffered(1))


def _dot(a, b):
    return jnp.dot(a, b, preferred_element_type=F32)


def _dot_nt(a, b):
    return lax.dot_general(a, b, (((1,), (1,)), ((), ())), preferred_element_type=F32)


def _dot_tn(a, b):
    return lax.dot_general(a, b, (((0,), (0,)), ((), ())), preferred_element_type=F32)


def _sum_matrix(tk, keep):
    row = lax.broadcasted_iota(jnp.int32, (tk, tk + LANES), 0)
    col = lax.broadcasted_iota(jnp.int32, (tk, tk + LANES), 1)
    return (keep(row, col) | (col >= tk)).astype(BF16)


def _block_sums(x, m):
    tk = x.shape[1]
    res = _dot(x.astype(BF16), m)
    return res[:, :tk], res[:, tk:]


def _before(t0, n, s0, tk):
    return s0 + lax.broadcasted_iota(jnp.int32, (n, tk), 1) < t0 + lax.broadcasted_iota(jnp.int32, (n, tk), 0)


def _across(v, tk):
    return jnp.concatenate([v] * (tk // LANES), axis=1)


def _allgather8(v, name, gather=(), scatter=()):
    m_per, n = v.shape
    n_g, n_s = len(gather), len(scatter)

    def body(x_ref, *refs):
        g_in, s_in = refs[:n_g], refs[n_g:n_g + n_s]
        out_ref = refs[n_g + n_s]
        g_out, s_out = refs[n_g + n_s + 1:2 * n_g + n_s + 1], refs[2 * n_g + n_s + 1:2 * (n_g + n_s) + 1]
        send_sems, recv_sems, local_sem, *more_sems = refs[2 * (n_g + n_s) + 1:]
        beside = ([_gather_exchange(g_in, g_out, *more_sems[:3])] if n_g else []) + (
            [_scatter_exchange(s_in, s_out, *more_sems[-3:])] if n_s else [])
        for ex in beside:
            ex.start()
        x, y, c = lax.axis_index("x"), lax.axis_index("y"), lax.axis_index("c")
        me, sibling = (x, y, c), (x, y, 1 - c)
        chips = [(1 - x, y), (x, 1 - y), (1 - x, 1 - y)]

        def rows(px, py, pc):
            return out_ref.at[pl.ds((4 * px + 2 * py + pc) * m_per, m_per), :]

        def copy(k, block, to, src=None):
            return pltpu.make_async_remote_copy(
                src_ref=rows(*block) if src is None else src, dst_ref=rows(*block),
                send_sem=send_sems.at[k], recv_sem=recv_sems.at[k], device_id=to, device_id_type=MESH)

        mine = pltpu.make_async_copy(x_ref, rows(*me), local_sem)
        mine.start()
        first = [copy(0, me, sibling, src=x_ref)]
        first += [copy(1 + j, me, (*chip, c), src=x_ref) for j, chip in enumerate(chips)]
        for cp in first:
            cp.start()
        passed = [copy(4 + j, (*chip, c), sibling) for j, chip in enumerate(chips)]
        for j, chip in enumerate(chips):
            copy(1 + j, (*chip, c), me).wait_recv()
            passed[j].start()
        copy(0, sibling, me).wait_recv()
        for j, chip in enumerate(chips):
            copy(4 + j, (*chip, 1 - c), me).wait_recv()
        for cp in first + passed:
            cp.wait_send()
        mine.wait()
        for ex in beside:
            ex.wait()

    hbm = pl.BlockSpec(memory_space=pl.ANY)
    return pl.pallas_call(
        body, name=name,
        out_shape=[jax.ShapeDtypeStruct((N_DEV * m_per, n), v.dtype)]
        + [jax.ShapeDtypeStruct((N_CHIPS,) + a.shape, a.dtype) for a in gather]
        + [jax.ShapeDtypeStruct(p.shape, p.dtype) for p in scatter],
        in_specs=[pl.BlockSpec(memory_space=pltpu.VMEM)] + [hbm] * (n_g + n_s),
        out_specs=[pl.BlockSpec(memory_space=pltpu.VMEM)] + [hbm] * (n_g + n_s),
        scratch_shapes=[pltpu.SemaphoreType.DMA((7,)), pltpu.SemaphoreType.DMA((7,)), pltpu.SemaphoreType.DMA]
        + (_exchange_sems(n_g) if n_g else []) + (_exchange_sems(n_s) if n_s else []),
        compiler_params=_cparams(),
    )(v, *gather, *scatter)


class _Exchange:
    def __init__(self, local, sends, arrivals):
        self.local, self.sends, self.arrivals = local, sends, arrivals

    def start(self):
        for cp in self.local + self.sends:
            cp.start()

    def wait(self):
        for cp in self.arrivals:
            cp.wait_recv()
        for cp in self.sends:
            cp.wait_send()
        for cp in self.local:
            cp.wait()


def _exchange_sems(n):
    return [pltpu.SemaphoreType.DMA((3 * n,)), pltpu.SemaphoreType.DMA((3 * n,)), pltpu.SemaphoreType.DMA((n,))]


def _gather_exchange(ins, outs, send_sems, recv_sems, local_sems):
    x, y, c = lax.axis_index("x"), lax.axis_index("y"), lax.axis_index("c")
    slot = 2 * x + y
    chips = [(1 - x, y), (x, 1 - y), (1 - x, 1 - y)]
    local, sends, arrivals = [], [], []
    for a in range(len(ins)):
        local.append(pltpu.make_async_copy(ins[a], outs[a].at[slot], local_sems.at[a]))
        for j, (px, py) in enumerate(chips):
            sems = dict(send_sem=send_sems.at[3 * a + j], recv_sem=recv_sems.at[3 * a + j],
                        device_id=(px, py, c), device_id_type=MESH)
            sends.append(pltpu.make_async_remote_copy(src_ref=ins[a], dst_ref=outs[a].at[slot], **sems))
            arrivals.append(pltpu.make_async_remote_copy(src_ref=ins[a], dst_ref=outs[a].at[2 * px + py], **sems))
    return _Exchange(local, sends, arrivals)


def _scatter_exchange(p_refs, out_refs, send_sems, recv_sems, local_sems):
    x, y, c = lax.axis_index("x"), lax.axis_index("y"), lax.axis_index("c")
    slot = 2 * x + y
    chips = [(1 - x, y), (x, 1 - y), (1 - x, 1 - y)]
    local, sends, arrivals = [], [], []
    for a, (p_ref, out_ref) in enumerate(zip(p_refs, out_refs)):
        local.append(pltpu.make_async_copy(p_ref.at[slot], out_ref.at[slot], local_sems.at[a]))
        for j, (px, py) in enumerate(chips):
            sems = dict(send_sem=send_sems.at[3 * a + j], recv_sem=recv_sems.at[3 * a + j],
                        device_id=(px, py, c), device_id_type=MESH)
            sends.append(pltpu.make_async_remote_copy(src_ref=p_ref.at[2 * px + py], dst_ref=out_ref.at[slot], **sems))
            arrivals.append(pltpu.make_async_remote_copy(src_ref=p_ref.at[slot], dst_ref=out_ref.at[2 * px + py], **sems))
    return _Exchange(local, sends, arrivals)


def _sibling_halves(give_refs, got_refs, send_sems, recv_sems):
    x, y, c = lax.axis_index("x"), lax.axis_index("y"), lax.axis_index("c")
    copies = []
    for a, (give_ref, got_ref) in enumerate(zip(give_refs, got_refs)):
        for s in range(N_CHIPS):
            copies.append(pltpu.make_async_remote_copy(
                src_ref=give_ref.at[s, 1 - c], dst_ref=got_ref.at[s], send_sem=send_sems.at[N_CHIPS * a + s],
                recv_sem=recv_sems.at[N_CHIPS * a + s], device_id=(x, y, 1 - c), device_id_type=MESH))
    return copies


def _halves_shapes(arrs):
    return [jax.ShapeDtypeStruct((a.shape[0],) + a.shape[2:], a.dtype) for a in arrs]


def _halves_sems(n):
    return [pltpu.SemaphoreType.DMA((N_CHIPS * n,)), pltpu.SemaphoreType.DMA((N_CHIPS * n,))]


def _halves_swap(arrs, name):
    n = len(arrs)

    def body(*refs):
        copies = _sibling_halves(refs[:n], refs[n:2 * n], *refs[2 * n:])
        for cp in copies:
            cp.start()
        for cp in copies:
            cp.wait()

    hbm = pl.BlockSpec(memory_space=pl.ANY)
    return pl.pallas_call(body, name=name, out_shape=_halves_shapes(arrs), in_specs=[hbm] * n, out_specs=[hbm] * n,
                          scratch_shapes=_halves_sems(n), compiler_params=_cparams())(*arrs)


def _sibling_send(arrs, name):
    n = len(arrs)

    def body(*refs):
        x, y, c = lax.axis_index("x"), lax.axis_index("y"), lax.axis_index("c")
        send_sems, recv_sems = refs[2 * n:]
        copies = [pltpu.make_async_remote_copy(src_ref=refs[a], dst_ref=refs[n + a], send_sem=send_sems.at[a],
                                               recv_sem=recv_sems.at[a], device_id=(x, y, 1 - c), device_id_type=MESH)
                  for a in range(n)]
        for cp in copies:
            cp.start()
        for cp in copies:
            cp.wait()

    hbm = pl.BlockSpec(memory_space=pl.ANY)
    return pl.pallas_call(
        body, name=name, out_shape=[jax.ShapeDtypeStruct(a.shape, a.dtype) for a in arrs],
        in_specs=[hbm] * n, out_specs=[hbm] * n,
        scratch_shapes=[pltpu.SemaphoreType.DMA((n,)), pltpu.SemaphoreType.DMA((n,))],
        compiler_params=_cparams(),
    )(*arrs)


def _row_tile(h):
    return h // 2 if (h // 2) % 8 == 0 else h


def _row_tiles(hs):
    tiles = [_row_tile(h) for h in hs]
    assert len({h // t for h, t in zip(hs, tiles)}) == 1
    return tiles, hs[0] // tiles[0]


def _chip_sums(arrs, gots, core, name):
    n = len(arrs)
    tiles, steps = _row_tiles([a.shape[2] for a in arrs])

    def body(core_ref, *refs):
        for a_ref, b_ref, o_ref in zip(refs[:n], refs[n:2 * n], refs[2 * n:]):
            o_ref[...] = a_ref[...] + b_ref[...]

    slabs = [pl.BlockSpec((None, tr, a.shape[3]), lambda s, i, core_ref: (s, i, 0)) for a, tr in zip(arrs, tiles)]
    grid_spec = pltpu.PrefetchScalarGridSpec(
        num_scalar_prefetch=1, grid=(N_CHIPS, steps),
        in_specs=[pl.BlockSpec((None, None, tr, a.shape[3]), lambda s, i, core_ref: (s, core_ref[0], i, 0))
                  for a, tr in zip(arrs, tiles)] + slabs,
        out_specs=slabs)
    return pl.pallas_call(body, name=name, grid_spec=grid_spec,
                          out_shape=[jax.ShapeDtypeStruct(g.shape, g.dtype) for g in gots],
                          compiler_params=_cparams())(core, *arrs, *gots)


def _sum4s(ps, name):
    tiles, steps = _row_tiles([p.shape[1] for p in ps])

    def body(*refs):
        for p_ref, o_ref in zip(refs[:len(ps)], refs[len(ps):]):
            o_ref[...] = ((p_ref[0] + p_ref[1]) + p_ref[2]) + p_ref[3]

    return pl.pallas_call(
        body, name=name, grid=(steps,), out_shape=[jax.ShapeDtypeStruct(p.shape[1:], p.dtype) for p in ps],
        in_specs=[pl.BlockSpec((4, tr, p.shape[2]), lambda i: (0, i, 0)) for p, tr in zip(ps, tiles)],
        out_specs=[pl.BlockSpec((tr, p.shape[2]), lambda i: (i, 0)) for p, tr in zip(ps, tiles)],
        compiler_params=_cparams())(*ps)


def _adam_math(w, g, m, v):
    m2 = ADAM_B1 * m + (1.0 - ADAM_B1) * g
    v2 = ADAM_B2 * v + (1.0 - ADAM_B2) * (g * g)
    m_hat = m2 / (1.0 - ADAM_B1 ** ADAM_STEP)
    v_hat = v2 / (1.0 - ADAM_B2 ** ADAM_STEP)
    delta = -ADAM_LR * (m_hat / (jnp.sqrt(v_hat) + ADAM_EPS) + ADAM_WD * w)
    return delta, m2, v2


def _adamw(w, g, m, v, name):
    rows, cols = w.shape
    tr = rows // 4 if rows % 32 == 0 else rows

    def body(w_ref, g_ref, m_ref, v_ref, d_ref, m2_ref, v2_ref):
        delta, m2, v2 = _adam_math(w_ref[...], g_ref[...], m_ref[...], v_ref[...])
        d_ref[...] = delta
        m2_ref[...] = m2
        v2_ref[...] = v2

    spec = pl.BlockSpec((tr, cols), lambda i: (i, 0))
    shp = jax.ShapeDtypeStruct(w.shape, F32)
    return pl.pallas_call(body, name=name, grid=(rows // tr,), out_shape=[shp, shp, shp],
                          in_specs=[spec] * 4, out_specs=[spec] * 3, compiler_params=_cparams())(w, g, m, v)


def _small_update(g8, offsets, ws, ms, vs, loss_at, name):
    k = len(ws)

    def summed(g8_ref, lo, width):
        g = g8_ref[0:1, lo:lo + width]
        for r in range(1, N_DEV):
            g = g + g8_ref[r:r + 1, lo:lo + width]
        return g

    def body(g8_ref, *refs):
        ins, outs = refs[:3 * k], refs[3 * k:]
        for j in range(k):
            g = summed(g8_ref, offsets[j], ws[j].shape[1])
            delta, m2, v2 = _adam_math(ins[j][...], g, ins[k + j][...], ins[2 * k + j][...])
            for kind, val in enumerate((g, delta, m2, v2)):
                outs[kind * k + j][...] = val
        outs[4 * k][...] = summed(g8_ref, loss_at, LANES)

    vm = pl.BlockSpec(memory_space=pltpu.VMEM)
    shapes = [jax.ShapeDtypeStruct(w.shape, F32) for w in ws] * 4 + [jax.ShapeDtypeStruct((1, LANES), F32)]
    res = pl.pallas_call(body, name=name, out_shape=shapes, in_specs=[vm] * (1 + 3 * k), out_specs=[vm] * (4 * k + 1),
                         compiler_params=_cparams())(g8, *ws, *ms, *vs)
    return res[:k], res[k:2 * k], res[2 * k:3 * k], res[3 * k:4 * k], res[4 * k]


def _mod_shard(c8, w_ada, b_ada_shard, name):
    n = w_ada.shape[1]
    tn = 512

    def body(c_ref, w_ref, b_ref, o_ref, s_ref):
        cv = c_ref[...]
        sc = cv * (1.0 / (1.0 + jnp.exp(-cv)))
        s_ref[...] = sc
        o_ref[...] = _dot(sc.astype(BF16), w_ref[...].astype(BF16)) + b_ref[...]

    return pl.pallas_call(
        body, name=name, grid=(n // tn,),
        out_shape=[jax.ShapeDtypeStruct((8, n), F32), jax.ShapeDtypeStruct((8, D), F32)],
        in_specs=[pl.BlockSpec((8, D), lambda j: (0, 0)), pl.BlockSpec((D, tn), lambda j: (0, j)),
                  pl.BlockSpec((1, tn), lambda j: (0, j))],
        out_specs=[pl.BlockSpec((8, tn), lambda j: (0, j)), pl.BlockSpec((8, D), lambda j: (0, 0))],
        compiler_params=_cparams())(c8, w_ada, b_ada_shard)


def _layer_norm_stats(u):
    mu = jnp.mean(u, axis=1, keepdims=True)
    d = u - mu
    var = jnp.mean(d * d, axis=1, keepdims=True)
    rstd = lax.rsqrt(var + LN_EPS)
    return d * rstd, rstd


def _in_proj(x, vec, w_in, b_in, name):
    s = x.shape[0]
    tb = min(TOK_TILE, s)

    def body(x_ref, vec_ref, w_ref, b_ref, ht_ref, p_ref):
        h = x_ref[...] * (1.0 + vec_ref[V_SC_A:V_SC_A + 1, :]) + vec_ref[V_SH_A:V_SH_A + 1, :]
        hb = h.astype(BF16)
        ht_ref[...] = h.T.astype(BF16)
        proj = _dot(hb, w_ref[...]) + b_ref[...]
        col = lax.broadcasted_iota(jnp.int32, (1, D_IN), 1)
        is_q = (col < SB_W) | ((col >= 3 * SB_W) & (col < 3 * SB_W + SWA_QW))
        p_ref[...] = (proj * jnp.where(is_q, QK_SCALE, 1.0)).astype(BF16)

    return pl.pallas_call(
        body, name=name, grid=(s // tb,),
        out_shape=[jax.ShapeDtypeStruct((D, s), BF16), jax.ShapeDtypeStruct((s, D_IN), BF16)],
        in_specs=[pl.BlockSpec((tb, D), lambda i: (i, 0)), _resident((VEC_ROWS, D)), _resident((D, D_IN)),
                  _resident((1, D_IN))],
        out_specs=[pl.BlockSpec((D, tb), lambda i: (0, i)), pl.BlockSpec((tb, D_IN), lambda i: (i, 0))],
        compiler_params=_cparams())(x, vec, w_in, b_in)


def _softplus_parts(z):
    e1 = jnp.exp(-jnp.abs(z))
    sp = jnp.maximum(z, 0.0) + jnp.log(1.0 + e1)
    return sp, e1


def _sb_forward(proj, shards, name):
    s = proj.shape[0]
    tq, tk = min(SB_TQ, s), min(SB_TK, s)
    r = tq // tk

    n_sh = len(shards)
    nkb = SB_W // LANES
    nq = s // tq

    assert r % SB_UNROLL == 0, "the sweep below the diagonal takes whole steps"

    def body(q_ref, k_ref, v_ref, *refs):
        sh_refs, (o_ref, tot_ref, start_ref), got_refs = refs[:n_sh], refs[n_sh:n_sh + 3], refs[n_sh + 3:2 * n_sh + 3]
        acc_refs, run_refs = refs[2 * n_sh + 3:2 * n_sh + 5]
        i = pl.program_id(1)
        step = pl.program_id(0) * nq + i
        gather = _gather_exchange(sh_refs, got_refs, *refs[2 * n_sh + 5:])

        @pl.when(step == 0)
        def _():
            gather.start()

        lane = lax.broadcasted_iota(jnp.int32, (1, LANES), 1)
        first = lane < HEAD_DIM
        qp = q_ref[...]
        zero = jnp.zeros((), BF16)
        qs = (jnp.where(first, qp, zero), jnp.where(first, zero, qp))
        later = _sum_matrix(tk, lambda row, col: row > col)
        acc_refs[...] = jnp.zeros_like(acc_refs)
        run_refs[...] = jnp.zeros_like(run_refs)

        def blocks(tiles):
            rows = [slice(r0, r0 + n) for r0, n, _, _ in tiles]
            kjs = [k_ref[pl.ds(pl.multiple_of(j * tk, tk), tk), :] for _, _, j, _ in tiles]
            vjs = [v_ref[pl.ds(pl.multiple_of(j * tk, tk), tk), :] for _, _, j, _ in tiles]
            chains = [(hd, t) for t in range(len(tiles)) for hd in range(2)]
            zs = [_dot_nt(qs[hd][rows[t]], kjs[t]) for hd, t in chains]
            sps = [_softplus_parts(z)[0] for z in zs]
            befores = [_before(i * tq + r0, n, j * tk, tk) if diag else None for r0, n, j, diag in tiles]
            spms = [sp if befores[t] is None else jnp.where(befores[t], sp, 0.0) for (hd, t), sp in zip(chains, sps)]
            cums = [_block_sums(spm, later) for spm in spms]
            runs, ws = {}, []
            for (hd, t), z, sp, (cum, sm) in zip(chains, zs, sps, cums):
                key = (hd, tiles[t][0])
                if key not in runs:
                    runs[key] = run_refs[hd, rows[t], :]
                w = jnp.exp(z - sp - cum - _across(runs[key], tk))
                if befores[t] is not None:
                    w = jnp.where(befores[t], w, 0.0)
                ws.append(w.astype(BF16))
                runs[key] = runs[key] + sm
            sums = {}
            for (hd, t), w in zip(chains, ws):
                key, pv = (hd, tiles[t][0]), _dot(w, vjs[t])
                sums[key] = pv if key not in sums else sums[key] + pv
            for (hd, r0), run in runs.items():
                span = slice(r0, r0 + run.shape[0])
                acc_refs[hd, span, :] += sums[(hd, r0)]
                run_refs[hd, span, :] = run

        blocks([(d * tk, tk, i * r + e, e == d) for d in range(r) for e in range(d, -1, -1)])

        below = i * r

        def swept_mass():
            return jnp.min(jnp.minimum(run_refs[0], run_refs[1]))

        def more(carry):
            n, mass = carry
            return (n < below // SB_UNROLL) & (mass < SB_DEAD_MASS)

        def sweep(carry):
            n, _ = carry
            top = below - 1 - SB_UNROLL * n
            blocks([(0, tq, top - u, False) for u in range(SB_UNROLL)])
            return n + 1, swept_mass()

        n_swept, _ = lax.while_loop(more, sweep, (0, swept_mass()))
        start_ref[pl.program_id(0), i] = (below - SB_UNROLL * n_swept).astype(F32)
        o_ref[...] = jnp.where(first, acc_refs[0], acc_refs[1])
        tot_ref[...] = jnp.where(first, run_refs[0], run_refs[1])

        @pl.when(step == nkb * nq - 1)
        def _():
            gather.wait()

    shp = jax.ShapeDtypeStruct((s, SB_W), F32)
    qspec = pl.BlockSpec((tq, LANES), lambda p, i: (i, p))
    hbm = pl.BlockSpec(memory_space=pl.ANY)
    return pl.pallas_call(
        body, name=name, grid=(nkb, nq),
        out_shape=[shp, shp, jax.ShapeDtypeStruct((nkb, nq), F32)]
        + [jax.ShapeDtypeStruct((N_CHIPS,) + a.shape, a.dtype) for a in shards],
        in_specs=[qspec,
                  pl.BlockSpec((s, LANES), lambda p, i: (0, nkb + p)),
                  pl.BlockSpec((s, LANES), lambda p, i: (0, 2 * nkb + p))] + [hbm] * n_sh,
        out_specs=[qspec, qspec, pl.BlockSpec(memory_space=pltpu.SMEM)] + [hbm] * n_sh,
        scratch_shapes=[pltpu.VMEM((2, tq, LANES), F32), pltpu.VMEM((2, tq, LANES), F32)] + _exchange_sems(n_sh),
        compiler_params=_cparams())(proj, proj, proj, *shards)


def _swa_masks(n):
    ti = lax.broadcasted_iota(jnp.int32, (WINDOW, 2 * WINDOW), 0)
    kj = lax.broadcasted_iota(jnp.int32, (WINDOW, 2 * WINDOW), 1)
    dist = ti + WINDOW - kj
    valid = (dist >= 0) & (dist < WINDOW) & ((n * WINDOW - WINDOW + kj) >= 0)
    return valid, dist.astype(F32)


def _swa_probs(sc, valid, distf, h, sink):
    slope = 2.0 ** (-(h + 1))
    sc = jnp.where(valid, sc - slope * distf, MASK_VALUE)
    mx = jnp.maximum(jnp.max(sc, axis=1, keepdims=True), sink)
    p = jnp.exp(sc - mx)
    es = jnp.exp(sink - mx)
    inv = 1.0 / (jnp.sum(p, axis=1, keepdims=True) + es)
    return p * inv, es * inv


def _swa_forward(proj, sinks, shards, name):
    s = proj.shape[0]
    nb = s // WINDOW
    qb, kb, vb = 3 * SB_W // SWA_QW, (3 * SB_W + SWA_QW) // LANES, (3 * SB_W + SWA_QW + SWA_KW) // LANES
    n_sh = len(shards)

    def body(q_ref, kp_ref, kc_ref, vp_ref, vc_ref, sink_ref, *refs):
        sh_refs, o_ref, got_refs = refs[:n_sh], refs[n_sh], refs[n_sh + 1:2 * n_sh + 1]
        n = pl.program_id(0)
        gather = _gather_exchange(sh_refs, got_refs, *refs[2 * n_sh + 1:])

        @pl.when(n == 0)
        def _():
            gather.start()

        k = jnp.concatenate([kp_ref[...], kc_ref[...]], axis=0)
        v = jnp.concatenate([vp_ref[...], vc_ref[...]], axis=0)
        k_sw = pltpu.roll(k.astype(F32), HEAD_DIM, 1).astype(BF16)
        v_sw = pltpu.roll(v.astype(F32), HEAD_DIM, 1).astype(BF16)
        lane = lax.broadcasted_iota(jnp.int32, (1, LANES), 1)
        halves = [lane < HEAD_DIM, lane >= HEAD_DIM]
        valid, distf = _swa_masks(n)
        heads = range(2 * 4)
        qms = [jnp.where(halves[h % 2], q_ref[:, (h // 2) * LANES:(h // 2 + 1) * LANES], jnp.zeros((), BF16))
               for h in heads]
        kus = [k if h // 4 == h % 2 else k_sw for h in heads]
        vus = [v if h // 4 == h % 2 else v_sw for h in heads]
        scores = [_dot_nt(qms[h], kus[h]) for h in heads]
        ps = [_swa_probs(scores[h], valid, distf, h, sink_ref[h])[0].astype(BF16) for h in heads]
        outs = [_dot(ps[h], vus[h]) for h in heads]
        for pair in range(4):
            o_ref[:, pair * LANES:(pair + 1) * LANES] = jnp.where(halves[0], outs[2 * pair], outs[2 * pair + 1])

        @pl.when(n == nb - 1)
        def _():
            gather.wait()

    prev = lambda n: jnp.maximum(n - 1, 0)
    hbm = pl.BlockSpec(memory_space=pl.ANY)
    return pl.pallas_call(
        body, name=name, grid=(nb,),
        out_shape=[jax.ShapeDtypeStruct((s, SWA_QW), F32)]
        + [jax.ShapeDtypeStruct((N_CHIPS,) + a.shape, a.dtype) for a in shards],
        in_specs=[pl.BlockSpec((WINDOW, SWA_QW), lambda n: (n, qb)),
                  pl.BlockSpec((WINDOW, LANES), lambda n: (prev(n), kb)),
                  pl.BlockSpec((WINDOW, LANES), lambda n: (n, kb)),
                  pl.BlockSpec((WINDOW, LANES), lambda n: (prev(n), vb)),
                  pl.BlockSpec((WINDOW, LANES), lambda n: (n, vb)),
                  pl.BlockSpec(memory_space=pltpu.SMEM)] + [hbm] * n_sh,
        out_specs=[pl.BlockSpec((WINDOW, SWA_QW), lambda n: (n, 0))] + [hbm] * n_sh,
        scratch_shapes=_exchange_sems(n_sh),
        compiler_params=_cparams())(proj, proj, proj, proj, proj, sinks, *shards)


def _rms_parts(y):
    return lax.rsqrt(jnp.mean(y * y, axis=1, keepdims=True) + RMS_EPS)


def _post_attention(y_sb, y_sw, x, vec, w_out, name):
    s = x.shape[0]
    tb = min(TOK_TILE, s)

    def body(ysb_ref, ysw_ref, x_ref, vec_ref, w_ref, mixedt_ref, attn_ref, x1_ref, h2_ref, h2t_ref):
        ysb, ysw = ysb_ref[...], ysw_ref[...]
        nsb_f = ysb * _rms_parts(ysb) * vec_ref[V_GN:V_GN + 1, :SB_W]
        nsw_f = ysw * _rms_parts(ysw) * vec_ref[V_GN:V_GN + 1, SB_W:]
        nsb, nsw = nsb_f.astype(BF16), nsw_f.astype(BF16)
        mixedt_ref[:SB_W, :] = nsb_f.T.astype(BF16)
        mixedt_ref[SB_W:, :] = nsw_f.T.astype(BF16)
        attn = _dot(nsb, w_ref[:SB_W, :]) + _dot(nsw, w_ref[SB_W:, :])
        attn_ref[...] = attn
        u1 = ALPHA * x_ref[...] + (1.0 + vec_ref[V_G_A:V_G_A + 1, :]) * attn
        xhat, _ = _layer_norm_stats(u1)
        x1 = xhat * vec_ref[V_LN1G:V_LN1G + 1, :] + vec_ref[V_LN1B:V_LN1B + 1, :]
        x1_ref[...] = x1
        h2 = x1 * (1.0 + vec_ref[V_SC_F:V_SC_F + 1, :]) + vec_ref[V_SH_F:V_SH_F + 1, :]
        h2_ref[...] = h2.astype(BF16)
        h2t_ref[...] = h2.T.astype(BF16)

    half = pl.BlockSpec((tb, SB_W), lambda i: (i, 0))
    full = pl.BlockSpec((tb, D), lambda i: (i, 0))
    full_t = pl.BlockSpec((D, tb), lambda i: (0, i))
    return pl.pallas_call(
        body, name=name, grid=(s // tb,),
        out_shape=[jax.ShapeDtypeStruct((D, s), BF16), jax.ShapeDtypeStruct((s, D), F32),
                   jax.ShapeDtypeStruct((s, D), F32), jax.ShapeDtypeStruct((s, D), BF16),
                   jax.ShapeDtypeStruct((D, s), BF16)],
        in_specs=[half, half, full, _resident((VEC_ROWS, D)), _resident((D, D))],
        out_specs=[full_t, full, full, full, full_t],
        compiler_params=_cparams())(y_sb, y_sw, x, vec, w_out)


def _ffn_forward(h2, w_gu, w_down, name):
    s = h2.shape[0]
    tb = min(FFN_TILE, s)

    def body(h_ref, wgu_ref, wd_ref, gu_ref, actt_ref, ffn_ref):
        gu = _dot(h_ref[...], wgu_ref[...])
        gu_ref[...] = gu.astype(BF16)
        gate, up = gu[:, :D_FF], gu[:, D_FF:]
        act = gate * (1.0 / (1.0 + jnp.exp(-gate))) * up
        actt_ref[...] = act.T.astype(BF16)
        ffn_ref[...] = _dot(act.astype(BF16), wd_ref[...])

    return pl.pallas_call(
        body, name=name, grid=(s // tb,),
        out_shape=[jax.ShapeDtypeStruct((s, 2 * D_FF), BF16), jax.ShapeDtypeStruct((D_FF, s), BF16),
                   jax.ShapeDtypeStruct((s, D), F32)],
        in_specs=[pl.BlockSpec((tb, D), lambda i: (i, 0)), _resident((D, 2 * D_FF)), _resident((D_FF, D))],
        out_specs=[pl.BlockSpec((tb, 2 * D_FF), lambda i: (i, 0)), pl.BlockSpec((D_FF, tb), lambda i: (0, i)),
                   pl.BlockSpec((tb, D), lambda i: (i, 0))],
        compiler_params=_cparams())(h2, w_gu, w_down)


def _layer_norm_bwd(dxhat, xhat, rstd):
    m1 = jnp.mean(dxhat, axis=1, keepdims=True)
    m2 = jnp.mean(dxhat * xhat, axis=1, keepdims=True)
    return rstd * (dxhat - m1 - xhat * m2)


def _colsum(a):
    return jnp.sum(a, axis=0, keepdims=True)


A_LN2G, A_LN2B, A_GF, A_SCF, A_SHF, A_LOSS = range(6)
B_LN1G, B_LN1B, B_GA, B_GN = range(4)
C_SCA, C_SHA = range(2)


def _ffn_backward(x1, ffn, target, gu, vec, w_gu, w_down, name):
    s = x1.shape[0]
    tb = min(FFN_BWD_TILE, s)

    def body(x1_ref, ffn_ref, t_ref, gu_ref, vec_ref, wgu_ref, wd_ref, dffn_ref, dgu_ref, dx1_ref, acc_ref):
        @pl.when(pl.program_id(0) == 0)
        def _():
            acc_ref[...] = jnp.zeros_like(acc_ref)

        x1v, ffn_v = x1_ref[...], ffn_ref[...]
        g_f = 1.0 + vec_ref[V_G_F:V_G_F + 1, :]
        u2 = ALPHA * x1v + g_f * ffn_v
        xhat, rstd = _layer_norm_stats(u2)
        ln_g = vec_ref[V_LN2G:V_LN2G + 1, :]
        err = xhat * ln_g + vec_ref[V_LN2B:V_LN2B + 1, :] - t_ref[...]
        dx2 = err * (1.0 / D)
        acc_ref[A_LOSS:A_LOSS + 1, :] += _colsum(err * err) * (0.5 / D)
        acc_ref[A_LN2G:A_LN2G + 1, :] += _colsum(dx2 * xhat)
        acc_ref[A_LN2B:A_LN2B + 1, :] += _colsum(dx2)
        du2 = _layer_norm_bwd(dx2 * ln_g, xhat, rstd)
        acc_ref[A_GF:A_GF + 1, :] += _colsum(du2 * ffn_v)
        dffn = (g_f * du2).astype(BF16)
        dffn_ref[...] = dffn
        dact = _dot_nt(dffn, wd_ref[...])
        gate, up = gu_ref[:, :D_FF].astype(F32), gu_ref[:, D_FF:].astype(F32)
        sg = 1.0 / (1.0 + jnp.exp(-gate))
        dgate = (dact * up * (sg * (1.0 + gate * (1.0 - sg)))).astype(BF16)
        dup = (dact * (gate * sg)).astype(BF16)
        dgu_ref[:, :D_FF] = dgate
        dgu_ref[:, D_FF:] = dup
        dh2 = _dot_nt(dgate, wgu_ref[:, :D_FF]) + _dot_nt(dup, wgu_ref[:, D_FF:])
        dx1_ref[...] = ALPHA * du2 + dh2 * (1.0 + vec_ref[V_SC_F:V_SC_F + 1, :])
        acc_ref[A_SCF:A_SCF + 1, :] += _colsum(dh2 * x1v)
        acc_ref[A_SHF:A_SHF + 1, :] += _colsum(dh2)

    full = pl.BlockSpec((tb, D), lambda i: (i, 0))
    wide = pl.BlockSpec((tb, 2 * D_FF), lambda i: (i, 0))
    return pl.pallas_call(
        body, name=name, grid=(s // tb,),
        out_shape=[jax.ShapeDtypeStruct((s, D), BF16), jax.ShapeDtypeStruct((s, 2 * D_FF), BF16),
                   jax.ShapeDtypeStruct((s, D), F32), jax.ShapeDtypeStruct((8, D), F32)],
        in_specs=[full, full, full, wide, _resident((VEC_ROWS, D)), _resident((D, 2 * D_FF)), _resident((D_FF, D))],
        out_specs=[full, wide, full, pl.BlockSpec((8, D), lambda i: (0, 0))],
        compiler_params=_cparams())(x1, ffn, target, gu, vec, w_gu, w_down)


def _attn_out_backward(dx1, x, attn, y_sb, y_sw, vec, w_out, name):
    s = x.shape[0]
    tb = min(TOK_TILE, s)

    def body(dx1_ref, x_ref, attn_ref, ysb_ref, ysw_ref, vec_ref, w_ref, du1_ref, dattn_ref, dy_ref, acc_ref):
        @pl.when(pl.program_id(0) == 0)
        def _():
            acc_ref[...] = jnp.zeros_like(acc_ref)

        attn = attn_ref[...]
        g_a = 1.0 + vec_ref[V_G_A:V_G_A + 1, :]
        xhat, rstd = _layer_norm_stats(ALPHA * x_ref[...] + g_a * attn)
        dx1v = dx1_ref[...]
        acc_ref[B_LN1G:B_LN1G + 1, :] += _colsum(dx1v * xhat)
        acc_ref[B_LN1B:B_LN1B + 1, :] += _colsum(dx1v)
        du1 = _layer_norm_bwd(dx1v * vec_ref[V_LN1G:V_LN1G + 1, :], xhat, rstd)
        du1_ref[...] = du1
        acc_ref[B_GA:B_GA + 1, :] += _colsum(du1 * attn)
        dattn = (g_a * du1).astype(BF16)
        dattn_ref[...] = dattn
        dmixed = _dot_nt(dattn, w_ref[...])
        for lo, y_ref in ((0, ysb_ref), (SB_W, ysw_ref)):
            y = y_ref[...]
            rr = _rms_parts(y)
            dn = dmixed[:, lo:lo + SB_W]
            acc_ref[B_GN:B_GN + 1, lo:lo + SB_W] += _colsum(dn * y * rr)
            dng = dn * vec_ref[V_GN:V_GN + 1, lo:lo + SB_W]
            dy_ref[:, lo:lo + SB_W] = rr * dng - y * (rr * rr * rr) * jnp.mean(dng * y, axis=1, keepdims=True)

    half = pl.BlockSpec((tb, SB_W), lambda i: (i, 0))
    full = pl.BlockSpec((tb, D), lambda i: (i, 0))
    return pl.pallas_call(
        body, name=name, grid=(s // tb,),
        out_shape=[jax.ShapeDtypeStruct((s, D), F32), jax.ShapeDtypeStruct((s, D), BF16),
                   jax.ShapeDtypeStruct((s, D), F32), jax.ShapeDtypeStruct((8, D), F32)],
        in_specs=[full, full, full, half, half, _resident((VEC_ROWS, D)), _resident((D, D))],
        out_specs=[full, full, full, pl.BlockSpec((8, D), lambda i: (0, 0))],
        compiler_params=_cparams())(dx1, x, attn, y_sb, y_sw, vec, w_out)


def _sb_backward(proj, sp_total, sweep_start, dy, slabs, name):
    s = proj.shape[0]
    tq, tk = min(SB_TQ, s), min(SB_TK, s)
    r = tq // tk
    nkb = SB_W // LANES
    nq = s // tq

    assert r % SB_UNROLL == 0, "the sweep below the diagonal takes whole steps"

    n_sl = len(slabs)

    def body(q_ref, k_ref, v_ref, tot_ref, do_ref, start_ref, *refs):
        slab_refs, (dq_ref, dk_ref, dv_ref), got_refs = refs[:n_sl], refs[n_sl:n_sl + 3], refs[n_sl + 3:2 * n_sl + 3]
        dq_acc, left_refs, gsum_refs = refs[2 * n_sl + 3:2 * n_sl + 6]
        i = pl.program_id(1)
        step = pl.program_id(0) * nq + i
        scatter = _scatter_exchange(slab_refs, got_refs, *refs[2 * n_sl + 6:])

        @pl.when(step == 0)
        def _():
            scatter.start()

        @pl.when(i == 0)
        def _():
            dk_ref[...] = jnp.zeros_like(dk_ref)
            dv_ref[...] = jnp.zeros_like(dv_ref)

        lane = lax.broadcasted_iota(jnp.int32, (1, LANES), 1)
        first = lane < HEAD_DIM
        qp, dop, totp = q_ref[...], do_ref[...], tot_ref[...]
        zero = jnp.zeros((), BF16)
        qs = (jnp.where(first, qp, zero), jnp.where(first, zero, qp))
        dofs = (jnp.where(first, dop, 0.0), jnp.where(first, 0.0, dop))
        dobs = tuple(d.astype(BF16) for d in dofs)
        dots = tuple(d.T.astype(BF16) for d in dofs)
        qts = tuple(qh.astype(F32).T.astype(BF16) for qh in qs)
        later = _sum_matrix(tk, lambda row, col: row > col)
        earlier = _sum_matrix(tk, lambda row, col: row < col)
        dq_acc[...] = jnp.zeros_like(dq_acc)
        gsum_refs[...] = jnp.zeros_like(gsum_refs)
        swapped = pltpu.roll(totp, HEAD_DIM, 1)
        left_refs[0] = jnp.where(first, totp, swapped)
        left_refs[1] = jnp.where(first, swapped, totp)

        def blocks(tiles):
            rows = [slice(r0, r0 + n) for r0, n, _, _ in tiles]
            kjs = [k_ref[pl.ds(pl.multiple_of(j * tk, tk), tk), :] for _, _, j, _ in tiles]
            vjs = [v_ref[pl.ds(pl.multiple_of(j * tk, tk), tk), :] for _, _, j, _ in tiles]
            chains = [(hd, t) for t in range(len(tiles)) for hd in range(2)]
            keys = [(hd, tiles[t][0]) for hd, t in chains]
            zs = [_dot_nt(qs[hd][rows[t]], kjs[t]) for hd, t in chains]
            dws = [_dot_nt(dobs[hd][rows[t]], vjs[t]) for hd, t in chains]
            parts = [_softplus_parts(z) for z in zs]
            sps = [p[0] for p in parts]
            befores = [_before(i * tq + r0, n, j * tk, tk) if diag else None for r0, n, j, diag in tiles]
            spms = [sp if befores[t] is None else jnp.where(befores[t], sp, 0.0) for (hd, t), sp in zip(chains, sps)]
            cums = [_block_sums(spm, later) for spm in spms]
            lefts, ws = {}, []
            for key, (hd, t), z, sp, (cum, sm) in zip(keys, chains, zs, sps, cums):
                if key not in lefts:
                    lefts[key] = left_refs[hd, rows[t], :]
                lefts[key] = lefts[key] - sm
                w = jnp.exp(z - sp - cum - _across(lefts[key], tk))
                ws.append(w if befores[t] is None else jnp.where(befores[t], w, 0.0))
            wbs = [w.astype(BF16) for w in ws]
            dvs = [_dot(dots[hd][:, rows[t]], wb) for (hd, t), wb in zip(chains, wbs)]
            gs = [dw * w for dw, w in zip(dws, ws)]
            gcums = [_block_sums(g, earlier) for g in gs]
            gsums, dzbs = {}, []
            for key, (hd, t), z, (sp, e1), g, (gcum, gsm) in zip(keys, chains, zs, parts, gs, gcums):
                if key not in gsums:
                    gsums[key] = gsum_refs[hd, rows[t], :]
                inv = 1.0 / (1.0 + e1)
                sig = jnp.where(z >= 0.0, inv, e1 * inv)
                dz = g - sig * (g + _across(gsums[key], tk) + gcum)
                dzbs.append((dz if befores[t] is None else jnp.where(befores[t], dz, 0.0)).astype(BF16))
                gsums[key] = gsums[key] + gsm
            dqs = [_dot(dzb, kjs[t]) for (hd, t), dzb in zip(chains, dzbs)]
            dks = [_dot(qts[hd][:, rows[t]], dzb) for (hd, t), dzb in zip(chains, dzbs)]
            for t, (_, _, j, _) in enumerate(tiles):
                dv_ref[j] += dvs[2 * t] + dvs[2 * t + 1]
                dk_ref[j] += dks[2 * t] + dks[2 * t + 1]
            totals = {}
            for key, dq in zip(keys, dqs):
                totals[key] = dq if key not in totals else totals[key] + dq
            for (hd, r0), tot in totals.items():
                span = slice(r0, r0 + tot.shape[0])
                dq_acc[hd, span, :] += tot
                left_refs[hd, span, :] = lefts[(hd, r0)]
                gsum_refs[hd, span, :] = gsums[(hd, r0)]

        below = i * r
        start = jnp.clip(start_ref[pl.program_id(0), i].astype(jnp.int32), 0, below) // SB_UNROLL * SB_UNROLL

        def sweep(n, carry):
            blocks([(0, tq, start + SB_UNROLL * n + u, False) for u in range(SB_UNROLL)])
            return carry

        lax.fori_loop(0, (below - start) // SB_UNROLL, sweep, 0)
        blocks([(d * tk, tk, below + e, e == d) for d in range(r) for e in range(d + 1)])
        dq_ref[...] = jnp.where(first, dq_acc[0], dq_acc[1])

        @pl.when(step == nkb * nq - 1)
        def _():
            scatter.wait()

    shp = jax.ShapeDtypeStruct((s, SB_W), F32)
    qspec = pl.BlockSpec((tq, LANES), lambda p, i: (i, p))
    whole = pl.BlockSpec((None, s // tk, LANES, tk), lambda p, i: (p, 0, 0, 0))
    shp_t = jax.ShapeDtypeStruct((nkb, s // tk, LANES, tk), F32)
    hbm = pl.BlockSpec(memory_space=pl.ANY)
    return pl.pallas_call(
        body, name=name, grid=(nkb, nq),
        out_shape=[shp, shp_t, shp_t] + [jax.ShapeDtypeStruct(p.shape, p.dtype) for p in slabs],
        in_specs=[qspec,
                  pl.BlockSpec((s, LANES), lambda p, i: (0, nkb + p)),
                  pl.BlockSpec((s, LANES), lambda p, i: (0, 2 * nkb + p)),
                  qspec, qspec, pl.BlockSpec(memory_space=pltpu.SMEM)] + [hbm] * n_sl,
        out_specs=[qspec, whole, whole] + [hbm] * n_sl,
        scratch_shapes=[pltpu.VMEM((2, tq, LANES), F32), pltpu.VMEM((2, tq, LANES), F32), pltpu.VMEM((2, tq, LANES), F32)]
        + _exchange_sems(n_sl),
        compiler_params=_cparams())(proj, proj, proj, sp_total, dy, sweep_start, *slabs)


def _swa_backward(proj, y_sw, dy, sinks, gives, name):
    s = proj.shape[0]
    nb = s // WINDOW
    qb, kb, vb = 3 * SB_W // SWA_QW, (3 * SB_W + SWA_QW) // LANES, (3 * SB_W + SWA_QW + SWA_KW) // LANES

    n_gv = len(gives)

    def body(q_ref, kp_ref, kc_ref, vp_ref, vc_ref, o_ref, do_ref, sink_ref, *refs):
        give_refs, (dq_ref, dk_ref, dv_ref, ds_ref), got_refs = refs[:n_gv], refs[n_gv:n_gv + 4], refs[n_gv + 4:2 * n_gv + 4]
        n = pl.program_id(0)
        swap = _sibling_halves(give_refs, got_refs, *refs[2 * n_gv + 4:])

        @pl.when(n == 0)
        def _():
            for cp in swap:
                cp.start()

        @pl.when(n == 0)
        def _():
            dk_ref[...] = jnp.zeros_like(dk_ref)
            dv_ref[...] = jnp.zeros_like(dv_ref)
            ds_ref[...] = jnp.zeros_like(ds_ref)

        k = jnp.concatenate([kp_ref[...], kc_ref[...]], axis=0)
        v = jnp.concatenate([vp_ref[...], vc_ref[...]], axis=0)
        k_sw = pltpu.roll(k.astype(F32), HEAD_DIM, 1).astype(BF16)
        v_sw = pltpu.roll(v.astype(F32), HEAD_DIM, 1).astype(BF16)
        lane = lax.broadcasted_iota(jnp.int32, (1, LANES), 1)
        halves = [lane < HEAD_DIM, lane >= HEAD_DIM]
        valid, distf = _swa_masks(n)
        heads = range(2 * 4)
        cols = [slice((h // 2) * LANES, (h // 2 + 1) * LANES) for h in heads]
        qms = [jnp.where(halves[h % 2], q_ref[:, cols[h]], jnp.zeros((), BF16)) for h in heads]
        dos = [jnp.where(halves[h % 2], do_ref[:, cols[h]], 0.0) for h in heads]
        dobs = [d.astype(BF16) for d in dos]
        native = [h // 4 == h % 2 for h in heads]
        kus = [k if native[h] else k_sw for h in heads]
        vus = [v if native[h] else v_sw for h in heads]
        scores = [_dot_nt(qms[h], kus[h]) for h in heads]
        dps = [_dot_nt(dobs[h], vus[h]) for h in heads]
        deltas = [jnp.sum(dos[h] * o_ref[:, cols[h]], axis=1, keepdims=True) for h in heads]
        probs = [_swa_probs(scores[h], valid, distf, h, sink_ref[h]) for h in heads]
        pbs = [probs[h][0].astype(BF16) for h in heads]
        dscs = [(probs[h][0] * (dps[h] - deltas[h])).astype(BF16) for h in heads]
        dqs = [_dot(dscs[h], kus[h]) for h in heads]
        dks = [_dot_tn(dscs[h], qms[h]) for h in heads]
        dvs = [_dot_tn(pbs[h], dobs[h]) for h in heads]
        for h in heads:
            ds_ref[h:h + 1, :] += jnp.zeros((1, LANES), F32) - jnp.sum(probs[h][1] * deltas[h])
        for pair in range(4):
            dq_ref[:, cols[2 * pair]] = jnp.where(halves[0], dqs[2 * pair], dqs[2 * pair + 1])

        def gathered(parts):
            nat = sum(parts[h] for h in heads if native[h])
            rot = sum(parts[h] for h in heads if not native[h])
            return nat + pltpu.roll(rot, HEAD_DIM, 1)

        dk, dv = gathered(dks), gathered(dvs)
        prev = pl.multiple_of(jnp.maximum(n - 1, 0) * WINDOW, WINDOW)
        cur = pl.multiple_of(n * WINDOW, WINDOW)
        dk_ref[pl.ds(prev, WINDOW), :] += dk[:WINDOW]
        dv_ref[pl.ds(prev, WINDOW), :] += dv[:WINDOW]
        dk_ref[pl.ds(cur, WINDOW), :] += dk[WINDOW:]
        dv_ref[pl.ds(cur, WINDOW), :] += dv[WINDOW:]

        @pl.when(n == nb - 1)
        def _():
            for cp in swap:
                cp.wait()

    prev_blk = lambda n: jnp.maximum(n - 1, 0)
    wide = pl.BlockSpec((WINDOW, SWA_QW), lambda n: (n, 0))
    whole = pl.BlockSpec((s, LANES), lambda n: (0, 0))
    hbm = pl.BlockSpec(memory_space=pl.ANY)
    return pl.pallas_call(
        body, name=name, grid=(nb,),
        out_shape=[jax.ShapeDtypeStruct((s, SWA_QW), F32), jax.ShapeDtypeStruct((s, LANES), F32),
                   jax.ShapeDtypeStruct((s, LANES), F32), jax.ShapeDtypeStruct((8, LANES), F32)] + _halves_shapes(gives),
        in_specs=[pl.BlockSpec((WINDOW, SWA_QW), lambda n: (n, qb)),
                  pl.BlockSpec((WINDOW, LANES), lambda n: (prev_blk(n), kb)),
                  pl.BlockSpec((WINDOW, LANES), lambda n: (n, kb)),
                  pl.BlockSpec((WINDOW, LANES), lambda n: (prev_blk(n), vb)),
                  pl.BlockSpec((WINDOW, LANES), lambda n: (n, vb)),
                  wide,
                  pl.BlockSpec((WINDOW, SWA_QW), lambda n: (n, 1)),
                  pl.BlockSpec(memory_space=pltpu.SMEM)] + [hbm] * n_gv,
        out_specs=[wide, whole, whole, pl.BlockSpec((8, LANES), lambda n: (0, 0))] + [hbm] * n_gv,
        scratch_shapes=_halves_sems(n_gv),
        compiler_params=_cparams())(proj, proj, proj, proj, proj, y_sw, dy, sinks, *gives)


def _in_proj_backward(dq_sb, dkt_sb, dvt_sb, dq_sw, dk_sw, dv_sw, du1, x, vec, w_in, name):
    s = x.shape[0]
    tb = min(TOK_TILE, s)
    n_pairs, _, _, tk = dkt_sb.shape

    def body(dqsb_ref, dktsb_ref, dvtsb_ref, dqsw_ref, dksw_ref, dvsw_ref, du1_ref, x_ref, vec_ref, w_ref,
             dproj_ref, gx_ref, acc_ref, bacc_ref):
        @pl.when(pl.program_id(0) == 0)
        def _():
            acc_ref[...] = jnp.zeros_like(acc_ref)
            bacc_ref[...] = jnp.zeros_like(bacc_ref)

        pieces = ((0, dqsb_ref, QK_SCALE), (3 * SB_W, dqsw_ref, QK_SCALE), (3 * SB_W + SWA_QW, dksw_ref, 1.0),
                  (3 * SB_W + SWA_QW + SWA_KW, dvsw_ref, 1.0))
        for lo, ref, scale in pieces:
            width = ref.shape[1]
            piece = ref[...] * scale
            bacc_ref[0:1, lo:lo + width] += _colsum(piece)
            dproj_ref[:, lo:lo + width] = piece.astype(BF16)
        for base, ref in ((SB_W, dktsb_ref), (2 * SB_W, dvtsb_ref)):
            for p in range(n_pairs):
                lo = base + p * LANES
                for jj in range(tb // tk):
                    piece = ref[p, jj].T
                    bacc_ref[0:1, lo:lo + LANES] += _colsum(piece)
                    dproj_ref[jj * tk:(jj + 1) * tk, lo:lo + LANES] = piece.astype(BF16)
        dh = _dot_nt(dproj_ref[...], w_ref[...])
        xv = x_ref[...]
        gx_ref[...] = ALPHA * du1_ref[...] + dh * (1.0 + vec_ref[V_SC_A:V_SC_A + 1, :])
        acc_ref[C_SCA:C_SCA + 1, :] += _colsum(dh * xv)
        acc_ref[C_SHA:C_SHA + 1, :] += _colsum(dh)

    half = pl.BlockSpec((tb, SB_W), lambda i: (i, 0))
    narrow = pl.BlockSpec((tb, LANES), lambda i: (i, 0))
    full = pl.BlockSpec((tb, D), lambda i: (i, 0))
    blocks_t = pl.BlockSpec((n_pairs, tb // tk, LANES, tk), lambda i: (0, i, 0, 0))
    return pl.pallas_call(
        body, name=name, grid=(s // tb,),
        out_shape=[jax.ShapeDtypeStruct((s, D_IN), BF16), jax.ShapeDtypeStruct((s, D), F32),
                   jax.ShapeDtypeStruct((8, D), F32), jax.ShapeDtypeStruct((8, D_IN), F32)],
        in_specs=[half, blocks_t, blocks_t, half, narrow, narrow, full, full, _resident((VEC_ROWS, D)),
                  _resident((D, D_IN))],
        out_specs=[pl.BlockSpec((tb, D_IN), lambda i: (i, 0)), full, pl.BlockSpec((8, D), lambda i: (0, 0)),
                   pl.BlockSpec((8, D_IN), lambda i: (0, 0))],
        compiler_params=_cparams())(dq_sb, dkt_sb, dvt_sb, dq_sw, dk_sw, dv_sw, du1, x, vec, w_in)


def _weight_grad(at, b, name, col_shards=1):
    m, s = at.shape
    n = b.shape[1]
    if col_shards > 1:
        tn = n // col_shards
        out_shape = jax.ShapeDtypeStruct((col_shards, m, tn), F32)
        out_spec = pl.BlockSpec((None, m, tn), lambda j, k: (j, 0, 0))
    else:
        tn = 512 if n % 512 == 0 else n
        out_shape = jax.ShapeDtypeStruct((m, n), F32)
        out_spec = pl.BlockSpec((m, tn), lambda j, k: (0, j))
    ts = min(WGRAD_TOKENS, s)
    while 2 * (m * ts * 2 + ts * tn * 2 + m * tn * 4) > WGRAD_VMEM and ts > 512:
        ts //= 2

    def body(at_ref, b_ref, o_ref):
        @pl.when(pl.program_id(1) == 0)
        def _():
            o_ref[...] = jnp.zeros_like(o_ref)

        o_ref[...] += _dot(at_ref[...], b_ref[...])

    return pl.pallas_call(
        body, name=name, grid=(n // tn, s // ts),
        out_shape=out_shape,
        in_specs=[pl.BlockSpec((m, ts), lambda j, k: (0, k)), pl.BlockSpec((ts, tn), lambda j, k: (k, j))],
        out_specs=out_spec,
        compiler_params=_cparams())(at, b)


def _pad_rows(v, rows):
    return jnp.concatenate([v, jnp.zeros((rows - v.shape[0], v.shape[1]), v.dtype)], axis=0)


def _col_shards(w, n_shards):
    r, n = w.shape
    return w.reshape(r, n_shards, n // n_shards).transpose(1, 0, 2)


def kernel(x, c, w_ada, b_ada, w_in, b_in, sinks, gn_sb, gn_swa, w_out, ln1_g, ln1_b, w_gu, w_down, ln2_g, ln2_b, loss_target, m_w_ada, m_b_ada, m_w_in, m_b_in, m_sinks, m_gn_sb, m_gn_swa, m_w_out, m_ln1_g, m_ln1_b, m_w_gu, m_w_down, m_ln2_g, m_ln2_b, v_w_ada, v_b_ada, v_w_in, v_b_in, v_sinks, v_gn_sb, v_gn_swa, v_w_out, v_ln1_g, v_ln1_b, v_w_gu, v_w_down, v_ln2_g, v_ln2_b):
    ix, iy, ic = lax.axis_index("x"), lax.axis_index("y"), lax.axis_index("c")
    chip = 2 * ix + iy
    dev = 4 * ix + 2 * iy + ic
    xs, target = x[0], loss_target[0]
    s = xs.shape[0]

    c_rows, g_in = _allgather8(_pad_rows(c, 8), "gather_c", gather=[w_in[0].astype(BF16)])
    c_all = c_rows[::8]
    n_ada = w_ada.shape[2]
    b_ada_shard = lax.dynamic_slice_in_dim(b_ada, chip * n_ada, n_ada, axis=1)
    mod_cols, silu_c = _mod_shard(c_all, w_ada[0], b_ada_shard, "mod_shard")
    mod_all = _allgather8(mod_cols, "gather_mod")[0].reshape(N_DEV, 8, n_ada)
    mod_mine = lax.dynamic_index_in_dim(mod_all, dev, axis=1, keepdims=False)
    mod = mod_mine.reshape(N_CHIPS, 2, n_ada)[:, 0].reshape(6, D)
    vec = jnp.concatenate([mod, ln1_g, ln1_b, ln2_g, ln2_b, jnp.concatenate([gn_sb, gn_swa], axis=1),
                           jnp.zeros((VEC_ROWS - 11, D), F32)], axis=0)

    w_in_b = g_in.transpose(1, 0, 2).reshape(D, D_IN)

    h_t, proj = _in_proj(xs, vec, w_in_b, b_in, "in_proj")
    y_sb, sp_total, sweep_start, g_out, g_gu = _sb_forward(
        proj, [w_out[0].astype(BF16), w_gu[0].astype(BF16)], "sb_forward")
    w_gu_b = g_gu.transpose(1, 0, 2).reshape(D, 2 * D_FF)
    w_out_b = g_out.reshape(D, D)
    sink_vec = sinks[0]
    y_sw, g_down = _swa_forward(proj, sink_vec, [w_down[0].astype(BF16)], "swa_forward")
    w_down_b = g_down.reshape(D_FF, D)
    mixed_t, attn, x1, h2_b, h2_t = _post_attention(y_sb, y_sw, xs, vec, w_out_b, "post_attention")
    gu, act_t, ffn = _ffn_forward(h2_b, w_gu_b, w_down_b, "ffn_forward")

    def in_halves(shards):
        n_sh, rows, cols = shards.shape
        return shards.reshape(n_sh, 2, rows // 2, cols)

    core = ic.reshape(1).astype(jnp.int32)
    dffn_b, dgu_b, dx1, acc_f = _ffn_backward(x1, ffn, target, gu, vec, w_gu_b, w_down_b, "ffn_backward")
    dw_gu = _weight_grad(h2_t, dgu_b, "grad_w_gu", col_shards=4)
    dw_down = _weight_grad(act_t, dffn_b, "grad_w_down")
    du1, dattn_b, dy, acc_a = _attn_out_backward(dx1, xs, attn, y_sb, y_sw, vec, w_out_b, "attn_out_backward")
    dw_out = _weight_grad(mixed_t, dattn_b, "grad_w_out")
    first = [in_halves(dw_gu), in_halves(dw_down.reshape(4, D_FF // 4, D)), in_halves(dw_out.reshape(4, D // 4, D))]
    dq_sw, dk_sw, dv_sw, dsink, *got_first = _swa_backward(proj, y_sw, dy, sink_vec, first, "swa_backward")
    sums_first = _chip_sums(first, got_first, core, "grad_chip_sums")
    dq_sb, dk_sb, dv_sb, *parts_first = _sb_backward(proj, sp_total, sweep_start, dy, sums_first, "sb_backward")
    dproj_b, grad_x, acc_i, acc_b = _in_proj_backward(dq_sb, dk_sb, dv_sb, dq_sw, dk_sw, dv_sw, du1, xs, vec, w_in_b,
                                                      "in_proj_backward")
    dw_in = _weight_grad(h_t, dproj_b, "grad_w_in")
    last = [in_halves(_col_shards(dw_in, 4))]
    sums_last = _chip_sums(last, _halves_swap(last, "grad_halves_swap_in"), core, "grad_chip_sum_in")

    dmod = jnp.concatenate([acc_i[C_SHA:C_SHA + 1], acc_i[C_SCA:C_SCA + 1], acc_a[B_GA:B_GA + 1],
                            acc_f[A_SHF:A_SHF + 1], acc_f[A_SCF:A_SCF + 1], acc_f[A_GF:A_GF + 1]], axis=1)
    dsink_row = jnp.concatenate([dsink[:, 0].reshape(1, 8), jnp.zeros((1, LANES - 8), F32)], axis=1)
    loss_row = jnp.concatenate([jnp.sum(acc_f[A_LOSS:A_LOSS + 1], axis=1, keepdims=True),
                                jnp.zeros((1, LANES - 1), F32)], axis=1)
    small = jnp.concatenate([dmod, acc_b[0:1], acc_a[B_LN1G:B_LN1G + 1], acc_a[B_LN1B:B_LN1B + 1],
                             acc_f[A_LN2G:A_LN2G + 1], acc_f[A_LN2B:A_LN2B + 1], acc_a[B_GN:B_GN + 1],
                             dsink_row, loss_row], axis=1)
    small_rows, *parts_last = _allgather8(_pad_rows(small, 8), "gather_small", scatter=sums_last)
    small_all = small_rows[::8]

    mine = _sum4s([*parts_first, *parts_last], "grad_reduce")
    theirs = _sibling_send(mine, "grad_half_return")
    gw_gu, gw_down, gw_out, gw_in = [
        jnp.concatenate([jnp.where(ic == 0, m_, t_), jnp.where(ic == 0, t_, m_)], axis=0) for m_, t_ in zip(mine, theirs)]

    small_names = ["b_ada", "b_in", "ln1_g", "ln1_b", "ln2_g", "ln2_b", "gn_sb", "gn_swa", "sinks"]
    small_at = [SM_MOD, SM_BIN, SM_LN1G, SM_LN1B, SM_LN2G, SM_LN2B, SM_GN, SM_GN + SB_W, SM_SINK]
    *small_out, loss_row_all = _small_update(
        small_all, small_at,
        [b_ada, b_in, ln1_g, ln1_b, ln2_g, ln2_b, gn_sb, gn_swa, sinks],
        [m_b_ada, m_b_in, m_ln1_g, m_ln1_b, m_ln2_g, m_ln2_b, m_gn_sb, m_gn_swa, m_sinks],
        [v_b_ada, v_b_in, v_ln1_g, v_ln1_b, v_ln2_g, v_ln2_b, v_gn_sb, v_gn_swa, v_sinks], SM_LOSS, "small_update")
    g_small, d_small, m2_small, v2_small = [dict(zip(small_names, leaves)) for leaves in small_out]
    loss = loss_row_all[0, 0]

    dmod_cols = lax.dynamic_slice_in_dim(small_all[:, SM_MOD:SM_BIN], chip * n_ada, n_ada, axis=1)
    gw_ada = _weight_grad(_pad_rows(silu_c, LANES).astype(BF16).T, _pad_rows(dmod_cols, LANES).astype(BF16), "grad_w_ada")

    big = {}
    for nm, w, g, m, v in (("w_ada", w_ada, gw_ada, m_w_ada, v_w_ada), ("w_in", w_in, gw_in, m_w_in, v_w_in),
                           ("w_out", w_out, gw_out, m_w_out, v_w_out), ("w_gu", w_gu, gw_gu, m_w_gu, v_w_gu),
                           ("w_down", w_down, gw_down, m_w_down, v_w_down)):
        d_, m2_, v2_ = _adamw(w[0], g, m[0], v[0], "adamw_" + nm)
        big[nm] = (g[None], d_[None], m2_[None], v2_[None])

    order = ["w_ada", "b_ada", "w_in", "b_in", "sinks", "gn_sb", "gn_swa", "w_out", "ln1_g", "ln1_b", "w_gu", "w_down",
             "ln2_g", "ln2_b"]

    def leaf(nm, which):
        if nm in big:
            return big[nm][which]
        return (g_small, d_small, m2_small, v2_small)[which][nm]

    outs = [loss, grad_x[None]]
    for which in range(4):
        outs += [leaf(nm, which) for nm in order]
    return tuple(outs)
```

```python
import math

import jax
import jax.numpy as jnp
from jax import lax
from jax.experimental import pallas as pl
from jax.experimental.pallas import tpu as pltpu

F32 = jnp.float32
BF16 = jnp.bfloat16

D = 1024
HEAD_DIM = 64
SB_W = 512
SWA_QW = 512
SWA_KW = 128
D_IN = 2304
D_FF = 2816
WINDOW = 128
ALPHA = 2.0 ** 0.25
LN_EPS = 1e-5
RMS_EPS = 1e-6
MASK_VALUE = -1e30
QK_SCALE = 1.0 / math.sqrt(HEAD_DIM)

ADAM_LR = 0.001
ADAM_B1 = 0.9
ADAM_B2 = 0.999
ADAM_EPS = 1e-08
ADAM_WD = 0.01
ADAM_STEP = 10

N_CHIPS = 4
N_DEV = 8
LANES = 128

SB_TQ = 512
SB_TK = 256
SB_DEAD_MASS = 110.0
TOK_TILE = 512
FFN_TILE = 256
FFN_BWD_TILE = 256
VMEM_LIMIT = 56 * 1024 * 1024
WGRAD_TOKENS = 2048
WGRAD_VMEM = 40 * 1024 * 1024

V_SH_A, V_SC_A, V_G_A, V_SH_F, V_SC_F, V_G_F, V_LN1G, V_LN1B, V_LN2G, V_LN2B, V_GN = range(11)
VEC_ROWS = 16

SM_MOD = 0
SM_BIN = 6 * D
SM_LN1G = SM_BIN + D_IN
SM_LN1B = SM_LN1G + D
SM_LN2G = SM_LN1B + D
SM_LN2B = SM_LN2G + D
SM_GN = SM_LN2B + D
SM_SINK = SM_GN + D
SM_LOSS = SM_SINK + LANES
SM_LEN = SM_LOSS + LANES

MESH = pl.DeviceIdType.MESH


def _cparams(**kw):
    return pltpu.CompilerParams(vmem_limit_bytes=VMEM_LIMIT, **kw)


def _resident(shape):
    nd = len(shape)
    return pl.BlockSpec(shape, lambda *_: (0,) * nd, pipeline_mode=pl.Buffered(1))


def _dot(a, b):
    return jnp.dot(a, b, preferred_element_type=F32)


def _dot_nt(a, b):
    return lax.dot_general(a, b, (((1,), (1,)), ((), ())), preferred_element_type=F32)


def _dot_tn(a, b):
    return lax.dot_general(a, b, (((0,), (0,)), ((), ())), preferred_element_type=F32)


def _sum_matrix(tk, keep):
    row = lax.broadcasted_iota(jnp.int32, (tk, tk + LANES), 0)
    col = lax.broadcasted_iota(jnp.int32, (tk, tk + LANES), 1)
    return (keep(row, col) | (col >= tk)).astype(BF16)


def _block_sums(x, m):
    tk = x.shape[1]
    res = _dot(x.astype(BF16), m)
    return res[:, :tk], res[:, tk:]


def _before(t0, n, s0, tk):
    return s0 + lax.broadcasted_iota(jnp.int32, (n, tk), 1) < t0 + lax.broadcasted_iota(jnp.int32, (n, tk), 0)


class _RowState:
    def __init__(self, ref, tk, add=False):
        self.ref, self.tk, self.add, self.vals = ref, tk, add, {}

    def _blocks(self, r0, n):
        return range(r0 // self.tk, (r0 + n) // self.tk)

    def get(self, hd, r0, n):
        for d in self._blocks(r0, n):
            if (hd, d) not in self.vals:
                self.vals[(hd, d)] = self.ref[hd, d * self.tk:(d + 1) * self.tk, :]
        parts = [self.vals[(hd, d)] for d in self._blocks(r0, n)]
        return parts[0] if len(parts) == 1 else jnp.concatenate(parts, axis=0)

    def put(self, hd, r0, n, val):
        for k, d in enumerate(self._blocks(r0, n)):
            self.vals[(hd, d)] = val[k * self.tk:(k + 1) * self.tk]

    def accumulate(self, hd, r0, n, val):
        for k, d in enumerate(self._blocks(r0, n)):
            part = val[k * self.tk:(k + 1) * self.tk]
            self.vals[(hd, d)] = part if (hd, d) not in self.vals else self.vals[(hd, d)] + part

    def store(self):
        for (hd, d), val in self.vals.items():
            span = slice(d * self.tk, (d + 1) * self.tk)
            if self.add:
                self.ref[hd, span, :] += val
            else:
                self.ref[hd, span, :] = val


def _across(v, tk):
    return jnp.concatenate([v] * (tk // LANES), axis=1)


def _allgather8(v, name, gather=(), scatter=()):
    m_per, n = v.shape
    n_g, n_s = len(gather), len(scatter)

    def body(x_ref, *refs):
        g_in, s_in = refs[:n_g], refs[n_g:n_g + n_s]
        out_ref = refs[n_g + n_s]
        g_out, s_out = refs[n_g + n_s + 1:2 * n_g + n_s + 1], refs[2 * n_g + n_s + 1:2 * (n_g + n_s) + 1]
        send_sems, recv_sems, local_sem, *more_sems = refs[2 * (n_g + n_s) + 1:]
        beside = ([_gather_exchange(g_in, g_out, *more_sems[:3])] if n_g else []) + (
            [_scatter_exchange(s_in, s_out, *more_sems[-3:])] if n_s else [])
        for ex in beside:
            ex.start()
        x, y, c = lax.axis_index("x"), lax.axis_index("y"), lax.axis_index("c")
        me, sibling = (x, y, c), (x, y, 1 - c)
        chips = [(1 - x, y), (x, 1 - y), (1 - x, 1 - y)]

        def rows(px, py, pc):
            return out_ref.at[pl.ds((4 * px + 2 * py + pc) * m_per, m_per), :]

        def copy(k, block, to, src=None):
            return pltpu.make_async_remote_copy(
                src_ref=rows(*block) if src is None else src, dst_ref=rows(*block),
                send_sem=send_sems.at[k], recv_sem=recv_sems.at[k], device_id=to, device_id_type=MESH)

        mine = pltpu.make_async_copy(x_ref, rows(*me), local_sem)
        mine.start()
        first = [copy(0, me, sibling, src=x_ref)]
        first += [copy(1 + j, me, (*chip, c), src=x_ref) for j, chip in enumerate(chips)]
        for cp in first:
            cp.start()
        passed = [copy(4 + j, (*chip, c), sibling) for j, chip in enumerate(chips)]
        for j, chip in enumerate(chips):
            copy(1 + j, (*chip, c), me).wait_recv()
            passed[j].start()
        copy(0, sibling, me).wait_recv()
        for j, chip in enumerate(chips):
            copy(4 + j, (*chip, 1 - c), me).wait_recv()
        for cp in first + passed:
            cp.wait_send()
        mine.wait()
        for ex in beside:
            ex.wait()

    hbm = pl.BlockSpec(memory_space=pl.ANY)
    return pl.pallas_call(
        body, name=name,
        out_shape=[jax.ShapeDtypeStruct((N_DEV * m_per, n), v.dtype)]
        + [jax.ShapeDtypeStruct((N_CHIPS,) + a.shape, a.dtype) for a in gather]
        + [jax.ShapeDtypeStruct(p.shape, p.dtype) for p in scatter],
        in_specs=[pl.BlockSpec(memory_space=pltpu.VMEM)] + [hbm] * (n_g + n_s),
        out_specs=[pl.BlockSpec(memory_space=pltpu.VMEM)] + [hbm] * (n_g + n_s),
        scratch_shapes=[pltpu.SemaphoreType.DMA((7,)), pltpu.SemaphoreType.DMA((7,)), pltpu.SemaphoreType.DMA]
        + (_exchange_sems(n_g) if n_g else []) + (_exchange_sems(n_s) if n_s else []),
        compiler_params=_cparams(),
    )(v, *gather, *scatter)


class _Exchange:
    def __init__(self, local, sends, arrivals):
        self.local, self.sends, self.arrivals = local, sends, arrivals

    def start(self):
        for cp in self.local + self.sends:
            cp.start()

    def wait(self):
        for cp in self.arrivals:
            cp.wait_recv()
        for cp in self.sends:
            cp.wait_send()
        for cp in self.local:
            cp.wait()


def _exchange_sems(n):
    return [pltpu.SemaphoreType.DMA((3 * n,)), pltpu.SemaphoreType.DMA((3 * n,)), pltpu.SemaphoreType.DMA((n,))]


def _gather_exchange(ins, outs, send_sems, recv_sems, local_sems):
    x, y, c = lax.axis_index("x"), lax.axis_index("y"), lax.axis_index("c")
    slot = 2 * x + y
    chips = [(1 - x, y), (x, 1 - y), (1 - x, 1 - y)]
    local, sends, arrivals = [], [], []
    for a in range(len(ins)):
        local.append(pltpu.make_async_copy(ins[a], outs[a].at[slot], local_sems.at[a]))
        for j, (px, py) in enumerate(chips):
            sems = dict(send_sem=send_sems.at[3 * a + j], recv_sem=recv_sems.at[3 * a + j],
                        device_id=(px, py, c), device_id_type=MESH)
            sends.append(pltpu.make_async_remote_copy(src_ref=ins[a], dst_ref=outs[a].at[slot], **sems))
            arrivals.append(pltpu.make_async_remote_copy(src_ref=ins[a], dst_ref=outs[a].at[2 * px + py], **sems))
    return _Exchange(local, sends, arrivals)


def _scatter_exchange(p_refs, out_refs, send_sems, recv_sems, local_sems):
    x, y, c = lax.axis_index("x"), lax.axis_index("y"), lax.axis_index("c")
    slot = 2 * x + y
    chips = [(1 - x, y), (x, 1 - y), (1 - x, 1 - y)]
    local, sends, arrivals = [], [], []
    for a, (p_ref, out_ref) in enumerate(zip(p_refs, out_refs)):
        local.append(pltpu.make_async_copy(p_ref.at[slot], out_ref.at[slot], local_sems.at[a]))
        for j, (px, py) in enumerate(chips):
            sems = dict(send_sem=send_sems.at[3 * a + j], recv_sem=recv_sems.at[3 * a + j],
                        device_id=(px, py, c), device_id_type=MESH)
            sends.append(pltpu.make_async_remote_copy(src_ref=p_ref.at[2 * px + py], dst_ref=out_ref.at[slot], **sems))
            arrivals.append(pltpu.make_async_remote_copy(src_ref=p_ref.at[slot], dst_ref=out_ref.at[2 * px + py], **sems))
    return _Exchange(local, sends, arrivals)


def _sibling_halves(give_refs, got_refs, send_sems, recv_sems):
    x, y, c = lax.axis_index("x"), lax.axis_index("y"), lax.axis_index("c")
    copies = []
    for a, (give_ref, got_ref) in enumerate(zip(give_refs, got_refs)):
        for s in range(N_CHIPS):
            copies.append(pltpu.make_async_remote_copy(
                src_ref=give_ref.at[s, 1 - c], dst_ref=got_ref.at[s], send_sem=send_sems.at[N_CHIPS * a + s],
                recv_sem=recv_sems.at[N_CHIPS * a + s], device_id=(x, y, 1 - c), device_id_type=MESH))
    return copies


def _halves_shapes(arrs):
    return [jax.ShapeDtypeStruct((a.shape[0],) + a.shape[2:], a.dtype) for a in arrs]


def _halves_sems(n):
    return [pltpu.SemaphoreType.DMA((N_CHIPS * n,)), pltpu.SemaphoreType.DMA((N_CHIPS * n,))]


def _halves_swap(arrs, name):
    n = len(arrs)

    def body(*refs):
        copies = _sibling_halves(refs[:n], refs[n:2 * n], *refs[2 * n:])
        for cp in copies:
            cp.start()
        for cp in copies:
            cp.wait()

    hbm = pl.BlockSpec(memory_space=pl.ANY)
    return pl.pallas_call(body, name=name, out_shape=_halves_shapes(arrs), in_specs=[hbm] * n, out_specs=[hbm] * n,
                          scratch_shapes=_halves_sems(n), compiler_params=_cparams())(*arrs)


def _sibling_send(arrs, name):
    n = len(arrs)

    def body(*refs):
        x, y, c = lax.axis_index("x"), lax.axis_index("y"), lax.axis_index("c")
        send_sems, recv_sems = refs[2 * n:]
        copies = [pltpu.make_async_remote_copy(src_ref=refs[a], dst_ref=refs[n + a], send_sem=send_sems.at[a],
                                               recv_sem=recv_sems.at[a], device_id=(x, y, 1 - c), device_id_type=MESH)
                  for a in range(n)]
        for cp in copies:
            cp.start()
        for cp in copies:
            cp.wait()

    hbm = pl.BlockSpec(memory_space=pl.ANY)
    return pl.pallas_call(
        body, name=name, out_shape=[jax.ShapeDtypeStruct(a.shape, a.dtype) for a in arrs],
        in_specs=[hbm] * n, out_specs=[hbm] * n,
        scratch_shapes=[pltpu.SemaphoreType.DMA((n,)), pltpu.SemaphoreType.DMA((n,))],
        compiler_params=_cparams(),
    )(*arrs)


def _row_tile(h):
    return h // 2 if (h // 2) % 8 == 0 else h


def _row_tiles(hs):
    tiles = [_row_tile(h) for h in hs]
    assert len({h // t for h, t in zip(hs, tiles)}) == 1
    return tiles, hs[0] // tiles[0]


def _chip_sums(arrs, gots, core, name):
    n = len(arrs)
    tiles, steps = _row_tiles([a.shape[2] for a in arrs])

    def body(core_ref, *refs):
        for a_ref, b_ref, o_ref in zip(refs[:n], refs[n:2 * n], refs[2 * n:]):
            o_ref[...] = a_ref[...] + b_ref[...]

    slabs = [pl.BlockSpec((None, tr, a.shape[3]), lambda s, i, core_ref: (s, i, 0)) for a, tr in zip(arrs, tiles)]
    grid_spec = pltpu.PrefetchScalarGridSpec(
        num_scalar_prefetch=1, grid=(N_CHIPS, steps),
        in_specs=[pl.BlockSpec((None, None, tr, a.shape[3]), lambda s, i, core_ref: (s, core_ref[0], i, 0))
                  for a, tr in zip(arrs, tiles)] + slabs,
        out_specs=slabs)
    return pl.pallas_call(body, name=name, grid_spec=grid_spec,
                          out_shape=[jax.ShapeDtypeStruct(g.shape, g.dtype) for g in gots],
                          compiler_params=_cparams())(core, *arrs, *gots)


def _sum4s(ps, name):
    tiles, steps = _row_tiles([p.shape[1] for p in ps])

    def body(*refs):
        for p_ref, o_ref in zip(refs[:len(ps)], refs[len(ps):]):
            o_ref[...] = ((p_ref[0] + p_ref[1]) + p_ref[2]) + p_ref[3]

    return pl.pallas_call(
        body, name=name, grid=(steps,), out_shape=[jax.ShapeDtypeStruct(p.shape[1:], p.dtype) for p in ps],
        in_specs=[pl.BlockSpec((4, tr, p.shape[2]), lambda i: (0, i, 0)) for p, tr in zip(ps, tiles)],
        out_specs=[pl.BlockSpec((tr, p.shape[2]), lambda i: (i, 0)) for p, tr in zip(ps, tiles)],
        compiler_params=_cparams())(*ps)


def _adam_math(w, g, m, v):
    m2 = ADAM_B1 * m + (1.0 - ADAM_B1) * g
    v2 = ADAM_B2 * v + (1.0 - ADAM_B2) * (g * g)
    m_hat = m2 / (1.0 - ADAM_B1 ** ADAM_STEP)
    v_hat = v2 / (1.0 - ADAM_B2 ** ADAM_STEP)
    delta = -ADAM_LR * (m_hat / (jnp.sqrt(v_hat) + ADAM_EPS) + ADAM_WD * w)
    return delta, m2, v2


def _adamw(w, g, m, v, name):
    rows, cols = w.shape
    tr = rows // 4 if rows % 32 == 0 else rows

    def body(w_ref, g_ref, m_ref, v_ref, d_ref, m2_ref, v2_ref):
        delta, m2, v2 = _adam_math(w_ref[...], g_ref[...], m_ref[...], v_ref[...])
        d_ref[...] = delta
        m2_ref[...] = m2
        v2_ref[...] = v2

    spec = pl.BlockSpec((tr, cols), lambda i: (i, 0))
    shp = jax.ShapeDtypeStruct(w.shape, F32)
    return pl.pallas_call(body, name=name, grid=(rows // tr,), out_shape=[shp, shp, shp],
                          in_specs=[spec] * 4, out_specs=[spec] * 3, compiler_params=_cparams())(w, g, m, v)


def _small_update(g8, offsets, ws, ms, vs, loss_at, name):
    k = len(ws)

    def summed(g8_ref, lo, width):
        g = g8_ref[0:1, lo:lo + width]
        for r in range(1, N_DEV):
            g = g + g8_ref[r:r + 1, lo:lo + width]
        return g

    def body(g8_ref, *refs):
        ins, outs = refs[:3 * k], refs[3 * k:]
        for j in range(k):
            g = summed(g8_ref, offsets[j], ws[j].shape[1])
            delta, m2, v2 = _adam_math(ins[j][...], g, ins[k + j][...], ins[2 * k + j][...])
            for kind, val in enumerate((g, delta, m2, v2)):
                outs[kind * k + j][...] = val
        outs[4 * k][...] = summed(g8_ref, loss_at, LANES)

    vm = pl.BlockSpec(memory_space=pltpu.VMEM)
    shapes = [jax.ShapeDtypeStruct(w.shape, F32) for w in ws] * 4 + [jax.ShapeDtypeStruct((1, LANES), F32)]
    res = pl.pallas_call(body, name=name, out_shape=shapes, in_specs=[vm] * (1 + 3 * k), out_specs=[vm] * (4 * k + 1),
                         compiler_params=_cparams())(g8, *ws, *ms, *vs)
    return res[:k], res[k:2 * k], res[2 * k:3 * k], res[3 * k:4 * k], res[4 * k]


def _mod_shard(c8, w_ada, b_ada_shard, name):
    n = w_ada.shape[1]
    tn = 512

    def body(c_ref, w_ref, b_ref, o_ref, s_ref):
        cv = c_ref[...]
        sc = cv * (1.0 / (1.0 + jnp.exp(-cv)))
        s_ref[...] = sc
        o_ref[...] = _dot(sc.astype(BF16), w_ref[...].astype(BF16)) + b_ref[...]

    return pl.pallas_call(
        body, name=name, grid=(n // tn,),
        out_shape=[jax.ShapeDtypeStruct((8, n), F32), jax.ShapeDtypeStruct((8, D), F32)],
        in_specs=[pl.BlockSpec((8, D), lambda j: (0, 0)), pl.BlockSpec((D, tn), lambda j: (0, j)),
                  pl.BlockSpec((1, tn), lambda j: (0, j))],
        out_specs=[pl.BlockSpec((8, tn), lambda j: (0, j)), pl.BlockSpec((8, D), lambda j: (0, 0))],
        compiler_params=_cparams())(c8, w_ada, b_ada_shard)


def _layer_norm_stats(u):
    mu = jnp.mean(u, axis=1, keepdims=True)
    d = u - mu
    var = jnp.mean(d * d, axis=1, keepdims=True)
    rstd = lax.rsqrt(var + LN_EPS)
    return d * rstd, rstd


def _in_proj(x, vec, w_in, b_in, name):
    s = x.shape[0]
    tb = min(TOK_TILE, s)

    def body(x_ref, vec_ref, w_ref, b_ref, ht_ref, p_ref):
        h = x_ref[...] * (1.0 + vec_ref[V_SC_A:V_SC_A + 1, :]) + vec_ref[V_SH_A:V_SH_A + 1, :]
        hb = h.astype(BF16)
        ht_ref[...] = h.T.astype(BF16)
        proj = _dot(hb, w_ref[...]) + b_ref[...]
        col = lax.broadcasted_iota(jnp.int32, (1, D_IN), 1)
        is_q = (col < SB_W) | ((col >= 3 * SB_W) & (col < 3 * SB_W + SWA_QW))
        p_ref[...] = (proj * jnp.where(is_q, QK_SCALE, 1.0)).astype(BF16)

    return pl.pallas_call(
        body, name=name, grid=(s // tb,),
        out_shape=[jax.ShapeDtypeStruct((D, s), BF16), jax.ShapeDtypeStruct((s, D_IN), BF16)],
        in_specs=[pl.BlockSpec((tb, D), lambda i: (i, 0)), _resident((VEC_ROWS, D)), _resident((D, D_IN)),
                  _resident((1, D_IN))],
        out_specs=[pl.BlockSpec((D, tb), lambda i: (0, i)), pl.BlockSpec((tb, D_IN), lambda i: (i, 0))],
        compiler_params=_cparams())(x, vec, w_in, b_in)


def _softplus_parts(z):
    e1 = jnp.exp(-jnp.abs(z))
    sp = jnp.maximum(z, 0.0) + jnp.log(1.0 + e1)
    return sp, e1


def _sb_forward(proj, shards, name):
    s = proj.shape[0]
    tq, tk = min(SB_TQ, s), min(SB_TK, s)
    r = tq // tk

    n_sh = len(shards)
    nkb = SB_W // LANES
    nq = s // tq

    def body(q_ref, k_ref, v_ref, *refs):
        sh_refs, (o_ref, tot_ref, start_ref), got_refs = refs[:n_sh], refs[n_sh:n_sh + 3], refs[n_sh + 3:2 * n_sh + 3]
        acc_refs, run_refs = refs[2 * n_sh + 3:2 * n_sh + 5]
        i = pl.program_id(1)
        step = pl.program_id(0) * nq + i
        gather = _gather_exchange(sh_refs, got_refs, *refs[2 * n_sh + 5:])

        @pl.when(step == 0)
        def _():
            gather.start()

        lane = lax.broadcasted_iota(jnp.int32, (1, LANES), 1)
        first = lane < HEAD_DIM
        qp = q_ref[...]
        zero = jnp.zeros((), BF16)
        qs = (jnp.where(first, qp, zero), jnp.where(first, zero, qp))
        later = _sum_matrix(tk, lambda row, col: row > col)
        acc_refs[...] = jnp.zeros_like(acc_refs)
        run_refs[...] = jnp.zeros_like(run_refs)

        def blocks(tiles):
            rows = [slice(r0, r0 + n) for r0, n, _, _ in tiles]
            kjs = [k_ref[pl.ds(pl.multiple_of(j * tk, tk), tk), :] for _, _, j, _ in tiles]
            vjs = [v_ref[pl.ds(pl.multiple_of(j * tk, tk), tk), :] for _, _, j, _ in tiles]
            chains = [(hd, t) for t in range(len(tiles)) for hd in range(2)]
            zs = [_dot_nt(qs[hd][rows[t]], kjs[t]) for hd, t in chains]
            sps = [_softplus_parts(z)[0] for z in zs]
            befores = [_before(i * tq + r0, n, j * tk, tk) if diag else None for r0, n, j, diag in tiles]
            spms = [sp if befores[t] is None else jnp.where(befores[t], sp, 0.0) for (hd, t), sp in zip(chains, sps)]
            cums = [_block_sums(spm, later) for spm in spms]
            runs, accs, ws = _RowState(run_refs, tk), _RowState(acc_refs, tk, add=True), []
            for (hd, t), z, sp, (cum, sm) in zip(chains, zs, sps, cums):
                r0, n = tiles[t][:2]
                run = runs.get(hd, r0, n)
                w = jnp.exp(z - sp - cum - _across(run, tk))
                if befores[t] is not None:
                    w = jnp.where(befores[t], w, 0.0)
                ws.append(w.astype(BF16))
                runs.put(hd, r0, n, run + sm)
            for (hd, t), pv in zip(chains, [_dot(w, vjs[t]) for (hd, t), w in zip(chains, ws)]):
                accs.accumulate(hd, *tiles[t][:2], pv)
            accs.store()
            runs.store()

        below = i * r
        diagonal = [(d * tk, tk, below + e, e == d) for d in range(r) for e in range(d, -1, -1)]

        @pl.when(i == 0)
        def _():
            blocks(diagonal)

        @pl.when(i > 0)
        def _():
            blocks(diagonal + [(0, tq, below - 1, False)])

        def swept_mass():
            return jnp.min(jnp.minimum(run_refs[0], run_refs[1]))

        def more(carry):
            n, mass = carry
            return (n < below) & (mass < SB_DEAD_MASS)

        def sweep(carry):
            n, _ = carry
            blocks([(0, tq, below - 1 - n, False)])
            return n + 1, swept_mass()

        n_swept, _ = lax.while_loop(more, sweep, (jnp.minimum(below, 1), swept_mass()))
        start_ref[pl.program_id(0), i] = (below - n_swept).astype(F32)
        o_ref[...] = jnp.where(first, acc_refs[0], acc_refs[1])
        tot_ref[...] = jnp.where(first, run_refs[0], run_refs[1])

        @pl.when(step == nkb * nq - 1)
        def _():
            gather.wait()

    shp = jax.ShapeDtypeStruct((s, SB_W), F32)
    qspec = pl.BlockSpec((tq, LANES), lambda p, i: (i, p))
    hbm = pl.BlockSpec(memory_space=pl.ANY)
    return pl.pallas_call(
        body, name=name, grid=(nkb, nq),
        out_shape=[shp, shp, jax.ShapeDtypeStruct((nkb, nq), F32)]
        + [jax.ShapeDtypeStruct((N_CHIPS,) + a.shape, a.dtype) for a in shards],
        in_specs=[qspec,
                  pl.BlockSpec((s, LANES), lambda p, i: (0, nkb + p)),
                  pl.BlockSpec((s, LANES), lambda p, i: (0, 2 * nkb + p))] + [hbm] * n_sh,
        out_specs=[qspec, qspec, pl.BlockSpec(memory_space=pltpu.SMEM)] + [hbm] * n_sh,
        scratch_shapes=[pltpu.VMEM((2, tq, LANES), F32), pltpu.VMEM((2, tq, LANES), F32)] + _exchange_sems(n_sh),
        compiler_params=_cparams())(proj, proj, proj, *shards)


def _swa_masks(n):
    ti = lax.broadcasted_iota(jnp.int32, (WINDOW, 2 * WINDOW), 0)
    kj = lax.broadcasted_iota(jnp.int32, (WINDOW, 2 * WINDOW), 1)
    dist = ti + WINDOW - kj
    valid = (dist >= 0) & (dist < WINDOW) & ((n * WINDOW - WINDOW + kj) >= 0)
    return valid, dist.astype(F32)


def _swa_probs(sc, valid, distf, h, sink):
    slope = 2.0 ** (-(h + 1))
    sc = jnp.where(valid, sc - slope * distf, MASK_VALUE)
    mx = jnp.maximum(jnp.max(sc, axis=1, keepdims=True), sink)
    p = jnp.exp(sc - mx)
    es = jnp.exp(sink - mx)
    inv = 1.0 / (jnp.sum(p, axis=1, keepdims=True) + es)
    return p * inv, es * inv


def _swa_forward(proj, sinks, shards, name):
    s = proj.shape[0]
    nb = s // WINDOW
    qb, kb, vb = 3 * SB_W // SWA_QW, (3 * SB_W + SWA_QW) // LANES, (3 * SB_W + SWA_QW + SWA_KW) // LANES
    n_sh = len(shards)

    def body(q_ref, kp_ref, kc_ref, vp_ref, vc_ref, sink_ref, *refs):
        sh_refs, o_ref, got_refs = refs[:n_sh], refs[n_sh], refs[n_sh + 1:2 * n_sh + 1]
        n = pl.program_id(0)
        gather = _gather_exchange(sh_refs, got_refs, *refs[2 * n_sh + 1:])

        @pl.when(n == 0)
        def _():
            gather.start()

        k = jnp.concatenate([kp_ref[...], kc_ref[...]], axis=0)
        v = jnp.concatenate([vp_ref[...], vc_ref[...]], axis=0)
        k_sw = pltpu.roll(k.astype(F32), HEAD_DIM, 1).astype(BF16)
        v_sw = pltpu.roll(v.astype(F32), HEAD_DIM, 1).astype(BF16)
        lane = lax.broadcasted_iota(jnp.int32, (1, LANES), 1)
        halves = [lane < HEAD_DIM, lane >= HEAD_DIM]
        valid, distf = _swa_masks(n)
        heads = range(2 * 4)
        qms = [jnp.where(halves[h % 2], q_ref[:, (h // 2) * LANES:(h // 2 + 1) * LANES], jnp.zeros((), BF16))
               for h in heads]
        kus = [k if h // 4 == h % 2 else k_sw for h in heads]
        vus = [v if h // 4 == h % 2 else v_sw for h in heads]
        scores = [_dot_nt(qms[h], kus[h]) for h in heads]
        ps = [_swa_probs(scores[h], valid, distf, h, sink_ref[h])[0].astype(BF16) for h in heads]
        outs = [_dot(ps[h], vus[h]) for h in heads]
        for pair in range(4):
            o_ref[:, pair * LANES:(pair + 1) * LANES] = jnp.where(halves[0], outs[2 * pair], outs[2 * pair + 1])

        @pl.when(n == nb - 1)
        def _():
            gather.wait()

    prev = lambda n: jnp.maximum(n - 1, 0)
    hbm = pl.BlockSpec(memory_space=pl.ANY)
    return pl.pallas_call(
        body, name=name, grid=(nb,),
        out_shape=[jax.ShapeDtypeStruct((s, SWA_QW), F32)]
        + [jax.ShapeDtypeStruct((N_CHIPS,) + a.shape, a.dtype) for a in shards],
        in_specs=[pl.BlockSpec((WINDOW, SWA_QW), lambda n: (n, qb)),
                  pl.BlockSpec((WINDOW, LANES), lambda n: (prev(n), kb)),
                  pl.BlockSpec((WINDOW, LANES), lambda n: (n, kb)),
                  pl.BlockSpec((WINDOW, LANES), lambda n: (prev(n), vb)),
                  pl.BlockSpec((WINDOW, LANES), lambda n: (n, vb)),
                  pl.BlockSpec(memory_space=pltpu.SMEM)] + [hbm] * n_sh,
        out_specs=[pl.BlockSpec((WINDOW, SWA_QW), lambda n: (n, 0))] + [hbm] * n_sh,
        scratch_shapes=_exchange_sems(n_sh),
        compiler_params=_cparams())(proj, proj, proj, proj, proj, sinks, *shards)


def _rms_parts(y):
    return lax.rsqrt(jnp.mean(y * y, axis=1, keepdims=True) + RMS_EPS)


def _post_attention(y_sb, y_sw, x, vec, w_out, name):
    s = x.shape[0]
    tb = min(TOK_TILE, s)

    def body(ysb_ref, ysw_ref, x_ref, vec_ref, w_ref, mixedt_ref, attn_ref, x1_ref, h2_ref, h2t_ref):
        ysb, ysw = ysb_ref[...], ysw_ref[...]
        nsb_f = ysb * _rms_parts(ysb) * vec_ref[V_GN:V_GN + 1, :SB_W]
        nsw_f = ysw * _rms_parts(ysw) * vec_ref[V_GN:V_GN + 1, SB_W:]
        nsb, nsw = nsb_f.astype(BF16), nsw_f.astype(BF16)
        mixedt_ref[:SB_W, :] = nsb_f.T.astype(BF16)
        mixedt_ref[SB_W:, :] = nsw_f.T.astype(BF16)
        attn = _dot(nsb, w_ref[:SB_W, :]) + _dot(nsw, w_ref[SB_W:, :])
        attn_ref[...] = attn
        u1 = ALPHA * x_ref[...] + (1.0 + vec_ref[V_G_A:V_G_A + 1, :]) * attn
        xhat, _ = _layer_norm_stats(u1)
        x1 = xhat * vec_ref[V_LN1G:V_LN1G + 1, :] + vec_ref[V_LN1B:V_LN1B + 1, :]
        x1_ref[...] = x1
        h2 = x1 * (1.0 + vec_ref[V_SC_F:V_SC_F + 1, :]) + vec_ref[V_SH_F:V_SH_F + 1, :]
        h2_ref[...] = h2.astype(BF16)
        h2t_ref[...] = h2.T.astype(BF16)

    half = pl.BlockSpec((tb, SB_W), lambda i: (i, 0))
    full = pl.BlockSpec((tb, D), lambda i: (i, 0))
    full_t = pl.BlockSpec((D, tb), lambda i: (0, i))
    return pl.pallas_call(
        body, name=name, grid=(s // tb,),
        out_shape=[jax.ShapeDtypeStruct((D, s), BF16), jax.ShapeDtypeStruct((s, D), F32),
                   jax.ShapeDtypeStruct((s, D), F32), jax.ShapeDtypeStruct((s, D), BF16),
                   jax.ShapeDtypeStruct((D, s), BF16)],
        in_specs=[half, half, full, _resident((VEC_ROWS, D)), _resident((D, D))],
        out_specs=[full_t, full, full, full, full_t],
        compiler_params=_cparams())(y_sb, y_sw, x, vec, w_out)


def _ffn_forward(h2, w_gu, w_down, name):
    s = h2.shape[0]
    tb = min(FFN_TILE, s)

    def body(h_ref, wgu_ref, wd_ref, gu_ref, actt_ref, ffn_ref):
        gu = _dot(h_ref[...], wgu_ref[...])
        gu_ref[...] = gu.astype(BF16)
        gate, up = gu[:, :D_FF], gu[:, D_FF:]
        act = gate * (1.0 / (1.0 + jnp.exp(-gate))) * up
        actt_ref[...] = act.T.astype(BF16)
        ffn_ref[...] = _dot(act.astype(BF16), wd_ref[...])

    return pl.pallas_call(
        body, name=name, grid=(s // tb,),
        out_shape=[jax.ShapeDtypeStruct((s, 2 * D_FF), BF16), jax.ShapeDtypeStruct((D_FF, s), BF16),
                   jax.ShapeDtypeStruct((s, D), F32)],
        in_specs=[pl.BlockSpec((tb, D), lambda i: (i, 0)), _resident((D, 2 * D_FF)), _resident((D_FF, D))],
        out_specs=[pl.BlockSpec((tb, 2 * D_FF), lambda i: (i, 0)), pl.BlockSpec((D_FF, tb), lambda i: (0, i)),
                   pl.BlockSpec((tb, D), lambda i: (i, 0))],
        compiler_params=_cparams())(h2, w_gu, w_down)


def _layer_norm_bwd(dxhat, xhat, rstd):
    m1 = jnp.mean(dxhat, axis=1, keepdims=True)
    m2 = jnp.mean(dxhat * xhat, axis=1, keepdims=True)
    return rstd * (dxhat - m1 - xhat * m2)


def _colsum(a):
    return jnp.sum(a, axis=0, keepdims=True)


A_LN2G, A_LN2B, A_GF, A_SCF, A_SHF, A_LOSS = range(6)
B_LN1G, B_LN1B, B_GA, B_GN = range(4)
C_SCA, C_SHA = range(2)


def _ffn_backward(x1, ffn, target, gu, vec, w_gu, w_down, name):
    s = x1.shape[0]
    tb = min(FFN_BWD_TILE, s)

    def body(x1_ref, ffn_ref, t_ref, gu_ref, vec_ref, wgu_ref, wd_ref, dffn_ref, dgu_ref, dx1_ref, acc_ref):
        @pl.when(pl.program_id(0) == 0)
        def _():
            acc_ref[...] = jnp.zeros_like(acc_ref)

        x1v, ffn_v = x1_ref[...], ffn_ref[...]
        g_f = 1.0 + vec_ref[V_G_F:V_G_F + 1, :]
        u2 = ALPHA * x1v + g_f * ffn_v
        xhat, rstd = _layer_norm_stats(u2)
        ln_g = vec_ref[V_LN2G:V_LN2G + 1, :]
        err = xhat * ln_g + vec_ref[V_LN2B:V_LN2B + 1, :] - t_ref[...]
        dx2 = err * (1.0 / D)
        acc_ref[A_LOSS:A_LOSS + 1, :] += _colsum(err * err) * (0.5 / D)
        acc_ref[A_LN2G:A_LN2G + 1, :] += _colsum(dx2 * xhat)
        acc_ref[A_LN2B:A_LN2B + 1, :] += _colsum(dx2)
        du2 = _layer_norm_bwd(dx2 * ln_g, xhat, rstd)
        acc_ref[A_GF:A_GF + 1, :] += _colsum(du2 * ffn_v)
        dffn = (g_f * du2).astype(BF16)
        dffn_ref[...] = dffn
        dact = _dot_nt(dffn, wd_ref[...])
        gate, up = gu_ref[:, :D_FF].astype(F32), gu_ref[:, D_FF:].astype(F32)
        sg = 1.0 / (1.0 + jnp.exp(-gate))
        dgate = (dact * up * (sg * (1.0 + gate * (1.0 - sg)))).astype(BF16)
        dup = (dact * (gate * sg)).astype(BF16)
        dgu_ref[:, :D_FF] = dgate
        dgu_ref[:, D_FF:] = dup
        dh2 = _dot_nt(dgate, wgu_ref[:, :D_FF]) + _dot_nt(dup, wgu_ref[:, D_FF:])
        dx1_ref[...] = ALPHA * du2 + dh2 * (1.0 + vec_ref[V_SC_F:V_SC_F + 1, :])
        acc_ref[A_SCF:A_SCF + 1, :] += _colsum(dh2 * x1v)
        acc_ref[A_SHF:A_SHF + 1, :] += _colsum(dh2)

    full = pl.BlockSpec((tb, D), lambda i: (i, 0))
    wide = pl.BlockSpec((tb, 2 * D_FF), lambda i: (i, 0))
    return pl.pallas_call(
        body, name=name, grid=(s // tb,),
        out_shape=[jax.ShapeDtypeStruct((s, D), BF16), jax.ShapeDtypeStruct((s, 2 * D_FF), BF16),
                   jax.ShapeDtypeStruct((s, D), F32), jax.ShapeDtypeStruct((8, D), F32)],
        in_specs=[full, full, full, wide, _resident((VEC_ROWS, D)), _resident((D, 2 * D_FF)), _resident((D_FF, D))],
        out_specs=[full, wide, full, pl.BlockSpec((8, D), lambda i: (0, 0))],
        compiler_params=_cparams())(x1, ffn, target, gu, vec, w_gu, w_down)


def _attn_out_backward(dx1, x, attn, y_sb, y_sw, vec, w_out, name):
    s = x.shape[0]
    tb = min(TOK_TILE, s)

    def body(dx1_ref, x_ref, attn_ref, ysb_ref, ysw_ref, vec_ref, w_ref, du1_ref, dattn_ref, dy_ref, acc_ref):
        @pl.when(pl.program_id(0) == 0)
        def _():
            acc_ref[...] = jnp.zeros_like(acc_ref)

        attn = attn_ref[...]
        g_a = 1.0 + vec_ref[V_G_A:V_G_A + 1, :]
        xhat, rstd = _layer_norm_stats(ALPHA * x_ref[...] + g_a * attn)
        dx1v = dx1_ref[...]
        acc_ref[B_LN1G:B_LN1G + 1, :] += _colsum(dx1v * xhat)
        acc_ref[B_LN1B:B_LN1B + 1, :] += _colsum(dx1v)
        du1 = _layer_norm_bwd(dx1v * vec_ref[V_LN1G:V_LN1G + 1, :], xhat, rstd)
        du1_ref[...] = du1
        acc_ref[B_GA:B_GA + 1, :] += _colsum(du1 * attn)
        dattn = (g_a * du1).astype(BF16)
        dattn_ref[...] = dattn
        dmixed = _dot_nt(dattn, w_ref[...])
        for lo, y_ref in ((0, ysb_ref), (SB_W, ysw_ref)):
            y = y_ref[...]
            rr = _rms_parts(y)
            dn = dmixed[:, lo:lo + SB_W]
            acc_ref[B_GN:B_GN + 1, lo:lo + SB_W] += _colsum(dn * y * rr)
            dng = dn * vec_ref[V_GN:V_GN + 1, lo:lo + SB_W]
            dy_ref[:, lo:lo + SB_W] = rr * dng - y * (rr * rr * rr) * jnp.mean(dng * y, axis=1, keepdims=True)

    half = pl.BlockSpec((tb, SB_W), lambda i: (i, 0))
    full = pl.BlockSpec((tb, D), lambda i: (i, 0))
    return pl.pallas_call(
        body, name=name, grid=(s // tb,),
        out_shape=[jax.ShapeDtypeStruct((s, D), F32), jax.ShapeDtypeStruct((s, D), BF16),
                   jax.ShapeDtypeStruct((s, D), F32), jax.ShapeDtypeStruct((8, D), F32)],
        in_specs=[full, full, full, half, half, _resident((VEC_ROWS, D)), _resident((D, D))],
        out_specs=[full, full, full, pl.BlockSpec((8, D), lambda i: (0, 0))],
        compiler_params=_cparams())(dx1, x, attn, y_sb, y_sw, vec, w_out)


def _sb_backward(proj, sp_total, sweep_start, dy, slabs, name):
    s = proj.shape[0]
    tq, tk = min(SB_TQ, s), min(SB_TK, s)
    r = tq // tk
    nkb = SB_W // LANES
    nq = s // tq

    n_sl = len(slabs)

    def body(q_ref, k_ref, v_ref, tot_ref, do_ref, start_ref, *refs):
        slab_refs, (dq_ref, dk_ref, dv_ref), got_refs = refs[:n_sl], refs[n_sl:n_sl + 3], refs[n_sl + 3:2 * n_sl + 3]
        dq_acc, left_refs, gsum_refs = refs[2 * n_sl + 3:2 * n_sl + 6]
        i = pl.program_id(1)
        step = pl.program_id(0) * nq + i
        scatter = _scatter_exchange(slab_refs, got_refs, *refs[2 * n_sl + 6:])

        @pl.when(step == 0)
        def _():
            scatter.start()

        @pl.when(i == 0)
        def _():
            dk_ref[...] = jnp.zeros_like(dk_ref)
            dv_ref[...] = jnp.zeros_like(dv_ref)

        lane = lax.broadcasted_iota(jnp.int32, (1, LANES), 1)
        first = lane < HEAD_DIM
        qp, dop, totp = q_ref[...], do_ref[...], tot_ref[...]
        zero = jnp.zeros((), BF16)
        qs = (jnp.where(first, qp, zero), jnp.where(first, zero, qp))
        dofs = (jnp.where(first, dop, 0.0), jnp.where(first, 0.0, dop))
        dobs = tuple(d.astype(BF16) for d in dofs)
        dots = tuple(d.T.astype(BF16) for d in dofs)
        qts = tuple(qh.astype(F32).T.astype(BF16) for qh in qs)
        later = _sum_matrix(tk, lambda row, col: row > col)
        earlier = _sum_matrix(tk, lambda row, col: row < col)
        dq_acc[...] = jnp.zeros_like(dq_acc)
        gsum_refs[...] = jnp.zeros_like(gsum_refs)
        swapped = pltpu.roll(totp, HEAD_DIM, 1)
        left_refs[0] = jnp.where(first, totp, swapped)
        left_refs[1] = jnp.where(first, swapped, totp)

        def blocks(tiles):
            rows = [slice(r0, r0 + n) for r0, n, _, _ in tiles]
            kjs = [k_ref[pl.ds(pl.multiple_of(j * tk, tk), tk), :] for _, _, j, _ in tiles]
            vjs = [v_ref[pl.ds(pl.multiple_of(j * tk, tk), tk), :] for _, _, j, _ in tiles]
            chains = [(hd, t) for t in range(len(tiles)) for hd in range(2)]
            zs = [_dot_nt(qs[hd][rows[t]], kjs[t]) for hd, t in chains]
            dws = [_dot_nt(dobs[hd][rows[t]], vjs[t]) for hd, t in chains]
            parts = [_softplus_parts(z) for z in zs]
            sps = [p[0] for p in parts]
            befores = [_before(i * tq + r0, n, j * tk, tk) if diag else None for r0, n, j, diag in tiles]
            spms = [sp if befores[t] is None else jnp.where(befores[t], sp, 0.0) for (hd, t), sp in zip(chains, sps)]
            cums = [_block_sums(spm, later) for spm in spms]
            lefts, gsums, dq_sums = _RowState(left_refs, tk), _RowState(gsum_refs, tk), _RowState(dq_acc, tk, add=True)
            ws = []
            for (hd, t), z, sp, (cum, sm) in zip(chains, zs, sps, cums):
                r0, n = tiles[t][:2]
                left = lefts.get(hd, r0, n) - sm
                lefts.put(hd, r0, n, left)
                w = jnp.exp(z - sp - cum - _across(left, tk))
                ws.append(w if befores[t] is None else jnp.where(befores[t], w, 0.0))
            wbs = [w.astype(BF16) for w in ws]
            dvs = [_dot(dots[hd][:, rows[t]], wb) for (hd, t), wb in zip(chains, wbs)]
            gs = [dw * w for dw, w in zip(dws, ws)]
            gcums = [_block_sums(g, earlier) for g in gs]
            dzbs = []
            for (hd, t), z, (sp, e1), g, (gcum, gsm) in zip(chains, zs, parts, gs, gcums):
                r0, n = tiles[t][:2]
                gsum = gsums.get(hd, r0, n)
                inv = 1.0 / (1.0 + e1)
                sig = jnp.where(z >= 0.0, inv, e1 * inv)
                dz = g - sig * (g + _across(gsum, tk) + gcum)
                dzbs.append((dz if befores[t] is None else jnp.where(befores[t], dz, 0.0)).astype(BF16))
                gsums.put(hd, r0, n, gsum + gsm)
            dqs = [_dot(dzb, kjs[t]) for (hd, t), dzb in zip(chains, dzbs)]
            dks = [_dot(qts[hd][:, rows[t]], dzb) for (hd, t), dzb in zip(chains, dzbs)]
            for t, (_, _, j, _) in enumerate(tiles):
                dv_ref[j] += dvs[2 * t] + dvs[2 * t + 1]
                dk_ref[j] += dks[2 * t] + dks[2 * t + 1]
            for (hd, t), dq in zip(chains, dqs):
                dq_sums.accumulate(hd, *tiles[t][:2], dq)
            dq_sums.store()
            lefts.store()
            gsums.store()

        below = i * r
        start = jnp.clip(start_ref[pl.program_id(0), i].astype(jnp.int32), 0, below)

        def sweep(n, carry):
            blocks([(0, tq, start + n, False)])
            return carry

        lax.fori_loop(0, jnp.maximum(below - 1 - start, 0), sweep, 0)
        diagonal = [(d * tk, tk, below + e, e == d) for d in range(r) for e in range(d + 1)]

        @pl.when(i == 0)
        def _():
            blocks(diagonal)

        @pl.when(i > 0)
        def _():
            blocks([(0, tq, below - 1, False)] + diagonal)
        dq_ref[...] = jnp.where(first, dq_acc[0], dq_acc[1])

        @pl.when(step == nkb * nq - 1)
        def _():
            scatter.wait()

    shp = jax.ShapeDtypeStruct((s, SB_W), F32)
    qspec = pl.BlockSpec((tq, LANES), lambda p, i: (i, p))
    whole = pl.BlockSpec((None, s // tk, LANES, tk), lambda p, i: (p, 0, 0, 0))
    shp_t = jax.ShapeDtypeStruct((nkb, s // tk, LANES, tk), F32)
    hbm = pl.BlockSpec(memory_space=pl.ANY)
    return pl.pallas_call(
        body, name=name, grid=(nkb, nq),
        out_shape=[shp, shp_t, shp_t] + [jax.ShapeDtypeStruct(p.shape, p.dtype) for p in slabs],
        in_specs=[qspec,
                  pl.BlockSpec((s, LANES), lambda p, i: (0, nkb + p)),
                  pl.BlockSpec((s, LANES), lambda p, i: (0, 2 * nkb + p)),
                  qspec, qspec, pl.BlockSpec(memory_space=pltpu.SMEM)] + [hbm] * n_sl,
        out_specs=[qspec, whole, whole] + [hbm] * n_sl,
        scratch_shapes=[pltpu.VMEM((2, tq, LANES), F32), pltpu.VMEM((2, tq, LANES), F32), pltpu.VMEM((2, tq, LANES), F32)]
        + _exchange_sems(n_sl),
        compiler_params=_cparams())(proj, proj, proj, sp_total, dy, sweep_start, *slabs)


def _swa_backward(proj, y_sw, dy, sinks, gives, name):
    s = proj.shape[0]
    nb = s // WINDOW
    qb, kb, vb = 3 * SB_W // SWA_QW, (3 * SB_W + SWA_QW) // LANES, (3 * SB_W + SWA_QW + SWA_KW) // LANES

    n_gv = len(gives)

    def body(q_ref, kp_ref, kc_ref, vp_ref, vc_ref, o_ref, do_ref, sink_ref, *refs):
        give_refs, (dq_ref, dk_ref, dv_ref, ds_ref), got_refs = refs[:n_gv], refs[n_gv:n_gv + 4], refs[n_gv + 4:2 * n_gv + 4]
        n = pl.program_id(0)
        swap = _sibling_halves(give_refs, got_refs, *refs[2 * n_gv + 4:])

        @pl.when(n == 0)
        def _():
            for cp in swap:
                cp.start()

        @pl.when(n == 0)
        def _():
            dk_ref[...] = jnp.zeros_like(dk_ref)
            dv_ref[...] = jnp.zeros_like(dv_ref)
            ds_ref[...] = jnp.zeros_like(ds_ref)

        k = jnp.concatenate([kp_ref[...], kc_ref[...]], axis=0)
        v = jnp.concatenate([vp_ref[...], vc_ref[...]], axis=0)
        k_sw = pltpu.roll(k.astype(F32), HEAD_DIM, 1).astype(BF16)
        v_sw = pltpu.roll(v.astype(F32), HEAD_DIM, 1).astype(BF16)
        lane = lax.broadcasted_iota(jnp.int32, (1, LANES), 1)
        halves = [lane < HEAD_DIM, lane >= HEAD_DIM]
        valid, distf = _swa_masks(n)
        heads = range(2 * 4)
        cols = [slice((h // 2) * LANES, (h // 2 + 1) * LANES) for h in heads]
        qms = [jnp.where(halves[h % 2], q_ref[:, cols[h]], jnp.zeros((), BF16)) for h in heads]
        dos = [jnp.where(halves[h % 2], do_ref[:, cols[h]], 0.0) for h in heads]
        dobs = [d.astype(BF16) for d in dos]
        native = [h // 4 == h % 2 for h in heads]
        kus = [k if native[h] else k_sw for h in heads]
        vus = [v if native[h] else v_sw for h in heads]
        scores = [_dot_nt(qms[h], kus[h]) for h in heads]
        dps = [_dot_nt(dobs[h], vus[h]) for h in heads]
        deltas = [jnp.sum(dos[h] * o_ref[:, cols[h]], axis=1, keepdims=True) for h in heads]
        probs = [_swa_probs(scores[h], valid, distf, h, sink_ref[h]) for h in heads]
        pbs = [probs[h][0].astype(BF16) for h in heads]
        dscs = [(probs[h][0] * (dps[h] - deltas[h])).astype(BF16) for h in heads]
        dqs = [_dot(dscs[h], kus[h]) for h in heads]
        dks = [_dot_tn(dscs[h], qms[h]) for h in heads]
        dvs = [_dot_tn(pbs[h], dobs[h]) for h in heads]
        for h in heads:
            ds_ref[h:h + 1, :] += jnp.zeros((1, LANES), F32) - jnp.sum(probs[h][1] * deltas[h])
        for pair in range(4):
            dq_ref[:, cols[2 * pair]] = jnp.where(halves[0], dqs[2 * pair], dqs[2 * pair + 1])

        def gathered(parts):
            nat = sum(parts[h] for h in heads if native[h])
            rot = sum(parts[h] for h in heads if not native[h])
            return nat + pltpu.roll(rot, HEAD_DIM, 1)

        dk, dv = gathered(dks), gathered(dvs)
        prev = pl.multiple_of(jnp.maximum(n - 1, 0) * WINDOW, WINDOW)
        cur = pl.multiple_of(n * WINDOW, WINDOW)
        dk_ref[pl.ds(prev, WINDOW), :] += dk[:WINDOW]
        dv_ref[pl.ds(prev, WINDOW), :] += dv[:WINDOW]
        dk_ref[pl.ds(cur, WINDOW), :] += dk[WINDOW:]
        dv_ref[pl.ds(cur, WINDOW), :] += dv[WINDOW:]

        @pl.when(n == nb - 1)
        def _():
            for cp in swap:
                cp.wait()

    prev_blk = lambda n: jnp.maximum(n - 1, 0)
    wide = pl.BlockSpec((WINDOW, SWA_QW), lambda n: (n, 0))
    whole = pl.BlockSpec((s, LANES), lambda n: (0, 0))
    hbm = pl.BlockSpec(memory_space=pl.ANY)
    return pl.pallas_call(
        body, name=name, grid=(nb,),
        out_shape=[jax.ShapeDtypeStruct((s, SWA_QW), F32), jax.ShapeDtypeStruct((s, LANES), F32),
                   jax.ShapeDtypeStruct((s, LANES), F32), jax.ShapeDtypeStruct((8, LANES), F32)] + _halves_shapes(gives),
        in_specs=[pl.BlockSpec((WINDOW, SWA_QW), lambda n: (n, qb)),
                  pl.BlockSpec((WINDOW, LANES), lambda n: (prev_blk(n), kb)),
                  pl.BlockSpec((WINDOW, LANES), lambda n: (n, kb)),
                  pl.BlockSpec((WINDOW, LANES), lambda n: (prev_blk(n), vb)),
                  pl.BlockSpec((WINDOW, LANES), lambda n: (n, vb)),
                  wide,
                  pl.BlockSpec((WINDOW, SWA_QW), lambda n: (n, 1)),
                  pl.BlockSpec(memory_space=pltpu.SMEM)] + [hbm] * n_gv,
        out_specs=[wide, whole, whole, pl.BlockSpec((8, LANES), lambda n: (0, 0))] + [hbm] * n_gv,
        scratch_shapes=_halves_sems(n_gv),
        compiler_params=_cparams())(proj, proj, proj, proj, proj, y_sw, dy, sinks, *gives)


def _in_proj_backward(dq_sb, dkt_sb, dvt_sb, dq_sw, dk_sw, dv_sw, du1, x, vec, w_in, name):
    s = x.shape[0]
    tb = min(TOK_TILE, s)
    n_pairs, _, _, tk = dkt_sb.shape

    def body(dqsb_ref, dktsb_ref, dvtsb_ref, dqsw_ref, dksw_ref, dvsw_ref, du1_ref, x_ref, vec_ref, w_ref,
             dproj_ref, gx_ref, acc_ref, bacc_ref):
        @pl.when(pl.program_id(0) == 0)
        def _():
            acc_ref[...] = jnp.zeros_like(acc_ref)
            bacc_ref[...] = jnp.zeros_like(bacc_ref)

        pieces = ((0, dqsb_ref, QK_SCALE), (3 * SB_W, dqsw_ref, QK_SCALE), (3 * SB_W + SWA_QW, dksw_ref, 1.0),
                  (3 * SB_W + SWA_QW + SWA_KW, dvsw_ref, 1.0))
        for lo, ref, scale in pieces:
            width = ref.shape[1]
            piece = ref[...] * scale
            bacc_ref[0:1, lo:lo + width] += _colsum(piece)
            dproj_ref[:, lo:lo + width] = piece.astype(BF16)
        for base, ref in ((SB_W, dktsb_ref), (2 * SB_W, dvtsb_ref)):
            for p in range(n_pairs):
                lo = base + p * LANES
                for jj in range(tb // tk):
                    piece = ref[p, jj].T
                    bacc_ref[0:1, lo:lo + LANES] += _colsum(piece)
                    dproj_ref[jj * tk:(jj + 1) * tk, lo:lo + LANES] = piece.astype(BF16)
        dh = _dot_nt(dproj_ref[...], w_ref[...])
        xv = x_ref[...]
        gx_ref[...] = ALPHA * du1_ref[...] + dh * (1.0 + vec_ref[V_SC_A:V_SC_A + 1, :])
        acc_ref[C_SCA:C_SCA + 1, :] += _colsum(dh * xv)
        acc_ref[C_SHA:C_SHA + 1, :] += _colsum(dh)

    half = pl.BlockSpec((tb, SB_W), lambda i: (i, 0))
    narrow = pl.BlockSpec((tb, LANES), lambda i: (i, 0))
    full = pl.BlockSpec((tb, D), lambda i: (i, 0))
    blocks_t = pl.BlockSpec((n_pairs, tb // tk, LANES, tk), lambda i: (0, i, 0, 0))
    return pl.pallas_call(
        body, name=name, grid=(s // tb,),
        out_shape=[jax.ShapeDtypeStruct((s, D_IN), BF16), jax.ShapeDtypeStruct((s, D), F32),
                   jax.ShapeDtypeStruct((8, D), F32), jax.ShapeDtypeStruct((8, D_IN), F32)],
        in_specs=[half, blocks_t, blocks_t, half, narrow, narrow, full, full, _resident((VEC_ROWS, D)),
                  _resident((D, D_IN))],
        out_specs=[pl.BlockSpec((tb, D_IN), lambda i: (i, 0)), full, pl.BlockSpec((8, D), lambda i: (0, 0)),
                   pl.BlockSpec((8, D_IN), lambda i: (0, 0))],
        compiler_params=_cparams())(dq_sb, dkt_sb, dvt_sb, dq_sw, dk_sw, dv_sw, du1, x, vec, w_in)


def _weight_grad(at, b, name, col_shards=1):
    m, s = at.shape
    n = b.shape[1]
    if col_shards > 1:
        tn = n // col_shards
        out_shape = jax.ShapeDtypeStruct((col_shards, m, tn), F32)
        out_spec = pl.BlockSpec((None, m, tn), lambda j, k: (j, 0, 0))
    else:
        tn = 512 if n % 512 == 0 else n
        out_shape = jax.ShapeDtypeStruct((m, n), F32)
        out_spec = pl.BlockSpec((m, tn), lambda j, k: (0, j))
    ts = min(WGRAD_TOKENS, s)
    while 2 * (m * ts * 2 + ts * tn * 2 + m * tn * 4) > WGRAD_VMEM and ts > 512:
        ts //= 2

    def body(at_ref, b_ref, o_ref):
        @pl.when(pl.program_id(1) == 0)
        def _():
            o_ref[...] = jnp.zeros_like(o_ref)

        o_ref[...] += _dot(at_ref[...], b_ref[...])

    return pl.pallas_call(
        body, name=name, grid=(n // tn, s // ts),
        out_shape=out_shape,
        in_specs=[pl.BlockSpec((m, ts), lambda j, k: (0, k)), pl.BlockSpec((ts, tn), lambda j, k: (k, j))],
        out_specs=out_spec,
        compiler_params=_cparams())(at, b)


def _pad_rows(v, rows):
    return jnp.concatenate([v, jnp.zeros((rows - v.shape[0], v.shape[1]), v.dtype)], axis=0)


def _col_shards(w, n_shards):
    r, n = w.shape
    return w.reshape(r, n_shards, n // n_shards).transpose(1, 0, 2)


def kernel(x, c, w_ada, b_ada, w_in, b_in, sinks, gn_sb, gn_swa, w_out, ln1_g, ln1_b, w_gu, w_down, ln2_g, ln2_b, loss_target, m_w_ada, m_b_ada, m_w_in, m_b_in, m_sinks, m_gn_sb, m_gn_swa, m_w_out, m_ln1_g, m_ln1_b, m_w_gu, m_w_down, m_ln2_g, m_ln2_b, v_w_ada, v_b_ada, v_w_in, v_b_in, v_sinks, v_gn_sb, v_gn_swa, v_w_out, v_ln1_g, v_ln1_b, v_w_gu, v_w_down, v_ln2_g, v_ln2_b):
    ix, iy, ic = lax.axis_index("x"), lax.axis_index("y"), lax.axis_index("c")
    chip = 2 * ix + iy
    dev = 4 * ix + 2 * iy + ic
    xs, target = x[0], loss_target[0]
    s = xs.shape[0]

    c_rows, g_in = _allgather8(_pad_rows(c, 8), "gather_c", gather=[w_in[0].astype(BF16)])
    c_all = c_rows[::8]
    n_ada = w_ada.shape[2]
    b_ada_shard = lax.dynamic_slice_in_dim(b_ada, chip * n_ada, n_ada, axis=1)
    mod_cols, silu_c = _mod_shard(c_all, w_ada[0], b_ada_shard, "mod_shard")
    mod_all = _allgather8(mod_cols, "gather_mod")[0].reshape(N_DEV, 8, n_ada)
    mod_mine = lax.dynamic_index_in_dim(mod_all, dev, axis=1, keepdims=False)
    mod = mod_mine.reshape(N_CHIPS, 2, n_ada)[:, 0].reshape(6, D)
    vec = jnp.concatenate([mod, ln1_g, ln1_b, ln2_g, ln2_b, jnp.concatenate([gn_sb, gn_swa], axis=1),
                           jnp.zeros((VEC_ROWS - 11, D), F32)], axis=0)

    w_in_b = g_in.transpose(1, 0, 2).reshape(D, D_IN)

    h_t, proj = _in_proj(xs, vec, w_in_b, b_in, "in_proj")
    y_sb, sp_total, sweep_start, g_out, g_gu = _sb_forward(
        proj, [w_out[0].astype(BF16), w_gu[0].astype(BF16)], "sb_forward")
    w_gu_b = g_gu.transpose(1, 0, 2).reshape(D, 2 * D_FF)
    w_out_b = g_out.reshape(D, D)
    sink_vec = sinks[0]
    y_sw, g_down = _swa_forward(proj, sink_vec, [w_down[0].astype(BF16)], "swa_forward")
    w_down_b = g_down.reshape(D_FF, D)
    mixed_t, attn, x1, h2_b, h2_t = _post_attention(y_sb, y_sw, xs, vec, w_out_b, "post_attention")
    gu, act_t, ffn = _ffn_forward(h2_b, w_gu_b, w_down_b, "ffn_forward")

    def in_halves(shards):
        n_sh, rows, cols = shards.shape
        return shards.reshape(n_sh, 2, rows // 2, cols)

    core = ic.reshape(1).astype(jnp.int32)
    dffn_b, dgu_b, dx1, acc_f = _ffn_backward(x1, ffn, target, gu, vec, w_gu_b, w_down_b, "ffn_backward")
    dw_gu = _weight_grad(h2_t, dgu_b, "grad_w_gu", col_shards=4)
    dw_down = _weight_grad(act_t, dffn_b, "grad_w_down")
    du1, dattn_b, dy, acc_a = _attn_out_backward(dx1, xs, attn, y_sb, y_sw, vec, w_out_b, "attn_out_backward")
    dw_out = _weight_grad(mixed_t, dattn_b, "grad_w_out")
    first = [in_halves(dw_gu), in_halves(dw_down.reshape(4, D_FF // 4, D)), in_halves(dw_out.reshape(4, D // 4, D))]
    dq_sw, dk_sw, dv_sw, dsink, *got_first = _swa_backward(proj, y_sw, dy, sink_vec, first, "swa_backward")
    sums_first = _chip_sums(first, got_first, core, "grad_chip_sums")
    dq_sb, dk_sb, dv_sb, *parts_first = _sb_backward(proj, sp_total, sweep_start, dy, sums_first, "sb_backward")
    dproj_b, grad_x, acc_i, acc_b = _in_proj_backward(dq_sb, dk_sb, dv_sb, dq_sw, dk_sw, dv_sw, du1, xs, vec, w_in_b,
                                                      "in_proj_backward")
    dw_in = _weight_grad(h_t, dproj_b, "grad_w_in")
    last = [in_halves(_col_shards(dw_in, 4))]
    sums_last = _chip_sums(last, _halves_swap(last, "grad_halves_swap_in"), core, "grad_chip_sum_in")

    dmod = jnp.concatenate([acc_i[C_SHA:C_SHA + 1], acc_i[C_SCA:C_SCA + 1], acc_a[B_GA:B_GA + 1],
                            acc_f[A_SHF:A_SHF + 1], acc_f[A_SCF:A_SCF + 1], acc_f[A_GF:A_GF + 1]], axis=1)
    dsink_row = jnp.concatenate([dsink[:, 0].reshape(1, 8), jnp.zeros((1, LANES - 8), F32)], axis=1)
    loss_row = jnp.concatenate([jnp.sum(acc_f[A_LOSS:A_LOSS + 1], axis=1, keepdims=True),
                                jnp.zeros((1, LANES - 1), F32)], axis=1)
    small = jnp.concatenate([dmod, acc_b[0:1], acc_a[B_LN1G:B_LN1G + 1], acc_a[B_LN1B:B_LN1B + 1],
                             acc_f[A_LN2G:A_LN2G + 1], acc_f[A_LN2B:A_LN2B + 1], acc_a[B_GN:B_GN + 1],
                             dsink_row, loss_row], axis=1)
    small_rows, *parts_last = _allgather8(_pad_rows(small, 8), "gather_small", scatter=sums_last)
    small_all = small_rows[::8]

    mine = _sum4s([*parts_first, *parts_last], "grad_reduce")
    theirs = _sibling_send(mine, "grad_half_return")
    gw_gu, gw_down, gw_out, gw_in = [
        jnp.concatenate([jnp.where(ic == 0, m_, t_), jnp.where(ic == 0, t_, m_)], axis=0) for m_, t_ in zip(mine, theirs)]

    small_names = ["b_ada", "b_in", "ln1_g", "ln1_b", "ln2_g", "ln2_b", "gn_sb", "gn_swa", "sinks"]
    small_at = [SM_MOD, SM_BIN, SM_LN1G, SM_LN1B, SM_LN2G, SM_LN2B, SM_GN, SM_GN + SB_W, SM_SINK]
    *small_out, loss_row_all = _small_update(
        small_all, small_at,
        [b_ada, b_in, ln1_g, ln1_b, ln2_g, ln2_b, gn_sb, gn_swa, sinks],
        [m_b_ada, m_b_in, m_ln1_g, m_ln1_b, m_ln2_g, m_ln2_b, m_gn_sb, m_gn_swa, m_sinks],
        [v_b_ada, v_b_in, v_ln1_g, v_ln1_b, v_ln2_g, v_ln2_b, v_gn_sb, v_gn_swa, v_sinks], SM_LOSS, "small_update")
    g_small, d_small, m2_small, v2_small = [dict(zip(small_names, leaves)) for leaves in small_out]
    loss = loss_row_all[0, 0]

    dmod_cols = lax.dynamic_slice_in_dim(small_all[:, SM_MOD:SM_BIN], chip * n_ada, n_ada, axis=1)
    gw_ada = _weight_grad(_pad_rows(silu_c, LANES).astype(BF16).T, _pad_rows(dmod_cols, LANES).astype(BF16), "grad_w_ada")

    big = {}
    for nm, w, g, m, v in (("w_ada", w_ada, gw_ada, m_w_ada, v_w_ada), ("w_in", w_in, gw_in, m_w_in, v_w_in),
                           ("w_out", w_out, gw_out, m_w_out, v_w_out), ("w_gu", w_gu, gw_gu, m_w_gu, v_w_gu),
                           ("w_down", w_down, gw_down, m_w_down, v_w_down)):
        d_, m2_, v2_ = _adamw(w[0], g, m[0], v[0], "adamw_" + nm)
        big[nm] = (g[None], d_[None], m2_[None], v2_[None])

    order = ["w_ada", "b_ada", "w_in", "b_in", "sinks", "gn_sb", "gn_swa", "w_out", "ln1_g", "ln1_b", "w_gu", "w_down",
             "ln2_g", "ln2_b"]

    def leaf(nm, which):
        if nm in big:
            return big[nm][which]
        return (g_small, d_small, m2_small, v2_small)[which][nm]

    outs = [loss, grad_x[None]]
    for which in range(4):
        outs += [leaf(nm, which) for nm in order]
    return tuple(outs)
```

```python
import math

import jax
import jax.numpy as jnp
from jax import lax
from jax.experimental import pallas as pl
from jax.experimental.pallas import tpu as pltpu

F32 = jnp.float32
BF16 = jnp.bfloat16

D = 1024
HEAD_DIM = 64
SB_W = 512
SWA_QW = 512
SWA_KW = 128
D_IN = 2304
D_FF = 2816
WINDOW = 128
ALPHA = 2.0 ** 0.25
LN_EPS = 1e-5
RMS_EPS = 1e-6
MASK_VALUE = -1e30
QK_SCALE = 1.0 / math.sqrt(HEAD_DIM)

ADAM_LR = 0.001
ADAM_B1 = 0.9
ADAM_B2 = 0.999
ADAM_EPS = 1e-08
ADAM_WD = 0.01
ADAM_STEP = 10

N_CHIPS = 4
N_DEV = 8
LANES = 128

SB_TQ = 512
SB_TK = 256
SB_DEAD_MASS = 110.0
TOK_TILE = 512
FFN_TILE = 256
FFN_BWD_TILE = 256
VMEM_LIMIT = 56 * 1024 * 1024
WGRAD_TOKENS = 2048
WGRAD_VMEM = 40 * 1024 * 1024

V_SH_A, V_SC_A, V_G_A, V_SH_F, V_SC_F, V_G_F, V_LN1G, V_LN1B, V_LN2G, V_LN2B, V_GN = range(11)
VEC_ROWS = 16

SM_MOD = 0
SM_BIN = 6 * D
SM_LN1G = SM_BIN + D_IN
SM_LN1B = SM_LN1G + D
SM_LN2G = SM_LN1B + D
SM_LN2B = SM_LN2G + D
SM_GN = SM_LN2B + D
SM_SINK = SM_GN + D
SM_LOSS = SM_SINK + LANES
SM_LEN = SM_LOSS + LANES

MESH = pl.DeviceIdType.MESH


def _cparams(**kw):
    return pltpu.CompilerParams(vmem_limit_bytes=VMEM_LIMIT, **kw)


def _resident(shape):
    nd = len(shape)
    return pl.BlockSpec(shape, lambda *_: (0,) * nd, pipeline_mode=pl.Buffered(1))


def _dot(a, b):
    return jnp.dot(a, b, preferred_element_type=F32)


def _dot_nt(a, b):
    return lax.dot_general(a, b, (((1,), (1,)), ((), ())), preferred_element_type=F32)


def _dot_tn(a, b):
    return lax.dot_general(a, b, (((0,), (0,)), ((), ())), preferred_element_type=F32)


def _sum_matrix(tk, keep):
    row = lax.broadcasted_iota(jnp.int32, (tk, tk + LANES), 0)
    col = lax.broadcasted_iota(jnp.int32, (tk, tk + LANES), 1)
    return (keep(row, col) | (col >= tk)).astype(BF16)


def _block_sums(x, m):
    tk = x.shape[1]
    res = _dot(x.astype(BF16), m)
    return res[:, :tk], res[:, tk:]


def _before(t0, n, s0, tk):
    return s0 + lax.broadcasted_iota(jnp.int32, (n, tk), 1) < t0 + lax.broadcasted_iota(jnp.int32, (n, tk), 0)


class _RowState:
    def __init__(self, ref, tk, add=False):
        self.ref, self.tk, self.add, self.vals = ref, tk, add, {}

    def _blocks(self, r0, n):
        return range(r0 // self.tk, (r0 + n) // self.tk)

    def get(self, hd, r0, n):
        for d in self._blocks(r0, n):
            if (hd, d) not in self.vals:
                self.vals[(hd, d)] = self.ref[hd, d * self.tk:(d + 1) * self.tk, :]
        parts = [self.vals[(hd, d)] for d in self._blocks(r0, n)]
        return parts[0] if len(parts) == 1 else jnp.concatenate(parts, axis=0)

    def put(self, hd, r0, n, val):
        for k, d in enumerate(self._blocks(r0, n)):
            self.vals[(hd, d)] = val[k * self.tk:(k + 1) * self.tk]

    def accumulate(self, hd, r0, n, val):
        for k, d in enumerate(self._blocks(r0, n)):
            part = val[k * self.tk:(k + 1) * self.tk]
            self.vals[(hd, d)] = part if (hd, d) not in self.vals else self.vals[(hd, d)] + part

    def store(self):
        for (hd, d), val in self.vals.items():
            span = slice(d * self.tk, (d + 1) * self.tk)
            if self.add:
                self.ref[hd, span, :] += val
            else:
                self.ref[hd, span, :] = val


def _across(v, tk):
    return jnp.concatenate([v] * (tk // LANES), axis=1)


def _allgather8(v, name, gather=(), scatter=()):
    m_per, n = v.shape
    n_g, n_s = len(gather), len(scatter)

    def body(x_ref, *refs):
        g_in, s_in = refs[:n_g], refs[n_g:n_g + n_s]
        out_ref = refs[n_g + n_s]
        g_out, s_out = refs[n_g + n_s + 1:2 * n_g + n_s + 1], refs[2 * n_g + n_s + 1:2 * (n_g + n_s) + 1]
        send_sems, recv_sems, local_sem, *more_sems = refs[2 * (n_g + n_s) + 1:]
        beside = ([_gather_exchange(g_in, g_out, *more_sems[:3])] if n_g else []) + (
            [_scatter_exchange(s_in, s_out, *more_sems[-3:])] if n_s else [])
        for ex in beside:
            ex.start()
        x, y, c = lax.axis_index("x"), lax.axis_index("y"), lax.axis_index("c")
        me, sibling = (x, y, c), (x, y, 1 - c)
        chips = [(1 - x, y), (x, 1 - y), (1 - x, 1 - y)]

        def rows(px, py, pc):
            return out_ref.at[pl.ds((4 * px + 2 * py + pc) * m_per, m_per), :]

        def copy(k, block, to, src=None):
            return pltpu.make_async_remote_copy(
                src_ref=rows(*block) if src is None else src, dst_ref=rows(*block),
                send_sem=send_sems.at[k], recv_sem=recv_sems.at[k], device_id=to, device_id_type=MESH)

        mine = pltpu.make_async_copy(x_ref, rows(*me), local_sem)
        mine.start()
        first = [copy(0, me, sibling, src=x_ref)]
        first += [copy(1 + j, me, (*chip, c), src=x_ref) for j, chip in enumerate(chips)]
        for cp in first:
            cp.start()
        passed = [copy(4 + j, (*chip, c), sibling) for j, chip in enumerate(chips)]
        for j, chip in enumerate(chips):
            copy(1 + j, (*chip, c), me).wait_recv()
            passed[j].start()
        copy(0, sibling, me).wait_recv()
        for j, chip in enumerate(chips):
            copy(4 + j, (*chip, 1 - c), me).wait_recv()
        for cp in first + passed:
            cp.wait_send()
        mine.wait()
        for ex in beside:
            ex.wait()

    hbm = pl.BlockSpec(memory_space=pl.ANY)
    return pl.pallas_call(
        body, name=name,
        out_shape=[jax.ShapeDtypeStruct((N_DEV * m_per, n), v.dtype)]
        + [jax.ShapeDtypeStruct((N_CHIPS,) + a.shape, a.dtype) for a in gather]
        + [jax.ShapeDtypeStruct(p.shape, p.dtype) for p in scatter],
        in_specs=[pl.BlockSpec(memory_space=pltpu.VMEM)] + [hbm] * (n_g + n_s),
        out_specs=[pl.BlockSpec(memory_space=pltpu.VMEM)] + [hbm] * (n_g + n_s),
        scratch_shapes=[pltpu.SemaphoreType.DMA((7,)), pltpu.SemaphoreType.DMA((7,)), pltpu.SemaphoreType.DMA]
        + (_exchange_sems(n_g) if n_g else []) + (_exchange_sems(n_s) if n_s else []),
        compiler_params=_cparams(),
    )(v, *gather, *scatter)


class _Exchange:
    def __init__(self, local, sends, arrivals):
        self.local, self.sends, self.arrivals = local, sends, arrivals

    def start(self):
        for cp in self.local + self.sends:
            cp.start()

    def wait(self):
        for cp in self.arrivals:
            cp.wait_recv()
        for cp in self.sends:
            cp.wait_send()
        for cp in self.local:
            cp.wait()


def _exchange_sems(n):
    return [pltpu.SemaphoreType.DMA((3 * n,)), pltpu.SemaphoreType.DMA((3 * n,)), pltpu.SemaphoreType.DMA((n,))]


def _gather_exchange(ins, outs, send_sems, recv_sems, local_sems):
    x, y, c = lax.axis_index("x"), lax.axis_index("y"), lax.axis_index("c")
    slot = 2 * x + y
    chips = [(1 - x, y), (x, 1 - y), (1 - x, 1 - y)]
    local, sends, arrivals = [], [], []
    for a in range(len(ins)):
        local.append(pltpu.make_async_copy(ins[a], outs[a].at[slot], local_sems.at[a]))
        for j, (px, py) in enumerate(chips):
            sems = dict(send_sem=send_sems.at[3 * a + j], recv_sem=recv_sems.at[3 * a + j],
                        device_id=(px, py, c), device_id_type=MESH)
            sends.append(pltpu.make_async_remote_copy(src_ref=ins[a], dst_ref=outs[a].at[slot], **sems))
            arrivals.append(pltpu.make_async_remote_copy(src_ref=ins[a], dst_ref=outs[a].at[2 * px + py], **sems))
    return _Exchange(local, sends, arrivals)


def _scatter_exchange(p_refs, out_refs, send_sems, recv_sems, local_sems):
    x, y, c = lax.axis_index("x"), lax.axis_index("y"), lax.axis_index("c")
    slot = 2 * x + y
    chips = [(1 - x, y), (x, 1 - y), (1 - x, 1 - y)]
    local, sends, arrivals = [], [], []
    for a, (p_ref, out_ref) in enumerate(zip(p_refs, out_refs)):
        local.append(pltpu.make_async_copy(p_ref.at[slot], out_ref.at[slot], local_sems.at[a]))
        for j, (px, py) in enumerate(chips):
            sems = dict(send_sem=send_sems.at[3 * a + j], recv_sem=recv_sems.at[3 * a + j],
                        device_id=(px, py, c), device_id_type=MESH)
            sends.append(pltpu.make_async_remote_copy(src_ref=p_ref.at[2 * px + py], dst_ref=out_ref.at[slot], **sems))
            arrivals.append(pltpu.make_async_remote_copy(src_ref=p_ref.at[slot], dst_ref=out_ref.at[2 * px + py], **sems))
    return _Exchange(local, sends, arrivals)


def _sibling_halves(give_refs, got_refs, send_sems, recv_sems):
    x, y, c = lax.axis_index("x"), lax.axis_index("y"), lax.axis_index("c")
    copies = []
    for a, (give_ref, got_ref) in enumerate(zip(give_refs, got_refs)):
        for s in range(N_CHIPS):
            copies.append(pltpu.make_async_remote_copy(
                src_ref=give_ref.at[s, 1 - c], dst_ref=got_ref.at[s], send_sem=send_sems.at[N_CHIPS * a + s],
                recv_sem=recv_sems.at[N_CHIPS * a + s], device_id=(x, y, 1 - c), device_id_type=MESH))
    return copies


def _halves_shapes(arrs):
    return [jax.ShapeDtypeStruct((a.shape[0],) + a.shape[2:], a.dtype) for a in arrs]


def _halves_sems(n):
    return [pltpu.SemaphoreType.DMA((N_CHIPS * n,)), pltpu.SemaphoreType.DMA((N_CHIPS * n,))]


def _halves_swap(arrs, name):
    n = len(arrs)

    def body(*refs):
        copies = _sibling_halves(refs[:n], refs[n:2 * n], *refs[2 * n:])
        for cp in copies:
            cp.start()
        for cp in copies:
            cp.wait()

    hbm = pl.BlockSpec(memory_space=pl.ANY)
    return pl.pallas_call(body, name=name, out_shape=_halves_shapes(arrs), in_specs=[hbm] * n, out_specs=[hbm] * n,
                          scratch_shapes=_halves_sems(n), compiler_params=_cparams())(*arrs)


def _sibling_send(arrs, name):
    n = len(arrs)

    def body(*refs):
        x, y, c = lax.axis_index("x"), lax.axis_index("y"), lax.axis_index("c")
        send_sems, recv_sems = refs[2 * n:]
        copies = [pltpu.make_async_remote_copy(src_ref=refs[a], dst_ref=refs[n + a], send_sem=send_sems.at[a],
                                               recv_sem=recv_sems.at[a], device_id=(x, y, 1 - c), device_id_type=MESH)
                  for a in range(n)]
        for cp in copies:
            cp.start()
        for cp in copies:
            cp.wait()

    hbm = pl.BlockSpec(memory_space=pl.ANY)
    return pl.pallas_call(
        body, name=name, out_shape=[jax.ShapeDtypeStruct(a.shape, a.dtype) for a in arrs],
        in_specs=[hbm] * n, out_specs=[hbm] * n,
        scratch_shapes=[pltpu.SemaphoreType.DMA((n,)), pltpu.SemaphoreType.DMA((n,))],
        compiler_params=_cparams(),
    )(*arrs)


def _row_tile(h):
    return h // 2 if (h // 2) % 8 == 0 else h


def _row_tiles(hs):
    tiles = [_row_tile(h) for h in hs]
    assert len({h // t for h, t in zip(hs, tiles)}) == 1
    return tiles, hs[0] // tiles[0]


def _chip_sums(arrs, gots, core, name):
    n = len(arrs)
    tiles, steps = _row_tiles([a.shape[2] for a in arrs])

    def body(core_ref, *refs):
        for a_ref, b_ref, o_ref in zip(refs[:n], refs[n:2 * n], refs[2 * n:]):
            o_ref[...] = a_ref[...] + b_ref[...]

    slabs = [pl.BlockSpec((None, tr, a.shape[3]), lambda s, i, core_ref: (s, i, 0)) for a, tr in zip(arrs, tiles)]
    grid_spec = pltpu.PrefetchScalarGridSpec(
        num_scalar_prefetch=1, grid=(N_CHIPS, steps),
        in_specs=[pl.BlockSpec((None, None, tr, a.shape[3]), lambda s, i, core_ref: (s, core_ref[0], i, 0))
                  for a, tr in zip(arrs, tiles)] + slabs,
        out_specs=slabs)
    return pl.pallas_call(body, name=name, grid_spec=grid_spec,
                          out_shape=[jax.ShapeDtypeStruct(g.shape, g.dtype) for g in gots],
                          compiler_params=_cparams())(core, *arrs, *gots)


def _sum4s(ps, name):
    tiles, steps = _row_tiles([p.shape[1] for p in ps])

    def body(*refs):
        for p_ref, o_ref in zip(refs[:len(ps)], refs[len(ps):]):
            o_ref[...] = ((p_ref[0] + p_ref[1]) + p_ref[2]) + p_ref[3]

    return pl.pallas_call(
        body, name=name, grid=(steps,), out_shape=[jax.ShapeDtypeStruct(p.shape[1:], p.dtype) for p in ps],
        in_specs=[pl.BlockSpec((4, tr, p.shape[2]), lambda i: (0, i, 0)) for p, tr in zip(ps, tiles)],
        out_specs=[pl.BlockSpec((tr, p.shape[2]), lambda i: (i, 0)) for p, tr in zip(ps, tiles)],
        compiler_params=_cparams())(*ps)


def _adam_math(w, g, m, v):
    m2 = ADAM_B1 * m + (1.0 - ADAM_B1) * g
    v2 = ADAM_B2 * v + (1.0 - ADAM_B2) * (g * g)
    m_hat = m2 / (1.0 - ADAM_B1 ** ADAM_STEP)
    v_hat = v2 / (1.0 - ADAM_B2 ** ADAM_STEP)
    delta = -ADAM_LR * (m_hat / (jnp.sqrt(v_hat) + ADAM_EPS) + ADAM_WD * w)
    return delta, m2, v2


def _adamw(w, g, m, v, name):
    rows, cols = w.shape
    tr = rows // 4 if rows % 32 == 0 else rows

    def body(w_ref, g_ref, m_ref, v_ref, d_ref, m2_ref, v2_ref):
        delta, m2, v2 = _adam_math(w_ref[...], g_ref[...], m_ref[...], v_ref[...])
        d_ref[...] = delta
        m2_ref[...] = m2
        v2_ref[...] = v2

    spec = pl.BlockSpec((tr, cols), lambda i: (i, 0))
    shp = jax.ShapeDtypeStruct(w.shape, F32)
    return pl.pallas_call(body, name=name, grid=(rows // tr,), out_shape=[shp, shp, shp],
                          in_specs=[spec] * 4, out_specs=[spec] * 3, compiler_params=_cparams())(w, g, m, v)


def _small_update(g8, offsets, ws, ms, vs, loss_at, name):
    k = len(ws)

    def summed(g8_ref, lo, width):
        g = g8_ref[0:1, lo:lo + width]
        for r in range(1, N_DEV):
            g = g + g8_ref[r:r + 1, lo:lo + width]
        return g

    def body(g8_ref, *refs):
        ins, outs = refs[:3 * k], refs[3 * k:]
        for j in range(k):
            g = summed(g8_ref, offsets[j], ws[j].shape[1])
            delta, m2, v2 = _adam_math(ins[j][...], g, ins[k + j][...], ins[2 * k + j][...])
            for kind, val in enumerate((g, delta, m2, v2)):
                outs[kind * k + j][...] = val
        outs[4 * k][...] = summed(g8_ref, loss_at, LANES)

    vm = pl.BlockSpec(memory_space=pltpu.VMEM)
    shapes = [jax.ShapeDtypeStruct(w.shape, F32) for w in ws] * 4 + [jax.ShapeDtypeStruct((1, LANES), F32)]
    res = pl.pallas_call(body, name=name, out_shape=shapes, in_specs=[vm] * (1 + 3 * k), out_specs=[vm] * (4 * k + 1),
                         compiler_params=_cparams())(g8, *ws, *ms, *vs)
    return res[:k], res[k:2 * k], res[2 * k:3 * k], res[3 * k:4 * k], res[4 * k]


def _mod_shard(c8, w_ada, b_ada_shard, name):
    n = w_ada.shape[1]
    tn = 512

    def body(c_ref, w_ref, b_ref, o_ref, s_ref):
        cv = c_ref[...]
        sc = cv * (1.0 / (1.0 + jnp.exp(-cv)))
        s_ref[...] = sc
        o_ref[...] = _dot(sc.astype(BF16), w_ref[...].astype(BF16)) + b_ref[...]

    return pl.pallas_call(
        body, name=name, grid=(n // tn,),
        out_shape=[jax.ShapeDtypeStruct((8, n), F32), jax.ShapeDtypeStruct((8, D), F32)],
        in_specs=[pl.BlockSpec((8, D), lambda j: (0, 0)), pl.BlockSpec((D, tn), lambda j: (0, j)),
                  pl.BlockSpec((1, tn), lambda j: (0, j))],
        out_specs=[pl.BlockSpec((8, tn), lambda j: (0, j)), pl.BlockSpec((8, D), lambda j: (0, 0))],
        compiler_params=_cparams())(c8, w_ada, b_ada_shard)


def _layer_norm_stats(u):
    mu = jnp.mean(u, axis=1, keepdims=True)
    d = u - mu
    var = jnp.mean(d * d, axis=1, keepdims=True)
    rstd = lax.rsqrt(var + LN_EPS)
    return d * rstd, rstd


def _in_proj(x, vec, w_in, b_in, shards, name):
    s = x.shape[0]
    tb = min(TOK_TILE, s)
    n_sh, steps = len(shards), s // tb

    def body(x_ref, vec_ref, w_ref, b_ref, *refs):
        sh_refs, (ht_ref, p_ref), got_refs = refs[:n_sh], refs[n_sh:n_sh + 2], refs[n_sh + 2:2 * n_sh + 2]
        gather = _gather_exchange(sh_refs, got_refs, *refs[2 * n_sh + 2:])

        @pl.when(pl.program_id(0) == 0)
        def _():
            gather.start()

        h = x_ref[...] * (1.0 + vec_ref[V_SC_A:V_SC_A + 1, :]) + vec_ref[V_SH_A:V_SH_A + 1, :]
        hb = h.astype(BF16)
        ht_ref[...] = h.T.astype(BF16)
        proj = _dot(hb, w_ref[...]) + b_ref[...]
        col = lax.broadcasted_iota(jnp.int32, (1, D_IN), 1)
        is_q = (col < SB_W) | ((col >= 3 * SB_W) & (col < 3 * SB_W + SWA_QW))
        p_ref[...] = (proj * jnp.where(is_q, QK_SCALE, 1.0)).astype(BF16)

        @pl.when(pl.program_id(0) == steps - 1)
        def _():
            gather.wait()

    hbm = pl.BlockSpec(memory_space=pl.ANY)
    return pl.pallas_call(
        body, name=name, grid=(steps,),
        out_shape=[jax.ShapeDtypeStruct((D, s), BF16), jax.ShapeDtypeStruct((s, D_IN), BF16)]
        + [jax.ShapeDtypeStruct((N_CHIPS,) + a.shape, a.dtype) for a in shards],
        in_specs=[pl.BlockSpec((tb, D), lambda i: (i, 0)), _resident((VEC_ROWS, D)), _resident((D, D_IN)),
                  _resident((1, D_IN))] + [hbm] * n_sh,
        out_specs=[pl.BlockSpec((D, tb), lambda i: (0, i)), pl.BlockSpec((tb, D_IN), lambda i: (i, 0))] + [hbm] * n_sh,
        scratch_shapes=_exchange_sems(n_sh),
        compiler_params=_cparams())(x, vec, w_in, b_in, *shards)


def _softplus_parts(z):
    e1 = jnp.exp(-jnp.abs(z))
    sp = jnp.maximum(z, 0.0) + jnp.log(1.0 + e1)
    return sp, e1


def _sb_forward(proj, shards, name):
    s = proj.shape[0]
    tq, tk = min(SB_TQ, s), min(SB_TK, s)
    r = tq // tk

    n_sh = len(shards)
    nkb = SB_W // LANES
    nq = s // tq

    def body(q_ref, k_ref, v_ref, *refs):
        sh_refs, (o_ref, tot_ref, start_ref), got_refs = refs[:n_sh], refs[n_sh:n_sh + 3], refs[n_sh + 3:2 * n_sh + 3]
        acc_refs, run_refs = refs[2 * n_sh + 3:2 * n_sh + 5]
        i = pl.program_id(1)
        step = pl.program_id(0) * nq + i
        gather = _gather_exchange(sh_refs, got_refs, *refs[2 * n_sh + 5:])

        @pl.when(step == 0)
        def _():
            gather.start()

        lane = lax.broadcasted_iota(jnp.int32, (1, LANES), 1)
        first = lane < HEAD_DIM
        qp = q_ref[...]
        zero = jnp.zeros((), BF16)
        qs = (jnp.where(first, qp, zero), jnp.where(first, zero, qp))
        later = _sum_matrix(tk, lambda row, col: row > col)
        acc_refs[...] = jnp.zeros_like(acc_refs)
        run_refs[...] = jnp.zeros_like(run_refs)

        def blocks(tiles):
            rows = [slice(r0, r0 + n) for r0, n, _, _ in tiles]
            kjs = [k_ref[pl.ds(pl.multiple_of(j * tk, tk), tk), :] for _, _, j, _ in tiles]
            vjs = [v_ref[pl.ds(pl.multiple_of(j * tk, tk), tk), :] for _, _, j, _ in tiles]
            chains = [(hd, t) for t in range(len(tiles)) for hd in range(2)]
            zs = [_dot_nt(qs[hd][rows[t]], kjs[t]) for hd, t in chains]
            sps = [_softplus_parts(z)[0] for z in zs]
            befores = [_before(i * tq + r0, n, j * tk, tk) if diag else None for r0, n, j, diag in tiles]
            spms = [sp if befores[t] is None else jnp.where(befores[t], sp, 0.0) for (hd, t), sp in zip(chains, sps)]
            cums = [_block_sums(spm, later) for spm in spms]
            runs, accs, ws = _RowState(run_refs, tk), _RowState(acc_refs, tk, add=True), []
            for (hd, t), z, sp, (cum, sm) in zip(chains, zs, sps, cums):
                r0, n = tiles[t][:2]
                run = runs.get(hd, r0, n)
                w = jnp.exp(z - sp - cum - _across(run, tk))
                if befores[t] is not None:
                    w = jnp.where(befores[t], w, 0.0)
                ws.append(w.astype(BF16))
                runs.put(hd, r0, n, run + sm)
            for (hd, t), pv in zip(chains, [_dot(w, vjs[t]) for (hd, t), w in zip(chains, ws)]):
                accs.accumulate(hd, *tiles[t][:2], pv)
            accs.store()
            runs.store()

        below = i * r
        diagonal = [(d * tk, tk, below + e, e == d) for d in range(r) for e in range(d, -1, -1)]

        @pl.when(i == 0)
        def _():
            blocks(diagonal)

        @pl.when(i > 0)
        def _():
            blocks(diagonal + [(0, tq, below - 1, False)])

        def swept_mass():
            return jnp.min(jnp.minimum(run_refs[0], run_refs[1]))

        def more(carry):
            n, mass = carry
            return (n < below) & (mass < SB_DEAD_MASS)

        def sweep(carry):
            n, _ = carry
            blocks([(0, tq, below - 1 - n, False)])
            return n + 1, swept_mass()

        n_swept, _ = lax.while_loop(more, sweep, (jnp.minimum(below, 1), swept_mass()))
        start_ref[pl.program_id(0), i] = (below - n_swept).astype(F32)
        o_ref[...] = jnp.where(first, acc_refs[0], acc_refs[1])
        tot_ref[...] = jnp.where(first, run_refs[0], run_refs[1])

        @pl.when(step == nkb * nq - 1)
        def _():
            gather.wait()

    shp = jax.ShapeDtypeStruct((s, SB_W), F32)
    qspec = pl.BlockSpec((tq, LANES), lambda p, i: (i, p))
    hbm = pl.BlockSpec(memory_space=pl.ANY)
    return pl.pallas_call(
        body, name=name, grid=(nkb, nq),
        out_shape=[shp, shp, jax.ShapeDtypeStruct((nkb, nq), F32)]
        + [jax.ShapeDtypeStruct((N_CHIPS,) + a.shape, a.dtype) for a in shards],
        in_specs=[qspec,
                  pl.BlockSpec((s, LANES), lambda p, i: (0, nkb + p)),
                  pl.BlockSpec((s, LANES), lambda p, i: (0, 2 * nkb + p))] + [hbm] * n_sh,
        out_specs=[qspec, qspec, pl.BlockSpec(memory_space=pltpu.SMEM)] + [hbm] * n_sh,
        scratch_shapes=[pltpu.VMEM((2, tq, LANES), F32), pltpu.VMEM((2, tq, LANES), F32)] + _exchange_sems(n_sh),
        compiler_params=_cparams())(proj, proj, proj, *shards)


def _swa_masks(n):
    ti = lax.broadcasted_iota(jnp.int32, (WINDOW, 2 * WINDOW), 0)
    kj = lax.broadcasted_iota(jnp.int32, (WINDOW, 2 * WINDOW), 1)
    dist = ti + WINDOW - kj
    valid = (dist >= 0) & (dist < WINDOW) & ((n * WINDOW - WINDOW + kj) >= 0)
    return valid, dist.astype(F32)


def _swa_probs(sc, valid, distf, h, sink):
    slope = 2.0 ** (-(h + 1))
    sc = jnp.where(valid, sc - slope * distf, MASK_VALUE)
    mx = jnp.maximum(jnp.max(sc, axis=1, keepdims=True), sink)
    p = jnp.exp(sc - mx)
    es = jnp.exp(sink - mx)
    inv = 1.0 / (jnp.sum(p, axis=1, keepdims=True) + es)
    return p * inv, es * inv


def _swa_forward(proj, sinks, shards, name):
    s = proj.shape[0]
    nb = s // WINDOW
    qb, kb, vb = 3 * SB_W // SWA_QW, (3 * SB_W + SWA_QW) // LANES, (3 * SB_W + SWA_QW + SWA_KW) // LANES
    n_sh = len(shards)

    def body(q_ref, kp_ref, kc_ref, vp_ref, vc_ref, sink_ref, *refs):
        sh_refs, o_ref, got_refs = refs[:n_sh], refs[n_sh], refs[n_sh + 1:2 * n_sh + 1]
        n = pl.program_id(0)
        gather = _gather_exchange(sh_refs, got_refs, *refs[2 * n_sh + 1:])

        @pl.when(n == 0)
        def _():
            gather.start()

        k = jnp.concatenate([kp_ref[...], kc_ref[...]], axis=0)
        v = jnp.concatenate([vp_ref[...], vc_ref[...]], axis=0)
        k_sw = pltpu.roll(k.astype(F32), HEAD_DIM, 1).astype(BF16)
        v_sw = pltpu.roll(v.astype(F32), HEAD_DIM, 1).astype(BF16)
        lane = lax.broadcasted_iota(jnp.int32, (1, LANES), 1)
        halves = [lane < HEAD_DIM, lane >= HEAD_DIM]
        valid, distf = _swa_masks(n)
        heads = range(2 * 4)
        qms = [jnp.where(halves[h % 2], q_ref[:, (h // 2) * LANES:(h // 2 + 1) * LANES], jnp.zeros((), BF16))
               for h in heads]
        kus = [k if h // 4 == h % 2 else k_sw for h in heads]
        vus = [v if h // 4 == h % 2 else v_sw for h in heads]
        scores = [_dot_nt(qms[h], kus[h]) for h in heads]
        ps = [_swa_probs(scores[h], valid, distf, h, sink_ref[h])[0].astype(BF16) for h in heads]
        outs = [_dot(ps[h], vus[h]) for h in heads]
        for pair in range(4):
            o_ref[:, pair * LANES:(pair + 1) * LANES] = jnp.where(halves[0], outs[2 * pair], outs[2 * pair + 1])

        @pl.when(n == nb - 1)
        def _():
            gather.wait()

    prev = lambda n: jnp.maximum(n - 1, 0)
    hbm = pl.BlockSpec(memory_space=pl.ANY)
    return pl.pallas_call(
        body, name=name, grid=(nb,),
        out_shape=[jax.ShapeDtypeStruct((s, SWA_QW), F32)]
        + [jax.ShapeDtypeStruct((N_CHIPS,) + a.shape, a.dtype) for a in shards],
        in_specs=[pl.BlockSpec((WINDOW, SWA_QW), lambda n: (n, qb)),
                  pl.BlockSpec((WINDOW, LANES), lambda n: (prev(n), kb)),
                  pl.BlockSpec((WINDOW, LANES), lambda n: (n, kb)),
                  pl.BlockSpec((WINDOW, LANES), lambda n: (prev(n), vb)),
                  pl.BlockSpec((WINDOW, LANES), lambda n: (n, vb)),
                  pl.BlockSpec(memory_space=pltpu.SMEM)] + [hbm] * n_sh,
        out_specs=[pl.BlockSpec((WINDOW, SWA_QW), lambda n: (n, 0))] + [hbm] * n_sh,
        scratch_shapes=_exchange_sems(n_sh),
        compiler_params=_cparams())(proj, proj, proj, proj, proj, sinks, *shards)


def _rms_parts(y):
    return lax.rsqrt(jnp.mean(y * y, axis=1, keepdims=True) + RMS_EPS)


def _post_attention(y_sb, y_sw, x, vec, w_out, name):
    s = x.shape[0]
    tb = min(TOK_TILE, s)

    def body(ysb_ref, ysw_ref, x_ref, vec_ref, w_ref, mixedt_ref, attn_ref, x1_ref, h2_ref, h2t_ref):
        ysb, ysw = ysb_ref[...], ysw_ref[...]
        nsb_f = ysb * _rms_parts(ysb) * vec_ref[V_GN:V_GN + 1, :SB_W]
        nsw_f = ysw * _rms_parts(ysw) * vec_ref[V_GN:V_GN + 1, SB_W:]
        nsb, nsw = nsb_f.astype(BF16), nsw_f.astype(BF16)
        mixedt_ref[:SB_W, :] = nsb_f.T.astype(BF16)
        mixedt_ref[SB_W:, :] = nsw_f.T.astype(BF16)
        attn = _dot(nsb, w_ref[:SB_W, :]) + _dot(nsw, w_ref[SB_W:, :])
        attn_ref[...] = attn
        u1 = ALPHA * x_ref[...] + (1.0 + vec_ref[V_G_A:V_G_A + 1, :]) * attn
        xhat, _ = _layer_norm_stats(u1)
        x1 = xhat * vec_ref[V_LN1G:V_LN1G + 1, :] + vec_ref[V_LN1B:V_LN1B + 1, :]
        x1_ref[...] = x1
        h2 = x1 * (1.0 + vec_ref[V_SC_F:V_SC_F + 1, :]) + vec_ref[V_SH_F:V_SH_F + 1, :]
        h2_ref[...] = h2.astype(BF16)
        h2t_ref[...] = h2.T.astype(BF16)

    half = pl.BlockSpec((tb, SB_W), lambda i: (i, 0))
    full = pl.BlockSpec((tb, D), lambda i: (i, 0))
    full_t = pl.BlockSpec((D, tb), lambda i: (0, i))
    return pl.pallas_call(
        body, name=name, grid=(s // tb,),
        out_shape=[jax.ShapeDtypeStruct((D, s), BF16), jax.ShapeDtypeStruct((s, D), F32),
                   jax.ShapeDtypeStruct((s, D), F32), jax.ShapeDtypeStruct((s, D), BF16),
                   jax.ShapeDtypeStruct((D, s), BF16)],
        in_specs=[half, half, full, _resident((VEC_ROWS, D)), _resident((D, D))],
        out_specs=[full_t, full, full, full, full_t],
        compiler_params=_cparams())(y_sb, y_sw, x, vec, w_out)


def _ffn_forward(h2, w_gu, w_down, name):
    s = h2.shape[0]
    tb = min(FFN_TILE, s)

    def body(h_ref, wgu_ref, wd_ref, gu_ref, actt_ref, ffn_ref):
        gu = _dot(h_ref[...], wgu_ref[...])
        gu_ref[...] = gu.astype(BF16)
        gate, up = gu[:, :D_FF], gu[:, D_FF:]
        act = gate * (1.0 / (1.0 + jnp.exp(-gate))) * up
        actt_ref[...] = act.T.astype(BF16)
        ffn_ref[...] = _dot(act.astype(BF16), wd_ref[...])

    return pl.pallas_call(
        body, name=name, grid=(s // tb,),
        out_shape=[jax.ShapeDtypeStruct((s, 2 * D_FF), BF16), jax.ShapeDtypeStruct((D_FF, s), BF16),
                   jax.ShapeDtypeStruct((s, D), F32)],
        in_specs=[pl.BlockSpec((tb, D), lambda i: (i, 0)), _resident((D, 2 * D_FF)), _resident((D_FF, D))],
        out_specs=[pl.BlockSpec((tb, 2 * D_FF), lambda i: (i, 0)), pl.BlockSpec((D_FF, tb), lambda i: (0, i)),
                   pl.BlockSpec((tb, D), lambda i: (i, 0))],
        compiler_params=_cparams())(h2, w_gu, w_down)


def _layer_norm_bwd(dxhat, xhat, rstd):
    m1 = jnp.mean(dxhat, axis=1, keepdims=True)
    m2 = jnp.mean(dxhat * xhat, axis=1, keepdims=True)
    return rstd * (dxhat - m1 - xhat * m2)


def _colsum(a):
    return jnp.sum(a, axis=0, keepdims=True)


A_LN2G, A_LN2B, A_GF, A_SCF, A_SHF, A_LOSS = range(6)
B_LN1G, B_LN1B, B_GA, B_GN = range(4)
C_SCA, C_SHA = range(2)


def _ffn_backward(x1, ffn, target, gu, vec, w_gu, w_down, name):
    s = x1.shape[0]
    tb = min(FFN_BWD_TILE, s)

    def body(x1_ref, ffn_ref, t_ref, gu_ref, vec_ref, wgu_ref, wd_ref, dffn_ref, dgu_ref, dx1_ref, acc_ref):
        @pl.when(pl.program_id(0) == 0)
        def _():
            acc_ref[...] = jnp.zeros_like(acc_ref)

        x1v, ffn_v = x1_ref[...], ffn_ref[...]
        g_f = 1.0 + vec_ref[V_G_F:V_G_F + 1, :]
        u2 = ALPHA * x1v + g_f * ffn_v
        xhat, rstd = _layer_norm_stats(u2)
        ln_g = vec_ref[V_LN2G:V_LN2G + 1, :]
        err = xhat * ln_g + vec_ref[V_LN2B:V_LN2B + 1, :] - t_ref[...]
        dx2 = err * (1.0 / D)
        acc_ref[A_LOSS:A_LOSS + 1, :] += _colsum(err * err) * (0.5 / D)
        acc_ref[A_LN2G:A_LN2G + 1, :] += _colsum(dx2 * xhat)
        acc_ref[A_LN2B:A_LN2B + 1, :] += _colsum(dx2)
        du2 = _layer_norm_bwd(dx2 * ln_g, xhat, rstd)
        acc_ref[A_GF:A_GF + 1, :] += _colsum(du2 * ffn_v)
        dffn = (g_f * du2).astype(BF16)
        dffn_ref[...] = dffn
        dact = _dot_nt(dffn, wd_ref[...])
        gate, up = gu_ref[:, :D_FF].astype(F32), gu_ref[:, D_FF:].astype(F32)
        sg = 1.0 / (1.0 + jnp.exp(-gate))
        dgate = (dact * up * (sg * (1.0 + gate * (1.0 - sg)))).astype(BF16)
        dup = (dact * (gate * sg)).astype(BF16)
        dgu_ref[:, :D_FF] = dgate
        dgu_ref[:, D_FF:] = dup
        dh2 = _dot_nt(dgate, wgu_ref[:, :D_FF]) + _dot_nt(dup, wgu_ref[:, D_FF:])
        dx1_ref[...] = ALPHA * du2 + dh2 * (1.0 + vec_ref[V_SC_F:V_SC_F + 1, :])
        acc_ref[A_SCF:A_SCF + 1, :] += _colsum(dh2 * x1v)
        acc_ref[A_SHF:A_SHF + 1, :] += _colsum(dh2)

    full = pl.BlockSpec((tb, D), lambda i: (i, 0))
    wide = pl.BlockSpec((tb, 2 * D_FF), lambda i: (i, 0))
    return pl.pallas_call(
        body, name=name, grid=(s // tb,),
        out_shape=[jax.ShapeDtypeStruct((s, D), BF16), jax.ShapeDtypeStruct((s, 2 * D_FF), BF16),
                   jax.ShapeDtypeStruct((s, D), F32), jax.ShapeDtypeStruct((8, D), F32)],
        in_specs=[full, full, full, wide, _resident((VEC_ROWS, D)), _resident((D, 2 * D_FF)), _resident((D_FF, D))],
        out_specs=[full, wide, full, pl.BlockSpec((8, D), lambda i: (0, 0))],
        compiler_params=_cparams())(x1, ffn, target, gu, vec, w_gu, w_down)


def _attn_out_backward(dx1, x, attn, y_sb, y_sw, vec, w_out, name):
    s = x.shape[0]
    tb = min(TOK_TILE, s)

    def body(dx1_ref, x_ref, attn_ref, ysb_ref, ysw_ref, vec_ref, w_ref, du1_ref, dattn_ref, dy_ref, acc_ref):
        @pl.when(pl.program_id(0) == 0)
        def _():
            acc_ref[...] = jnp.zeros_like(acc_ref)

        attn = attn_ref[...]
        g_a = 1.0 + vec_ref[V_G_A:V_G_A + 1, :]
        xhat, rstd = _layer_norm_stats(ALPHA * x_ref[...] + g_a * attn)
        dx1v = dx1_ref[...]
        acc_ref[B_LN1G:B_LN1G + 1, :] += _colsum(dx1v * xhat)
        acc_ref[B_LN1B:B_LN1B + 1, :] += _colsum(dx1v)
        du1 = _layer_norm_bwd(dx1v * vec_ref[V_LN1G:V_LN1G + 1, :], xhat, rstd)
        du1_ref[...] = du1
        acc_ref[B_GA:B_GA + 1, :] += _colsum(du1 * attn)
        dattn = (g_a * du1).astype(BF16)
        dattn_ref[...] = dattn
        dmixed = _dot_nt(dattn, w_ref[...])
        for lo, y_ref in ((0, ysb_ref), (SB_W, ysw_ref)):
            y = y_ref[...]
            rr = _rms_parts(y)
            dn = dmixed[:, lo:lo + SB_W]
            acc_ref[B_GN:B_GN + 1, lo:lo + SB_W] += _colsum(dn * y * rr)
            dng = dn * vec_ref[V_GN:V_GN + 1, lo:lo + SB_W]
            dy_ref[:, lo:lo + SB_W] = rr * dng - y * (rr * rr * rr) * jnp.mean(dng * y, axis=1, keepdims=True)

    half = pl.BlockSpec((tb, SB_W), lambda i: (i, 0))
    full = pl.BlockSpec((tb, D), lambda i: (i, 0))
    return pl.pallas_call(
        body, name=name, grid=(s // tb,),
        out_shape=[jax.ShapeDtypeStruct((s, D), F32), jax.ShapeDtypeStruct((s, D), BF16),
                   jax.ShapeDtypeStruct((s, D), F32), jax.ShapeDtypeStruct((8, D), F32)],
        in_specs=[full, full, full, half, half, _resident((VEC_ROWS, D)), _resident((D, D))],
        out_specs=[full, full, full, pl.BlockSpec((8, D), lambda i: (0, 0))],
        compiler_params=_cparams())(dx1, x, attn, y_sb, y_sw, vec, w_out)


def _sb_backward(proj, sp_total, sweep_start, dy, slabs, name):
    s = proj.shape[0]
    tq, tk = min(SB_TQ, s), min(SB_TK, s)
    r = tq // tk
    nkb = SB_W // LANES
    nq = s // tq

    n_sl = len(slabs)

    def body(q_ref, k_ref, v_ref, tot_ref, do_ref, start_ref, *refs):
        slab_refs, (dq_ref, dk_ref, dv_ref), got_refs = refs[:n_sl], refs[n_sl:n_sl + 3], refs[n_sl + 3:2 * n_sl + 3]
        dq_acc, left_refs, gsum_refs = refs[2 * n_sl + 3:2 * n_sl + 6]
        i = pl.program_id(1)
        step = pl.program_id(0) * nq + i
        scatter = _scatter_exchange(slab_refs, got_refs, *refs[2 * n_sl + 6:])

        @pl.when(step == 0)
        def _():
            scatter.start()

        @pl.when(i == 0)
        def _():
            dk_ref[...] = jnp.zeros_like(dk_ref)
            dv_ref[...] = jnp.zeros_like(dv_ref)

        lane = lax.broadcasted_iota(jnp.int32, (1, LANES), 1)
        first = lane < HEAD_DIM
        qp, dop, totp = q_ref[...], do_ref[...], tot_ref[...]
        zero = jnp.zeros((), BF16)
        qs = (jnp.where(first, qp, zero), jnp.where(first, zero, qp))
        dofs = (jnp.where(first, dop, 0.0), jnp.where(first, 0.0, dop))
        dobs = tuple(d.astype(BF16) for d in dofs)
        dots = tuple(d.T.astype(BF16) for d in dofs)
        qts = tuple(qh.astype(F32).T.astype(BF16) for qh in qs)
        later = _sum_matrix(tk, lambda row, col: row > col)
        earlier = _sum_matrix(tk, lambda row, col: row < col)
        dq_acc[...] = jnp.zeros_like(dq_acc)
        gsum_refs[...] = jnp.zeros_like(gsum_refs)
        swapped = pltpu.roll(totp, HEAD_DIM, 1)
        left_refs[0] = jnp.where(first, totp, swapped)
        left_refs[1] = jnp.where(first, swapped, totp)

        def blocks(tiles):
            rows = [slice(r0, r0 + n) for r0, n, _, _ in tiles]
            kjs = [k_ref[pl.ds(pl.multiple_of(j * tk, tk), tk), :] for _, _, j, _ in tiles]
            vjs = [v_ref[pl.ds(pl.multiple_of(j * tk, tk), tk), :] for _, _, j, _ in tiles]
            chains = [(hd, t) for t in range(len(tiles)) for hd in range(2)]
            zs = [_dot_nt(qs[hd][rows[t]], kjs[t]) for hd, t in chains]
            dws = [_dot_nt(dobs[hd][rows[t]], vjs[t]) for hd, t in chains]
            parts = [_softplus_parts(z) for z in zs]
            sps = [p[0] for p in parts]
            befores = [_before(i * tq + r0, n, j * tk, tk) if diag else None for r0, n, j, diag in tiles]
            spms = [sp if befores[t] is None else jnp.where(befores[t], sp, 0.0) for (hd, t), sp in zip(chains, sps)]
            cums = [_block_sums(spm, later) for spm in spms]
            lefts, gsums, dq_sums = _RowState(left_refs, tk), _RowState(gsum_refs, tk), _RowState(dq_acc, tk, add=True)
            ws = []
            for (hd, t), z, sp, (cum, sm) in zip(chains, zs, sps, cums):
                r0, n = tiles[t][:2]
                left = lefts.get(hd, r0, n) - sm
                lefts.put(hd, r0, n, left)
                w = jnp.exp(z - sp - cum - _across(left, tk))
                ws.append(w if befores[t] is None else jnp.where(befores[t], w, 0.0))
            wbs = [w.astype(BF16) for w in ws]
            dvs = [_dot(dots[hd][:, rows[t]], wb) for (hd, t), wb in zip(chains, wbs)]
            gs = [dw * w for dw, w in zip(dws, ws)]
            gcums = [_block_sums(g, earlier) for g in gs]
            dzbs = []
            for (hd, t), z, (sp, e1), g, (gcum, gsm) in zip(chains, zs, parts, gs, gcums):
                r0, n = tiles[t][:2]
                gsum = gsums.get(hd, r0, n)
                inv = 1.0 / (1.0 + e1)
                sig = jnp.where(z >= 0.0, inv, e1 * inv)
                dz = g - sig * (g + _across(gsum, tk) + gcum)
                dzbs.append((dz if befores[t] is None else jnp.where(befores[t], dz, 0.0)).astype(BF16))
                gsums.put(hd, r0, n, gsum + gsm)
            dqs = [_dot(dzb, kjs[t]) for (hd, t), dzb in zip(chains, dzbs)]
            dks = [_dot(qts[hd][:, rows[t]], dzb) for (hd, t), dzb in zip(chains, dzbs)]
            for t, (_, _, j, _) in enumerate(tiles):
                dv_ref[j] += dvs[2 * t] + dvs[2 * t + 1]
                dk_ref[j] += dks[2 * t] + dks[2 * t + 1]
            for (hd, t), dq in zip(chains, dqs):
                dq_sums.accumulate(hd, *tiles[t][:2], dq)
            dq_sums.store()
            lefts.store()
            gsums.store()

        below = i * r
        start = jnp.clip(start_ref[pl.program_id(0), i].astype(jnp.int32), 0, below)

        def sweep(n, carry):
            blocks([(0, tq, start + n, False)])
            return carry

        lax.fori_loop(0, jnp.maximum(below - 1 - start, 0), sweep, 0)
        diagonal = [(d * tk, tk, below + e, e == d) for d in range(r) for e in range(d + 1)]

        @pl.when(i == 0)
        def _():
            blocks(diagonal)

        @pl.when(i > 0)
        def _():
            blocks([(0, tq, below - 1, False)] + diagonal)
        dq_ref[...] = jnp.where(first, dq_acc[0], dq_acc[1])

        @pl.when(step == nkb * nq - 1)
        def _():
            scatter.wait()

    shp = jax.ShapeDtypeStruct((s, SB_W), F32)
    qspec = pl.BlockSpec((tq, LANES), lambda p, i: (i, p))
    whole = pl.BlockSpec((None, s // tk, LANES, tk), lambda p, i: (p, 0, 0, 0))
    shp_t = jax.ShapeDtypeStruct((nkb, s // tk, LANES, tk), F32)
    hbm = pl.BlockSpec(memory_space=pl.ANY)
    return pl.pallas_call(
        body, name=name, grid=(nkb, nq),
        out_shape=[shp, shp_t, shp_t] + [jax.ShapeDtypeStruct(p.shape, p.dtype) for p in slabs],
        in_specs=[qspec,
                  pl.BlockSpec((s, LANES), lambda p, i: (0, nkb + p)),
                  pl.BlockSpec((s, LANES), lambda p, i: (0, 2 * nkb + p)),
                  qspec, qspec, pl.BlockSpec(memory_space=pltpu.SMEM)] + [hbm] * n_sl,
        out_specs=[qspec, whole, whole] + [hbm] * n_sl,
        scratch_shapes=[pltpu.VMEM((2, tq, LANES), F32), pltpu.VMEM((2, tq, LANES), F32), pltpu.VMEM((2, tq, LANES), F32)]
        + _exchange_sems(n_sl),
        compiler_params=_cparams())(proj, proj, proj, sp_total, dy, sweep_start, *slabs)


def _swa_backward(proj, y_sw, dy, sinks, gives, name):
    s = proj.shape[0]
    nb = s // WINDOW
    qb, kb, vb = 3 * SB_W // SWA_QW, (3 * SB_W + SWA_QW) // LANES, (3 * SB_W + SWA_QW + SWA_KW) // LANES

    n_gv = len(gives)

    def body(q_ref, kp_ref, kc_ref, vp_ref, vc_ref, o_ref, do_ref, sink_ref, *refs):
        give_refs, (dq_ref, dk_ref, dv_ref, ds_ref), got_refs = refs[:n_gv], refs[n_gv:n_gv + 4], refs[n_gv + 4:2 * n_gv + 4]
        n = pl.program_id(0)
        swap = _sibling_halves(give_refs, got_refs, *refs[2 * n_gv + 4:])

        @pl.when(n == 0)
        def _():
            for cp in swap:
                cp.start()

        @pl.when(n == 0)
        def _():
            dk_ref[...] = jnp.zeros_like(dk_ref)
            dv_ref[...] = jnp.zeros_like(dv_ref)
            ds_ref[...] = jnp.zeros_like(ds_ref)

        k = jnp.concatenate([kp_ref[...], kc_ref[...]], axis=0)
        v = jnp.concatenate([vp_ref[...], vc_ref[...]], axis=0)
        k_sw = pltpu.roll(k.astype(F32), HEAD_DIM, 1).astype(BF16)
        v_sw = pltpu.roll(v.astype(F32), HEAD_DIM, 1).astype(BF16)
        lane = lax.broadcasted_iota(jnp.int32, (1, LANES), 1)
        halves = [lane < HEAD_DIM, lane >= HEAD_DIM]
        valid, distf = _swa_masks(n)
        heads = range(2 * 4)
        cols = [slice((h // 2) * LANES, (h // 2 + 1) * LANES) for h in heads]
        qms = [jnp.where(halves[h % 2], q_ref[:, cols[h]], jnp.zeros((), BF16)) for h in heads]
        dos = [jnp.where(halves[h % 2], do_ref[:, cols[h]], 0.0) for h in heads]
        dobs = [d.astype(BF16) for d in dos]
        native = [h // 4 == h % 2 for h in heads]
        kus = [k if native[h] else k_sw for h in heads]
        vus = [v if native[h] else v_sw for h in heads]
        scores = [_dot_nt(qms[h], kus[h]) for h in heads]
        dps = [_dot_nt(dobs[h], vus[h]) for h in heads]
        deltas = [jnp.sum(dos[h] * o_ref[:, cols[h]], axis=1, keepdims=True) for h in heads]
        probs = [_swa_probs(scores[h], valid, distf, h, sink_ref[h]) for h in heads]
        pbs = [probs[h][0].astype(BF16) for h in heads]
        dscs = [(probs[h][0] * (dps[h] - deltas[h])).astype(BF16) for h in heads]
        dqs = [_dot(dscs[h], kus[h]) for h in heads]
        dks = [_dot_tn(dscs[h], qms[h]) for h in heads]
        dvs = [_dot_tn(pbs[h], dobs[h]) for h in heads]
        for h in heads:
            ds_ref[h:h + 1, :] += jnp.zeros((1, LANES), F32) - jnp.sum(probs[h][1] * deltas[h])
        for pair in range(4):
            dq_ref[:, cols[2 * pair]] = jnp.where(halves[0], dqs[2 * pair], dqs[2 * pair + 1])

        def gathered(parts):
            nat = sum(parts[h] for h in heads if native[h])
            rot = sum(parts[h] for h in heads if not native[h])
            return nat + pltpu.roll(rot, HEAD_DIM, 1)

        dk, dv = gathered(dks), gathered(dvs)
        prev = pl.multiple_of(jnp.maximum(n - 1, 0) * WINDOW, WINDOW)
        cur = pl.multiple_of(n * WINDOW, WINDOW)
        dk_ref[pl.ds(prev, WINDOW), :] += dk[:WINDOW]
        dv_ref[pl.ds(prev, WINDOW), :] += dv[:WINDOW]
        dk_ref[pl.ds(cur, WINDOW), :] += dk[WINDOW:]
        dv_ref[pl.ds(cur, WINDOW), :] += dv[WINDOW:]

        @pl.when(n == nb - 1)
        def _():
            for cp in swap:
                cp.wait()

    prev_blk = lambda n: jnp.maximum(n - 1, 0)
    wide = pl.BlockSpec((WINDOW, SWA_QW), lambda n: (n, 0))
    whole = pl.BlockSpec((s, LANES), lambda n: (0, 0))
    hbm = pl.BlockSpec(memory_space=pl.ANY)
    return pl.pallas_call(
        body, name=name, grid=(nb,),
        out_shape=[jax.ShapeDtypeStruct((s, SWA_QW), F32), jax.ShapeDtypeStruct((s, LANES), F32),
                   jax.ShapeDtypeStruct((s, LANES), F32), jax.ShapeDtypeStruct((8, LANES), F32)] + _halves_shapes(gives),
        in_specs=[pl.BlockSpec((WINDOW, SWA_QW), lambda n: (n, qb)),
                  pl.BlockSpec((WINDOW, LANES), lambda n: (prev_blk(n), kb)),
                  pl.BlockSpec((WINDOW, LANES), lambda n: (n, kb)),
                  pl.BlockSpec((WINDOW, LANES), lambda n: (prev_blk(n), vb)),
                  pl.BlockSpec((WINDOW, LANES), lambda n: (n, vb)),
                  wide,
                  pl.BlockSpec((WINDOW, SWA_QW), lambda n: (n, 1)),
                  pl.BlockSpec(memory_space=pltpu.SMEM)] + [hbm] * n_gv,
        out_specs=[wide, whole, whole, pl.BlockSpec((8, LANES), lambda n: (0, 0))] + [hbm] * n_gv,
        scratch_shapes=_halves_sems(n_gv),
        compiler_params=_cparams())(proj, proj, proj, proj, proj, y_sw, dy, sinks, *gives)


def _in_proj_backward(dq_sb, dkt_sb, dvt_sb, dq_sw, dk_sw, dv_sw, du1, x, vec, w_in, slabs, name):
    s = x.shape[0]
    tb = min(TOK_TILE, s)
    n_pairs, _, _, tk = dkt_sb.shape
    n_sl, steps = len(slabs), s // tb

    def body(dqsb_ref, dktsb_ref, dvtsb_ref, dqsw_ref, dksw_ref, dvsw_ref, du1_ref, x_ref, vec_ref, w_ref, *refs):
        slab_refs, (dproj_ref, gx_ref, acc_ref, bacc_ref) = refs[:n_sl], refs[n_sl:n_sl + 4]
        scatter = _scatter_exchange(slab_refs, refs[n_sl + 4:2 * n_sl + 4], *refs[2 * n_sl + 4:])

        @pl.when(pl.program_id(0) == 0)
        def _():
            scatter.start()
            acc_ref[...] = jnp.zeros_like(acc_ref)
            bacc_ref[...] = jnp.zeros_like(bacc_ref)

        pieces = ((0, dqsb_ref, QK_SCALE), (3 * SB_W, dqsw_ref, QK_SCALE), (3 * SB_W + SWA_QW, dksw_ref, 1.0),
                  (3 * SB_W + SWA_QW + SWA_KW, dvsw_ref, 1.0))
        for lo, ref, scale in pieces:
            width = ref.shape[1]
            piece = ref[...] * scale
            bacc_ref[0:1, lo:lo + width] += _colsum(piece)
            dproj_ref[:, lo:lo + width] = piece.astype(BF16)
        for base, ref in ((SB_W, dktsb_ref), (2 * SB_W, dvtsb_ref)):
            for p in range(n_pairs):
                lo = base + p * LANES
                for jj in range(tb // tk):
                    piece = ref[p, jj].T
                    bacc_ref[0:1, lo:lo + LANES] += _colsum(piece)
                    dproj_ref[jj * tk:(jj + 1) * tk, lo:lo + LANES] = piece.astype(BF16)
        dh = _dot_nt(dproj_ref[...], w_ref[...])
        xv = x_ref[...]
        gx_ref[...] = ALPHA * du1_ref[...] + dh * (1.0 + vec_ref[V_SC_A:V_SC_A + 1, :])
        acc_ref[C_SCA:C_SCA + 1, :] += _colsum(dh * xv)
        acc_ref[C_SHA:C_SHA + 1, :] += _colsum(dh)

        @pl.when(pl.program_id(0) == steps - 1)
        def _():
            scatter.wait()

    hbm = pl.BlockSpec(memory_space=pl.ANY)
    half = pl.BlockSpec((tb, SB_W), lambda i: (i, 0))
    narrow = pl.BlockSpec((tb, LANES), lambda i: (i, 0))
    full = pl.BlockSpec((tb, D), lambda i: (i, 0))
    blocks_t = pl.BlockSpec((n_pairs, tb // tk, LANES, tk), lambda i: (0, i, 0, 0))
    return pl.pallas_call(
        body, name=name, grid=(steps,),
        out_shape=[jax.ShapeDtypeStruct((s, D_IN), BF16), jax.ShapeDtypeStruct((s, D), F32),
                   jax.ShapeDtypeStruct((8, D), F32), jax.ShapeDtypeStruct((8, D_IN), F32)]
        + [jax.ShapeDtypeStruct(p.shape, p.dtype) for p in slabs],
        in_specs=[half, blocks_t, blocks_t, half, narrow, narrow, full, full, _resident((VEC_ROWS, D)),
                  _resident((D, D_IN))] + [hbm] * n_sl,
        out_specs=[pl.BlockSpec((tb, D_IN), lambda i: (i, 0)), full, pl.BlockSpec((8, D), lambda i: (0, 0)),
                   pl.BlockSpec((8, D_IN), lambda i: (0, 0))] + [hbm] * n_sl,
        scratch_shapes=_exchange_sems(n_sl),
        compiler_params=_cparams())(dq_sb, dkt_sb, dvt_sb, dq_sw, dk_sw, dv_sw, du1, x, vec, w_in, *slabs)


def _weight_grad(at, b, name, col_shards=1):
    m, s = at.shape
    n = b.shape[1]
    if col_shards > 1:
        tn = n // col_shards
        out_shape = jax.ShapeDtypeStruct((col_shards, m, tn), F32)
        out_spec = pl.BlockSpec((None, m, tn), lambda j, k: (j, 0, 0))
    else:
        tn = 512 if n % 512 == 0 else n
        out_shape = jax.ShapeDtypeStruct((m, n), F32)
        out_spec = pl.BlockSpec((m, tn), lambda j, k: (0, j))
    ts = min(WGRAD_TOKENS, s)
    while 2 * (m * ts * 2 + ts * tn * 2 + m * tn * 4) > WGRAD_VMEM and ts > 512:
        ts //= 2

    def body(at_ref, b_ref, o_ref):
        @pl.when(pl.program_id(1) == 0)
        def _():
            o_ref[...] = jnp.zeros_like(o_ref)

        o_ref[...] += _dot(at_ref[...], b_ref[...])

    return pl.pallas_call(
        body, name=name, grid=(n // tn, s // ts),
        out_shape=out_shape,
        in_specs=[pl.BlockSpec((m, ts), lambda j, k: (0, k)), pl.BlockSpec((ts, tn), lambda j, k: (k, j))],
        out_specs=out_spec,
        compiler_params=_cparams())(at, b)


def _pad_rows(v, rows):
    return jnp.concatenate([v, jnp.zeros((rows - v.shape[0], v.shape[1]), v.dtype)], axis=0)


def _col_shards(w, n_shards):
    r, n = w.shape
    return w.reshape(r, n_shards, n // n_shards).transpose(1, 0, 2)


def kernel(x, c, w_ada, b_ada, w_in, b_in, sinks, gn_sb, gn_swa, w_out, ln1_g, ln1_b, w_gu, w_down, ln2_g, ln2_b, loss_target, m_w_ada, m_b_ada, m_w_in, m_b_in, m_sinks, m_gn_sb, m_gn_swa, m_w_out, m_ln1_g, m_ln1_b, m_w_gu, m_w_down, m_ln2_g, m_ln2_b, v_w_ada, v_b_ada, v_w_in, v_b_in, v_sinks, v_gn_sb, v_gn_swa, v_w_out, v_ln1_g, v_ln1_b, v_w_gu, v_w_down, v_ln2_g, v_ln2_b):
    ix, iy, ic = lax.axis_index("x"), lax.axis_index("y"), lax.axis_index("c")
    chip = 2 * ix + iy
    dev = 4 * ix + 2 * iy + ic
    xs, target = x[0], loss_target[0]
    s = xs.shape[0]

    c_rows, g_in = _allgather8(_pad_rows(c, 8), "gather_c", gather=[w_in[0].astype(BF16)])
    c_all = c_rows[::8]
    n_ada = w_ada.shape[2]
    b_ada_shard = lax.dynamic_slice_in_dim(b_ada, chip * n_ada, n_ada, axis=1)
    mod_cols, silu_c = _mod_shard(c_all, w_ada[0], b_ada_shard, "mod_shard")
    mod_all = _allgather8(mod_cols, "gather_mod")[0].reshape(N_DEV, 8, n_ada)
    mod_mine = lax.dynamic_index_in_dim(mod_all, dev, axis=1, keepdims=False)
    mod = mod_mine.reshape(N_CHIPS, 2, n_ada)[:, 0].reshape(6, D)
    vec = jnp.concatenate([mod, ln1_g, ln1_b, ln2_g, ln2_b, jnp.concatenate([gn_sb, gn_swa], axis=1),
                           jnp.zeros((VEC_ROWS - 11, D), F32)], axis=0)

    w_in_b = g_in.transpose(1, 0, 2).reshape(D, D_IN)

    h_t, proj, g_out = _in_proj(xs, vec, w_in_b, b_in, [w_out[0].astype(BF16)], "in_proj")
    y_sb, sp_total, sweep_start, g_gu = _sb_forward(proj, [w_gu[0].astype(BF16)], "sb_forward")
    w_gu_b = g_gu.transpose(1, 0, 2).reshape(D, 2 * D_FF)
    w_out_b = g_out.reshape(D, D)
    sink_vec = sinks[0]
    y_sw, g_down = _swa_forward(proj, sink_vec, [w_down[0].astype(BF16)], "swa_forward")
    w_down_b = g_down.reshape(D_FF, D)
    mixed_t, attn, x1, h2_b, h2_t = _post_attention(y_sb, y_sw, xs, vec, w_out_b, "post_attention")
    gu, act_t, ffn = _ffn_forward(h2_b, w_gu_b, w_down_b, "ffn_forward")

    def in_halves(shards):
        n_sh, rows, cols = shards.shape
        return shards.reshape(n_sh, 2, rows // 2, cols)

    core = ic.reshape(1).astype(jnp.int32)
    dffn_b, dgu_b, dx1, acc_f = _ffn_backward(x1, ffn, target, gu, vec, w_gu_b, w_down_b, "ffn_backward")
    dw_gu = _weight_grad(h2_t, dgu_b, "grad_w_gu", col_shards=4)
    dw_down = _weight_grad(act_t, dffn_b, "grad_w_down")
    du1, dattn_b, dy, acc_a = _attn_out_backward(dx1, xs, attn, y_sb, y_sw, vec, w_out_b, "attn_out_backward")
    dw_out = _weight_grad(mixed_t, dattn_b, "grad_w_out")
    first = [in_halves(dw_gu), in_halves(dw_down.reshape(4, D_FF // 4, D)), in_halves(dw_out.reshape(4, D // 4, D))]
    dq_sw, dk_sw, dv_sw, dsink, *got_first = _swa_backward(proj, y_sw, dy, sink_vec, first, "swa_backward")
    sums_first = _chip_sums(first, got_first, core, "grad_chip_sums")
    dq_sb, dk_sb, dv_sb, parts_gu = _sb_backward(proj, sp_total, sweep_start, dy, sums_first[:1], "sb_backward")
    dproj_b, grad_x, acc_i, acc_b, *parts_rest = _in_proj_backward(
        dq_sb, dk_sb, dv_sb, dq_sw, dk_sw, dv_sw, du1, xs, vec, w_in_b, sums_first[1:], "in_proj_backward")
    parts_first = [parts_gu, *parts_rest]
    dw_in = _weight_grad(h_t, dproj_b, "grad_w_in")
    last = [in_halves(_col_shards(dw_in, 4))]
    sums_last = _chip_sums(last, _halves_swap(last, "grad_halves_swap_in"), core, "grad_chip_sum_in")

    dmod = jnp.concatenate([acc_i[C_SHA:C_SHA + 1], acc_i[C_SCA:C_SCA + 1], acc_a[B_GA:B_GA + 1],
                            acc_f[A_SHF:A_SHF + 1], acc_f[A_SCF:A_SCF + 1], acc_f[A_GF:A_GF + 1]], axis=1)
    dsink_row = jnp.concatenate([dsink[:, 0].reshape(1, 8), jnp.zeros((1, LANES - 8), F32)], axis=1)
    loss_row = jnp.concatenate([jnp.sum(acc_f[A_LOSS:A_LOSS + 1], axis=1, keepdims=True),
                                jnp.zeros((1, LANES - 1), F32)], axis=1)
    small = jnp.concatenate([dmod, acc_b[0:1], acc_a[B_LN1G:B_LN1G + 1], acc_a[B_LN1B:B_LN1B + 1],
                             acc_f[A_LN2G:A_LN2G + 1], acc_f[A_LN2B:A_LN2B + 1], acc_a[B_GN:B_GN + 1],
                             dsink_row, loss_row], axis=1)
    small_rows, *parts_last = _allgather8(_pad_rows(small, 8), "gather_small", scatter=sums_last)
    small_all = small_rows[::8]

    mine = _sum4s([*parts_first, *parts_last], "grad_reduce")
    theirs = _sibling_send(mine, "grad_half_return")
    gw_gu, gw_down, gw_out, gw_in = [
        jnp.concatenate([jnp.where(ic == 0, m_, t_), jnp.where(ic == 0, t_, m_)], axis=0) for m_, t_ in zip(mine, theirs)]

    small_names = ["b_ada", "b_in", "ln1_g", "ln1_b", "ln2_g", "ln2_b", "gn_sb", "gn_swa", "sinks"]
    small_at = [SM_MOD, SM_BIN, SM_LN1G, SM_LN1B, SM_LN2G, SM_LN2B, SM_GN, SM_GN + SB_W, SM_SINK]
    *small_out, loss_row_all = _small_update(
        small_all, small_at,
        [b_ada, b_in, ln1_g, ln1_b, ln2_g, ln2_b, gn_sb, gn_swa, sinks],
        [m_b_ada, m_b_in, m_ln1_g, m_ln1_b, m_ln2_g, m_ln2_b, m_gn_sb, m_gn_swa, m_sinks],
        [v_b_ada, v_b_in, v_ln1_g, v_ln1_b, v_ln2_g, v_ln2_b, v_gn_sb, v_gn_swa, v_sinks], SM_LOSS, "small_update")
    g_small, d_small, m2_small, v2_small = [dict(zip(small_names, leaves)) for leaves in small_out]
    loss = loss_row_all[0, 0]

    dmod_cols = lax.dynamic_slice_in_dim(small_all[:, SM_MOD:SM_BIN], chip * n_ada, n_ada, axis=1)
    gw_ada = _weight_grad(_pad_rows(silu_c, LANES).astype(BF16).T, _pad_rows(dmod_cols, LANES).astype(BF16), "grad_w_ada")

    big = {}
    for nm, w, g, m, v in (("w_ada", w_ada, gw_ada, m_w_ada, v_w_ada), ("w_in", w_in, gw_in, m_w_in, v_w_in),
                           ("w_out", w_out, gw_out, m_w_out, v_w_out), ("w_gu", w_gu, gw_gu, m_w_gu, v_w_gu),
                           ("w_down", w_down, gw_down, m_w_down, v_w_down)):
        d_, m2_, v2_ = _adamw(w[0], g, m[0], v[0], "adamw_" + nm)
        big[nm] = (g[None], d_[None], m2_[None], v2_[None])

    order = ["w_ada", "b_ada", "w_in", "b_in", "sinks", "gn_sb", "gn_swa", "w_out", "ln1_g", "ln1_b", "w_gu", "w_down",
             "ln2_g", "ln2_b"]

    def leaf(nm, which):
        if nm in big:
            return big[nm][which]
        return (g_small, d_small, m2_small, v2_small)[which][nm]

    outs = [loss, grad_x[None]]
    for which in range(4):
        outs += [leaf(nm, which) for nm in order]
    return tuple(outs)
```

```python
import math

import jax
import jax.numpy as jnp
from jax import lax
from jax.experimental import pallas as pl
from jax.experimental.pallas import tpu as pltpu

F32 = jnp.float32
BF16 = jnp.bfloat16

D = 1024
HEAD_DIM = 64
SB_W = 512
SWA_QW = 512
SWA_KW = 128
D_IN = 2304
D_FF = 2816
WINDOW = 128
ALPHA = 2.0 ** 0.25
LN_EPS = 1e-5
RMS_EPS = 1e-6
MASK_VALUE = -1e30
QK_SCALE = 1.0 / math.sqrt(HEAD_DIM)

ADAM_LR = 0.001
ADAM_B1 = 0.9
ADAM_B2 = 0.999
ADAM_EPS = 1e-08
ADAM_WD = 0.01
ADAM_STEP = 10

N_CHIPS = 4
N_DEV = 8
LANES = 128

SB_TQ = 512
SB_TK = 256
SB_DEAD_MASS = 110.0
TOK_TILE = 512
FFN_TILE = 256
FFN_BWD_TILE = 256
VMEM_LIMIT = 56 * 1024 * 1024
WGRAD_TOKENS = 2048
WGRAD_VMEM = 40 * 1024 * 1024

V_SH_A, V_SC_A, V_G_A, V_SH_F, V_SC_F, V_G_F, V_LN1G, V_LN1B, V_LN2G, V_LN2B, V_GN = range(11)
VEC_ROWS = 16

SM_MOD = 0
SM_BIN = 6 * D
SM_LN1G = SM_BIN + D_IN
SM_LN1B = SM_LN1G + D
SM_LN2G = SM_LN1B + D
SM_LN2B = SM_LN2G + D
SM_GN = SM_LN2B + D
SM_SINK = SM_GN + D
SM_LOSS = SM_SINK + LANES
SM_LEN = SM_LOSS + LANES

MESH = pl.DeviceIdType.MESH


def _cparams(**kw):
    return pltpu.CompilerParams(vmem_limit_bytes=VMEM_LIMIT, **kw)


def _resident(shape):
    nd = len(shape)
    return pl.BlockSpec(shape, lambda *_: (0,) * nd, pipeline_mode=pl.Buffered(1))


def _dot(a, b):
    return jnp.dot(a, b, preferred_element_type=F32)


def _dot_nt(a, b):
    return lax.dot_general(a, b, (((1,), (1,)), ((), ())), preferred_element_type=F32)


def _dot_tn(a, b):
    return lax.dot_general(a, b, (((0,), (0,)), ((), ())), preferred_element_type=F32)


def _sum_matrix(tk, keep):
    row = lax.broadcasted_iota(jnp.int32, (tk, tk + LANES), 0)
    col = lax.broadcasted_iota(jnp.int32, (tk, tk + LANES), 1)
    return (keep(row, col) | (col >= tk)).astype(BF16)


def _block_sums(x, m):
    tk = x.shape[1]
    res = _dot(x.astype(BF16), m)
    return res[:, :tk], res[:, tk:]


def _before(t0, n, s0, tk):
    return s0 + lax.broadcasted_iota(jnp.int32, (n, tk), 1) < t0 + lax.broadcasted_iota(jnp.int32, (n, tk), 0)


class _RowState:
    def __init__(self, ref, tk, add=False):
        self.ref, self.tk, self.add, self.vals = ref, tk, add, {}

    def _blocks(self, r0, n):
        return range(r0 // self.tk, (r0 + n) // self.tk)

    def get(self, hd, r0, n):
        for d in self._blocks(r0, n):
            if (hd, d) not in self.vals:
                self.vals[(hd, d)] = self.ref[hd, d * self.tk:(d + 1) * self.tk, :]
        parts = [self.vals[(hd, d)] for d in self._blocks(r0, n)]
        return parts[0] if len(parts) == 1 else jnp.concatenate(parts, axis=0)

    def put(self, hd, r0, n, val):
        for k, d in enumerate(self._blocks(r0, n)):
            self.vals[(hd, d)] = val[k * self.tk:(k + 1) * self.tk]

    def accumulate(self, hd, r0, n, val):
        for k, d in enumerate(self._blocks(r0, n)):
            part = val[k * self.tk:(k + 1) * self.tk]
            self.vals[(hd, d)] = part if (hd, d) not in self.vals else self.vals[(hd, d)] + part

    def store(self):
        for (hd, d), val in self.vals.items():
            span = slice(d * self.tk, (d + 1) * self.tk)
            if self.add:
                self.ref[hd, span, :] += val
            else:
                self.ref[hd, span, :] = val


def _across(v, tk):
    return jnp.concatenate([v] * (tk // LANES), axis=1)


def _allgather8(v, name, gather=(), scatter=()):
    m_per, n = v.shape
    n_g, n_s = len(gather), len(scatter)

    def body(x_ref, *refs):
        g_in, s_in = refs[:n_g], refs[n_g:n_g + n_s]
        out_ref = refs[n_g + n_s]
        g_out, s_out = refs[n_g + n_s + 1:2 * n_g + n_s + 1], refs[2 * n_g + n_s + 1:2 * (n_g + n_s) + 1]
        send_sems, recv_sems, local_sem, *more_sems = refs[2 * (n_g + n_s) + 1:]
        halves = _HalvesGather(g_in, g_out, *more_sems[:5]) if n_g else None
        beside = ([halves] if n_g else []) + ([_scatter_exchange(s_in, s_out, *more_sems[-3:])] if n_s else [])
        for ex in beside:
            ex.start()
        x, y, c = lax.axis_index("x"), lax.axis_index("y"), lax.axis_index("c")
        me, sibling = (x, y, c), (x, y, 1 - c)
        chips = [(1 - x, y), (x, 1 - y), (1 - x, 1 - y)]

        def rows(px, py, pc):
            return out_ref.at[pl.ds((4 * px + 2 * py + pc) * m_per, m_per), :]

        def copy(k, block, to, src=None):
            return pltpu.make_async_remote_copy(
                src_ref=rows(*block) if src is None else src, dst_ref=rows(*block),
                send_sem=send_sems.at[k], recv_sem=recv_sems.at[k], device_id=to, device_id_type=MESH)

        mine = pltpu.make_async_copy(x_ref, rows(*me), local_sem)
        mine.start()
        first = [copy(0, me, sibling, src=x_ref)]
        first += [copy(1 + j, me, (*chip, c), src=x_ref) for j, chip in enumerate(chips)]
        for cp in first:
            cp.start()
        passed = [copy(4 + j, (*chip, c), sibling) for j, chip in enumerate(chips)]
        for j, chip in enumerate(chips):
            copy(1 + j, (*chip, c), me).wait_recv()
            passed[j].start()
        copy(0, sibling, me).wait_recv()
        for j, chip in enumerate(chips):
            copy(4 + j, (*chip, 1 - c), me).wait_recv()
        for cp in first + passed:
            cp.wait_send()
        mine.wait()
        if halves is not None:
            halves.forward()
        for ex in beside:
            ex.wait()

    hbm = pl.BlockSpec(memory_space=pl.ANY)
    return pl.pallas_call(
        body, name=name,
        out_shape=[jax.ShapeDtypeStruct((N_DEV * m_per, n), v.dtype)]
        + [jax.ShapeDtypeStruct((N_CHIPS,) + a.shape, a.dtype) for a in gather]
        + [jax.ShapeDtypeStruct(p.shape, p.dtype) for p in scatter],
        in_specs=[pl.BlockSpec(memory_space=pltpu.VMEM)] + [hbm] * (n_g + n_s),
        out_specs=[pl.BlockSpec(memory_space=pltpu.VMEM)] + [hbm] * (n_g + n_s),
        scratch_shapes=[pltpu.SemaphoreType.DMA((7,)), pltpu.SemaphoreType.DMA((7,)), pltpu.SemaphoreType.DMA]
        + ([pltpu.SemaphoreType.DMA((3 * n_g,))] * 4 + [pltpu.SemaphoreType.DMA((n_g,))] if n_g else [])
        + (_exchange_sems(n_s) if n_s else []),
        compiler_params=_cparams(),
    )(v, *gather, *scatter)


class _Exchange:
    def __init__(self, local, sends, arrivals):
        self.local, self.sends, self.arrivals = local, sends, arrivals

    def start(self):
        for cp in self.local + self.sends:
            cp.start()

    def wait(self):
        for cp in self.arrivals:
            cp.wait_recv()
        for cp in self.sends:
            cp.wait_send()
        for cp in self.local:
            cp.wait()


def _exchange_sems(n):
    return [pltpu.SemaphoreType.DMA((3 * n,)), pltpu.SemaphoreType.DMA((3 * n,)), pltpu.SemaphoreType.DMA((n,))]


def _gather_exchange(ins, outs, send_sems, recv_sems, local_sems):
    x, y, c = lax.axis_index("x"), lax.axis_index("y"), lax.axis_index("c")
    slot = 2 * x + y
    chips = [(1 - x, y), (x, 1 - y), (1 - x, 1 - y)]
    local, sends, arrivals = [], [], []
    for a in range(len(ins)):
        local.append(pltpu.make_async_copy(ins[a], outs[a].at[slot], local_sems.at[a]))
        for j, (px, py) in enumerate(chips):
            sems = dict(send_sem=send_sems.at[3 * a + j], recv_sem=recv_sems.at[3 * a + j],
                        device_id=(px, py, c), device_id_type=MESH)
            sends.append(pltpu.make_async_remote_copy(src_ref=ins[a], dst_ref=outs[a].at[slot], **sems))
            arrivals.append(pltpu.make_async_remote_copy(src_ref=ins[a], dst_ref=outs[a].at[2 * px + py], **sems))
    return _Exchange(local, sends, arrivals)


class _HalvesGather:
    def __init__(self, ins, outs, far_send, far_recv, near_send, near_recv, local_sems):
        x, y, c = lax.axis_index("x"), lax.axis_index("y"), lax.axis_index("c")
        slot = 2 * x + y
        chips = [(1 - x, y), (x, 1 - y), (1 - x, 1 - y)]
        self.local, self.far, self.landed, self.near, self.passed = [], [], [], [], []
        for a in range(len(ins)):
            h = ins[a].shape[0] // 2
            mine, theirs = pl.ds(c * h, h), pl.ds((1 - c) * h, h)
            self.local.append(pltpu.make_async_copy(ins[a], outs[a].at[slot], local_sems.at[a]))
            for j, (px, py) in enumerate(chips):
                k, there = 3 * a + j, 2 * px + py
                far = dict(send_sem=far_send.at[k], recv_sem=far_recv.at[k], device_id=(px, py, c), device_id_type=MESH)
                near = dict(send_sem=near_send.at[k], recv_sem=near_recv.at[k], device_id=(x, y, 1 - c),
                            device_id_type=MESH)
                self.far.append(pltpu.make_async_remote_copy(
                    src_ref=ins[a].at[mine], dst_ref=outs[a].at[slot, mine], **far))
                self.landed.append(pltpu.make_async_remote_copy(
                    src_ref=ins[a].at[mine], dst_ref=outs[a].at[there, mine], **far))
                self.near.append(pltpu.make_async_remote_copy(
                    src_ref=outs[a].at[there, mine], dst_ref=outs[a].at[there, mine], **near))
                self.passed.append(pltpu.make_async_remote_copy(
                    src_ref=outs[a].at[there, mine], dst_ref=outs[a].at[there, theirs], **near))

    def start(self):
        for cp in self.local + self.far:
            cp.start()

    def forward(self):
        for landed, near in zip(self.landed, self.near):
            landed.wait_recv()
            near.start()

    def wait(self):
        for cp in self.passed:
            cp.wait_recv()
        for cp in self.far + self.near:
            cp.wait_send()
        for cp in self.local:
            cp.wait()


def _scatter_exchange(p_refs, out_refs, send_sems, recv_sems, local_sems):
    x, y, c = lax.axis_index("x"), lax.axis_index("y"), lax.axis_index("c")
    slot = 2 * x + y
    chips = [(1 - x, y), (x, 1 - y), (1 - x, 1 - y)]
    local, sends, arrivals = [], [], []
    for a, (p_ref, out_ref) in enumerate(zip(p_refs, out_refs)):
        local.append(pltpu.make_async_copy(p_ref.at[slot], out_ref.at[slot], local_sems.at[a]))
        for j, (px, py) in enumerate(chips):
            sems = dict(send_sem=send_sems.at[3 * a + j], recv_sem=recv_sems.at[3 * a + j],
                        device_id=(px, py, c), device_id_type=MESH)
            sends.append(pltpu.make_async_remote_copy(src_ref=p_ref.at[2 * px + py], dst_ref=out_ref.at[slot], **sems))
            arrivals.append(pltpu.make_async_remote_copy(src_ref=p_ref.at[slot], dst_ref=out_ref.at[2 * px + py], **sems))
    return _Exchange(local, sends, arrivals)


def _sibling_halves(give_refs, got_refs, send_sems, recv_sems):
    x, y, c = lax.axis_index("x"), lax.axis_index("y"), lax.axis_index("c")
    copies = []
    for a, (give_ref, got_ref) in enumerate(zip(give_refs, got_refs)):
        for s in range(N_CHIPS):
            copies.append(pltpu.make_async_remote_copy(
                src_ref=give_ref.at[s, 1 - c], dst_ref=got_ref.at[s], send_sem=send_sems.at[N_CHIPS * a + s],
                recv_sem=recv_sems.at[N_CHIPS * a + s], device_id=(x, y, 1 - c), device_id_type=MESH))
    return copies


def _halves_shapes(arrs):
    return [jax.ShapeDtypeStruct((a.shape[0],) + a.shape[2:], a.dtype) for a in arrs]


def _halves_sems(n):
    return [pltpu.SemaphoreType.DMA((N_CHIPS * n,)), pltpu.SemaphoreType.DMA((N_CHIPS * n,))]


def _halves_swap(arrs, name):
    n = len(arrs)

    def body(*refs):
        copies = _sibling_halves(refs[:n], refs[n:2 * n], *refs[2 * n:])
        for cp in copies:
            cp.start()
        for cp in copies:
            cp.wait()

    hbm = pl.BlockSpec(memory_space=pl.ANY)
    return pl.pallas_call(body, name=name, out_shape=_halves_shapes(arrs), in_specs=[hbm] * n, out_specs=[hbm] * n,
                          scratch_shapes=_halves_sems(n), compiler_params=_cparams())(*arrs)


def _sibling_send(arrs, name):
    n = len(arrs)

    def body(*refs):
        x, y, c = lax.axis_index("x"), lax.axis_index("y"), lax.axis_index("c")
        send_sems, recv_sems = refs[2 * n:]
        copies = [pltpu.make_async_remote_copy(src_ref=refs[a], dst_ref=refs[n + a], send_sem=send_sems.at[a],
                                               recv_sem=recv_sems.at[a], device_id=(x, y, 1 - c), device_id_type=MESH)
                  for a in range(n)]
        for cp in copies:
            cp.start()
        for cp in copies:
            cp.wait()

    hbm = pl.BlockSpec(memory_space=pl.ANY)
    return pl.pallas_call(
        body, name=name, out_shape=[jax.ShapeDtypeStruct(a.shape, a.dtype) for a in arrs],
        in_specs=[hbm] * n, out_specs=[hbm] * n,
        scratch_shapes=[pltpu.SemaphoreType.DMA((n,)), pltpu.SemaphoreType.DMA((n,))],
        compiler_params=_cparams(),
    )(*arrs)


def _row_tile(h):
    return h // 2 if (h // 2) % 8 == 0 else h


def _row_tiles(hs):
    tiles = [_row_tile(h) for h in hs]
    assert len({h // t for h, t in zip(hs, tiles)}) == 1
    return tiles, hs[0] // tiles[0]


def _chip_sums(arrs, gots, core, name):
    n = len(arrs)
    tiles, steps = _row_tiles([a.shape[2] for a in arrs])

    def body(core_ref, *refs):
        for a_ref, b_ref, o_ref in zip(refs[:n], refs[n:2 * n], refs[2 * n:]):
            o_ref[...] = a_ref[...] + b_ref[...]

    slabs = [pl.BlockSpec((None, tr, a.shape[3]), lambda s, i, core_ref: (s, i, 0)) for a, tr in zip(arrs, tiles)]
    grid_spec = pltpu.PrefetchScalarGridSpec(
        num_scalar_prefetch=1, grid=(N_CHIPS, steps),
        in_specs=[pl.BlockSpec((None, None, tr, a.shape[3]), lambda s, i, core_ref: (s, core_ref[0], i, 0))
                  for a, tr in zip(arrs, tiles)] + slabs,
        out_specs=slabs)
    return pl.pallas_call(body, name=name, grid_spec=grid_spec,
                          out_shape=[jax.ShapeDtypeStruct(g.shape, g.dtype) for g in gots],
                          compiler_params=_cparams())(core, *arrs, *gots)


def _sum4s(ps, name):
    tiles, steps = _row_tiles([p.shape[1] for p in ps])

    def body(*refs):
        for p_ref, o_ref in zip(refs[:len(ps)], refs[len(ps):]):
            o_ref[...] = ((p_ref[0] + p_ref[1]) + p_ref[2]) + p_ref[3]

    return pl.pallas_call(
        body, name=name, grid=(steps,), out_shape=[jax.ShapeDtypeStruct(p.shape[1:], p.dtype) for p in ps],
        in_specs=[pl.BlockSpec((4, tr, p.shape[2]), lambda i: (0, i, 0)) for p, tr in zip(ps, tiles)],
        out_specs=[pl.BlockSpec((tr, p.shape[2]), lambda i: (i, 0)) for p, tr in zip(ps, tiles)],
        compiler_params=_cparams())(*ps)


def _adam_math(w, g, m, v):
    m2 = ADAM_B1 * m + (1.0 - ADAM_B1) * g
    v2 = ADAM_B2 * v + (1.0 - ADAM_B2) * (g * g)
    m_hat = m2 / (1.0 - ADAM_B1 ** ADAM_STEP)
    v_hat = v2 / (1.0 - ADAM_B2 ** ADAM_STEP)
    delta = -ADAM_LR * (m_hat / (jnp.sqrt(v_hat) + ADAM_EPS) + ADAM_WD * w)
    return delta, m2, v2


def _adamw(w, g, m, v, name):
    rows, cols = w.shape
    tr = rows // 4 if rows % 32 == 0 else rows

    def body(w_ref, g_ref, m_ref, v_ref, d_ref, m2_ref, v2_ref):
        delta, m2, v2 = _adam_math(w_ref[...], g_ref[...], m_ref[...], v_ref[...])
        d_ref[...] = delta
        m2_ref[...] = m2
        v2_ref[...] = v2

    spec = pl.BlockSpec((tr, cols), lambda i: (i, 0))
    shp = jax.ShapeDtypeStruct(w.shape, F32)
    return pl.pallas_call(body, name=name, grid=(rows // tr,), out_shape=[shp, shp, shp],
                          in_specs=[spec] * 4, out_specs=[spec] * 3, compiler_params=_cparams())(w, g, m, v)


def _small_update(g8, offsets, ws, ms, vs, loss_at, name):
    k = len(ws)

    def summed(g8_ref, lo, width):
        g = g8_ref[0:1, lo:lo + width]
        for r in range(1, N_DEV):
            g = g + g8_ref[r:r + 1, lo:lo + width]
        return g

    def body(g8_ref, *refs):
        ins, outs = refs[:3 * k], refs[3 * k:]
        for j in range(k):
            g = summed(g8_ref, offsets[j], ws[j].shape[1])
            delta, m2, v2 = _adam_math(ins[j][...], g, ins[k + j][...], ins[2 * k + j][...])
            for kind, val in enumerate((g, delta, m2, v2)):
                outs[kind * k + j][...] = val
        outs[4 * k][...] = summed(g8_ref, loss_at, LANES)

    vm = pl.BlockSpec(memory_space=pltpu.VMEM)
    shapes = [jax.ShapeDtypeStruct(w.shape, F32) for w in ws] * 4 + [jax.ShapeDtypeStruct((1, LANES), F32)]
    res = pl.pallas_call(body, name=name, out_shape=shapes, in_specs=[vm] * (1 + 3 * k), out_specs=[vm] * (4 * k + 1),
                         compiler_params=_cparams())(g8, *ws, *ms, *vs)
    return res[:k], res[k:2 * k], res[2 * k:3 * k], res[3 * k:4 * k], res[4 * k]


def _mod_shard(c8, w_ada, b_ada_shard, name):
    n = w_ada.shape[1]
    tn = 512

    def body(c_ref, w_ref, b_ref, o_ref, s_ref):
        cv = c_ref[...]
        sc = cv * (1.0 / (1.0 + jnp.exp(-cv)))
        s_ref[...] = sc
        o_ref[...] = _dot(sc.astype(BF16), w_ref[...].astype(BF16)) + b_ref[...]

    return pl.pallas_call(
        body, name=name, grid=(n // tn,),
        out_shape=[jax.ShapeDtypeStruct((8, n), F32), jax.ShapeDtypeStruct((8, D), F32)],
        in_specs=[pl.BlockSpec((8, D), lambda j: (0, 0)), pl.BlockSpec((D, tn), lambda j: (0, j)),
                  pl.BlockSpec((1, tn), lambda j: (0, j))],
        out_specs=[pl.BlockSpec((8, tn), lambda j: (0, j)), pl.BlockSpec((8, D), lambda j: (0, 0))],
        compiler_params=_cparams())(c8, w_ada, b_ada_shard)


def _layer_norm_stats(u):
    mu = jnp.mean(u, axis=1, keepdims=True)
    d = u - mu
    var = jnp.mean(d * d, axis=1, keepdims=True)
    rstd = lax.rsqrt(var + LN_EPS)
    return d * rstd, rstd


def _in_proj(x, vec, w_in, b_in, name):
    s = x.shape[0]
    tb = min(TOK_TILE, s)

    def body(x_ref, vec_ref, w_ref, b_ref, ht_ref, p_ref):
        h = x_ref[...] * (1.0 + vec_ref[V_SC_A:V_SC_A + 1, :]) + vec_ref[V_SH_A:V_SH_A + 1, :]
        hb = h.astype(BF16)
        ht_ref[...] = h.T.astype(BF16)
        proj = _dot(hb, w_ref[...]) + b_ref[...]
        col = lax.broadcasted_iota(jnp.int32, (1, D_IN), 1)
        is_q = (col < SB_W) | ((col >= 3 * SB_W) & (col < 3 * SB_W + SWA_QW))
        p_ref[...] = (proj * jnp.where(is_q, QK_SCALE, 1.0)).astype(BF16)

    return pl.pallas_call(
        body, name=name, grid=(s // tb,),
        out_shape=[jax.ShapeDtypeStruct((D, s), BF16), jax.ShapeDtypeStruct((s, D_IN), BF16)],
        in_specs=[pl.BlockSpec((tb, D), lambda i: (i, 0)), _resident((VEC_ROWS, D)), _resident((D, D_IN)),
                  _resident((1, D_IN))],
        out_specs=[pl.BlockSpec((D, tb), lambda i: (0, i)), pl.BlockSpec((tb, D_IN), lambda i: (i, 0))],
        compiler_params=_cparams())(x, vec, w_in, b_in)


def _softplus_parts(z):
    e1 = jnp.exp(-jnp.abs(z))
    sp = jnp.maximum(z, 0.0) + jnp.log(1.0 + e1)
    return sp, e1


def _sb_forward(proj, shards, name):
    s = proj.shape[0]
    tq, tk = min(SB_TQ, s), min(SB_TK, s)
    r = tq // tk

    n_sh = len(shards)
    nkb = SB_W // LANES
    nq = s // tq

    def body(q_ref, k_ref, v_ref, *refs):
        sh_refs, (o_ref, tot_ref, start_ref), got_refs = refs[:n_sh], refs[n_sh:n_sh + 3], refs[n_sh + 3:2 * n_sh + 3]
        acc_refs, run_refs = refs[2 * n_sh + 3:2 * n_sh + 5]
        i = pl.program_id(1)
        step = pl.program_id(0) * nq + i
        gather = _gather_exchange(sh_refs, got_refs, *refs[2 * n_sh + 5:])

        @pl.when(step == 0)
        def _():
            gather.start()

        lane = lax.broadcasted_iota(jnp.int32, (1, LANES), 1)
        first = lane < HEAD_DIM
        qp = q_ref[...]
        zero = jnp.zeros((), BF16)
        qs = (jnp.where(first, qp, zero), jnp.where(first, zero, qp))
        later = _sum_matrix(tk, lambda row, col: row > col)
        acc_refs[...] = jnp.zeros_like(acc_refs)
        run_refs[...] = jnp.zeros_like(run_refs)

        def blocks(tiles):
            rows = [slice(r0, r0 + n) for r0, n, _, _ in tiles]
            kjs = [k_ref[pl.ds(pl.multiple_of(j * tk, tk), tk), :] for _, _, j, _ in tiles]
            vjs = [v_ref[pl.ds(pl.multiple_of(j * tk, tk), tk), :] for _, _, j, _ in tiles]
            chains = [(hd, t) for t in range(len(tiles)) for hd in range(2)]
            zs = [_dot_nt(qs[hd][rows[t]], kjs[t]) for hd, t in chains]
            sps = [_softplus_parts(z)[0] for z in zs]
            befores = [_before(i * tq + r0, n, j * tk, tk) if diag else None for r0, n, j, diag in tiles]
            spms = [sp if befores[t] is None else jnp.where(befores[t], sp, 0.0) for (hd, t), sp in zip(chains, sps)]
            cums = [_block_sums(spm, later) for spm in spms]
            runs, accs, ws = _RowState(run_refs, tk), _RowState(acc_refs, tk, add=True), []
            for (hd, t), z, sp, (cum, sm) in zip(chains, zs, sps, cums):
                r0, n = tiles[t][:2]
                run = runs.get(hd, r0, n)
                w = jnp.exp(z - sp - cum - _across(run, tk))
                if befores[t] is not None:
                    w = jnp.where(befores[t], w, 0.0)
                ws.append(w.astype(BF16))
                runs.put(hd, r0, n, run + sm)
            for (hd, t), pv in zip(chains, [_dot(w, vjs[t]) for (hd, t), w in zip(chains, ws)]):
                accs.accumulate(hd, *tiles[t][:2], pv)
            accs.store()
            runs.store()

        below = i * r
        diagonal = [(d * tk, tk, below + e, e == d) for d in range(r) for e in range(d, -1, -1)]

        @pl.when(i == 0)
        def _():
            blocks(diagonal)

        @pl.when(i > 0)
        def _():
            blocks(diagonal + [(0, tq, below - 1, False)])

        def swept_mass():
            return jnp.min(jnp.minimum(run_refs[0], run_refs[1]))

        def more(carry):
            n, mass = carry
            return (n < below) & (mass < SB_DEAD_MASS)

        def sweep(carry):
            n, _ = carry
            blocks([(0, tq, below - 1 - n, False)])
            return n + 1, swept_mass()

        n_swept, _ = lax.while_loop(more, sweep, (jnp.minimum(below, 1), swept_mass()))
        start_ref[pl.program_id(0), i] = (below - n_swept).astype(F32)
        o_ref[...] = jnp.where(first, acc_refs[0], acc_refs[1])
        tot_ref[...] = jnp.where(first, run_refs[0], run_refs[1])

        @pl.when(step == nkb * nq - 1)
        def _():
            gather.wait()

    shp = jax.ShapeDtypeStruct((s, SB_W), F32)
    qspec = pl.BlockSpec((tq, LANES), lambda p, i: (i, p))
    hbm = pl.BlockSpec(memory_space=pl.ANY)
    return pl.pallas_call(
        body, name=name, grid=(nkb, nq),
        out_shape=[shp, shp, jax.ShapeDtypeStruct((nkb, nq), F32)]
        + [jax.ShapeDtypeStruct((N_CHIPS,) + a.shape, a.dtype) for a in shards],
        in_specs=[qspec,
                  pl.BlockSpec((s, LANES), lambda p, i: (0, nkb + p)),
                  pl.BlockSpec((s, LANES), lambda p, i: (0, 2 * nkb + p))] + [hbm] * n_sh,
        out_specs=[qspec, qspec, pl.BlockSpec(memory_space=pltpu.SMEM)] + [hbm] * n_sh,
        scratch_shapes=[pltpu.VMEM((2, tq, LANES), F32), pltpu.VMEM((2, tq, LANES), F32)] + _exchange_sems(n_sh),
        compiler_params=_cparams())(proj, proj, proj, *shards)


def _swa_masks(n):
    ti = lax.broadcasted_iota(jnp.int32, (WINDOW, 2 * WINDOW), 0)
    kj = lax.broadcasted_iota(jnp.int32, (WINDOW, 2 * WINDOW), 1)
    dist = ti + WINDOW - kj
    valid = (dist >= 0) & (dist < WINDOW) & ((n * WINDOW - WINDOW + kj) >= 0)
    return valid, dist.astype(F32)


def _swa_probs(sc, valid, distf, h, sink):
    slope = 2.0 ** (-(h + 1))
    sc = jnp.where(valid, sc - slope * distf, MASK_VALUE)
    mx = jnp.maximum(jnp.max(sc, axis=1, keepdims=True), sink)
    p = jnp.exp(sc - mx)
    es = jnp.exp(sink - mx)
    inv = 1.0 / (jnp.sum(p, axis=1, keepdims=True) + es)
    return p * inv, es * inv


def _swa_forward(proj, sinks, shards, name):
    s = proj.shape[0]
    nb = s // WINDOW
    qb, kb, vb = 3 * SB_W // SWA_QW, (3 * SB_W + SWA_QW) // LANES, (3 * SB_W + SWA_QW + SWA_KW) // LANES
    n_sh = len(shards)

    def body(q_ref, kp_ref, kc_ref, vp_ref, vc_ref, sink_ref, *refs):
        sh_refs, o_ref, got_refs = refs[:n_sh], refs[n_sh], refs[n_sh + 1:2 * n_sh + 1]
        n = pl.program_id(0)
        gather = _gather_exchange(sh_refs, got_refs, *refs[2 * n_sh + 1:])

        @pl.when(n == 0)
        def _():
            gather.start()

        k = jnp.concatenate([kp_ref[...], kc_ref[...]], axis=0)
        v = jnp.concatenate([vp_ref[...], vc_ref[...]], axis=0)
        k_sw = pltpu.roll(k.astype(F32), HEAD_DIM, 1).astype(BF16)
        v_sw = pltpu.roll(v.astype(F32), HEAD_DIM, 1).astype(BF16)
        lane = lax.broadcasted_iota(jnp.int32, (1, LANES), 1)
        halves = [lane < HEAD_DIM, lane >= HEAD_DIM]
        valid, distf = _swa_masks(n)
        heads = range(2 * 4)
        qms = [jnp.where(halves[h % 2], q_ref[:, (h // 2) * LANES:(h // 2 + 1) * LANES], jnp.zeros((), BF16))
               for h in heads]
        kus = [k if h // 4 == h % 2 else k_sw for h in heads]
        vus = [v if h // 4 == h % 2 else v_sw for h in heads]
        scores = [_dot_nt(qms[h], kus[h]) for h in heads]
        ps = [_swa_probs(scores[h], valid, distf, h, sink_ref[h])[0].astype(BF16) for h in heads]
        outs = [_dot(ps[h], vus[h]) for h in heads]
        for pair in range(4):
            o_ref[:, pair * LANES:(pair + 1) * LANES] = jnp.where(halves[0], outs[2 * pair], outs[2 * pair + 1])

        @pl.when(n == nb - 1)
        def _():
            gather.wait()

    prev = lambda n: jnp.maximum(n - 1, 0)
    hbm = pl.BlockSpec(memory_space=pl.ANY)
    return pl.pallas_call(
        body, name=name, grid=(nb,),
        out_shape=[jax.ShapeDtypeStruct((s, SWA_QW), F32)]
        + [jax.ShapeDtypeStruct((N_CHIPS,) + a.shape, a.dtype) for a in shards],
        in_specs=[pl.BlockSpec((WINDOW, SWA_QW), lambda n: (n, qb)),
                  pl.BlockSpec((WINDOW, LANES), lambda n: (prev(n), kb)),
                  pl.BlockSpec((WINDOW, LANES), lambda n: (n, kb)),
                  pl.BlockSpec((WINDOW, LANES), lambda n: (prev(n), vb)),
                  pl.BlockSpec((WINDOW, LANES), lambda n: (n, vb)),
                  pl.BlockSpec(memory_space=pltpu.SMEM)] + [hbm] * n_sh,
        out_specs=[pl.BlockSpec((WINDOW, SWA_QW), lambda n: (n, 0))] + [hbm] * n_sh,
        scratch_shapes=_exchange_sems(n_sh),
        compiler_params=_cparams())(proj, proj, proj, proj, proj, sinks, *shards)


def _rms_parts(y):
    return lax.rsqrt(jnp.mean(y * y, axis=1, keepdims=True) + RMS_EPS)


def _post_attention(y_sb, y_sw, x, vec, w_out, name):
    s = x.shape[0]
    tb = min(TOK_TILE, s)

    def body(ysb_ref, ysw_ref, x_ref, vec_ref, w_ref, mixedt_ref, attn_ref, x1_ref, h2_ref, h2t_ref):
        ysb, ysw = ysb_ref[...], ysw_ref[...]
        nsb_f = ysb * _rms_parts(ysb) * vec_ref[V_GN:V_GN + 1, :SB_W]
        nsw_f = ysw * _rms_parts(ysw) * vec_ref[V_GN:V_GN + 1, SB_W:]
        nsb, nsw = nsb_f.astype(BF16), nsw_f.astype(BF16)
        mixedt_ref[:SB_W, :] = nsb_f.T.astype(BF16)
        mixedt_ref[SB_W:, :] = nsw_f.T.astype(BF16)
        attn = _dot(nsb, w_ref[:SB_W, :]) + _dot(nsw, w_ref[SB_W:, :])
        attn_ref[...] = attn
        u1 = ALPHA * x_ref[...] + (1.0 + vec_ref[V_G_A:V_G_A + 1, :]) * attn
        xhat, _ = _layer_norm_stats(u1)
        x1 = xhat * vec_ref[V_LN1G:V_LN1G + 1, :] + vec_ref[V_LN1B:V_LN1B + 1, :]
        x1_ref[...] = x1
        h2 = x1 * (1.0 + vec_ref[V_SC_F:V_SC_F + 1, :]) + vec_ref[V_SH_F:V_SH_F + 1, :]
        h2_ref[...] = h2.astype(BF16)
        h2t_ref[...] = h2.T.astype(BF16)

    half = pl.BlockSpec((tb, SB_W), lambda i: (i, 0))
    full = pl.BlockSpec((tb, D), lambda i: (i, 0))
    full_t = pl.BlockSpec((D, tb), lambda i: (0, i))
    return pl.pallas_call(
        body, name=name, grid=(s // tb,),
        out_shape=[jax.ShapeDtypeStruct((D, s), BF16), jax.ShapeDtypeStruct((s, D), F32),
                   jax.ShapeDtypeStruct((s, D), F32), jax.ShapeDtypeStruct((s, D), BF16),
                   jax.ShapeDtypeStruct((D, s), BF16)],
        in_specs=[half, half, full, _resident((VEC_ROWS, D)), _resident((D, D))],
        out_specs=[full_t, full, full, full, full_t],
        compiler_params=_cparams())(y_sb, y_sw, x, vec, w_out)


def _ffn_forward(h2, w_gu, w_down, name):
    s = h2.shape[0]
    tb = min(FFN_TILE, s)

    def body(h_ref, wgu_ref, wd_ref, gu_ref, actt_ref, ffn_ref):
        gu = _dot(h_ref[...], wgu_ref[...])
        gu_ref[...] = gu.astype(BF16)
        gate, up = gu[:, :D_FF], gu[:, D_FF:]
        act = gate * (1.0 / (1.0 + jnp.exp(-gate))) * up
        actt_ref[...] = act.T.astype(BF16)
        ffn_ref[...] = _dot(act.astype(BF16), wd_ref[...])

    return pl.pallas_call(
        body, name=name, grid=(s // tb,),
        out_shape=[jax.ShapeDtypeStruct((s, 2 * D_FF), BF16), jax.ShapeDtypeStruct((D_FF, s), BF16),
                   jax.ShapeDtypeStruct((s, D), F32)],
        in_specs=[pl.BlockSpec((tb, D), lambda i: (i, 0)), _resident((D, 2 * D_FF)), _resident((D_FF, D))],
        out_specs=[pl.BlockSpec((tb, 2 * D_FF), lambda i: (i, 0)), pl.BlockSpec((D_FF, tb), lambda i: (0, i)),
                   pl.BlockSpec((tb, D), lambda i: (i, 0))],
        compiler_params=_cparams())(h2, w_gu, w_down)


def _layer_norm_bwd(dxhat, xhat, rstd):
    m1 = jnp.mean(dxhat, axis=1, keepdims=True)
    m2 = jnp.mean(dxhat * xhat, axis=1, keepdims=True)
    return rstd * (dxhat - m1 - xhat * m2)


def _colsum(a):
    return jnp.sum(a, axis=0, keepdims=True)


A_LN2G, A_LN2B, A_GF, A_SCF, A_SHF, A_LOSS = range(6)
B_LN1G, B_LN1B, B_GA, B_GN = range(4)
C_SCA, C_SHA = range(2)


def _ffn_backward(x1, ffn, target, gu, vec, w_gu, w_down, name):
    s = x1.shape[0]
    tb = min(FFN_BWD_TILE, s)

    def body(x1_ref, ffn_ref, t_ref, gu_ref, vec_ref, wgu_ref, wd_ref, dffn_ref, dgu_ref, dx1_ref, acc_ref):
        @pl.when(pl.program_id(0) == 0)
        def _():
            acc_ref[...] = jnp.zeros_like(acc_ref)

        x1v, ffn_v = x1_ref[...], ffn_ref[...]
        g_f = 1.0 + vec_ref[V_G_F:V_G_F + 1, :]
        u2 = ALPHA * x1v + g_f * ffn_v
        xhat, rstd = _layer_norm_stats(u2)
        ln_g = vec_ref[V_LN2G:V_LN2G + 1, :]
        err = xhat * ln_g + vec_ref[V_LN2B:V_LN2B + 1, :] - t_ref[...]
        dx2 = err * (1.0 / D)
        acc_ref[A_LOSS:A_LOSS + 1, :] += _colsum(err * err) * (0.5 / D)
        acc_ref[A_LN2G:A_LN2G + 1, :] += _colsum(dx2 * xhat)
        acc_ref[A_LN2B:A_LN2B + 1, :] += _colsum(dx2)
        du2 = _layer_norm_bwd(dx2 * ln_g, xhat, rstd)
        acc_ref[A_GF:A_GF + 1, :] += _colsum(du2 * ffn_v)
        dffn = (g_f * du2).astype(BF16)
        dffn_ref[...] = dffn
        dact = _dot_nt(dffn, wd_ref[...])
        gate, up = gu_ref[:, :D_FF].astype(F32), gu_ref[:, D_FF:].astype(F32)
        sg = 1.0 / (1.0 + jnp.exp(-gate))
        dgate = (dact * up * (sg * (1.0 + gate * (1.0 - sg)))).astype(BF16)
        dup = (dact * (gate * sg)).astype(BF16)
        dgu_ref[:, :D_FF] = dgate
        dgu_ref[:, D_FF:] = dup
        dh2 = _dot_nt(dgate, wgu_ref[:, :D_FF]) + _dot_nt(dup, wgu_ref[:, D_FF:])
        dx1_ref[...] = ALPHA * du2 + dh2 * (1.0 + vec_ref[V_SC_F:V_SC_F + 1, :])
        acc_ref[A_SCF:A_SCF + 1, :] += _colsum(dh2 * x1v)
        acc_ref[A_SHF:A_SHF + 1, :] += _colsum(dh2)

    full = pl.BlockSpec((tb, D), lambda i: (i, 0))
    wide = pl.BlockSpec((tb, 2 * D_FF), lambda i: (i, 0))
    return pl.pallas_call(
        body, name=name, grid=(s // tb,),
        out_shape=[jax.ShapeDtypeStruct((s, D), BF16), jax.ShapeDtypeStruct((s, 2 * D_FF), BF16),
                   jax.ShapeDtypeStruct((s, D), F32), jax.ShapeDtypeStruct((8, D), F32)],
        in_specs=[full, full, full, wide, _resident((VEC_ROWS, D)), _resident((D, 2 * D_FF)), _resident((D_FF, D))],
        out_specs=[full, wide, full, pl.BlockSpec((8, D), lambda i: (0, 0))],
        compiler_params=_cparams())(x1, ffn, target, gu, vec, w_gu, w_down)


def _attn_out_backward(dx1, x, attn, y_sb, y_sw, vec, w_out, name):
    s = x.shape[0]
    tb = min(TOK_TILE, s)

    def body(dx1_ref, x_ref, attn_ref, ysb_ref, ysw_ref, vec_ref, w_ref, du1_ref, dattn_ref, dy_ref, acc_ref):
        @pl.when(pl.program_id(0) == 0)
        def _():
            acc_ref[...] = jnp.zeros_like(acc_ref)

        attn = attn_ref[...]
        g_a = 1.0 + vec_ref[V_G_A:V_G_A + 1, :]
        xhat, rstd = _layer_norm_stats(ALPHA * x_ref[...] + g_a * attn)
        dx1v = dx1_ref[...]
        acc_ref[B_LN1G:B_LN1G + 1, :] += _colsum(dx1v * xhat)
        acc_ref[B_LN1B:B_LN1B + 1, :] += _colsum(dx1v)
        du1 = _layer_norm_bwd(dx1v * vec_ref[V_LN1G:V_LN1G + 1, :], xhat, rstd)
        du1_ref[...] = du1
        acc_ref[B_GA:B_GA + 1, :] += _colsum(du1 * attn)
        dattn = (g_a * du1).astype(BF16)
        dattn_ref[...] = dattn
        dmixed = _dot_nt(dattn, w_ref[...])
        for lo, y_ref in ((0, ysb_ref), (SB_W, ysw_ref)):
            y = y_ref[...]
            rr = _rms_parts(y)
            dn = dmixed[:, lo:lo + SB_W]
            acc_ref[B_GN:B_GN + 1, lo:lo + SB_W] += _colsum(dn * y * rr)
            dng = dn * vec_ref[V_GN:V_GN + 1, lo:lo + SB_W]
            dy_ref[:, lo:lo + SB_W] = rr * dng - y * (rr * rr * rr) * jnp.mean(dng * y, axis=1, keepdims=True)

    half = pl.BlockSpec((tb, SB_W), lambda i: (i, 0))
    full = pl.BlockSpec((tb, D), lambda i: (i, 0))
    return pl.pallas_call(
        body, name=name, grid=(s // tb,),
        out_shape=[jax.ShapeDtypeStruct((s, D), F32), jax.ShapeDtypeStruct((s, D), BF16),
                   jax.ShapeDtypeStruct((s, D), F32), jax.ShapeDtypeStruct((8, D), F32)],
        in_specs=[full, full, full, half, half, _resident((VEC_ROWS, D)), _resident((D, D))],
        out_specs=[full, full, full, pl.BlockSpec((8, D), lambda i: (0, 0))],
        compiler_params=_cparams())(dx1, x, attn, y_sb, y_sw, vec, w_out)


def _sb_backward(proj, sp_total, sweep_start, dy, slabs, name):
    s = proj.shape[0]
    tq, tk = min(SB_TQ, s), min(SB_TK, s)
    r = tq // tk
    nkb = SB_W // LANES
    nq = s // tq

    n_sl = len(slabs)

    def body(q_ref, k_ref, v_ref, tot_ref, do_ref, start_ref, *refs):
        slab_refs, (dq_ref, dk_ref, dv_ref), got_refs = refs[:n_sl], refs[n_sl:n_sl + 3], refs[n_sl + 3:2 * n_sl + 3]
        dq_acc, left_refs, gsum_refs = refs[2 * n_sl + 3:2 * n_sl + 6]
        i = pl.program_id(1)
        step = pl.program_id(0) * nq + i
        scatter = _scatter_exchange(slab_refs, got_refs, *refs[2 * n_sl + 6:])

        @pl.when(step == 0)
        def _():
            scatter.start()

        @pl.when(i == 0)
        def _():
            dk_ref[...] = jnp.zeros_like(dk_ref)
            dv_ref[...] = jnp.zeros_like(dv_ref)

        lane = lax.broadcasted_iota(jnp.int32, (1, LANES), 1)
        first = lane < HEAD_DIM
        qp, dop, totp = q_ref[...], do_ref[...], tot_ref[...]
        zero = jnp.zeros((), BF16)
        qs = (jnp.where(first, qp, zero), jnp.where(first, zero, qp))
        dofs = (jnp.where(first, dop, 0.0), jnp.where(first, 0.0, dop))
        dobs = tuple(d.astype(BF16) for d in dofs)
        dots = tuple(d.T.astype(BF16) for d in dofs)
        qts = tuple(qh.astype(F32).T.astype(BF16) for qh in qs)
        later = _sum_matrix(tk, lambda row, col: row > col)
        earlier = _sum_matrix(tk, lambda row, col: row < col)
        dq_acc[...] = jnp.zeros_like(dq_acc)
        gsum_refs[...] = jnp.zeros_like(gsum_refs)
        swapped = pltpu.roll(totp, HEAD_DIM, 1)
        left_refs[0] = jnp.where(first, totp, swapped)
        left_refs[1] = jnp.where(first, swapped, totp)

        def blocks(tiles):
            rows = [slice(r0, r0 + n) for r0, n, _, _ in tiles]
            kjs = [k_ref[pl.ds(pl.multiple_of(j * tk, tk), tk), :] for _, _, j, _ in tiles]
            vjs = [v_ref[pl.ds(pl.multiple_of(j * tk, tk), tk), :] for _, _, j, _ in tiles]
            chains = [(hd, t) for t in range(len(tiles)) for hd in range(2)]
            zs = [_dot_nt(qs[hd][rows[t]], kjs[t]) for hd, t in chains]
            dws = [_dot_nt(dobs[hd][rows[t]], vjs[t]) for hd, t in chains]
            parts = [_softplus_parts(z) for z in zs]
            sps = [p[0] for p in parts]
            befores = [_before(i * tq + r0, n, j * tk, tk) if diag else None for r0, n, j, diag in tiles]
            spms = [sp if befores[t] is None else jnp.where(befores[t], sp, 0.0) for (hd, t), sp in zip(chains, sps)]
            cums = [_block_sums(spm, later) for spm in spms]
            lefts, gsums, dq_sums = _RowState(left_refs, tk), _RowState(gsum_refs, tk), _RowState(dq_acc, tk, add=True)
            ws = []
            for (hd, t), z, sp, (cum, sm) in zip(chains, zs, sps, cums):
                r0, n = tiles[t][:2]
                left = lefts.get(hd, r0, n) - sm
                lefts.put(hd, r0, n, left)
                w = jnp.exp(z - sp - cum - _across(left, tk))
                ws.append(w if befores[t] is None else jnp.where(befores[t], w, 0.0))
            wbs = [w.astype(BF16) for w in ws]
            dvs = [_dot(dots[hd][:, rows[t]], wb) for (hd, t), wb in zip(chains, wbs)]
            gs = [dw * w for dw, w in zip(dws, ws)]
            gcums = [_block_sums(g, earlier) for g in gs]
            dzbs = []
            for (hd, t), z, (sp, e1), g, (gcum, gsm) in zip(chains, zs, parts, gs, gcums):
                r0, n = tiles[t][:2]
                gsum = gsums.get(hd, r0, n)
                inv = 1.0 / (1.0 + e1)
                sig = jnp.where(z >= 0.0, inv, e1 * inv)
                dz = g - sig * (g + _across(gsum, tk) + gcum)
                dzbs.append((dz if befores[t] is None else jnp.where(befores[t], dz, 0.0)).astype(BF16))
                gsums.put(hd, r0, n, gsum + gsm)
            dqs = [_dot(dzb, kjs[t]) for (hd, t), dzb in zip(chains, dzbs)]
            dks = [_dot(qts[hd][:, rows[t]], dzb) for (hd, t), dzb in zip(chains, dzbs)]
            for t, (_, _, j, _) in enumerate(tiles):
                dv_ref[j] += dvs[2 * t] + dvs[2 * t + 1]
                dk_ref[j] += dks[2 * t] + dks[2 * t + 1]
            for (hd, t), dq in zip(chains, dqs):
                dq_sums.accumulate(hd, *tiles[t][:2], dq)
            dq_sums.store()
            lefts.store()
            gsums.store()

        below = i * r
        start = jnp.clip(start_ref[pl.program_id(0), i].astype(jnp.int32), 0, below)

        def sweep(n, carry):
            blocks([(0, tq, start + n, False)])
            return carry

        lax.fori_loop(0, jnp.maximum(below - 1 - start, 0), sweep, 0)
        diagonal = [(d * tk, tk, below + e, e == d) for d in range(r) for e in range(d + 1)]

        @pl.when(i == 0)
        def _():
            blocks(diagonal)

        @pl.when(i > 0)
        def _():
            blocks([(0, tq, below - 1, False)] + diagonal)
        dq_ref[...] = jnp.where(first, dq_acc[0], dq_acc[1])

        @pl.when(step == nkb * nq - 1)
        def _():
            scatter.wait()

    shp = jax.ShapeDtypeStruct((s, SB_W), F32)
    qspec = pl.BlockSpec((tq, LANES), lambda p, i: (i, p))
    whole = pl.BlockSpec((None, s // tk, LANES, tk), lambda p, i: (p, 0, 0, 0))
    shp_t = jax.ShapeDtypeStruct((nkb, s // tk, LANES, tk), F32)
    hbm = pl.BlockSpec(memory_space=pl.ANY)
    return pl.pallas_call(
        body, name=name, grid=(nkb, nq),
        out_shape=[shp, shp_t, shp_t] + [jax.ShapeDtypeStruct(p.shape, p.dtype) for p in slabs],
        in_specs=[qspec,
                  pl.BlockSpec((s, LANES), lambda p, i: (0, nkb + p)),
                  pl.BlockSpec((s, LANES), lambda p, i: (0, 2 * nkb + p)),
                  qspec, qspec, pl.BlockSpec(memory_space=pltpu.SMEM)] + [hbm] * n_sl,
        out_specs=[qspec, whole, whole] + [hbm] * n_sl,
        scratch_shapes=[pltpu.VMEM((2, tq, LANES), F32), pltpu.VMEM((2, tq, LANES), F32), pltpu.VMEM((2, tq, LANES), F32)]
        + _exchange_sems(n_sl),
        compiler_params=_cparams())(proj, proj, proj, sp_total, dy, sweep_start, *slabs)


def _swa_backward(proj, y_sw, dy, sinks, gives, name):
    s = proj.shape[0]
    nb = s // WINDOW
    qb, kb, vb = 3 * SB_W // SWA_QW, (3 * SB_W + SWA_QW) // LANES, (3 * SB_W + SWA_QW + SWA_KW) // LANES

    n_gv = len(gives)

    def body(q_ref, kp_ref, kc_ref, vp_ref, vc_ref, o_ref, do_ref, sink_ref, *refs):
        give_refs, (dq_ref, dk_ref, dv_ref, ds_ref), got_refs = refs[:n_gv], refs[n_gv:n_gv + 4], refs[n_gv + 4:2 * n_gv + 4]
        n = pl.program_id(0)
        swap = _sibling_halves(give_refs, got_refs, *refs[2 * n_gv + 4:])

        @pl.when(n == 0)
        def _():
            for cp in swap:
                cp.start()

        @pl.when(n == 0)
        def _():
            dk_ref[...] = jnp.zeros_like(dk_ref)
            dv_ref[...] = jnp.zeros_like(dv_ref)
            ds_ref[...] = jnp.zeros_like(ds_ref)

        k = jnp.concatenate([kp_ref[...], kc_ref[...]], axis=0)
        v = jnp.concatenate([vp_ref[...], vc_ref[...]], axis=0)
        k_sw = pltpu.roll(k.astype(F32), HEAD_DIM, 1).astype(BF16)
        v_sw = pltpu.roll(v.astype(F32), HEAD_DIM, 1).astype(BF16)
        lane = lax.broadcasted_iota(jnp.int32, (1, LANES), 1)
        halves = [lane < HEAD_DIM, lane >= HEAD_DIM]
        valid, distf = _swa_masks(n)
        heads = range(2 * 4)
        cols = [slice((h // 2) * LANES, (h // 2 + 1) * LANES) for h in heads]
        qms = [jnp.where(halves[h % 2], q_ref[:, cols[h]], jnp.zeros((), BF16)) for h in heads]
        dos = [jnp.where(halves[h % 2], do_ref[:, cols[h]], 0.0) for h in heads]
        dobs = [d.astype(BF16) for d in dos]
        native = [h // 4 == h % 2 for h in heads]
        kus = [k if native[h] else k_sw for h in heads]
        vus = [v if native[h] else v_sw for h in heads]
        scores = [_dot_nt(qms[h], kus[h]) for h in heads]
        dps = [_dot_nt(dobs[h], vus[h]) for h in heads]
        deltas = [jnp.sum(dos[h] * o_ref[:, cols[h]], axis=1, keepdims=True) for h in heads]
        probs = [_swa_probs(scores[h], valid, distf, h, sink_ref[h]) for h in heads]
        pbs = [probs[h][0].astype(BF16) for h in heads]
        dscs = [(probs[h][0] * (dps[h] - deltas[h])).astype(BF16) for h in heads]
        dqs = [_dot(dscs[h], kus[h]) for h in heads]
        dks = [_dot_tn(dscs[h], qms[h]) for h in heads]
        dvs = [_dot_tn(pbs[h], dobs[h]) for h in heads]
        for h in heads:
            ds_ref[h:h + 1, :] += jnp.zeros((1, LANES), F32) - jnp.sum(probs[h][1] * deltas[h])
        for pair in range(4):
            dq_ref[:, cols[2 * pair]] = jnp.where(halves[0], dqs[2 * pair], dqs[2 * pair + 1])

        def gathered(parts):
            nat = sum(parts[h] for h in heads if native[h])
            rot = sum(parts[h] for h in heads if not native[h])
            return nat + pltpu.roll(rot, HEAD_DIM, 1)

        dk, dv = gathered(dks), gathered(dvs)
        prev = pl.multiple_of(jnp.maximum(n - 1, 0) * WINDOW, WINDOW)
        cur = pl.multiple_of(n * WINDOW, WINDOW)
        dk_ref[pl.ds(prev, WINDOW), :] += dk[:WINDOW]
        dv_ref[pl.ds(prev, WINDOW), :] += dv[:WINDOW]
        dk_ref[pl.ds(cur, WINDOW), :] += dk[WINDOW:]
        dv_ref[pl.ds(cur, WINDOW), :] += dv[WINDOW:]

        @pl.when(n == nb - 1)
        def _():
            for cp in swap:
                cp.wait()

    prev_blk = lambda n: jnp.maximum(n - 1, 0)
    wide = pl.BlockSpec((WINDOW, SWA_QW), lambda n: (n, 0))
    whole = pl.BlockSpec((s, LANES), lambda n: (0, 0))
    hbm = pl.BlockSpec(memory_space=pl.ANY)
    return pl.pallas_call(
        body, name=name, grid=(nb,),
        out_shape=[jax.ShapeDtypeStruct((s, SWA_QW), F32), jax.ShapeDtypeStruct((s, LANES), F32),
                   jax.ShapeDtypeStruct((s, LANES), F32), jax.ShapeDtypeStruct((8, LANES), F32)] + _halves_shapes(gives),
        in_specs=[pl.BlockSpec((WINDOW, SWA_QW), lambda n: (n, qb)),
                  pl.BlockSpec((WINDOW, LANES), lambda n: (prev_blk(n), kb)),
                  pl.BlockSpec((WINDOW, LANES), lambda n: (n, kb)),
                  pl.BlockSpec((WINDOW, LANES), lambda n: (prev_blk(n), vb)),
                  pl.BlockSpec((WINDOW, LANES), lambda n: (n, vb)),
                  wide,
                  pl.BlockSpec((WINDOW, SWA_QW), lambda n: (n, 1)),
                  pl.BlockSpec(memory_space=pltpu.SMEM)] + [hbm] * n_gv,
        out_specs=[wide, whole, whole, pl.BlockSpec((8, LANES), lambda n: (0, 0))] + [hbm] * n_gv,
        scratch_shapes=_halves_sems(n_gv),
        compiler_params=_cparams())(proj, proj, proj, proj, proj, y_sw, dy, sinks, *gives)


def _in_proj_backward(dq_sb, dkt_sb, dvt_sb, dq_sw, dk_sw, dv_sw, du1, x, vec, w_in, name):
    s = x.shape[0]
    tb = min(TOK_TILE, s)
    n_pairs, _, _, tk = dkt_sb.shape

    def body(dqsb_ref, dktsb_ref, dvtsb_ref, dqsw_ref, dksw_ref, dvsw_ref, du1_ref, x_ref, vec_ref, w_ref,
             dproj_ref, gx_ref, acc_ref, bacc_ref):
        @pl.when(pl.program_id(0) == 0)
        def _():
            acc_ref[...] = jnp.zeros_like(acc_ref)
            bacc_ref[...] = jnp.zeros_like(bacc_ref)

        pieces = ((0, dqsb_ref, QK_SCALE), (3 * SB_W, dqsw_ref, QK_SCALE), (3 * SB_W + SWA_QW, dksw_ref, 1.0),
                  (3 * SB_W + SWA_QW + SWA_KW, dvsw_ref, 1.0))
        for lo, ref, scale in pieces:
            width = ref.shape[1]
            piece = ref[...] * scale
            bacc_ref[0:1, lo:lo + width] += _colsum(piece)
            dproj_ref[:, lo:lo + width] = piece.astype(BF16)
        for base, ref in ((SB_W, dktsb_ref), (2 * SB_W, dvtsb_ref)):
            for p in range(n_pairs):
                lo = base + p * LANES
                for jj in range(tb // tk):
                    piece = ref[p, jj].T
                    bacc_ref[0:1, lo:lo + LANES] += _colsum(piece)
                    dproj_ref[jj * tk:(jj + 1) * tk, lo:lo + LANES] = piece.astype(BF16)
        dh = _dot_nt(dproj_ref[...], w_ref[...])
        xv = x_ref[...]
        gx_ref[...] = ALPHA * du1_ref[...] + dh * (1.0 + vec_ref[V_SC_A:V_SC_A + 1, :])
        acc_ref[C_SCA:C_SCA + 1, :] += _colsum(dh * xv)
        acc_ref[C_SHA:C_SHA + 1, :] += _colsum(dh)

    half = pl.BlockSpec((tb, SB_W), lambda i: (i, 0))
    narrow = pl.BlockSpec((tb, LANES), lambda i: (i, 0))
    full = pl.BlockSpec((tb, D), lambda i: (i, 0))
    blocks_t = pl.BlockSpec((n_pairs, tb // tk, LANES, tk), lambda i: (0, i, 0, 0))
    return pl.pallas_call(
        body, name=name, grid=(s // tb,),
        out_shape=[jax.ShapeDtypeStruct((s, D_IN), BF16), jax.ShapeDtypeStruct((s, D), F32),
                   jax.ShapeDtypeStruct((8, D), F32), jax.ShapeDtypeStruct((8, D_IN), F32)],
        in_specs=[half, blocks_t, blocks_t, half, narrow, narrow, full, full, _resident((VEC_ROWS, D)),
                  _resident((D, D_IN))],
        out_specs=[pl.BlockSpec((tb, D_IN), lambda i: (i, 0)), full, pl.BlockSpec((8, D), lambda i: (0, 0)),
                   pl.BlockSpec((8, D_IN), lambda i: (0, 0))],
        compiler_params=_cparams())(dq_sb, dkt_sb, dvt_sb, dq_sw, dk_sw, dv_sw, du1, x, vec, w_in)


def _weight_grad(at, b, name, col_shards=1):
    m, s = at.shape
    n = b.shape[1]
    if col_shards > 1:
        tn = n // col_shards
        out_shape = jax.ShapeDtypeStruct((col_shards, m, tn), F32)
        out_spec = pl.BlockSpec((None, m, tn), lambda j, k: (j, 0, 0))
    else:
        tn = 512 if n % 512 == 0 else n
        out_shape = jax.ShapeDtypeStruct((m, n), F32)
        out_spec = pl.BlockSpec((m, tn), lambda j, k: (0, j))
    ts = min(WGRAD_TOKENS, s)
    while 2 * (m * ts * 2 + ts * tn * 2 + m * tn * 4) > WGRAD_VMEM and ts > 512:
        ts //= 2

    def body(at_ref, b_ref, o_ref):
        @pl.when(pl.program_id(1) == 0)
        def _():
            o_ref[...] = jnp.zeros_like(o_ref)

        o_ref[...] += _dot(at_ref[...], b_ref[...])

    return pl.pallas_call(
        body, name=name, grid=(n // tn, s // ts),
        out_shape=out_shape,
        in_specs=[pl.BlockSpec((m, ts), lambda j, k: (0, k)), pl.BlockSpec((ts, tn), lambda j, k: (k, j))],
        out_specs=out_spec,
        compiler_params=_cparams())(at, b)


def _pad_rows(v, rows):
    return jnp.concatenate([v, jnp.zeros((rows - v.shape[0], v.shape[1]), v.dtype)], axis=0)


def _col_shards(w, n_shards):
    r, n = w.shape
    return w.reshape(r, n_shards, n // n_shards).transpose(1, 0, 2)


def kernel(x, c, w_ada, b_ada, w_in, b_in, sinks, gn_sb, gn_swa, w_out, ln1_g, ln1_b, w_gu, w_down, ln2_g, ln2_b, loss_target, m_w_ada, m_b_ada, m_w_in, m_b_in, m_sinks, m_gn_sb, m_gn_swa, m_w_out, m_ln1_g, m_ln1_b, m_w_gu, m_w_down, m_ln2_g, m_ln2_b, v_w_ada, v_b_ada, v_w_in, v_b_in, v_sinks, v_gn_sb, v_gn_swa, v_w_out, v_ln1_g, v_ln1_b, v_w_gu, v_w_down, v_ln2_g, v_ln2_b):
    ix, iy, ic = lax.axis_index("x"), lax.axis_index("y"), lax.axis_index("c")
    chip = 2 * ix + iy
    dev = 4 * ix + 2 * iy + ic
    xs, target = x[0], loss_target[0]
    s = xs.shape[0]

    c_rows, g_in = _allgather8(_pad_rows(c, 8), "gather_c", gather=[w_in[0].astype(BF16)])
    c_all = c_rows[::8]
    n_ada = w_ada.shape[2]
    b_ada_shard = lax.dynamic_slice_in_dim(b_ada, chip * n_ada, n_ada, axis=1)
    mod_cols, silu_c = _mod_shard(c_all, w_ada[0], b_ada_shard, "mod_shard")
    mod_all = _allgather8(mod_cols, "gather_mod")[0].reshape(N_DEV, 8, n_ada)
    mod_mine = lax.dynamic_index_in_dim(mod_all, dev, axis=1, keepdims=False)
    mod = mod_mine.reshape(N_CHIPS, 2, n_ada)[:, 0].reshape(6, D)
    vec = jnp.concatenate([mod, ln1_g, ln1_b, ln2_g, ln2_b, jnp.concatenate([gn_sb, gn_swa], axis=1),
                           jnp.zeros((VEC_ROWS - 11, D), F32)], axis=0)

    w_in_b = g_in.transpose(1, 0, 2).reshape(D, D_IN)

    h_t, proj = _in_proj(xs, vec, w_in_b, b_in, "in_proj")
    y_sb, sp_total, sweep_start, g_out, g_gu = _sb_forward(
        proj, [w_out[0].astype(BF16), w_gu[0].astype(BF16)], "sb_forward")
    w_gu_b = g_gu.transpose(1, 0, 2).reshape(D, 2 * D_FF)
    w_out_b = g_out.reshape(D, D)
    sink_vec = sinks[0]
    y_sw, g_down = _swa_forward(proj, sink_vec, [w_down[0].astype(BF16)], "swa_forward")
    w_down_b = g_down.reshape(D_FF, D)
    mixed_t, attn, x1, h2_b, h2_t = _post_attention(y_sb, y_sw, xs, vec, w_out_b, "post_attention")
    gu, act_t, ffn = _ffn_forward(h2_b, w_gu_b, w_down_b, "ffn_forward")

    def in_halves(shards):
        n_sh, rows, cols = shards.shape
        return shards.reshape(n_sh, 2, rows // 2, cols)

    core = ic.reshape(1).astype(jnp.int32)
    dffn_b, dgu_b, dx1, acc_f = _ffn_backward(x1, ffn, target, gu, vec, w_gu_b, w_down_b, "ffn_backward")
    dw_gu = _weight_grad(h2_t, dgu_b, "grad_w_gu", col_shards=4)
    dw_down = _weight_grad(act_t, dffn_b, "grad_w_down")
    du1, dattn_b, dy, acc_a = _attn_out_backward(dx1, xs, attn, y_sb, y_sw, vec, w_out_b, "attn_out_backward")
    dw_out = _weight_grad(mixed_t, dattn_b, "grad_w_out")
    first = [in_halves(dw_gu), in_halves(dw_down.reshape(4, D_FF // 4, D)), in_halves(dw_out.reshape(4, D // 4, D))]
    dq_sw, dk_sw, dv_sw, dsink, *got_first = _swa_backward(proj, y_sw, dy, sink_vec, first, "swa_backward")
    sums_first = _chip_sums(first, got_first, core, "grad_chip_sums")
    dq_sb, dk_sb, dv_sb, *parts_first = _sb_backward(proj, sp_total, sweep_start, dy, sums_first, "sb_backward")
    dproj_b, grad_x, acc_i, acc_b = _in_proj_backward(dq_sb, dk_sb, dv_sb, dq_sw, dk_sw, dv_sw, du1, xs, vec, w_in_b,
                                                      "in_proj_backward")
    dw_in = _weight_grad(h_t, dproj_b, "grad_w_in")
    last = [in_halves(_col_shards(dw_in, 4))]
    sums_last = _chip_sums(last, _halves_swap(last, "grad_halves_swap_in"), core, "grad_chip_sum_in")

    dmod = jnp.concatenate([acc_i[C_SHA:C_SHA + 1], acc_i[C_SCA:C_SCA + 1], acc_a[B_GA:B_GA + 1],
                            acc_f[A_SHF:A_SHF + 1], acc_f[A_SCF:A_SCF + 1], acc_f[A_GF:A_GF + 1]], axis=1)
    dsink_row = jnp.concatenate([dsink[:, 0].reshape(1, 8), jnp.zeros((1, LANES - 8), F32)], axis=1)
    loss_row = jnp.concatenate([jnp.sum(acc_f[A_LOSS:A_LOSS + 1], axis=1, keepdims=True),
                                jnp.zeros((1, LANES - 1), F32)], axis=1)
    small = jnp.concatenate([dmod, acc_b[0:1], acc_a[B_LN1G:B_LN1G + 1], acc_a[B_LN1B:B_LN1B + 1],
                             acc_f[A_LN2G:A_LN2G + 1], acc_f[A_LN2B:A_LN2B + 1], acc_a[B_GN:B_GN + 1],
                             dsink_row, loss_row], axis=1)
    small_rows, *parts_last = _allgather8(_pad_rows(small, 8), "gather_small", scatter=sums_last)
    small_all = small_rows[::8]

    mine = _sum4s([*parts_first, *parts_last], "grad_reduce")
    theirs = _sibling_send(mine, "grad_half_return")
    gw_gu, gw_down, gw_out, gw_in = [
        jnp.concatenate([jnp.where(ic == 0, m_, t_), jnp.where(ic == 0, t_, m_)], axis=0) for m_, t_ in zip(mine, theirs)]

    small_names = ["b_ada", "b_in", "ln1_g", "ln1_b", "ln2_g", "ln2_b", "gn_sb", "gn_swa", "sinks"]
    small_at = [SM_MOD, SM_BIN, SM_LN1G, SM_LN1B, SM_LN2G, SM_LN2B, SM_GN, SM_GN + SB_W, SM_SINK]
    *small_out, loss_row_all = _small_update(
        small_all, small_at,
        [b_ada, b_in, ln1_g, ln1_b, ln2_g, ln2_b, gn_sb, gn_swa, sinks],
        [m_b_ada, m_b_in, m_ln1_g, m_ln1_b, m_ln2_g, m_ln2_b, m_gn_sb, m_gn_swa, m_sinks],
        [v_b_ada, v_b_in, v_ln1_g, v_ln1_b, v_ln2_g, v_ln2_b, v_gn_sb, v_gn_swa, v_sinks], SM_LOSS, "small_update")
    g_small, d_small, m2_small, v2_small = [dict(zip(small_names, leaves)) for leaves in small_out]
    loss = loss_row_all[0, 0]

    dmod_cols = lax.dynamic_slice_in_dim(small_all[:, SM_MOD:SM_BIN], chip * n_ada, n_ada, axis=1)
    gw_ada = _weight_grad(_pad_rows(silu_c, LANES).astype(BF16).T, _pad_rows(dmod_cols, LANES).astype(BF16), "grad_w_ada")

    big = {}
    for nm, w, g, m, v in (("w_ada", w_ada, gw_ada, m_w_ada, v_w_ada), ("w_in", w_in, gw_in, m_w_in, v_w_in),
                           ("w_out", w_out, gw_out, m_w_out, v_w_out), ("w_gu", w_gu, gw_gu, m_w_gu, v_w_gu),
                           ("w_down", w_down, gw_down, m_w_down, v_w_down)):
        d_, m2_, v2_ = _adamw(w[0], g, m[0], v[0], "adamw_" + nm)
        big[nm] = (g[None], d_[None], m2_[None], v2_[None])

    order = ["w_ada", "b_ada", "w_in", "b_in", "sinks", "gn_sb", "gn_swa", "w_out", "ln1_g", "ln1_b", "w_gu", "w_down",
             "ln2_g", "ln2_b"]

    def leaf(nm, which):
        if nm in big:
            return big[nm][which]
        return (g_small, d_small, m2_small, v2_small)[which][nm]

    outs = [loss, grad_x[None]]
    for which in range(4):
        outs += [leaf(nm, which) for nm in order]
    return tuple(outs)
```

```python
import math

import jax
import jax.numpy as jnp
from jax import lax
from jax.experimental import pallas as pl
from jax.experimental.pallas import tpu as pltpu

F32 = jnp.float32
BF16 = jnp.bfloat16

D = 1024
HEAD_DIM = 64
SB_W = 512
SWA_QW = 512
SWA_KW = 128
D_IN = 2304
D_FF = 2816
WINDOW = 128
ALPHA = 2.0 ** 0.25
LN_EPS = 1e-5
RMS_EPS = 1e-6
MASK_VALUE = -1e30
QK_SCALE = 1.0 / math.sqrt(HEAD_DIM)

ADAM_LR = 0.001
ADAM_B1 = 0.9
ADAM_B2 = 0.999
ADAM_EPS = 1e-08
ADAM_WD = 0.01
ADAM_STEP = 10

N_CHIPS = 4
N_DEV = 8
LANES = 128

SB_TQ = 512
SB_TK = 256
SB_DEAD_MASS = 110.0
TOK_TILE = 512
FFN_TILE = 256
FFN_BWD_TILE = 256
VMEM_LIMIT = 56 * 1024 * 1024
WGRAD_TOKENS = 2048
WGRAD_VMEM = 40 * 1024 * 1024

V_SH_A, V_SC_A, V_G_A, V_SH_F, V_SC_F, V_G_F, V_LN1G, V_LN1B, V_LN2G, V_LN2B, V_GN = range(11)
VEC_ROWS = 16

SM_MOD = 0
SM_BIN = 6 * D
SM_LN1G = SM_BIN + D_IN
SM_LN1B = SM_LN1G + D
SM_LN2G = SM_LN1B + D
SM_LN2B = SM_LN2G + D
SM_GN = SM_LN2B + D
SM_SINK = SM_GN + D
SM_LOSS = SM_SINK + LANES
SM_LEN = SM_LOSS + LANES

MESH = pl.DeviceIdType.MESH


def _cparams(**kw):
    return pltpu.CompilerParams(vmem_limit_bytes=VMEM_LIMIT, **kw)


def _resident(shape):
    nd = len(shape)
    return pl.BlockSpec(shape, lambda *_: (0,) * nd, pipeline_mode=pl.Buffered(1))


def _dot(a, b):
    return jnp.dot(a, b, preferred_element_type=F32)


def _dot_nt(a, b):
    return lax.dot_general(a, b, (((1,), (1,)), ((), ())), preferred_element_type=F32)


def _dot_tn(a, b):
    return lax.dot_general(a, b, (((0,), (0,)), ((), ())), preferred_element_type=F32)


def _sum_matrix(tk, keep):
    row = lax.broadcasted_iota(jnp.int32, (tk, tk + LANES), 0)
    col = lax.broadcasted_iota(jnp.int32, (tk, tk + LANES), 1)
    return (keep(row, col) | (col >= tk)).astype(BF16)


def _block_sums(x, m):
    tk = x.shape[1]
    res = _dot(x.astype(BF16), m)
    return res[:, :tk], res[:, tk:]


def _before(t0, n, s0, tk):
    return s0 + lax.broadcasted_iota(jnp.int32, (n, tk), 1) < t0 + lax.broadcasted_iota(jnp.int32, (n, tk), 0)


class _RowState:
    def __init__(self, ref, tk, add=False):
        self.ref, self.tk, self.add, self.vals = ref, tk, add, {}

    def _blocks(self, r0, n):
        return range(r0 // self.tk, (r0 + n) // self.tk)

    def get(self, hd, r0, n):
        for d in self._blocks(r0, n):
            if (hd, d) not in self.vals:
                self.vals[(hd, d)] = self.ref[hd, d * self.tk:(d + 1) * self.tk, :]
        parts = [self.vals[(hd, d)] for d in self._blocks(r0, n)]
        return parts[0] if len(parts) == 1 else jnp.concatenate(parts, axis=0)

    def put(self, hd, r0, n, val):
        for k, d in enumerate(self._blocks(r0, n)):
            self.vals[(hd, d)] = val[k * self.tk:(k + 1) * self.tk]

    def accumulate(self, hd, r0, n, val):
        for k, d in enumerate(self._blocks(r0, n)):
            part = val[k * self.tk:(k + 1) * self.tk]
            self.vals[(hd, d)] = part if (hd, d) not in self.vals else self.vals[(hd, d)] + part

    def store(self):
        for (hd, d), val in self.vals.items():
            span = slice(d * self.tk, (d + 1) * self.tk)
            if self.add:
                self.ref[hd, span, :] += val
            else:
                self.ref[hd, span, :] = val


def _across(v, tk):
    return jnp.concatenate([v] * (tk // LANES), axis=1)


def _allgather8(v, name, gather=(), scatter=()):
    m_per, n = v.shape
    n_g, n_s = len(gather), len(scatter)

    def body(x_ref, *refs):
        g_in, s_in = refs[:n_g], refs[n_g:n_g + n_s]
        out_ref = refs[n_g + n_s]
        g_out, s_out = refs[n_g + n_s + 1:2 * n_g + n_s + 1], refs[2 * n_g + n_s + 1:2 * (n_g + n_s) + 1]
        send_sems, recv_sems, local_sem, *more_sems = refs[2 * (n_g + n_s) + 1:]
        halves = _HalvesGather(g_in, g_out, *more_sems[:5]) if n_g else None
        beside = ([halves] if n_g else []) + ([_scatter_exchange(s_in, s_out, *more_sems[-3:])] if n_s else [])
        for ex in beside:
            ex.start()
        x, y, c = lax.axis_index("x"), lax.axis_index("y"), lax.axis_index("c")
        me, sibling = (x, y, c), (x, y, 1 - c)
        chips = [(1 - x, y), (x, 1 - y), (1 - x, 1 - y)]

        def rows(px, py, pc):
            return out_ref.at[pl.ds((4 * px + 2 * py + pc) * m_per, m_per), :]

        def copy(k, block, to, src=None):
            return pltpu.make_async_remote_copy(
                src_ref=rows(*block) if src is None else src, dst_ref=rows(*block),
                send_sem=send_sems.at[k], recv_sem=recv_sems.at[k], device_id=to, device_id_type=MESH)

        mine = pltpu.make_async_copy(x_ref, rows(*me), local_sem)
        mine.start()
        first = [copy(0, me, sibling, src=x_ref)]
        first += [copy(1 + j, me, (*chip, c), src=x_ref) for j, chip in enumerate(chips)]
        for cp in first:
            cp.start()
        passed = [copy(4 + j, (*chip, c), sibling) for j, chip in enumerate(chips)]
        for j, chip in enumerate(chips):
            copy(1 + j, (*chip, c), me).wait_recv()
            passed[j].start()
        copy(0, sibling, me).wait_recv()
        for j, chip in enumerate(chips):
            copy(4 + j, (*chip, 1 - c), me).wait_recv()
        for cp in first + passed:
            cp.wait_send()
        mine.wait()
        if halves is not None:
            halves.forward()
        for ex in beside:
            ex.wait()

    hbm = pl.BlockSpec(memory_space=pl.ANY)
    return pl.pallas_call(
        body, name=name,
        out_shape=[jax.ShapeDtypeStruct((N_DEV * m_per, n), v.dtype)]
        + [jax.ShapeDtypeStruct((N_CHIPS,) + a.shape, a.dtype) for a in gather]
        + [jax.ShapeDtypeStruct(p.shape, p.dtype) for p in scatter],
        in_specs=[pl.BlockSpec(memory_space=pltpu.VMEM)] + [hbm] * (n_g + n_s),
        out_specs=[pl.BlockSpec(memory_space=pltpu.VMEM)] + [hbm] * (n_g + n_s),
        scratch_shapes=[pltpu.SemaphoreType.DMA((7,)), pltpu.SemaphoreType.DMA((7,)), pltpu.SemaphoreType.DMA]
        + (_halves_gather_sems(n_g) if n_g else [])
        + (_exchange_sems(n_s) if n_s else []),
        compiler_params=_cparams(),
    )(v, *gather, *scatter)


class _Exchange:
    def __init__(self, local, sends, arrivals):
        self.local, self.sends, self.arrivals = local, sends, arrivals

    def start(self):
        for cp in self.local + self.sends:
            cp.start()

    def wait(self):
        for cp in self.arrivals:
            cp.wait_recv()
        for cp in self.sends:
            cp.wait_send()
        for cp in self.local:
            cp.wait()


def _exchange_sems(n):
    return [pltpu.SemaphoreType.DMA((3 * n,)), pltpu.SemaphoreType.DMA((3 * n,)), pltpu.SemaphoreType.DMA((n,))]


class _HalvesGather:
    def __init__(self, ins, outs, far_send, far_recv, near_send, near_recv, local_sems):
        x, y, c = lax.axis_index("x"), lax.axis_index("y"), lax.axis_index("c")
        slot = 2 * x + y
        chips = [(1 - x, y), (x, 1 - y), (1 - x, 1 - y)]
        self.local, self.far, self.landed, self.near, self.passed = [], [], [], [], []
        for a in range(len(ins)):
            h = ins[a].shape[0] // 2
            mine, theirs = pl.ds(c * h, h), pl.ds((1 - c) * h, h)
            self.local.append(pltpu.make_async_copy(ins[a], outs[a].at[slot], local_sems.at[a]))
            for j, (px, py) in enumerate(chips):
                k, there = 3 * a + j, 2 * px + py
                far = dict(send_sem=far_send.at[k], recv_sem=far_recv.at[k], device_id=(px, py, c), device_id_type=MESH)
                near = dict(send_sem=near_send.at[k], recv_sem=near_recv.at[k], device_id=(x, y, 1 - c),
                            device_id_type=MESH)
                self.far.append(pltpu.make_async_remote_copy(
                    src_ref=ins[a].at[mine], dst_ref=outs[a].at[slot, mine], **far))
                self.landed.append(pltpu.make_async_remote_copy(
                    src_ref=ins[a].at[mine], dst_ref=outs[a].at[there, mine], **far))
                self.near.append(pltpu.make_async_remote_copy(
                    src_ref=outs[a].at[there, mine], dst_ref=outs[a].at[there, mine], **near))
                self.passed.append(pltpu.make_async_remote_copy(
                    src_ref=outs[a].at[there, mine], dst_ref=outs[a].at[there, theirs], **near))

    def start(self):
        for cp in self.local + self.far:
            cp.start()

    def forward(self):
        for landed, near in zip(self.landed, self.near):
            landed.wait_recv()
            near.start()

    def wait(self):
        for cp in self.passed:
            cp.wait_recv()
        for cp in self.far + self.near:
            cp.wait_send()
        for cp in self.local:
            cp.wait()


def _halves_gather_sems(n):
    return [pltpu.SemaphoreType.DMA((3 * n,))] * 4 + [pltpu.SemaphoreType.DMA((n,))]


def _scatter_exchange(p_refs, out_refs, send_sems, recv_sems, local_sems):
    x, y, c = lax.axis_index("x"), lax.axis_index("y"), lax.axis_index("c")
    slot = 2 * x + y
    chips = [(1 - x, y), (x, 1 - y), (1 - x, 1 - y)]
    local, sends, arrivals = [], [], []
    for a, (p_ref, out_ref) in enumerate(zip(p_refs, out_refs)):
        local.append(pltpu.make_async_copy(p_ref.at[slot], out_ref.at[slot], local_sems.at[a]))
        for j, (px, py) in enumerate(chips):
            sems = dict(send_sem=send_sems.at[3 * a + j], recv_sem=recv_sems.at[3 * a + j],
                        device_id=(px, py, c), device_id_type=MESH)
            sends.append(pltpu.make_async_remote_copy(src_ref=p_ref.at[2 * px + py], dst_ref=out_ref.at[slot], **sems))
            arrivals.append(pltpu.make_async_remote_copy(src_ref=p_ref.at[slot], dst_ref=out_ref.at[2 * px + py], **sems))
    return _Exchange(local, sends, arrivals)


def _sibling_halves(give_refs, got_refs, send_sems, recv_sems):
    x, y, c = lax.axis_index("x"), lax.axis_index("y"), lax.axis_index("c")
    copies = []
    for a, (give_ref, got_ref) in enumerate(zip(give_refs, got_refs)):
        for s in range(N_CHIPS):
            copies.append(pltpu.make_async_remote_copy(
                src_ref=give_ref.at[s, 1 - c], dst_ref=got_ref.at[s], send_sem=send_sems.at[N_CHIPS * a + s],
                recv_sem=recv_sems.at[N_CHIPS * a + s], device_id=(x, y, 1 - c), device_id_type=MESH))
    return copies


def _halves_shapes(arrs):
    return [jax.ShapeDtypeStruct((a.shape[0],) + a.shape[2:], a.dtype) for a in arrs]


def _halves_sems(n):
    return [pltpu.SemaphoreType.DMA((N_CHIPS * n,)), pltpu.SemaphoreType.DMA((N_CHIPS * n,))]


def _halves_swap(arrs, name):
    n = len(arrs)

    def body(*refs):
        copies = _sibling_halves(refs[:n], refs[n:2 * n], *refs[2 * n:])
        for cp in copies:
            cp.start()
        for cp in copies:
            cp.wait()

    hbm = pl.BlockSpec(memory_space=pl.ANY)
    return pl.pallas_call(body, name=name, out_shape=_halves_shapes(arrs), in_specs=[hbm] * n, out_specs=[hbm] * n,
                          scratch_shapes=_halves_sems(n), compiler_params=_cparams())(*arrs)


def _sibling_send(arrs, name):
    n = len(arrs)

    def body(*refs):
        x, y, c = lax.axis_index("x"), lax.axis_index("y"), lax.axis_index("c")
        send_sems, recv_sems = refs[2 * n:]
        copies = [pltpu.make_async_remote_copy(src_ref=refs[a], dst_ref=refs[n + a], send_sem=send_sems.at[a],
                                               recv_sem=recv_sems.at[a], device_id=(x, y, 1 - c), device_id_type=MESH)
                  for a in range(n)]
        for cp in copies:
            cp.start()
        for cp in copies:
            cp.wait()

    hbm = pl.BlockSpec(memory_space=pl.ANY)
    return pl.pallas_call(
        body, name=name, out_shape=[jax.ShapeDtypeStruct(a.shape, a.dtype) for a in arrs],
        in_specs=[hbm] * n, out_specs=[hbm] * n,
        scratch_shapes=[pltpu.SemaphoreType.DMA((n,)), pltpu.SemaphoreType.DMA((n,))],
        compiler_params=_cparams(),
    )(*arrs)


def _row_tile(h):
    return h // 2 if (h // 2) % 8 == 0 else h


def _row_tiles(hs):
    tiles = [_row_tile(h) for h in hs]
    assert len({h // t for h, t in zip(hs, tiles)}) == 1
    return tiles, hs[0] // tiles[0]


def _chip_sums(arrs, gots, core, name):
    n = len(arrs)
    tiles, steps = _row_tiles([a.shape[2] for a in arrs])

    def body(core_ref, *refs):
        for a_ref, b_ref, o_ref in zip(refs[:n], refs[n:2 * n], refs[2 * n:]):
            o_ref[...] = a_ref[...] + b_ref[...]

    slabs = [pl.BlockSpec((None, tr, a.shape[3]), lambda s, i, core_ref: (s, i, 0)) for a, tr in zip(arrs, tiles)]
    grid_spec = pltpu.PrefetchScalarGridSpec(
        num_scalar_prefetch=1, grid=(N_CHIPS, steps),
        in_specs=[pl.BlockSpec((None, None, tr, a.shape[3]), lambda s, i, core_ref: (s, core_ref[0], i, 0))
                  for a, tr in zip(arrs, tiles)] + slabs,
        out_specs=slabs)
    return pl.pallas_call(body, name=name, grid_spec=grid_spec,
                          out_shape=[jax.ShapeDtypeStruct(g.shape, g.dtype) for g in gots],
                          compiler_params=_cparams())(core, *arrs, *gots)


def _sum4s(ps, name):
    tiles, steps = _row_tiles([p.shape[1] for p in ps])

    def body(*refs):
        for p_ref, o_ref in zip(refs[:len(ps)], refs[len(ps):]):
            o_ref[...] = ((p_ref[0] + p_ref[1]) + p_ref[2]) + p_ref[3]

    return pl.pallas_call(
        body, name=name, grid=(steps,), out_shape=[jax.ShapeDtypeStruct(p.shape[1:], p.dtype) for p in ps],
        in_specs=[pl.BlockSpec((4, tr, p.shape[2]), lambda i: (0, i, 0)) for p, tr in zip(ps, tiles)],
        out_specs=[pl.BlockSpec((tr, p.shape[2]), lambda i: (i, 0)) for p, tr in zip(ps, tiles)],
        compiler_params=_cparams())(*ps)


def _adam_math(w, g, m, v):
    m2 = ADAM_B1 * m + (1.0 - ADAM_B1) * g
    v2 = ADAM_B2 * v + (1.0 - ADAM_B2) * (g * g)
    m_hat = m2 / (1.0 - ADAM_B1 ** ADAM_STEP)
    v_hat = v2 / (1.0 - ADAM_B2 ** ADAM_STEP)
    delta = -ADAM_LR * (m_hat / (jnp.sqrt(v_hat) + ADAM_EPS) + ADAM_WD * w)
    return delta, m2, v2


def _adamw(w, g, m, v, name):
    rows, cols = w.shape
    tr = rows // 4 if rows % 32 == 0 else rows

    def body(w_ref, g_ref, m_ref, v_ref, d_ref, m2_ref, v2_ref):
        delta, m2, v2 = _adam_math(w_ref[...], g_ref[...], m_ref[...], v_ref[...])
        d_ref[...] = delta
        m2_ref[...] = m2
        v2_ref[...] = v2

    spec = pl.BlockSpec((tr, cols), lambda i: (i, 0))
    shp = jax.ShapeDtypeStruct(w.shape, F32)
    return pl.pallas_call(body, name=name, grid=(rows // tr,), out_shape=[shp, shp, shp],
                          in_specs=[spec] * 4, out_specs=[spec] * 3, compiler_params=_cparams())(w, g, m, v)


def _small_update(g8, offsets, ws, ms, vs, loss_at, name):
    k = len(ws)

    def summed(g8_ref, lo, width):
        g = g8_ref[0:1, lo:lo + width]
        for r in range(1, N_DEV):
            g = g + g8_ref[r:r + 1, lo:lo + width]
        return g

    def body(g8_ref, *refs):
        ins, outs = refs[:3 * k], refs[3 * k:]
        for j in range(k):
            g = summed(g8_ref, offsets[j], ws[j].shape[1])
            delta, m2, v2 = _adam_math(ins[j][...], g, ins[k + j][...], ins[2 * k + j][...])
            for kind, val in enumerate((g, delta, m2, v2)):
                outs[kind * k + j][...] = val
        outs[4 * k][...] = summed(g8_ref, loss_at, LANES)

    vm = pl.BlockSpec(memory_space=pltpu.VMEM)
    shapes = [jax.ShapeDtypeStruct(w.shape, F32) for w in ws] * 4 + [jax.ShapeDtypeStruct((1, LANES), F32)]
    res = pl.pallas_call(body, name=name, out_shape=shapes, in_specs=[vm] * (1 + 3 * k), out_specs=[vm] * (4 * k + 1),
                         compiler_params=_cparams())(g8, *ws, *ms, *vs)
    return res[:k], res[k:2 * k], res[2 * k:3 * k], res[3 * k:4 * k], res[4 * k]


def _mod_shard(c8, w_ada, b_ada_shard, name):
    n = w_ada.shape[1]
    tn = 512

    def body(c_ref, w_ref, b_ref, o_ref, s_ref):
        cv = c_ref[...]
        sc = cv * (1.0 / (1.0 + jnp.exp(-cv)))
        s_ref[...] = sc
        o_ref[...] = _dot(sc.astype(BF16), w_ref[...].astype(BF16)) + b_ref[...]

    return pl.pallas_call(
        body, name=name, grid=(n // tn,),
        out_shape=[jax.ShapeDtypeStruct((8, n), F32), jax.ShapeDtypeStruct((8, D), F32)],
        in_specs=[pl.BlockSpec((8, D), lambda j: (0, 0)), pl.BlockSpec((D, tn), lambda j: (0, j)),
                  pl.BlockSpec((1, tn), lambda j: (0, j))],
        out_specs=[pl.BlockSpec((8, tn), lambda j: (0, j)), pl.BlockSpec((8, D), lambda j: (0, 0))],
        compiler_params=_cparams())(c8, w_ada, b_ada_shard)


def _layer_norm_stats(u):
    mu = jnp.mean(u, axis=1, keepdims=True)
    d = u - mu
    var = jnp.mean(d * d, axis=1, keepdims=True)
    rstd = lax.rsqrt(var + LN_EPS)
    return d * rstd, rstd


def _in_proj(x, vec, w_in, b_in, name):
    s = x.shape[0]
    tb = min(TOK_TILE, s)

    def body(x_ref, vec_ref, w_ref, b_ref, ht_ref, p_ref):
        h = x_ref[...] * (1.0 + vec_ref[V_SC_A:V_SC_A + 1, :]) + vec_ref[V_SH_A:V_SH_A + 1, :]
        hb = h.astype(BF16)
        ht_ref[...] = h.T.astype(BF16)
        proj = _dot(hb, w_ref[...]) + b_ref[...]
        col = lax.broadcasted_iota(jnp.int32, (1, D_IN), 1)
        is_q = (col < SB_W) | ((col >= 3 * SB_W) & (col < 3 * SB_W + SWA_QW))
        p_ref[...] = (proj * jnp.where(is_q, QK_SCALE, 1.0)).astype(BF16)

    return pl.pallas_call(
        body, name=name, grid=(s // tb,),
        out_shape=[jax.ShapeDtypeStruct((D, s), BF16), jax.ShapeDtypeStruct((s, D_IN), BF16)],
        in_specs=[pl.BlockSpec((tb, D), lambda i: (i, 0)), _resident((VEC_ROWS, D)), _resident((D, D_IN)),
                  _resident((1, D_IN))],
        out_specs=[pl.BlockSpec((D, tb), lambda i: (0, i)), pl.BlockSpec((tb, D_IN), lambda i: (i, 0))],
        compiler_params=_cparams())(x, vec, w_in, b_in)


def _softplus_parts(z):
    e1 = jnp.exp(-jnp.abs(z))
    sp = jnp.maximum(z, 0.0) + jnp.log(1.0 + e1)
    return sp, e1


def _sb_forward(proj, shards, name):
    s = proj.shape[0]
    tq, tk = min(SB_TQ, s), min(SB_TK, s)
    r = tq // tk

    n_sh = len(shards)
    nkb = SB_W // LANES
    nq = s // tq

    def body(q_ref, k_ref, v_ref, *refs):
        sh_refs, (o_ref, tot_ref, start_ref), got_refs = refs[:n_sh], refs[n_sh:n_sh + 3], refs[n_sh + 3:2 * n_sh + 3]
        acc_refs, run_refs = refs[2 * n_sh + 3:2 * n_sh + 5]
        i = pl.program_id(1)
        step = pl.program_id(0) * nq + i
        gather = _HalvesGather(sh_refs, got_refs, *refs[2 * n_sh + 5:])

        @pl.when(step == 0)
        def _():
            gather.start()

        @pl.when(step == nkb * nq // 2)
        def _():
            gather.forward()

        lane = lax.broadcasted_iota(jnp.int32, (1, LANES), 1)
        first = lane < HEAD_DIM
        qp = q_ref[...]
        zero = jnp.zeros((), BF16)
        qs = (jnp.where(first, qp, zero), jnp.where(first, zero, qp))
        later = _sum_matrix(tk, lambda row, col: row > col)
        acc_refs[...] = jnp.zeros_like(acc_refs)
        run_refs[...] = jnp.zeros_like(run_refs)

        def blocks(tiles):
            rows = [slice(r0, r0 + n) for r0, n, _, _ in tiles]
            kjs = [k_ref[pl.ds(pl.multiple_of(j * tk, tk), tk), :] for _, _, j, _ in tiles]
            vjs = [v_ref[pl.ds(pl.multiple_of(j * tk, tk), tk), :] for _, _, j, _ in tiles]
            chains = [(hd, t) for t in range(len(tiles)) for hd in range(2)]
            zs = [_dot_nt(qs[hd][rows[t]], kjs[t]) for hd, t in chains]
            sps = [_softplus_parts(z)[0] for z in zs]
            befores = [_before(i * tq + r0, n, j * tk, tk) if diag else None for r0, n, j, diag in tiles]
            spms = [sp if befores[t] is None else jnp.where(befores[t], sp, 0.0) for (hd, t), sp in zip(chains, sps)]
            cums = [_block_sums(spm, later) for spm in spms]
            runs, accs, ws = _RowState(run_refs, tk), _RowState(acc_refs, tk, add=True), []
            for (hd, t), z, sp, (cum, sm) in zip(chains, zs, sps, cums):
                r0, n = tiles[t][:2]
                run = runs.get(hd, r0, n)
                w = jnp.exp(z - sp - cum - _across(run, tk))
                if befores[t] is not None:
                    w = jnp.where(befores[t], w, 0.0)
                ws.append(w.astype(BF16))
                runs.put(hd, r0, n, run + sm)
            for (hd, t), pv in zip(chains, [_dot(w, vjs[t]) for (hd, t), w in zip(chains, ws)]):
                accs.accumulate(hd, *tiles[t][:2], pv)
            accs.store()
            runs.store()

        below = i * r
        diagonal = [(d * tk, tk, below + e, e == d) for d in range(r) for e in range(d, -1, -1)]

        @pl.when(i == 0)
        def _():
            blocks(diagonal)

        @pl.when(i > 0)
        def _():
            blocks(diagonal + [(0, tq, below - 1, False)])

        def swept_mass():
            return jnp.min(jnp.minimum(run_refs[0], run_refs[1]))

        def more(carry):
            n, mass = carry
            return (n < below) & (mass < SB_DEAD_MASS)

        def sweep(carry):
            n, _ = carry
            blocks([(0, tq, below - 1 - n, False)])
            return n + 1, swept_mass()

        n_swept, _ = lax.while_loop(more, sweep, (jnp.minimum(below, 1), swept_mass()))
        start_ref[pl.program_id(0), i] = (below - n_swept).astype(F32)
        o_ref[...] = jnp.where(first, acc_refs[0], acc_refs[1])
        tot_ref[...] = jnp.where(first, run_refs[0], run_refs[1])

        @pl.when(step == nkb * nq - 1)
        def _():
            gather.wait()

    shp = jax.ShapeDtypeStruct((s, SB_W), F32)
    qspec = pl.BlockSpec((tq, LANES), lambda p, i: (i, p))
    hbm = pl.BlockSpec(memory_space=pl.ANY)
    return pl.pallas_call(
        body, name=name, grid=(nkb, nq),
        out_shape=[shp, shp, jax.ShapeDtypeStruct((nkb, nq), F32)]
        + [jax.ShapeDtypeStruct((N_CHIPS,) + a.shape, a.dtype) for a in shards],
        in_specs=[qspec,
                  pl.BlockSpec((s, LANES), lambda p, i: (0, nkb + p)),
                  pl.BlockSpec((s, LANES), lambda p, i: (0, 2 * nkb + p))] + [hbm] * n_sh,
        out_specs=[qspec, qspec, pl.BlockSpec(memory_space=pltpu.SMEM)] + [hbm] * n_sh,
        scratch_shapes=[pltpu.VMEM((2, tq, LANES), F32), pltpu.VMEM((2, tq, LANES), F32)] + _halves_gather_sems(n_sh),
        compiler_params=_cparams())(proj, proj, proj, *shards)


def _swa_masks(n):
    ti = lax.broadcasted_iota(jnp.int32, (WINDOW, 2 * WINDOW), 0)
    kj = lax.broadcasted_iota(jnp.int32, (WINDOW, 2 * WINDOW), 1)
    dist = ti + WINDOW - kj
    valid = (dist >= 0) & (dist < WINDOW) & ((n * WINDOW - WINDOW + kj) >= 0)
    return valid, dist.astype(F32)


def _swa_probs(sc, valid, distf, h, sink):
    slope = 2.0 ** (-(h + 1))
    sc = jnp.where(valid, sc - slope * distf, MASK_VALUE)
    mx = jnp.maximum(jnp.max(sc, axis=1, keepdims=True), sink)
    p = jnp.exp(sc - mx)
    es = jnp.exp(sink - mx)
    inv = 1.0 / (jnp.sum(p, axis=1, keepdims=True) + es)
    return p * inv, es * inv


def _swa_forward(proj, sinks, shards, name):
    s = proj.shape[0]
    nb = s // WINDOW
    qb, kb, vb = 3 * SB_W // SWA_QW, (3 * SB_W + SWA_QW) // LANES, (3 * SB_W + SWA_QW + SWA_KW) // LANES
    n_sh = len(shards)

    def body(q_ref, kp_ref, kc_ref, vp_ref, vc_ref, sink_ref, *refs):
        sh_refs, o_ref, got_refs = refs[:n_sh], refs[n_sh], refs[n_sh + 1:2 * n_sh + 1]
        n = pl.program_id(0)
        gather = _HalvesGather(sh_refs, got_refs, *refs[2 * n_sh + 1:])

        @pl.when(n == 0)
        def _():
            gather.start()

        @pl.when(n == nb // 2)
        def _():
            gather.forward()

        k = jnp.concatenate([kp_ref[...], kc_ref[...]], axis=0)
        v = jnp.concatenate([vp_ref[...], vc_ref[...]], axis=0)
        k_sw = pltpu.roll(k.astype(F32), HEAD_DIM, 1).astype(BF16)
        v_sw = pltpu.roll(v.astype(F32), HEAD_DIM, 1).astype(BF16)
        lane = lax.broadcasted_iota(jnp.int32, (1, LANES), 1)
        halves = [lane < HEAD_DIM, lane >= HEAD_DIM]
        valid, distf = _swa_masks(n)
        heads = range(2 * 4)
        qms = [jnp.where(halves[h % 2], q_ref[:, (h // 2) * LANES:(h // 2 + 1) * LANES], jnp.zeros((), BF16))
               for h in heads]
        kus = [k if h // 4 == h % 2 else k_sw for h in heads]
        vus = [v if h // 4 == h % 2 else v_sw for h in heads]
        scores = [_dot_nt(qms[h], kus[h]) for h in heads]
        ps = [_swa_probs(scores[h], valid, distf, h, sink_ref[h])[0].astype(BF16) for h in heads]
        outs = [_dot(ps[h], vus[h]) for h in heads]
        for pair in range(4):
            o_ref[:, pair * LANES:(pair + 1) * LANES] = jnp.where(halves[0], outs[2 * pair], outs[2 * pair + 1])

        @pl.when(n == nb - 1)
        def _():
            gather.wait()

    prev = lambda n: jnp.maximum(n - 1, 0)
    hbm = pl.BlockSpec(memory_space=pl.ANY)
    return pl.pallas_call(
        body, name=name, grid=(nb,),
        out_shape=[jax.ShapeDtypeStruct((s, SWA_QW), F32)]
        + [jax.ShapeDtypeStruct((N_CHIPS,) + a.shape, a.dtype) for a in shards],
        in_specs=[pl.BlockSpec((WINDOW, SWA_QW), lambda n: (n, qb)),
                  pl.BlockSpec((WINDOW, LANES), lambda n: (prev(n), kb)),
                  pl.BlockSpec((WINDOW, LANES), lambda n: (n, kb)),
                  pl.BlockSpec((WINDOW, LANES), lambda n: (prev(n), vb)),
                  pl.BlockSpec((WINDOW, LANES), lambda n: (n, vb)),
                  pl.BlockSpec(memory_space=pltpu.SMEM)] + [hbm] * n_sh,
        out_specs=[pl.BlockSpec((WINDOW, SWA_QW), lambda n: (n, 0))] + [hbm] * n_sh,
        scratch_shapes=_halves_gather_sems(n_sh),
        compiler_params=_cparams())(proj, proj, proj, proj, proj, sinks, *shards)


def _rms_parts(y):
    return lax.rsqrt(jnp.mean(y * y, axis=1, keepdims=True) + RMS_EPS)


def _post_attention(y_sb, y_sw, x, vec, w_out, name):
    s = x.shape[0]
    tb = min(TOK_TILE, s)

    def body(ysb_ref, ysw_ref, x_ref, vec_ref, w_ref, mixedt_ref, attn_ref, x1_ref, h2_ref, h2t_ref):
        ysb, ysw = ysb_ref[...], ysw_ref[...]
        nsb_f = ysb * _rms_parts(ysb) * vec_ref[V_GN:V_GN + 1, :SB_W]
        nsw_f = ysw * _rms_parts(ysw) * vec_ref[V_GN:V_GN + 1, SB_W:]
        nsb, nsw = nsb_f.astype(BF16), nsw_f.astype(BF16)
        mixedt_ref[:SB_W, :] = nsb_f.T.astype(BF16)
        mixedt_ref[SB_W:, :] = nsw_f.T.astype(BF16)
        attn = _dot(nsb, w_ref[:SB_W, :]) + _dot(nsw, w_ref[SB_W:, :])
        attn_ref[...] = attn
        u1 = ALPHA * x_ref[...] + (1.0 + vec_ref[V_G_A:V_G_A + 1, :]) * attn
        xhat, _ = _layer_norm_stats(u1)
        x1 = xhat * vec_ref[V_LN1G:V_LN1G + 1, :] + vec_ref[V_LN1B:V_LN1B + 1, :]
        x1_ref[...] = x1
        h2 = x1 * (1.0 + vec_ref[V_SC_F:V_SC_F + 1, :]) + vec_ref[V_SH_F:V_SH_F + 1, :]
        h2_ref[...] = h2.astype(BF16)
        h2t_ref[...] = h2.T.astype(BF16)

    half = pl.BlockSpec((tb, SB_W), lambda i: (i, 0))
    full = pl.BlockSpec((tb, D), lambda i: (i, 0))
    full_t = pl.BlockSpec((D, tb), lambda i: (0, i))
    return pl.pallas_call(
        body, name=name, grid=(s // tb,),
        out_shape=[jax.ShapeDtypeStruct((D, s), BF16), jax.ShapeDtypeStruct((s, D), F32),
                   jax.ShapeDtypeStruct((s, D), F32), jax.ShapeDtypeStruct((s, D), BF16),
                   jax.ShapeDtypeStruct((D, s), BF16)],
        in_specs=[half, half, full, _resident((VEC_ROWS, D)), _resident((D, D))],
        out_specs=[full_t, full, full, full, full_t],
        compiler_params=_cparams())(y_sb, y_sw, x, vec, w_out)


def _ffn_forward(h2, w_gu, w_down, name):
    s = h2.shape[0]
    tb = min(FFN_TILE, s)

    def body(h_ref, wgu_ref, wd_ref, gu_ref, actt_ref, ffn_ref):
        gu = _dot(h_ref[...], wgu_ref[...])
        gu_ref[...] = gu.astype(BF16)
        gate, up = gu[:, :D_FF], gu[:, D_FF:]
        act = gate * (1.0 / (1.0 + jnp.exp(-gate))) * up
        actt_ref[...] = act.T.astype(BF16)
        ffn_ref[...] = _dot(act.astype(BF16), wd_ref[...])

    return pl.pallas_call(
        body, name=name, grid=(s // tb,),
        out_shape=[jax.ShapeDtypeStruct((s, 2 * D_FF), BF16), jax.ShapeDtypeStruct((D_FF, s), BF16),
                   jax.ShapeDtypeStruct((s, D), F32)],
        in_specs=[pl.BlockSpec((tb, D), lambda i: (i, 0)), _resident((D, 2 * D_FF)), _resident((D_FF, D))],
        out_specs=[pl.BlockSpec((tb, 2 * D_FF), lambda i: (i, 0)), pl.BlockSpec((D_FF, tb), lambda i: (0, i)),
                   pl.BlockSpec((tb, D), lambda i: (i, 0))],
        compiler_params=_cparams())(h2, w_gu, w_down)


def _layer_norm_bwd(dxhat, xhat, rstd):
    m1 = jnp.mean(dxhat, axis=1, keepdims=True)
    m2 = jnp.mean(dxhat * xhat, axis=1, keepdims=True)
    return rstd * (dxhat - m1 - xhat * m2)


def _colsum(a):
    return jnp.sum(a, axis=0, keepdims=True)


A_LN2G, A_LN2B, A_GF, A_SCF, A_SHF, A_LOSS = range(6)
B_LN1G, B_LN1B, B_GA, B_GN = range(4)
C_SCA, C_SHA = range(2)


def _ffn_backward(x1, ffn, target, gu, vec, w_gu, w_down, name):
    s = x1.shape[0]
    tb = min(FFN_BWD_TILE, s)

    def body(x1_ref, ffn_ref, t_ref, gu_ref, vec_ref, wgu_ref, wd_ref, dffn_ref, dgu_ref, dx1_ref, acc_ref):
        @pl.when(pl.program_id(0) == 0)
        def _():
            acc_ref[...] = jnp.zeros_like(acc_ref)

        x1v, ffn_v = x1_ref[...], ffn_ref[...]
        g_f = 1.0 + vec_ref[V_G_F:V_G_F + 1, :]
        u2 = ALPHA * x1v + g_f * ffn_v
        xhat, rstd = _layer_norm_stats(u2)
        ln_g = vec_ref[V_LN2G:V_LN2G + 1, :]
        err = xhat * ln_g + vec_ref[V_LN2B:V_LN2B + 1, :] - t_ref[...]
        dx2 = err * (1.0 / D)
        acc_ref[A_LOSS:A_LOSS + 1, :] += _colsum(err * err) * (0.5 / D)
        acc_ref[A_LN2G:A_LN2G + 1, :] += _colsum(dx2 * xhat)
        acc_ref[A_LN2B:A_LN2B + 1, :] += _colsum(dx2)
        du2 = _layer_norm_bwd(dx2 * ln_g, xhat, rstd)
        acc_ref[A_GF:A_GF + 1, :] += _colsum(du2 * ffn_v)
        dffn = (g_f * du2).astype(BF16)
        dffn_ref[...] = dffn
        dact = _dot_nt(dffn, wd_ref[...])
        gate, up = gu_ref[:, :D_FF].astype(F32), gu_ref[:, D_FF:].astype(F32)
        sg = 1.0 / (1.0 + jnp.exp(-gate))
        dgate = (dact * up * (sg * (1.0 + gate * (1.0 - sg)))).astype(BF16)
        dup = (dact * (gate * sg)).astype(BF16)
        dgu_ref[:, :D_FF] = dgate
        dgu_ref[:, D_FF:] = dup
        dh2 = _dot_nt(dgate, wgu_ref[:, :D_FF]) + _dot_nt(dup, wgu_ref[:, D_FF:])
        dx1_ref[...] = ALPHA * du2 + dh2 * (1.0 + vec_ref[V_SC_F:V_SC_F + 1, :])
        acc_ref[A_SCF:A_SCF + 1, :] += _colsum(dh2 * x1v)
        acc_ref[A_SHF:A_SHF + 1, :] += _colsum(dh2)

    full = pl.BlockSpec((tb, D), lambda i: (i, 0))
    wide = pl.BlockSpec((tb, 2 * D_FF), lambda i: (i, 0))
    return pl.pallas_call(
        body, name=name, grid=(s // tb,),
        out_shape=[jax.ShapeDtypeStruct((s, D), BF16), jax.ShapeDtypeStruct((s, 2 * D_FF), BF16),
                   jax.ShapeDtypeStruct((s, D), F32), jax.ShapeDtypeStruct((8, D), F32)],
        in_specs=[full, full, full, wide, _resident((VEC_ROWS, D)), _resident((D, 2 * D_FF)), _resident((D_FF, D))],
        out_specs=[full, wide, full, pl.BlockSpec((8, D), lambda i: (0, 0))],
        compiler_params=_cparams())(x1, ffn, target, gu, vec, w_gu, w_down)


def _attn_out_backward(dx1, x, attn, y_sb, y_sw, vec, w_out, name):
    s = x.shape[0]
    tb = min(TOK_TILE, s)

    def body(dx1_ref, x_ref, attn_ref, ysb_ref, ysw_ref, vec_ref, w_ref, du1_ref, dattn_ref, dy_ref, acc_ref):
        @pl.when(pl.program_id(0) == 0)
        def _():
            acc_ref[...] = jnp.zeros_like(acc_ref)

        attn = attn_ref[...]
        g_a = 1.0 + vec_ref[V_G_A:V_G_A + 1, :]
        xhat, rstd = _layer_norm_stats(ALPHA * x_ref[...] + g_a * attn)
        dx1v = dx1_ref[...]
        acc_ref[B_LN1G:B_LN1G + 1, :] += _colsum(dx1v * xhat)
        acc_ref[B_LN1B:B_LN1B + 1, :] += _colsum(dx1v)
        du1 = _layer_norm_bwd(dx1v * vec_ref[V_LN1G:V_LN1G + 1, :], xhat, rstd)
        du1_ref[...] = du1
        acc_ref[B_GA:B_GA + 1, :] += _colsum(du1 * attn)
        dattn = (g_a * du1).astype(BF16)
        dattn_ref[...] = dattn
        dmixed = _dot_nt(dattn, w_ref[...])
        for lo, y_ref in ((0, ysb_ref), (SB_W, ysw_ref)):
            y = y_ref[...]
            rr = _rms_parts(y)
            dn = dmixed[:, lo:lo + SB_W]
            acc_ref[B_GN:B_GN + 1, lo:lo + SB_W] += _colsum(dn * y * rr)
            dng = dn * vec_ref[V_GN:V_GN + 1, lo:lo + SB_W]
            dy_ref[:, lo:lo + SB_W] = rr * dng - y * (rr * rr * rr) * jnp.mean(dng * y, axis=1, keepdims=True)

    half = pl.BlockSpec((tb, SB_W), lambda i: (i, 0))
    full = pl.BlockSpec((tb, D), lambda i: (i, 0))
    return pl.pallas_call(
        body, name=name, grid=(s // tb,),
        out_shape=[jax.ShapeDtypeStruct((s, D), F32), jax.ShapeDtypeStruct((s, D), BF16),
                   jax.ShapeDtypeStruct((s, D), F32), jax.ShapeDtypeStruct((8, D), F32)],
        in_specs=[full, full, full, half, half, _resident((VEC_ROWS, D)), _resident((D, D))],
        out_specs=[full, full, full, pl.BlockSpec((8, D), lambda i: (0, 0))],
        compiler_params=_cparams())(dx1, x, attn, y_sb, y_sw, vec, w_out)


def _sb_backward(proj, sp_total, sweep_start, dy, slabs, name):
    s = proj.shape[0]
    tq, tk = min(SB_TQ, s), min(SB_TK, s)
    r = tq // tk
    nkb = SB_W // LANES
    nq = s // tq

    n_sl = len(slabs)

    def body(q_ref, k_ref, v_ref, tot_ref, do_ref, start_ref, *refs):
        slab_refs, (dq_ref, dk_ref, dv_ref), got_refs = refs[:n_sl], refs[n_sl:n_sl + 3], refs[n_sl + 3:2 * n_sl + 3]
        dq_acc, left_refs, gsum_refs = refs[2 * n_sl + 3:2 * n_sl + 6]
        i = pl.program_id(1)
        step = pl.program_id(0) * nq + i
        scatter = _scatter_exchange(slab_refs, got_refs, *refs[2 * n_sl + 6:])

        @pl.when(step == 0)
        def _():
            scatter.start()

        @pl.when(i == 0)
        def _():
            dk_ref[...] = jnp.zeros_like(dk_ref)
            dv_ref[...] = jnp.zeros_like(dv_ref)

        lane = lax.broadcasted_iota(jnp.int32, (1, LANES), 1)
        first = lane < HEAD_DIM
        qp, dop, totp = q_ref[...], do_ref[...], tot_ref[...]
        zero = jnp.zeros((), BF16)
        qs = (jnp.where(first, qp, zero), jnp.where(first, zero, qp))
        dofs = (jnp.where(first, dop, 0.0), jnp.where(first, 0.0, dop))
        dobs = tuple(d.astype(BF16) for d in dofs)
        dots = tuple(d.T.astype(BF16) for d in dofs)
        qts = tuple(qh.astype(F32).T.astype(BF16) for qh in qs)
        later = _sum_matrix(tk, lambda row, col: row > col)
        earlier = _sum_matrix(tk, lambda row, col: row < col)
        dq_acc[...] = jnp.zeros_like(dq_acc)
        gsum_refs[...] = jnp.zeros_like(gsum_refs)
        swapped = pltpu.roll(totp, HEAD_DIM, 1)
        left_refs[0] = jnp.where(first, totp, swapped)
        left_refs[1] = jnp.where(first, swapped, totp)

        def blocks(tiles):
            rows = [slice(r0, r0 + n) for r0, n, _, _ in tiles]
            kjs = [k_ref[pl.ds(pl.multiple_of(j * tk, tk), tk), :] for _, _, j, _ in tiles]
            vjs = [v_ref[pl.ds(pl.multiple_of(j * tk, tk), tk), :] for _, _, j, _ in tiles]
            chains = [(hd, t) for t in range(len(tiles)) for hd in range(2)]
            zs = [_dot_nt(qs[hd][rows[t]], kjs[t]) for hd, t in chains]
            dws = [_dot_nt(dobs[hd][rows[t]], vjs[t]) for hd, t in chains]
            parts = [_softplus_parts(z) for z in zs]
            sps = [p[0] for p in parts]
            befores = [_before(i * tq + r0, n, j * tk, tk) if diag else None for r0, n, j, diag in tiles]
            spms = [sp if befores[t] is None else jnp.where(befores[t], sp, 0.0) for (hd, t), sp in zip(chains, sps)]
            cums = [_block_sums(spm, later) for spm in spms]
            lefts, gsums, dq_sums = _RowState(left_refs, tk), _RowState(gsum_refs, tk), _RowState(dq_acc, tk, add=True)
            ws = []
            for (hd, t), z, sp, (cum, sm) in zip(chains, zs, sps, cums):
                r0, n = tiles[t][:2]
                left = lefts.get(hd, r0, n) - sm
                lefts.put(hd, r0, n, left)
                w = jnp.exp(z - sp - cum - _across(left, tk))
                ws.append(w if befores[t] is None else jnp.where(befores[t], w, 0.0))
            wbs = [w.astype(BF16) for w in ws]
            dvs = [_dot(dots[hd][:, rows[t]], wb) for (hd, t), wb in zip(chains, wbs)]
            gs = [dw * w for dw, w in zip(dws, ws)]
            gcums = [_block_sums(g, earlier) for g in gs]
            dzbs = []
            for (hd, t), z, (sp, e1), g, (gcum, gsm) in zip(chains, zs, parts, gs, gcums):
                r0, n = tiles[t][:2]
                gsum = gsums.get(hd, r0, n)
                inv = 1.0 / (1.0 + e1)
                sig = jnp.where(z >= 0.0, inv, e1 * inv)
                dz = g - sig * (g + _across(gsum, tk) + gcum)
                dzbs.append((dz if befores[t] is None else jnp.where(befores[t], dz, 0.0)).astype(BF16))
                gsums.put(hd, r0, n, gsum + gsm)
            dqs = [_dot(dzb, kjs[t]) for (hd, t), dzb in zip(chains, dzbs)]
            dks = [_dot(qts[hd][:, rows[t]], dzb) for (hd, t), dzb in zip(chains, dzbs)]
            for t, (_, _, j, _) in enumerate(tiles):
                dv_ref[j] += dvs[2 * t] + dvs[2 * t + 1]
                dk_ref[j] += dks[2 * t] + dks[2 * t + 1]
            for (hd, t), dq in zip(chains, dqs):
                dq_sums.accumulate(hd, *tiles[t][:2], dq)
            dq_sums.store()
            lefts.store()
            gsums.store()

        below = i * r
        start = jnp.clip(start_ref[pl.program_id(0), i].astype(jnp.int32), 0, below)

        def sweep(n, carry):
            blocks([(0, tq, start + n, False)])
            return carry

        lax.fori_loop(0, jnp.maximum(below - 1 - start, 0), sweep, 0)
        diagonal = [(d * tk, tk, below + e, e == d) for d in range(r) for e in range(d + 1)]

        @pl.when(i == 0)
        def _():
            blocks(diagonal)

        @pl.when(i > 0)
        def _():
            blocks([(0, tq, below - 1, False)] + diagonal)
        dq_ref[...] = jnp.where(first, dq_acc[0], dq_acc[1])

        @pl.when(step == nkb * nq - 1)
        def _():
            scatter.wait()

    shp = jax.ShapeDtypeStruct((s, SB_W), F32)
    qspec = pl.BlockSpec((tq, LANES), lambda p, i: (i, p))
    whole = pl.BlockSpec((None, s // tk, LANES, tk), lambda p, i: (p, 0, 0, 0))
    shp_t = jax.ShapeDtypeStruct((nkb, s // tk, LANES, tk), F32)
    hbm = pl.BlockSpec(memory_space=pl.ANY)
    return pl.pallas_call(
        body, name=name, grid=(nkb, nq),
        out_shape=[shp, shp_t, shp_t] + [jax.ShapeDtypeStruct(p.shape, p.dtype) for p in slabs],
        in_specs=[qspec,
                  pl.BlockSpec((s, LANES), lambda p, i: (0, nkb + p)),
                  pl.BlockSpec((s, LANES), lambda p, i: (0, 2 * nkb + p)),
                  qspec, qspec, pl.BlockSpec(memory_space=pltpu.SMEM)] + [hbm] * n_sl,
        out_specs=[qspec, whole, whole] + [hbm] * n_sl,
        scratch_shapes=[pltpu.VMEM((2, tq, LANES), F32), pltpu.VMEM((2, tq, LANES), F32), pltpu.VMEM((2, tq, LANES), F32)]
        + _exchange_sems(n_sl),
        compiler_params=_cparams())(proj, proj, proj, sp_total, dy, sweep_start, *slabs)


def _swa_backward(proj, y_sw, dy, sinks, gives, name):
    s = proj.shape[0]
    nb = s // WINDOW
    qb, kb, vb = 3 * SB_W // SWA_QW, (3 * SB_W + SWA_QW) // LANES, (3 * SB_W + SWA_QW + SWA_KW) // LANES

    n_gv = len(gives)

    def body(q_ref, kp_ref, kc_ref, vp_ref, vc_ref, o_ref, do_ref, sink_ref, *refs):
        give_refs, (dq_ref, dk_ref, dv_ref, ds_ref), got_refs = refs[:n_gv], refs[n_gv:n_gv + 4], refs[n_gv + 4:2 * n_gv + 4]
        n = pl.program_id(0)
        swap = _sibling_halves(give_refs, got_refs, *refs[2 * n_gv + 4:])

        @pl.when(n == 0)
        def _():
            for cp in swap:
                cp.start()

        @pl.when(n == 0)
        def _():
            dk_ref[...] = jnp.zeros_like(dk_ref)
            dv_ref[...] = jnp.zeros_like(dv_ref)
            ds_ref[...] = jnp.zeros_like(ds_ref)

        k = jnp.concatenate([kp_ref[...], kc_ref[...]], axis=0)
        v = jnp.concatenate([vp_ref[...], vc_ref[...]], axis=0)
        k_sw = pltpu.roll(k.astype(F32), HEAD_DIM, 1).astype(BF16)
        v_sw = pltpu.roll(v.astype(F32), HEAD_DIM, 1).astype(BF16)
        lane = lax.broadcasted_iota(jnp.int32, (1, LANES), 1)
        halves = [lane < HEAD_DIM, lane >= HEAD_DIM]
        valid, distf = _swa_masks(n)
        heads = range(2 * 4)
        cols = [slice((h // 2) * LANES, (h // 2 + 1) * LANES) for h in heads]
        qms = [jnp.where(halves[h % 2], q_ref[:, cols[h]], jnp.zeros((), BF16)) for h in heads]
        dos = [jnp.where(halves[h % 2], do_ref[:, cols[h]], 0.0) for h in heads]
        dobs = [d.astype(BF16) for d in dos]
        native = [h // 4 == h % 2 for h in heads]
        kus = [k if native[h] else k_sw for h in heads]
        vus = [v if native[h] else v_sw for h in heads]
        scores = [_dot_nt(qms[h], kus[h]) for h in heads]
        dps = [_dot_nt(dobs[h], vus[h]) for h in heads]
        deltas = [jnp.sum(dos[h] * o_ref[:, cols[h]], axis=1, keepdims=True) for h in heads]
        probs = [_swa_probs(scores[h], valid, distf, h, sink_ref[h]) for h in heads]
        pbs = [probs[h][0].astype(BF16) for h in heads]
        dscs = [(probs[h][0] * (dps[h] - deltas[h])).astype(BF16) for h in heads]
        dqs = [_dot(dscs[h], kus[h]) for h in heads]
        dks = [_dot_tn(dscs[h], qms[h]) for h in heads]
        dvs = [_dot_tn(pbs[h], dobs[h]) for h in heads]
        for h in heads:
            ds_ref[h:h + 1, :] += jnp.zeros((1, LANES), F32) - jnp.sum(probs[h][1] * deltas[h])
        for pair in range(4):
            dq_ref[:, cols[2 * pair]] = jnp.where(halves[0], dqs[2 * pair], dqs[2 * pair + 1])

        def gathered(parts):
            nat = sum(parts[h] for h in heads if native[h])
            rot = sum(parts[h] for h in heads if not native[h])
            return nat + pltpu.roll(rot, HEAD_DIM, 1)

        dk, dv = gathered(dks), gathered(dvs)
        prev = pl.multiple_of(jnp.maximum(n - 1, 0) * WINDOW, WINDOW)
        cur = pl.multiple_of(n * WINDOW, WINDOW)
        dk_ref[pl.ds(prev, WINDOW), :] += dk[:WINDOW]
        dv_ref[pl.ds(prev, WINDOW), :] += dv[:WINDOW]
        dk_ref[pl.ds(cur, WINDOW), :] += dk[WINDOW:]
        dv_ref[pl.ds(cur, WINDOW), :] += dv[WINDOW:]

        @pl.when(n == nb - 1)
        def _():
            for cp in swap:
                cp.wait()

    prev_blk = lambda n: jnp.maximum(n - 1, 0)
    wide = pl.BlockSpec((WINDOW, SWA_QW), lambda n: (n, 0))
    whole = pl.BlockSpec((s, LANES), lambda n: (0, 0))
    hbm = pl.BlockSpec(memory_space=pl.ANY)
    return pl.pallas_call(
        body, name=name, grid=(nb,),
        out_shape=[jax.ShapeDtypeStruct((s, SWA_QW), F32), jax.ShapeDtypeStruct((s, LANES), F32),
                   jax.ShapeDtypeStruct((s, LANES), F32), jax.ShapeDtypeStruct((8, LANES), F32)] + _halves_shapes(gives),
        in_specs=[pl.BlockSpec((WINDOW, SWA_QW), lambda n: (n, qb)),
                  pl.BlockSpec((WINDOW, LANES), lambda n: (prev_blk(n), kb)),
                  pl.BlockSpec((WINDOW, LANES), lambda n: (n, kb)),
                  pl.BlockSpec((WINDOW, LANES), lambda n: (prev_blk(n), vb)),
                  pl.BlockSpec((WINDOW, LANES), lambda n: (n, vb)),
                  wide,
                  pl.BlockSpec((WINDOW, SWA_QW), lambda n: (n, 1)),
                  pl.BlockSpec(memory_space=pltpu.SMEM)] + [hbm] * n_gv,
        out_specs=[wide, whole, whole, pl.BlockSpec((8, LANES), lambda n: (0, 0))] + [hbm] * n_gv,
        scratch_shapes=_halves_sems(n_gv),
        compiler_params=_cparams())(proj, proj, proj, proj, proj, y_sw, dy, sinks, *gives)


def _in_proj_backward(dq_sb, dkt_sb, dvt_sb, dq_sw, dk_sw, dv_sw, du1, x, vec, w_in, name):
    s = x.shape[0]
    tb = min(TOK_TILE, s)
    n_pairs, _, _, tk = dkt_sb.shape

    def body(dqsb_ref, dktsb_ref, dvtsb_ref, dqsw_ref, dksw_ref, dvsw_ref, du1_ref, x_ref, vec_ref, w_ref,
             dproj_ref, gx_ref, acc_ref, bacc_ref):
        @pl.when(pl.program_id(0) == 0)
        def _():
            acc_ref[...] = jnp.zeros_like(acc_ref)
            bacc_ref[...] = jnp.zeros_like(bacc_ref)

        pieces = ((0, dqsb_ref, QK_SCALE), (3 * SB_W, dqsw_ref, QK_SCALE), (3 * SB_W + SWA_QW, dksw_ref, 1.0),
                  (3 * SB_W + SWA_QW + SWA_KW, dvsw_ref, 1.0))
        for lo, ref, scale in pieces:
            width = ref.shape[1]
            piece = ref[...] * scale
            bacc_ref[0:1, lo:lo + width] += _colsum(piece)
            dproj_ref[:, lo:lo + width] = piece.astype(BF16)
        for base, ref in ((SB_W, dktsb_ref), (2 * SB_W, dvtsb_ref)):
            for p in range(n_pairs):
                lo = base + p * LANES
                for jj in range(tb // tk):
                    piece = ref[p, jj].T
                    bacc_ref[0:1, lo:lo + LANES] += _colsum(piece)
                    dproj_ref[jj * tk:(jj + 1) * tk, lo:lo + LANES] = piece.astype(BF16)
        dh = _dot_nt(dproj_ref[...], w_ref[...])
        xv = x_ref[...]
        gx_ref[...] = ALPHA * du1_ref[...] + dh * (1.0 + vec_ref[V_SC_A:V_SC_A + 1, :])
        acc_ref[C_SCA:C_SCA + 1, :] += _colsum(dh * xv)
        acc_ref[C_SHA:C_SHA + 1, :] += _colsum(dh)

    half = pl.BlockSpec((tb, SB_W), lambda i: (i, 0))
    narrow = pl.BlockSpec((tb, LANES), lambda i: (i, 0))
    full = pl.BlockSpec((tb, D), lambda i: (i, 0))
    blocks_t = pl.BlockSpec((n_pairs, tb // tk, LANES, tk), lambda i: (0, i, 0, 0))
    return pl.pallas_call(
        body, name=name, grid=(s // tb,),
        out_shape=[jax.ShapeDtypeStruct((s, D_IN), BF16), jax.ShapeDtypeStruct((s, D), F32),
                   jax.ShapeDtypeStruct((8, D), F32), jax.ShapeDtypeStruct((8, D_IN), F32)],
        in_specs=[half, blocks_t, blocks_t, half, narrow, narrow, full, full, _resident((VEC_ROWS, D)),
                  _resident((D, D_IN))],
        out_specs=[pl.BlockSpec((tb, D_IN), lambda i: (i, 0)), full, pl.BlockSpec((8, D), lambda i: (0, 0)),
                   pl.BlockSpec((8, D_IN), lambda i: (0, 0))],
        compiler_params=_cparams())(dq_sb, dkt_sb, dvt_sb, dq_sw, dk_sw, dv_sw, du1, x, vec, w_in)


def _weight_grad(at, b, name, col_shards=1):
    m, s = at.shape
    n = b.shape[1]
    if col_shards > 1:
        tn = n // col_shards
        out_shape = jax.ShapeDtypeStruct((col_shards, m, tn), F32)
        out_spec = pl.BlockSpec((None, m, tn), lambda j, k: (j, 0, 0))
    else:
        tn = 512 if n % 512 == 0 else n
        out_shape = jax.ShapeDtypeStruct((m, n), F32)
        out_spec = pl.BlockSpec((m, tn), lambda j, k: (0, j))
    ts = min(WGRAD_TOKENS, s)
    while 2 * (m * ts * 2 + ts * tn * 2 + m * tn * 4) > WGRAD_VMEM and ts > 512:
        ts //= 2

    def body(at_ref, b_ref, o_ref):
        @pl.when(pl.program_id(1) == 0)
        def _():
            o_ref[...] = jnp.zeros_like(o_ref)

        o_ref[...] += _dot(at_ref[...], b_ref[...])

    return pl.pallas_call(
        body, name=name, grid=(n // tn, s // ts),
        out_shape=out_shape,
        in_specs=[pl.BlockSpec((m, ts), lambda j, k: (0, k)), pl.BlockSpec((ts, tn), lambda j, k: (k, j))],
        out_specs=out_spec,
        compiler_params=_cparams())(at, b)


def _pad_rows(v, rows):
    return jnp.concatenate([v, jnp.zeros((rows - v.shape[0], v.shape[1]), v.dtype)], axis=0)


def _col_shards(w, n_shards):
    r, n = w.shape
    return w.reshape(r, n_shards, n // n_shards).transpose(1, 0, 2)


def kernel(x, c, w_ada, b_ada, w_in, b_in, sinks, gn_sb, gn_swa, w_out, ln1_g, ln1_b, w_gu, w_down, ln2_g, ln2_b, loss_target, m_w_ada, m_b_ada, m_w_in, m_b_in, m_sinks, m_gn_sb, m_gn_swa, m_w_out, m_ln1_g, m_ln1_b, m_w_gu, m_w_down, m_ln2_g, m_ln2_b, v_w_ada, v_b_ada, v_w_in, v_b_in, v_sinks, v_gn_sb, v_gn_swa, v_w_out, v_ln1_g, v_ln1_b, v_w_gu, v_w_down, v_ln2_g, v_ln2_b):
    ix, iy, ic = lax.axis_index("x"), lax.axis_index("y"), lax.axis_index("c")
    chip = 2 * ix + iy
    dev = 4 * ix + 2 * iy + ic
    xs, target = x[0], loss_target[0]
    s = xs.shape[0]

    c_rows, g_in = _allgather8(_pad_rows(c, 8), "gather_c", gather=[w_in[0].astype(BF16)])
    c_all = c_rows[::8]
    n_ada = w_ada.shape[2]
    b_ada_shard = lax.dynamic_slice_in_dim(b_ada, chip * n_ada, n_ada, axis=1)
    mod_cols, silu_c = _mod_shard(c_all, w_ada[0], b_ada_shard, "mod_shard")
    mod_all = _allgather8(mod_cols, "gather_mod")[0].reshape(N_DEV, 8, n_ada)
    mod_mine = lax.dynamic_index_in_dim(mod_all, dev, axis=1, keepdims=False)
    mod = mod_mine.reshape(N_CHIPS, 2, n_ada)[:, 0].reshape(6, D)
    vec = jnp.concatenate([mod, ln1_g, ln1_b, ln2_g, ln2_b, jnp.concatenate([gn_sb, gn_swa], axis=1),
                           jnp.zeros((VEC_ROWS - 11, D), F32)], axis=0)

    w_in_b = g_in.transpose(1, 0, 2).reshape(D, D_IN)

    h_t, proj = _in_proj(xs, vec, w_in_b, b_in, "in_proj")
    y_sb, sp_total, sweep_start, g_out, g_gu = _sb_forward(
        proj, [w_out[0].astype(BF16), w_gu[0].astype(BF16)], "sb_forward")
    w_gu_b = g_gu.transpose(1, 0, 2).reshape(D, 2 * D_FF)
    w_out_b = g_out.reshape(D, D)
    sink_vec = sinks[0]
    y_sw, g_down = _swa_forward(proj, sink_vec, [w_down[0].astype(BF16)], "swa_forward")
    w_down_b = g_down.reshape(D_FF, D)
    mixed_t, attn, x1, h2_b, h2_t = _post_attention(y_sb, y_sw, xs, vec, w_out_b, "post_attention")
    gu, act_t, ffn = _ffn_forward(h2_b, w_gu_b, w_down_b, "ffn_forward")

    def in_halves(shards):
        n_sh, rows, cols = shards.shape
        return shards.reshape(n_sh, 2, rows // 2, cols)

    core = ic.reshape(1).astype(jnp.int32)
    dffn_b, dgu_b, dx1, acc_f = _ffn_backward(x1, ffn, target, gu, vec, w_gu_b, w_down_b, "ffn_backward")
    dw_gu = _weight_grad(h2_t, dgu_b, "grad_w_gu", col_shards=4)
    dw_down = _weight_grad(act_t, dffn_b, "grad_w_down")
    du1, dattn_b, dy, acc_a = _attn_out_backward(dx1, xs, attn, y_sb, y_sw, vec, w_out_b, "attn_out_backward")
    dw_out = _weight_grad(mixed_t, dattn_b, "grad_w_out")
    first = [in_halves(dw_gu), in_halves(dw_down.reshape(4, D_FF // 4, D)), in_halves(dw_out.reshape(4, D // 4, D))]
    dq_sw, dk_sw, dv_sw, dsink, *got_first = _swa_backward(proj, y_sw, dy, sink_vec, first, "swa_backward")
    sums_first = _chip_sums(first, got_first, core, "grad_chip_sums")
    dq_sb, dk_sb, dv_sb, *parts_first = _sb_backward(proj, sp_total, sweep_start, dy, sums_first, "sb_backward")
    dproj_b, grad_x, acc_i, acc_b = _in_proj_backward(dq_sb, dk_sb, dv_sb, dq_sw, dk_sw, dv_sw, du1, xs, vec, w_in_b,
                                                      "in_proj_backward")
    dw_in = _weight_grad(h_t, dproj_b, "grad_w_in")
    last = [in_halves(_col_shards(dw_in, 4))]
    sums_last = _chip_sums(last, _halves_swap(last, "grad_halves_swap_in"), core, "grad_chip_sum_in")

    dmod = jnp.concatenate([acc_i[C_SHA:C_SHA + 1], acc_i[C_SCA:C_SCA + 1], acc_a[B_GA:B_GA + 1],
                            acc_f[A_SHF:A_SHF + 1], acc_f[A_SCF:A_SCF + 1], acc_f[A_GF:A_GF + 1]], axis=1)
    dsink_row = jnp.concatenate([dsink[:, 0].reshape(1, 8), jnp.zeros((1, LANES - 8), F32)], axis=1)
    loss_row = jnp.concatenate([jnp.sum(acc_f[A_LOSS:A_LOSS + 1], axis=1, keepdims=True),
                                jnp.zeros((1, LANES - 1), F32)], axis=1)
    small = jnp.concatenate([dmod, acc_b[0:1], acc_a[B_LN1G:B_LN1G + 1], acc_a[B_LN1B:B_LN1B + 1],
                             acc_f[A_LN2G:A_LN2G + 1], acc_f[A_LN2B:A_LN2B + 1], acc_a[B_GN:B_GN + 1],
                             dsink_row, loss_row], axis=1)
    small_rows, *parts_last = _allgather8(_pad_rows(small, 8), "gather_small", scatter=sums_last)
    small_all = small_rows[::8]

    mine = _sum4s([*parts_first, *parts_last], "grad_reduce")
    theirs = _sibling_send(mine, "grad_half_return")
    gw_gu, gw_down, gw_out, gw_in = [
        jnp.concatenate([jnp.where(ic == 0, m_, t_), jnp.where(ic == 0, t_, m_)], axis=0) for m_, t_ in zip(mine, theirs)]

    small_names = ["b_ada", "b_in", "ln1_g", "ln1_b", "ln2_g", "ln2_b", "gn_sb", "gn_swa", "sinks"]
    small_at = [SM_MOD, SM_BIN, SM_LN1G, SM_LN1B, SM_LN2G, SM_LN2B, SM_GN, SM_GN + SB_W, SM_SINK]
    *small_out, loss_row_all = _small_update(
        small_all, small_at,
        [b_ada, b_in, ln1_g, ln1_b, ln2_g, ln2_b, gn_sb, gn_swa, sinks],
        [m_b_ada, m_b_in, m_ln1_g, m_ln1_b, m_ln2_g, m_ln2_b, m_gn_sb, m_gn_swa, m_sinks],
        [v_b_ada, v_b_in, v_ln1_g, v_ln1_b, v_ln2_g, v_ln2_b, v_gn_sb, v_gn_swa, v_sinks], SM_LOSS, "small_update")
    g_small, d_small, m2_small, v2_small = [dict(zip(small_names, leaves)) for leaves in small_out]
    loss = loss_row_all[0, 0]

    dmod_cols = lax.dynamic_slice_in_dim(small_all[:, SM_MOD:SM_BIN], chip * n_ada, n_ada, axis=1)
    gw_ada = _weight_grad(_pad_rows(silu_c, LANES).astype(BF16).T, _pad_rows(dmod_cols, LANES).astype(BF16), "grad_w_ada")

    big = {}
    for nm, w, g, m, v in (("w_ada", w_ada, gw_ada, m_w_ada, v_w_ada), ("w_in", w_in, gw_in, m_w_in, v_w_in),
                           ("w_out", w_out, gw_out, m_w_out, v_w_out), ("w_gu", w_gu, gw_gu, m_w_gu, v_w_gu),
                           ("w_down", w_down, gw_down, m_w_down, v_w_down)):
        d_, m2_, v2_ = _adamw(w[0], g, m[0], v[0], "adamw_" + nm)
        big[nm] = (g[None], d_[None], m2_[None], v2_[None])

    order = ["w_ada", "b_ada", "w_in", "b_in", "sinks", "gn_sb", "gn_swa", "w_out", "ln1_g", "ln1_b", "w_gu", "w_down",
             "ln2_g", "ln2_b"]

    def leaf(nm, which):
        if nm in big:
            return big[nm][which]
        return (g_small, d_small, m2_small, v2_small)[which][nm]

    outs = [loss, grad_x[None]]
    for which in range(4):
        outs += [leaf(nm, which) for nm in order]
    return tuple(outs)
```

```python
import math

import jax
import jax.numpy as jnp
from jax import lax
from jax.experimental import pallas as pl
from jax.experimental.pallas import tpu as pltpu

F32 = jnp.float32
BF16 = jnp.bfloat16

D = 1024
HEAD_DIM = 64
SB_W = 512
SWA_QW = 512
SWA_KW = 128
D_IN = 2304
D_FF = 2816
WINDOW = 128
ALPHA = 2.0 ** 0.25
LN_EPS = 1e-5
RMS_EPS = 1e-6
MASK_VALUE = -1e30
QK_SCALE = 1.0 / math.sqrt(HEAD_DIM)

ADAM_LR = 0.001
ADAM_B1 = 0.9
ADAM_B2 = 0.999
ADAM_EPS = 1e-08
ADAM_WD = 0.01
ADAM_STEP = 10

N_CHIPS = 4
N_DEV = 8
LANES = 128

SB_TQ = 512
SB_TK = 256
SB_DEAD_MASS = 110.0
TOK_TILE = 512
FFN_TILE = 256
FFN_BWD_TILE = 256
VMEM_LIMIT = 56 * 1024 * 1024
WGRAD_TOKENS = 2048
WGRAD_VMEM = 40 * 1024 * 1024

V_SH_A, V_SC_A, V_G_A, V_SH_F, V_SC_F, V_G_F, V_LN1G, V_LN1B, V_LN2G, V_LN2B, V_GN = range(11)
VEC_ROWS = 16

SM_MOD = 0
SM_BIN = 6 * D
SM_LN1G = SM_BIN + D_IN
SM_LN1B = SM_LN1G + D
SM_LN2G = SM_LN1B + D
SM_LN2B = SM_LN2G + D
SM_GN = SM_LN2B + D
SM_SINK = SM_GN + D
SM_LOSS = SM_SINK + LANES
SM_LEN = SM_LOSS + LANES

MESH = pl.DeviceIdType.MESH


def _cparams(**kw):
    return pltpu.CompilerParams(vmem_limit_bytes=VMEM_LIMIT, **kw)


def _resident(shape):
    nd = len(shape)
    return pl.BlockSpec(shape, lambda *_: (0,) * nd, pipeline_mode=pl.Buffered(1))


def _dot(a, b):
    return jnp.dot(a, b, preferred_element_type=F32)


def _dot_nt(a, b):
    return lax.dot_general(a, b, (((1,), (1,)), ((), ())), preferred_element_type=F32)


def _dot_tn(a, b):
    return lax.dot_general(a, b, (((0,), (0,)), ((), ())), preferred_element_type=F32)


def _sum_matrix(tk, keep):
    row = lax.broadcasted_iota(jnp.int32, (tk, tk + LANES), 0)
    col = lax.broadcasted_iota(jnp.int32, (tk, tk + LANES), 1)
    return (keep(row, col) | (col >= tk)).astype(BF16)


def _block_sums(x, m):
    tk = x.shape[1]
    res = _dot(x.astype(BF16), m)
    return res[:, :tk], res[:, tk:]


def _before(t0, n, s0, tk):
    return s0 + lax.broadcasted_iota(jnp.int32, (n, tk), 1) < t0 + lax.broadcasted_iota(jnp.int32, (n, tk), 0)


class _RowState:
    def __init__(self, ref, tk, add=False):
        self.ref, self.tk, self.add, self.vals = ref, tk, add, {}

    def _blocks(self, r0, n):
        return range(r0 // self.tk, (r0 + n) // self.tk)

    def get(self, hd, r0, n):
        for d in self._blocks(r0, n):
            if (hd, d) not in self.vals:
                self.vals[(hd, d)] = self.ref[hd, d * self.tk:(d + 1) * self.tk, :]
        parts = [self.vals[(hd, d)] for d in self._blocks(r0, n)]
        return parts[0] if len(parts) == 1 else jnp.concatenate(parts, axis=0)

    def put(self, hd, r0, n, val):
        for k, d in enumerate(self._blocks(r0, n)):
            self.vals[(hd, d)] = val[k * self.tk:(k + 1) * self.tk]

    def accumulate(self, hd, r0, n, val):
        for k, d in enumerate(self._blocks(r0, n)):
            part = val[k * self.tk:(k + 1) * self.tk]
            self.vals[(hd, d)] = part if (hd, d) not in self.vals else self.vals[(hd, d)] + part

    def store(self):
        for (hd, d), val in self.vals.items():
            span = slice(d * self.tk, (d + 1) * self.tk)
            if self.add:
                self.ref[hd, span, :] += val
            else:
                self.ref[hd, span, :] = val


def _across(v, tk):
    return jnp.concatenate([v] * (tk // LANES), axis=1)


def _allgather8(v, name, gather=(), scatter=()):
    m_per, n = v.shape
    n_g, n_s = len(gather), len(scatter)

    def body(x_ref, *refs):
        g_in, s_in = refs[:n_g], refs[n_g:n_g + n_s]
        out_ref = refs[n_g + n_s]
        g_out, s_out = refs[n_g + n_s + 1:2 * n_g + n_s + 1], refs[2 * n_g + n_s + 1:2 * (n_g + n_s) + 1]
        send_sems, recv_sems, local_sem, *more_sems = refs[2 * (n_g + n_s) + 1:]
        halves = _HalvesGather(g_in, g_out, *more_sems[:5]) if n_g else None
        beside = ([halves] if n_g else []) + ([_scatter_exchange(s_in, s_out, *more_sems[-3:])] if n_s else [])
        for ex in beside:
            ex.start()
        x, y, c = lax.axis_index("x"), lax.axis_index("y"), lax.axis_index("c")
        me, sibling = (x, y, c), (x, y, 1 - c)
        chips = [(1 - x, y), (x, 1 - y), (1 - x, 1 - y)]

        def rows(px, py, pc):
            return out_ref.at[pl.ds((4 * px + 2 * py + pc) * m_per, m_per), :]

        def copy(k, block, to, src=None):
            return pltpu.make_async_remote_copy(
                src_ref=rows(*block) if src is None else src, dst_ref=rows(*block),
                send_sem=send_sems.at[k], recv_sem=recv_sems.at[k], device_id=to, device_id_type=MESH)

        mine = pltpu.make_async_copy(x_ref, rows(*me), local_sem)
        mine.start()
        first = [copy(0, me, sibling, src=x_ref)]
        first += [copy(1 + j, me, (*chip, c), src=x_ref) for j, chip in enumerate(chips)]
        for cp in first:
            cp.start()
        passed = [copy(4 + j, (*chip, c), sibling) for j, chip in enumerate(chips)]
        for j, chip in enumerate(chips):
            copy(1 + j, (*chip, c), me).wait_recv()
            passed[j].start()
        copy(0, sibling, me).wait_recv()
        for j, chip in enumerate(chips):
            copy(4 + j, (*chip, 1 - c), me).wait_recv()
        for cp in first + passed:
            cp.wait_send()
        mine.wait()
        if halves is not None:
            halves.forward()
        for ex in beside:
            ex.wait()

    hbm = pl.BlockSpec(memory_space=pl.ANY)
    return pl.pallas_call(
        body, name=name,
        out_shape=[jax.ShapeDtypeStruct((N_DEV * m_per, n), v.dtype)]
        + [jax.ShapeDtypeStruct((N_CHIPS,) + a.shape, a.dtype) for a in gather]
        + [jax.ShapeDtypeStruct(p.shape, p.dtype) for p in scatter],
        in_specs=[pl.BlockSpec(memory_space=pltpu.VMEM)] + [hbm] * (n_g + n_s),
        out_specs=[pl.BlockSpec(memory_space=pltpu.VMEM)] + [hbm] * (n_g + n_s),
        scratch_shapes=[pltpu.SemaphoreType.DMA((7,)), pltpu.SemaphoreType.DMA((7,)), pltpu.SemaphoreType.DMA]
        + (_halves_gather_sems(n_g) if n_g else [])
        + (_exchange_sems(n_s) if n_s else []),
        compiler_params=_cparams(),
    )(v, *gather, *scatter)


class _Exchange:
    def __init__(self, local, sends, arrivals):
        self.local, self.sends, self.arrivals = local, sends, arrivals

    def start(self):
        for cp in self.local + self.sends:
            cp.start()

    def wait(self):
        for cp in self.arrivals:
            cp.wait_recv()
        for cp in self.sends:
            cp.wait_send()
        for cp in self.local:
            cp.wait()


def _exchange_sems(n):
    return [pltpu.SemaphoreType.DMA((3 * n,)), pltpu.SemaphoreType.DMA((3 * n,)), pltpu.SemaphoreType.DMA((n,))]


class _HalvesGather:
    def __init__(self, ins, outs, far_send, far_recv, near_send, near_recv, local_sems):
        x, y, c = lax.axis_index("x"), lax.axis_index("y"), lax.axis_index("c")
        slot = 2 * x + y
        chips = [(1 - x, y), (x, 1 - y), (1 - x, 1 - y)]
        self.local, self.far, self.landed, self.near, self.passed = [], [], [], [], []
        for a in range(len(ins)):
            h = ins[a].shape[0] // 2
            mine, theirs = pl.ds(c * h, h), pl.ds((1 - c) * h, h)
            self.local.append(pltpu.make_async_copy(ins[a], outs[a].at[slot], local_sems.at[a]))
            for j, (px, py) in enumerate(chips):
                k, there = 3 * a + j, 2 * px + py
                far = dict(send_sem=far_send.at[k], recv_sem=far_recv.at[k], device_id=(px, py, c), device_id_type=MESH)
                near = dict(send_sem=near_send.at[k], recv_sem=near_recv.at[k], device_id=(x, y, 1 - c),
                            device_id_type=MESH)
                self.far.append(pltpu.make_async_remote_copy(
                    src_ref=ins[a].at[mine], dst_ref=outs[a].at[slot, mine], **far))
                self.landed.append(pltpu.make_async_remote_copy(
                    src_ref=ins[a].at[mine], dst_ref=outs[a].at[there, mine], **far))
                self.near.append(pltpu.make_async_remote_copy(
                    src_ref=outs[a].at[there, mine], dst_ref=outs[a].at[there, mine], **near))
                self.passed.append(pltpu.make_async_remote_copy(
                    src_ref=outs[a].at[there, mine], dst_ref=outs[a].at[there, theirs], **near))

    def start(self):
        for cp in self.local + self.far:
            cp.start()

    def forward(self):
        for landed, near in zip(self.landed, self.near):
            landed.wait_recv()
            near.start()

    def wait(self):
        for cp in self.passed:
            cp.wait_recv()
        for cp in self.far + self.near:
            cp.wait_send()
        for cp in self.local:
            cp.wait()


def _halves_gather_sems(n):
    return [pltpu.SemaphoreType.DMA((3 * n,))] * 4 + [pltpu.SemaphoreType.DMA((n,))]


def _scatter_exchange(p_refs, out_refs, send_sems, recv_sems, local_sems):
    x, y, c = lax.axis_index("x"), lax.axis_index("y"), lax.axis_index("c")
    slot = 2 * x + y
    chips = [(1 - x, y), (x, 1 - y), (1 - x, 1 - y)]
    local, sends, arrivals = [], [], []
    for a, (p_ref, out_ref) in enumerate(zip(p_refs, out_refs)):
        local.append(pltpu.make_async_copy(p_ref.at[slot], out_ref.at[slot], local_sems.at[a]))
        for j, (px, py) in enumerate(chips):
            sems = dict(send_sem=send_sems.at[3 * a + j], recv_sem=recv_sems.at[3 * a + j],
                        device_id=(px, py, c), device_id_type=MESH)
            sends.append(pltpu.make_async_remote_copy(src_ref=p_ref.at[2 * px + py], dst_ref=out_ref.at[slot], **sems))
            arrivals.append(pltpu.make_async_remote_copy(src_ref=p_ref.at[slot], dst_ref=out_ref.at[2 * px + py], **sems))
    return _Exchange(local, sends, arrivals)


def _sibling_halves(give_refs, got_refs, send_sems, recv_sems):
    x, y, c = lax.axis_index("x"), lax.axis_index("y"), lax.axis_index("c")
    copies = []
    for a, (give_ref, got_ref) in enumerate(zip(give_refs, got_refs)):
        for s in range(N_CHIPS):
            copies.append(pltpu.make_async_remote_copy(
                src_ref=give_ref.at[s, 1 - c], dst_ref=got_ref.at[s], send_sem=send_sems.at[N_CHIPS * a + s],
                recv_sem=recv_sems.at[N_CHIPS * a + s], device_id=(x, y, 1 - c), device_id_type=MESH))
    return copies


def _halves_shapes(arrs):
    return [jax.ShapeDtypeStruct((a.shape[0],) + a.shape[2:], a.dtype) for a in arrs]


def _halves_sems(n):
    return [pltpu.SemaphoreType.DMA((N_CHIPS * n,)), pltpu.SemaphoreType.DMA((N_CHIPS * n,))]


def _halves_swap(arrs, name):
    n = len(arrs)

    def body(*refs):
        copies = _sibling_halves(refs[:n], refs[n:2 * n], *refs[2 * n:])
        for cp in copies:
            cp.start()
        for cp in copies:
            cp.wait()

    hbm = pl.BlockSpec(memory_space=pl.ANY)
    return pl.pallas_call(body, name=name, out_shape=_halves_shapes(arrs), in_specs=[hbm] * n, out_specs=[hbm] * n,
                          scratch_shapes=_halves_sems(n), compiler_params=_cparams())(*arrs)


def _sibling_send(arrs, name):
    n = len(arrs)

    def body(*refs):
        x, y, c = lax.axis_index("x"), lax.axis_index("y"), lax.axis_index("c")
        send_sems, recv_sems = refs[2 * n:]
        copies = [pltpu.make_async_remote_copy(src_ref=refs[a], dst_ref=refs[n + a], send_sem=send_sems.at[a],
                                               recv_sem=recv_sems.at[a], device_id=(x, y, 1 - c), device_id_type=MESH)
                  for a in range(n)]
        for cp in copies:
            cp.start()
        for cp in copies:
            cp.wait()

    hbm = pl.BlockSpec(memory_space=pl.ANY)
    return pl.pallas_call(
        body, name=name, out_shape=[jax.ShapeDtypeStruct(a.shape, a.dtype) for a in arrs],
        in_specs=[hbm] * n, out_specs=[hbm] * n,
        scratch_shapes=[pltpu.SemaphoreType.DMA((n,)), pltpu.SemaphoreType.DMA((n,))],
        compiler_params=_cparams(),
    )(*arrs)


def _row_tile(h):
    return h // 2 if (h // 2) % 8 == 0 else h


def _row_tiles(hs):
    tiles = [_row_tile(h) for h in hs]
    assert len({h // t for h, t in zip(hs, tiles)}) == 1
    return tiles, hs[0] // tiles[0]


def _chip_sums(arrs, gots, core, name, out_dtype=F32):
    n = len(arrs)
    tiles, steps = _row_tiles([a.shape[2] for a in arrs])

    def body(core_ref, *refs):
        for a_ref, b_ref, o_ref in zip(refs[:n], refs[n:2 * n], refs[2 * n:]):
            o_ref[...] = (a_ref[...] + b_ref[...]).astype(out_dtype)

    slabs = [pl.BlockSpec((None, tr, a.shape[3]), lambda s, i, core_ref: (s, i, 0)) for a, tr in zip(arrs, tiles)]
    grid_spec = pltpu.PrefetchScalarGridSpec(
        num_scalar_prefetch=1, grid=(N_CHIPS, steps),
        in_specs=[pl.BlockSpec((None, None, tr, a.shape[3]), lambda s, i, core_ref: (s, core_ref[0], i, 0))
                  for a, tr in zip(arrs, tiles)] + slabs,
        out_specs=slabs)
    return pl.pallas_call(body, name=name, grid_spec=grid_spec,
                          out_shape=[jax.ShapeDtypeStruct(g.shape, out_dtype) for g in gots],
                          compiler_params=_cparams())(core, *arrs, *gots)


def _sum4s(ps, name):
    tiles, steps = _row_tiles([p.shape[1] for p in ps])

    def body(*refs):
        for p_ref, o_ref in zip(refs[:len(ps)], refs[len(ps):]):
            o_ref[...] = ((p_ref[0].astype(F32) + p_ref[1].astype(F32)) + p_ref[2].astype(F32)) + p_ref[3].astype(F32)

    return pl.pallas_call(
        body, name=name, grid=(steps,), out_shape=[jax.ShapeDtypeStruct(p.shape[1:], F32) for p in ps],
        in_specs=[pl.BlockSpec((4, tr, p.shape[2]), lambda i: (0, i, 0)) for p, tr in zip(ps, tiles)],
        out_specs=[pl.BlockSpec((tr, p.shape[2]), lambda i: (i, 0)) for p, tr in zip(ps, tiles)],
        compiler_params=_cparams())(*ps)


def _adam_math(w, g, m, v):
    m2 = ADAM_B1 * m + (1.0 - ADAM_B1) * g
    v2 = ADAM_B2 * v + (1.0 - ADAM_B2) * (g * g)
    m_hat = m2 / (1.0 - ADAM_B1 ** ADAM_STEP)
    v_hat = v2 / (1.0 - ADAM_B2 ** ADAM_STEP)
    delta = -ADAM_LR * (m_hat / (jnp.sqrt(v_hat) + ADAM_EPS) + ADAM_WD * w)
    return delta, m2, v2


def _adamw(w, g, m, v, name):
    rows, cols = w.shape
    tr = rows // 4 if rows % 32 == 0 else rows

    def body(w_ref, g_ref, m_ref, v_ref, d_ref, m2_ref, v2_ref):
        delta, m2, v2 = _adam_math(w_ref[...], g_ref[...], m_ref[...], v_ref[...])
        d_ref[...] = delta
        m2_ref[...] = m2
        v2_ref[...] = v2

    spec = pl.BlockSpec((tr, cols), lambda i: (i, 0))
    shp = jax.ShapeDtypeStruct(w.shape, F32)
    return pl.pallas_call(body, name=name, grid=(rows // tr,), out_shape=[shp, shp, shp],
                          in_specs=[spec] * 4, out_specs=[spec] * 3, compiler_params=_cparams())(w, g, m, v)


def _small_update(g8, offsets, ws, ms, vs, loss_at, name):
    k = len(ws)

    def summed(g8_ref, lo, width):
        g = g8_ref[0:1, lo:lo + width]
        for r in range(1, N_DEV):
            g = g + g8_ref[r:r + 1, lo:lo + width]
        return g

    def body(g8_ref, *refs):
        ins, outs = refs[:3 * k], refs[3 * k:]
        for j in range(k):
            g = summed(g8_ref, offsets[j], ws[j].shape[1])
            delta, m2, v2 = _adam_math(ins[j][...], g, ins[k + j][...], ins[2 * k + j][...])
            for kind, val in enumerate((g, delta, m2, v2)):
                outs[kind * k + j][...] = val
        outs[4 * k][...] = summed(g8_ref, loss_at, LANES)

    vm = pl.BlockSpec(memory_space=pltpu.VMEM)
    shapes = [jax.ShapeDtypeStruct(w.shape, F32) for w in ws] * 4 + [jax.ShapeDtypeStruct((1, LANES), F32)]
    res = pl.pallas_call(body, name=name, out_shape=shapes, in_specs=[vm] * (1 + 3 * k), out_specs=[vm] * (4 * k + 1),
                         compiler_params=_cparams())(g8, *ws, *ms, *vs)
    return res[:k], res[k:2 * k], res[2 * k:3 * k], res[3 * k:4 * k], res[4 * k]


def _mod_shard(c8, w_ada, b_ada_shard, name):
    n = w_ada.shape[1]
    tn = 512

    def body(c_ref, w_ref, b_ref, o_ref, s_ref):
        cv = c_ref[...]
        sc = cv * (1.0 / (1.0 + jnp.exp(-cv)))
        s_ref[...] = sc
        o_ref[...] = _dot(sc.astype(BF16), w_ref[...].astype(BF16)) + b_ref[...]

    return pl.pallas_call(
        body, name=name, grid=(n // tn,),
        out_shape=[jax.ShapeDtypeStruct((8, n), F32), jax.ShapeDtypeStruct((8, D), F32)],
        in_specs=[pl.BlockSpec((8, D), lambda j: (0, 0)), pl.BlockSpec((D, tn), lambda j: (0, j)),
                  pl.BlockSpec((1, tn), lambda j: (0, j))],
        out_specs=[pl.BlockSpec((8, tn), lambda j: (0, j)), pl.BlockSpec((8, D), lambda j: (0, 0))],
        compiler_params=_cparams())(c8, w_ada, b_ada_shard)


def _layer_norm_stats(u):
    mu = jnp.mean(u, axis=1, keepdims=True)
    d = u - mu
    var = jnp.mean(d * d, axis=1, keepdims=True)
    rstd = lax.rsqrt(var + LN_EPS)
    return d * rstd, rstd


def _in_proj(x, vec, w_in, b_in, name):
    s = x.shape[0]
    tb = min(TOK_TILE, s)

    def body(x_ref, vec_ref, w_ref, b_ref, ht_ref, p_ref):
        h = x_ref[...] * (1.0 + vec_ref[V_SC_A:V_SC_A + 1, :]) + vec_ref[V_SH_A:V_SH_A + 1, :]
        hb = h.astype(BF16)
        ht_ref[...] = h.T.astype(BF16)
        proj = _dot(hb, w_ref[...]) + b_ref[...]
        col = lax.broadcasted_iota(jnp.int32, (1, D_IN), 1)
        is_q = (col < SB_W) | ((col >= 3 * SB_W) & (col < 3 * SB_W + SWA_QW))
        p_ref[...] = (proj * jnp.where(is_q, QK_SCALE, 1.0)).astype(BF16)

    return pl.pallas_call(
        body, name=name, grid=(s // tb,),
        out_shape=[jax.ShapeDtypeStruct((D, s), BF16), jax.ShapeDtypeStruct((s, D_IN), BF16)],
        in_specs=[pl.BlockSpec((tb, D), lambda i: (i, 0)), _resident((VEC_ROWS, D)), _resident((D, D_IN)),
                  _resident((1, D_IN))],
        out_specs=[pl.BlockSpec((D, tb), lambda i: (0, i)), pl.BlockSpec((tb, D_IN), lambda i: (i, 0))],
        compiler_params=_cparams())(x, vec, w_in, b_in)


def _softplus_parts(z):
    e1 = jnp.exp(-jnp.abs(z))
    sp = jnp.maximum(z, 0.0) + jnp.log(1.0 + e1)
    return sp, e1


def _sb_forward(proj, shards, name):
    s = proj.shape[0]
    tq, tk = min(SB_TQ, s), min(SB_TK, s)
    r = tq // tk

    n_sh = len(shards)
    nkb = SB_W // LANES
    nq = s // tq

    def body(q_ref, k_ref, v_ref, *refs):
        sh_refs, (o_ref, tot_ref, start_ref), got_refs = refs[:n_sh], refs[n_sh:n_sh + 3], refs[n_sh + 3:2 * n_sh + 3]
        acc_refs, run_refs = refs[2 * n_sh + 3:2 * n_sh + 5]
        i = pl.program_id(1)
        step = pl.program_id(0) * nq + i
        gather = _HalvesGather(sh_refs, got_refs, *refs[2 * n_sh + 5:])

        @pl.when(step == 0)
        def _():
            gather.start()

        @pl.when(step == nkb * nq // 2)
        def _():
            gather.forward()

        lane = lax.broadcasted_iota(jnp.int32, (1, LANES), 1)
        first = lane < HEAD_DIM
        qp = q_ref[...]
        zero = jnp.zeros((), BF16)
        qs = (jnp.where(first, qp, zero), jnp.where(first, zero, qp))
        later = _sum_matrix(tk, lambda row, col: row > col)
        acc_refs[...] = jnp.zeros_like(acc_refs)
        run_refs[...] = jnp.zeros_like(run_refs)

        def blocks(tiles):
            rows = [slice(r0, r0 + n) for r0, n, _, _ in tiles]
            kjs = [k_ref[pl.ds(pl.multiple_of(j * tk, tk), tk), :] for _, _, j, _ in tiles]
            vjs = [v_ref[pl.ds(pl.multiple_of(j * tk, tk), tk), :] for _, _, j, _ in tiles]
            chains = [(hd, t) for t in range(len(tiles)) for hd in range(2)]
            zs = [_dot_nt(qs[hd][rows[t]], kjs[t]) for hd, t in chains]
            sps = [_softplus_parts(z)[0] for z in zs]
            befores = [_before(i * tq + r0, n, j * tk, tk) if diag else None for r0, n, j, diag in tiles]
            spms = [sp if befores[t] is None else jnp.where(befores[t], sp, 0.0) for (hd, t), sp in zip(chains, sps)]
            cums = [_block_sums(spm, later) for spm in spms]
            runs, accs, ws = _RowState(run_refs, tk), _RowState(acc_refs, tk, add=True), []
            for (hd, t), z, sp, (cum, sm) in zip(chains, zs, sps, cums):
                r0, n = tiles[t][:2]
                run = runs.get(hd, r0, n)
                w = jnp.exp(z - sp - cum - _across(run, tk))
                if befores[t] is not None:
                    w = jnp.where(befores[t], w, 0.0)
                ws.append(w.astype(BF16))
                runs.put(hd, r0, n, run + sm)
            for (hd, t), pv in zip(chains, [_dot(w, vjs[t]) for (hd, t), w in zip(chains, ws)]):
                accs.accumulate(hd, *tiles[t][:2], pv)
            accs.store()
            runs.store()

        below = i * r
        diagonal = [(d * tk, tk, below + e, e == d) for d in range(r) for e in range(d, -1, -1)]

        @pl.when(i == 0)
        def _():
            blocks(diagonal)

        @pl.when(i > 0)
        def _():
            blocks(diagonal + [(0, tq, below - 1, False)])

        def swept_mass():
            return jnp.min(jnp.minimum(run_refs[0], run_refs[1]))

        def more(carry):
            n, mass = carry
            return (n < below) & (mass < SB_DEAD_MASS)

        def sweep(carry):
            n, _ = carry
            blocks([(0, tq, below - 1 - n, False)])
            return n + 1, swept_mass()

        n_swept, _ = lax.while_loop(more, sweep, (jnp.minimum(below, 1), swept_mass()))
        start_ref[pl.program_id(0), i] = (below - n_swept).astype(F32)
        o_ref[...] = jnp.where(first, acc_refs[0], acc_refs[1])
        tot_ref[...] = jnp.where(first, run_refs[0], run_refs[1])

        @pl.when(step == nkb * nq - 1)
        def _():
            gather.wait()

    shp = jax.ShapeDtypeStruct((s, SB_W), F32)
    qspec = pl.BlockSpec((tq, LANES), lambda p, i: (i, p))
    hbm = pl.BlockSpec(memory_space=pl.ANY)
    return pl.pallas_call(
        body, name=name, grid=(nkb, nq),
        out_shape=[shp, shp, jax.ShapeDtypeStruct((nkb, nq), F32)]
        + [jax.ShapeDtypeStruct((N_CHIPS,) + a.shape, a.dtype) for a in shards],
        in_specs=[qspec,
                  pl.BlockSpec((s, LANES), lambda p, i: (0, nkb + p)),
                  pl.BlockSpec((s, LANES), lambda p, i: (0, 2 * nkb + p))] + [hbm] * n_sh,
        out_specs=[qspec, qspec, pl.BlockSpec(memory_space=pltpu.SMEM)] + [hbm] * n_sh,
        scratch_shapes=[pltpu.VMEM((2, tq, LANES), F32), pltpu.VMEM((2, tq, LANES), F32)] + _halves_gather_sems(n_sh),
        compiler_params=_cparams())(proj, proj, proj, *shards)


def _swa_masks(n):
    ti = lax.broadcasted_iota(jnp.int32, (WINDOW, 2 * WINDOW), 0)
    kj = lax.broadcasted_iota(jnp.int32, (WINDOW, 2 * WINDOW), 1)
    dist = ti + WINDOW - kj
    valid = (dist >= 0) & (dist < WINDOW) & ((n * WINDOW - WINDOW + kj) >= 0)
    return valid, dist.astype(F32)


def _swa_probs(sc, valid, distf, h, sink):
    slope = 2.0 ** (-(h + 1))
    sc = jnp.where(valid, sc - slope * distf, MASK_VALUE)
    mx = jnp.maximum(jnp.max(sc, axis=1, keepdims=True), sink)
    p = jnp.exp(sc - mx)
    es = jnp.exp(sink - mx)
    inv = 1.0 / (jnp.sum(p, axis=1, keepdims=True) + es)
    return p * inv, es * inv


def _swa_forward(proj, sinks, shards, name):
    s = proj.shape[0]
    nb = s // WINDOW
    qb, kb, vb = 3 * SB_W // SWA_QW, (3 * SB_W + SWA_QW) // LANES, (3 * SB_W + SWA_QW + SWA_KW) // LANES
    n_sh = len(shards)

    def body(q_ref, kp_ref, kc_ref, vp_ref, vc_ref, sink_ref, *refs):
        sh_refs, o_ref, got_refs = refs[:n_sh], refs[n_sh], refs[n_sh + 1:2 * n_sh + 1]
        n = pl.program_id(0)
        gather = _HalvesGather(sh_refs, got_refs, *refs[2 * n_sh + 1:])

        @pl.when(n == 0)
        def _():
            gather.start()

        @pl.when(n == nb // 2)
        def _():
            gather.forward()

        k = jnp.concatenate([kp_ref[...], kc_ref[...]], axis=0)
        v = jnp.concatenate([vp_ref[...], vc_ref[...]], axis=0)
        k_sw = pltpu.roll(k.astype(F32), HEAD_DIM, 1).astype(BF16)
        v_sw = pltpu.roll(v.astype(F32), HEAD_DIM, 1).astype(BF16)
        lane = lax.broadcasted_iota(jnp.int32, (1, LANES), 1)
        halves = [lane < HEAD_DIM, lane >= HEAD_DIM]
        valid, distf = _swa_masks(n)
        heads = range(2 * 4)
        qms = [jnp.where(halves[h % 2], q_ref[:, (h // 2) * LANES:(h // 2 + 1) * LANES], jnp.zeros((), BF16))
               for h in heads]
        kus = [k if h // 4 == h % 2 else k_sw for h in heads]
        vus = [v if h // 4 == h % 2 else v_sw for h in heads]
        scores = [_dot_nt(qms[h], kus[h]) for h in heads]
        ps = [_swa_probs(scores[h], valid, distf, h, sink_ref[h])[0].astype(BF16) for h in heads]
        outs = [_dot(ps[h], vus[h]) for h in heads]
        for pair in range(4):
            o_ref[:, pair * LANES:(pair + 1) * LANES] = jnp.where(halves[0], outs[2 * pair], outs[2 * pair + 1])

        @pl.when(n == nb - 1)
        def _():
            gather.wait()

    prev = lambda n: jnp.maximum(n - 1, 0)
    hbm = pl.BlockSpec(memory_space=pl.ANY)
    return pl.pallas_call(
        body, name=name, grid=(nb,),
        out_shape=[jax.ShapeDtypeStruct((s, SWA_QW), F32)]
        + [jax.ShapeDtypeStruct((N_CHIPS,) + a.shape, a.dtype) for a in shards],
        in_specs=[pl.BlockSpec((WINDOW, SWA_QW), lambda n: (n, qb)),
                  pl.BlockSpec((WINDOW, LANES), lambda n: (prev(n), kb)),
                  pl.BlockSpec((WINDOW, LANES), lambda n: (n, kb)),
                  pl.BlockSpec((WINDOW, LANES), lambda n: (prev(n), vb)),
                  pl.BlockSpec((WINDOW, LANES), lambda n: (n, vb)),
                  pl.BlockSpec(memory_space=pltpu.SMEM)] + [hbm] * n_sh,
        out_specs=[pl.BlockSpec((WINDOW, SWA_QW), lambda n: (n, 0))] + [hbm] * n_sh,
        scratch_shapes=_halves_gather_sems(n_sh),
        compiler_params=_cparams())(proj, proj, proj, proj, proj, sinks, *shards)


def _rms_parts(y):
    return lax.rsqrt(jnp.mean(y * y, axis=1, keepdims=True) + RMS_EPS)


def _post_attention(y_sb, y_sw, x, vec, w_out, name):
    s = x.shape[0]
    tb = min(TOK_TILE, s)

    def body(ysb_ref, ysw_ref, x_ref, vec_ref, w_ref, mixedt_ref, attn_ref, x1_ref, h2_ref, h2t_ref):
        ysb, ysw = ysb_ref[...], ysw_ref[...]
        nsb_f = ysb * _rms_parts(ysb) * vec_ref[V_GN:V_GN + 1, :SB_W]
        nsw_f = ysw * _rms_parts(ysw) * vec_ref[V_GN:V_GN + 1, SB_W:]
        nsb, nsw = nsb_f.astype(BF16), nsw_f.astype(BF16)
        mixedt_ref[:SB_W, :] = nsb_f.T.astype(BF16)
        mixedt_ref[SB_W:, :] = nsw_f.T.astype(BF16)
        attn = _dot(nsb, w_ref[:SB_W, :]) + _dot(nsw, w_ref[SB_W:, :])
        attn_ref[...] = attn
        u1 = ALPHA * x_ref[...] + (1.0 + vec_ref[V_G_A:V_G_A + 1, :]) * attn
        xhat, _ = _layer_norm_stats(u1)
        x1 = xhat * vec_ref[V_LN1G:V_LN1G + 1, :] + vec_ref[V_LN1B:V_LN1B + 1, :]
        x1_ref[...] = x1
        h2 = x1 * (1.0 + vec_ref[V_SC_F:V_SC_F + 1, :]) + vec_ref[V_SH_F:V_SH_F + 1, :]
        h2_ref[...] = h2.astype(BF16)
        h2t_ref[...] = h2.T.astype(BF16)

    half = pl.BlockSpec((tb, SB_W), lambda i: (i, 0))
    full = pl.BlockSpec((tb, D), lambda i: (i, 0))
    full_t = pl.BlockSpec((D, tb), lambda i: (0, i))
    return pl.pallas_call(
        body, name=name, grid=(s // tb,),
        out_shape=[jax.ShapeDtypeStruct((D, s), BF16), jax.ShapeDtypeStruct((s, D), F32),
                   jax.ShapeDtypeStruct((s, D), F32), jax.ShapeDtypeStruct((s, D), BF16),
                   jax.ShapeDtypeStruct((D, s), BF16)],
        in_specs=[half, half, full, _resident((VEC_ROWS, D)), _resident((D, D))],
        out_specs=[full_t, full, full, full, full_t],
        compiler_params=_cparams())(y_sb, y_sw, x, vec, w_out)


def _ffn_forward(h2, w_gu, w_down, name):
    s = h2.shape[0]
    tb = min(FFN_TILE, s)

    def body(h_ref, wgu_ref, wd_ref, gu_ref, actt_ref, ffn_ref):
        gu = _dot(h_ref[...], wgu_ref[...])
        gu_ref[...] = gu.astype(BF16)
        gate, up = gu[:, :D_FF], gu[:, D_FF:]
        act = gate * (1.0 / (1.0 + jnp.exp(-gate))) * up
        actt_ref[...] = act.T.astype(BF16)
        ffn_ref[...] = _dot(act.astype(BF16), wd_ref[...])

    return pl.pallas_call(
        body, name=name, grid=(s // tb,),
        out_shape=[jax.ShapeDtypeStruct((s, 2 * D_FF), BF16), jax.ShapeDtypeStruct((D_FF, s), BF16),
                   jax.ShapeDtypeStruct((s, D), F32)],
        in_specs=[pl.BlockSpec((tb, D), lambda i: (i, 0)), _resident((D, 2 * D_FF)), _resident((D_FF, D))],
        out_specs=[pl.BlockSpec((tb, 2 * D_FF), lambda i: (i, 0)), pl.BlockSpec((D_FF, tb), lambda i: (0, i)),
                   pl.BlockSpec((tb, D), lambda i: (i, 0))],
        compiler_params=_cparams())(h2, w_gu, w_down)


def _layer_norm_bwd(dxhat, xhat, rstd):
    m1 = jnp.mean(dxhat, axis=1, keepdims=True)
    m2 = jnp.mean(dxhat * xhat, axis=1, keepdims=True)
    return rstd * (dxhat - m1 - xhat * m2)


def _colsum(a):
    return jnp.sum(a, axis=0, keepdims=True)


A_LN2G, A_LN2B, A_GF, A_SCF, A_SHF, A_LOSS = range(6)
B_LN1G, B_LN1B, B_GA, B_GN = range(4)
C_SCA, C_SHA = range(2)


def _ffn_backward(x1, ffn, target, gu, vec, w_gu, w_down, name):
    s = x1.shape[0]
    tb = min(FFN_BWD_TILE, s)

    def body(x1_ref, ffn_ref, t_ref, gu_ref, vec_ref, wgu_ref, wd_ref, dffn_ref, dgu_ref, dx1_ref, acc_ref):
        @pl.when(pl.program_id(0) == 0)
        def _():
            acc_ref[...] = jnp.zeros_like(acc_ref)

        x1v, ffn_v = x1_ref[...], ffn_ref[...]
        g_f = 1.0 + vec_ref[V_G_F:V_G_F + 1, :]
        u2 = ALPHA * x1v + g_f * ffn_v
        xhat, rstd = _layer_norm_stats(u2)
        ln_g = vec_ref[V_LN2G:V_LN2G + 1, :]
        err = xhat * ln_g + vec_ref[V_LN2B:V_LN2B + 1, :] - t_ref[...]
        dx2 = err * (1.0 / D)
        acc_ref[A_LOSS:A_LOSS + 1, :] += _colsum(err * err) * (0.5 / D)
        acc_ref[A_LN2G:A_LN2G + 1, :] += _colsum(dx2 * xhat)
        acc_ref[A_LN2B:A_LN2B + 1, :] += _colsum(dx2)
        du2 = _layer_norm_bwd(dx2 * ln_g, xhat, rstd)
        acc_ref[A_GF:A_GF + 1, :] += _colsum(du2 * ffn_v)
        dffn = (g_f * du2).astype(BF16)
        dffn_ref[...] = dffn
        dact = _dot_nt(dffn, wd_ref[...])
        gate, up = gu_ref[:, :D_FF].astype(F32), gu_ref[:, D_FF:].astype(F32)
        sg = 1.0 / (1.0 + jnp.exp(-gate))
        dgate = (dact * up * (sg * (1.0 + gate * (1.0 - sg)))).astype(BF16)
        dup = (dact * (gate * sg)).astype(BF16)
        dgu_ref[:, :D_FF] = dgate
        dgu_ref[:, D_FF:] = dup
        dh2 = _dot_nt(dgate, wgu_ref[:, :D_FF]) + _dot_nt(dup, wgu_ref[:, D_FF:])
        dx1_ref[...] = ALPHA * du2 + dh2 * (1.0 + vec_ref[V_SC_F:V_SC_F + 1, :])
        acc_ref[A_SCF:A_SCF + 1, :] += _colsum(dh2 * x1v)
        acc_ref[A_SHF:A_SHF + 1, :] += _colsum(dh2)

    full = pl.BlockSpec((tb, D), lambda i: (i, 0))
    wide = pl.BlockSpec((tb, 2 * D_FF), lambda i: (i, 0))
    return pl.pallas_call(
        body, name=name, grid=(s // tb,),
        out_shape=[jax.ShapeDtypeStruct((s, D), BF16), jax.ShapeDtypeStruct((s, 2 * D_FF), BF16),
                   jax.ShapeDtypeStruct((s, D), F32), jax.ShapeDtypeStruct((8, D), F32)],
        in_specs=[full, full, full, wide, _resident((VEC_ROWS, D)), _resident((D, 2 * D_FF)), _resident((D_FF, D))],
        out_specs=[full, wide, full, pl.BlockSpec((8, D), lambda i: (0, 0))],
        compiler_params=_cparams())(x1, ffn, target, gu, vec, w_gu, w_down)


def _attn_out_backward(dx1, x, attn, y_sb, y_sw, vec, w_out, name):
    s = x.shape[0]
    tb = min(TOK_TILE, s)

    def body(dx1_ref, x_ref, attn_ref, ysb_ref, ysw_ref, vec_ref, w_ref, du1_ref, dattn_ref, dy_ref, acc_ref):
        @pl.when(pl.program_id(0) == 0)
        def _():
            acc_ref[...] = jnp.zeros_like(acc_ref)

        attn = attn_ref[...]
        g_a = 1.0 + vec_ref[V_G_A:V_G_A + 1, :]
        xhat, rstd = _layer_norm_stats(ALPHA * x_ref[...] + g_a * attn)
        dx1v = dx1_ref[...]
        acc_ref[B_LN1G:B_LN1G + 1, :] += _colsum(dx1v * xhat)
        acc_ref[B_LN1B:B_LN1B + 1, :] += _colsum(dx1v)
        du1 = _layer_norm_bwd(dx1v * vec_ref[V_LN1G:V_LN1G + 1, :], xhat, rstd)
        du1_ref[...] = du1
        acc_ref[B_GA:B_GA + 1, :] += _colsum(du1 * attn)
        dattn = (g_a * du1).astype(BF16)
        dattn_ref[...] = dattn
        dmixed = _dot_nt(dattn, w_ref[...])
        for lo, y_ref in ((0, ysb_ref), (SB_W, ysw_ref)):
            y = y_ref[...]
            rr = _rms_parts(y)
            dn = dmixed[:, lo:lo + SB_W]
            acc_ref[B_GN:B_GN + 1, lo:lo + SB_W] += _colsum(dn * y * rr)
            dng = dn * vec_ref[V_GN:V_GN + 1, lo:lo + SB_W]
            dy_ref[:, lo:lo + SB_W] = rr * dng - y * (rr * rr * rr) * jnp.mean(dng * y, axis=1, keepdims=True)

    half = pl.BlockSpec((tb, SB_W), lambda i: (i, 0))
    full = pl.BlockSpec((tb, D), lambda i: (i, 0))
    return pl.pallas_call(
        body, name=name, grid=(s // tb,),
        out_shape=[jax.ShapeDtypeStruct((s, D), F32), jax.ShapeDtypeStruct((s, D), BF16),
                   jax.ShapeDtypeStruct((s, D), F32), jax.ShapeDtypeStruct((8, D), F32)],
        in_specs=[full, full, full, half, half, _resident((VEC_ROWS, D)), _resident((D, D))],
        out_specs=[full, full, full, pl.BlockSpec((8, D), lambda i: (0, 0))],
        compiler_params=_cparams())(dx1, x, attn, y_sb, y_sw, vec, w_out)


def _sb_backward(proj, sp_total, sweep_start, dy, slabs, name):
    s = proj.shape[0]
    tq, tk = min(SB_TQ, s), min(SB_TK, s)
    r = tq // tk
    nkb = SB_W // LANES
    nq = s // tq

    n_sl = len(slabs)

    def body(q_ref, k_ref, v_ref, tot_ref, do_ref, start_ref, *refs):
        slab_refs, (dq_ref, dk_ref, dv_ref), got_refs = refs[:n_sl], refs[n_sl:n_sl + 3], refs[n_sl + 3:2 * n_sl + 3]
        dq_acc, left_refs, gsum_refs = refs[2 * n_sl + 3:2 * n_sl + 6]
        i = pl.program_id(1)
        step = pl.program_id(0) * nq + i
        scatter = _scatter_exchange(slab_refs, got_refs, *refs[2 * n_sl + 6:])

        @pl.when(step == 0)
        def _():
            scatter.start()

        @pl.when(i == 0)
        def _():
            dk_ref[...] = jnp.zeros_like(dk_ref)
            dv_ref[...] = jnp.zeros_like(dv_ref)

        lane = lax.broadcasted_iota(jnp.int32, (1, LANES), 1)
        first = lane < HEAD_DIM
        qp, dop, totp = q_ref[...], do_ref[...], tot_ref[...]
        zero = jnp.zeros((), BF16)
        qs = (jnp.where(first, qp, zero), jnp.where(first, zero, qp))
        dofs = (jnp.where(first, dop, 0.0), jnp.where(first, 0.0, dop))
        dobs = tuple(d.astype(BF16) for d in dofs)
        dots = tuple(d.T.astype(BF16) for d in dofs)
        qts = tuple(qh.astype(F32).T.astype(BF16) for qh in qs)
        later = _sum_matrix(tk, lambda row, col: row > col)
        earlier = _sum_matrix(tk, lambda row, col: row < col)
        dq_acc[...] = jnp.zeros_like(dq_acc)
        gsum_refs[...] = jnp.zeros_like(gsum_refs)
        swapped = pltpu.roll(totp, HEAD_DIM, 1)
        left_refs[0] = jnp.where(first, totp, swapped)
        left_refs[1] = jnp.where(first, swapped, totp)

        def blocks(tiles):
            rows = [slice(r0, r0 + n) for r0, n, _, _ in tiles]
            kjs = [k_ref[pl.ds(pl.multiple_of(j * tk, tk), tk), :] for _, _, j, _ in tiles]
            vjs = [v_ref[pl.ds(pl.multiple_of(j * tk, tk), tk), :] for _, _, j, _ in tiles]
            chains = [(hd, t) for t in range(len(tiles)) for hd in range(2)]
            zs = [_dot_nt(qs[hd][rows[t]], kjs[t]) for hd, t in chains]
            dws = [_dot_nt(dobs[hd][rows[t]], vjs[t]) for hd, t in chains]
            parts = [_softplus_parts(z) for z in zs]
            sps = [p[0] for p in parts]
            befores = [_before(i * tq + r0, n, j * tk, tk) if diag else None for r0, n, j, diag in tiles]
            spms = [sp if befores[t] is None else jnp.where(befores[t], sp, 0.0) for (hd, t), sp in zip(chains, sps)]
            cums = [_block_sums(spm, later) for spm in spms]
            lefts, gsums, dq_sums = _RowState(left_refs, tk), _RowState(gsum_refs, tk), _RowState(dq_acc, tk, add=True)
            ws = []
            for (hd, t), z, sp, (cum, sm) in zip(chains, zs, sps, cums):
                r0, n = tiles[t][:2]
                left = lefts.get(hd, r0, n) - sm
                lefts.put(hd, r0, n, left)
                w = jnp.exp(z - sp - cum - _across(left, tk))
                ws.append(w if befores[t] is None else jnp.where(befores[t], w, 0.0))
            wbs = [w.astype(BF16) for w in ws]
            dvs = [_dot(dots[hd][:, rows[t]], wb) for (hd, t), wb in zip(chains, wbs)]
            gs = [dw * w for dw, w in zip(dws, ws)]
            gcums = [_block_sums(g, earlier) for g in gs]
            dzbs = []
            for (hd, t), z, (sp, e1), g, (gcum, gsm) in zip(chains, zs, parts, gs, gcums):
                r0, n = tiles[t][:2]
                gsum = gsums.get(hd, r0, n)
                inv = 1.0 / (1.0 + e1)
                sig = jnp.where(z >= 0.0, inv, e1 * inv)
                dz = g - sig * (g + _across(gsum, tk) + gcum)
                dzbs.append((dz if befores[t] is None else jnp.where(befores[t], dz, 0.0)).astype(BF16))
                gsums.put(hd, r0, n, gsum + gsm)
            dqs = [_dot(dzb, kjs[t]) for (hd, t), dzb in zip(chains, dzbs)]
            dks = [_dot(qts[hd][:, rows[t]], dzb) for (hd, t), dzb in zip(chains, dzbs)]
            for t, (_, _, j, _) in enumerate(tiles):
                dv_ref[j] += dvs[2 * t] + dvs[2 * t + 1]
                dk_ref[j] += dks[2 * t] + dks[2 * t + 1]
            for (hd, t), dq in zip(chains, dqs):
                dq_sums.accumulate(hd, *tiles[t][:2], dq)
            dq_sums.store()
            lefts.store()
            gsums.store()

        below = i * r
        start = jnp.clip(start_ref[pl.program_id(0), i].astype(jnp.int32), 0, below)

        def sweep(n, carry):
            blocks([(0, tq, start + n, False)])
            return carry

        lax.fori_loop(0, jnp.maximum(below - 1 - start, 0), sweep, 0)
        diagonal = [(d * tk, tk, below + e, e == d) for d in range(r) for e in range(d + 1)]

        @pl.when(i == 0)
        def _():
            blocks(diagonal)

        @pl.when(i > 0)
        def _():
            blocks([(0, tq, below - 1, False)] + diagonal)
        dq_ref[...] = jnp.where(first, dq_acc[0], dq_acc[1])

        @pl.when(step == nkb * nq - 1)
        def _():
            scatter.wait()

    shp = jax.ShapeDtypeStruct((s, SB_W), F32)
    qspec = pl.BlockSpec((tq, LANES), lambda p, i: (i, p))
    whole = pl.BlockSpec((None, s // tk, LANES, tk), lambda p, i: (p, 0, 0, 0))
    shp_t = jax.ShapeDtypeStruct((nkb, s // tk, LANES, tk), F32)
    hbm = pl.BlockSpec(memory_space=pl.ANY)
    return pl.pallas_call(
        body, name=name, grid=(nkb, nq),
        out_shape=[shp, shp_t, shp_t] + [jax.ShapeDtypeStruct(p.shape, p.dtype) for p in slabs],
        in_specs=[qspec,
                  pl.BlockSpec((s, LANES), lambda p, i: (0, nkb + p)),
                  pl.BlockSpec((s, LANES), lambda p, i: (0, 2 * nkb + p)),
                  qspec, qspec, pl.BlockSpec(memory_space=pltpu.SMEM)] + [hbm] * n_sl,
        out_specs=[qspec, whole, whole] + [hbm] * n_sl,
        scratch_shapes=[pltpu.VMEM((2, tq, LANES), F32), pltpu.VMEM((2, tq, LANES), F32), pltpu.VMEM((2, tq, LANES), F32)]
        + _exchange_sems(n_sl),
        compiler_params=_cparams())(proj, proj, proj, sp_total, dy, sweep_start, *slabs)


def _swa_backward(proj, y_sw, dy, sinks, gives, name):
    s = proj.shape[0]
    nb = s // WINDOW
    qb, kb, vb = 3 * SB_W // SWA_QW, (3 * SB_W + SWA_QW) // LANES, (3 * SB_W + SWA_QW + SWA_KW) // LANES

    n_gv = len(gives)

    def body(q_ref, kp_ref, kc_ref, vp_ref, vc_ref, o_ref, do_ref, sink_ref, *refs):
        give_refs, (dq_ref, dk_ref, dv_ref, ds_ref), got_refs = refs[:n_gv], refs[n_gv:n_gv + 4], refs[n_gv + 4:2 * n_gv + 4]
        n = pl.program_id(0)
        swap = _sibling_halves(give_refs, got_refs, *refs[2 * n_gv + 4:])

        @pl.when(n == 0)
        def _():
            for cp in swap:
                cp.start()

        @pl.when(n == 0)
        def _():
            dk_ref[...] = jnp.zeros_like(dk_ref)
            dv_ref[...] = jnp.zeros_like(dv_ref)
            ds_ref[...] = jnp.zeros_like(ds_ref)

        k = jnp.concatenate([kp_ref[...], kc_ref[...]], axis=0)
        v = jnp.concatenate([vp_ref[...], vc_ref[...]], axis=0)
        k_sw = pltpu.roll(k.astype(F32), HEAD_DIM, 1).astype(BF16)
        v_sw = pltpu.roll(v.astype(F32), HEAD_DIM, 1).astype(BF16)
        lane = lax.broadcasted_iota(jnp.int32, (1, LANES), 1)
        halves = [lane < HEAD_DIM, lane >= HEAD_DIM]
        valid, distf = _swa_masks(n)
        heads = range(2 * 4)
        cols = [slice((h // 2) * LANES, (h // 2 + 1) * LANES) for h in heads]
        qms = [jnp.where(halves[h % 2], q_ref[:, cols[h]], jnp.zeros((), BF16)) for h in heads]
        dos = [jnp.where(halves[h % 2], do_ref[:, cols[h]], 0.0) for h in heads]
        dobs = [d.astype(BF16) for d in dos]
        native = [h // 4 == h % 2 for h in heads]
        kus = [k if native[h] else k_sw for h in heads]
        vus = [v if native[h] else v_sw for h in heads]
        scores = [_dot_nt(qms[h], kus[h]) for h in heads]
        dps = [_dot_nt(dobs[h], vus[h]) for h in heads]
        deltas = [jnp.sum(dos[h] * o_ref[:, cols[h]], axis=1, keepdims=True) for h in heads]
        probs = [_swa_probs(scores[h], valid, distf, h, sink_ref[h]) for h in heads]
        pbs = [probs[h][0].astype(BF16) for h in heads]
        dscs = [(probs[h][0] * (dps[h] - deltas[h])).astype(BF16) for h in heads]
        dqs = [_dot(dscs[h], kus[h]) for h in heads]
        dks = [_dot_tn(dscs[h], qms[h]) for h in heads]
        dvs = [_dot_tn(pbs[h], dobs[h]) for h in heads]
        for h in heads:
            ds_ref[h:h + 1, :] += jnp.zeros((1, LANES), F32) - jnp.sum(probs[h][1] * deltas[h])
        for pair in range(4):
            dq_ref[:, cols[2 * pair]] = jnp.where(halves[0], dqs[2 * pair], dqs[2 * pair + 1])

        def gathered(parts):
            nat = sum(parts[h] for h in heads if native[h])
            rot = sum(parts[h] for h in heads if not native[h])
            return nat + pltpu.roll(rot, HEAD_DIM, 1)

        dk, dv = gathered(dks), gathered(dvs)
        prev = pl.multiple_of(jnp.maximum(n - 1, 0) * WINDOW, WINDOW)
        cur = pl.multiple_of(n * WINDOW, WINDOW)
        dk_ref[pl.ds(prev, WINDOW), :] += dk[:WINDOW]
        dv_ref[pl.ds(prev, WINDOW), :] += dv[:WINDOW]
        dk_ref[pl.ds(cur, WINDOW), :] += dk[WINDOW:]
        dv_ref[pl.ds(cur, WINDOW), :] += dv[WINDOW:]

        @pl.when(n == nb - 1)
        def _():
            for cp in swap:
                cp.wait()

    prev_blk = lambda n: jnp.maximum(n - 1, 0)
    wide = pl.BlockSpec((WINDOW, SWA_QW), lambda n: (n, 0))
    whole = pl.BlockSpec((s, LANES), lambda n: (0, 0))
    hbm = pl.BlockSpec(memory_space=pl.ANY)
    return pl.pallas_call(
        body, name=name, grid=(nb,),
        out_shape=[jax.ShapeDtypeStruct((s, SWA_QW), F32), jax.ShapeDtypeStruct((s, LANES), F32),
                   jax.ShapeDtypeStruct((s, LANES), F32), jax.ShapeDtypeStruct((8, LANES), F32)] + _halves_shapes(gives),
        in_specs=[pl.BlockSpec((WINDOW, SWA_QW), lambda n: (n, qb)),
                  pl.BlockSpec((WINDOW, LANES), lambda n: (prev_blk(n), kb)),
                  pl.BlockSpec((WINDOW, LANES), lambda n: (n, kb)),
                  pl.BlockSpec((WINDOW, LANES), lambda n: (prev_blk(n), vb)),
                  pl.BlockSpec((WINDOW, LANES), lambda n: (n, vb)),
                  wide,
                  pl.BlockSpec((WINDOW, SWA_QW), lambda n: (n, 1)),
                  pl.BlockSpec(memory_space=pltpu.SMEM)] + [hbm] * n_gv,
        out_specs=[wide, whole, whole, pl.BlockSpec((8, LANES), lambda n: (0, 0))] + [hbm] * n_gv,
        scratch_shapes=_halves_sems(n_gv),
        compiler_params=_cparams())(proj, proj, proj, proj, proj, y_sw, dy, sinks, *gives)


def _in_proj_backward(dq_sb, dkt_sb, dvt_sb, dq_sw, dk_sw, dv_sw, du1, x, vec, w_in, name):
    s = x.shape[0]
    tb = min(TOK_TILE, s)
    n_pairs, _, _, tk = dkt_sb.shape

    def body(dqsb_ref, dktsb_ref, dvtsb_ref, dqsw_ref, dksw_ref, dvsw_ref, du1_ref, x_ref, vec_ref, w_ref,
             dproj_ref, gx_ref, acc_ref, bacc_ref):
        @pl.when(pl.program_id(0) == 0)
        def _():
            acc_ref[...] = jnp.zeros_like(acc_ref)
            bacc_ref[...] = jnp.zeros_like(bacc_ref)

        pieces = ((0, dqsb_ref, QK_SCALE), (3 * SB_W, dqsw_ref, QK_SCALE), (3 * SB_W + SWA_QW, dksw_ref, 1.0),
                  (3 * SB_W + SWA_QW + SWA_KW, dvsw_ref, 1.0))
        for lo, ref, scale in pieces:
            width = ref.shape[1]
            piece = ref[...] * scale
            bacc_ref[0:1, lo:lo + width] += _colsum(piece)
            dproj_ref[:, lo:lo + width] = piece.astype(BF16)
        for base, ref in ((SB_W, dktsb_ref), (2 * SB_W, dvtsb_ref)):
            for p in range(n_pairs):
                lo = base + p * LANES
                for jj in range(tb // tk):
                    piece = ref[p, jj].T
                    bacc_ref[0:1, lo:lo + LANES] += _colsum(piece)
                    dproj_ref[jj * tk:(jj + 1) * tk, lo:lo + LANES] = piece.astype(BF16)
        dh = _dot_nt(dproj_ref[...], w_ref[...])
        xv = x_ref[...]
        gx_ref[...] = ALPHA * du1_ref[...] + dh * (1.0 + vec_ref[V_SC_A:V_SC_A + 1, :])
        acc_ref[C_SCA:C_SCA + 1, :] += _colsum(dh * xv)
        acc_ref[C_SHA:C_SHA + 1, :] += _colsum(dh)

    half = pl.BlockSpec((tb, SB_W), lambda i: (i, 0))
    narrow = pl.BlockSpec((tb, LANES), lambda i: (i, 0))
    full = pl.BlockSpec((tb, D), lambda i: (i, 0))
    blocks_t = pl.BlockSpec((n_pairs, tb // tk, LANES, tk), lambda i: (0, i, 0, 0))
    return pl.pallas_call(
        body, name=name, grid=(s // tb,),
        out_shape=[jax.ShapeDtypeStruct((s, D_IN), BF16), jax.ShapeDtypeStruct((s, D), F32),
                   jax.ShapeDtypeStruct((8, D), F32), jax.ShapeDtypeStruct((8, D_IN), F32)],
        in_specs=[half, blocks_t, blocks_t, half, narrow, narrow, full, full, _resident((VEC_ROWS, D)),
                  _resident((D, D_IN))],
        out_specs=[pl.BlockSpec((tb, D_IN), lambda i: (i, 0)), full, pl.BlockSpec((8, D), lambda i: (0, 0)),
                   pl.BlockSpec((8, D_IN), lambda i: (0, 0))],
        compiler_params=_cparams())(dq_sb, dkt_sb, dvt_sb, dq_sw, dk_sw, dv_sw, du1, x, vec, w_in)


def _weight_grad(at, b, name, col_shards=1):
    m, s = at.shape
    n = b.shape[1]
    if col_shards > 1:
        tn = n // col_shards
        out_shape = jax.ShapeDtypeStruct((col_shards, m, tn), F32)
        out_spec = pl.BlockSpec((None, m, tn), lambda j, k: (j, 0, 0))
    else:
        tn = 512 if n % 512 == 0 else n
        out_shape = jax.ShapeDtypeStruct((m, n), F32)
        out_spec = pl.BlockSpec((m, tn), lambda j, k: (0, j))
    ts = min(WGRAD_TOKENS, s)
    while 2 * (m * ts * 2 + ts * tn * 2 + m * tn * 4) > WGRAD_VMEM and ts > 512:
        ts //= 2

    def body(at_ref, b_ref, o_ref):
        @pl.when(pl.program_id(1) == 0)
        def _():
            o_ref[...] = jnp.zeros_like(o_ref)

        o_ref[...] += _dot(at_ref[...], b_ref[...])

    return pl.pallas_call(
        body, name=name, grid=(n // tn, s // ts),
        out_shape=out_shape,
        in_specs=[pl.BlockSpec((m, ts), lambda j, k: (0, k)), pl.BlockSpec((ts, tn), lambda j, k: (k, j))],
        out_specs=out_spec,
        compiler_params=_cparams())(at, b)


def _pad_rows(v, rows):
    return jnp.concatenate([v, jnp.zeros((rows - v.shape[0], v.shape[1]), v.dtype)], axis=0)


def _col_shards(w, n_shards):
    r, n = w.shape
    return w.reshape(r, n_shards, n // n_shards).transpose(1, 0, 2)


def kernel(x, c, w_ada, b_ada, w_in, b_in, sinks, gn_sb, gn_swa, w_out, ln1_g, ln1_b, w_gu, w_down, ln2_g, ln2_b, loss_target, m_w_ada, m_b_ada, m_w_in, m_b_in, m_sinks, m_gn_sb, m_gn_swa, m_w_out, m_ln1_g, m_ln1_b, m_w_gu, m_w_down, m_ln2_g, m_ln2_b, v_w_ada, v_b_ada, v_w_in, v_b_in, v_sinks, v_gn_sb, v_gn_swa, v_w_out, v_ln1_g, v_ln1_b, v_w_gu, v_w_down, v_ln2_g, v_ln2_b):
    ix, iy, ic = lax.axis_index("x"), lax.axis_index("y"), lax.axis_index("c")
    chip = 2 * ix + iy
    dev = 4 * ix + 2 * iy + ic
    xs, target = x[0], loss_target[0]
    s = xs.shape[0]

    c_rows, g_in = _allgather8(_pad_rows(c, 8), "gather_c", gather=[w_in[0].astype(BF16)])
    c_all = c_rows[::8]
    n_ada = w_ada.shape[2]
    b_ada_shard = lax.dynamic_slice_in_dim(b_ada, chip * n_ada, n_ada, axis=1)
    mod_cols, silu_c = _mod_shard(c_all, w_ada[0], b_ada_shard, "mod_shard")
    mod_all = _allgather8(mod_cols, "gather_mod")[0].reshape(N_DEV, 8, n_ada)
    mod_mine = lax.dynamic_index_in_dim(mod_all, dev, axis=1, keepdims=False)
    mod = mod_mine.reshape(N_CHIPS, 2, n_ada)[:, 0].reshape(6, D)
    vec = jnp.concatenate([mod, ln1_g, ln1_b, ln2_g, ln2_b, jnp.concatenate([gn_sb, gn_swa], axis=1),
                           jnp.zeros((VEC_ROWS - 11, D), F32)], axis=0)

    w_in_b = g_in.transpose(1, 0, 2).reshape(D, D_IN)

    h_t, proj = _in_proj(xs, vec, w_in_b, b_in, "in_proj")
    y_sb, sp_total, sweep_start, g_out, g_gu = _sb_forward(
        proj, [w_out[0].astype(BF16), w_gu[0].astype(BF16)], "sb_forward")
    w_gu_b = g_gu.transpose(1, 0, 2).reshape(D, 2 * D_FF)
    w_out_b = g_out.reshape(D, D)
    sink_vec = sinks[0]
    y_sw, g_down = _swa_forward(proj, sink_vec, [w_down[0].astype(BF16)], "swa_forward")
    w_down_b = g_down.reshape(D_FF, D)
    mixed_t, attn, x1, h2_b, h2_t = _post_attention(y_sb, y_sw, xs, vec, w_out_b, "post_attention")
    gu, act_t, ffn = _ffn_forward(h2_b, w_gu_b, w_down_b, "ffn_forward")

    def in_halves(shards):
        n_sh, rows, cols = shards.shape
        return shards.reshape(n_sh, 2, rows // 2, cols)

    core = ic.reshape(1).astype(jnp.int32)
    dffn_b, dgu_b, dx1, acc_f = _ffn_backward(x1, ffn, target, gu, vec, w_gu_b, w_down_b, "ffn_backward")
    dw_gu = _weight_grad(h2_t, dgu_b, "grad_w_gu", col_shards=4)
    dw_down = _weight_grad(act_t, dffn_b, "grad_w_down")
    du1, dattn_b, dy, acc_a = _attn_out_backward(dx1, xs, attn, y_sb, y_sw, vec, w_out_b, "attn_out_backward")
    dw_out = _weight_grad(mixed_t, dattn_b, "grad_w_out")
    first = [in_halves(dw_gu), in_halves(dw_down.reshape(4, D_FF // 4, D)), in_halves(dw_out.reshape(4, D // 4, D))]
    dq_sw, dk_sw, dv_sw, dsink, *got_first = _swa_backward(proj, y_sw, dy, sink_vec, first, "swa_backward")
    sums_first = _chip_sums(first, got_first, core, "grad_chip_sums")
    dq_sb, dk_sb, dv_sb, *parts_first = _sb_backward(proj, sp_total, sweep_start, dy, sums_first, "sb_backward")
    dproj_b, grad_x, acc_i, acc_b = _in_proj_backward(dq_sb, dk_sb, dv_sb, dq_sw, dk_sw, dv_sw, du1, xs, vec, w_in_b,
                                                      "in_proj_backward")
    dw_in = _weight_grad(h_t, dproj_b, "grad_w_in")
    last = [in_halves(_col_shards(dw_in, 4))]
    sums_last = _chip_sums(last, _halves_swap(last, "grad_halves_swap_in"), core, "grad_chip_sum_in", out_dtype=BF16)

    dmod = jnp.concatenate([acc_i[C_SHA:C_SHA + 1], acc_i[C_SCA:C_SCA + 1], acc_a[B_GA:B_GA + 1],
                            acc_f[A_SHF:A_SHF + 1], acc_f[A_SCF:A_SCF + 1], acc_f[A_GF:A_GF + 1]], axis=1)
    dsink_row = jnp.concatenate([dsink[:, 0].reshape(1, 8), jnp.zeros((1, LANES - 8), F32)], axis=1)
    loss_row = jnp.concatenate([jnp.sum(acc_f[A_LOSS:A_LOSS + 1], axis=1, keepdims=True),
                                jnp.zeros((1, LANES - 1), F32)], axis=1)
    small = jnp.concatenate([dmod, acc_b[0:1], acc_a[B_LN1G:B_LN1G + 1], acc_a[B_LN1B:B_LN1B + 1],
                             acc_f[A_LN2G:A_LN2G + 1], acc_f[A_LN2B:A_LN2B + 1], acc_a[B_GN:B_GN + 1],
                             dsink_row, loss_row], axis=1)
    small_rows, *parts_last = _allgather8(_pad_rows(small, 8), "gather_small", scatter=sums_last)
    small_all = small_rows[::8]

    mine = _sum4s([*parts_first, *parts_last], "grad_reduce")
    theirs = _sibling_send(mine, "grad_half_return")
    gw_gu, gw_down, gw_out, gw_in = [
        jnp.concatenate([jnp.where(ic == 0, m_, t_), jnp.where(ic == 0, t_, m_)], axis=0) for m_, t_ in zip(mine, theirs)]

    small_names = ["b_ada", "b_in", "ln1_g", "ln1_b", "ln2_g", "ln2_b", "gn_sb", "gn_swa", "sinks"]
    small_at = [SM_MOD, SM_BIN, SM_LN1G, SM_LN1B, SM_LN2G, SM_LN2B, SM_GN, SM_GN + SB_W, SM_SINK]
    *small_out, loss_row_all = _small_update(
        small_all, small_at,
        [b_ada, b_in, ln1_g, ln1_b, ln2_g, ln2_b, gn_sb, gn_swa, sinks],
        [m_b_ada, m_b_in, m_ln1_g, m_ln1_b, m_ln2_g, m_ln2_b, m_gn_sb, m_gn_swa, m_sinks],
        [v_b_ada, v_b_in, v_ln1_g, v_ln1_b, v_ln2_g, v_ln2_b, v_gn_sb, v_gn_swa, v_sinks], SM_LOSS, "small_update")
    g_small, d_small, m2_small, v2_small = [dict(zip(small_names, leaves)) for leaves in small_out]
    loss = loss_row_all[0, 0]

    dmod_cols = lax.dynamic_slice_in_dim(small_all[:, SM_MOD:SM_BIN], chip * n_ada, n_ada, axis=1)
    gw_ada = _weight_grad(_pad_rows(silu_c, LANES).astype(BF16).T, _pad_rows(dmod_cols, LANES).astype(BF16), "grad_w_ada")

    big = {}
    for nm, w, g, m, v in (("w_ada", w_ada, gw_ada, m_w_ada, v_w_ada), ("w_in", w_in, gw_in, m_w_in, v_w_in),
                           ("w_out", w_out, gw_out, m_w_out, v_w_out), ("w_gu", w_gu, gw_gu, m_w_gu, v_w_gu),
                           ("w_down", w_down, gw_down, m_w_down, v_w_down)):
        d_, m2_, v2_ = _adamw(w[0], g, m[0], v[0], "adamw_" + nm)
        big[nm] = (g[None], d_[None], m2_[None], v2_[None])

    order = ["w_ada", "b_ada", "w_in", "b_in", "sinks", "gn_sb", "gn_swa", "w_out", "ln1_g", "ln1_b", "w_gu", "w_down",
             "ln2_g", "ln2_b"]

    def leaf(nm, which):
        if nm in big:
            return big[nm][which]
        return (g_small, d_small, m2_small, v2_small)[which][nm]

    outs = [loss, grad_x[None]]
    for which in range(4):
        outs += [leaf(nm, which) for nm in order]
    return tuple(outs)
```

```python
import math

import jax
import jax.numpy as jnp
from jax import lax
from jax.experimental import pallas as pl
from jax.experimental.pallas import tpu as pltpu

F32 = jnp.float32
BF16 = jnp.bfloat16

D = 1024
HEAD_DIM = 64
SB_W = 512
SWA_QW = 512
SWA_KW = 128
D_IN = 2304
D_FF = 2816
WINDOW = 128
ALPHA = 2.0 ** 0.25
LN_EPS = 1e-5
RMS_EPS = 1e-6
MASK_VALUE = -1e30
QK_SCALE = 1.0 / math.sqrt(HEAD_DIM)

ADAM_LR = 0.001
ADAM_B1 = 0.9
ADAM_B2 = 0.999
ADAM_EPS = 1e-08
ADAM_WD = 0.01
ADAM_STEP = 10

N_CHIPS = 4
N_DEV = 8
LANES = 128

SB_TQ = 512
SB_TK = 256
SB_DEAD_MASS = 110.0
TOK_TILE = 512
FFN_TILE = 256
FFN_BWD_TILE = 256
VMEM_LIMIT = 56 * 1024 * 1024
WGRAD_TOKENS = 2048
WGRAD_VMEM = 40 * 1024 * 1024

V_SH_A, V_SC_A, V_G_A, V_SH_F, V_SC_F, V_G_F, V_LN1G, V_LN1B, V_LN2G, V_LN2B, V_GN = range(11)
VEC_ROWS = 16

SM_MOD = 0
SM_BIN = 6 * D
SM_LN1G = SM_BIN + D_IN
SM_LN1B = SM_LN1G + D
SM_LN2G = SM_LN1B + D
SM_LN2B = SM_LN2G + D
SM_GN = SM_LN2B + D
SM_SINK = SM_GN + D
SM_LOSS = SM_SINK + LANES
SM_LEN = SM_LOSS + LANES
SM_PAD = -(-SM_LEN // (8 * LANES)) * (8 * LANES)

MESH = pl.DeviceIdType.MESH


def _cparams(**kw):
    return pltpu.CompilerParams(vmem_limit_bytes=VMEM_LIMIT, **kw)


def _resident(shape):
    nd = len(shape)
    return pl.BlockSpec(shape, lambda *_: (0,) * nd, pipeline_mode=pl.Buffered(1))


def _dot(a, b):
    return jnp.dot(a, b, preferred_element_type=F32)


def _dot_nt(a, b):
    return lax.dot_general(a, b, (((1,), (1,)), ((), ())), preferred_element_type=F32)


def _dot_tn(a, b):
    return lax.dot_general(a, b, (((0,), (0,)), ((), ())), preferred_element_type=F32)


def _sum_matrix(tk, keep):
    row = lax.broadcasted_iota(jnp.int32, (tk, tk + LANES), 0)
    col = lax.broadcasted_iota(jnp.int32, (tk, tk + LANES), 1)
    return (keep(row, col) | (col >= tk)).astype(BF16)


def _block_sums(x, m):
    tk = x.shape[1]
    res = _dot(x.astype(BF16), m)
    return res[:, :tk], res[:, tk:]


def _before(t0, n, s0, tk):
    return s0 + lax.broadcasted_iota(jnp.int32, (n, tk), 1) < t0 + lax.broadcasted_iota(jnp.int32, (n, tk), 0)


class _RowState:
    def __init__(self, ref, tk, add=False):
        self.ref, self.tk, self.add, self.vals = ref, tk, add, {}

    def _blocks(self, r0, n):
        return range(r0 // self.tk, (r0 + n) // self.tk)

    def get(self, hd, r0, n):
        for d in self._blocks(r0, n):
            if (hd, d) not in self.vals:
                self.vals[(hd, d)] = self.ref[hd, d * self.tk:(d + 1) * self.tk, :]
        parts = [self.vals[(hd, d)] for d in self._blocks(r0, n)]
        return parts[0] if len(parts) == 1 else jnp.concatenate(parts, axis=0)

    def put(self, hd, r0, n, val):
        for k, d in enumerate(self._blocks(r0, n)):
            self.vals[(hd, d)] = val[k * self.tk:(k + 1) * self.tk]

    def accumulate(self, hd, r0, n, val):
        for k, d in enumerate(self._blocks(r0, n)):
            part = val[k * self.tk:(k + 1) * self.tk]
            self.vals[(hd, d)] = part if (hd, d) not in self.vals else self.vals[(hd, d)] + part

    def store(self):
        for (hd, d), val in self.vals.items():
            span = slice(d * self.tk, (d + 1) * self.tk)
            if self.add:
                self.ref[hd, span, :] += val
            else:
                self.ref[hd, span, :] = val


def _across(v, tk):
    return jnp.concatenate([v] * (tk // LANES), axis=1)


def _allgather8(v, name, gather=(), scatter=()):
    m_per, n = v.shape
    n_g, n_s = len(gather), len(scatter)

    def body(x_ref, *refs):
        g_in, s_in = refs[:n_g], refs[n_g:n_g + n_s]
        out_ref = refs[n_g + n_s]
        g_out, s_out = refs[n_g + n_s + 1:2 * n_g + n_s + 1], refs[2 * n_g + n_s + 1:2 * (n_g + n_s) + 1]
        send_sems, recv_sems, local_sem, *more_sems = refs[2 * (n_g + n_s) + 1:]
        halves = _HalvesGather(g_in, g_out, *more_sems[:5]) if n_g else None
        beside = ([halves] if n_g else []) + ([_scatter_exchange(s_in, s_out, *more_sems[-3:])] if n_s else [])
        for ex in beside:
            ex.start()
        x, y, c = lax.axis_index("x"), lax.axis_index("y"), lax.axis_index("c")
        me, sibling = (x, y, c), (x, y, 1 - c)
        chips = [(1 - x, y), (x, 1 - y), (1 - x, 1 - y)]

        def rows(px, py, pc):
            return out_ref.at[pl.ds((4 * px + 2 * py + pc) * m_per, m_per), :]

        def copy(k, block, to, src=None):
            return pltpu.make_async_remote_copy(
                src_ref=rows(*block) if src is None else src, dst_ref=rows(*block),
                send_sem=send_sems.at[k], recv_sem=recv_sems.at[k], device_id=to, device_id_type=MESH)

        mine = pltpu.make_async_copy(x_ref, rows(*me), local_sem)
        mine.start()
        first = [copy(0, me, sibling, src=x_ref)]
        first += [copy(1 + j, me, (*chip, c), src=x_ref) for j, chip in enumerate(chips)]
        for cp in first:
            cp.start()
        passed = [copy(4 + j, (*chip, c), sibling) for j, chip in enumerate(chips)]
        for j, chip in enumerate(chips):
            copy(1 + j, (*chip, c), me).wait_recv()
            passed[j].start()
        copy(0, sibling, me).wait_recv()
        for j, chip in enumerate(chips):
            copy(4 + j, (*chip, 1 - c), me).wait_recv()
        for cp in first + passed:
            cp.wait_send()
        mine.wait()
        if halves is not None:
            halves.forward()
        for ex in beside:
            ex.wait()

    hbm = pl.BlockSpec(memory_space=pl.ANY)
    return pl.pallas_call(
        body, name=name,
        out_shape=[jax.ShapeDtypeStruct((N_DEV * m_per, n), v.dtype)]
        + [jax.ShapeDtypeStruct((N_CHIPS,) + a.shape, a.dtype) for a in gather]
        + [jax.ShapeDtypeStruct(p.shape, p.dtype) for p in scatter],
        in_specs=[pl.BlockSpec(memory_space=pltpu.VMEM)] + [hbm] * (n_g + n_s),
        out_specs=[pl.BlockSpec(memory_space=pltpu.VMEM)] + [hbm] * (n_g + n_s),
        scratch_shapes=[pltpu.SemaphoreType.DMA((7,)), pltpu.SemaphoreType.DMA((7,)), pltpu.SemaphoreType.DMA]
        + (_halves_gather_sems(n_g) if n_g else [])
        + (_exchange_sems(n_s) if n_s else []),
        compiler_params=_cparams(),
    )(v, *gather, *scatter)


class _Exchange:
    def __init__(self, local, sends, arrivals):
        self.local, self.sends, self.arrivals = local, sends, arrivals

    def start(self):
        for cp in self.local + self.sends:
            cp.start()

    def wait(self):
        for cp in self.arrivals:
            cp.wait_recv()
        for cp in self.sends:
            cp.wait_send()
        for cp in self.local:
            cp.wait()


def _exchange_sems(n):
    return [pltpu.SemaphoreType.DMA((3 * n,)), pltpu.SemaphoreType.DMA((3 * n,)), pltpu.SemaphoreType.DMA((n,))]


class _HalvesGather:
    def __init__(self, ins, outs, far_send, far_recv, near_send, near_recv, local_sems):
        x, y, c = lax.axis_index("x"), lax.axis_index("y"), lax.axis_index("c")
        slot = 2 * x + y
        chips = [(1 - x, y), (x, 1 - y), (1 - x, 1 - y)]
        self.local, self.far, self.landed, self.near, self.passed = [], [], [], [], []
        for a in range(len(ins)):
            h = ins[a].shape[0] // 2
            mine, theirs = pl.ds(c * h, h), pl.ds((1 - c) * h, h)
            self.local.append(pltpu.make_async_copy(ins[a], outs[a].at[slot], local_sems.at[a]))
            for j, (px, py) in enumerate(chips):
                k, there = 3 * a + j, 2 * px + py
                far = dict(send_sem=far_send.at[k], recv_sem=far_recv.at[k], device_id=(px, py, c), device_id_type=MESH)
                near = dict(send_sem=near_send.at[k], recv_sem=near_recv.at[k], device_id=(x, y, 1 - c),
                            device_id_type=MESH)
                self.far.append(pltpu.make_async_remote_copy(
                    src_ref=ins[a].at[mine], dst_ref=outs[a].at[slot, mine], **far))
                self.landed.append(pltpu.make_async_remote_copy(
                    src_ref=ins[a].at[mine], dst_ref=outs[a].at[there, mine], **far))
                self.near.append(pltpu.make_async_remote_copy(
                    src_ref=outs[a].at[there, mine], dst_ref=outs[a].at[there, mine], **near))
                self.passed.append(pltpu.make_async_remote_copy(
                    src_ref=outs[a].at[there, mine], dst_ref=outs[a].at[there, theirs], **near))

    def start(self):
        for cp in self.local + self.far:
            cp.start()

    def forward(self):
        for landed, near in zip(self.landed, self.near):
            landed.wait_recv()
            near.start()

    def wait(self):
        for cp in self.passed:
            cp.wait_recv()
        for cp in self.far + self.near:
            cp.wait_send()
        for cp in self.local:
            cp.wait()


def _halves_gather_sems(n):
    return [pltpu.SemaphoreType.DMA((3 * n,))] * 4 + [pltpu.SemaphoreType.DMA((n,))]


def _scatter_exchange(p_refs, out_refs, send_sems, recv_sems, local_sems):
    x, y, c = lax.axis_index("x"), lax.axis_index("y"), lax.axis_index("c")
    slot = 2 * x + y
    chips = [(1 - x, y), (x, 1 - y), (1 - x, 1 - y)]
    local, sends, arrivals = [], [], []
    for a, (p_ref, out_ref) in enumerate(zip(p_refs, out_refs)):
        local.append(pltpu.make_async_copy(p_ref.at[slot], out_ref.at[slot], local_sems.at[a]))
        for j, (px, py) in enumerate(chips):
            sems = dict(send_sem=send_sems.at[3 * a + j], recv_sem=recv_sems.at[3 * a + j],
                        device_id=(px, py, c), device_id_type=MESH)
            sends.append(pltpu.make_async_remote_copy(src_ref=p_ref.at[2 * px + py], dst_ref=out_ref.at[slot], **sems))
            arrivals.append(pltpu.make_async_remote_copy(src_ref=p_ref.at[slot], dst_ref=out_ref.at[2 * px + py], **sems))
    return _Exchange(local, sends, arrivals)


def _sibling_halves(give_refs, got_refs, send_sems, recv_sems):
    x, y, c = lax.axis_index("x"), lax.axis_index("y"), lax.axis_index("c")
    copies = []
    for a, (give_ref, got_ref) in enumerate(zip(give_refs, got_refs)):
        for s in range(N_CHIPS):
            copies.append(pltpu.make_async_remote_copy(
                src_ref=give_ref.at[s, 1 - c], dst_ref=got_ref.at[s], send_sem=send_sems.at[N_CHIPS * a + s],
                recv_sem=recv_sems.at[N_CHIPS * a + s], device_id=(x, y, 1 - c), device_id_type=MESH))
    return copies


def _halves_shapes(arrs):
    return [jax.ShapeDtypeStruct((a.shape[0],) + a.shape[2:], a.dtype) for a in arrs]


def _halves_sems(n):
    return [pltpu.SemaphoreType.DMA((N_CHIPS * n,)), pltpu.SemaphoreType.DMA((N_CHIPS * n,))]


def _halves_swap(arrs, name):
    n = len(arrs)

    def body(*refs):
        copies = _sibling_halves(refs[:n], refs[n:2 * n], *refs[2 * n:])
        for cp in copies:
            cp.start()
        for cp in copies:
            cp.wait()

    hbm = pl.BlockSpec(memory_space=pl.ANY)
    return pl.pallas_call(body, name=name, out_shape=_halves_shapes(arrs), in_specs=[hbm] * n, out_specs=[hbm] * n,
                          scratch_shapes=_halves_sems(n), compiler_params=_cparams())(*arrs)


def _sibling_send(arrs, name):
    n = len(arrs)

    def body(*refs):
        x, y, c = lax.axis_index("x"), lax.axis_index("y"), lax.axis_index("c")
        send_sems, recv_sems = refs[2 * n:]
        copies = [pltpu.make_async_remote_copy(src_ref=refs[a], dst_ref=refs[n + a], send_sem=send_sems.at[a],
                                               recv_sem=recv_sems.at[a], device_id=(x, y, 1 - c), device_id_type=MESH)
                  for a in range(n)]
        for cp in copies:
            cp.start()
        for cp in copies:
            cp.wait()

    hbm = pl.BlockSpec(memory_space=pl.ANY)
    return pl.pallas_call(
        body, name=name, out_shape=[jax.ShapeDtypeStruct(a.shape, a.dtype) for a in arrs],
        in_specs=[hbm] * n, out_specs=[hbm] * n,
        scratch_shapes=[pltpu.SemaphoreType.DMA((n,)), pltpu.SemaphoreType.DMA((n,))],
        compiler_params=_cparams(),
    )(*arrs)


def _row_tile(h):
    return h // 2 if (h // 2) % 8 == 0 else h


def _row_tiles(hs):
    tiles = [_row_tile(h) for h in hs]
    assert len({h // t for h, t in zip(hs, tiles)}) == 1
    return tiles, hs[0] // tiles[0]


def _chip_sums(arrs, gots, core, name, out_dtype=F32):
    n = len(arrs)
    tiles, steps = _row_tiles([a.shape[2] for a in arrs])

    def body(core_ref, *refs):
        for a_ref, b_ref, o_ref in zip(refs[:n], refs[n:2 * n], refs[2 * n:]):
            o_ref[...] = (a_ref[...] + b_ref[...]).astype(out_dtype)

    slabs = [pl.BlockSpec((None, tr, a.shape[3]), lambda s, i, core_ref: (s, i, 0)) for a, tr in zip(arrs, tiles)]
    grid_spec = pltpu.PrefetchScalarGridSpec(
        num_scalar_prefetch=1, grid=(N_CHIPS, steps),
        in_specs=[pl.BlockSpec((None, None, tr, a.shape[3]), lambda s, i, core_ref: (s, core_ref[0], i, 0))
                  for a, tr in zip(arrs, tiles)] + slabs,
        out_specs=slabs)
    return pl.pallas_call(body, name=name, grid_spec=grid_spec,
                          out_shape=[jax.ShapeDtypeStruct(g.shape, out_dtype) for g in gots],
                          compiler_params=_cparams())(core, *arrs, *gots)


def _sum4s(ps, name):
    tiles, steps = _row_tiles([p.shape[1] for p in ps])

    def body(*refs):
        for p_ref, o_ref in zip(refs[:len(ps)], refs[len(ps):]):
            o_ref[...] = ((p_ref[0].astype(F32) + p_ref[1].astype(F32)) + p_ref[2].astype(F32)) + p_ref[3].astype(F32)

    return pl.pallas_call(
        body, name=name, grid=(steps,), out_shape=[jax.ShapeDtypeStruct(p.shape[1:], F32) for p in ps],
        in_specs=[pl.BlockSpec((4, tr, p.shape[2]), lambda i: (0, i, 0)) for p, tr in zip(ps, tiles)],
        out_specs=[pl.BlockSpec((tr, p.shape[2]), lambda i: (i, 0)) for p, tr in zip(ps, tiles)],
        compiler_params=_cparams())(*ps)


def _adam_math(w, g, m, v):
    m2 = ADAM_B1 * m + (1.0 - ADAM_B1) * g
    v2 = ADAM_B2 * v + (1.0 - ADAM_B2) * (g * g)
    m_hat = m2 / (1.0 - ADAM_B1 ** ADAM_STEP)
    v_hat = v2 / (1.0 - ADAM_B2 ** ADAM_STEP)
    delta = -ADAM_LR * (m_hat / (jnp.sqrt(v_hat) + ADAM_EPS) + ADAM_WD * w)
    return delta, m2, v2


def _adamw(w, g, m, v, name):
    rows, cols = w.shape
    tr = rows // 4 if rows % 32 == 0 else rows

    def body(w_ref, g_ref, m_ref, v_ref, d_ref, m2_ref, v2_ref):
        delta, m2, v2 = _adam_math(w_ref[...], g_ref[...], m_ref[...], v_ref[...])
        d_ref[...] = delta
        m2_ref[...] = m2
        v2_ref[...] = v2

    spec = pl.BlockSpec((tr, cols), lambda i: (i, 0))
    shp = jax.ShapeDtypeStruct(w.shape, F32)
    return pl.pallas_call(body, name=name, grid=(rows // tr,), out_shape=[shp, shp, shp],
                          in_specs=[spec] * 4, out_specs=[spec] * 3, compiler_params=_cparams())(w, g, m, v)


def _small_update(g8, offsets, ws, ms, vs, loss_at, name):
    k = len(ws)

    def summed(g8_ref, lo, width):
        g = g8_ref[0:1, lo:lo + width]
        for r in range(1, N_DEV):
            g = g + g8_ref[r:r + 1, lo:lo + width]
        return g

    def body(g8_ref, *refs):
        ins, outs = refs[:3 * k], refs[3 * k:]
        for j in range(k):
            g = summed(g8_ref, offsets[j], ws[j].shape[1])
            delta, m2, v2 = _adam_math(ins[j][...], g, ins[k + j][...], ins[2 * k + j][...])
            for kind, val in enumerate((g, delta, m2, v2)):
                outs[kind * k + j][...] = val
        outs[4 * k][...] = summed(g8_ref, loss_at, LANES)

    vm = pl.BlockSpec(memory_space=pltpu.VMEM)
    shapes = [jax.ShapeDtypeStruct(w.shape, F32) for w in ws] * 4 + [jax.ShapeDtypeStruct((1, LANES), F32)]
    res = pl.pallas_call(body, name=name, out_shape=shapes, in_specs=[vm] * (1 + 3 * k), out_specs=[vm] * (4 * k + 1),
                         compiler_params=_cparams())(g8, *ws, *ms, *vs)
    return res[:k], res[k:2 * k], res[2 * k:3 * k], res[3 * k:4 * k], res[4 * k]


def _mod_shard(c8, w_ada, b_ada_shard, name):
    n = w_ada.shape[1]
    tn = 512

    def body(c_ref, w_ref, b_ref, o_ref, s_ref):
        cv = c_ref[...]
        sc = cv * (1.0 / (1.0 + jnp.exp(-cv)))
        s_ref[...] = sc
        o_ref[...] = _dot(sc.astype(BF16), w_ref[...].astype(BF16)) + b_ref[...]

    return pl.pallas_call(
        body, name=name, grid=(n // tn,),
        out_shape=[jax.ShapeDtypeStruct((8, n), F32), jax.ShapeDtypeStruct((8, D), F32)],
        in_specs=[pl.BlockSpec((8, D), lambda j: (0, 0)), pl.BlockSpec((D, tn), lambda j: (0, j)),
                  pl.BlockSpec((1, tn), lambda j: (0, j))],
        out_specs=[pl.BlockSpec((8, tn), lambda j: (0, j)), pl.BlockSpec((8, D), lambda j: (0, 0))],
        compiler_params=_cparams())(c8, w_ada, b_ada_shard)


def _layer_norm_stats(u):
    mu = jnp.mean(u, axis=1, keepdims=True)
    d = u - mu
    var = jnp.mean(d * d, axis=1, keepdims=True)
    rstd = lax.rsqrt(var + LN_EPS)
    return d * rstd, rstd


def _in_proj(x, vec, w_in, b_in, name):
    s = x.shape[0]
    tb = min(TOK_TILE, s)

    def body(x_ref, vec_ref, w_ref, b_ref, ht_ref, p_ref):
        h = x_ref[...] * (1.0 + vec_ref[V_SC_A:V_SC_A + 1, :]) + vec_ref[V_SH_A:V_SH_A + 1, :]
        hb = h.astype(BF16)
        ht_ref[...] = h.T.astype(BF16)
        proj = _dot(hb, w_ref[...]) + b_ref[...]
        col = lax.broadcasted_iota(jnp.int32, (1, D_IN), 1)
        is_q = (col < SB_W) | ((col >= 3 * SB_W) & (col < 3 * SB_W + SWA_QW))
        p_ref[...] = (proj * jnp.where(is_q, QK_SCALE, 1.0)).astype(BF16)

    return pl.pallas_call(
        body, name=name, grid=(s // tb,),
        out_shape=[jax.ShapeDtypeStruct((D, s), BF16), jax.ShapeDtypeStruct((s, D_IN), BF16)],
        in_specs=[pl.BlockSpec((tb, D), lambda i: (i, 0)), _resident((VEC_ROWS, D)), _resident((D, D_IN)),
                  _resident((1, D_IN))],
        out_specs=[pl.BlockSpec((D, tb), lambda i: (0, i)), pl.BlockSpec((tb, D_IN), lambda i: (i, 0))],
        compiler_params=_cparams())(x, vec, w_in, b_in)


def _softplus_parts(z):
    e1 = jnp.exp(-jnp.abs(z))
    sp = jnp.maximum(z, 0.0) + jnp.log(1.0 + e1)
    return sp, e1


def _sb_forward(proj, shards, name):
    s = proj.shape[0]
    tq, tk = min(SB_TQ, s), min(SB_TK, s)
    r = tq // tk

    n_sh = len(shards)
    nkb = SB_W // LANES
    nq = s // tq

    def body(q_ref, k_ref, v_ref, *refs):
        sh_refs, (o_ref, tot_ref, start_ref), got_refs = refs[:n_sh], refs[n_sh:n_sh + 3], refs[n_sh + 3:2 * n_sh + 3]
        acc_refs, run_refs = refs[2 * n_sh + 3:2 * n_sh + 5]
        i = pl.program_id(1)
        step = pl.program_id(0) * nq + i
        gather = _HalvesGather(sh_refs, got_refs, *refs[2 * n_sh + 5:])

        @pl.when(step == 0)
        def _():
            gather.start()

        @pl.when(step == nkb * nq // 2)
        def _():
            gather.forward()

        lane = lax.broadcasted_iota(jnp.int32, (1, LANES), 1)
        first = lane < HEAD_DIM
        qp = q_ref[...]
        zero = jnp.zeros((), BF16)
        qs = (jnp.where(first, qp, zero), jnp.where(first, zero, qp))
        later = _sum_matrix(tk, lambda row, col: row > col)
        acc_refs[...] = jnp.zeros_like(acc_refs)
        run_refs[...] = jnp.zeros_like(run_refs)

        def blocks(tiles):
            rows = [slice(r0, r0 + n) for r0, n, _, _ in tiles]
            kjs = [k_ref[pl.ds(pl.multiple_of(j * tk, tk), tk), :] for _, _, j, _ in tiles]
            vjs = [v_ref[pl.ds(pl.multiple_of(j * tk, tk), tk), :] for _, _, j, _ in tiles]
            chains = [(hd, t) for t in range(len(tiles)) for hd in range(2)]
            zs = [_dot_nt(qs[hd][rows[t]], kjs[t]) for hd, t in chains]
            sps = [_softplus_parts(z)[0] for z in zs]
            befores = [_before(i * tq + r0, n, j * tk, tk) if diag else None for r0, n, j, diag in tiles]
            spms = [sp if befores[t] is None else jnp.where(befores[t], sp, 0.0) for (hd, t), sp in zip(chains, sps)]
            cums = [_block_sums(spm, later) for spm in spms]
            runs, accs, ws = _RowState(run_refs, tk), _RowState(acc_refs, tk, add=True), []
            for (hd, t), z, sp, (cum, sm) in zip(chains, zs, sps, cums):
                r0, n = tiles[t][:2]
                run = runs.get(hd, r0, n)
                w = jnp.exp(z - sp - cum - _across(run, tk))
                if befores[t] is not None:
                    w = jnp.where(befores[t], w, 0.0)
                ws.append(w.astype(BF16))
                runs.put(hd, r0, n, run + sm)
            for (hd, t), pv in zip(chains, [_dot(w, vjs[t]) for (hd, t), w in zip(chains, ws)]):
                accs.accumulate(hd, *tiles[t][:2], pv)
            accs.store()
            runs.store()

        below = i * r
        diagonal = [(d * tk, tk, below + e, e == d) for d in range(r) for e in range(d, -1, -1)]

        @pl.when(i == 0)
        def _():
            blocks(diagonal)

        @pl.when(i > 0)
        def _():
            blocks(diagonal + [(0, tq, below - 1, False)])

        def swept_mass():
            return jnp.min(jnp.minimum(run_refs[0], run_refs[1]))

        def more(carry):
            n, mass = carry
            return (n < below) & (mass < SB_DEAD_MASS)

        def sweep(carry):
            n, _ = carry
            blocks([(0, tq, below - 1 - n, False)])
            return n + 1, swept_mass()

        n_swept, _ = lax.while_loop(more, sweep, (jnp.minimum(below, 1), swept_mass()))
        start_ref[pl.program_id(0), i] = (below - n_swept).astype(F32)
        o_ref[...] = jnp.where(first, acc_refs[0], acc_refs[1])
        tot_ref[...] = jnp.where(first, run_refs[0], run_refs[1])

        @pl.when(step == nkb * nq - 1)
        def _():
            gather.wait()

    shp = jax.ShapeDtypeStruct((s, SB_W), F32)
    qspec = pl.BlockSpec((tq, LANES), lambda p, i: (i, p))
    hbm = pl.BlockSpec(memory_space=pl.ANY)
    return pl.pallas_call(
        body, name=name, grid=(nkb, nq),
        out_shape=[shp, shp, jax.ShapeDtypeStruct((nkb, nq), F32)]
        + [jax.ShapeDtypeStruct((N_CHIPS,) + a.shape, a.dtype) for a in shards],
        in_specs=[qspec,
                  pl.BlockSpec((s, LANES), lambda p, i: (0, nkb + p)),
                  pl.BlockSpec((s, LANES), lambda p, i: (0, 2 * nkb + p))] + [hbm] * n_sh,
        out_specs=[qspec, qspec, pl.BlockSpec(memory_space=pltpu.SMEM)] + [hbm] * n_sh,
        scratch_shapes=[pltpu.VMEM((2, tq, LANES), F32), pltpu.VMEM((2, tq, LANES), F32)] + _halves_gather_sems(n_sh),
        compiler_params=_cparams())(proj, proj, proj, *shards)


def _swa_masks(n):
    ti = lax.broadcasted_iota(jnp.int32, (WINDOW, 2 * WINDOW), 0)
    kj = lax.broadcasted_iota(jnp.int32, (WINDOW, 2 * WINDOW), 1)
    dist = ti + WINDOW - kj
    valid = (dist >= 0) & (dist < WINDOW) & ((n * WINDOW - WINDOW + kj) >= 0)
    return valid, dist.astype(F32)


def _swa_probs(sc, valid, distf, h, sink):
    slope = 2.0 ** (-(h + 1))
    sc = jnp.where(valid, sc - slope * distf, MASK_VALUE)
    mx = jnp.maximum(jnp.max(sc, axis=1, keepdims=True), sink)
    p = jnp.exp(sc - mx)
    es = jnp.exp(sink - mx)
    inv = 1.0 / (jnp.sum(p, axis=1, keepdims=True) + es)
    return p * inv, es * inv


def _swa_forward(proj, sinks, shards, name):
    s = proj.shape[0]
    nb = s // WINDOW
    qb, kb, vb = 3 * SB_W // SWA_QW, (3 * SB_W + SWA_QW) // LANES, (3 * SB_W + SWA_QW + SWA_KW) // LANES
    n_sh = len(shards)

    def body(q_ref, kp_ref, kc_ref, vp_ref, vc_ref, sink_ref, *refs):
        sh_refs, o_ref, got_refs = refs[:n_sh], refs[n_sh], refs[n_sh + 1:2 * n_sh + 1]
        n = pl.program_id(0)
        gather = _HalvesGather(sh_refs, got_refs, *refs[2 * n_sh + 1:])

        @pl.when(n == 0)
        def _():
            gather.start()

        @pl.when(n == nb // 2)
        def _():
            gather.forward()

        k = jnp.concatenate([kp_ref[...], kc_ref[...]], axis=0)
        v = jnp.concatenate([vp_ref[...], vc_ref[...]], axis=0)
        k_sw = pltpu.roll(k.astype(F32), HEAD_DIM, 1).astype(BF16)
        v_sw = pltpu.roll(v.astype(F32), HEAD_DIM, 1).astype(BF16)
        lane = lax.broadcasted_iota(jnp.int32, (1, LANES), 1)
        halves = [lane < HEAD_DIM, lane >= HEAD_DIM]
        valid, distf = _swa_masks(n)
        heads = range(2 * 4)
        qms = [jnp.where(halves[h % 2], q_ref[:, (h // 2) * LANES:(h // 2 + 1) * LANES], jnp.zeros((), BF16))
               for h in heads]
        kus = [k if h // 4 == h % 2 else k_sw for h in heads]
        vus = [v if h // 4 == h % 2 else v_sw for h in heads]
        scores = [_dot_nt(qms[h], kus[h]) for h in heads]
        ps = [_swa_probs(scores[h], valid, distf, h, sink_ref[h])[0].astype(BF16) for h in heads]
        outs = [_dot(ps[h], vus[h]) for h in heads]
        for pair in range(4):
            o_ref[:, pair * LANES:(pair + 1) * LANES] = jnp.where(halves[0], outs[2 * pair], outs[2 * pair + 1])

        @pl.when(n == nb - 1)
        def _():
            gather.wait()

    prev = lambda n: jnp.maximum(n - 1, 0)
    hbm = pl.BlockSpec(memory_space=pl.ANY)
    return pl.pallas_call(
        body, name=name, grid=(nb,),
        out_shape=[jax.ShapeDtypeStruct((s, SWA_QW), F32)]
        + [jax.ShapeDtypeStruct((N_CHIPS,) + a.shape, a.dtype) for a in shards],
        in_specs=[pl.BlockSpec((WINDOW, SWA_QW), lambda n: (n, qb)),
                  pl.BlockSpec((WINDOW, LANES), lambda n: (prev(n), kb)),
                  pl.BlockSpec((WINDOW, LANES), lambda n: (n, kb)),
                  pl.BlockSpec((WINDOW, LANES), lambda n: (prev(n), vb)),
                  pl.BlockSpec((WINDOW, LANES), lambda n: (n, vb)),
                  pl.BlockSpec(memory_space=pltpu.SMEM)] + [hbm] * n_sh,
        out_specs=[pl.BlockSpec((WINDOW, SWA_QW), lambda n: (n, 0))] + [hbm] * n_sh,
        scratch_shapes=_halves_gather_sems(n_sh),
        compiler_params=_cparams())(proj, proj, proj, proj, proj, sinks, *shards)


def _rms_parts(y):
    return lax.rsqrt(jnp.mean(y * y, axis=1, keepdims=True) + RMS_EPS)


def _post_attention(y_sb, y_sw, x, vec, w_out, name):
    s = x.shape[0]
    tb = min(TOK_TILE, s)

    def body(ysb_ref, ysw_ref, x_ref, vec_ref, w_ref, mixedt_ref, attn_ref, x1_ref, h2_ref, h2t_ref):
        ysb, ysw = ysb_ref[...], ysw_ref[...]
        nsb_f = ysb * _rms_parts(ysb) * vec_ref[V_GN:V_GN + 1, :SB_W]
        nsw_f = ysw * _rms_parts(ysw) * vec_ref[V_GN:V_GN + 1, SB_W:]
        nsb, nsw = nsb_f.astype(BF16), nsw_f.astype(BF16)
        mixedt_ref[:SB_W, :] = nsb_f.T.astype(BF16)
        mixedt_ref[SB_W:, :] = nsw_f.T.astype(BF16)
        attn = _dot(nsb, w_ref[:SB_W, :]) + _dot(nsw, w_ref[SB_W:, :])
        attn_ref[...] = attn
        u1 = ALPHA * x_ref[...] + (1.0 + vec_ref[V_G_A:V_G_A + 1, :]) * attn
        xhat, _ = _layer_norm_stats(u1)
        x1 = xhat * vec_ref[V_LN1G:V_LN1G + 1, :] + vec_ref[V_LN1B:V_LN1B + 1, :]
        x1_ref[...] = x1
        h2 = x1 * (1.0 + vec_ref[V_SC_F:V_SC_F + 1, :]) + vec_ref[V_SH_F:V_SH_F + 1, :]
        h2_ref[...] = h2.astype(BF16)
        h2t_ref[...] = h2.T.astype(BF16)

    half = pl.BlockSpec((tb, SB_W), lambda i: (i, 0))
    full = pl.BlockSpec((tb, D), lambda i: (i, 0))
    full_t = pl.BlockSpec((D, tb), lambda i: (0, i))
    return pl.pallas_call(
        body, name=name, grid=(s // tb,),
        out_shape=[jax.ShapeDtypeStruct((D, s), BF16), jax.ShapeDtypeStruct((s, D), F32),
                   jax.ShapeDtypeStruct((s, D), F32), jax.ShapeDtypeStruct((s, D), BF16),
                   jax.ShapeDtypeStruct((D, s), BF16)],
        in_specs=[half, half, full, _resident((VEC_ROWS, D)), _resident((D, D))],
        out_specs=[full_t, full, full, full, full_t],
        compiler_params=_cparams())(y_sb, y_sw, x, vec, w_out)


def _ffn_forward(h2, w_gu, w_down, name):
    s = h2.shape[0]
    tb = min(FFN_TILE, s)

    def body(h_ref, wgu_ref, wd_ref, gu_ref, actt_ref, ffn_ref):
        gu = _dot(h_ref[...], wgu_ref[...])
        gu_ref[...] = gu.astype(BF16)
        gate, up = gu[:, :D_FF], gu[:, D_FF:]
        act = gate * (1.0 / (1.0 + jnp.exp(-gate))) * up
        actt_ref[...] = act.T.astype(BF16)
        ffn_ref[...] = _dot(act.astype(BF16), wd_ref[...])

    return pl.pallas_call(
        body, name=name, grid=(s // tb,),
        out_shape=[jax.ShapeDtypeStruct((s, 2 * D_FF), BF16), jax.ShapeDtypeStruct((D_FF, s), BF16),
                   jax.ShapeDtypeStruct((s, D), F32)],
        in_specs=[pl.BlockSpec((tb, D), lambda i: (i, 0)), _resident((D, 2 * D_FF)), _resident((D_FF, D))],
        out_specs=[pl.BlockSpec((tb, 2 * D_FF), lambda i: (i, 0)), pl.BlockSpec((D_FF, tb), lambda i: (0, i)),
                   pl.BlockSpec((tb, D), lambda i: (i, 0))],
        compiler_params=_cparams())(h2, w_gu, w_down)


def _layer_norm_bwd(dxhat, xhat, rstd):
    m1 = jnp.mean(dxhat, axis=1, keepdims=True)
    m2 = jnp.mean(dxhat * xhat, axis=1, keepdims=True)
    return rstd * (dxhat - m1 - xhat * m2)


def _colsum(a):
    return jnp.sum(a, axis=0, keepdims=True)


A_LN2G, A_LN2B, A_GF, A_SCF, A_SHF, A_LOSS = range(6)
B_LN1G, B_LN1B, B_GA, B_GN = range(4)
C_SCA, C_SHA = range(2)


def _ffn_backward(x1, ffn, target, gu, vec, w_gu, w_down, name):
    s = x1.shape[0]
    tb = min(FFN_BWD_TILE, s)

    def body(x1_ref, ffn_ref, t_ref, gu_ref, vec_ref, wgu_ref, wd_ref, dffn_ref, dgu_ref, dx1_ref, acc_ref):
        @pl.when(pl.program_id(0) == 0)
        def _():
            acc_ref[...] = jnp.zeros_like(acc_ref)

        x1v, ffn_v = x1_ref[...], ffn_ref[...]
        g_f = 1.0 + vec_ref[V_G_F:V_G_F + 1, :]
        u2 = ALPHA * x1v + g_f * ffn_v
        xhat, rstd = _layer_norm_stats(u2)
        ln_g = vec_ref[V_LN2G:V_LN2G + 1, :]
        err = xhat * ln_g + vec_ref[V_LN2B:V_LN2B + 1, :] - t_ref[...]
        dx2 = err * (1.0 / D)
        acc_ref[A_LOSS:A_LOSS + 1, :] += _colsum(err * err) * (0.5 / D)
        acc_ref[A_LN2G:A_LN2G + 1, :] += _colsum(dx2 * xhat)
        acc_ref[A_LN2B:A_LN2B + 1, :] += _colsum(dx2)
        du2 = _layer_norm_bwd(dx2 * ln_g, xhat, rstd)
        acc_ref[A_GF:A_GF + 1, :] += _colsum(du2 * ffn_v)
        dffn = (g_f * du2).astype(BF16)
        dffn_ref[...] = dffn
        dact = _dot_nt(dffn, wd_ref[...])
        gate, up = gu_ref[:, :D_FF].astype(F32), gu_ref[:, D_FF:].astype(F32)
        sg = 1.0 / (1.0 + jnp.exp(-gate))
        dgate = (dact * up * (sg * (1.0 + gate * (1.0 - sg)))).astype(BF16)
        dup = (dact * (gate * sg)).astype(BF16)
        dgu_ref[:, :D_FF] = dgate
        dgu_ref[:, D_FF:] = dup
        dh2 = _dot_nt(dgate, wgu_ref[:, :D_FF]) + _dot_nt(dup, wgu_ref[:, D_FF:])
        dx1_ref[...] = ALPHA * du2 + dh2 * (1.0 + vec_ref[V_SC_F:V_SC_F + 1, :])
        acc_ref[A_SCF:A_SCF + 1, :] += _colsum(dh2 * x1v)
        acc_ref[A_SHF:A_SHF + 1, :] += _colsum(dh2)

    full = pl.BlockSpec((tb, D), lambda i: (i, 0))
    wide = pl.BlockSpec((tb, 2 * D_FF), lambda i: (i, 0))
    return pl.pallas_call(
        body, name=name, grid=(s // tb,),
        out_shape=[jax.ShapeDtypeStruct((s, D), BF16), jax.ShapeDtypeStruct((s, 2 * D_FF), BF16),
                   jax.ShapeDtypeStruct((s, D), F32), jax.ShapeDtypeStruct((8, D), F32)],
        in_specs=[full, full, full, wide, _resident((VEC_ROWS, D)), _resident((D, 2 * D_FF)), _resident((D_FF, D))],
        out_specs=[full, wide, full, pl.BlockSpec((8, D), lambda i: (0, 0))],
        compiler_params=_cparams())(x1, ffn, target, gu, vec, w_gu, w_down)


def _attn_out_backward(dx1, x, attn, y_sb, y_sw, vec, w_out, name):
    s = x.shape[0]
    tb = min(TOK_TILE, s)

    def body(dx1_ref, x_ref, attn_ref, ysb_ref, ysw_ref, vec_ref, w_ref, du1_ref, dattn_ref, dy_ref, acc_ref):
        @pl.when(pl.program_id(0) == 0)
        def _():
            acc_ref[...] = jnp.zeros_like(acc_ref)

        attn = attn_ref[...]
        g_a = 1.0 + vec_ref[V_G_A:V_G_A + 1, :]
        xhat, rstd = _layer_norm_stats(ALPHA * x_ref[...] + g_a * attn)
        dx1v = dx1_ref[...]
        acc_ref[B_LN1G:B_LN1G + 1, :] += _colsum(dx1v * xhat)
        acc_ref[B_LN1B:B_LN1B + 1, :] += _colsum(dx1v)
        du1 = _layer_norm_bwd(dx1v * vec_ref[V_LN1G:V_LN1G + 1, :], xhat, rstd)
        du1_ref[...] = du1
        acc_ref[B_GA:B_GA + 1, :] += _colsum(du1 * attn)
        dattn = (g_a * du1).astype(BF16)
        dattn_ref[...] = dattn
        dmixed = _dot_nt(dattn, w_ref[...])
        for lo, y_ref in ((0, ysb_ref), (SB_W, ysw_ref)):
            y = y_ref[...]
            rr = _rms_parts(y)
            dn = dmixed[:, lo:lo + SB_W]
            acc_ref[B_GN:B_GN + 1, lo:lo + SB_W] += _colsum(dn * y * rr)
            dng = dn * vec_ref[V_GN:V_GN + 1, lo:lo + SB_W]
            dy_ref[:, lo:lo + SB_W] = rr * dng - y * (rr * rr * rr) * jnp.mean(dng * y, axis=1, keepdims=True)

    half = pl.BlockSpec((tb, SB_W), lambda i: (i, 0))
    full = pl.BlockSpec((tb, D), lambda i: (i, 0))
    return pl.pallas_call(
        body, name=name, grid=(s // tb,),
        out_shape=[jax.ShapeDtypeStruct((s, D), F32), jax.ShapeDtypeStruct((s, D), BF16),
                   jax.ShapeDtypeStruct((s, D), F32), jax.ShapeDtypeStruct((8, D), F32)],
        in_specs=[full, full, full, half, half, _resident((VEC_ROWS, D)), _resident((D, D))],
        out_specs=[full, full, full, pl.BlockSpec((8, D), lambda i: (0, 0))],
        compiler_params=_cparams())(dx1, x, attn, y_sb, y_sw, vec, w_out)


def _sb_backward(proj, sp_total, sweep_start, dy, slabs, name):
    s = proj.shape[0]
    tq, tk = min(SB_TQ, s), min(SB_TK, s)
    r = tq // tk
    nkb = SB_W // LANES
    nq = s // tq

    n_sl = len(slabs)

    def body(q_ref, k_ref, v_ref, tot_ref, do_ref, start_ref, *refs):
        slab_refs, (dq_ref, dk_ref, dv_ref), got_refs = refs[:n_sl], refs[n_sl:n_sl + 3], refs[n_sl + 3:2 * n_sl + 3]
        dq_acc, left_refs, gsum_refs = refs[2 * n_sl + 3:2 * n_sl + 6]
        i = pl.program_id(1)
        step = pl.program_id(0) * nq + i
        scatter = _scatter_exchange(slab_refs, got_refs, *refs[2 * n_sl + 6:])

        @pl.when(step == 0)
        def _():
            scatter.start()

        @pl.when(i == 0)
        def _():
            dk_ref[...] = jnp.zeros_like(dk_ref)
            dv_ref[...] = jnp.zeros_like(dv_ref)

        lane = lax.broadcasted_iota(jnp.int32, (1, LANES), 1)
        first = lane < HEAD_DIM
        qp, dop, totp = q_ref[...], do_ref[...], tot_ref[...]
        zero = jnp.zeros((), BF16)
        qs = (jnp.where(first, qp, zero), jnp.where(first, zero, qp))
        dofs = (jnp.where(first, dop, 0.0), jnp.where(first, 0.0, dop))
        dobs = tuple(d.astype(BF16) for d in dofs)
        dots = tuple(d.T.astype(BF16) for d in dofs)
        qts = tuple(qh.astype(F32).T.astype(BF16) for qh in qs)
        later = _sum_matrix(tk, lambda row, col: row > col)
        earlier = _sum_matrix(tk, lambda row, col: row < col)
        dq_acc[...] = jnp.zeros_like(dq_acc)
        gsum_refs[...] = jnp.zeros_like(gsum_refs)
        swapped = pltpu.roll(totp, HEAD_DIM, 1)
        left_refs[0] = jnp.where(first, totp, swapped)
        left_refs[1] = jnp.where(first, swapped, totp)

        def blocks(tiles):
            rows = [slice(r0, r0 + n) for r0, n, _, _ in tiles]
            kjs = [k_ref[pl.ds(pl.multiple_of(j * tk, tk), tk), :] for _, _, j, _ in tiles]
            vjs = [v_ref[pl.ds(pl.multiple_of(j * tk, tk), tk), :] for _, _, j, _ in tiles]
            chains = [(hd, t) for t in range(len(tiles)) for hd in range(2)]
            zs = [_dot_nt(qs[hd][rows[t]], kjs[t]) for hd, t in chains]
            dws = [_dot_nt(dobs[hd][rows[t]], vjs[t]) for hd, t in chains]
            parts = [_softplus_parts(z) for z in zs]
            sps = [p[0] for p in parts]
            befores = [_before(i * tq + r0, n, j * tk, tk) if diag else None for r0, n, j, diag in tiles]
            spms = [sp if befores[t] is None else jnp.where(befores[t], sp, 0.0) for (hd, t), sp in zip(chains, sps)]
            cums = [_block_sums(spm, later) for spm in spms]
            lefts, gsums, dq_sums = _RowState(left_refs, tk), _RowState(gsum_refs, tk), _RowState(dq_acc, tk, add=True)
            ws = []
            for (hd, t), z, sp, (cum, sm) in zip(chains, zs, sps, cums):
                r0, n = tiles[t][:2]
                left = lefts.get(hd, r0, n) - sm
                lefts.put(hd, r0, n, left)
                w = jnp.exp(z - sp - cum - _across(left, tk))
                ws.append(w if befores[t] is None else jnp.where(befores[t], w, 0.0))
            wbs = [w.astype(BF16) for w in ws]
            dvs = [_dot(dots[hd][:, rows[t]], wb) for (hd, t), wb in zip(chains, wbs)]
            gs = [dw * w for dw, w in zip(dws, ws)]
            gcums = [_block_sums(g, earlier) for g in gs]
            dzbs = []
            for (hd, t), z, (sp, e1), g, (gcum, gsm) in zip(chains, zs, parts, gs, gcums):
                r0, n = tiles[t][:2]
                gsum = gsums.get(hd, r0, n)
                inv = 1.0 / (1.0 + e1)
                sig = jnp.where(z >= 0.0, inv, e1 * inv)
                dz = g - sig * (g + _across(gsum, tk) + gcum)
                dzbs.append((dz if befores[t] is None else jnp.where(befores[t], dz, 0.0)).astype(BF16))
                gsums.put(hd, r0, n, gsum + gsm)
            dqs = [_dot(dzb, kjs[t]) for (hd, t), dzb in zip(chains, dzbs)]
            dks = [_dot(qts[hd][:, rows[t]], dzb) for (hd, t), dzb in zip(chains, dzbs)]
            for t, (_, _, j, _) in enumerate(tiles):
                dv_ref[j] += dvs[2 * t] + dvs[2 * t + 1]
                dk_ref[j] += dks[2 * t] + dks[2 * t + 1]
            for (hd, t), dq in zip(chains, dqs):
                dq_sums.accumulate(hd, *tiles[t][:2], dq)
            dq_sums.store()
            lefts.store()
            gsums.store()

        below = i * r
        start = jnp.clip(start_ref[pl.program_id(0), i].astype(jnp.int32), 0, below)

        def sweep(n, carry):
            blocks([(0, tq, start + n, False)])
            return carry

        lax.fori_loop(0, jnp.maximum(below - 1 - start, 0), sweep, 0)
        diagonal = [(d * tk, tk, below + e, e == d) for d in range(r) for e in range(d + 1)]

        @pl.when(i == 0)
        def _():
            blocks(diagonal)

        @pl.when(i > 0)
        def _():
            blocks([(0, tq, below - 1, False)] + diagonal)
        dq_ref[...] = jnp.where(first, dq_acc[0], dq_acc[1])

        @pl.when(step == nkb * nq - 1)
        def _():
            scatter.wait()

    shp = jax.ShapeDtypeStruct((s, SB_W), F32)
    qspec = pl.BlockSpec((tq, LANES), lambda p, i: (i, p))
    whole = pl.BlockSpec((None, s // tk, LANES, tk), lambda p, i: (p, 0, 0, 0))
    shp_t = jax.ShapeDtypeStruct((nkb, s // tk, LANES, tk), F32)
    hbm = pl.BlockSpec(memory_space=pl.ANY)
    return pl.pallas_call(
        body, name=name, grid=(nkb, nq),
        out_shape=[shp, shp_t, shp_t] + [jax.ShapeDtypeStruct(p.shape, p.dtype) for p in slabs],
        in_specs=[qspec,
                  pl.BlockSpec((s, LANES), lambda p, i: (0, nkb + p)),
                  pl.BlockSpec((s, LANES), lambda p, i: (0, 2 * nkb + p)),
                  qspec, qspec, pl.BlockSpec(memory_space=pltpu.SMEM)] + [hbm] * n_sl,
        out_specs=[qspec, whole, whole] + [hbm] * n_sl,
        scratch_shapes=[pltpu.VMEM((2, tq, LANES), F32), pltpu.VMEM((2, tq, LANES), F32), pltpu.VMEM((2, tq, LANES), F32)]
        + _exchange_sems(n_sl),
        compiler_params=_cparams())(proj, proj, proj, sp_total, dy, sweep_start, *slabs)


def _swa_backward(proj, y_sw, dy, sinks, gives, name):
    s = proj.shape[0]
    nb = s // WINDOW
    qb, kb, vb = 3 * SB_W // SWA_QW, (3 * SB_W + SWA_QW) // LANES, (3 * SB_W + SWA_QW + SWA_KW) // LANES

    n_gv = len(gives)

    def body(q_ref, kp_ref, kc_ref, vp_ref, vc_ref, o_ref, do_ref, sink_ref, *refs):
        give_refs, (dq_ref, dk_ref, dv_ref, ds_ref), got_refs = refs[:n_gv], refs[n_gv:n_gv + 4], refs[n_gv + 4:2 * n_gv + 4]
        n = pl.program_id(0)
        swap = _sibling_halves(give_refs, got_refs, *refs[2 * n_gv + 4:])

        @pl.when(n == 0)
        def _():
            for cp in swap:
                cp.start()

        @pl.when(n == 0)
        def _():
            dk_ref[...] = jnp.zeros_like(dk_ref)
            dv_ref[...] = jnp.zeros_like(dv_ref)
            ds_ref[...] = jnp.zeros_like(ds_ref)

        k = jnp.concatenate([kp_ref[...], kc_ref[...]], axis=0)
        v = jnp.concatenate([vp_ref[...], vc_ref[...]], axis=0)
        k_sw = pltpu.roll(k.astype(F32), HEAD_DIM, 1).astype(BF16)
        v_sw = pltpu.roll(v.astype(F32), HEAD_DIM, 1).astype(BF16)
        lane = lax.broadcasted_iota(jnp.int32, (1, LANES), 1)
        halves = [lane < HEAD_DIM, lane >= HEAD_DIM]
        valid, distf = _swa_masks(n)
        heads = range(2 * 4)
        cols = [slice((h // 2) * LANES, (h // 2 + 1) * LANES) for h in heads]
        qms = [jnp.where(halves[h % 2], q_ref[:, cols[h]], jnp.zeros((), BF16)) for h in heads]
        dos = [jnp.where(halves[h % 2], do_ref[:, cols[h]], 0.0) for h in heads]
        dobs = [d.astype(BF16) for d in dos]
        native = [h // 4 == h % 2 for h in heads]
        kus = [k if native[h] else k_sw for h in heads]
        vus = [v if native[h] else v_sw for h in heads]
        scores = [_dot_nt(qms[h], kus[h]) for h in heads]
        dps = [_dot_nt(dobs[h], vus[h]) for h in heads]
        deltas = [jnp.sum(dos[h] * o_ref[:, cols[h]], axis=1, keepdims=True) for h in heads]
        probs = [_swa_probs(scores[h], valid, distf, h, sink_ref[h]) for h in heads]
        pbs = [probs[h][0].astype(BF16) for h in heads]
        dscs = [(probs[h][0] * (dps[h] - deltas[h])).astype(BF16) for h in heads]
        dqs = [_dot(dscs[h], kus[h]) for h in heads]
        dks = [_dot_tn(dscs[h], qms[h]) for h in heads]
        dvs = [_dot_tn(pbs[h], dobs[h]) for h in heads]
        for h in heads:
            ds_ref[h:h + 1, :] += jnp.zeros((1, LANES), F32) - jnp.sum(probs[h][1] * deltas[h])
        for pair in range(4):
            dq_ref[:, cols[2 * pair]] = jnp.where(halves[0], dqs[2 * pair], dqs[2 * pair + 1])

        def gathered(parts):
            nat = sum(parts[h] for h in heads if native[h])
            rot = sum(parts[h] for h in heads if not native[h])
            return nat + pltpu.roll(rot, HEAD_DIM, 1)

        dk, dv = gathered(dks), gathered(dvs)
        prev = pl.multiple_of(jnp.maximum(n - 1, 0) * WINDOW, WINDOW)
        cur = pl.multiple_of(n * WINDOW, WINDOW)
        dk_ref[pl.ds(prev, WINDOW), :] += dk[:WINDOW]
        dv_ref[pl.ds(prev, WINDOW), :] += dv[:WINDOW]
        dk_ref[pl.ds(cur, WINDOW), :] += dk[WINDOW:]
        dv_ref[pl.ds(cur, WINDOW), :] += dv[WINDOW:]

        @pl.when(n == nb - 1)
        def _():
            for cp in swap:
                cp.wait()

    prev_blk = lambda n: jnp.maximum(n - 1, 0)
    wide = pl.BlockSpec((WINDOW, SWA_QW), lambda n: (n, 0))
    whole = pl.BlockSpec((s, LANES), lambda n: (0, 0))
    hbm = pl.BlockSpec(memory_space=pl.ANY)
    return pl.pallas_call(
        body, name=name, grid=(nb,),
        out_shape=[jax.ShapeDtypeStruct((s, SWA_QW), F32), jax.ShapeDtypeStruct((s, LANES), F32),
                   jax.ShapeDtypeStruct((s, LANES), F32), jax.ShapeDtypeStruct((8, LANES), F32)] + _halves_shapes(gives),
        in_specs=[pl.BlockSpec((WINDOW, SWA_QW), lambda n: (n, qb)),
                  pl.BlockSpec((WINDOW, LANES), lambda n: (prev_blk(n), kb)),
                  pl.BlockSpec((WINDOW, LANES), lambda n: (n, kb)),
                  pl.BlockSpec((WINDOW, LANES), lambda n: (prev_blk(n), vb)),
                  pl.BlockSpec((WINDOW, LANES), lambda n: (n, vb)),
                  wide,
                  pl.BlockSpec((WINDOW, SWA_QW), lambda n: (n, 1)),
                  pl.BlockSpec(memory_space=pltpu.SMEM)] + [hbm] * n_gv,
        out_specs=[wide, whole, whole, pl.BlockSpec((8, LANES), lambda n: (0, 0))] + [hbm] * n_gv,
        scratch_shapes=_halves_sems(n_gv),
        compiler_params=_cparams())(proj, proj, proj, proj, proj, y_sw, dy, sinks, *gives)


def _in_proj_backward(dq_sb, dkt_sb, dvt_sb, dq_sw, dk_sw, dv_sw, du1, x, vec, w_in, name):
    s = x.shape[0]
    tb = min(TOK_TILE, s)
    n_pairs, _, _, tk = dkt_sb.shape

    def body(dqsb_ref, dktsb_ref, dvtsb_ref, dqsw_ref, dksw_ref, dvsw_ref, du1_ref, x_ref, vec_ref, w_ref,
             dproj_ref, gx_ref, acc_ref, bacc_ref):
        @pl.when(pl.program_id(0) == 0)
        def _():
            acc_ref[...] = jnp.zeros_like(acc_ref)
            bacc_ref[...] = jnp.zeros_like(bacc_ref)

        pieces = ((0, dqsb_ref, QK_SCALE), (3 * SB_W, dqsw_ref, QK_SCALE), (3 * SB_W + SWA_QW, dksw_ref, 1.0),
                  (3 * SB_W + SWA_QW + SWA_KW, dvsw_ref, 1.0))
        for lo, ref, scale in pieces:
            width = ref.shape[1]
            piece = ref[...] * scale
            bacc_ref[0:1, lo:lo + width] += _colsum(piece)
            dproj_ref[:, lo:lo + width] = piece.astype(BF16)
        for base, ref in ((SB_W, dktsb_ref), (2 * SB_W, dvtsb_ref)):
            for p in range(n_pairs):
                lo = base + p * LANES
                for jj in range(tb // tk):
                    piece = ref[p, jj].T
                    bacc_ref[0:1, lo:lo + LANES] += _colsum(piece)
                    dproj_ref[jj * tk:(jj + 1) * tk, lo:lo + LANES] = piece.astype(BF16)
        dh = _dot_nt(dproj_ref[...], w_ref[...])
        xv = x_ref[...]
        gx_ref[...] = ALPHA * du1_ref[...] + dh * (1.0 + vec_ref[V_SC_A:V_SC_A + 1, :])
        acc_ref[C_SCA:C_SCA + 1, :] += _colsum(dh * xv)
        acc_ref[C_SHA:C_SHA + 1, :] += _colsum(dh)

    half = pl.BlockSpec((tb, SB_W), lambda i: (i, 0))
    narrow = pl.BlockSpec((tb, LANES), lambda i: (i, 0))
    full = pl.BlockSpec((tb, D), lambda i: (i, 0))
    blocks_t = pl.BlockSpec((n_pairs, tb // tk, LANES, tk), lambda i: (0, i, 0, 0))
    return pl.pallas_call(
        body, name=name, grid=(s // tb,),
        out_shape=[jax.ShapeDtypeStruct((s, D_IN), BF16), jax.ShapeDtypeStruct((s, D), F32),
                   jax.ShapeDtypeStruct((8, D), F32), jax.ShapeDtypeStruct((8, D_IN), F32)],
        in_specs=[half, blocks_t, blocks_t, half, narrow, narrow, full, full, _resident((VEC_ROWS, D)),
                  _resident((D, D_IN))],
        out_specs=[pl.BlockSpec((tb, D_IN), lambda i: (i, 0)), full, pl.BlockSpec((8, D), lambda i: (0, 0)),
                   pl.BlockSpec((8, D_IN), lambda i: (0, 0))],
        compiler_params=_cparams())(dq_sb, dkt_sb, dvt_sb, dq_sw, dk_sw, dv_sw, du1, x, vec, w_in)


def _weight_grad(at, b, name, col_shards=1):
    m, s = at.shape
    n = b.shape[1]
    if col_shards > 1:
        tn = n // col_shards
        out_shape = jax.ShapeDtypeStruct((col_shards, m, tn), F32)
        out_spec = pl.BlockSpec((None, m, tn), lambda j, k: (j, 0, 0))
    else:
        tn = 512 if n % 512 == 0 else n
        out_shape = jax.ShapeDtypeStruct((m, n), F32)
        out_spec = pl.BlockSpec((m, tn), lambda j, k: (0, j))
    ts = min(WGRAD_TOKENS, s)
    while 2 * (m * ts * 2 + ts * tn * 2 + m * tn * 4) > WGRAD_VMEM and ts > 512:
        ts //= 2

    def body(at_ref, b_ref, o_ref):
        @pl.when(pl.program_id(1) == 0)
        def _():
            o_ref[...] = jnp.zeros_like(o_ref)

        o_ref[...] += _dot(at_ref[...], b_ref[...])

    return pl.pallas_call(
        body, name=name, grid=(n // tn, s // ts),
        out_shape=out_shape,
        in_specs=[pl.BlockSpec((m, ts), lambda j, k: (0, k)), pl.BlockSpec((ts, tn), lambda j, k: (k, j))],
        out_specs=out_spec,
        compiler_params=_cparams())(at, b)


def _pad_rows(v, rows):
    return jnp.concatenate([v, jnp.zeros((rows - v.shape[0], v.shape[1]), v.dtype)], axis=0)


def _col_shards(w, n_shards):
    r, n = w.shape
    return w.reshape(r, n_shards, n // n_shards).transpose(1, 0, 2)


def kernel(x, c, w_ada, b_ada, w_in, b_in, sinks, gn_sb, gn_swa, w_out, ln1_g, ln1_b, w_gu, w_down, ln2_g, ln2_b, loss_target, m_w_ada, m_b_ada, m_w_in, m_b_in, m_sinks, m_gn_sb, m_gn_swa, m_w_out, m_ln1_g, m_ln1_b, m_w_gu, m_w_down, m_ln2_g, m_ln2_b, v_w_ada, v_b_ada, v_w_in, v_b_in, v_sinks, v_gn_sb, v_gn_swa, v_w_out, v_ln1_g, v_ln1_b, v_w_gu, v_w_down, v_ln2_g, v_ln2_b):
    ix, iy, ic = lax.axis_index("x"), lax.axis_index("y"), lax.axis_index("c")
    chip = 2 * ix + iy
    dev = 4 * ix + 2 * iy + ic
    xs, target = x[0], loss_target[0]
    s = xs.shape[0]

    c_rows, g_in = _allgather8(_pad_rows(c, 8), "gather_c", gather=[w_in[0].astype(BF16)])
    c_all = c_rows[::8]
    n_ada = w_ada.shape[2]
    b_ada_shard = lax.dynamic_slice_in_dim(b_ada, chip * n_ada, n_ada, axis=1)
    mod_cols, silu_c = _mod_shard(c_all, w_ada[0], b_ada_shard, "mod_shard")
    mod_all = _allgather8(mod_cols, "gather_mod")[0].reshape(N_DEV, 8, n_ada)
    mod_mine = lax.dynamic_index_in_dim(mod_all, dev, axis=1, keepdims=False)
    mod = mod_mine.reshape(N_CHIPS, 2, n_ada)[:, 0].reshape(6, D)
    vec = jnp.concatenate([mod, ln1_g, ln1_b, ln2_g, ln2_b, jnp.concatenate([gn_sb, gn_swa], axis=1),
                           jnp.zeros((VEC_ROWS - 11, D), F32)], axis=0)

    w_in_b = g_in.transpose(1, 0, 2).reshape(D, D_IN)

    h_t, proj = _in_proj(xs, vec, w_in_b, b_in, "in_proj")
    y_sb, sp_total, sweep_start, g_out, g_gu = _sb_forward(
        proj, [w_out[0].astype(BF16), w_gu[0].astype(BF16)], "sb_forward")
    w_gu_b = g_gu.transpose(1, 0, 2).reshape(D, 2 * D_FF)
    w_out_b = g_out.reshape(D, D)
    sink_vec = sinks[0]
    y_sw, g_down = _swa_forward(proj, sink_vec, [w_down[0].astype(BF16)], "swa_forward")
    w_down_b = g_down.reshape(D_FF, D)
    mixed_t, attn, x1, h2_b, h2_t = _post_attention(y_sb, y_sw, xs, vec, w_out_b, "post_attention")
    gu, act_t, ffn = _ffn_forward(h2_b, w_gu_b, w_down_b, "ffn_forward")

    def in_halves(shards):
        n_sh, rows, cols = shards.shape
        return shards.reshape(n_sh, 2, rows // 2, cols)

    core = ic.reshape(1).astype(jnp.int32)
    dffn_b, dgu_b, dx1, acc_f = _ffn_backward(x1, ffn, target, gu, vec, w_gu_b, w_down_b, "ffn_backward")
    dw_gu = _weight_grad(h2_t, dgu_b, "grad_w_gu", col_shards=4)
    dw_down = _weight_grad(act_t, dffn_b, "grad_w_down")
    du1, dattn_b, dy, acc_a = _attn_out_backward(dx1, xs, attn, y_sb, y_sw, vec, w_out_b, "attn_out_backward")
    dw_out = _weight_grad(mixed_t, dattn_b, "grad_w_out")
    first = [in_halves(dw_gu), in_halves(dw_down.reshape(4, D_FF // 4, D)), in_halves(dw_out.reshape(4, D // 4, D))]
    dq_sw, dk_sw, dv_sw, dsink, *got_first = _swa_backward(proj, y_sw, dy, sink_vec, first, "swa_backward")
    sums_first = _chip_sums(first, got_first, core, "grad_chip_sums")
    dq_sb, dk_sb, dv_sb, *parts_first = _sb_backward(proj, sp_total, sweep_start, dy, sums_first, "sb_backward")
    dproj_b, grad_x, acc_i, acc_b = _in_proj_backward(dq_sb, dk_sb, dv_sb, dq_sw, dk_sw, dv_sw, du1, xs, vec, w_in_b,
                                                      "in_proj_backward")
    dw_in = _weight_grad(h_t, dproj_b, "grad_w_in")
    last = [in_halves(_col_shards(dw_in, 4))]
    sums_last = _chip_sums(last, _halves_swap(last, "grad_halves_swap_in"), core, "grad_chip_sum_in", out_dtype=BF16)

    dmod = jnp.concatenate([acc_i[C_SHA:C_SHA + 1], acc_i[C_SCA:C_SCA + 1], acc_a[B_GA:B_GA + 1],
                            acc_f[A_SHF:A_SHF + 1], acc_f[A_SCF:A_SCF + 1], acc_f[A_GF:A_GF + 1]], axis=1)
    dsink_row = jnp.concatenate([dsink[:, 0].reshape(1, 8), jnp.zeros((1, LANES - 8), F32)], axis=1)
    loss_row = jnp.concatenate([jnp.sum(acc_f[A_LOSS:A_LOSS + 1], axis=1, keepdims=True),
                                jnp.zeros((1, LANES - 1), F32)], axis=1)
    small = jnp.concatenate([dmod, acc_b[0:1], acc_a[B_LN1G:B_LN1G + 1], acc_a[B_LN1B:B_LN1B + 1],
                             acc_f[A_LN2G:A_LN2G + 1], acc_f[A_LN2B:A_LN2B + 1], acc_a[B_GN:B_GN + 1],
                             dsink_row, loss_row, jnp.zeros((1, SM_PAD - SM_LEN), F32)], axis=1)
    small_rows, *parts_last = _allgather8(small.reshape(8, SM_PAD // 8), "gather_small", scatter=sums_last)
    small_all = small_rows.reshape(N_DEV, SM_PAD)

    mine = _sum4s([*parts_first, *parts_last], "grad_reduce")
    theirs = _sibling_send(mine, "grad_half_return")
    gw_gu, gw_down, gw_out, gw_in = [
        jnp.concatenate([jnp.where(ic == 0, m_, t_), jnp.where(ic == 0, t_, m_)], axis=0) for m_, t_ in zip(mine, theirs)]

    small_names = ["b_ada", "b_in", "ln1_g", "ln1_b", "ln2_g", "ln2_b", "gn_sb", "gn_swa", "sinks"]
    small_at = [SM_MOD, SM_BIN, SM_LN1G, SM_LN1B, SM_LN2G, SM_LN2B, SM_GN, SM_GN + SB_W, SM_SINK]
    *small_out, loss_row_all = _small_update(
        small_all, small_at,
        [b_ada, b_in, ln1_g, ln1_b, ln2_g, ln2_b, gn_sb, gn_swa, sinks],
        [m_b_ada, m_b_in, m_ln1_g, m_ln1_b, m_ln2_g, m_ln2_b, m_gn_sb, m_gn_swa, m_sinks],
        [v_b_ada, v_b_in, v_ln1_g, v_ln1_b, v_ln2_g, v_ln2_b, v_gn_sb, v_gn_swa, v_sinks], SM_LOSS, "small_update")
    g_small, d_small, m2_small, v2_small = [dict(zip(small_names, leaves)) for leaves in small_out]
    loss = loss_row_all[0, 0]

    dmod_cols = lax.dynamic_slice_in_dim(small_all[:, SM_MOD:SM_BIN], chip * n_ada, n_ada, axis=1)
    gw_ada = _weight_grad(_pad_rows(silu_c, LANES).astype(BF16).T, _pad_rows(dmod_cols, LANES).astype(BF16), "grad_w_ada")

    big = {}
    for nm, w, g, m, v in (("w_ada", w_ada, gw_ada, m_w_ada, v_w_ada), ("w_in", w_in, gw_in, m_w_in, v_w_in),
                           ("w_out", w_out, gw_out, m_w_out, v_w_out), ("w_gu", w_gu, gw_gu, m_w_gu, v_w_gu),
                           ("w_down", w_down, gw_down, m_w_down, v_w_down)):
        d_, m2_, v2_ = _adamw(w[0], g, m[0], v[0], "adamw_" + nm)
        big[nm] = (g[None], d_[None], m2_[None], v2_[None])

    order = ["w_ada", "b_ada", "w_in", "b_in", "sinks", "gn_sb", "gn_swa", "w_out", "ln1_g", "ln1_b", "w_gu", "w_down",
             "ln2_g", "ln2_b"]

    def leaf(nm, which):
        if nm in big:
            return big[nm][which]
        return (g_small, d_small, m2_small, v2_small)[which][nm]

    outs = [loss, grad_x[None]]
    for which in range(4):
        outs += [leaf(nm, which) for nm in order]
    return tuple(outs)
```

```python
import math

import jax
import jax.numpy as jnp
from jax import lax
from jax.experimental import pallas as pl
from jax.experimental.pallas import tpu as pltpu

F32 = jnp.float32
BF16 = jnp.bfloat16

D = 1024
HEAD_DIM = 64
SB_W = 512
SWA_QW = 512
SWA_KW = 128
D_IN = 2304
D_FF = 2816
WINDOW = 128
ALPHA = 2.0 ** 0.25
LN_EPS = 1e-5
RMS_EPS = 1e-6
MASK_VALUE = -1e30
QK_SCALE = 1.0 / math.sqrt(HEAD_DIM)

ADAM_LR = 0.001
ADAM_B1 = 0.9
ADAM_B2 = 0.999
ADAM_EPS = 1e-08
ADAM_WD = 0.01
ADAM_STEP = 10

N_CHIPS = 4
N_DEV = 8
LANES = 128

SB_TQ = 512
SB_TK = 256
SB_DEAD_MASS = 110.0
TOK_TILE = 512
FFN_TILE = 256
FFN_BWD_TILE = 256
VMEM_LIMIT = 56 * 1024 * 1024
WGRAD_TOKENS = 2048
WGRAD_VMEM = 40 * 1024 * 1024

V_SH_A, V_SC_A, V_G_A, V_SH_F, V_SC_F, V_G_F, V_LN1G, V_LN1B, V_LN2G, V_LN2B, V_GN = range(11)
VEC_ROWS = 16

SM_MOD = 0
SM_BIN = 6 * D
SM_LN1G = SM_BIN + D_IN
SM_LN1B = SM_LN1G + D
SM_LN2G = SM_LN1B + D
SM_LN2B = SM_LN2G + D
SM_GN = SM_LN2B + D
SM_SINK = SM_GN + D
SM_LOSS = SM_SINK + LANES
SM_LEN = SM_LOSS + LANES
SM_PAD = -(-SM_LEN // (8 * LANES)) * (8 * LANES)

MESH = pl.DeviceIdType.MESH


def _cparams(**kw):
    return pltpu.CompilerParams(vmem_limit_bytes=VMEM_LIMIT, **kw)


def _resident(shape):
    nd = len(shape)
    return pl.BlockSpec(shape, lambda *_: (0,) * nd, pipeline_mode=pl.Buffered(1))


def _dot(a, b):
    return jnp.dot(a, b, preferred_element_type=F32)


def _dot_nt(a, b):
    return lax.dot_general(a, b, (((1,), (1,)), ((), ())), preferred_element_type=F32)


def _dot_tn(a, b):
    return lax.dot_general(a, b, (((0,), (0,)), ((), ())), preferred_element_type=F32)


def _sum_matrix(tk, keep):
    row = lax.broadcasted_iota(jnp.int32, (tk, tk + LANES), 0)
    col = lax.broadcasted_iota(jnp.int32, (tk, tk + LANES), 1)
    return (keep(row, col) | (col >= tk)).astype(BF16)


def _block_sums(x, m):
    tk = x.shape[1]
    res = _dot(x.astype(BF16), m)
    return res[:, :tk], res[:, tk:]


def _before(t0, n, s0, tk):
    return s0 + lax.broadcasted_iota(jnp.int32, (n, tk), 1) < t0 + lax.broadcasted_iota(jnp.int32, (n, tk), 0)


class _RowState:
    def __init__(self, ref, tk, add=False):
        self.ref, self.tk, self.add, self.vals = ref, tk, add, {}

    def _blocks(self, r0, n):
        return range(r0 // self.tk, (r0 + n) // self.tk)

    def get(self, hd, r0, n):
        for d in self._blocks(r0, n):
            if (hd, d) not in self.vals:
                self.vals[(hd, d)] = self.ref[hd, d * self.tk:(d + 1) * self.tk, :]
        parts = [self.vals[(hd, d)] for d in self._blocks(r0, n)]
        return parts[0] if len(parts) == 1 else jnp.concatenate(parts, axis=0)

    def put(self, hd, r0, n, val):
        for k, d in enumerate(self._blocks(r0, n)):
            self.vals[(hd, d)] = val[k * self.tk:(k + 1) * self.tk]

    def accumulate(self, hd, r0, n, val):
        for k, d in enumerate(self._blocks(r0, n)):
            part = val[k * self.tk:(k + 1) * self.tk]
            self.vals[(hd, d)] = part if (hd, d) not in self.vals else self.vals[(hd, d)] + part

    def store(self):
        for (hd, d), val in self.vals.items():
            span = slice(d * self.tk, (d + 1) * self.tk)
            if self.add:
                self.ref[hd, span, :] += val
            else:
                self.ref[hd, span, :] = val


def _across(v, tk):
    return jnp.concatenate([v] * (tk // LANES), axis=1)


def _allgather8(v, name, gather=(), scatter=()):
    m_per, n = v.shape
    n_g, n_s = len(gather), len(scatter)

    def body(x_ref, *refs):
        g_in, s_in = refs[:n_g], refs[n_g:n_g + n_s]
        out_ref = refs[n_g + n_s]
        g_out, s_out = refs[n_g + n_s + 1:2 * n_g + n_s + 1], refs[2 * n_g + n_s + 1:2 * (n_g + n_s) + 1]
        send_sems, recv_sems, local_sem, *more_sems = refs[2 * (n_g + n_s) + 1:]
        halves = _HalvesGather(g_in, g_out, *more_sems[:5]) if n_g else None
        beside = ([halves] if n_g else []) + ([_scatter_exchange(s_in, s_out, *more_sems[-3:])] if n_s else [])
        for ex in beside:
            ex.start()
        x, y, c = lax.axis_index("x"), lax.axis_index("y"), lax.axis_index("c")
        me, sibling = (x, y, c), (x, y, 1 - c)
        chips = [(1 - x, y), (x, 1 - y), (1 - x, 1 - y)]

        def rows(px, py, pc):
            return out_ref.at[pl.ds((4 * px + 2 * py + pc) * m_per, m_per), :]

        def copy(k, block, to, src=None):
            return pltpu.make_async_remote_copy(
                src_ref=rows(*block) if src is None else src, dst_ref=rows(*block),
                send_sem=send_sems.at[k], recv_sem=recv_sems.at[k], device_id=to, device_id_type=MESH)

        mine = pltpu.make_async_copy(x_ref, rows(*me), local_sem)
        mine.start()
        first = [copy(0, me, sibling, src=x_ref)]
        first += [copy(1 + j, me, (*chip, c), src=x_ref) for j, chip in enumerate(chips)]
        for cp in first:
            cp.start()
        passed = [copy(4 + j, (*chip, c), sibling) for j, chip in enumerate(chips)]
        for j, chip in enumerate(chips):
            copy(1 + j, (*chip, c), me).wait_recv()
            passed[j].start()
        copy(0, sibling, me).wait_recv()
        for j, chip in enumerate(chips):
            copy(4 + j, (*chip, 1 - c), me).wait_recv()
        for cp in first + passed:
            cp.wait_send()
        mine.wait()
        if halves is not None:
            halves.forward()
        for ex in beside:
            ex.wait()

    hbm = pl.BlockSpec(memory_space=pl.ANY)
    return pl.pallas_call(
        body, name=name,
        out_shape=[jax.ShapeDtypeStruct((N_DEV * m_per, n), v.dtype)]
        + [jax.ShapeDtypeStruct((N_CHIPS,) + a.shape, a.dtype) for a in gather]
        + [jax.ShapeDtypeStruct(p.shape, p.dtype) for p in scatter],
        in_specs=[pl.BlockSpec(memory_space=pltpu.VMEM)] + [hbm] * (n_g + n_s),
        out_specs=[pl.BlockSpec(memory_space=pltpu.VMEM)] + [hbm] * (n_g + n_s),
        scratch_shapes=[pltpu.SemaphoreType.DMA((7,)), pltpu.SemaphoreType.DMA((7,)), pltpu.SemaphoreType.DMA]
        + (_halves_gather_sems(n_g) if n_g else [])
        + (_exchange_sems(n_s) if n_s else []),
        compiler_params=_cparams(),
    )(v, *gather, *scatter)


class _Exchange:
    def __init__(self, local, sends, arrivals):
        self.local, self.sends, self.arrivals = local, sends, arrivals

    def start(self):
        for cp in self.local + self.sends:
            cp.start()

    def wait(self):
        for cp in self.arrivals:
            cp.wait_recv()
        for cp in self.sends:
            cp.wait_send()
        for cp in self.local:
            cp.wait()


def _exchange_sems(n):
    return [pltpu.SemaphoreType.DMA((3 * n,)), pltpu.SemaphoreType.DMA((3 * n,)), pltpu.SemaphoreType.DMA((n,))]


class _HalvesGather:
    def __init__(self, ins, outs, far_send, far_recv, near_send, near_recv, local_sems):
        x, y, c = lax.axis_index("x"), lax.axis_index("y"), lax.axis_index("c")
        slot = 2 * x + y
        chips = [(1 - x, y), (x, 1 - y), (1 - x, 1 - y)]
        self.local, self.far, self.landed, self.near, self.passed = [], [], [], [], []
        for a in range(len(ins)):
            h = ins[a].shape[0] // 2
            mine, theirs = pl.ds(c * h, h), pl.ds((1 - c) * h, h)
            self.local.append(pltpu.make_async_copy(ins[a], outs[a].at[slot], local_sems.at[a]))
            for j, (px, py) in enumerate(chips):
                k, there = 3 * a + j, 2 * px + py
                far = dict(send_sem=far_send.at[k], recv_sem=far_recv.at[k], device_id=(px, py, c), device_id_type=MESH)
                near = dict(send_sem=near_send.at[k], recv_sem=near_recv.at[k], device_id=(x, y, 1 - c),
                            device_id_type=MESH)
                self.far.append(pltpu.make_async_remote_copy(
                    src_ref=ins[a].at[mine], dst_ref=outs[a].at[slot, mine], **far))
                self.landed.append(pltpu.make_async_remote_copy(
                    src_ref=ins[a].at[mine], dst_ref=outs[a].at[there, mine], **far))
                self.near.append(pltpu.make_async_remote_copy(
                    src_ref=outs[a].at[there, mine], dst_ref=outs[a].at[there, mine], **near))
                self.passed.append(pltpu.make_async_remote_copy(
                    src_ref=outs[a].at[there, mine], dst_ref=outs[a].at[there, theirs], **near))

    def start(self):
        for cp in self.local + self.far:
            cp.start()

    def forward(self):
        for landed, near in zip(self.landed, self.near):
            landed.wait_recv()
            near.start()

    def wait(self):
        for cp in self.passed:
            cp.wait_recv()
        for cp in self.far + self.near:
            cp.wait_send()
        for cp in self.local:
            cp.wait()


def _halves_gather_sems(n):
    return [pltpu.SemaphoreType.DMA((3 * n,))] * 4 + [pltpu.SemaphoreType.DMA((n,))]


def _scatter_exchange(p_refs, out_refs, send_sems, recv_sems, local_sems):
    x, y, c = lax.axis_index("x"), lax.axis_index("y"), lax.axis_index("c")
    slot = 2 * x + y
    chips = [(1 - x, y), (x, 1 - y), (1 - x, 1 - y)]
    local, sends, arrivals = [], [], []
    for a, (p_ref, out_ref) in enumerate(zip(p_refs, out_refs)):
        local.append(pltpu.make_async_copy(p_ref.at[slot], out_ref.at[slot], local_sems.at[a]))
        for j, (px, py) in enumerate(chips):
            sems = dict(send_sem=send_sems.at[3 * a + j], recv_sem=recv_sems.at[3 * a + j],
                        device_id=(px, py, c), device_id_type=MESH)
            sends.append(pltpu.make_async_remote_copy(src_ref=p_ref.at[2 * px + py], dst_ref=out_ref.at[slot], **sems))
            arrivals.append(pltpu.make_async_remote_copy(src_ref=p_ref.at[slot], dst_ref=out_ref.at[2 * px + py], **sems))
    return _Exchange(local, sends, arrivals)


def _sibling_halves(give_refs, got_refs, send_sems, recv_sems):
    x, y, c = lax.axis_index("x"), lax.axis_index("y"), lax.axis_index("c")
    copies = []
    for a, (give_ref, got_ref) in enumerate(zip(give_refs, got_refs)):
        for s in range(N_CHIPS):
            copies.append(pltpu.make_async_remote_copy(
                src_ref=give_ref.at[s, 1 - c], dst_ref=got_ref.at[s], send_sem=send_sems.at[N_CHIPS * a + s],
                recv_sem=recv_sems.at[N_CHIPS * a + s], device_id=(x, y, 1 - c), device_id_type=MESH))
    return copies


def _halves_shapes(arrs):
    return [jax.ShapeDtypeStruct((a.shape[0],) + a.shape[2:], a.dtype) for a in arrs]


def _halves_sems(n):
    return [pltpu.SemaphoreType.DMA((N_CHIPS * n,)), pltpu.SemaphoreType.DMA((N_CHIPS * n,))]


def _halves_swap(arrs, name):
    n = len(arrs)

    def body(*refs):
        copies = _sibling_halves(refs[:n], refs[n:2 * n], *refs[2 * n:])
        for cp in copies:
            cp.start()
        for cp in copies:
            cp.wait()

    hbm = pl.BlockSpec(memory_space=pl.ANY)
    return pl.pallas_call(body, name=name, out_shape=_halves_shapes(arrs), in_specs=[hbm] * n, out_specs=[hbm] * n,
                          scratch_shapes=_halves_sems(n), compiler_params=_cparams())(*arrs)


def _sibling_send(arrs, name):
    n = len(arrs)

    def body(*refs):
        x, y, c = lax.axis_index("x"), lax.axis_index("y"), lax.axis_index("c")
        send_sems, recv_sems = refs[2 * n:]
        copies = [pltpu.make_async_remote_copy(src_ref=refs[a], dst_ref=refs[n + a], send_sem=send_sems.at[a],
                                               recv_sem=recv_sems.at[a], device_id=(x, y, 1 - c), device_id_type=MESH)
                  for a in range(n)]
        for cp in copies:
            cp.start()
        for cp in copies:
            cp.wait()

    hbm = pl.BlockSpec(memory_space=pl.ANY)
    return pl.pallas_call(
        body, name=name, out_shape=[jax.ShapeDtypeStruct(a.shape, a.dtype) for a in arrs],
        in_specs=[hbm] * n, out_specs=[hbm] * n,
        scratch_shapes=[pltpu.SemaphoreType.DMA((n,)), pltpu.SemaphoreType.DMA((n,))],
        compiler_params=_cparams(),
    )(*arrs)


def _row_tile(h):
    return h // 2 if (h // 2) % 8 == 0 else h


def _row_tiles(hs):
    tiles = [_row_tile(h) for h in hs]
    assert len({h // t for h, t in zip(hs, tiles)}) == 1
    return tiles, hs[0] // tiles[0]


def _chip_sums(arrs, gots, core, name, out_dtype=F32):
    n = len(arrs)
    tiles, steps = _row_tiles([a.shape[2] for a in arrs])

    def body(core_ref, *refs):
        for a_ref, b_ref, o_ref in zip(refs[:n], refs[n:2 * n], refs[2 * n:]):
            o_ref[...] = (a_ref[...] + b_ref[...]).astype(out_dtype)

    slabs = [pl.BlockSpec((None, tr, a.shape[3]), lambda s, i, core_ref: (s, i, 0)) for a, tr in zip(arrs, tiles)]
    grid_spec = pltpu.PrefetchScalarGridSpec(
        num_scalar_prefetch=1, grid=(N_CHIPS, steps),
        in_specs=[pl.BlockSpec((None, None, tr, a.shape[3]), lambda s, i, core_ref: (s, core_ref[0], i, 0))
                  for a, tr in zip(arrs, tiles)] + slabs,
        out_specs=slabs)
    return pl.pallas_call(body, name=name, grid_spec=grid_spec,
                          out_shape=[jax.ShapeDtypeStruct(g.shape, out_dtype) for g in gots],
                          compiler_params=_cparams())(core, *arrs, *gots)


def _sum4s(ps, name):
    tiles, steps = _row_tiles([p.shape[1] for p in ps])

    def body(*refs):
        for p_ref, o_ref in zip(refs[:len(ps)], refs[len(ps):]):
            o_ref[...] = ((p_ref[0].astype(F32) + p_ref[1].astype(F32)) + p_ref[2].astype(F32)) + p_ref[3].astype(F32)

    return pl.pallas_call(
        body, name=name, grid=(steps,), out_shape=[jax.ShapeDtypeStruct(p.shape[1:], F32) for p in ps],
        in_specs=[pl.BlockSpec((4, tr, p.shape[2]), lambda i: (0, i, 0)) for p, tr in zip(ps, tiles)],
        out_specs=[pl.BlockSpec((tr, p.shape[2]), lambda i: (i, 0)) for p, tr in zip(ps, tiles)],
        compiler_params=_cparams())(*ps)


def _adam_math(w, g, m, v):
    m2 = ADAM_B1 * m + (1.0 - ADAM_B1) * g
    v2 = ADAM_B2 * v + (1.0 - ADAM_B2) * (g * g)
    m_hat = m2 / (1.0 - ADAM_B1 ** ADAM_STEP)
    v_hat = v2 / (1.0 - ADAM_B2 ** ADAM_STEP)
    delta = -ADAM_LR * (m_hat / (jnp.sqrt(v_hat) + ADAM_EPS) + ADAM_WD * w)
    return delta, m2, v2


def _adamw(w, g, m, v, name):
    rows, cols = w.shape
    tr = rows // 4 if rows % 32 == 0 else rows

    def body(w_ref, g_ref, m_ref, v_ref, d_ref, m2_ref, v2_ref):
        delta, m2, v2 = _adam_math(w_ref[...], g_ref[...], m_ref[...], v_ref[...])
        d_ref[...] = delta
        m2_ref[...] = m2
        v2_ref[...] = v2

    spec = pl.BlockSpec((tr, cols), lambda i: (i, 0))
    shp = jax.ShapeDtypeStruct(w.shape, F32)
    return pl.pallas_call(body, name=name, grid=(rows // tr,), out_shape=[shp, shp, shp],
                          in_specs=[spec] * 4, out_specs=[spec] * 3, compiler_params=_cparams())(w, g, m, v)


def _small_update(g8, offsets, ws, ms, vs, loss_at, name):
    k = len(ws)

    def summed(g8_ref, lo, width):
        g = g8_ref[0:1, lo:lo + width]
        for r in range(1, N_DEV):
            g = g + g8_ref[r:r + 1, lo:lo + width]
        return g

    def body(g8_ref, *refs):
        ins, outs = refs[:3 * k], refs[3 * k:]
        for j in range(k):
            g = summed(g8_ref, offsets[j], ws[j].shape[1])
            delta, m2, v2 = _adam_math(ins[j][...], g, ins[k + j][...], ins[2 * k + j][...])
            for kind, val in enumerate((g, delta, m2, v2)):
                outs[kind * k + j][...] = val
        outs[4 * k][...] = summed(g8_ref, loss_at, LANES)

    vm = pl.BlockSpec(memory_space=pltpu.VMEM)
    shapes = [jax.ShapeDtypeStruct(w.shape, F32) for w in ws] * 4 + [jax.ShapeDtypeStruct((1, LANES), F32)]
    res = pl.pallas_call(body, name=name, out_shape=shapes, in_specs=[vm] * (1 + 3 * k), out_specs=[vm] * (4 * k + 1),
                         compiler_params=_cparams())(g8, *ws, *ms, *vs)
    return res[:k], res[k:2 * k], res[2 * k:3 * k], res[3 * k:4 * k], res[4 * k]


def _mod_shard(c8, w_ada, b_ada_shard, name):
    n = w_ada.shape[1]
    tn = 512

    def body(c_ref, w_ref, b_ref, o_ref, s_ref):
        cv = c_ref[...]
        sc = cv * (1.0 / (1.0 + jnp.exp(-cv)))
        s_ref[...] = sc
        o_ref[...] = _dot(sc.astype(BF16), w_ref[...].astype(BF16)) + b_ref[...]

    return pl.pallas_call(
        body, name=name, grid=(n // tn,),
        out_shape=[jax.ShapeDtypeStruct((8, n), F32), jax.ShapeDtypeStruct((8, D), F32)],
        in_specs=[pl.BlockSpec((8, D), lambda j: (0, 0)), pl.BlockSpec((D, tn), lambda j: (0, j)),
                  pl.BlockSpec((1, tn), lambda j: (0, j))],
        out_specs=[pl.BlockSpec((8, tn), lambda j: (0, j)), pl.BlockSpec((8, D), lambda j: (0, 0))],
        compiler_params=_cparams())(c8, w_ada, b_ada_shard)


def _layer_norm_stats(u):
    mu = jnp.mean(u, axis=1, keepdims=True)
    d = u - mu
    var = jnp.mean(d * d, axis=1, keepdims=True)
    rstd = lax.rsqrt(var + LN_EPS)
    return d * rstd, rstd


def _in_proj(x, vec, w_in, b_in, name):
    s = x.shape[0]
    tb = min(TOK_TILE, s)

    def body(x_ref, vec_ref, w_ref, b_ref, ht_ref, p_ref):
        h = x_ref[...] * (1.0 + vec_ref[V_SC_A:V_SC_A + 1, :]) + vec_ref[V_SH_A:V_SH_A + 1, :]
        hb = h.astype(BF16)
        ht_ref[...] = h.T.astype(BF16)
        proj = _dot(hb, w_ref[...]) + b_ref[...]
        col = lax.broadcasted_iota(jnp.int32, (1, D_IN), 1)
        is_q = (col < SB_W) | ((col >= 3 * SB_W) & (col < 3 * SB_W + SWA_QW))
        p_ref[...] = (proj * jnp.where(is_q, QK_SCALE, 1.0)).astype(BF16)

    return pl.pallas_call(
        body, name=name, grid=(s // tb,),
        out_shape=[jax.ShapeDtypeStruct((D, s), BF16), jax.ShapeDtypeStruct((s, D_IN), BF16)],
        in_specs=[pl.BlockSpec((tb, D), lambda i: (i, 0)), _resident((VEC_ROWS, D)), _resident((D, D_IN)),
                  _resident((1, D_IN))],
        out_specs=[pl.BlockSpec((D, tb), lambda i: (0, i)), pl.BlockSpec((tb, D_IN), lambda i: (i, 0))],
        compiler_params=_cparams())(x, vec, w_in, b_in)


def _softplus_parts(z):
    e1 = jnp.exp(-jnp.abs(z))
    sp = jnp.maximum(z, 0.0) + jnp.log(1.0 + e1)
    return sp, e1


def _sb_forward(proj, shards, name):
    s = proj.shape[0]
    tq, tk = min(SB_TQ, s), min(SB_TK, s)
    r = tq // tk

    n_sh = len(shards)
    nkb = SB_W // LANES
    nq = s // tq

    def body(q_ref, k_ref, v_ref, *refs):
        sh_refs, (o_ref, tot_ref, start_ref), got_refs = refs[:n_sh], refs[n_sh:n_sh + 3], refs[n_sh + 3:2 * n_sh + 3]
        acc_refs, run_refs = refs[2 * n_sh + 3:2 * n_sh + 5]
        i = pl.program_id(1)
        step = pl.program_id(0) * nq + i
        gather = _HalvesGather(sh_refs, got_refs, *refs[2 * n_sh + 5:])

        @pl.when(step == 0)
        def _():
            gather.start()

        @pl.when(step == nkb * nq // 2)
        def _():
            gather.forward()

        lane = lax.broadcasted_iota(jnp.int32, (1, LANES), 1)
        first = lane < HEAD_DIM
        qp = q_ref[...]
        zero = jnp.zeros((), BF16)
        qs = (jnp.where(first, qp, zero), jnp.where(first, zero, qp))
        later = _sum_matrix(tk, lambda row, col: row > col)
        acc_refs[...] = jnp.zeros_like(acc_refs)
        run_refs[...] = jnp.zeros_like(run_refs)

        def blocks(tiles):
            rows = [slice(r0, r0 + n) for r0, n, _, _ in tiles]
            kjs = [k_ref[pl.ds(pl.multiple_of(j * tk, tk), tk), :] for _, _, j, _ in tiles]
            vjs = [v_ref[pl.ds(pl.multiple_of(j * tk, tk), tk), :] for _, _, j, _ in tiles]
            chains = [(hd, t) for t in range(len(tiles)) for hd in range(2)]
            zs = [_dot_nt(qs[hd][rows[t]], kjs[t]) for hd, t in chains]
            sps = [_softplus_parts(z)[0] for z in zs]
            befores = [_before(i * tq + r0, n, j * tk, tk) if diag else None for r0, n, j, diag in tiles]
            spms = [sp if befores[t] is None else jnp.where(befores[t], sp, 0.0) for (hd, t), sp in zip(chains, sps)]
            cums = [_block_sums(spm, later) for spm in spms]
            runs, accs, ws = _RowState(run_refs, tk), _RowState(acc_refs, tk, add=True), []
            for (hd, t), z, sp, (cum, sm) in zip(chains, zs, sps, cums):
                r0, n = tiles[t][:2]
                run = runs.get(hd, r0, n)
                w = jnp.exp(z - sp - cum - _across(run, tk))
                if befores[t] is not None:
                    w = jnp.where(befores[t], w, 0.0)
                ws.append(w.astype(BF16))
                runs.put(hd, r0, n, run + sm)
            for (hd, t), pv in zip(chains, [_dot(w, vjs[t]) for (hd, t), w in zip(chains, ws)]):
                accs.accumulate(hd, *tiles[t][:2], pv)
            accs.store()
            runs.store()

        below = i * r
        diagonal = [(d * tk, tk, below + e, e == d) for d in range(r) for e in range(d, -1, -1)]

        @pl.when(i == 0)
        def _():
            blocks(diagonal)

        @pl.when(i > 0)
        def _():
            blocks(diagonal + [(0, tq, below - 1, False)])

        def swept_mass():
            return jnp.min(jnp.minimum(run_refs[0], run_refs[1]))

        def more(carry):
            n, mass = carry
            return (n < below) & (mass < SB_DEAD_MASS)

        def sweep(carry):
            n, _ = carry
            blocks([(0, tq, below - 1 - n, False)])
            return n + 1, swept_mass()

        n_swept, _ = lax.while_loop(more, sweep, (jnp.minimum(below, 1), swept_mass()))
        start_ref[pl.program_id(0), i] = (below - n_swept).astype(F32)
        o_ref[...] = jnp.where(first, acc_refs[0], acc_refs[1])
        tot_ref[...] = jnp.where(first, run_refs[0], run_refs[1])

        @pl.when(step == nkb * nq - 1)
        def _():
            gather.wait()

    shp = jax.ShapeDtypeStruct((s, SB_W), F32)
    qspec = pl.BlockSpec((tq, LANES), lambda p, i: (i, p))
    hbm = pl.BlockSpec(memory_space=pl.ANY)
    return pl.pallas_call(
        body, name=name, grid=(nkb, nq),
        out_shape=[shp, shp, jax.ShapeDtypeStruct((nkb, nq), F32)]
        + [jax.ShapeDtypeStruct((N_CHIPS,) + a.shape, a.dtype) for a in shards],
        in_specs=[qspec,
                  pl.BlockSpec((s, LANES), lambda p, i: (0, nkb + p)),
                  pl.BlockSpec((s, LANES), lambda p, i: (0, 2 * nkb + p))] + [hbm] * n_sh,
        out_specs=[qspec, qspec, pl.BlockSpec(memory_space=pltpu.SMEM)] + [hbm] * n_sh,
        scratch_shapes=[pltpu.VMEM((2, tq, LANES), F32), pltpu.VMEM((2, tq, LANES), F32)] + _halves_gather_sems(n_sh),
        compiler_params=_cparams())(proj, proj, proj, *shards)


def _swa_masks(n):
    ti = lax.broadcasted_iota(jnp.int32, (WINDOW, 2 * WINDOW), 0)
    kj = lax.broadcasted_iota(jnp.int32, (WINDOW, 2 * WINDOW), 1)
    dist = ti + WINDOW - kj
    valid = (dist >= 0) & (dist < WINDOW) & ((n * WINDOW - WINDOW + kj) >= 0)
    return valid, dist.astype(F32)


def _swa_probs(sc, valid, distf, h, sink):
    slope = 2.0 ** (-(h + 1))
    sc = jnp.where(valid, sc - slope * distf, MASK_VALUE)
    mx = jnp.maximum(jnp.max(sc, axis=1, keepdims=True), sink)
    p = jnp.exp(sc - mx)
    es = jnp.exp(sink - mx)
    inv = 1.0 / (jnp.sum(p, axis=1, keepdims=True) + es)
    return p * inv, es * inv


def _swa_forward(proj, sinks, shards, name):
    s = proj.shape[0]
    nb = s // WINDOW
    qb, kb, vb = 3 * SB_W // SWA_QW, (3 * SB_W + SWA_QW) // LANES, (3 * SB_W + SWA_QW + SWA_KW) // LANES
    n_sh = len(shards)

    def body(q_ref, kp_ref, kc_ref, vp_ref, vc_ref, sink_ref, *refs):
        sh_refs, o_ref, got_refs = refs[:n_sh], refs[n_sh], refs[n_sh + 1:2 * n_sh + 1]
        n = pl.program_id(0)
        gather = _HalvesGather(sh_refs, got_refs, *refs[2 * n_sh + 1:])

        @pl.when(n == 0)
        def _():
            gather.start()

        @pl.when(n == nb // 2)
        def _():
            gather.forward()

        k = jnp.concatenate([kp_ref[...], kc_ref[...]], axis=0)
        v = jnp.concatenate([vp_ref[...], vc_ref[...]], axis=0)
        k_sw = pltpu.roll(k.astype(F32), HEAD_DIM, 1).astype(BF16)
        v_sw = pltpu.roll(v.astype(F32), HEAD_DIM, 1).astype(BF16)
        lane = lax.broadcasted_iota(jnp.int32, (1, LANES), 1)
        halves = [lane < HEAD_DIM, lane >= HEAD_DIM]
        valid, distf = _swa_masks(n)
        heads = range(2 * 4)
        qms = [jnp.where(halves[h % 2], q_ref[:, (h // 2) * LANES:(h // 2 + 1) * LANES], jnp.zeros((), BF16))
               for h in heads]
        kus = [k if h // 4 == h % 2 else k_sw for h in heads]
        vus = [v if h // 4 == h % 2 else v_sw for h in heads]
        scores = [_dot_nt(qms[h], kus[h]) for h in heads]
        ps = [_swa_probs(scores[h], valid, distf, h, sink_ref[h])[0].astype(BF16) for h in heads]
        outs = [_dot(ps[h], vus[h]) for h in heads]
        for pair in range(4):
            o_ref[:, pair * LANES:(pair + 1) * LANES] = jnp.where(halves[0], outs[2 * pair], outs[2 * pair + 1])

        @pl.when(n == nb - 1)
        def _():
            gather.wait()

    prev = lambda n: jnp.maximum(n - 1, 0)
    hbm = pl.BlockSpec(memory_space=pl.ANY)
    return pl.pallas_call(
        body, name=name, grid=(nb,),
        out_shape=[jax.ShapeDtypeStruct((s, SWA_QW), F32)]
        + [jax.ShapeDtypeStruct((N_CHIPS,) + a.shape, a.dtype) for a in shards],
        in_specs=[pl.BlockSpec((WINDOW, SWA_QW), lambda n: (n, qb)),
                  pl.BlockSpec((WINDOW, LANES), lambda n: (prev(n), kb)),
                  pl.BlockSpec((WINDOW, LANES), lambda n: (n, kb)),
                  pl.BlockSpec((WINDOW, LANES), lambda n: (prev(n), vb)),
                  pl.BlockSpec((WINDOW, LANES), lambda n: (n, vb)),
                  pl.BlockSpec(memory_space=pltpu.SMEM)] + [hbm] * n_sh,
        out_specs=[pl.BlockSpec((WINDOW, SWA_QW), lambda n: (n, 0))] + [hbm] * n_sh,
        scratch_shapes=_halves_gather_sems(n_sh),
        compiler_params=_cparams())(proj, proj, proj, proj, proj, sinks, *shards)


def _rms_parts(y):
    return lax.rsqrt(jnp.mean(y * y, axis=1, keepdims=True) + RMS_EPS)


def _post_attention(y_sb, y_sw, x, vec, w_out, name):
    s = x.shape[0]
    tb = min(TOK_TILE, s)

    def body(ysb_ref, ysw_ref, x_ref, vec_ref, w_ref, mixedt_ref, attn_ref, x1_ref, h2_ref, h2t_ref):
        ysb, ysw = ysb_ref[...], ysw_ref[...]
        nsb_f = ysb * _rms_parts(ysb) * vec_ref[V_GN:V_GN + 1, :SB_W]
        nsw_f = ysw * _rms_parts(ysw) * vec_ref[V_GN:V_GN + 1, SB_W:]
        nsb, nsw = nsb_f.astype(BF16), nsw_f.astype(BF16)
        mixedt_ref[:SB_W, :] = nsb_f.T.astype(BF16)
        mixedt_ref[SB_W:, :] = nsw_f.T.astype(BF16)
        attn = _dot(nsb, w_ref[:SB_W, :]) + _dot(nsw, w_ref[SB_W:, :])
        attn_ref[...] = attn
        u1 = ALPHA * x_ref[...] + (1.0 + vec_ref[V_G_A:V_G_A + 1, :]) * attn
        xhat, _ = _layer_norm_stats(u1)
        x1 = xhat * vec_ref[V_LN1G:V_LN1G + 1, :] + vec_ref[V_LN1B:V_LN1B + 1, :]
        x1_ref[...] = x1
        h2 = x1 * (1.0 + vec_ref[V_SC_F:V_SC_F + 1, :]) + vec_ref[V_SH_F:V_SH_F + 1, :]
        h2_ref[...] = h2.astype(BF16)
        h2t_ref[...] = h2.T.astype(BF16)

    half = pl.BlockSpec((tb, SB_W), lambda i: (i, 0))
    full = pl.BlockSpec((tb, D), lambda i: (i, 0))
    full_t = pl.BlockSpec((D, tb), lambda i: (0, i))
    return pl.pallas_call(
        body, name=name, grid=(s // tb,),
        out_shape=[jax.ShapeDtypeStruct((D, s), BF16), jax.ShapeDtypeStruct((s, D), F32),
                   jax.ShapeDtypeStruct((s, D), F32), jax.ShapeDtypeStruct((s, D), BF16),
                   jax.ShapeDtypeStruct((D, s), BF16)],
        in_specs=[half, half, full, _resident((VEC_ROWS, D)), _resident((D, D))],
        out_specs=[full_t, full, full, full, full_t],
        compiler_params=_cparams())(y_sb, y_sw, x, vec, w_out)


def _ffn_forward(h2, w_gu, w_down, name):
    s = h2.shape[0]
    tb = min(FFN_TILE, s)

    def body(h_ref, wgu_ref, wd_ref, gu_ref, actt_ref, ffn_ref):
        gu = _dot(h_ref[...], wgu_ref[...])
        gu_ref[...] = gu.astype(BF16)
        gate, up = gu[:, :D_FF], gu[:, D_FF:]
        act = gate * (1.0 / (1.0 + jnp.exp(-gate))) * up
        actt_ref[...] = act.T.astype(BF16)
        ffn_ref[...] = _dot(act.astype(BF16), wd_ref[...])

    return pl.pallas_call(
        body, name=name, grid=(s // tb,),
        out_shape=[jax.ShapeDtypeStruct((s, 2 * D_FF), BF16), jax.ShapeDtypeStruct((D_FF, s), BF16),
                   jax.ShapeDtypeStruct((s, D), F32)],
        in_specs=[pl.BlockSpec((tb, D), lambda i: (i, 0)), _resident((D, 2 * D_FF)), _resident((D_FF, D))],
        out_specs=[pl.BlockSpec((tb, 2 * D_FF), lambda i: (i, 0)), pl.BlockSpec((D_FF, tb), lambda i: (0, i)),
                   pl.BlockSpec((tb, D), lambda i: (i, 0))],
        compiler_params=_cparams())(h2, w_gu, w_down)


def _layer_norm_bwd(dxhat, xhat, rstd):
    m1 = jnp.mean(dxhat, axis=1, keepdims=True)
    m2 = jnp.mean(dxhat * xhat, axis=1, keepdims=True)
    return rstd * (dxhat - m1 - xhat * m2)


def _colsum(a):
    return jnp.sum(a, axis=0, keepdims=True)


A_LN2G, A_LN2B, A_GF, A_SCF, A_SHF, A_LOSS = range(6)
B_LN1G, B_LN1B, B_GA, B_GN = range(4)
C_SCA, C_SHA = range(2)


def _ffn_backward(x1, ffn, target, gu, vec, w_gu, w_down, name):
    s = x1.shape[0]
    tb = min(FFN_BWD_TILE, s)

    def body(x1_ref, ffn_ref, t_ref, gu_ref, vec_ref, wgu_ref, wd_ref, dffn_ref, dgu_ref, dx1_ref, acc_ref):
        @pl.when(pl.program_id(0) == 0)
        def _():
            acc_ref[...] = jnp.zeros_like(acc_ref)

        x1v, ffn_v = x1_ref[...], ffn_ref[...]
        g_f = 1.0 + vec_ref[V_G_F:V_G_F + 1, :]
        u2 = ALPHA * x1v + g_f * ffn_v
        xhat, rstd = _layer_norm_stats(u2)
        ln_g = vec_ref[V_LN2G:V_LN2G + 1, :]
        err = xhat * ln_g + vec_ref[V_LN2B:V_LN2B + 1, :] - t_ref[...]
        dx2 = err * (1.0 / D)
        acc_ref[A_LOSS:A_LOSS + 1, :] += _colsum(err * err) * (0.5 / D)
        acc_ref[A_LN2G:A_LN2G + 1, :] += _colsum(dx2 * xhat)
        acc_ref[A_LN2B:A_LN2B + 1, :] += _colsum(dx2)
        du2 = _layer_norm_bwd(dx2 * ln_g, xhat, rstd)
        acc_ref[A_GF:A_GF + 1, :] += _colsum(du2 * ffn_v)
        dffn = (g_f * du2).astype(BF16)
        dffn_ref[...] = dffn
        dact = _dot_nt(dffn, wd_ref[...])
        gate, up = gu_ref[:, :D_FF].astype(F32), gu_ref[:, D_FF:].astype(F32)
        sg = 1.0 / (1.0 + jnp.exp(-gate))
        dgate = (dact * up * (sg * (1.0 + gate * (1.0 - sg)))).astype(BF16)
        dup = (dact * (gate * sg)).astype(BF16)
        dgu_ref[:, :D_FF] = dgate
        dgu_ref[:, D_FF:] = dup
        dh2 = _dot_nt(dgate, wgu_ref[:, :D_FF]) + _dot_nt(dup, wgu_ref[:, D_FF:])
        dx1_ref[...] = ALPHA * du2 + dh2 * (1.0 + vec_ref[V_SC_F:V_SC_F + 1, :])
        acc_ref[A_SCF:A_SCF + 1, :] += _colsum(dh2 * x1v)
        acc_ref[A_SHF:A_SHF + 1, :] += _colsum(dh2)

    full = pl.BlockSpec((tb, D), lambda i: (i, 0))
    wide = pl.BlockSpec((tb, 2 * D_FF), lambda i: (i, 0))
    return pl.pallas_call(
        body, name=name, grid=(s // tb,),
        out_shape=[jax.ShapeDtypeStruct((s, D), BF16), jax.ShapeDtypeStruct((s, 2 * D_FF), BF16),
                   jax.ShapeDtypeStruct((s, D), F32), jax.ShapeDtypeStruct((8, D), F32)],
        in_specs=[full, full, full, wide, _resident((VEC_ROWS, D)), _resident((D, 2 * D_FF)), _resident((D_FF, D))],
        out_specs=[full, wide, full, pl.BlockSpec((8, D), lambda i: (0, 0))],
        compiler_params=_cparams())(x1, ffn, target, gu, vec, w_gu, w_down)


def _attn_out_backward(dx1, x, attn, y_sb, y_sw, vec, w_out, name):
    s = x.shape[0]
    tb = min(TOK_TILE, s)

    def body(dx1_ref, x_ref, attn_ref, ysb_ref, ysw_ref, vec_ref, w_ref, du1_ref, dattn_ref, dy_ref, acc_ref):
        @pl.when(pl.program_id(0) == 0)
        def _():
            acc_ref[...] = jnp.zeros_like(acc_ref)

        attn = attn_ref[...]
        g_a = 1.0 + vec_ref[V_G_A:V_G_A + 1, :]
        xhat, rstd = _layer_norm_stats(ALPHA * x_ref[...] + g_a * attn)
        dx1v = dx1_ref[...]
        acc_ref[B_LN1G:B_LN1G + 1, :] += _colsum(dx1v * xhat)
        acc_ref[B_LN1B:B_LN1B + 1, :] += _colsum(dx1v)
        du1 = _layer_norm_bwd(dx1v * vec_ref[V_LN1G:V_LN1G + 1, :], xhat, rstd)
        du1_ref[...] = du1
        acc_ref[B_GA:B_GA + 1, :] += _colsum(du1 * attn)
        dattn = (g_a * du1).astype(BF16)
        dattn_ref[...] = dattn
        dmixed = _dot_nt(dattn, w_ref[...])
        for lo, y_ref in ((0, ysb_ref), (SB_W, ysw_ref)):
            y = y_ref[...]
            rr = _rms_parts(y)
            dn = dmixed[:, lo:lo + SB_W]
            acc_ref[B_GN:B_GN + 1, lo:lo + SB_W] += _colsum(dn * y * rr)
            dng = dn * vec_ref[V_GN:V_GN + 1, lo:lo + SB_W]
            dy_ref[:, lo:lo + SB_W] = rr * dng - y * (rr * rr * rr) * jnp.mean(dng * y, axis=1, keepdims=True)

    half = pl.BlockSpec((tb, SB_W), lambda i: (i, 0))
    full = pl.BlockSpec((tb, D), lambda i: (i, 0))
    return pl.pallas_call(
        body, name=name, grid=(s // tb,),
        out_shape=[jax.ShapeDtypeStruct((s, D), F32), jax.ShapeDtypeStruct((s, D), BF16),
                   jax.ShapeDtypeStruct((s, D), F32), jax.ShapeDtypeStruct((8, D), F32)],
        in_specs=[full, full, full, half, half, _resident((VEC_ROWS, D)), _resident((D, D))],
        out_specs=[full, full, full, pl.BlockSpec((8, D), lambda i: (0, 0))],
        compiler_params=_cparams())(dx1, x, attn, y_sb, y_sw, vec, w_out)


def _sb_backward(proj, sp_total, sweep_start, dy, slabs, name):
    s = proj.shape[0]
    tq, tk = min(SB_TQ, s), min(SB_TK, s)
    r = tq // tk
    nkb = SB_W // LANES
    nq = s // tq

    n_sl = len(slabs)

    def body(q_ref, k_ref, v_ref, tot_ref, do_ref, start_ref, *refs):
        slab_refs, (dq_ref, dk_ref, dv_ref), got_refs = refs[:n_sl], refs[n_sl:n_sl + 3], refs[n_sl + 3:2 * n_sl + 3]
        dq_acc, left_refs, gsum_refs = refs[2 * n_sl + 3:2 * n_sl + 6]
        i = pl.program_id(1)
        step = pl.program_id(0) * nq + i
        scatter = _scatter_exchange(slab_refs, got_refs, *refs[2 * n_sl + 6:])

        @pl.when(step == 0)
        def _():
            scatter.start()

        @pl.when(i == 0)
        def _():
            dk_ref[...] = jnp.zeros_like(dk_ref)
            dv_ref[...] = jnp.zeros_like(dv_ref)

        lane = lax.broadcasted_iota(jnp.int32, (1, LANES), 1)
        first = lane < HEAD_DIM
        qp, dop, totp = q_ref[...], do_ref[...], tot_ref[...]
        zero = jnp.zeros((), BF16)
        qs = (jnp.where(first, qp, zero), jnp.where(first, zero, qp))
        dofs = (jnp.where(first, dop, 0.0), jnp.where(first, 0.0, dop))
        dobs = tuple(d.astype(BF16) for d in dofs)
        dots = tuple(d.T.astype(BF16) for d in dofs)
        qts = tuple(qh.astype(F32).T.astype(BF16) for qh in qs)
        later = _sum_matrix(tk, lambda row, col: row > col)
        earlier = _sum_matrix(tk, lambda row, col: row < col)
        dq_acc[...] = jnp.zeros_like(dq_acc)
        gsum_refs[...] = jnp.zeros_like(gsum_refs)
        swapped = pltpu.roll(totp, HEAD_DIM, 1)
        left_refs[0] = jnp.where(first, totp, swapped)
        left_refs[1] = jnp.where(first, swapped, totp)

        def blocks(tiles):
            rows = [slice(r0, r0 + n) for r0, n, _, _ in tiles]
            kjs = [k_ref[pl.ds(pl.multiple_of(j * tk, tk), tk), :] for _, _, j, _ in tiles]
            vjs = [v_ref[pl.ds(pl.multiple_of(j * tk, tk), tk), :] for _, _, j, _ in tiles]
            chains = [(hd, t) for t in range(len(tiles)) for hd in range(2)]
            zs = [_dot_nt(qs[hd][rows[t]], kjs[t]) for hd, t in chains]
            dws = [_dot_nt(dobs[hd][rows[t]], vjs[t]) for hd, t in chains]
            parts = [_softplus_parts(z) for z in zs]
            sps = [p[0] for p in parts]
            befores = [_before(i * tq + r0, n, j * tk, tk) if diag else None for r0, n, j, diag in tiles]
            spms = [sp if befores[t] is None else jnp.where(befores[t], sp, 0.0) for (hd, t), sp in zip(chains, sps)]
            cums = [_block_sums(spm, later) for spm in spms]
            lefts, gsums, dq_sums = _RowState(left_refs, tk), _RowState(gsum_refs, tk), _RowState(dq_acc, tk, add=True)
            ws = []
            for (hd, t), z, sp, (cum, sm) in zip(chains, zs, sps, cums):
                r0, n = tiles[t][:2]
                left = lefts.get(hd, r0, n) - sm
                lefts.put(hd, r0, n, left)
                w = jnp.exp(z - sp - cum - _across(left, tk))
                ws.append(w if befores[t] is None else jnp.where(befores[t], w, 0.0))
            wbs = [w.astype(BF16) for w in ws]
            dvs = [_dot(dots[hd][:, rows[t]], wb) for (hd, t), wb in zip(chains, wbs)]
            gs = [dw * w for dw, w in zip(dws, ws)]
            gcums = [_block_sums(g, earlier) for g in gs]
            dzbs = []
            for (hd, t), z, (sp, e1), g, (gcum, gsm) in zip(chains, zs, parts, gs, gcums):
                r0, n = tiles[t][:2]
                gsum = gsums.get(hd, r0, n)
                inv = 1.0 / (1.0 + e1)
                sig = jnp.where(z >= 0.0, inv, e1 * inv)
                dz = g - sig * (g + _across(gsum, tk) + gcum)
                dzbs.append((dz if befores[t] is None else jnp.where(befores[t], dz, 0.0)).astype(BF16))
                gsums.put(hd, r0, n, gsum + gsm)
            dqs = [_dot(dzb, kjs[t]) for (hd, t), dzb in zip(chains, dzbs)]
            dks = [_dot(qts[hd][:, rows[t]], dzb) for (hd, t), dzb in zip(chains, dzbs)]
            for t, (_, _, j, _) in enumerate(tiles):
                dv_ref[j] += dvs[2 * t] + dvs[2 * t + 1]
                dk_ref[j] += dks[2 * t] + dks[2 * t + 1]
            for (hd, t), dq in zip(chains, dqs):
                dq_sums.accumulate(hd, *tiles[t][:2], dq)
            dq_sums.store()
            lefts.store()
            gsums.store()

        below = i * r
        start = jnp.clip(start_ref[pl.program_id(0), i].astype(jnp.int32), 0, below)

        def sweep(n, carry):
            blocks([(0, tq, start + n, False)])
            return carry

        lax.fori_loop(0, jnp.maximum(below - 1 - start, 0), sweep, 0)
        diagonal = [(d * tk, tk, below + e, e == d) for d in range(r) for e in range(d + 1)]

        @pl.when(i == 0)
        def _():
            blocks(diagonal)

        @pl.when(i > 0)
        def _():
            blocks([(0, tq, below - 1, False)] + diagonal)
        dq_ref[...] = jnp.where(first, dq_acc[0], dq_acc[1])

        @pl.when(step == nkb * nq - 1)
        def _():
            scatter.wait()

    shp = jax.ShapeDtypeStruct((s, SB_W), F32)
    qspec = pl.BlockSpec((tq, LANES), lambda p, i: (i, p))
    whole = pl.BlockSpec((None, s // tk, LANES, tk), lambda p, i: (p, 0, 0, 0))
    shp_t = jax.ShapeDtypeStruct((nkb, s // tk, LANES, tk), F32)
    hbm = pl.BlockSpec(memory_space=pl.ANY)
    return pl.pallas_call(
        body, name=name, grid=(nkb, nq),
        out_shape=[shp, shp_t, shp_t] + [jax.ShapeDtypeStruct(p.shape, p.dtype) for p in slabs],
        in_specs=[qspec,
                  pl.BlockSpec((s, LANES), lambda p, i: (0, nkb + p)),
                  pl.BlockSpec((s, LANES), lambda p, i: (0, 2 * nkb + p)),
                  qspec, qspec, pl.BlockSpec(memory_space=pltpu.SMEM)] + [hbm] * n_sl,
        out_specs=[qspec, whole, whole] + [hbm] * n_sl,
        scratch_shapes=[pltpu.VMEM((2, tq, LANES), F32), pltpu.VMEM((2, tq, LANES), F32), pltpu.VMEM((2, tq, LANES), F32)]
        + _exchange_sems(n_sl),
        compiler_params=_cparams())(proj, proj, proj, sp_total, dy, sweep_start, *slabs)


def _swa_backward(proj, y_sw, dy, sinks, gives, name):
    s = proj.shape[0]
    nb = s // WINDOW
    qb, kb, vb = 3 * SB_W // SWA_QW, (3 * SB_W + SWA_QW) // LANES, (3 * SB_W + SWA_QW + SWA_KW) // LANES

    n_gv = len(gives)

    def body(q_ref, kp_ref, kc_ref, vp_ref, vc_ref, o_ref, do_ref, sink_ref, *refs):
        give_refs, (dq_ref, dk_ref, dv_ref, ds_ref), got_refs = refs[:n_gv], refs[n_gv:n_gv + 4], refs[n_gv + 4:2 * n_gv + 4]
        n = pl.program_id(0)
        swap = _sibling_halves(give_refs, got_refs, *refs[2 * n_gv + 4:])

        @pl.when(n == 0)
        def _():
            for cp in swap:
                cp.start()

        @pl.when(n == 0)
        def _():
            dk_ref[...] = jnp.zeros_like(dk_ref)
            dv_ref[...] = jnp.zeros_like(dv_ref)
            ds_ref[...] = jnp.zeros_like(ds_ref)

        k = jnp.concatenate([kp_ref[...], kc_ref[...]], axis=0)
        v = jnp.concatenate([vp_ref[...], vc_ref[...]], axis=0)
        k_sw = pltpu.roll(k.astype(F32), HEAD_DIM, 1).astype(BF16)
        v_sw = pltpu.roll(v.astype(F32), HEAD_DIM, 1).astype(BF16)
        lane = lax.broadcasted_iota(jnp.int32, (1, LANES), 1)
        halves = [lane < HEAD_DIM, lane >= HEAD_DIM]
        valid, distf = _swa_masks(n)
        heads = range(2 * 4)
        cols = [slice((h // 2) * LANES, (h // 2 + 1) * LANES) for h in heads]
        qms = [jnp.where(halves[h % 2], q_ref[:, cols[h]], jnp.zeros((), BF16)) for h in heads]
        dos = [jnp.where(halves[h % 2], do_ref[:, cols[h]], 0.0) for h in heads]
        dobs = [d.astype(BF16) for d in dos]
        native = [h // 4 == h % 2 for h in heads]
        kus = [k if native[h] else k_sw for h in heads]
        vus = [v if native[h] else v_sw for h in heads]
        scores = [_dot_nt(qms[h], kus[h]) for h in heads]
        dps = [_dot_nt(dobs[h], vus[h]) for h in heads]
        deltas = [jnp.sum(dos[h] * o_ref[:, cols[h]], axis=1, keepdims=True) for h in heads]
        probs = [_swa_probs(scores[h], valid, distf, h, sink_ref[h]) for h in heads]
        pbs = [probs[h][0].astype(BF16) for h in heads]
        dscs = [(probs[h][0] * (dps[h] - deltas[h])).astype(BF16) for h in heads]
        dqs = [_dot(dscs[h], kus[h]) for h in heads]
        dks = [_dot_tn(dscs[h], qms[h]) for h in heads]
        dvs = [_dot_tn(pbs[h], dobs[h]) for h in heads]
        for h in heads:
            ds_ref[h:h + 1, :] += jnp.zeros((1, LANES), F32) - jnp.sum(probs[h][1] * deltas[h])
        for pair in range(4):
            dq_ref[:, cols[2 * pair]] = jnp.where(halves[0], dqs[2 * pair], dqs[2 * pair + 1])

        def gathered(parts):
            nat = sum(parts[h] for h in heads if native[h])
            rot = sum(parts[h] for h in heads if not native[h])
            return nat + pltpu.roll(rot, HEAD_DIM, 1)

        dk, dv = gathered(dks), gathered(dvs)
        prev = pl.multiple_of(jnp.maximum(n - 1, 0) * WINDOW, WINDOW)
        cur = pl.multiple_of(n * WINDOW, WINDOW)
        dk_ref[pl.ds(prev, WINDOW), :] += dk[:WINDOW]
        dv_ref[pl.ds(prev, WINDOW), :] += dv[:WINDOW]
        dk_ref[pl.ds(cur, WINDOW), :] += dk[WINDOW:]
        dv_ref[pl.ds(cur, WINDOW), :] += dv[WINDOW:]

        @pl.when(n == nb - 1)
        def _():
            for cp in swap:
                cp.wait()

    prev_blk = lambda n: jnp.maximum(n - 1, 0)
    wide = pl.BlockSpec((WINDOW, SWA_QW), lambda n: (n, 0))
    whole = pl.BlockSpec((s, LANES), lambda n: (0, 0))
    hbm = pl.BlockSpec(memory_space=pl.ANY)
    return pl.pallas_call(
        body, name=name, grid=(nb,),
        out_shape=[jax.ShapeDtypeStruct((s, SWA_QW), F32), jax.ShapeDtypeStruct((s, LANES), F32),
                   jax.ShapeDtypeStruct((s, LANES), F32), jax.ShapeDtypeStruct((8, LANES), F32)] + _halves_shapes(gives),
        in_specs=[pl.BlockSpec((WINDOW, SWA_QW), lambda n: (n, qb)),
                  pl.BlockSpec((WINDOW, LANES), lambda n: (prev_blk(n), kb)),
                  pl.BlockSpec((WINDOW, LANES), lambda n: (n, kb)),
                  pl.BlockSpec((WINDOW, LANES), lambda n: (prev_blk(n), vb)),
                  pl.BlockSpec((WINDOW, LANES), lambda n: (n, vb)),
                  wide,
                  pl.BlockSpec((WINDOW, SWA_QW), lambda n: (n, 1)),
                  pl.BlockSpec(memory_space=pltpu.SMEM)] + [hbm] * n_gv,
        out_specs=[wide, whole, whole, pl.BlockSpec((8, LANES), lambda n: (0, 0))] + [hbm] * n_gv,
        scratch_shapes=_halves_sems(n_gv),
        compiler_params=_cparams())(proj, proj, proj, proj, proj, y_sw, dy, sinks, *gives)


def _in_proj_backward(dq_sb, dkt_sb, dvt_sb, dq_sw, dk_sw, dv_sw, du1, x, h_t, vec, w_in, name):
    s = x.shape[0]
    tb = min(TOK_TILE, s)
    n_pairs, _, _, tk = dkt_sb.shape

    def body(dqsb_ref, dktsb_ref, dvtsb_ref, dqsw_ref, dksw_ref, dvsw_ref, du1_ref, x_ref, ht_ref, vec_ref, w_ref,
             dw_ref, gx_ref, acc_ref, bacc_ref, dproj_ref):
        @pl.when(pl.program_id(0) == 0)
        def _():
            dw_ref[...] = jnp.zeros_like(dw_ref)
            acc_ref[...] = jnp.zeros_like(acc_ref)
            bacc_ref[...] = jnp.zeros_like(bacc_ref)

        pieces = ((0, dqsb_ref, QK_SCALE), (3 * SB_W, dqsw_ref, QK_SCALE), (3 * SB_W + SWA_QW, dksw_ref, 1.0),
                  (3 * SB_W + SWA_QW + SWA_KW, dvsw_ref, 1.0))
        for lo, ref, scale in pieces:
            width = ref.shape[1]
            piece = ref[...] * scale
            bacc_ref[0:1, lo:lo + width] += _colsum(piece)
            dproj_ref[:, lo:lo + width] = piece.astype(BF16)
        for base, ref in ((SB_W, dktsb_ref), (2 * SB_W, dvtsb_ref)):
            for p in range(n_pairs):
                lo = base + p * LANES
                for jj in range(tb // tk):
                    piece = ref[p, jj].T
                    bacc_ref[0:1, lo:lo + LANES] += _colsum(piece)
                    dproj_ref[jj * tk:(jj + 1) * tk, lo:lo + LANES] = piece.astype(BF16)
        dproj = dproj_ref[...]
        dw_ref[...] += _dot(ht_ref[...], dproj)
        dh = _dot_nt(dproj, w_ref[...])
        xv = x_ref[...]
        gx_ref[...] = ALPHA * du1_ref[...] + dh * (1.0 + vec_ref[V_SC_A:V_SC_A + 1, :])
        acc_ref[C_SCA:C_SCA + 1, :] += _colsum(dh * xv)
        acc_ref[C_SHA:C_SHA + 1, :] += _colsum(dh)

    half = pl.BlockSpec((tb, SB_W), lambda i: (i, 0))
    narrow = pl.BlockSpec((tb, LANES), lambda i: (i, 0))
    full = pl.BlockSpec((tb, D), lambda i: (i, 0))
    blocks_t = pl.BlockSpec((n_pairs, tb // tk, LANES, tk), lambda i: (0, i, 0, 0))
    return pl.pallas_call(
        body, name=name, grid=(s // tb,),
        out_shape=[jax.ShapeDtypeStruct((D, D_IN), F32), jax.ShapeDtypeStruct((s, D), F32),
                   jax.ShapeDtypeStruct((8, D), F32), jax.ShapeDtypeStruct((8, D_IN), F32)],
        in_specs=[half, blocks_t, blocks_t, half, narrow, narrow, full, full, pl.BlockSpec((D, tb), lambda i: (0, i)),
                  _resident((VEC_ROWS, D)), _resident((D, D_IN))],
        out_specs=[pl.BlockSpec((D, D_IN), lambda i: (0, 0)), full, pl.BlockSpec((8, D), lambda i: (0, 0)),
                   pl.BlockSpec((8, D_IN), lambda i: (0, 0))],
        scratch_shapes=[pltpu.VMEM((tb, D_IN), BF16)],
        compiler_params=_cparams())(dq_sb, dkt_sb, dvt_sb, dq_sw, dk_sw, dv_sw, du1, x, h_t, vec, w_in)


def _weight_grad(at, b, name, col_shards=1):
    m, s = at.shape
    n = b.shape[1]
    if col_shards > 1:
        tn = n // col_shards
        out_shape = jax.ShapeDtypeStruct((col_shards, m, tn), F32)
        out_spec = pl.BlockSpec((None, m, tn), lambda j, k: (j, 0, 0))
    else:
        tn = 512 if n % 512 == 0 else n
        out_shape = jax.ShapeDtypeStruct((m, n), F32)
        out_spec = pl.BlockSpec((m, tn), lambda j, k: (0, j))
    ts = min(WGRAD_TOKENS, s)
    while 2 * (m * ts * 2 + ts * tn * 2 + m * tn * 4) > WGRAD_VMEM and ts > 512:
        ts //= 2

    def body(at_ref, b_ref, o_ref):
        @pl.when(pl.program_id(1) == 0)
        def _():
            o_ref[...] = jnp.zeros_like(o_ref)

        o_ref[...] += _dot(at_ref[...], b_ref[...])

    return pl.pallas_call(
        body, name=name, grid=(n // tn, s // ts),
        out_shape=out_shape,
        in_specs=[pl.BlockSpec((m, ts), lambda j, k: (0, k)), pl.BlockSpec((ts, tn), lambda j, k: (k, j))],
        out_specs=out_spec,
        compiler_params=_cparams())(at, b)


def _pad_rows(v, rows):
    return jnp.concatenate([v, jnp.zeros((rows - v.shape[0], v.shape[1]), v.dtype)], axis=0)


def _col_shards(w, n_shards):
    r, n = w.shape
    return w.reshape(r, n_shards, n // n_shards).transpose(1, 0, 2)


def kernel(x, c, w_ada, b_ada, w_in, b_in, sinks, gn_sb, gn_swa, w_out, ln1_g, ln1_b, w_gu, w_down, ln2_g, ln2_b, loss_target, m_w_ada, m_b_ada, m_w_in, m_b_in, m_sinks, m_gn_sb, m_gn_swa, m_w_out, m_ln1_g, m_ln1_b, m_w_gu, m_w_down, m_ln2_g, m_ln2_b, v_w_ada, v_b_ada, v_w_in, v_b_in, v_sinks, v_gn_sb, v_gn_swa, v_w_out, v_ln1_g, v_ln1_b, v_w_gu, v_w_down, v_ln2_g, v_ln2_b):
    ix, iy, ic = lax.axis_index("x"), lax.axis_index("y"), lax.axis_index("c")
    chip = 2 * ix + iy
    dev = 4 * ix + 2 * iy + ic
    xs, target = x[0], loss_target[0]
    s = xs.shape[0]

    c_rows, g_in = _allgather8(_pad_rows(c, 8), "gather_c", gather=[w_in[0].astype(BF16)])
    c_all = c_rows[::8]
    n_ada = w_ada.shape[2]
    b_ada_shard = lax.dynamic_slice_in_dim(b_ada, chip * n_ada, n_ada, axis=1)
    mod_cols, silu_c = _mod_shard(c_all, w_ada[0], b_ada_shard, "mod_shard")
    mod_all = _allgather8(mod_cols, "gather_mod")[0].reshape(N_DEV, 8, n_ada)
    mod_mine = lax.dynamic_index_in_dim(mod_all, dev, axis=1, keepdims=False)
    mod = mod_mine.reshape(N_CHIPS, 2, n_ada)[:, 0].reshape(6, D)
    vec = jnp.concatenate([mod, ln1_g, ln1_b, ln2_g, ln2_b, jnp.concatenate([gn_sb, gn_swa], axis=1),
                           jnp.zeros((VEC_ROWS - 11, D), F32)], axis=0)

    w_in_b = g_in.transpose(1, 0, 2).reshape(D, D_IN)

    h_t, proj = _in_proj(xs, vec, w_in_b, b_in, "in_proj")
    y_sb, sp_total, sweep_start, g_out, g_gu = _sb_forward(
        proj, [w_out[0].astype(BF16), w_gu[0].astype(BF16)], "sb_forward")
    w_gu_b = g_gu.transpose(1, 0, 2).reshape(D, 2 * D_FF)
    w_out_b = g_out.reshape(D, D)
    sink_vec = sinks[0]
    y_sw, g_down = _swa_forward(proj, sink_vec, [w_down[0].astype(BF16)], "swa_forward")
    w_down_b = g_down.reshape(D_FF, D)
    mixed_t, attn, x1, h2_b, h2_t = _post_attention(y_sb, y_sw, xs, vec, w_out_b, "post_attention")
    gu, act_t, ffn = _ffn_forward(h2_b, w_gu_b, w_down_b, "ffn_forward")

    def in_halves(shards):
        n_sh, rows, cols = shards.shape
        return shards.reshape(n_sh, 2, rows // 2, cols)

    core = ic.reshape(1).astype(jnp.int32)
    dffn_b, dgu_b, dx1, acc_f = _ffn_backward(x1, ffn, target, gu, vec, w_gu_b, w_down_b, "ffn_backward")
    dw_gu = _weight_grad(h2_t, dgu_b, "grad_w_gu", col_shards=4)
    dw_down = _weight_grad(act_t, dffn_b, "grad_w_down")
    du1, dattn_b, dy, acc_a = _attn_out_backward(dx1, xs, attn, y_sb, y_sw, vec, w_out_b, "attn_out_backward")
    dw_out = _weight_grad(mixed_t, dattn_b, "grad_w_out")
    first = [in_halves(dw_gu), in_halves(dw_down.reshape(4, D_FF // 4, D)), in_halves(dw_out.reshape(4, D // 4, D))]
    dq_sw, dk_sw, dv_sw, dsink, *got_first = _swa_backward(proj, y_sw, dy, sink_vec, first, "swa_backward")
    sums_first = _chip_sums(first, got_first, core, "grad_chip_sums")
    dq_sb, dk_sb, dv_sb, *parts_first = _sb_backward(proj, sp_total, sweep_start, dy, sums_first, "sb_backward")
    dw_in, grad_x, acc_i, acc_b = _in_proj_backward(dq_sb, dk_sb, dv_sb, dq_sw, dk_sw, dv_sw, du1, xs, h_t, vec, w_in_b,
                                                    "in_proj_backward")
    last = [in_halves(_col_shards(dw_in, 4))]
    sums_last = _chip_sums(last, _halves_swap(last, "grad_halves_swap_in"), core, "grad_chip_sum_in", out_dtype=BF16)

    dmod = jnp.concatenate([acc_i[C_SHA:C_SHA + 1], acc_i[C_SCA:C_SCA + 1], acc_a[B_GA:B_GA + 1],
                            acc_f[A_SHF:A_SHF + 1], acc_f[A_SCF:A_SCF + 1], acc_f[A_GF:A_GF + 1]], axis=1)
    dsink_row = jnp.concatenate([dsink[:, 0].reshape(1, 8), jnp.zeros((1, LANES - 8), F32)], axis=1)
    loss_row = jnp.concatenate([jnp.sum(acc_f[A_LOSS:A_LOSS + 1], axis=1, keepdims=True),
                                jnp.zeros((1, LANES - 1), F32)], axis=1)
    small = jnp.concatenate([dmod, acc_b[0:1], acc_a[B_LN1G:B_LN1G + 1], acc_a[B_LN1B:B_LN1B + 1],
                             acc_f[A_LN2G:A_LN2G + 1], acc_f[A_LN2B:A_LN2B + 1], acc_a[B_GN:B_GN + 1],
                             dsink_row, loss_row, jnp.zeros((1, SM_PAD - SM_LEN), F32)], axis=1)
    small_rows, *parts_last = _allgather8(small.reshape(8, SM_PAD // 8), "gather_small", scatter=sums_last)
    small_all = small_rows.reshape(N_DEV, SM_PAD)

    mine = _sum4s([*parts_first, *parts_last], "grad_reduce")
    theirs = _sibling_send(mine, "grad_half_return")
    gw_gu, gw_down, gw_out, gw_in = [
        jnp.concatenate([jnp.where(ic == 0, m_, t_), jnp.where(ic == 0, t_, m_)], axis=0) for m_, t_ in zip(mine, theirs)]

    small_names = ["b_ada", "b_in", "ln1_g", "ln1_b", "ln2_g", "ln2_b", "gn_sb", "gn_swa", "sinks"]
    small_at = [SM_MOD, SM_BIN, SM_LN1G, SM_LN1B, SM_LN2G, SM_LN2B, SM_GN, SM_GN + SB_W, SM_SINK]
    *small_out, loss_row_all = _small_update(
        small_all, small_at,
        [b_ada, b_in, ln1_g, ln1_b, ln2_g, ln2_b, gn_sb, gn_swa, sinks],
        [m_b_ada, m_b_in, m_ln1_g, m_ln1_b, m_ln2_g, m_ln2_b, m_gn_sb, m_gn_swa, m_sinks],
        [v_b_ada, v_b_in, v_ln1_g, v_ln1_b, v_ln2_g, v_ln2_b, v_gn_sb, v_gn_swa, v_sinks], SM_LOSS, "small_update")
    g_small, d_small, m2_small, v2_small = [dict(zip(small_names, leaves)) for leaves in small_out]
    loss = loss_row_all[0, 0]

    dmod_cols = lax.dynamic_slice_in_dim(small_all[:, SM_MOD:SM_BIN], chip * n_ada, n_ada, axis=1)
    gw_ada = _weight_grad(_pad_rows(silu_c, LANES).astype(BF16).T, _pad_rows(dmod_cols, LANES).astype(BF16), "grad_w_ada")

    big = {}
    for nm, w, g, m, v in (("w_ada", w_ada, gw_ada, m_w_ada, v_w_ada), ("w_in", w_in, gw_in, m_w_in, v_w_in),
                           ("w_out", w_out, gw_out, m_w_out, v_w_out), ("w_gu", w_gu, gw_gu, m_w_gu, v_w_gu),
                           ("w_down", w_down, gw_down, m_w_down, v_w_down)):
        d_, m2_, v2_ = _adamw(w[0], g, m[0], v[0], "adamw_" + nm)
        big[nm] = (g[None], d_[None], m2_[None], v2_[None])

    order = ["w_ada", "b_ada", "w_in", "b_in", "sinks", "gn_sb", "gn_swa", "w_out", "ln1_g", "ln1_b", "w_gu", "w_down",
             "ln2_g", "ln2_b"]

    def leaf(nm, which):
        if nm in big:
            return big[nm][which]
        return (g_small, d_small, m2_small, v2_small)[which][nm]

    outs = [loss, grad_x[None]]
    for which in range(4):
        outs += [leaf(nm, which) for nm in order]
    return tuple(outs)
```

```python
import math

import jax
import jax.numpy as jnp
from jax import lax
from jax.experimental import pallas as pl
from jax.experimental.pallas import tpu as pltpu

F32 = jnp.float32
BF16 = jnp.bfloat16

D = 1024
HEAD_DIM = 64
SB_W = 512
SWA_QW = 512
SWA_KW = 128
D_IN = 2304
D_FF = 2816
WINDOW = 128
ALPHA = 2.0 ** 0.25
LN_EPS = 1e-5
RMS_EPS = 1e-6
MASK_VALUE = -1e30
QK_SCALE = 1.0 / math.sqrt(HEAD_DIM)

ADAM_LR = 0.001
ADAM_B1 = 0.9
ADAM_B2 = 0.999
ADAM_EPS = 1e-08
ADAM_WD = 0.01
ADAM_STEP = 10

N_CHIPS = 4
N_DEV = 8
LANES = 128

SB_TQ = 512
SB_TK = 256
SB_DEAD_MASS = 110.0
TOK_TILE = 512
FFN_TILE = 256
FFN_BWD_TILE = 256
VMEM_LIMIT = 56 * 1024 * 1024
WGRAD_TOKENS = 2048
WGRAD_VMEM = 40 * 1024 * 1024

V_SH_A, V_SC_A, V_G_A, V_SH_F, V_SC_F, V_G_F, V_LN1G, V_LN1B, V_LN2G, V_LN2B, V_GN = range(11)
VEC_ROWS = 16

SM_MOD = 0
SM_BIN = 6 * D
SM_LN1G = SM_BIN + D_IN
SM_LN1B = SM_LN1G + D
SM_LN2G = SM_LN1B + D
SM_LN2B = SM_LN2G + D
SM_GN = SM_LN2B + D
SM_SINK = SM_GN + D
SM_LOSS = SM_SINK + LANES
SM_LEN = SM_LOSS + LANES
SM_PAD = -(-SM_LEN // (8 * LANES)) * (8 * LANES)

MESH = pl.DeviceIdType.MESH


def _cparams(**kw):
    return pltpu.CompilerParams(vmem_limit_bytes=VMEM_LIMIT, **kw)


def _resident(shape):
    nd = len(shape)
    return pl.BlockSpec(shape, lambda *_: (0,) * nd, pipeline_mode=pl.Buffered(1))


def _dot(a, b):
    return jnp.dot(a, b, preferred_element_type=F32)


def _dot_nt(a, b):
    return lax.dot_general(a, b, (((1,), (1,)), ((), ())), preferred_element_type=F32)


def _dot_tn(a, b):
    return lax.dot_general(a, b, (((0,), (0,)), ((), ())), preferred_element_type=F32)


def _sum_matrix(tk, keep):
    row = lax.broadcasted_iota(jnp.int32, (tk, tk + LANES), 0)
    col = lax.broadcasted_iota(jnp.int32, (tk, tk + LANES), 1)
    return (keep(row, col) | (col >= tk)).astype(BF16)


def _block_sums(x, m):
    tk = x.shape[1]
    res = _dot(x.astype(BF16), m)
    return res[:, :tk], res[:, tk:]


def _before(t0, n, s0, tk):
    return s0 + lax.broadcasted_iota(jnp.int32, (n, tk), 1) < t0 + lax.broadcasted_iota(jnp.int32, (n, tk), 0)


class _RowState:
    def __init__(self, ref, tk, add=False):
        self.ref, self.tk, self.add, self.vals = ref, tk, add, {}

    def _blocks(self, r0, n):
        return range(r0 // self.tk, (r0 + n) // self.tk)

    def get(self, hd, r0, n):
        for d in self._blocks(r0, n):
            if (hd, d) not in self.vals:
                self.vals[(hd, d)] = self.ref[hd, d * self.tk:(d + 1) * self.tk, :]
        parts = [self.vals[(hd, d)] for d in self._blocks(r0, n)]
        return parts[0] if len(parts) == 1 else jnp.concatenate(parts, axis=0)

    def put(self, hd, r0, n, val):
        for k, d in enumerate(self._blocks(r0, n)):
            self.vals[(hd, d)] = val[k * self.tk:(k + 1) * self.tk]

    def accumulate(self, hd, r0, n, val):
        for k, d in enumerate(self._blocks(r0, n)):
            part = val[k * self.tk:(k + 1) * self.tk]
            self.vals[(hd, d)] = part if (hd, d) not in self.vals else self.vals[(hd, d)] + part

    def store(self):
        for (hd, d), val in self.vals.items():
            span = slice(d * self.tk, (d + 1) * self.tk)
            if self.add:
                self.ref[hd, span, :] += val
            else:
                self.ref[hd, span, :] = val


def _across(v, tk):
    return jnp.concatenate([v] * (tk // LANES), axis=1)


def _allgather8(v, name, gather=(), scatter=()):
    m_per, n = v.shape
    n_g, n_s = len(gather), len(scatter)

    def body(x_ref, *refs):
        g_in, s_in = refs[:n_g], refs[n_g:n_g + n_s]
        out_ref = refs[n_g + n_s]
        g_out, s_out = refs[n_g + n_s + 1:2 * n_g + n_s + 1], refs[2 * n_g + n_s + 1:2 * (n_g + n_s) + 1]
        send_sems, recv_sems, local_sem, *more_sems = refs[2 * (n_g + n_s) + 1:]
        halves = _HalvesGather(g_in, g_out, *more_sems[:5]) if n_g else None
        beside = ([halves] if n_g else []) + ([_scatter_exchange(s_in, s_out, *more_sems[-3:])] if n_s else [])
        for ex in beside:
            ex.start()
        x, y, c = lax.axis_index("x"), lax.axis_index("y"), lax.axis_index("c")
        me, sibling = (x, y, c), (x, y, 1 - c)
        chips = [(1 - x, y), (x, 1 - y), (1 - x, 1 - y)]

        def rows(px, py, pc):
            return out_ref.at[pl.ds((4 * px + 2 * py + pc) * m_per, m_per), :]

        def copy(k, block, to, src=None):
            return pltpu.make_async_remote_copy(
                src_ref=rows(*block) if src is None else src, dst_ref=rows(*block),
                send_sem=send_sems.at[k], recv_sem=recv_sems.at[k], device_id=to, device_id_type=MESH)

        mine = pltpu.make_async_copy(x_ref, rows(*me), local_sem)
        mine.start()
        first = [copy(0, me, sibling, src=x_ref)]
        first += [copy(1 + j, me, (*chip, c), src=x_ref) for j, chip in enumerate(chips)]
        for cp in first:
            cp.start()
        passed = [copy(4 + j, (*chip, c), sibling) for j, chip in enumerate(chips)]
        for j, chip in enumerate(chips):
            copy(1 + j, (*chip, c), me).wait_recv()
            passed[j].start()
        copy(0, sibling, me).wait_recv()
        for j, chip in enumerate(chips):
            copy(4 + j, (*chip, 1 - c), me).wait_recv()
        for cp in first + passed:
            cp.wait_send()
        mine.wait()
        if halves is not None:
            halves.forward()
        for ex in beside:
            ex.wait()

    hbm = pl.BlockSpec(memory_space=pl.ANY)
    return pl.pallas_call(
        body, name=name,
        out_shape=[jax.ShapeDtypeStruct((N_DEV * m_per, n), v.dtype)]
        + [jax.ShapeDtypeStruct((N_CHIPS,) + a.shape, a.dtype) for a in gather]
        + [jax.ShapeDtypeStruct(p.shape, p.dtype) for p in scatter],
        in_specs=[pl.BlockSpec(memory_space=pltpu.VMEM)] + [hbm] * (n_g + n_s),
        out_specs=[pl.BlockSpec(memory_space=pltpu.VMEM)] + [hbm] * (n_g + n_s),
        scratch_shapes=[pltpu.SemaphoreType.DMA((7,)), pltpu.SemaphoreType.DMA((7,)), pltpu.SemaphoreType.DMA]
        + (_halves_gather_sems(n_g) if n_g else [])
        + (_exchange_sems(n_s) if n_s else []),
        compiler_params=_cparams(),
    )(v, *gather, *scatter)


class _Exchange:
    def __init__(self, local, sends, arrivals):
        self.local, self.sends, self.arrivals = local, sends, arrivals

    def start(self):
        for cp in self.local + self.sends:
            cp.start()

    def wait(self):
        for cp in self.arrivals:
            cp.wait_recv()
        for cp in self.sends:
            cp.wait_send()
        for cp in self.local:
            cp.wait()


def _exchange_sems(n):
    return [pltpu.SemaphoreType.DMA((3 * n,)), pltpu.SemaphoreType.DMA((3 * n,)), pltpu.SemaphoreType.DMA((n,))]


class _HalvesGather:
    def __init__(self, ins, outs, far_send, far_recv, near_send, near_recv, local_sems):
        x, y, c = lax.axis_index("x"), lax.axis_index("y"), lax.axis_index("c")
        slot = 2 * x + y
        chips = [(1 - x, y), (x, 1 - y), (1 - x, 1 - y)]
        self.local, self.far, self.landed, self.near, self.passed = [], [], [], [], []
        for a in range(len(ins)):
            h = ins[a].shape[0] // 2
            mine, theirs = pl.ds(c * h, h), pl.ds((1 - c) * h, h)
            self.local.append(pltpu.make_async_copy(ins[a], outs[a].at[slot], local_sems.at[a]))
            for j, (px, py) in enumerate(chips):
                k, there = 3 * a + j, 2 * px + py
                far = dict(send_sem=far_send.at[k], recv_sem=far_recv.at[k], device_id=(px, py, c), device_id_type=MESH)
                near = dict(send_sem=near_send.at[k], recv_sem=near_recv.at[k], device_id=(x, y, 1 - c),
                            device_id_type=MESH)
                self.far.append(pltpu.make_async_remote_copy(
                    src_ref=ins[a].at[mine], dst_ref=outs[a].at[slot, mine], **far))
                self.landed.append(pltpu.make_async_remote_copy(
                    src_ref=ins[a].at[mine], dst_ref=outs[a].at[there, mine], **far))
                self.near.append(pltpu.make_async_remote_copy(
                    src_ref=outs[a].at[there, mine], dst_ref=outs[a].at[there, mine], **near))
                self.passed.append(pltpu.make_async_remote_copy(
                    src_ref=outs[a].at[there, mine], dst_ref=outs[a].at[there, theirs], **near))

    def start(self):
        for cp in self.local + self.far:
            cp.start()

    def forward(self):
        for landed, near in zip(self.landed, self.near):
            landed.wait_recv()
            near.start()

    def wait(self):
        for cp in self.passed:
            cp.wait_recv()
        for cp in self.far + self.near:
            cp.wait_send()
        for cp in self.local:
            cp.wait()


def _halves_gather_sems(n):
    return [pltpu.SemaphoreType.DMA((3 * n,))] * 4 + [pltpu.SemaphoreType.DMA((n,))]


def _scatter_exchange(p_refs, out_refs, send_sems, recv_sems, local_sems):
    x, y, c = lax.axis_index("x"), lax.axis_index("y"), lax.axis_index("c")
    slot = 2 * x + y
    chips = [(1 - x, y), (x, 1 - y), (1 - x, 1 - y)]
    local, sends, arrivals = [], [], []
    for a, (p_ref, out_ref) in enumerate(zip(p_refs, out_refs)):
        local.append(pltpu.make_async_copy(p_ref.at[slot], out_ref.at[slot], local_sems.at[a]))
        for j, (px, py) in enumerate(chips):
            sems = dict(send_sem=send_sems.at[3 * a + j], recv_sem=recv_sems.at[3 * a + j],
                        device_id=(px, py, c), device_id_type=MESH)
            sends.append(pltpu.make_async_remote_copy(src_ref=p_ref.at[2 * px + py], dst_ref=out_ref.at[slot], **sems))
            arrivals.append(pltpu.make_async_remote_copy(src_ref=p_ref.at[slot], dst_ref=out_ref.at[2 * px + py], **sems))
    return _Exchange(local, sends, arrivals)


def _sibling_halves(give_refs, got_refs, send_sems, recv_sems):
    x, y, c = lax.axis_index("x"), lax.axis_index("y"), lax.axis_index("c")
    copies = []
    for a, (give_ref, got_ref) in enumerate(zip(give_refs, got_refs)):
        for s in range(N_CHIPS):
            copies.append(pltpu.make_async_remote_copy(
                src_ref=give_ref.at[s, 1 - c], dst_ref=got_ref.at[s], send_sem=send_sems.at[N_CHIPS * a + s],
                recv_sem=recv_sems.at[N_CHIPS * a + s], device_id=(x, y, 1 - c), device_id_type=MESH))
    return copies


def _halves_shapes(arrs):
    return [jax.ShapeDtypeStruct((a.shape[0],) + a.shape[2:], a.dtype) for a in arrs]


def _halves_sems(n):
    return [pltpu.SemaphoreType.DMA((N_CHIPS * n,)), pltpu.SemaphoreType.DMA((N_CHIPS * n,))]


def _halves_swap(arrs, name):
    n = len(arrs)

    def body(*refs):
        copies = _sibling_halves(refs[:n], refs[n:2 * n], *refs[2 * n:])
        for cp in copies:
            cp.start()
        for cp in copies:
            cp.wait()

    hbm = pl.BlockSpec(memory_space=pl.ANY)
    return pl.pallas_call(body, name=name, out_shape=_halves_shapes(arrs), in_specs=[hbm] * n, out_specs=[hbm] * n,
                          scratch_shapes=_halves_sems(n), compiler_params=_cparams())(*arrs)


def _sibling_send(arrs, name):
    n = len(arrs)

    def body(*refs):
        x, y, c = lax.axis_index("x"), lax.axis_index("y"), lax.axis_index("c")
        send_sems, recv_sems = refs[2 * n:]
        copies = [pltpu.make_async_remote_copy(src_ref=refs[a], dst_ref=refs[n + a], send_sem=send_sems.at[a],
                                               recv_sem=recv_sems.at[a], device_id=(x, y, 1 - c), device_id_type=MESH)
                  for a in range(n)]
        for cp in copies:
            cp.start()
        for cp in copies:
            cp.wait()

    hbm = pl.BlockSpec(memory_space=pl.ANY)
    return pl.pallas_call(
        body, name=name, out_shape=[jax.ShapeDtypeStruct(a.shape, a.dtype) for a in arrs],
        in_specs=[hbm] * n, out_specs=[hbm] * n,
        scratch_shapes=[pltpu.SemaphoreType.DMA((n,)), pltpu.SemaphoreType.DMA((n,))],
        compiler_params=_cparams(),
    )(*arrs)


def _row_tile(h):
    return h // 2 if (h // 2) % 8 == 0 else h


def _row_tiles(hs):
    tiles = [_row_tile(h) for h in hs]
    assert len({h // t for h, t in zip(hs, tiles)}) == 1
    return tiles, hs[0] // tiles[0]


def _chip_sums(arrs, gots, core, name, out_dtype=F32):
    n = len(arrs)
    tiles, steps = _row_tiles([a.shape[2] for a in arrs])

    def body(core_ref, *refs):
        for a_ref, b_ref, o_ref in zip(refs[:n], refs[n:2 * n], refs[2 * n:]):
            o_ref[...] = (a_ref[...] + b_ref[...]).astype(out_dtype)

    slabs = [pl.BlockSpec((None, tr, a.shape[3]), lambda s, i, core_ref: (s, i, 0)) for a, tr in zip(arrs, tiles)]
    grid_spec = pltpu.PrefetchScalarGridSpec(
        num_scalar_prefetch=1, grid=(N_CHIPS, steps),
        in_specs=[pl.BlockSpec((None, None, tr, a.shape[3]), lambda s, i, core_ref: (s, core_ref[0], i, 0))
                  for a, tr in zip(arrs, tiles)] + slabs,
        out_specs=slabs)
    return pl.pallas_call(body, name=name, grid_spec=grid_spec,
                          out_shape=[jax.ShapeDtypeStruct(g.shape, out_dtype) for g in gots],
                          compiler_params=_cparams())(core, *arrs, *gots)


def _sum4s(ps, name):
    tiles, steps = _row_tiles([p.shape[1] for p in ps])

    def body(*refs):
        for p_ref, o_ref in zip(refs[:len(ps)], refs[len(ps):]):
            o_ref[...] = ((p_ref[0].astype(F32) + p_ref[1].astype(F32)) + p_ref[2].astype(F32)) + p_ref[3].astype(F32)

    return pl.pallas_call(
        body, name=name, grid=(steps,), out_shape=[jax.ShapeDtypeStruct(p.shape[1:], F32) for p in ps],
        in_specs=[pl.BlockSpec((4, tr, p.shape[2]), lambda i: (0, i, 0)) for p, tr in zip(ps, tiles)],
        out_specs=[pl.BlockSpec((tr, p.shape[2]), lambda i: (i, 0)) for p, tr in zip(ps, tiles)],
        compiler_params=_cparams())(*ps)


def _adam_math(w, g, m, v):
    m2 = ADAM_B1 * m + (1.0 - ADAM_B1) * g
    v2 = ADAM_B2 * v + (1.0 - ADAM_B2) * (g * g)
    m_hat = m2 / (1.0 - ADAM_B1 ** ADAM_STEP)
    v_hat = v2 / (1.0 - ADAM_B2 ** ADAM_STEP)
    delta = -ADAM_LR * (m_hat / (jnp.sqrt(v_hat) + ADAM_EPS) + ADAM_WD * w)
    return delta, m2, v2


def _adamw(w, g, m, v, name):
    rows, cols = w.shape
    tr = rows // 4 if rows % 32 == 0 else rows

    def body(w_ref, g_ref, m_ref, v_ref, d_ref, m2_ref, v2_ref):
        delta, m2, v2 = _adam_math(w_ref[...], g_ref[...], m_ref[...], v_ref[...])
        d_ref[...] = delta
        m2_ref[...] = m2
        v2_ref[...] = v2

    spec = pl.BlockSpec((tr, cols), lambda i: (i, 0))
    shp = jax.ShapeDtypeStruct(w.shape, F32)
    return pl.pallas_call(body, name=name, grid=(rows // tr,), out_shape=[shp, shp, shp],
                          in_specs=[spec] * 4, out_specs=[spec] * 3, compiler_params=_cparams())(w, g, m, v)


def _small_update(g8, offsets, ws, ms, vs, loss_at, name):
    k = len(ws)

    def summed(g8_ref, lo, width):
        g = g8_ref[0:1, lo:lo + width]
        for r in range(1, N_DEV):
            g = g + g8_ref[r:r + 1, lo:lo + width]
        return g

    def body(g8_ref, *refs):
        ins, outs = refs[:3 * k], refs[3 * k:]
        for j in range(k):
            g = summed(g8_ref, offsets[j], ws[j].shape[1])
            delta, m2, v2 = _adam_math(ins[j][...], g, ins[k + j][...], ins[2 * k + j][...])
            for kind, val in enumerate((g, delta, m2, v2)):
                outs[kind * k + j][...] = val
        outs[4 * k][...] = summed(g8_ref, loss_at, LANES)

    vm = pl.BlockSpec(memory_space=pltpu.VMEM)
    shapes = [jax.ShapeDtypeStruct(w.shape, F32) for w in ws] * 4 + [jax.ShapeDtypeStruct((1, LANES), F32)]
    res = pl.pallas_call(body, name=name, out_shape=shapes, in_specs=[vm] * (1 + 3 * k), out_specs=[vm] * (4 * k + 1),
                         compiler_params=_cparams())(g8, *ws, *ms, *vs)
    return res[:k], res[k:2 * k], res[2 * k:3 * k], res[3 * k:4 * k], res[4 * k]


def _mod_shard(c8, w_ada, b_ada_shard, name):
    n = w_ada.shape[1]
    tn = 512

    def body(c_ref, w_ref, b_ref, o_ref, s_ref):
        cv = c_ref[...]
        sc = cv * (1.0 / (1.0 + jnp.exp(-cv)))
        s_ref[...] = sc
        o_ref[...] = _dot(sc.astype(BF16), w_ref[...].astype(BF16)) + b_ref[...]

    return pl.pallas_call(
        body, name=name, grid=(n // tn,),
        out_shape=[jax.ShapeDtypeStruct((8, n), F32), jax.ShapeDtypeStruct((8, D), F32)],
        in_specs=[pl.BlockSpec((8, D), lambda j: (0, 0)), pl.BlockSpec((D, tn), lambda j: (0, j)),
                  pl.BlockSpec((1, tn), lambda j: (0, j))],
        out_specs=[pl.BlockSpec((8, tn), lambda j: (0, j)), pl.BlockSpec((8, D), lambda j: (0, 0))],
        compiler_params=_cparams())(c8, w_ada, b_ada_shard)


def _layer_norm_stats(u):
    mu = jnp.mean(u, axis=1, keepdims=True)
    d = u - mu
    var = jnp.mean(d * d, axis=1, keepdims=True)
    rstd = lax.rsqrt(var + LN_EPS)
    return d * rstd, rstd


def _in_proj(x, vec, w_in, b_in, name):
    s = x.shape[0]
    tb = min(TOK_TILE, s)

    def body(x_ref, vec_ref, w_ref, b_ref, ht_ref, p_ref):
        h = x_ref[...] * (1.0 + vec_ref[V_SC_A:V_SC_A + 1, :]) + vec_ref[V_SH_A:V_SH_A + 1, :]
        hb = h.astype(BF16)
        ht_ref[...] = h.T.astype(BF16)
        proj = _dot(hb, w_ref[...]) + b_ref[...]
        col = lax.broadcasted_iota(jnp.int32, (1, D_IN), 1)
        is_q = (col < SB_W) | ((col >= 3 * SB_W) & (col < 3 * SB_W + SWA_QW))
        p_ref[...] = (proj * jnp.where(is_q, QK_SCALE, 1.0)).astype(BF16)

    return pl.pallas_call(
        body, name=name, grid=(s // tb,),
        out_shape=[jax.ShapeDtypeStruct((D, s), BF16), jax.ShapeDtypeStruct((s, D_IN), BF16)],
        in_specs=[pl.BlockSpec((tb, D), lambda i: (i, 0)), _resident((VEC_ROWS, D)), _resident((D, D_IN)),
                  _resident((1, D_IN))],
        out_specs=[pl.BlockSpec((D, tb), lambda i: (0, i)), pl.BlockSpec((tb, D_IN), lambda i: (i, 0))],
        compiler_params=_cparams())(x, vec, w_in, b_in)


def _softplus_parts(z):
    e1 = jnp.exp(-jnp.abs(z))
    sp = jnp.maximum(z, 0.0) + jnp.log(1.0 + e1)
    return sp, e1


def _sb_forward(proj, shards, name):
    s = proj.shape[0]
    tq, tk = min(SB_TQ, s), min(SB_TK, s)
    r = tq // tk

    n_sh = len(shards)
    nkb = SB_W // LANES
    nq = s // tq

    def body(q_ref, k_ref, v_ref, *refs):
        sh_refs, (o_ref, tot_ref, start_ref), got_refs = refs[:n_sh], refs[n_sh:n_sh + 3], refs[n_sh + 3:2 * n_sh + 3]
        acc_refs, run_refs = refs[2 * n_sh + 3:2 * n_sh + 5]
        i = pl.program_id(1)
        step = pl.program_id(0) * nq + i
        gather = _HalvesGather(sh_refs, got_refs, *refs[2 * n_sh + 5:])

        @pl.when(step == 0)
        def _():
            gather.start()

        @pl.when(step == nkb * nq // 2)
        def _():
            gather.forward()

        lane = lax.broadcasted_iota(jnp.int32, (1, LANES), 1)
        first = lane < HEAD_DIM
        qp = q_ref[...]
        zero = jnp.zeros((), BF16)
        qs = (jnp.where(first, qp, zero), jnp.where(first, zero, qp))
        later = _sum_matrix(tk, lambda row, col: row > col)
        acc_refs[...] = jnp.zeros_like(acc_refs)
        run_refs[...] = jnp.zeros_like(run_refs)

        def blocks(tiles):
            rows = [slice(r0, r0 + n) for r0, n, _, _ in tiles]
            kjs = [k_ref[pl.ds(pl.multiple_of(j * tk, tk), tk), :] for _, _, j, _ in tiles]
            vjs = [v_ref[pl.ds(pl.multiple_of(j * tk, tk), tk), :] for _, _, j, _ in tiles]
            chains = [(hd, t) for t in range(len(tiles)) for hd in range(2)]
            zs = [_dot_nt(qs[hd][rows[t]], kjs[t]) for hd, t in chains]
            sps = [_softplus_parts(z)[0] for z in zs]
            befores = [_before(i * tq + r0, n, j * tk, tk) if diag else None for r0, n, j, diag in tiles]
            spms = [sp if befores[t] is None else jnp.where(befores[t], sp, 0.0) for (hd, t), sp in zip(chains, sps)]
            cums = [_block_sums(spm, later) for spm in spms]
            runs, accs, ws = _RowState(run_refs, tk), _RowState(acc_refs, tk, add=True), []
            for (hd, t), z, sp, (cum, sm) in zip(chains, zs, sps, cums):
                r0, n = tiles[t][:2]
                run = runs.get(hd, r0, n)
                w = jnp.exp(z - sp - cum - _across(run, tk))
                if befores[t] is not None:
                    w = jnp.where(befores[t], w, 0.0)
                ws.append(w.astype(BF16))
                runs.put(hd, r0, n, run + sm)
            for (hd, t), pv in zip(chains, [_dot(w, vjs[t]) for (hd, t), w in zip(chains, ws)]):
                accs.accumulate(hd, *tiles[t][:2], pv)
            accs.store()
            runs.store()

        below = i * r
        diagonal = [(d * tk, tk, below + e, e == d) for d in range(r) for e in range(d, -1, -1)]

        @pl.when(i == 0)
        def _():
            blocks(diagonal)

        @pl.when(i > 0)
        def _():
            blocks(diagonal + [(0, tq, below - 1, False)])

        def swept_mass():
            return jnp.min(jnp.minimum(run_refs[0], run_refs[1]))

        def more(carry):
            n, mass = carry
            return (n < below) & (mass < SB_DEAD_MASS)

        def sweep(carry):
            n, _ = carry
            blocks([(0, tq, below - 1 - n, False)])
            return n + 1, swept_mass()

        n_swept, _ = lax.while_loop(more, sweep, (jnp.minimum(below, 1), swept_mass()))
        start_ref[pl.program_id(0), i] = (below - n_swept).astype(F32)
        o_ref[...] = jnp.where(first, acc_refs[0], acc_refs[1])
        tot_ref[...] = jnp.where(first, run_refs[0], run_refs[1])

        @pl.when(step == nkb * nq - 1)
        def _():
            gather.wait()

    shp = jax.ShapeDtypeStruct((s, SB_W), F32)
    qspec = pl.BlockSpec((tq, LANES), lambda p, i: (i, p))
    hbm = pl.BlockSpec(memory_space=pl.ANY)
    return pl.pallas_call(
        body, name=name, grid=(nkb, nq),
        out_shape=[shp, shp, jax.ShapeDtypeStruct((nkb, nq), F32)]
        + [jax.ShapeDtypeStruct((N_CHIPS,) + a.shape, a.dtype) for a in shards],
        in_specs=[qspec,
                  pl.BlockSpec((s, LANES), lambda p, i: (0, nkb + p)),
                  pl.BlockSpec((s, LANES), lambda p, i: (0, 2 * nkb + p))] + [hbm] * n_sh,
        out_specs=[qspec, qspec, pl.BlockSpec(memory_space=pltpu.SMEM)] + [hbm] * n_sh,
        scratch_shapes=[pltpu.VMEM((2, tq, LANES), F32), pltpu.VMEM((2, tq, LANES), F32)] + _halves_gather_sems(n_sh),
        compiler_params=_cparams())(proj, proj, proj, *shards)


def _swa_masks(n):
    ti = lax.broadcasted_iota(jnp.int32, (WINDOW, 2 * WINDOW), 0)
    kj = lax.broadcasted_iota(jnp.int32, (WINDOW, 2 * WINDOW), 1)
    dist = ti + WINDOW - kj
    valid = (dist >= 0) & (dist < WINDOW) & ((n * WINDOW - WINDOW + kj) >= 0)
    return valid, dist.astype(F32)


def _swa_probs(sc, valid, distf, h, sink):
    slope = 2.0 ** (-(h + 1))
    sc = jnp.where(valid, sc - slope * distf, MASK_VALUE)
    mx = jnp.maximum(jnp.max(sc, axis=1, keepdims=True), sink)
    p = jnp.exp(sc - mx)
    es = jnp.exp(sink - mx)
    inv = 1.0 / (jnp.sum(p, axis=1, keepdims=True) + es)
    return p * inv, es * inv


def _swa_forward(proj, sinks, shards, name):
    s = proj.shape[0]
    nb = s // WINDOW
    qb, kb, vb = 3 * SB_W // SWA_QW, (3 * SB_W + SWA_QW) // LANES, (3 * SB_W + SWA_QW + SWA_KW) // LANES
    n_sh = len(shards)

    def body(q_ref, kp_ref, kc_ref, vp_ref, vc_ref, sink_ref, *refs):
        sh_refs, o_ref, got_refs = refs[:n_sh], refs[n_sh], refs[n_sh + 1:2 * n_sh + 1]
        n = pl.program_id(0)
        gather = _HalvesGather(sh_refs, got_refs, *refs[2 * n_sh + 1:])

        @pl.when(n == 0)
        def _():
            gather.start()

        @pl.when(n == nb // 2)
        def _():
            gather.forward()

        k = jnp.concatenate([kp_ref[...], kc_ref[...]], axis=0)
        v = jnp.concatenate([vp_ref[...], vc_ref[...]], axis=0)
        k_sw = pltpu.roll(k.astype(F32), HEAD_DIM, 1).astype(BF16)
        v_sw = pltpu.roll(v.astype(F32), HEAD_DIM, 1).astype(BF16)
        lane = lax.broadcasted_iota(jnp.int32, (1, LANES), 1)
        halves = [lane < HEAD_DIM, lane >= HEAD_DIM]
        valid, distf = _swa_masks(n)
        heads = range(2 * 4)
        qms = [jnp.where(halves[h % 2], q_ref[:, (h // 2) * LANES:(h // 2 + 1) * LANES], jnp.zeros((), BF16))
               for h in heads]
        kus = [k if h // 4 == h % 2 else k_sw for h in heads]
        vus = [v if h // 4 == h % 2 else v_sw for h in heads]
        scores = [_dot_nt(qms[h], kus[h]) for h in heads]
        ps = [_swa_probs(scores[h], valid, distf, h, sink_ref[h])[0].astype(BF16) for h in heads]
        outs = [_dot(ps[h], vus[h]) for h in heads]
        for pair in range(4):
            o_ref[:, pair * LANES:(pair + 1) * LANES] = jnp.where(halves[0], outs[2 * pair], outs[2 * pair + 1])

        @pl.when(n == nb - 1)
        def _():
            gather.wait()

    prev = lambda n: jnp.maximum(n - 1, 0)
    hbm = pl.BlockSpec(memory_space=pl.ANY)
    return pl.pallas_call(
        body, name=name, grid=(nb,),
        out_shape=[jax.ShapeDtypeStruct((s, SWA_QW), F32)]
        + [jax.ShapeDtypeStruct((N_CHIPS,) + a.shape, a.dtype) for a in shards],
        in_specs=[pl.BlockSpec((WINDOW, SWA_QW), lambda n: (n, qb)),
                  pl.BlockSpec((WINDOW, LANES), lambda n: (prev(n), kb)),
                  pl.BlockSpec((WINDOW, LANES), lambda n: (n, kb)),
                  pl.BlockSpec((WINDOW, LANES), lambda n: (prev(n), vb)),
                  pl.BlockSpec((WINDOW, LANES), lambda n: (n, vb)),
                  pl.BlockSpec(memory_space=pltpu.SMEM)] + [hbm] * n_sh,
        out_specs=[pl.BlockSpec((WINDOW, SWA_QW), lambda n: (n, 0))] + [hbm] * n_sh,
        scratch_shapes=_halves_gather_sems(n_sh),
        compiler_params=_cparams())(proj, proj, proj, proj, proj, sinks, *shards)


def _rms_parts(y):
    return lax.rsqrt(jnp.mean(y * y, axis=1, keepdims=True) + RMS_EPS)


def _post_attention(y_sb, y_sw, x, vec, w_out, name):
    s = x.shape[0]
    tb = min(TOK_TILE, s)

    def body(ysb_ref, ysw_ref, x_ref, vec_ref, w_ref, mixedt_ref, attn_ref, x1_ref, h2_ref, h2t_ref):
        ysb, ysw = ysb_ref[...], ysw_ref[...]
        nsb_f = ysb * _rms_parts(ysb) * vec_ref[V_GN:V_GN + 1, :SB_W]
        nsw_f = ysw * _rms_parts(ysw) * vec_ref[V_GN:V_GN + 1, SB_W:]
        nsb, nsw = nsb_f.astype(BF16), nsw_f.astype(BF16)
        mixedt_ref[:SB_W, :] = nsb_f.T.astype(BF16)
        mixedt_ref[SB_W:, :] = nsw_f.T.astype(BF16)
        attn = _dot(nsb, w_ref[:SB_W, :]) + _dot(nsw, w_ref[SB_W:, :])
        attn_ref[...] = attn
        u1 = ALPHA * x_ref[...] + (1.0 + vec_ref[V_G_A:V_G_A + 1, :]) * attn
        xhat, _ = _layer_norm_stats(u1)
        x1 = xhat * vec_ref[V_LN1G:V_LN1G + 1, :] + vec_ref[V_LN1B:V_LN1B + 1, :]
        x1_ref[...] = x1
        h2 = x1 * (1.0 + vec_ref[V_SC_F:V_SC_F + 1, :]) + vec_ref[V_SH_F:V_SH_F + 1, :]
        h2_ref[...] = h2.astype(BF16)
        h2t_ref[...] = h2.T.astype(BF16)

    half = pl.BlockSpec((tb, SB_W), lambda i: (i, 0))
    full = pl.BlockSpec((tb, D), lambda i: (i, 0))
    full_t = pl.BlockSpec((D, tb), lambda i: (0, i))
    return pl.pallas_call(
        body, name=name, grid=(s // tb,),
        out_shape=[jax.ShapeDtypeStruct((D, s), BF16), jax.ShapeDtypeStruct((s, D), F32),
                   jax.ShapeDtypeStruct((s, D), F32), jax.ShapeDtypeStruct((s, D), BF16),
                   jax.ShapeDtypeStruct((D, s), BF16)],
        in_specs=[half, half, full, _resident((VEC_ROWS, D)), _resident((D, D))],
        out_specs=[full_t, full, full, full, full_t],
        compiler_params=_cparams())(y_sb, y_sw, x, vec, w_out)


def _ffn_forward(h2, w_gu, w_down, name):
    s = h2.shape[0]
    tb = min(FFN_TILE, s)

    def body(h_ref, wgu_ref, wd_ref, gu_ref, actt_ref, ffn_ref):
        gu = _dot(h_ref[...], wgu_ref[...])
        gu_ref[...] = gu.astype(BF16)
        gate, up = gu[:, :D_FF], gu[:, D_FF:]
        act = gate * (1.0 / (1.0 + jnp.exp(-gate))) * up
        actt_ref[...] = act.T.astype(BF16)
        ffn_ref[...] = _dot(act.astype(BF16), wd_ref[...])

    return pl.pallas_call(
        body, name=name, grid=(s // tb,),
        out_shape=[jax.ShapeDtypeStruct((s, 2 * D_FF), BF16), jax.ShapeDtypeStruct((D_FF, s), BF16),
                   jax.ShapeDtypeStruct((s, D), F32)],
        in_specs=[pl.BlockSpec((tb, D), lambda i: (i, 0)), _resident((D, 2 * D_FF)), _resident((D_FF, D))],
        out_specs=[pl.BlockSpec((tb, 2 * D_FF), lambda i: (i, 0)), pl.BlockSpec((D_FF, tb), lambda i: (0, i)),
                   pl.BlockSpec((tb, D), lambda i: (i, 0))],
        compiler_params=_cparams())(h2, w_gu, w_down)


def _layer_norm_bwd(dxhat, xhat, rstd):
    m1 = jnp.mean(dxhat, axis=1, keepdims=True)
    m2 = jnp.mean(dxhat * xhat, axis=1, keepdims=True)
    return rstd * (dxhat - m1 - xhat * m2)


def _colsum(a):
    return jnp.sum(a, axis=0, keepdims=True)


A_LN2G, A_LN2B, A_GF, A_SCF, A_SHF, A_LOSS = range(6)
B_LN1G, B_LN1B, B_GA, B_GN = range(4)
C_SCA, C_SHA = range(2)


def _ffn_backward(x1, ffn, target, gu, vec, w_gu, w_down, name):
    s = x1.shape[0]
    tb = min(FFN_BWD_TILE, s)

    def body(x1_ref, ffn_ref, t_ref, gu_ref, vec_ref, wgu_ref, wd_ref, dffn_ref, dgu_ref, dx1_ref, acc_ref):
        @pl.when(pl.program_id(0) == 0)
        def _():
            acc_ref[...] = jnp.zeros_like(acc_ref)

        x1v, ffn_v = x1_ref[...], ffn_ref[...]
        g_f = 1.0 + vec_ref[V_G_F:V_G_F + 1, :]
        u2 = ALPHA * x1v + g_f * ffn_v
        xhat, rstd = _layer_norm_stats(u2)
        ln_g = vec_ref[V_LN2G:V_LN2G + 1, :]
        err = xhat * ln_g + vec_ref[V_LN2B:V_LN2B + 1, :] - t_ref[...]
        dx2 = err * (1.0 / D)
        acc_ref[A_LOSS:A_LOSS + 1, :] += _colsum(err * err) * (0.5 / D)
        acc_ref[A_LN2G:A_LN2G + 1, :] += _colsum(dx2 * xhat)
        acc_ref[A_LN2B:A_LN2B + 1, :] += _colsum(dx2)
        du2 = _layer_norm_bwd(dx2 * ln_g, xhat, rstd)
        acc_ref[A_GF:A_GF + 1, :] += _colsum(du2 * ffn_v)
        dffn = (g_f * du2).astype(BF16)
        dffn_ref[...] = dffn
        dact = _dot_nt(dffn, wd_ref[...])
        gate, up = gu_ref[:, :D_FF].astype(F32), gu_ref[:, D_FF:].astype(F32)
        sg = 1.0 / (1.0 + jnp.exp(-gate))
        dgate = (dact * up * (sg * (1.0 + gate * (1.0 - sg)))).astype(BF16)
        dup = (dact * (gate * sg)).astype(BF16)
        dgu_ref[:, :D_FF] = dgate
        dgu_ref[:, D_FF:] = dup
        dh2 = _dot_nt(dgate, wgu_ref[:, :D_FF]) + _dot_nt(dup, wgu_ref[:, D_FF:])
        dx1_ref[...] = ALPHA * du2 + dh2 * (1.0 + vec_ref[V_SC_F:V_SC_F + 1, :])
        acc_ref[A_SCF:A_SCF + 1, :] += _colsum(dh2 * x1v)
        acc_ref[A_SHF:A_SHF + 1, :] += _colsum(dh2)

    full = pl.BlockSpec((tb, D), lambda i: (i, 0))
    wide = pl.BlockSpec((tb, 2 * D_FF), lambda i: (i, 0))
    return pl.pallas_call(
        body, name=name, grid=(s // tb,),
        out_shape=[jax.ShapeDtypeStruct((s, D), BF16), jax.ShapeDtypeStruct((s, 2 * D_FF), BF16),
                   jax.ShapeDtypeStruct((s, D), F32), jax.ShapeDtypeStruct((8, D), F32)],
        in_specs=[full, full, full, wide, _resident((VEC_ROWS, D)), _resident((D, 2 * D_FF)), _resident((D_FF, D))],
        out_specs=[full, wide, full, pl.BlockSpec((8, D), lambda i: (0, 0))],
        compiler_params=_cparams())(x1, ffn, target, gu, vec, w_gu, w_down)


def _attn_out_backward(dx1, x, attn, y_sb, y_sw, mixed_t, vec, w_out, name):
    s = x.shape[0]
    tb = min(TOK_TILE, s)

    def body(dx1_ref, x_ref, attn_ref, ysb_ref, ysw_ref, mt_ref, vec_ref, w_ref, du1_ref, dw_ref, dy_ref, acc_ref):
        @pl.when(pl.program_id(0) == 0)
        def _():
            dw_ref[...] = jnp.zeros_like(dw_ref)
            acc_ref[...] = jnp.zeros_like(acc_ref)

        attn = attn_ref[...]
        g_a = 1.0 + vec_ref[V_G_A:V_G_A + 1, :]
        xhat, rstd = _layer_norm_stats(ALPHA * x_ref[...] + g_a * attn)
        dx1v = dx1_ref[...]
        acc_ref[B_LN1G:B_LN1G + 1, :] += _colsum(dx1v * xhat)
        acc_ref[B_LN1B:B_LN1B + 1, :] += _colsum(dx1v)
        du1 = _layer_norm_bwd(dx1v * vec_ref[V_LN1G:V_LN1G + 1, :], xhat, rstd)
        du1_ref[...] = du1
        acc_ref[B_GA:B_GA + 1, :] += _colsum(du1 * attn)
        dattn = (g_a * du1).astype(BF16)
        dw_ref[...] += _dot(mt_ref[...], dattn)
        dmixed = _dot_nt(dattn, w_ref[...])
        for lo, y_ref in ((0, ysb_ref), (SB_W, ysw_ref)):
            y = y_ref[...]
            rr = _rms_parts(y)
            dn = dmixed[:, lo:lo + SB_W]
            acc_ref[B_GN:B_GN + 1, lo:lo + SB_W] += _colsum(dn * y * rr)
            dng = dn * vec_ref[V_GN:V_GN + 1, lo:lo + SB_W]
            dy_ref[:, lo:lo + SB_W] = rr * dng - y * (rr * rr * rr) * jnp.mean(dng * y, axis=1, keepdims=True)

    half = pl.BlockSpec((tb, SB_W), lambda i: (i, 0))
    full = pl.BlockSpec((tb, D), lambda i: (i, 0))
    return pl.pallas_call(
        body, name=name, grid=(s // tb,),
        out_shape=[jax.ShapeDtypeStruct((s, D), F32), jax.ShapeDtypeStruct((D, D), F32),
                   jax.ShapeDtypeStruct((s, D), F32), jax.ShapeDtypeStruct((8, D), F32)],
        in_specs=[full, full, full, half, half, pl.BlockSpec((D, tb), lambda i: (0, i)), _resident((VEC_ROWS, D)),
                  _resident((D, D))],
        out_specs=[full, pl.BlockSpec((D, D), lambda i: (0, 0)), full, pl.BlockSpec((8, D), lambda i: (0, 0))],
        compiler_params=_cparams())(dx1, x, attn, y_sb, y_sw, mixed_t, vec, w_out)


def _sb_backward(proj, sp_total, sweep_start, dy, slabs, name):
    s = proj.shape[0]
    tq, tk = min(SB_TQ, s), min(SB_TK, s)
    r = tq // tk
    nkb = SB_W // LANES
    nq = s // tq

    n_sl = len(slabs)

    def body(q_ref, k_ref, v_ref, tot_ref, do_ref, start_ref, *refs):
        slab_refs, (dq_ref, dk_ref, dv_ref), got_refs = refs[:n_sl], refs[n_sl:n_sl + 3], refs[n_sl + 3:2 * n_sl + 3]
        dq_acc, left_refs, gsum_refs = refs[2 * n_sl + 3:2 * n_sl + 6]
        i = pl.program_id(1)
        step = pl.program_id(0) * nq + i
        scatter = _scatter_exchange(slab_refs, got_refs, *refs[2 * n_sl + 6:])

        @pl.when(step == 0)
        def _():
            scatter.start()

        @pl.when(i == 0)
        def _():
            dk_ref[...] = jnp.zeros_like(dk_ref)
            dv_ref[...] = jnp.zeros_like(dv_ref)

        lane = lax.broadcasted_iota(jnp.int32, (1, LANES), 1)
        first = lane < HEAD_DIM
        qp, dop, totp = q_ref[...], do_ref[...], tot_ref[...]
        zero = jnp.zeros((), BF16)
        qs = (jnp.where(first, qp, zero), jnp.where(first, zero, qp))
        dofs = (jnp.where(first, dop, 0.0), jnp.where(first, 0.0, dop))
        dobs = tuple(d.astype(BF16) for d in dofs)
        dots = tuple(d.T.astype(BF16) for d in dofs)
        qts = tuple(qh.astype(F32).T.astype(BF16) for qh in qs)
        later = _sum_matrix(tk, lambda row, col: row > col)
        earlier = _sum_matrix(tk, lambda row, col: row < col)
        dq_acc[...] = jnp.zeros_like(dq_acc)
        gsum_refs[...] = jnp.zeros_like(gsum_refs)
        swapped = pltpu.roll(totp, HEAD_DIM, 1)
        left_refs[0] = jnp.where(first, totp, swapped)
        left_refs[1] = jnp.where(first, swapped, totp)

        def blocks(tiles):
            rows = [slice(r0, r0 + n) for r0, n, _, _ in tiles]
            kjs = [k_ref[pl.ds(pl.multiple_of(j * tk, tk), tk), :] for _, _, j, _ in tiles]
            vjs = [v_ref[pl.ds(pl.multiple_of(j * tk, tk), tk), :] for _, _, j, _ in tiles]
            chains = [(hd, t) for t in range(len(tiles)) for hd in range(2)]
            zs = [_dot_nt(qs[hd][rows[t]], kjs[t]) for hd, t in chains]
            dws = [_dot_nt(dobs[hd][rows[t]], vjs[t]) for hd, t in chains]
            parts = [_softplus_parts(z) for z in zs]
            sps = [p[0] for p in parts]
            befores = [_before(i * tq + r0, n, j * tk, tk) if diag else None for r0, n, j, diag in tiles]
            spms = [sp if befores[t] is None else jnp.where(befores[t], sp, 0.0) for (hd, t), sp in zip(chains, sps)]
            cums = [_block_sums(spm, later) for spm in spms]
            lefts, gsums, dq_sums = _RowState(left_refs, tk), _RowState(gsum_refs, tk), _RowState(dq_acc, tk, add=True)
            ws = []
            for (hd, t), z, sp, (cum, sm) in zip(chains, zs, sps, cums):
                r0, n = tiles[t][:2]
                left = lefts.get(hd, r0, n) - sm
                lefts.put(hd, r0, n, left)
                w = jnp.exp(z - sp - cum - _across(left, tk))
                ws.append(w if befores[t] is None else jnp.where(befores[t], w, 0.0))
            wbs = [w.astype(BF16) for w in ws]
            dvs = [_dot(dots[hd][:, rows[t]], wb) for (hd, t), wb in zip(chains, wbs)]
            gs = [dw * w for dw, w in zip(dws, ws)]
            gcums = [_block_sums(g, earlier) for g in gs]
            dzbs = []
            for (hd, t), z, (sp, e1), g, (gcum, gsm) in zip(chains, zs, parts, gs, gcums):
                r0, n = tiles[t][:2]
                gsum = gsums.get(hd, r0, n)
                inv = 1.0 / (1.0 + e1)
                sig = jnp.where(z >= 0.0, inv, e1 * inv)
                dz = g - sig * (g + _across(gsum, tk) + gcum)
                dzbs.append((dz if befores[t] is None else jnp.where(befores[t], dz, 0.0)).astype(BF16))
                gsums.put(hd, r0, n, gsum + gsm)
            dqs = [_dot(dzb, kjs[t]) for (hd, t), dzb in zip(chains, dzbs)]
            dks = [_dot(qts[hd][:, rows[t]], dzb) for (hd, t), dzb in zip(chains, dzbs)]
            for t, (_, _, j, _) in enumerate(tiles):
                dv_ref[j] += dvs[2 * t] + dvs[2 * t + 1]
                dk_ref[j] += dks[2 * t] + dks[2 * t + 1]
            for (hd, t), dq in zip(chains, dqs):
                dq_sums.accumulate(hd, *tiles[t][:2], dq)
            dq_sums.store()
            lefts.store()
            gsums.store()

        below = i * r
        start = jnp.clip(start_ref[pl.program_id(0), i].astype(jnp.int32), 0, below)

        def sweep(n, carry):
            blocks([(0, tq, start + n, False)])
            return carry

        lax.fori_loop(0, jnp.maximum(below - 1 - start, 0), sweep, 0)
        diagonal = [(d * tk, tk, below + e, e == d) for d in range(r) for e in range(d + 1)]

        @pl.when(i == 0)
        def _():
            blocks(diagonal)

        @pl.when(i > 0)
        def _():
            blocks([(0, tq, below - 1, False)] + diagonal)
        dq_ref[...] = jnp.where(first, dq_acc[0], dq_acc[1])

        @pl.when(step == nkb * nq - 1)
        def _():
            scatter.wait()

    shp = jax.ShapeDtypeStruct((s, SB_W), F32)
    qspec = pl.BlockSpec((tq, LANES), lambda p, i: (i, p))
    whole = pl.BlockSpec((None, s // tk, LANES, tk), lambda p, i: (p, 0, 0, 0))
    shp_t = jax.ShapeDtypeStruct((nkb, s // tk, LANES, tk), F32)
    hbm = pl.BlockSpec(memory_space=pl.ANY)
    return pl.pallas_call(
        body, name=name, grid=(nkb, nq),
        out_shape=[shp, shp_t, shp_t] + [jax.ShapeDtypeStruct(p.shape, p.dtype) for p in slabs],
        in_specs=[qspec,
                  pl.BlockSpec((s, LANES), lambda p, i: (0, nkb + p)),
                  pl.BlockSpec((s, LANES), lambda p, i: (0, 2 * nkb + p)),
                  qspec, qspec, pl.BlockSpec(memory_space=pltpu.SMEM)] + [hbm] * n_sl,
        out_specs=[qspec, whole, whole] + [hbm] * n_sl,
        scratch_shapes=[pltpu.VMEM((2, tq, LANES), F32), pltpu.VMEM((2, tq, LANES), F32), pltpu.VMEM((2, tq, LANES), F32)]
        + _exchange_sems(n_sl),
        compiler_params=_cparams())(proj, proj, proj, sp_total, dy, sweep_start, *slabs)


def _swa_backward(proj, y_sw, dy, sinks, gives, name):
    s = proj.shape[0]
    nb = s // WINDOW
    qb, kb, vb = 3 * SB_W // SWA_QW, (3 * SB_W + SWA_QW) // LANES, (3 * SB_W + SWA_QW + SWA_KW) // LANES

    n_gv = len(gives)

    def body(q_ref, kp_ref, kc_ref, vp_ref, vc_ref, o_ref, do_ref, sink_ref, *refs):
        give_refs, (dq_ref, dk_ref, dv_ref, ds_ref), got_refs = refs[:n_gv], refs[n_gv:n_gv + 4], refs[n_gv + 4:2 * n_gv + 4]
        n = pl.program_id(0)
        swap = _sibling_halves(give_refs, got_refs, *refs[2 * n_gv + 4:])

        @pl.when(n == 0)
        def _():
            for cp in swap:
                cp.start()

        @pl.when(n == 0)
        def _():
            dk_ref[...] = jnp.zeros_like(dk_ref)
            dv_ref[...] = jnp.zeros_like(dv_ref)
            ds_ref[...] = jnp.zeros_like(ds_ref)

        k = jnp.concatenate([kp_ref[...], kc_ref[...]], axis=0)
        v = jnp.concatenate([vp_ref[...], vc_ref[...]], axis=0)
        k_sw = pltpu.roll(k.astype(F32), HEAD_DIM, 1).astype(BF16)
        v_sw = pltpu.roll(v.astype(F32), HEAD_DIM, 1).astype(BF16)
        lane = lax.broadcasted_iota(jnp.int32, (1, LANES), 1)
        halves = [lane < HEAD_DIM, lane >= HEAD_DIM]
        valid, distf = _swa_masks(n)
        heads = range(2 * 4)
        cols = [slice((h // 2) * LANES, (h // 2 + 1) * LANES) for h in heads]
        qms = [jnp.where(halves[h % 2], q_ref[:, cols[h]], jnp.zeros((), BF16)) for h in heads]
        dos = [jnp.where(halves[h % 2], do_ref[:, cols[h]], 0.0) for h in heads]
        dobs = [d.astype(BF16) for d in dos]
        native = [h // 4 == h % 2 for h in heads]
        kus = [k if native[h] else k_sw for h in heads]
        vus = [v if native[h] else v_sw for h in heads]
        scores = [_dot_nt(qms[h], kus[h]) for h in heads]
        dps = [_dot_nt(dobs[h], vus[h]) for h in heads]
        deltas = [jnp.sum(dos[h] * o_ref[:, cols[h]], axis=1, keepdims=True) for h in heads]
        probs = [_swa_probs(scores[h], valid, distf, h, sink_ref[h]) for h in heads]
        pbs = [probs[h][0].astype(BF16) for h in heads]
        dscs = [(probs[h][0] * (dps[h] - deltas[h])).astype(BF16) for h in heads]
        dqs = [_dot(dscs[h], kus[h]) for h in heads]
        dks = [_dot_tn(dscs[h], qms[h]) for h in heads]
        dvs = [_dot_tn(pbs[h], dobs[h]) for h in heads]
        for h in heads:
            ds_ref[h:h + 1, :] += jnp.zeros((1, LANES), F32) - jnp.sum(probs[h][1] * deltas[h])
        for pair in range(4):
            dq_ref[:, cols[2 * pair]] = jnp.where(halves[0], dqs[2 * pair], dqs[2 * pair + 1])

        def gathered(parts):
            nat = sum(parts[h] for h in heads if native[h])
            rot = sum(parts[h] for h in heads if not native[h])
            return nat + pltpu.roll(rot, HEAD_DIM, 1)

        dk, dv = gathered(dks), gathered(dvs)
        prev = pl.multiple_of(jnp.maximum(n - 1, 0) * WINDOW, WINDOW)
        cur = pl.multiple_of(n * WINDOW, WINDOW)
        dk_ref[pl.ds(prev, WINDOW), :] += dk[:WINDOW]
        dv_ref[pl.ds(prev, WINDOW), :] += dv[:WINDOW]
        dk_ref[pl.ds(cur, WINDOW), :] += dk[WINDOW:]
        dv_ref[pl.ds(cur, WINDOW), :] += dv[WINDOW:]

        @pl.when(n == nb - 1)
        def _():
            for cp in swap:
                cp.wait()

    prev_blk = lambda n: jnp.maximum(n - 1, 0)
    wide = pl.BlockSpec((WINDOW, SWA_QW), lambda n: (n, 0))
    whole = pl.BlockSpec((s, LANES), lambda n: (0, 0))
    hbm = pl.BlockSpec(memory_space=pl.ANY)
    return pl.pallas_call(
        body, name=name, grid=(nb,),
        out_shape=[jax.ShapeDtypeStruct((s, SWA_QW), F32), jax.ShapeDtypeStruct((s, LANES), F32),
                   jax.ShapeDtypeStruct((s, LANES), F32), jax.ShapeDtypeStruct((8, LANES), F32)] + _halves_shapes(gives),
        in_specs=[pl.BlockSpec((WINDOW, SWA_QW), lambda n: (n, qb)),
                  pl.BlockSpec((WINDOW, LANES), lambda n: (prev_blk(n), kb)),
                  pl.BlockSpec((WINDOW, LANES), lambda n: (n, kb)),
                  pl.BlockSpec((WINDOW, LANES), lambda n: (prev_blk(n), vb)),
                  pl.BlockSpec((WINDOW, LANES), lambda n: (n, vb)),
                  wide,
                  pl.BlockSpec((WINDOW, SWA_QW), lambda n: (n, 1)),
                  pl.BlockSpec(memory_space=pltpu.SMEM)] + [hbm] * n_gv,
        out_specs=[wide, whole, whole, pl.BlockSpec((8, LANES), lambda n: (0, 0))] + [hbm] * n_gv,
        scratch_shapes=_halves_sems(n_gv),
        compiler_params=_cparams())(proj, proj, proj, proj, proj, y_sw, dy, sinks, *gives)


def _in_proj_backward(dq_sb, dkt_sb, dvt_sb, dq_sw, dk_sw, dv_sw, du1, x, h_t, vec, w_in, name):
    s = x.shape[0]
    tb = min(TOK_TILE, s)
    n_pairs, _, _, tk = dkt_sb.shape

    def body(dqsb_ref, dktsb_ref, dvtsb_ref, dqsw_ref, dksw_ref, dvsw_ref, du1_ref, x_ref, ht_ref, vec_ref, w_ref,
             dw_ref, gx_ref, acc_ref, bacc_ref, dproj_ref):
        @pl.when(pl.program_id(0) == 0)
        def _():
            dw_ref[...] = jnp.zeros_like(dw_ref)
            acc_ref[...] = jnp.zeros_like(acc_ref)
            bacc_ref[...] = jnp.zeros_like(bacc_ref)

        pieces = ((0, dqsb_ref, QK_SCALE), (3 * SB_W, dqsw_ref, QK_SCALE), (3 * SB_W + SWA_QW, dksw_ref, 1.0),
                  (3 * SB_W + SWA_QW + SWA_KW, dvsw_ref, 1.0))
        for lo, ref, scale in pieces:
            width = ref.shape[1]
            piece = ref[...] * scale
            bacc_ref[0:1, lo:lo + width] += _colsum(piece)
            dproj_ref[:, lo:lo + width] = piece.astype(BF16)
        for base, ref in ((SB_W, dktsb_ref), (2 * SB_W, dvtsb_ref)):
            for p in range(n_pairs):
                lo = base + p * LANES
                for jj in range(tb // tk):
                    piece = ref[p, jj].T
                    bacc_ref[0:1, lo:lo + LANES] += _colsum(piece)
                    dproj_ref[jj * tk:(jj + 1) * tk, lo:lo + LANES] = piece.astype(BF16)
        dproj = dproj_ref[...]
        dw_ref[...] += _dot(ht_ref[...], dproj)
        dh = _dot_nt(dproj, w_ref[...])
        xv = x_ref[...]
        gx_ref[...] = ALPHA * du1_ref[...] + dh * (1.0 + vec_ref[V_SC_A:V_SC_A + 1, :])
        acc_ref[C_SCA:C_SCA + 1, :] += _colsum(dh * xv)
        acc_ref[C_SHA:C_SHA + 1, :] += _colsum(dh)

    half = pl.BlockSpec((tb, SB_W), lambda i: (i, 0))
    narrow = pl.BlockSpec((tb, LANES), lambda i: (i, 0))
    full = pl.BlockSpec((tb, D), lambda i: (i, 0))
    blocks_t = pl.BlockSpec((n_pairs, tb // tk, LANES, tk), lambda i: (0, i, 0, 0))
    return pl.pallas_call(
        body, name=name, grid=(s // tb,),
        out_shape=[jax.ShapeDtypeStruct((D, D_IN), F32), jax.ShapeDtypeStruct((s, D), F32),
                   jax.ShapeDtypeStruct((8, D), F32), jax.ShapeDtypeStruct((8, D_IN), F32)],
        in_specs=[half, blocks_t, blocks_t, half, narrow, narrow, full, full, pl.BlockSpec((D, tb), lambda i: (0, i)),
                  _resident((VEC_ROWS, D)), _resident((D, D_IN))],
        out_specs=[pl.BlockSpec((D, D_IN), lambda i: (0, 0)), full, pl.BlockSpec((8, D), lambda i: (0, 0)),
                   pl.BlockSpec((8, D_IN), lambda i: (0, 0))],
        scratch_shapes=[pltpu.VMEM((tb, D_IN), BF16)],
        compiler_params=_cparams())(dq_sb, dkt_sb, dvt_sb, dq_sw, dk_sw, dv_sw, du1, x, h_t, vec, w_in)


def _weight_grad(at, b, name, col_shards=1):
    m, s = at.shape
    n = b.shape[1]
    if col_shards > 1:
        tn = n // col_shards
        out_shape = jax.ShapeDtypeStruct((col_shards, m, tn), F32)
        out_spec = pl.BlockSpec((None, m, tn), lambda j, k: (j, 0, 0))
    else:
        tn = 512 if n % 512 == 0 else n
        out_shape = jax.ShapeDtypeStruct((m, n), F32)
        out_spec = pl.BlockSpec((m, tn), lambda j, k: (0, j))
    ts = min(WGRAD_TOKENS, s)
    while 2 * (m * ts * 2 + ts * tn * 2 + m * tn * 4) > WGRAD_VMEM and ts > 512:
        ts //= 2

    def body(at_ref, b_ref, o_ref):
        @pl.when(pl.program_id(1) == 0)
        def _():
            o_ref[...] = jnp.zeros_like(o_ref)

        o_ref[...] += _dot(at_ref[...], b_ref[...])

    return pl.pallas_call(
        body, name=name, grid=(n // tn, s // ts),
        out_shape=out_shape,
        in_specs=[pl.BlockSpec((m, ts), lambda j, k: (0, k)), pl.BlockSpec((ts, tn), lambda j, k: (k, j))],
        out_specs=out_spec,
        compiler_params=_cparams())(at, b)


def _pad_rows(v, rows):
    return jnp.concatenate([v, jnp.zeros((rows - v.shape[0], v.shape[1]), v.dtype)], axis=0)


def _col_shards(w, n_shards):
    r, n = w.shape
    return w.reshape(r, n_shards, n // n_shards).transpose(1, 0, 2)


def kernel(x, c, w_ada, b_ada, w_in, b_in, sinks, gn_sb, gn_swa, w_out, ln1_g, ln1_b, w_gu, w_down, ln2_g, ln2_b, loss_target, m_w_ada, m_b_ada, m_w_in, m_b_in, m_sinks, m_gn_sb, m_gn_swa, m_w_out, m_ln1_g, m_ln1_b, m_w_gu, m_w_down, m_ln2_g, m_ln2_b, v_w_ada, v_b_ada, v_w_in, v_b_in, v_sinks, v_gn_sb, v_gn_swa, v_w_out, v_ln1_g, v_ln1_b, v_w_gu, v_w_down, v_ln2_g, v_ln2_b):
    ix, iy, ic = lax.axis_index("x"), lax.axis_index("y"), lax.axis_index("c")
    chip = 2 * ix + iy
    dev = 4 * ix + 2 * iy + ic
    xs, target = x[0], loss_target[0]
    s = xs.shape[0]

    c_rows, g_in = _allgather8(_pad_rows(c, 8), "gather_c", gather=[w_in[0].astype(BF16)])
    c_all = c_rows[::8]
    n_ada = w_ada.shape[2]
    b_ada_shard = lax.dynamic_slice_in_dim(b_ada, chip * n_ada, n_ada, axis=1)
    mod_cols, silu_c = _mod_shard(c_all, w_ada[0], b_ada_shard, "mod_shard")
    mod_all = _allgather8(mod_cols, "gather_mod")[0].reshape(N_DEV, 8, n_ada)
    mod_mine = lax.dynamic_index_in_dim(mod_all, dev, axis=1, keepdims=False)
    mod = mod_mine.reshape(N_CHIPS, 2, n_ada)[:, 0].reshape(6, D)
    vec = jnp.concatenate([mod, ln1_g, ln1_b, ln2_g, ln2_b, jnp.concatenate([gn_sb, gn_swa], axis=1),
                           jnp.zeros((VEC_ROWS - 11, D), F32)], axis=0)

    w_in_b = g_in.transpose(1, 0, 2).reshape(D, D_IN)

    h_t, proj = _in_proj(xs, vec, w_in_b, b_in, "in_proj")
    y_sb, sp_total, sweep_start, g_out, g_gu = _sb_forward(
        proj, [w_out[0].astype(BF16), w_gu[0].astype(BF16)], "sb_forward")
    w_gu_b = g_gu.transpose(1, 0, 2).reshape(D, 2 * D_FF)
    w_out_b = g_out.reshape(D, D)
    sink_vec = sinks[0]
    y_sw, g_down = _swa_forward(proj, sink_vec, [w_down[0].astype(BF16)], "swa_forward")
    w_down_b = g_down.reshape(D_FF, D)
    mixed_t, attn, x1, h2_b, h2_t = _post_attention(y_sb, y_sw, xs, vec, w_out_b, "post_attention")
    gu, act_t, ffn = _ffn_forward(h2_b, w_gu_b, w_down_b, "ffn_forward")

    def in_halves(shards):
        n_sh, rows, cols = shards.shape
        return shards.reshape(n_sh, 2, rows // 2, cols)

    core = ic.reshape(1).astype(jnp.int32)
    dffn_b, dgu_b, dx1, acc_f = _ffn_backward(x1, ffn, target, gu, vec, w_gu_b, w_down_b, "ffn_backward")
    dw_gu = _weight_grad(h2_t, dgu_b, "grad_w_gu", col_shards=4)
    dw_down = _weight_grad(act_t, dffn_b, "grad_w_down")
    du1, dw_out, dy, acc_a = _attn_out_backward(dx1, xs, attn, y_sb, y_sw, mixed_t, vec, w_out_b, "attn_out_backward")
    first = [in_halves(dw_gu), in_halves(dw_down.reshape(4, D_FF // 4, D)), in_halves(dw_out.reshape(4, D // 4, D))]
    dq_sw, dk_sw, dv_sw, dsink, *got_first = _swa_backward(proj, y_sw, dy, sink_vec, first, "swa_backward")
    sums_first = _chip_sums(first, got_first, core, "grad_chip_sums")
    dq_sb, dk_sb, dv_sb, *parts_first = _sb_backward(proj, sp_total, sweep_start, dy, sums_first, "sb_backward")
    dw_in, grad_x, acc_i, acc_b = _in_proj_backward(dq_sb, dk_sb, dv_sb, dq_sw, dk_sw, dv_sw, du1, xs, h_t, vec, w_in_b,
                                                    "in_proj_backward")
    last = [in_halves(_col_shards(dw_in, 4))]
    sums_last = _chip_sums(last, _halves_swap(last, "grad_halves_swap_in"), core, "grad_chip_sum_in", out_dtype=BF16)

    dmod = jnp.concatenate([acc_i[C_SHA:C_SHA + 1], acc_i[C_SCA:C_SCA + 1], acc_a[B_GA:B_GA + 1],
                            acc_f[A_SHF:A_SHF + 1], acc_f[A_SCF:A_SCF + 1], acc_f[A_GF:A_GF + 1]], axis=1)
    dsink_row = jnp.concatenate([dsink[:, 0].reshape(1, 8), jnp.zeros((1, LANES - 8), F32)], axis=1)
    loss_row = jnp.concatenate([jnp.sum(acc_f[A_LOSS:A_LOSS + 1], axis=1, keepdims=True),
                                jnp.zeros((1, LANES - 1), F32)], axis=1)
    small = jnp.concatenate([dmod, acc_b[0:1], acc_a[B_LN1G:B_LN1G + 1], acc_a[B_LN1B:B_LN1B + 1],
                             acc_f[A_LN2G:A_LN2G + 1], acc_f[A_LN2B:A_LN2B + 1], acc_a[B_GN:B_GN + 1],
                             dsink_row, loss_row, jnp.zeros((1, SM_PAD - SM_LEN), F32)], axis=1)
    small_rows, *parts_last = _allgather8(small.reshape(8, SM_PAD // 8), "gather_small", scatter=sums_last)
    small_all = small_rows.reshape(N_DEV, SM_PAD)

    mine = _sum4s([*parts_first, *parts_last], "grad_reduce")
    theirs = _sibling_send(mine, "grad_half_return")
    gw_gu, gw_down, gw_out, gw_in = [
        jnp.concatenate([jnp.where(ic == 0, m_, t_), jnp.where(ic == 0, t_, m_)], axis=0) for m_, t_ in zip(mine, theirs)]

    small_names = ["b_ada", "b_in", "ln1_g", "ln1_b", "ln2_g", "ln2_b", "gn_sb", "gn_swa", "sinks"]
    small_at = [SM_MOD, SM_BIN, SM_LN1G, SM_LN1B, SM_LN2G, SM_LN2B, SM_GN, SM_GN + SB_W, SM_SINK]
    *small_out, loss_row_all = _small_update(
        small_all, small_at,
        [b_ada, b_in, ln1_g, ln1_b, ln2_g, ln2_b, gn_sb, gn_swa, sinks],
        [m_b_ada, m_b_in, m_ln1_g, m_ln1_b, m_ln2_g, m_ln2_b, m_gn_sb, m_gn_swa, m_sinks],
        [v_b_ada, v_b_in, v_ln1_g, v_ln1_b, v_ln2_g, v_ln2_b, v_gn_sb, v_gn_swa, v_sinks], SM_LOSS, "small_update")
    g_small, d_small, m2_small, v2_small = [dict(zip(small_names, leaves)) for leaves in small_out]
    loss = loss_row_all[0, 0]

    dmod_cols = lax.dynamic_slice_in_dim(small_all[:, SM_MOD:SM_BIN], chip * n_ada, n_ada, axis=1)
    gw_ada = _weight_grad(_pad_rows(silu_c, LANES).astype(BF16).T, _pad_rows(dmod_cols, LANES).astype(BF16), "grad_w_ada")

    big = {}
    for nm, w, g, m, v in (("w_ada", w_ada, gw_ada, m_w_ada, v_w_ada), ("w_in", w_in, gw_in, m_w_in, v_w_in),
                           ("w_out", w_out, gw_out, m_w_out, v_w_out), ("w_gu", w_gu, gw_gu, m_w_gu, v_w_gu),
                           ("w_down", w_down, gw_down, m_w_down, v_w_down)):
        d_, m2_, v2_ = _adamw(w[0], g, m[0], v[0], "adamw_" + nm)
        big[nm] = (g[None], d_[None], m2_[None], v2_[None])

    order = ["w_ada", "b_ada", "w_in", "b_in", "sinks", "gn_sb", "gn_swa", "w_out", "ln1_g", "ln1_b", "w_gu", "w_down",
             "ln2_g", "ln2_b"]

    def leaf(nm, which):
        if nm in big:
            return big[nm][which]
        return (g_small, d_small, m2_small, v2_small)[which][nm]

    outs = [loss, grad_x[None]]
    for which in range(4):
        outs += [leaf(nm, which) for nm in order]
    return tuple(outs)
```
